```python
import jax, jax.numpy as jnp
from jax import lax
import numpy as np

D_MODEL = 1024
BATCH = 8
SEQ = 4096
DEPTH = 4

HEAD_DIM = 64
MIX_WIDTH = D_MODEL
MLA_HEADS = MIX_WIDTH // 2 // HEAD_DIM
MLA_WIDTH = MLA_HEADS * HEAD_DIM
NOPE_DIM = HEAD_DIM
ROPE_DIM = HEAD_DIM // 2
V_DIM = HEAD_DIM
Q_RANK = 3 * D_MODEL // 8
KV_RANK = 4 * HEAD_DIM
ROPE_THETA = 10000.0
Q_BLOCK = 128
SG_GROUPS = MIX_WIDTH // 4 // HEAD_DIM
SG_WIDTH = SG_GROUPS * HEAD_DIM
CHUNK = 128
CV_GROUPS = MIX_WIDTH // 4 // HEAD_DIM
CV_WIDTH = CV_GROUPS * HEAD_DIM
CONV_WIDTH = 3
D_FF = ((8 * D_MODEL + 3 * 256 - 1) // (3 * 256)) * 256
EPS = 1e-6
OFF_CQ = 0
OFF_CKV = OFF_CQ + Q_RANK
OFF_KR = OFF_CKV + KV_RANK
OFF_SG = OFF_KR + ROPE_DIM
OFF_CV = OFF_SG + 2 * SG_WIDTH
IN_WIDTH = OFF_CV + 3 * CV_WIDTH

kernel_name = "hybrid_mla_sgu_shortconv_sandwich"


def rms_norm(x, g):
    xf = x.astype(jnp.float32)
    y = xf * lax.rsqrt(jnp.mean(xf * xf, axis=-1, keepdims=True) + EPS)
    return y.astype(x.dtype) * g


def group_layer_norm(x, g, b, groups):
    shp = x.shape
    xf = x.astype(jnp.float32).reshape(shp[:-1] + (groups, shp[-1] // groups))
    mu = jnp.mean(xf, axis=-1, keepdims=True)
    var = jnp.mean(jnp.square(xf - mu), axis=-1, keepdims=True)
    y = ((xf - mu) * lax.rsqrt(var + EPS)).reshape(shp)
    return y.astype(x.dtype) * g + b


def rope_tables(positions):
    inv_freq = 1.0 / (ROPE_THETA ** (jnp.arange(0, ROPE_DIM // 2, dtype=jnp.float32) / (ROPE_DIM // 2)))
    ang = positions.astype(jnp.float32)[..., None] * inv_freq
    return jnp.cos(ang), jnp.sin(ang)


def apply_rope(t, cos, sin):
    tf = t.astype(jnp.float32)
    t1, t2 = jnp.split(tf, 2, axis=-1)
    return jnp.concatenate([t1 * cos - t2 * sin, t2 * cos + t1 * sin], axis=-1).astype(t.dtype)


def causal_latent_attention(q_nope, q_rope, k_nope, k_rope, v):
    b, s, h, _ = q_nope.shape
    nb = s // Q_BLOCK
    scale = (NOPE_DIM + ROPE_DIM) ** -0.5
    kpos = jnp.arange(s)

    def to_blocks(t):
        return jnp.moveaxis(t.reshape((b, nb, Q_BLOCK) + t.shape[2:]), 1, 0)

    def one_block(args):
        qn, qr, i = args
        sc = (jnp.einsum('bqhd,bkhd->bhqk', qn, k_nope)
              + jnp.einsum('bqhd,bkd->bhqk', qr, k_rope)).astype(jnp.float32) * scale
        qpos = i * Q_BLOCK + jnp.arange(Q_BLOCK)
        mask = kpos[None, :] <= qpos[:, None]
        sc = jnp.where(mask, sc, jnp.finfo(jnp.float32).min)
        p = jax.nn.softmax(sc, axis=-1).astype(v.dtype)
        return jnp.einsum('bhqk,bkhd->bqhd', p, v)

    out = lax.map(one_block, (to_blocks(q_nope), to_blocks(q_rope), jnp.arange(nb)))
    return jnp.moveaxis(out, 0, 1).reshape(b, s, h * V_DIM)


def mla_branch(z, cos, sin, q_norm_g, w_uq, kv_norm_g, w_ukv):
    b, s, _ = z.shape
    c_q = rms_norm(z[..., OFF_CQ:OFF_CKV], q_norm_g)
    q = (c_q @ w_uq).reshape(b, s, MLA_HEADS, NOPE_DIM + ROPE_DIM)
    q_nope = q[..., :NOPE_DIM]
    q_rope = apply_rope(q[..., NOPE_DIM:], cos[:, :, None, :], sin[:, :, None, :])
    c_kv = rms_norm(z[..., OFF_CKV:OFF_KR], kv_norm_g)
    kv = (c_kv @ w_ukv).reshape(b, s, MLA_HEADS, NOPE_DIM + V_DIM)
    k_nope, v = kv[..., :NOPE_DIM], kv[..., NOPE_DIM:]
    k_rope = apply_rope(z[..., OFF_KR:OFF_SG], cos, sin)
    return causal_latent_attention(q_nope, q_rope, k_nope, k_rope, v)


def sgu_branch(z, sg_ln_g, sg_ln_b, w_sp, b_sp):
    b, s, _ = z.shape
    uv = jax.nn.gelu(z[..., OFF_SG:OFF_CV])
    u, v = uv[..., :SG_WIDTH], uv[..., SG_WIDTH:]
    v = group_layer_norm(v, sg_ln_g, sg_ln_b, SG_GROUPS)
    vc = v.reshape(b, s // CHUNK, CHUNK, SG_GROUPS, HEAD_DIM)
    w_causal = w_sp * jnp.tril(jnp.ones((CHUNK, CHUNK), w_sp.dtype))
    mixed = jnp.einsum('gts,bcsge->bctge', w_causal, vc) + jnp.swapaxes(b_sp, 0, 1)[:, :, None]
    return u * mixed.reshape(b, s, SG_WIDTH)


def conv_branch(z, conv_w):
    gate_b = z[..., OFF_CV:OFF_CV + CV_WIDTH]
    gate_c = z[..., OFF_CV + CV_WIDTH:OFF_CV + 2 * CV_WIDTH]
    h = z[..., OFF_CV + 2 * CV_WIDTH:IN_WIDTH]
    y = gate_c * h
    yp = jnp.pad(y, ((0, 0), (CONV_WIDTH - 1, 0), (0, 0)))
    s = y.shape[1]
    conv = yp[:, 0:s] * conv_w[0] + yp[:, 1:s + 1] * conv_w[1] + yp[:, 2:s + 2] * conv_w[2]
    return gate_b * conv


def _fwd_setup_inputs(seed: int = 0) -> dict:
    key = jax.random.key(seed)
    ks = jax.random.split(key, 24)
    L, D = DEPTH, D_MODEL

    def nrm(k, shape, fan_in):
        return jax.random.normal(k, shape, jnp.float32) * fan_in ** -0.5

    def gain(k, shape):
        return 1.0 + 0.05 * jax.random.normal(k, shape, jnp.float32)

    x = jax.random.normal(ks[0], (BATCH, SEQ, D), jnp.float32)
    offsets = jax.random.randint(ks[1], (BATCH, 1), 0, 1024, dtype=jnp.int32)
    positions = (offsets + jnp.arange(SEQ, dtype=jnp.int32)[None, :]).astype(jnp.int32)
    return {
        "x": x,
        "positions": positions,
        "mix_pre_g": gain(ks[2], (L, D)),
        "mix_post_g": gain(ks[3], (L, D)),
        "ffn_pre_g": gain(ks[4], (L, D)),
        "ffn_post_g": gain(ks[5], (L, D)),
        "w_in": nrm(ks[6], (L, D, IN_WIDTH), D),
        "q_norm_g": gain(ks[7], (L, Q_RANK)),
        "w_uq": nrm(ks[8], (L, Q_RANK, MLA_HEADS * (NOPE_DIM + ROPE_DIM)), Q_RANK),
        "kv_norm_g": gain(ks[9], (L, KV_RANK)),
        "w_ukv": nrm(ks[10], (L, KV_RANK, MLA_HEADS * (NOPE_DIM + V_DIM)), KV_RANK),
        "sg_ln_g": gain(ks[11], (L, SG_WIDTH)),
        "sg_ln_b": 0.02 * jax.random.normal(ks[12], (L, SG_WIDTH), jnp.float32),
        "w_sp": nrm(ks[13], (L, SG_GROUPS, CHUNK, CHUNK), CHUNK),
        "b_sp": gain(ks[14], (L, SG_GROUPS, CHUNK)),
        "conv_w": nrm(ks[15], (L, CONV_WIDTH, CV_WIDTH), CONV_WIDTH),
        "out_norm_g": gain(ks[16], (L, MIX_WIDTH)),
        "w_out": nrm(ks[17], (L, MIX_WIDTH, D), MIX_WIDTH),
        "w_gate": nrm(ks[18], (L, D, D_FF), D),
        "w_up": nrm(ks[19], (L, D, D_FF), D),
        "w_down": nrm(ks[20], (L, D_FF, D), D_FF),
    }


def _fwd_reference(x, positions, mix_pre_g, mix_post_g, ffn_pre_g, ffn_post_g, w_in, q_norm_g, w_uq,
              kv_norm_g, w_ukv, sg_ln_g, sg_ln_b, w_sp, b_sp, conv_w, out_norm_g, w_out,
              w_gate, w_up, w_down):
    cos, sin = rope_tables(positions)
    a_end = MLA_WIDTH
    s_end = MLA_WIDTH + SG_WIDTH
    for l in range(DEPTH):
        h = rms_norm(x, mix_pre_g[l])
        z = h @ w_in[l]
        y_a = mla_branch(z, cos, sin, q_norm_g[l], w_uq[l], kv_norm_g[l], w_ukv[l])
        y_b = sgu_branch(z, sg_ln_g[l], sg_ln_b[l], w_sp[l], b_sp[l])
        y_c = conv_branch(z, conv_w[l])
        g = out_norm_g[l]
        mix = jnp.concatenate([rms_norm(y_a, g[:a_end]),
                               rms_norm(y_b, g[a_end:s_end]),
                               rms_norm(y_c, g[s_end:])], axis=-1)
        x = x + rms_norm(mix @ w_out[l], mix_post_g[l])
        h = rms_norm(x, ffn_pre_g[l])
        f = (jax.nn.silu(h @ w_gate[l]) * (h @ w_up[l])) @ w_down[l]
        x = x + rms_norm(f, ffn_post_g[l])
    return x


import jax as _jax
import jax.numpy as _jnp

TWIN_FORMAT = 'train_step'
FWD_PARAMS = ['x', 'positions', 'mix_pre_g', 'mix_post_g', 'ffn_pre_g', 'ffn_post_g', 'w_in', 'q_norm_g', 'w_uq', 'kv_norm_g', 'w_ukv', 'sg_ln_g', 'sg_ln_b', 'w_sp', 'b_sp', 'conv_w', 'out_norm_g', 'w_out', 'w_gate', 'w_up', 'w_down']
TWIN_WEIGHTS = ['mix_pre_g', 'mix_post_g', 'ffn_pre_g', 'ffn_post_g', 'w_in', 'q_norm_g', 'w_uq', 'kv_norm_g', 'w_ukv', 'sg_ln_g', 'sg_ln_b', 'w_sp', 'b_sp', 'conv_w', 'out_norm_g', 'w_out', 'w_gate', 'w_up', 'w_down']
TWIN_DIFF_INPUT = 'x'
TWIN_INPUTS = ['x', 'positions', 'mix_pre_g', 'mix_post_g', 'ffn_pre_g', 'ffn_post_g', 'w_in', 'q_norm_g', 'w_uq', 'kv_norm_g', 'w_ukv', 'sg_ln_g', 'sg_ln_b', 'w_sp', 'b_sp', 'conv_w', 'out_norm_g', 'w_out', 'w_gate', 'w_up', 'w_down', 'loss_target', 'm_mix_pre_g', 'm_mix_post_g', 'm_ffn_pre_g', 'm_ffn_post_g', 'm_w_in', 'm_q_norm_g', 'm_w_uq', 'm_kv_norm_g', 'm_w_ukv', 'm_sg_ln_g', 'm_sg_ln_b', 'm_w_sp', 'm_b_sp', 'm_conv_w', 'm_out_norm_g', 'm_w_out', 'm_w_gate', 'm_w_up', 'm_w_down', 'v_mix_pre_g', 'v_mix_post_g', 'v_ffn_pre_g', 'v_ffn_post_g', 'v_w_in', 'v_q_norm_g', 'v_w_uq', 'v_kv_norm_g', 'v_w_ukv', 'v_sg_ln_g', 'v_sg_ln_b', 'v_w_sp', 'v_b_sp', 'v_conv_w', 'v_out_norm_g', 'v_w_out', 'v_w_gate', 'v_w_up', 'v_w_down']
TWIN_OUTPUTS = ['loss', 'grad_x', 'grad_mix_pre_g', 'grad_mix_post_g', 'grad_ffn_pre_g', 'grad_ffn_post_g', 'grad_w_in', 'grad_q_norm_g', 'grad_w_uq', 'grad_kv_norm_g', 'grad_w_ukv', 'grad_sg_ln_g', 'grad_sg_ln_b', 'grad_w_sp', 'grad_b_sp', 'grad_conv_w', 'grad_out_norm_g', 'grad_w_out', 'grad_w_gate', 'grad_w_up', 'grad_w_down', 'delta_mix_pre_g', 'delta_mix_post_g', 'delta_ffn_pre_g', 'delta_ffn_post_g', 'delta_w_in', 'delta_q_norm_g', 'delta_w_uq', 'delta_kv_norm_g', 'delta_w_ukv', 'delta_sg_ln_g', 'delta_sg_ln_b', 'delta_w_sp', 'delta_b_sp', 'delta_conv_w', 'delta_out_norm_g', 'delta_w_out', 'delta_w_gate', 'delta_w_up', 'delta_w_down', 'new_m_mix_pre_g', 'new_m_mix_post_g', 'new_m_ffn_pre_g', 'new_m_ffn_post_g', 'new_m_w_in', 'new_m_q_norm_g', 'new_m_w_uq', 'new_m_kv_norm_g', 'new_m_w_ukv', 'new_m_sg_ln_g', 'new_m_sg_ln_b', 'new_m_w_sp', 'new_m_b_sp', 'new_m_conv_w', 'new_m_out_norm_g', 'new_m_w_out', 'new_m_w_gate', 'new_m_w_up', 'new_m_w_down', 'new_v_mix_pre_g', 'new_v_mix_post_g', 'new_v_ffn_pre_g', 'new_v_ffn_post_g', 'new_v_w_in', 'new_v_q_norm_g', 'new_v_w_uq', 'new_v_kv_norm_g', 'new_v_w_ukv', 'new_v_sg_ln_g', 'new_v_sg_ln_b', 'new_v_w_sp', 'new_v_b_sp', 'new_v_conv_w', 'new_v_out_norm_g', 'new_v_w_out', 'new_v_w_gate', 'new_v_w_up', 'new_v_w_down']
TWIN_LEAF_KINDS = {'loss': 'loss', 'grad_x': 'grad_x', 'grad_mix_pre_g': 'grad_w', 'grad_mix_post_g': 'grad_w', 'grad_ffn_pre_g': 'grad_w', 'grad_ffn_post_g': 'grad_w', 'grad_w_in': 'grad_w', 'grad_q_norm_g': 'grad_w', 'grad_w_uq': 'grad_w', 'grad_kv_norm_g': 'grad_w', 'grad_w_ukv': 'grad_w', 'grad_sg_ln_g': 'grad_w', 'grad_sg_ln_b': 'grad_w', 'grad_w_sp': 'grad_w', 'grad_b_sp': 'grad_w', 'grad_conv_w': 'grad_w', 'grad_out_norm_g': 'grad_w', 'grad_w_out': 'grad_w', 'grad_w_gate': 'grad_w', 'grad_w_up': 'grad_w', 'grad_w_down': 'grad_w', 'delta_mix_pre_g': 'delta_w', 'delta_mix_post_g': 'delta_w', 'delta_ffn_pre_g': 'delta_w', 'delta_ffn_post_g': 'delta_w', 'delta_w_in': 'delta_w', 'delta_q_norm_g': 'delta_w', 'delta_w_uq': 'delta_w', 'delta_kv_norm_g': 'delta_w', 'delta_w_ukv': 'delta_w', 'delta_sg_ln_g': 'delta_w', 'delta_sg_ln_b': 'delta_w', 'delta_w_sp': 'delta_w', 'delta_b_sp': 'delta_w', 'delta_conv_w': 'delta_w', 'delta_out_norm_g': 'delta_w', 'delta_w_out': 'delta_w', 'delta_w_gate': 'delta_w', 'delta_w_up': 'delta_w', 'delta_w_down': 'delta_w', 'new_m_mix_pre_g': 'new_m', 'new_m_mix_post_g': 'new_m', 'new_m_ffn_pre_g': 'new_m', 'new_m_ffn_post_g': 'new_m', 'new_m_w_in': 'new_m', 'new_m_q_norm_g': 'new_m', 'new_m_w_uq': 'new_m', 'new_m_kv_norm_g': 'new_m', 'new_m_w_ukv': 'new_m', 'new_m_sg_ln_g': 'new_m', 'new_m_sg_ln_b': 'new_m', 'new_m_w_sp': 'new_m', 'new_m_b_sp': 'new_m', 'new_m_conv_w': 'new_m', 'new_m_out_norm_g': 'new_m', 'new_m_w_out': 'new_m', 'new_m_w_gate': 'new_m', 'new_m_w_up': 'new_m', 'new_m_w_down': 'new_m', 'new_v_mix_pre_g': 'new_v', 'new_v_mix_post_g': 'new_v', 'new_v_ffn_pre_g': 'new_v', 'new_v_ffn_post_g': 'new_v', 'new_v_w_in': 'new_v', 'new_v_q_norm_g': 'new_v', 'new_v_w_uq': 'new_v', 'new_v_kv_norm_g': 'new_v', 'new_v_w_ukv': 'new_v', 'new_v_sg_ln_g': 'new_v', 'new_v_sg_ln_b': 'new_v', 'new_v_w_sp': 'new_v', 'new_v_b_sp': 'new_v', 'new_v_conv_w': 'new_v', 'new_v_out_norm_g': 'new_v', 'new_v_w_out': 'new_v', 'new_v_w_gate': 'new_v', 'new_v_w_up': 'new_v', 'new_v_w_down': 'new_v'}


def _forward(args):
    return _fwd_reference(*[args[k] for k in FWD_PARAMS])


def _output_shape():
    out = _jax.eval_shape(lambda: _forward(_fwd_setup_inputs(0)))
    return out.shape, out.dtype

N_MICROBATCH = 1
ADAM_LR = 0.001
ADAM_B1 = 0.9
ADAM_B2 = 0.999
ADAM_EPS = 1e-08
ADAM_WD = 0.01
ADAM_STEP = 10
PER_EXAMPLE_BATCH_AXIS = {'x': 0, 'positions': 0, 'loss_target': 0}
SHARED_INPUTS = []
_WEIGHT_DTYPES = {'mix_pre_g': _jnp.float32, 'mix_post_g': _jnp.float32, 'ffn_pre_g': _jnp.float32, 'ffn_post_g': _jnp.float32, 'w_in': _jnp.float32, 'q_norm_g': _jnp.float32, 'w_uq': _jnp.float32, 'kv_norm_g': _jnp.float32, 'w_ukv': _jnp.float32, 'sg_ln_g': _jnp.float32, 'sg_ln_b': _jnp.float32, 'w_sp': _jnp.float32, 'b_sp': _jnp.float32, 'conv_w': _jnp.float32, 'out_norm_g': _jnp.float32, 'w_out': _jnp.float32, 'w_gate': _jnp.float32, 'w_up': _jnp.float32, 'w_down': _jnp.float32}
MOMENT_SCALE = {'mix_pre_g': 1.601476e+01, 'mix_post_g': 3.853404e+01, 'ffn_pre_g': 6.469540e+00, 'ffn_post_g': 3.248278e+01, 'w_in': 1.202285e+01, 'q_norm_g': 1.738917e+00, 'w_uq': 1.350343e+00, 'kv_norm_g': 3.562491e+01, 'w_ukv': 1.755990e+01, 'sg_ln_g': 9.130901e-01, 'sg_ln_b': 1.367159e+00, 'w_sp': 6.162504e-01, 'b_sp': 9.594576e-01, 'conv_w': 1.785157e+00, 'out_norm_g': 1.977058e+01, 'w_out': 2.002662e+01, 'w_gate': 1.953611e+00, 'w_up': 3.166750e+00, 'w_down': 5.253944e+00}


def _to_microbatches(a, axis):
    t = _jnp.moveaxis(a, axis, 0)
    t = t.reshape((N_MICROBATCH, t.shape[0] // N_MICROBATCH) + t.shape[1:])
    return _jnp.moveaxis(t, 1, axis + 1)


def setup_inputs(seed: int = 0) -> dict:
    inp = _fwd_setup_inputs(seed)
    key = _jax.random.fold_in(_jax.random.key(seed), 7919)
    shape, _ = _output_shape()
    out = dict(inp)
    out["loss_target"] = _jax.random.normal(_jax.random.fold_in(key, 0), shape, _jnp.float32)
    for i, name in enumerate(TWIN_WEIGHTS):
        w = inp[name].astype(_jnp.float32)
        if MOMENT_SCALE is None:
            s = _jnp.sqrt(_jnp.mean(_jnp.square(w)) + 1e-30)
        else:
            s = MOMENT_SCALE[name]
        km, kv = _jax.random.split(_jax.random.fold_in(key, i + 1))
        out[name] = w
        out["m_" + name] = s * _jax.random.normal(km, w.shape, _jnp.float32)
        out["v_" + name] = (s * s) * _jax.random.uniform(kv, w.shape, _jnp.float32, 0.5, 1.5)
    if N_MICROBATCH > 1:
        for name, axis in PER_EXAMPLE_BATCH_AXIS.items():
            out[name] = _to_microbatches(out[name], axis)
    return {'x': out['x'], 'positions': out['positions'], 'mix_pre_g': out['mix_pre_g'], 'mix_post_g': out['mix_post_g'], 'ffn_pre_g': out['ffn_pre_g'], 'ffn_post_g': out['ffn_post_g'], 'w_in': out['w_in'], 'q_norm_g': out['q_norm_g'], 'w_uq': out['w_uq'], 'kv_norm_g': out['kv_norm_g'], 'w_ukv': out['w_ukv'], 'sg_ln_g': out['sg_ln_g'], 'sg_ln_b': out['sg_ln_b'], 'w_sp': out['w_sp'], 'b_sp': out['b_sp'], 'conv_w': out['conv_w'], 'out_norm_g': out['out_norm_g'], 'w_out': out['w_out'], 'w_gate': out['w_gate'], 'w_up': out['w_up'], 'w_down': out['w_down'], 'loss_target': out['loss_target'], 'm_mix_pre_g': out['m_mix_pre_g'], 'm_mix_post_g': out['m_mix_post_g'], 'm_ffn_pre_g': out['m_ffn_pre_g'], 'm_ffn_post_g': out['m_ffn_post_g'], 'm_w_in': out['m_w_in'], 'm_q_norm_g': out['m_q_norm_g'], 'm_w_uq': out['m_w_uq'], 'm_kv_norm_g': out['m_kv_norm_g'], 'm_w_ukv': out['m_w_ukv'], 'm_sg_ln_g': out['m_sg_ln_g'], 'm_sg_ln_b': out['m_sg_ln_b'], 'm_w_sp': out['m_w_sp'], 'm_b_sp': out['m_b_sp'], 'm_conv_w': out['m_conv_w'], 'm_out_norm_g': out['m_out_norm_g'], 'm_w_out': out['m_w_out'], 'm_w_gate': out['m_w_gate'], 'm_w_up': out['m_w_up'], 'm_w_down': out['m_w_down'], 'v_mix_pre_g': out['v_mix_pre_g'], 'v_mix_post_g': out['v_mix_post_g'], 'v_ffn_pre_g': out['v_ffn_pre_g'], 'v_ffn_post_g': out['v_ffn_post_g'], 'v_w_in': out['v_w_in'], 'v_q_norm_g': out['v_q_norm_g'], 'v_w_uq': out['v_w_uq'], 'v_kv_norm_g': out['v_kv_norm_g'], 'v_w_ukv': out['v_w_ukv'], 'v_sg_ln_g': out['v_sg_ln_g'], 'v_sg_ln_b': out['v_sg_ln_b'], 'v_w_sp': out['v_w_sp'], 'v_b_sp': out['v_b_sp'], 'v_conv_w': out['v_conv_w'], 'v_out_norm_g': out['v_out_norm_g'], 'v_w_out': out['v_w_out'], 'v_w_gate': out['v_w_gate'], 'v_w_up': out['v_w_up'], 'v_w_down': out['v_w_down']}


def _loss(weights, diff, rest, loss_target):
    with _jax.named_scope("forward"):
        args = {**rest, TWIN_DIFF_INPUT: diff, **{k: w.astype(_WEIGHT_DTYPES[k]) for k, w in weights.items()}}
        y = _forward(args)
    with _jax.named_scope("loss_head"):
        err = _jnp.square(y.astype(_jnp.float32) - loss_target)
        return 0.5 * _jnp.sum(_jnp.mean(err, axis=-1)) if err.ndim else 0.5 * err


def _adamw(w, g, m, v):
    m = ADAM_B1 * m + (1.0 - ADAM_B1) * g
    v = ADAM_B2 * v + (1.0 - ADAM_B2) * _jnp.square(g)
    m_hat = m / (1.0 - ADAM_B1 ** ADAM_STEP)
    v_hat = v / (1.0 - ADAM_B2 ** ADAM_STEP)
    delta = -ADAM_LR * (m_hat / (_jnp.sqrt(v_hat) + ADAM_EPS) + ADAM_WD * w)
    return delta, m, v


def reference(x, positions, mix_pre_g, mix_post_g, ffn_pre_g, ffn_post_g, w_in, q_norm_g, w_uq, kv_norm_g, w_ukv, sg_ln_g, sg_ln_b, w_sp, b_sp, conv_w, out_norm_g, w_out, w_gate, w_up, w_down, loss_target, m_mix_pre_g, m_mix_post_g, m_ffn_pre_g, m_ffn_post_g, m_w_in, m_q_norm_g, m_w_uq, m_kv_norm_g, m_w_ukv, m_sg_ln_g, m_sg_ln_b, m_w_sp, m_b_sp, m_conv_w, m_out_norm_g, m_w_out, m_w_gate, m_w_up, m_w_down, v_mix_pre_g, v_mix_post_g, v_ffn_pre_g, v_ffn_post_g, v_w_in, v_q_norm_g, v_w_uq, v_kv_norm_g, v_w_ukv, v_sg_ln_g, v_sg_ln_b, v_w_sp, v_b_sp, v_conv_w, v_out_norm_g, v_w_out, v_w_gate, v_w_up, v_w_down):
    given = dict(x=x, positions=positions, mix_pre_g=mix_pre_g, mix_post_g=mix_post_g, ffn_pre_g=ffn_pre_g, ffn_post_g=ffn_post_g, w_in=w_in, q_norm_g=q_norm_g, w_uq=w_uq, kv_norm_g=kv_norm_g, w_ukv=w_ukv, sg_ln_g=sg_ln_g, sg_ln_b=sg_ln_b, w_sp=w_sp, b_sp=b_sp, conv_w=conv_w, out_norm_g=out_norm_g, w_out=w_out, w_gate=w_gate, w_up=w_up, w_down=w_down, loss_target=loss_target, m_mix_pre_g=m_mix_pre_g, m_mix_post_g=m_mix_post_g, m_ffn_pre_g=m_ffn_pre_g, m_ffn_post_g=m_ffn_post_g, m_w_in=m_w_in, m_q_norm_g=m_q_norm_g, m_w_uq=m_w_uq, m_kv_norm_g=m_kv_norm_g, m_w_ukv=m_w_ukv, m_sg_ln_g=m_sg_ln_g, m_sg_ln_b=m_sg_ln_b, m_w_sp=m_w_sp, m_b_sp=m_b_sp, m_conv_w=m_conv_w, m_out_norm_g=m_out_norm_g, m_w_out=m_w_out, m_w_gate=m_w_gate, m_w_up=m_w_up, m_w_down=m_w_down, v_mix_pre_g=v_mix_pre_g, v_mix_post_g=v_mix_post_g, v_ffn_pre_g=v_ffn_pre_g, v_ffn_post_g=v_ffn_post_g, v_w_in=v_w_in, v_q_norm_g=v_q_norm_g, v_w_uq=v_w_uq, v_kv_norm_g=v_kv_norm_g, v_w_ukv=v_w_ukv, v_sg_ln_g=v_sg_ln_g, v_sg_ln_b=v_sg_ln_b, v_w_sp=v_w_sp, v_b_sp=v_b_sp, v_conv_w=v_conv_w, v_out_norm_g=v_out_norm_g, v_w_out=v_w_out, v_w_gate=v_w_gate, v_w_up=v_w_up, v_w_down=v_w_down)
    weights = {n: given[n] for n in TWIN_WEIGHTS}
    shared = {n: given[n] for n in SHARED_INPUTS}
    per_example = {n: given[n] for n in ['x', 'positions']}
    grad_fn = _jax.value_and_grad(_loss, argnums=(0, 1))

    def one_microbatch(ex, loss_target):
        ex = dict(ex)
        diff = ex.pop(TWIN_DIFF_INPUT)
        return grad_fn(weights, diff, {**shared, **ex}, loss_target)

    if N_MICROBATCH == 1:
        loss, (grad_w, grad_x) = one_microbatch(per_example, given["loss_target"])
    else:
        def body(carry, xs):
            loss_sum, grad_sum = carry
            l_k, (gw_k, gx_k) = one_microbatch(xs[0], xs[1])
            with _jax.named_scope("update"):
                return (loss_sum + l_k, _jax.tree.map(_jnp.add, grad_sum, gw_k)), gx_k

        init = (_jnp.zeros((), _jnp.float32), _jax.tree.map(_jnp.zeros_like, weights))
        (loss, grad_w), grad_x = _jax.lax.scan(body, init, (per_example, given["loss_target"]))
    with _jax.named_scope("update"):
        delta_w, new_m, new_v = {}, {}, {}
        for n in TWIN_WEIGHTS:
            delta_w[n], new_m[n], new_v[n] = _adamw(weights[n], grad_w[n], given["m_" + n], given["v_" + n])
    return (loss, grad_x, *[grad_w[n] for n in TWIN_WEIGHTS], *[delta_w[n] for n in TWIN_WEIGHTS],
            *[new_m[n] for n in TWIN_WEIGHTS], *[new_v[n] for n in TWIN_WEIGHTS])
```

```python
import functools

import jax
import jax.numpy as jnp
import numpy as np
from jax import lax
from jax.experimental import pallas as pl
from jax.experimental.pallas import tpu as pltpu

F32 = jnp.float32
BF16 = jnp.bfloat16

D = 1024
Q_RANK = 384
KV_RANK = 256
ROPE = 32
HEADS = 8
NOPE = 64
V_DIM = 64
HEAD_PAD = 128
SG_W = 256
CV_W = 256
CHUNK = 128
GROUP = 64
D_FF = 2816
IN_W = 1952
Z_W = 2048
Z_CV, Z_MLA, Z_SG = 0, 768, 1536
EPS = 1e-6
ROPE_THETA = 10000.0
SCALE = (NOPE + ROPE) ** -0.5
NEG = -1e30
N_DEV = 8

ADAM_LR, ADAM_B1, ADAM_B2, ADAM_EPS, ADAM_WD, ADAM_STEP = 0.001, 0.9, 0.999, 1e-08, 0.01, 10

VMEM_LIMIT = 56 * 1024 * 1024

TM = 512
TM_FFN = 256
TQ = 512
TT = 512


def _cp(sem, vmem=VMEM_LIMIT):
    return pltpu.CompilerParams(dimension_semantics=sem, vmem_limit_bytes=vmem)


def _whole():
    return pl.BlockSpec(memory_space=pltpu.VMEM)


def _mm(a, b):
    return jnp.dot(a, b, preferred_element_type=F32)


def _mm_nt(a, b):
    return lax.dot_general(a, b, (((1,), (1,)), ((), ())), preferred_element_type=F32)


def _mm_tn(a, b):
    return lax.dot_general(a, b, (((0,), (0,)), ((), ())), preferred_element_type=F32)


def _rms_fwd(x, g):
    r = lax.rsqrt(jnp.mean(x * x, axis=-1, keepdims=True) + EPS)
    xh = x * r
    return xh * g, xh, r


def _rms_bwd(xh, r, g, dy):
    dxh = dy * g
    dx = r * (dxh - xh * jnp.mean(dxh * xh, axis=-1, keepdims=True))
    dg = jnp.sum(dy * xh, axis=0, keepdims=True)
    return dx, dg


def _gmean(v, gm):
    return jnp.dot(v, gm, precision=lax.Precision.HIGHEST, preferred_element_type=F32)


def _gelu(x):
    c = np.float32(np.sqrt(2.0 / np.pi))
    u = c * (x + 0.044715 * x * x * x)
    t = jnp.tanh(u)
    return 0.5 * x * (1.0 + t), t


def _gelu_grad(x, t):
    c = np.float32(np.sqrt(2.0 / np.pi))
    return 0.5 * (1.0 + t) + 0.5 * x * (1.0 - t * t) * c * (1.0 + 3.0 * 0.044715 * x * x)


def _rope(t, ca, sb, sc):
    return t * ca + pltpu.roll(t, HEAD_PAD - 16, 1) * sb + pltpu.roll(t, 16, 1) * sc


def _rope_t(dt, ca, sb, sc):
    return dt * ca + pltpu.roll(dt * sb, 16, 1) + pltpu.roll(dt * sc, HEAD_PAD - 16, 1)


def _shift_down(y, k, head):
    n = y.shape[0]
    out = pltpu.roll(y, k, 0)
    row = lax.broadcasted_iota(jnp.int32, y.shape, 0)
    for j in range(k):
        out = jnp.where(row == j, head[8 - k + j:8 - k + j + 1, :], out)
    return out


def _shift_up(y, k, tail):
    n = y.shape[0]
    out = pltpu.roll(y, n - k, 0)
    row = lax.broadcasted_iota(jnp.int32, y.shape, 0)
    for j in range(k):
        out = jnp.where(row == n - k + j, tail[j:j + 1, :], out)
    return out


def rope_tables(pos, inv):
    t = pos.shape[0]
    tm = min(TM, t)

    def body(pos_ref, inv_ref, ca_ref, sb_ref, sc_ref):
        ang = pos_ref[...] * inv_ref[...]
        c = jnp.cos(ang)
        s = jnp.sin(ang)
        lane = lax.broadcasted_iota(jnp.int32, ang.shape, 1)
        ca_ref[...] = jnp.where(lane < NOPE, 1.0, jnp.where(lane < NOPE + ROPE, c, 0.0))
        sb_ref[...] = jnp.where((lane >= NOPE) & (lane < NOPE + 16), -s, 0.0)
        sc_ref[...] = jnp.where((lane >= NOPE + 16) & (lane < NOPE + ROPE), s, 0.0)

    out = jax.ShapeDtypeStruct((t, HEAD_PAD), F32)
    blk = pl.BlockSpec((tm, HEAD_PAD), lambda i: (i, 0))
    return pl.pallas_call(
        body, name="rope_tables", grid=(t // tm,),
        in_specs=[pl.BlockSpec((tm, 1), lambda i: (i, 0)), pl.BlockSpec((1, HEAD_PAD), lambda i: (0, 0))],
        out_specs=[blk, blk, blk], out_shape=[out, out, out],
        compiler_params=_cp(("parallel",)),
    )(pos, inv)


def pre_in_fwd(x, g, w):
    t = x.shape[0]
    tm = min(TM, t)

    def body(x_ref, g_ref, w_ref, z_ref):
        h, _, _ = _rms_fwd(x_ref[...], g_ref[...])
        z_ref[...] = _mm(h.astype(BF16), w_ref[...])

    return pl.pallas_call(
        body, name="pre_in_fwd", grid=(t // tm,),
        in_specs=[pl.BlockSpec((tm, D), lambda i: (i, 0)), _whole(), _whole()],
        out_specs=pl.BlockSpec((tm, Z_W), lambda i: (i, 0)),
        out_shape=jax.ShapeDtypeStruct((t, Z_W), F32),
        compiler_params=_cp(("parallel",)),
    )(x, g, w)


def mla_proj_fwd(z, ca, sb, sc, gq, gkv, wuq, wukv):
    t = z.shape[0]
    tm = min(TM, t)

    def body(z_ref, ca_ref, sb_ref, sc_ref, gq_ref, gkv_ref, wuq_ref, wukv_ref, q_ref, k_ref, v_ref):
        z = z_ref[...]
        ca, sb, sc = ca_ref[...], sb_ref[...], sc_ref[...]
        cq, _, _ = _rms_fwd(z[:, :Q_RANK], gq_ref[...])
        ckv, _, _ = _rms_fwd(z[:, Q_RANK:Q_RANK + KV_RANK], gkv_ref[...])
        q = _mm(cq.astype(BF16), wuq_ref[...])
        kv = _mm(ckv.astype(BF16), wukv_ref[...])
        kr = _rope(pltpu.roll(z[:, Q_RANK + KV_RANK:], NOPE, 1), ca, sb, sc)
        for h in range(HEADS):
            lanes = slice(h * HEAD_PAD, (h + 1) * HEAD_PAD)
            q_ref[:, lanes] = _rope(q[:, lanes], ca, sb, sc).astype(BF16)
            k_ref[:, lanes] = (kv[:, lanes] + kr).astype(BF16)
        v_ref[...] = kv[:, HEADS * HEAD_PAD:].astype(BF16)

    tab = pl.BlockSpec((tm, HEAD_PAD), lambda i: (i, 0))
    return pl.pallas_call(
        body, name="mla_proj_fwd", grid=(t // tm,),
        in_specs=[pl.BlockSpec((tm, 768), lambda i: (i, 1)), tab, tab, tab, _whole(), _whole(), _whole(), _whole()],
        out_specs=[pl.BlockSpec((tm, HEADS * HEAD_PAD), lambda i: (i, 0)),
                   pl.BlockSpec((tm, HEADS * HEAD_PAD), lambda i: (i, 0)),
                   pl.BlockSpec((tm, HEADS * V_DIM), lambda i: (i, 0))],
        out_shape=[jax.ShapeDtypeStruct((t, HEADS * HEAD_PAD), BF16),
                   jax.ShapeDtypeStruct((t, HEADS * HEAD_PAD), BF16),
                   jax.ShapeDtypeStruct((t, HEADS * V_DIM), BF16)],
        compiler_params=_cp(("parallel",)),
    )(z, ca, sb, sc, gq, gkv, wuq, wukv)


def attn_fwd(q, k, v):
    t = q.shape[0]
    tq = min(TQ, t)
    nq = t // tq

    def body(q_ref, k_ref, v_ref, o_ref, lse_ref):
        i = pl.program_id(1)
        row = lax.broadcasted_iota(jnp.int32, (tq, tq), 0)
        col = lax.broadcasted_iota(jnp.int32, (tq, tq), 1)
        outs = []
        for h in range(2):
            lanes = slice(h * HEAD_PAD, (h + 1) * HEAD_PAD)
            qh = q_ref[:, lanes]

            def step(j, carry, masked):
                m, l, acc = carry
                start = pl.multiple_of(j * tq, tq)
                kb = k_ref[pl.ds(start, tq), lanes]
                vb = v_ref[pl.ds(start, tq), :]
                s = _mm_nt(qh, kb) * SCALE
                if masked:
                    s = jnp.where(col <= row, s, NEG)
                m_new = jnp.maximum(m, jnp.max(s, axis=-1, keepdims=True))
                p = jnp.exp(s - m_new)
                alpha = jnp.exp(m - m_new)
                l = alpha * l + jnp.sum(p, axis=-1, keepdims=True)
                acc = alpha * acc + _mm(p.astype(BF16), vb)
                return m_new, l, acc

            init = (jnp.full((tq, 1), NEG, F32), jnp.zeros((tq, 1), F32), jnp.zeros((tq, 2 * V_DIM), F32))
            carry = lax.fori_loop(0, i, lambda j, c: step(j, c, False), init)
            m, l, acc = step(i, carry, True)
            outs.append(acc / l)
            lse_ref[:, lanes] = jnp.broadcast_to(m + jnp.log(l), (tq, HEAD_PAD))
        lane = lax.broadcasted_iota(jnp.int32, (tq, 2 * V_DIM), 1)
        o_ref[...] = jnp.where(lane < V_DIM, outs[0], outs[1])

    return pl.pallas_call(
        body, name="attn_fwd", grid=(HEADS // 2, nq),
        in_specs=[pl.BlockSpec((tq, 2 * HEAD_PAD), lambda p, i: (i, p)),
                  pl.BlockSpec((t, 2 * HEAD_PAD), lambda p, i: (0, p)),
                  pl.BlockSpec((t, 2 * V_DIM), lambda p, i: (0, p))],
        out_specs=[pl.BlockSpec((tq, 2 * V_DIM), lambda p, i: (i, p)),
                   pl.BlockSpec((tq, 2 * HEAD_PAD), lambda p, i: (i, p))],
        out_shape=[jax.ShapeDtypeStruct((t, HEADS * V_DIM), F32), jax.ShapeDtypeStruct((t, HEADS * HEAD_PAD), F32)],
        compiler_params=_cp(("parallel", "parallel")),
    )(q, k, v)


def _sgu_fwd(zsg, gm, lng, lnb, wc_ref, bias, mixed_ref):
    uv, th = _gelu(zsg)
    u, v0 = uv[:, :SG_W], uv[:, SG_W:]
    vc = v0 - _gmean(v0, gm)
    r = lax.rsqrt(_gmean(vc * vc, gm) + EPS)
    vh = vc * r
    v = vh * lng + lnb
    lane = lax.broadcasted_iota(jnp.int32, (CHUNK, SG_W), 1)
    for c in range(zsg.shape[0] // CHUNK):
        rows = slice(c * CHUNK, (c + 1) * CHUNK)
        vb = v[rows].astype(BF16)
        mixed = bias
        for g in range(SG_W // GROUP):
            mixed = mixed + jnp.where(lane // GROUP == g, _mm(wc_ref[g], vb), 0.0)
        mixed_ref[rows, :] = mixed
    return u, v, vh, r, th


def _conv_fwd(zcv, halo, first, cw):
    gb, gc, hh = zcv[:, :CV_W], zcv[:, CV_W:2 * CV_W], zcv[:, 2 * CV_W:]
    y = gc * hh
    yh = jnp.where(first, 0.0, halo[:, CV_W:2 * CV_W] * halo[:, 2 * CV_W:])
    y1 = _shift_down(y, 1, yh)
    y2 = _shift_down(y, 2, yh)
    conv = y2 * cw[0:1, :] + y1 * cw[1:2, :] + y * cw[2:3, :]
    return gb * conv, conv, y, y1, y2


def _tril_bf16(w_ref, g):
    row = lax.broadcasted_iota(jnp.int32, (CHUNK, CHUNK), 0)
    col = lax.broadcasted_iota(jnp.int32, (CHUNK, CHUNK), 1)
    return jnp.where(col <= row, w_ref[g], 0.0).astype(BF16)


def mix_fwd(x, z, ya, gm, lng, lnb, wsp, bias, cw, gout, wout, gpost):
    t = x.shape[0]
    tm = min(TM, t)

    def body(x_ref, zcv_ref, halo_ref, zsg_ref, ya_ref, gm_ref, lng_ref, lnb_ref, wsp_ref, bias_ref, cw_ref,
             gout_ref, wout_ref, gpost_ref, x1_ref, wc_ref, mixed_ref):
        i = pl.program_id(0)
        for g in range(SG_W // GROUP):
            wc_ref[g] = _tril_bf16(wsp_ref, g)
        u, _, _, _, _ = _sgu_fwd(zsg_ref[...], gm_ref[...], lng_ref[...], lnb_ref[...], wc_ref, bias_ref[...], mixed_ref)
        yb = u * mixed_ref[...]
        yc, _, _, _, _ = _conv_fwd(zcv_ref[...], halo_ref[...], i == 0, cw_ref[...])
        gout = gout_ref[...]
        na, _, _ = _rms_fwd(ya_ref[...], gout[:, :512])
        nb, _, _ = _rms_fwd(yb, gout[:, 512:768])
        nc, _, _ = _rms_fwd(yc, gout[:, 768:])
        mix = jnp.concatenate([na, nb, nc], axis=1).astype(BF16)
        o, _, _ = _rms_fwd(_mm(mix, wout_ref[...]), gpost_ref[...])
        x1_ref[...] = x_ref[...] + o

    hb = tm // 8
    return pl.pallas_call(
        body, name="mix_fwd", grid=(t // tm,),
        in_specs=[pl.BlockSpec((tm, D), lambda i: (i, 0)),
                  pl.BlockSpec((tm, 768), lambda i: (i, 0)),
                  pl.BlockSpec((8, 768), lambda i: (jnp.maximum(i * hb - 1, 0), 0)),
                  pl.BlockSpec((tm, 512), lambda i: (i, 3)),
                  pl.BlockSpec((tm, 512), lambda i: (i, 0)),
                  _whole(), _whole(), _whole(), _whole(), _whole(), _whole(), _whole(), _whole(), _whole()],
        out_specs=pl.BlockSpec((tm, D), lambda i: (i, 0)),
        out_shape=jax.ShapeDtypeStruct((t, D), F32),
        scratch_shapes=[pltpu.VMEM((SG_W // GROUP, CHUNK, CHUNK), BF16), pltpu.VMEM((tm, SG_W), F32)],
        compiler_params=_cp(("arbitrary",)),
    )(x, z, z, z, ya, gm, lng, lnb, wsp, bias, cw, gout, wout, gpost)


def _sigmoid(a):
    return 1.0 / (1.0 + jnp.exp(-a))


def ffn_fwd(x1, gpre, wgu, wd, gpost):
    t = x1.shape[0]
    tm = min(TM_FFN, t)

    def body(x_ref, gpre_ref, wgu_ref, wd_ref, gpost_ref, x2_ref):
        x = x_ref[...]
        h, _, _ = _rms_fwd(x, gpre_ref[...])
        ab = _mm(h.astype(BF16), wgu_ref[...])
        a, b = ab[:, :D_FF], ab[:, D_FF:]
        s = a * _sigmoid(a) * b
        f, _, _ = _rms_fwd(_mm(s.astype(BF16), wd_ref[...]), gpost_ref[...])
        x2_ref[...] = x + f

    return pl.pallas_call(
        body, name="ffn_fwd", grid=(t // tm,),
        in_specs=[pl.BlockSpec((tm, D), lambda i: (i, 0)), _whole(), _whole(), _whole(), _whole()],
        out_specs=pl.BlockSpec((tm, D), lambda i: (i, 0)),
        out_shape=jax.ShapeDtypeStruct((t, D), F32),
        compiler_params=_cp(("parallel",)),
    )(x1, gpre, wgu, wd, gpost)


def loss_head(y, target):
    t = y.shape[0]
    tm = min(TM, t)

    def body(y_ref, t_ref, loss_ref, dy_ref):
        @pl.when(pl.program_id(0) == 0)
        def _():
            loss_ref[...] = jnp.zeros_like(loss_ref)

        e = y_ref[...] - t_ref[...]
        dy_ref[...] = e * (1.0 / D)
        loss_ref[...] += jnp.sum(jnp.sum(e * e, axis=-1, keepdims=True), axis=0, keepdims=True)

    return pl.pallas_call(
        body, name="loss_head", grid=(t // tm,),
        in_specs=[pl.BlockSpec((tm, D), lambda i: (i, 0)), pl.BlockSpec((tm, D), lambda i: (i, 0))],
        out_specs=[pl.BlockSpec((1, 128), lambda i: (0, 0)), pl.BlockSpec((tm, D), lambda i: (i, 0))],
        out_shape=[jax.ShapeDtypeStruct((1, 128), F32), jax.ShapeDtypeStruct((t, D), F32)],
        compiler_params=_cp(("arbitrary",)),
    )(y, target)


def _acc(ref, first, val):
    @pl.when(first)
    def _():
        ref[...] = val

    @pl.when(jnp.logical_not(first))
    def _():
        ref[...] += val


def ffn_bwd(x1, dx2, gpre, wgu, wd, gpost):
    t = x1.shape[0]
    tm = min(TM_FFN, t)

    def body(x_ref, dx2_ref, gpre_ref, wgu_ref, wd_ref, gpost_ref,
             dx1_ref, h_ref, dab_ref, s_ref, df_ref, dgpre_ref, dgpost_ref):
        first = pl.program_id(0) == 0
        x, dx2 = x_ref[...], dx2_ref[...]
        gpre, gpost = gpre_ref[...], gpost_ref[...]
        h, xh, rx = _rms_fwd(x, gpre)
        hb = h.astype(BF16)
        ab = _mm(hb, wgu_ref[...])
        a, b = ab[:, :D_FF], ab[:, D_FF:]
        sg = _sigmoid(a)
        sil = a * sg
        sb = (sil * b).astype(BF16)
        _, fh, rf = _rms_fwd(_mm(sb, wd_ref[...]), gpost)
        df, dgpost = _rms_bwd(fh, rf, gpost, dx2)
        dfb = df.astype(BF16)
        ds = _mm_nt(dfb, wd_ref[...])
        da = (ds * b * (sg * (1.0 + a * (1.0 - sg)))).astype(BF16)
        db = (ds * sil).astype(BF16)
        dab = jnp.concatenate([da, db], axis=1)
        dh = _mm_nt(dab, wgu_ref[...])
        dx, dgpre = _rms_bwd(xh, rx, gpre, dh)
        dx1_ref[...] = dx2 + dx
        h_ref[...] = hb
        dab_ref[...] = dab
        s_ref[...] = sb
        df_ref[...] = dfb
        _acc(dgpre_ref, first, dgpre)
        _acc(dgpost_ref, first, dgpost)

    row = lambda w: pl.BlockSpec((tm, w), lambda i: (i, 0))
    vec = pl.BlockSpec((1, D), lambda i: (0, 0))
    return pl.pallas_call(
        body, name="ffn_bwd", grid=(t // tm,),
        in_specs=[row(D), row(D), _whole(), _whole(), _whole(), _whole()],
        out_specs=[row(D), row(D), row(2 * D_FF), row(D_FF), row(D), vec, vec],
        out_shape=[jax.ShapeDtypeStruct((t, D), F32), jax.ShapeDtypeStruct((t, D), BF16),
                   jax.ShapeDtypeStruct((t, 2 * D_FF), BF16), jax.ShapeDtypeStruct((t, D_FF), BF16),
                   jax.ShapeDtypeStruct((t, D), BF16), jax.ShapeDtypeStruct((1, D), F32),
                   jax.ShapeDtypeStruct((1, D), F32)],
        compiler_params=_cp(("arbitrary",)),
    )(x1, dx2, gpre, wgu, wd, gpost)


def atb(a, b, tn):
    t, k = a.shape
    n = b.shape[1]
    tt = min(TT, t)
    tn = min(tn, n)

    def body(a_ref, b_ref, o_ref):
        _acc(o_ref, pl.program_id(1) == 0, _mm_tn(a_ref[...], b_ref[...]))

    return pl.pallas_call(
        body, name="atb", grid=(n // tn, t // tt),
        in_specs=[pl.BlockSpec((tt, k), lambda j, i: (i, 0)), pl.BlockSpec((tt, tn), lambda j, i: (i, j))],
        out_specs=pl.BlockSpec((k, tn), lambda j, i: (0, j)),
        out_shape=jax.ShapeDtypeStruct((k, n), F32),
        compiler_params=_cp(("parallel", "arbitrary")),
    )(a, b)


def mix_bwd(dx1, z, ya, gm, lng, lnb, wsp, wspt, bias, cw, gout, wout, gpost):
    t = dx1.shape[0]
    tm = min(TM, t)
    ng = SG_W // GROUP

    def body(dx1_ref, zcv_ref, halo_ref, zsg_ref, ya_ref, gm_ref, lng_ref, lnb_ref, wsp_ref, wspt_ref, bias_ref,
             cw_ref, gout_ref, wout_ref, gpost_ref,
             dya_ref, dyc_ref, dzsg_ref, mix_ref, do_ref, dgpost_ref, dgout_ref, dlng_ref, dlnb_ref, dwsp_ref,
             dbias_ref, wc_ref, wct_ref, mixed_ref, dv_ref):
        i = pl.program_id(0)
        first = i == 0
        gm = gm_ref[...]
        for g in range(ng):
            wc_ref[g] = _tril_bf16(wsp_ref, g)
            wct_ref[g] = jnp.where(
                lax.broadcasted_iota(jnp.int32, (CHUNK, CHUNK), 0) <= lax.broadcasted_iota(jnp.int32, (CHUNK, CHUNK), 1),
                wspt_ref[g], 0.0).astype(BF16)
        zsg = zsg_ref[...]
        lng = lng_ref[...]
        u, v, vh, r, th = _sgu_fwd(zsg, gm, lng, lnb_ref[...], wc_ref, bias_ref[...], mixed_ref)
        mixed = mixed_ref[...]
        yb = u * mixed
        yc, _, _, _, _ = _conv_fwd(zcv_ref[...], halo_ref[...], first, cw_ref[...])
        gout, gpost = gout_ref[...], gpost_ref[...]
        ga, gb_, gc_ = gout[:, :512], gout[:, 512:768], gout[:, 768:]
        na, yah, ra = _rms_fwd(ya_ref[...], ga)
        nb, ybh, rb = _rms_fwd(yb, gb_)
        nc, ych, rc = _rms_fwd(yc, gc_)
        mix = jnp.concatenate([na, nb, nc], axis=1).astype(BF16)
        _, oh, ro = _rms_fwd(_mm(mix, wout_ref[...]), gpost)
        do, dgpost = _rms_bwd(oh, ro, gpost, dx1_ref[...])
        dob = do.astype(BF16)
        dmix = _mm_nt(dob, wout_ref[...])
        dya, dga = _rms_bwd(yah, ra, ga, dmix[:, :512])
        dyb, dgb = _rms_bwd(ybh, rb, gb_, dmix[:, 512:768])
        dyc, dgc = _rms_bwd(ych, rc, gc_, dmix[:, 768:])
        dya_ref[...] = dya
        dyc_ref[...] = dyc
        mix_ref[...] = mix
        do_ref[...] = dob
        _acc(dgpost_ref, first, dgpost)
        _acc(dgout_ref, first, jnp.concatenate([dga, dgb, dgc], axis=1))
        du = dyb * mixed
        dmixed = dyb * u
        lane = lax.broadcasted_iota(jnp.int32, (CHUNK, SG_W), 1)
        row = lax.broadcasted_iota(jnp.int32, (CHUNK, CHUNK), 0)
        col = lax.broadcasted_iota(jnp.int32, (CHUNK, CHUNK), 1)
        dbias = jnp.zeros((CHUNK, SG_W), F32)
        dw = [jnp.zeros((CHUNK, CHUNK), F32) for _ in range(ng)]
        for c in range(tm // CHUNK):
            rows = slice(c * CHUNK, (c + 1) * CHUNK)
            dm = dmixed[rows]
            dbias = dbias + dm
            dmb = dm.astype(BF16)
            vb = v[rows].astype(BF16)
            dvc = jnp.zeros((CHUNK, SG_W), F32)
            for g in range(ng):
                in_g = lane // GROUP == g
                dvc = dvc + jnp.where(in_g, _mm(wct_ref[g], dmb), 0.0)
                dw[g] = dw[g] + _mm_nt(jnp.where(in_g, dmb, jnp.zeros_like(dmb)), vb)
            dv_ref[rows, :] = dvc
        for g in range(ng):
            dwg = jnp.where(col <= row, dw[g], 0.0)

            @pl.when(first)
            def _():
                dwsp_ref[g] = dwg

            @pl.when(jnp.logical_not(first))
            def _():
                dwsp_ref[g] += dwg
        _acc(dbias_ref, first, _gmean(dbias, gm) * GROUP)
        dv = dv_ref[...]
        _acc(dlng_ref, first, jnp.sum(dv * vh, axis=0, keepdims=True))
        _acc(dlnb_ref, first, jnp.sum(dv, axis=0, keepdims=True))
        dvh = dv * lng
        dv0 = r * (dvh - _gmean(dvh, gm) - vh * _gmean(dvh * vh, gm))
        dzsg_ref[...] = (jnp.concatenate([du, dv0], axis=1) * _gelu_grad(zsg, th)).astype(BF16)

    hb = tm // 8
    row_ = lambda w: pl.BlockSpec((tm, w), lambda i: (i, 0))
    vec = lambda w: pl.BlockSpec((1, w), lambda i: (0, 0))
    return pl.pallas_call(
        body, name="mix_bwd", grid=(t // tm,),
        in_specs=[row_(D),
                  pl.BlockSpec((tm, 768), lambda i: (i, 0)),
                  pl.BlockSpec((8, 768), lambda i: (jnp.maximum(i * hb - 1, 0), 0)),
                  pl.BlockSpec((tm, 512), lambda i: (i, 3)),
                  row_(512),
                  _whole(), _whole(), _whole(), _whole(), _whole(), _whole(), _whole(), _whole(), _whole(), _whole()],
        out_specs=[row_(512), row_(CV_W), row_(512), row_(D), row_(D), vec(D), vec(D), vec(SG_W), vec(SG_W),
                   pl.BlockSpec((ng, CHUNK, CHUNK), lambda i: (0, 0, 0)),
                   pl.BlockSpec((CHUNK, SG_W), lambda i: (0, 0))],
        out_shape=[jax.ShapeDtypeStruct((t, 512), F32), jax.ShapeDtypeStruct((t, CV_W), F32),
                   jax.ShapeDtypeStruct((t, 512), BF16), jax.ShapeDtypeStruct((t, D), BF16),
                   jax.ShapeDtypeStruct((t, D), BF16), jax.ShapeDtypeStruct((1, D), F32),
                   jax.ShapeDtypeStruct((1, D), F32), jax.ShapeDtypeStruct((1, SG_W), F32),
                   jax.ShapeDtypeStruct((1, SG_W), F32), jax.ShapeDtypeStruct((ng, CHUNK, CHUNK), F32),
                   jax.ShapeDtypeStruct((CHUNK, SG_W), F32)],
        scratch_shapes=[pltpu.VMEM((ng, CHUNK, CHUNK), BF16), pltpu.VMEM((ng, CHUNK, CHUNK), BF16),
                        pltpu.VMEM((tm, SG_W), F32), pltpu.VMEM((tm, SG_W), F32)],
        compiler_params=_cp(("arbitrary",)),
    )(dx1, z, z, z, ya, gm, lng, lnb, wsp, wspt, bias, cw, gout, wout, gpost)


def conv_bwd(dyc, z, cw):
    t = dyc.shape[0]
    tm = min(TM, t)
    hb = tm // 8
    last_blk = t // 8 - 1

    def body(dyc_ref, dyct_ref, zcv_ref, head_ref, tail_ref, cw_ref, dz_ref, dcw_ref):
        i = pl.program_id(0)
        first = i == 0
        last = i == pl.num_programs(0) - 1
        cw = cw_ref[...]
        zcv = zcv_ref[...]
        gb, gc, hh = zcv[:, :CV_W], zcv[:, CV_W:2 * CV_W], zcv[:, 2 * CV_W:]
        _, conv, y, y1, y2 = _conv_fwd(zcv, head_ref[...], first, cw)
        dyc = dyc_ref[...]
        dconv = dyc * gb
        tail = jnp.where(last, 0.0, dyct_ref[...] * tail_ref[:, :CV_W])
        d1 = _shift_up(dconv, 1, tail)
        d2 = _shift_up(dconv, 2, tail)
        dy = dconv * cw[2:3, :] + d1 * cw[1:2, :] + d2 * cw[0:1, :]
        dz_ref[...] = jnp.concatenate([dyc * conv, dy * hh, dy * gc], axis=1).astype(BF16)
        tap = lax.broadcasted_iota(jnp.int32, (8, CV_W), 0)
        dcw = jnp.where(tap == 0, jnp.sum(dconv * y2, axis=0, keepdims=True),
                        jnp.where(tap == 1, jnp.sum(dconv * y1, axis=0, keepdims=True),
                                  jnp.where(tap == 2, jnp.sum(dconv * y, axis=0, keepdims=True), 0.0)))
        _acc(dcw_ref, first, dcw)

    return pl.pallas_call(
        body, name="conv_bwd", grid=(t // tm,),
        in_specs=[pl.BlockSpec((tm, CV_W), lambda i: (i, 0)),
                  pl.BlockSpec((8, CV_W), lambda i: (jnp.minimum((i + 1) * hb, last_blk), 0)),
                  pl.BlockSpec((tm, 768), lambda i: (i, 0)),
                  pl.BlockSpec((8, 768), lambda i: (jnp.maximum(i * hb - 1, 0), 0)),
                  pl.BlockSpec((8, 768), lambda i: (jnp.minimum((i + 1) * hb, last_blk), 0)),
                  _whole()],
        out_specs=[pl.BlockSpec((tm, 768), lambda i: (i, 0)), pl.BlockSpec((8, CV_W), lambda i: (0, 0))],
        out_shape=[jax.ShapeDtypeStruct((t, 768), BF16), jax.ShapeDtypeStruct((8, CV_W), F32)],
        compiler_params=_cp(("arbitrary",)),
    )(dyc, dyc, z, z, z, cw)


def attn_bwd(q, k, v, o, lse, do):
    t = q.shape[0]
    tq = min(TQ, t)
    nq = t // tq

    def body(q_ref, k_ref, v_ref, o_ref, lse_ref, do_ref, dq_ref, dk_ref, dv_ref):
        j = pl.program_id(1)

        @pl.when(j == 0)
        def _():
            dq_ref[...] = jnp.zeros_like(dq_ref)

        row = lax.broadcasted_iota(jnp.int32, (tq, tq), 0)
        col = lax.broadcasted_iota(jnp.int32, (tq, tq), 1)
        vlane = lax.broadcasted_iota(jnp.int32, (tq, 2 * V_DIM), 1)
        vb = v_ref[...]
        dv_acc = jnp.zeros((tq, 2 * V_DIM), F32)
        for h in range(2):
            lanes = slice(h * HEAD_PAD, (h + 1) * HEAD_PAD)
            kb = k_ref[:, lanes]
            in_h = (vlane // V_DIM) == h

            def step(i, carry, masked):
                dk_acc, dv_acc = carry
                start = pl.multiple_of(i * tq, tq)
                qb = q_ref[pl.ds(start, tq), lanes]
                dob = jnp.where(in_h, do_ref[pl.ds(start, tq), :], 0.0)
                delta = jnp.sum(dob * o_ref[pl.ds(start, tq), :], axis=-1, keepdims=True)
                lse = lse_ref[pl.ds(start, tq), lanes][:, 0:1]
                s = _mm_nt(qb, kb) * SCALE
                if masked:
                    s = jnp.where(col <= row, s, NEG)
                p = jnp.exp(s - lse)
                dob16 = dob.astype(BF16)
                dp = _mm_nt(dob16, vb)
                ds = (p * (dp - delta) * SCALE).astype(BF16)
                dv_acc = dv_acc + _mm_tn(p.astype(BF16), dob16)
                dk_acc = dk_acc + _mm_tn(ds, qb)
                dq_ref[pl.ds(start, tq), lanes] += _mm(ds, kb)
                return dk_acc, dv_acc

            carry = step(j, (jnp.zeros((tq, HEAD_PAD), F32), dv_acc), True)
            dk_acc, dv_acc = lax.fori_loop(j + 1, nq, lambda i, c: step(i, c, False), carry)
            dk_ref[:, lanes] = dk_acc
        dv_ref[...] = dv_acc

    return pl.pallas_call(
        body, name="attn_bwd", grid=(HEADS // 2, nq),
        in_specs=[pl.BlockSpec((t, 2 * HEAD_PAD), lambda p, j: (0, p)),
                  pl.BlockSpec((tq, 2 * HEAD_PAD), lambda p, j: (j, p)),
                  pl.BlockSpec((tq, 2 * V_DIM), lambda p, j: (j, p)),
                  pl.BlockSpec((t, 2 * V_DIM), lambda p, j: (0, p)),
                  pl.BlockSpec((t, 2 * HEAD_PAD), lambda p, j: (0, p)),
                  pl.BlockSpec((t, 2 * V_DIM), lambda p, j: (0, p))],
        out_specs=[pl.BlockSpec((t, 2 * HEAD_PAD), lambda p, j: (0, p)),
                   pl.BlockSpec((tq, 2 * HEAD_PAD), lambda p, j: (j, p)),
                   pl.BlockSpec((tq, 2 * V_DIM), lambda p, j: (j, p))],
        out_shape=[jax.ShapeDtypeStruct((t, HEADS * HEAD_PAD), F32), jax.ShapeDtypeStruct((t, HEADS * HEAD_PAD), F32),
                   jax.ShapeDtypeStruct((t, HEADS * V_DIM), F32)],
        compiler_params=_cp(("parallel", "arbitrary")),
    )(q, k, v, o, lse, do)


def mla_proj_bwd(dq, dk, dv, z, ca, sb, sc, gq, gkv, wuq, wukv):
    t = z.shape[0]
    tm = min(TM, t)

    def body(dq_ref, dk_ref, dv_ref, z_ref, ca_ref, sb_ref, sc_ref, gq_ref, gkv_ref, wuq_ref, wukv_ref,
             dz_ref, cq_ref, ckv_ref, dqp_ref, dkvp_ref, dgq_ref, dgkv_ref):
        first = pl.program_id(0) == 0
        z = z_ref[...]
        ca, sb, sc = ca_ref[...], sb_ref[...], sc_ref[...]
        gq, gkv = gq_ref[...], gkv_ref[...]
        cq, cqh, rq = _rms_fwd(z[:, :Q_RANK], gq)
        ckv, ckvh, rkv = _rms_fwd(z[:, Q_RANK:Q_RANK + KV_RANK], gkv)
        lane = lax.broadcasted_iota(jnp.int32, (tm, HEAD_PAD), 1)
        dkr = jnp.zeros((tm, HEAD_PAD), F32)
        for h in range(HEADS):
            lanes = slice(h * HEAD_PAD, (h + 1) * HEAD_PAD)
            dqp_ref[:, lanes] = _rope_t(dq_ref[:, lanes], ca, sb, sc).astype(BF16)
            dkh = dk_ref[:, lanes]
            dkr = dkr + dkh
            dkvp_ref[:, lanes] = jnp.where(lane < NOPE, dkh, 0.0).astype(BF16)
        dkvp_ref[:, HEADS * HEAD_PAD:] = dv_ref[...].astype(BF16)
        dkr = pltpu.roll(_rope_t(jnp.where(lane >= NOPE, dkr, 0.0), ca, sb, sc), HEAD_PAD - NOPE, 1)
        dkr = jnp.where(lane < ROPE, dkr, 0.0)
        dcq = _mm_nt(dqp_ref[...], wuq_ref[...])
        dckv = _mm_nt(dkvp_ref[...], wukv_ref[...])
        dzq, dgq = _rms_bwd(cqh, rq, gq, dcq)
        dzkv, dgkv = _rms_bwd(ckvh, rkv, gkv, dckv)
        dz_ref[...] = jnp.concatenate([dzq, dzkv, dkr], axis=1).astype(BF16)
        cq_ref[...] = cq.astype(BF16)
        ckv_ref[...] = ckv.astype(BF16)
        _acc(dgq_ref, first, dgq)
        _acc(dgkv_ref, first, dgkv)

    row = lambda w: pl.BlockSpec((tm, w), lambda i: (i, 0))
    vec = lambda w: pl.BlockSpec((1, w), lambda i: (0, 0))
    return pl.pallas_call(
        body, name="mla_proj_bwd", grid=(t // tm,),
        in_specs=[row(1024), row(1024), row(512), pl.BlockSpec((tm, 768), lambda i: (i, 1)),
                  row(HEAD_PAD), row(HEAD_PAD), row(HEAD_PAD), _whole(), _whole(), _whole(), _whole()],
        out_specs=[row(768), row(Q_RANK), row(KV_RANK), row(1024), row(1536), vec(Q_RANK), vec(KV_RANK)],
        out_shape=[jax.ShapeDtypeStruct((t, 768), BF16), jax.ShapeDtypeStruct((t, Q_RANK), BF16),
                   jax.ShapeDtypeStruct((t, KV_RANK), BF16), jax.ShapeDtypeStruct((t, 1024), BF16),
                   jax.ShapeDtypeStruct((t, 1536), BF16), jax.ShapeDtypeStruct((1, Q_RANK), F32),
                   jax.ShapeDtypeStruct((1, KV_RANK), F32)],
        compiler_params=_cp(("arbitrary",)),
    )(dq, dk, dv, z, ca, sb, sc, gq, gkv, wuq, wukv)


def pre_in_bwd(x, dx1, dzcv, dzmla, dzsg, g, w):
    t = x.shape[0]
    tm = min(TM, t)

    def body(x_ref, dx1_ref, dzcv_ref, dzmla_ref, dzsg_ref, g_ref, w_ref, dx_ref, h_ref, dz_ref, dg_ref):
        g = g_ref[...]
        h, xh, r = _rms_fwd(x_ref[...], g)
        dz = jnp.concatenate([dzcv_ref[...], dzmla_ref[...], dzsg_ref[...]], axis=1)
        dx, dg = _rms_bwd(xh, r, g, _mm_nt(dz, w_ref[...]))
        dx_ref[...] = dx1_ref[...] + dx
        h_ref[...] = h.astype(BF16)
        dz_ref[...] = dz
        _acc(dg_ref, pl.program_id(0) == 0, dg)

    row = lambda w_: pl.BlockSpec((tm, w_), lambda i: (i, 0))
    return pl.pallas_call(
        body, name="pre_in_bwd", grid=(t // tm,),
        in_specs=[row(D), row(D), row(768), row(768), row(512), _whole(), _whole()],
        out_specs=[row(D), row(D), row(Z_W), pl.BlockSpec((1, D), lambda i: (0, 0))],
        out_shape=[jax.ShapeDtypeStruct((t, D), F32), jax.ShapeDtypeStruct((t, D), BF16),
                   jax.ShapeDtypeStruct((t, Z_W), BF16), jax.ShapeDtypeStruct((1, D), F32)],
        compiler_params=_cp(("arbitrary",)),
    )(x, dx1, dzcv, dzmla, dzsg, g, w)


MESH = pl.DeviceIdType.MESH


def _place():
    return lax.axis_index("x"), lax.axis_index("y"), lax.axis_index("c")


def all_gather(block):
    r, w = block.shape

    def body(x_ref, out_ref, send_sems, recv_sems, local_sem):
        x, y, c = _place()
        me, sibling = (x, y, c), (x, y, 1 - c)
        chips = [(1 - x, y), (x, 1 - y), (1 - x, 1 - y)]

        def slot(px, py, pc):
            return out_ref.at[4 * px + 2 * py + pc]

        def copy(k, blk, to, src=None):
            return pltpu.make_async_remote_copy(
                src_ref=slot(*blk) if src is None else src, dst_ref=slot(*blk),
                send_sem=send_sems.at[k], recv_sem=recv_sems.at[k], device_id=to, device_id_type=MESH)

        mine = pltpu.make_async_copy(x_ref, slot(*me), local_sem)
        mine.start()
        first = [copy(0, me, sibling, src=x_ref)]
        first += [copy(1 + j, me, (*chip, c), src=x_ref) for j, chip in enumerate(chips)]
        for cp in first:
            cp.start()
        passed = [copy(4 + j, (*chip, c), sibling) for j, chip in enumerate(chips)]
        for j, chip in enumerate(chips):
            copy(1 + j, (*chip, c), me).wait_recv()
            passed[j].start()
        copy(0, sibling, me).wait_recv()
        for j, chip in enumerate(chips):
            copy(4 + j, (*chip, 1 - c), me).wait_recv()
        for cp in first + passed:
            cp.wait_send()
        mine.wait()

    return pl.pallas_call(
        body, name="all_gather",
        in_specs=[pl.BlockSpec(memory_space=pl.ANY)],
        out_specs=pl.BlockSpec(memory_space=pl.ANY),
        out_shape=jax.ShapeDtypeStruct((N_DEV, r, w), block.dtype),
        scratch_shapes=[pltpu.SemaphoreType.DMA((7,)), pltpu.SemaphoreType.DMA((7,)), pltpu.SemaphoreType.DMA],
    )(block)


def grad_exchange(big, small):
    _, r, w = big.shape
    rs = small.shape[0]

    def body(big_ref, small_ref, obig_ref, osmall_ref, send_sems, recv_sems, local_sems):
        x, y, c = _place()
        me = 4 * x + 2 * y + c
        own_big = pltpu.make_async_copy(big_ref.at[me], obig_ref.at[me], local_sems.at[0])
        own_small = pltpu.make_async_copy(small_ref, osmall_ref.at[me], local_sems.at[1])
        own_big.start()
        own_small.start()
        copies = []
        for k in range(1, N_DEV):
            px = 1 - x if k & 4 else x
            py = 1 - y if k & 2 else y
            pc = 1 - c if k & 1 else c
            peer = 4 * px + 2 * py + pc
            copies.append(pltpu.make_async_remote_copy(
                src_ref=big_ref.at[peer], dst_ref=obig_ref.at[me], send_sem=send_sems.at[2 * (k - 1)],
                recv_sem=recv_sems.at[2 * (k - 1)], device_id=(px, py, pc), device_id_type=MESH))
            copies.append(pltpu.make_async_remote_copy(
                src_ref=small_ref, dst_ref=osmall_ref.at[me], send_sem=send_sems.at[2 * (k - 1) + 1],
                recv_sem=recv_sems.at[2 * (k - 1) + 1], device_id=(px, py, pc), device_id_type=MESH))
        for cp in copies:
            cp.start()
        for cp in copies:
            cp.wait_recv()
        for cp in copies:
            cp.wait_send()
        own_big.wait()
        own_small.wait()

    return pl.pallas_call(
        body, name="grad_exchange",
        in_specs=[pl.BlockSpec(memory_space=pl.ANY), pl.BlockSpec(memory_space=pl.ANY)],
        out_specs=[pl.BlockSpec(memory_space=pl.ANY), pl.BlockSpec(memory_space=pl.ANY)],
        out_shape=[jax.ShapeDtypeStruct((N_DEV, r, w), big.dtype), jax.ShapeDtypeStruct((N_DEV, rs, w), small.dtype)],
        scratch_shapes=[pltpu.SemaphoreType.DMA((14,)), pltpu.SemaphoreType.DMA((14,)), pltpu.SemaphoreType.DMA((2,))],
    )(big, small)


def _row_tile(r, cap):
    return max(d for d in range(16, cap + 1, 16) if r % d == 0)


def sum_adamw(parts, w, m, v, cap):
    r = w.shape[0]
    tr = _row_tile(r, cap)
    c1 = 1.0 / (1.0 - ADAM_B1 ** ADAM_STEP)
    c2 = 1.0 / (1.0 - ADAM_B2 ** ADAM_STEP)

    def body(p_ref, w_ref, m_ref, v_ref, g_ref, d_ref, nm_ref, nv_ref):
        g = p_ref[0].astype(F32)
        for k in range(1, N_DEV):
            g = g + p_ref[k].astype(F32)
        m = ADAM_B1 * m_ref[...] + (1.0 - ADAM_B1) * g
        v = ADAM_B2 * v_ref[...] + (1.0 - ADAM_B2) * (g * g)
        g_ref[...] = g
        nm_ref[...] = m
        nv_ref[...] = v
        d_ref[...] = -ADAM_LR * ((m * c1) / (jnp.sqrt(v * c2) + ADAM_EPS) + ADAM_WD * w_ref[...])

    blk = pl.BlockSpec((tr, 128), lambda i: (i, 0))
    out = jax.ShapeDtypeStruct((r, 128), F32)
    return pl.pallas_call(
        body, name="sum_adamw", grid=(r // tr,),
        in_specs=[pl.BlockSpec((N_DEV, tr, 128), lambda i: (0, i, 0)), blk, blk, blk],
        out_specs=[blk, blk, blk, blk], out_shape=[out, out, out, out],
        compiler_params=_cp(("parallel",)),
    )(parts, w, m, v)


BIG = (("w_in", (D, IN_W), 1), ("w_uq", (Q_RANK, HEADS * (NOPE + ROPE)), 1), ("w_ukv", (KV_RANK, HEADS * (NOPE + V_DIM)), 1),
       ("w_out", (D, D), 0), ("w_gate", (D, D_FF), 1), ("w_up", (D, D_FF), 1), ("w_down", (D_FF, D), 0))
CONV_ROWS = 16
SMALL = (("mix_pre_g", (D,)), ("mix_post_g", (D,)), ("ffn_pre_g", (D,)), ("ffn_post_g", (D,)), ("q_norm_g", (Q_RANK,)),
         ("kv_norm_g", (KV_RANK,)), ("sg_ln_g", (SG_W,)), ("sg_ln_b", (SG_W,)), ("w_sp", (4, CHUNK, CHUNK)),
         ("b_sp", (4, CHUNK)), ("out_norm_g", (D,)))
SMALL_ROWS = 576


def _shard_shape(shape, axis):
    s = list(shape)
    s[axis] //= N_DEV
    return tuple(s)


def _big_rows():
    return [int(np.prod(_shard_shape(shape, axis))) // 128 for _, shape, axis in BIG]


LAYER_ROWS = sum(_big_rows()) + CONV_ROWS


def _pack_big(shards, conv=None, dtype=BF16):
    nl = shards["w_in"].shape[0]
    parts = [shards[name].astype(dtype).reshape(nl, -1, 128) for name, _, _ in BIG]
    if conv is None:
        tail = jnp.zeros((nl, CONV_ROWS, 128), dtype)
    else:
        bits = lax.bitcast_convert_type(conv.astype(F32), BF16).reshape(nl, -1)
        tail = jnp.pad(bits, ((0, 0), (0, CONV_ROWS * 128 - bits.shape[1]))).reshape(nl, CONV_ROWS, 128)
    return jnp.concatenate(parts + [tail], axis=1).reshape(nl * LAYER_ROWS, 128)


def _unpack_big(pack, nl):
    pack = pack.reshape(nl, LAYER_ROWS, 128)
    out, off = {}, 0
    for (name, shape, axis), rows in zip(BIG, _big_rows()):
        out[name] = pack[:, off:off + rows].reshape((nl,) + _shard_shape(shape, axis))
        off += rows
    return out


def _gathered_weights(g, nl):
    g = g.reshape(N_DEV, nl, LAYER_ROWS, 128)
    layers = []
    for l in range(nl):
        off, ws = 0, {}
        for (name, shape, axis), rows in zip(BIG, _big_rows()):
            blk = g[:, l, off:off + rows].reshape((N_DEV,) + _shard_shape(shape, axis))
            off += rows
            if axis == 0:
                ws[name] = blk.reshape(shape)
            else:
                ws[name] = jnp.moveaxis(blk, 0, 1).reshape(shape)
        bits = g[:, l, off:off + CONV_ROWS].reshape(N_DEV, -1)[:, :3 * 32 * 2].reshape(N_DEV, 3, 32, 2)
        ws["conv_w"] = jnp.moveaxis(lax.bitcast_convert_type(bits, F32), 0, 1).reshape(3, CV_W)
        layers.append(ws)
    return layers


def _kernel_weights(ws):
    w_in = ws["w_in"]
    w_in_p = jnp.concatenate([w_in[:, 1184:], w_in[:, :672], jnp.zeros((D, 96), BF16), w_in[:, 672:1184]], axis=1)
    w_uq_p = jnp.pad(ws["w_uq"].reshape(Q_RANK, HEADS, NOPE + ROPE),
                     ((0, 0), (0, 0), (0, HEAD_PAD - NOPE - ROPE))).reshape(Q_RANK, HEADS * HEAD_PAD)
    kv = ws["w_ukv"].reshape(KV_RANK, HEADS, NOPE + V_DIM)
    w_k = jnp.pad(kv[:, :, :NOPE], ((0, 0), (0, 0), (0, HEAD_PAD - NOPE))).reshape(KV_RANK, HEADS * HEAD_PAD)
    w_ukv_p = jnp.concatenate([w_k, kv[:, :, NOPE:].reshape(KV_RANK, HEADS * V_DIM)], axis=1)
    return dict(w_in=w_in_p, w_uq=w_uq_p, w_ukv=w_ukv_p, w_out=ws["w_out"],
                w_gu=jnp.concatenate([ws["w_gate"], ws["w_up"]], axis=1), w_down=ws["w_down"], conv_w=ws["conv_w"])


def _grad_chunks(full):
    d_in = full["w_in"]
    d_in = jnp.concatenate([d_in[:, 768:768 + 672], d_in[:, 1536:], d_in[:, :768]], axis=1)
    d_uq = full["w_uq"].reshape(Q_RANK, HEADS, HEAD_PAD)[:, :, :NOPE + ROPE].reshape(Q_RANK, -1)
    d_k = full["w_ukv"][:, :HEADS * HEAD_PAD].reshape(KV_RANK, HEADS, HEAD_PAD)[:, :, :NOPE]
    d_v = full["w_ukv"][:, HEADS * HEAD_PAD:].reshape(KV_RANK, HEADS, V_DIM)
    d_ukv = jnp.concatenate([d_k, d_v], axis=2).reshape(KV_RANK, -1)
    mats = dict(w_in=d_in, w_uq=d_uq, w_ukv=d_ukv, w_out=full["w_out"], w_gate=full["w_gu"][:, :D_FF],
                w_up=full["w_gu"][:, D_FF:], w_down=full["w_down"])
    out = {}
    for name, shape, axis in BIG:
        m = mats[name]
        if axis == 0:
            out[name] = m.reshape((N_DEV,) + _shard_shape(shape, axis))
        else:
            out[name] = jnp.moveaxis(m.reshape(shape[0], N_DEV, shape[1] // N_DEV), 1, 0)
    return out


def _pack_small(vals, nl):
    flat = jnp.concatenate([vals[name].reshape(nl, -1) for name, _ in SMALL] + [vals["conv_w"].reshape(nl, -1)], axis=1)
    return jnp.pad(flat, ((0, 0), (0, SMALL_ROWS * 128 - flat.shape[1]))).reshape(nl * SMALL_ROWS, 128)


def _unpack_small(pack, nl):
    flat = pack.reshape(nl, SMALL_ROWS * 128)
    out, off = {}, 0
    for name, shape in SMALL + (("conv_w", (3, CV_W)),):
        n = int(np.prod(shape))
        out[name] = flat[:, off:off + n].reshape((nl,) + shape)
        off += n
    return out


def _layer_fwd(x, lw, sp, tabs, consts):
    ca, sb, sc = tabs
    z = pre_in_fwd(x, sp["mix_pre_g"], lw["w_in"])
    q, k, v = mla_proj_fwd(z, ca, sb, sc, sp["q_norm_g"], sp["kv_norm_g"], lw["w_uq"], lw["w_ukv"])
    ya, lse = attn_fwd(q, k, v)
    x1 = mix_fwd(x, z, ya, consts["gm"], sp["sg_ln_g"], sp["sg_ln_b"], sp["w_sp"], sp["bias"], lw["conv_w"],
                 sp["out_norm_g"], lw["w_out"], sp["mix_post_g"])
    x2 = ffn_fwd(x1, sp["ffn_pre_g"], lw["w_gu"], lw["w_down"], sp["ffn_post_g"])
    return x2, (x, z, q, k, v, ya, lse, x1)


def _layer_bwd(dx2, saved, lw, sp, tabs, consts):
    ca, sb, sc = tabs
    x, z, q, k, v, ya, lse, x1 = saved
    dx1, h2, dab, s, df, d_ffn_pre, d_ffn_post = ffn_bwd(x1, dx2, sp["ffn_pre_g"], lw["w_gu"], lw["w_down"], sp["ffn_post_g"])
    d_w_gu = atb(h2, dab, 1408)
    d_w_down = atb(s, df, 512)
    dya, dyc, dzsg, mix, do, d_mix_post, d_out_norm, d_lng, d_lnb, d_wsp, d_bias = mix_bwd(
        dx1, z, ya, consts["gm"], sp["sg_ln_g"], sp["sg_ln_b"], sp["w_sp"], sp["w_sp_t"], sp["bias"], lw["conv_w"],
        sp["out_norm_g"], lw["w_out"], sp["mix_post_g"])
    d_w_out = atb(mix, do, 1024)
    dzcv, d_cw = conv_bwd(dyc, z, lw["conv_w"])
    dq, dk, dv = attn_bwd(q, k, v, ya, lse, dya)
    dzmla, cq, ckv, dqp, dkvp, d_gq, d_gkv = mla_proj_bwd(dq, dk, dv, z, ca, sb, sc, sp["q_norm_g"], sp["kv_norm_g"],
                                                          lw["w_uq"], lw["w_ukv"])
    d_w_uq = atb(cq, dqp, 1024)
    d_w_ukv = atb(ckv, dkvp, 1536)
    dx, h1, dz, d_mix_pre = pre_in_bwd(x, dx1, dzcv, dzmla, dzsg, sp["mix_pre_g"], lw["w_in"])
    d_w_in = atb(h1, dz, 1024)
    big = dict(w_in=d_w_in, w_uq=d_w_uq, w_ukv=d_w_ukv, w_out=d_w_out, w_gu=d_w_gu, w_down=d_w_down)
    d_bsp = d_bias[:, ::GROUP].T
    small = dict(mix_pre_g=d_mix_pre[0], mix_post_g=d_mix_post[0], ffn_pre_g=d_ffn_pre[0], ffn_post_g=d_ffn_post[0],
                 q_norm_g=d_gq[0], kv_norm_g=d_gkv[0], sg_ln_g=d_lng[0], sg_ln_b=d_lnb[0], w_sp=d_wsp, b_sp=d_bsp,
                 out_norm_g=d_out_norm[0], conv_w=d_cw[:3])
    return dx, big, small


def kernel(x, positions, mix_pre_g, mix_post_g, ffn_pre_g, ffn_post_g, w_in, q_norm_g, w_uq, kv_norm_g, w_ukv, sg_ln_g, sg_ln_b, w_sp, b_sp, conv_w, out_norm_g, w_out, w_gate, w_up, w_down, loss_target, m_mix_pre_g, m_mix_post_g, m_ffn_pre_g, m_ffn_post_g, m_w_in, m_q_norm_g, m_w_uq, m_kv_norm_g, m_w_ukv, m_sg_ln_g, m_sg_ln_b, m_w_sp, m_b_sp, m_conv_w, m_out_norm_g, m_w_out, m_w_gate, m_w_up, m_w_down, v_mix_pre_g, v_mix_post_g, v_ffn_pre_g, v_ffn_post_g, v_w_in, v_q_norm_g, v_w_uq, v_kv_norm_g, v_w_ukv, v_sg_ln_g, v_sg_ln_b, v_w_sp, v_b_sp, v_conv_w, v_out_norm_g, v_w_out, v_w_gate, v_w_up, v_w_down):
    nl = w_in.shape[0]
    t = x.shape[1]
    w = dict(mix_pre_g=mix_pre_g, mix_post_g=mix_post_g, ffn_pre_g=ffn_pre_g, ffn_post_g=ffn_post_g, w_in=w_in,
             q_norm_g=q_norm_g, w_uq=w_uq, kv_norm_g=kv_norm_g, w_ukv=w_ukv, sg_ln_g=sg_ln_g, sg_ln_b=sg_ln_b, w_sp=w_sp,
             b_sp=b_sp, conv_w=conv_w, out_norm_g=out_norm_g, w_out=w_out, w_gate=w_gate, w_up=w_up, w_down=w_down)
    m = dict(mix_pre_g=m_mix_pre_g, mix_post_g=m_mix_post_g, ffn_pre_g=m_ffn_pre_g, ffn_post_g=m_ffn_post_g, w_in=m_w_in,
             q_norm_g=m_q_norm_g, w_uq=m_w_uq, kv_norm_g=m_kv_norm_g, w_ukv=m_w_ukv, sg_ln_g=m_sg_ln_g, sg_ln_b=m_sg_ln_b,
             w_sp=m_w_sp, b_sp=m_b_sp, conv_w=m_conv_w, out_norm_g=m_out_norm_g, w_out=m_w_out, w_gate=m_w_gate,
             w_up=m_w_up, w_down=m_w_down)
    v = dict(mix_pre_g=v_mix_pre_g, mix_post_g=v_mix_post_g, ffn_pre_g=v_ffn_pre_g, ffn_post_g=v_ffn_post_g, w_in=v_w_in,
             q_norm_g=v_q_norm_g, w_uq=v_w_uq, kv_norm_g=v_kv_norm_g, w_ukv=v_w_ukv, sg_ln_g=v_sg_ln_g, sg_ln_b=v_sg_ln_b,
             w_sp=v_w_sp, b_sp=v_b_sp, conv_w=v_conv_w, out_norm_g=v_out_norm_g, w_out=v_w_out, w_gate=v_w_gate,
             w_up=v_w_up, w_down=v_w_down)

    gathered = all_gather(_pack_big(w, conv=w["conv_w"]))
    layers = [_kernel_weights(ws) for ws in _gathered_weights(gathered, nl)]
    consts = dict(gm=jnp.asarray(np.kron(np.eye(SG_W // GROUP), np.full((GROUP, GROUP), 1.0 / GROUP)), F32))
    smalls = []
    for l in range(nl):
        sp = {name: w[name][l].reshape(1, -1) for name, shape in SMALL if len(shape) == 1}
        sp["w_sp"] = w["w_sp"][l]
        sp["w_sp_t"] = jnp.swapaxes(w["w_sp"][l], 1, 2)
        sp["bias"] = jnp.repeat(w["b_sp"][l].T, GROUP, axis=1)
        smalls.append(sp)
    inv_freq = 1.0 / (ROPE_THETA ** (jnp.arange(0, ROPE // 2, dtype=F32) / (ROPE // 2)))
    inv = jnp.zeros((1, HEAD_PAD), F32).at[0, NOPE:NOPE + ROPE].set(jnp.concatenate([inv_freq, inv_freq]))
    tabs = rope_tables(positions.reshape(t, 1).astype(F32), inv)

    h = x[0]
    saved = []
    for l in range(nl):
        h, s = _layer_fwd(h, layers[l], smalls[l], tabs, consts)
        saved.append(s)
    sq, dh = loss_head(h, loss_target[0])
    loss = lax.psum(0.5 * sq[0, 0] / D, ("x", "y", "c"))
    big_grads, small_grads = [None] * nl, [None] * nl
    for l in reversed(range(nl)):
        dh, big_grads[l], small_grads[l] = _layer_bwd(dh, saved[l], layers[l], smalls[l], tabs, consts)

    chunks = [_grad_chunks(g) for g in big_grads]
    send_big = jnp.stack([_pack_big({name: jnp.stack([ch[name][j] for ch in chunks]) for name, _, _ in BIG})
                          for j in range(N_DEV)])
    send_small = _pack_small({name: jnp.stack([g[name] for g in small_grads]) for name in small_grads[0]}, nl)
    got_big, got_small = grad_exchange(send_big, send_small)

    me = 4 * lax.axis_index("x") + 2 * lax.axis_index("y") + lax.axis_index("c")
    g_big, d_big, m_big, v_big = [_unpack_big(p, nl) for p in
                                  sum_adamw(got_big, _pack_big(w, dtype=F32), _pack_big(m, dtype=F32),
                                            _pack_big(v, dtype=F32), 2048)]

    def full_conv(a):
        return lax.dynamic_update_slice(jnp.zeros((nl, 3, CV_W), F32), a, (0, 0, me * (CV_W // N_DEV)))

    def small_pack(d):
        return _pack_small({**{name: d[name] for name, _ in SMALL}, "conv_w": full_conv(d["conv_w"])}, nl)

    g_small, d_small, m_small, v_small = [_unpack_small(p, nl) for p in
                                          sum_adamw(got_small, small_pack(w), small_pack(m), small_pack(v), 1152)]
    outs = []
    for big, small in ((g_big, g_small), (d_big, d_small), (m_big, m_small), (v_big, v_small)):
        for name in w:
            if name == "conv_w":
                outs.append(lax.dynamic_slice(small[name], (0, 0, me * (CV_W // N_DEV)), (nl, 3, CV_W // N_DEV)))
            elif name in small:
                outs.append(small[name])
            else:
                outs.append(big[name])
    return (loss, dh[None], *outs)
```

```python
import functools

import jax
import jax.numpy as jnp
import numpy as np
from jax import lax
from jax.experimental import pallas as pl
from jax.experimental.pallas import tpu as pltpu

F32 = jnp.float32
BF16 = jnp.bfloat16

D = 1024
Q_RANK = 384
KV_RANK = 256
ROPE = 32
HEADS = 8
NOPE = 64
V_DIM = 64
HEAD_PAD = 128
SG_W = 256
CV_W = 256
CHUNK = 128
GROUP = 64
D_FF = 2816
IN_W = 1952
Z_W = 2048
Z_CV, Z_MLA, Z_SG = 0, 768, 1536
EPS = 1e-6
ROPE_THETA = 10000.0
SCALE = (NOPE + ROPE) ** -0.5
NEG = -1e30
N_DEV = 8

ADAM_LR, ADAM_B1, ADAM_B2, ADAM_EPS, ADAM_WD, ADAM_STEP = 0.001, 0.9, 0.999, 1e-08, 0.01, 10

VMEM_LIMIT = 56 * 1024 * 1024

TM = 512
TM_FFN = 256
FFN_SLAB = 256
TQ = 512
TT = 512


def _cp(sem, vmem=VMEM_LIMIT):
    return pltpu.CompilerParams(dimension_semantics=sem, vmem_limit_bytes=vmem)


def _whole():
    return pl.BlockSpec(memory_space=pltpu.VMEM)


def _mm(a, b):
    return jnp.dot(a, b, preferred_element_type=F32)


def _mm_nt(a, b):
    return lax.dot_general(a, b, (((1,), (1,)), ((), ())), preferred_element_type=F32)


def _mm_tn(a, b):
    return lax.dot_general(a, b, (((0,), (0,)), ((), ())), preferred_element_type=F32)


def _rms_fwd(x, g):
    r = lax.rsqrt(jnp.mean(x * x, axis=-1, keepdims=True) + EPS)
    xh = x * r
    return xh * g, xh, r


def _rms_bwd(xh, r, g, dy):
    dxh = dy * g
    dx = r * (dxh - xh * jnp.mean(dxh * xh, axis=-1, keepdims=True))
    dg = jnp.sum(dy * xh, axis=0, keepdims=True)
    return dx, dg


def _gmean(v, gm):
    return jnp.dot(v, gm, precision=lax.Precision.HIGHEST, preferred_element_type=F32)


def _gelu(x):
    c = np.float32(np.sqrt(2.0 / np.pi))
    u = c * (x + 0.044715 * x * x * x)
    t = jnp.tanh(u)
    return 0.5 * x * (1.0 + t), t


def _gelu_grad(x, t):
    c = np.float32(np.sqrt(2.0 / np.pi))
    return 0.5 * (1.0 + t) + 0.5 * x * (1.0 - t * t) * c * (1.0 + 3.0 * 0.044715 * x * x)


def _rope(t, ca, sb, sc):
    return t * ca + pltpu.roll(t, HEAD_PAD - 16, 1) * sb + pltpu.roll(t, 16, 1) * sc


def _rope_t(dt, ca, sb, sc):
    return dt * ca + pltpu.roll(dt * sb, 16, 1) + pltpu.roll(dt * sc, HEAD_PAD - 16, 1)


def _shift_down(y, k, head):
    n = y.shape[0]
    out = pltpu.roll(y, k, 0)
    row = lax.broadcasted_iota(jnp.int32, y.shape, 0)
    for j in range(k):
        out = jnp.where(row == j, head[8 - k + j:8 - k + j + 1, :], out)
    return out


def _shift_up(y, k, tail):
    n = y.shape[0]
    out = pltpu.roll(y, n - k, 0)
    row = lax.broadcasted_iota(jnp.int32, y.shape, 0)
    for j in range(k):
        out = jnp.where(row == n - k + j, tail[j:j + 1, :], out)
    return out


def rope_tables(pos, inv):
    t = pos.shape[0]
    tm = min(TM, t)

    def body(pos_ref, inv_ref, ca_ref, sb_ref, sc_ref):
        ang = pos_ref[...] * inv_ref[...]
        c = jnp.cos(ang)
        s = jnp.sin(ang)
        lane = lax.broadcasted_iota(jnp.int32, ang.shape, 1)
        ca_ref[...] = jnp.where(lane < NOPE, 1.0, jnp.where(lane < NOPE + ROPE, c, 0.0))
        sb_ref[...] = jnp.where((lane >= NOPE) & (lane < NOPE + 16), -s, 0.0)
        sc_ref[...] = jnp.where((lane >= NOPE + 16) & (lane < NOPE + ROPE), s, 0.0)

    out = jax.ShapeDtypeStruct((t, HEAD_PAD), F32)
    blk = pl.BlockSpec((tm, HEAD_PAD), lambda i: (i, 0))
    return pl.pallas_call(
        body, name="rope_tables", grid=(t // tm,),
        in_specs=[pl.BlockSpec((tm, 1), lambda i: (i, 0)), pl.BlockSpec((1, HEAD_PAD), lambda i: (0, 0))],
        out_specs=[blk, blk, blk], out_shape=[out, out, out],
        compiler_params=_cp(("parallel",)),
    )(pos, inv)


def pre_in_fwd(x, g, w):
    t = x.shape[0]
    tm = min(TM, t)

    def body(x_ref, g_ref, w_ref, z_ref):
        h, _, _ = _rms_fwd(x_ref[...], g_ref[...])
        z_ref[...] = _mm_nt(h.astype(BF16), w_ref[...])

    return pl.pallas_call(
        body, name="pre_in_fwd", grid=(t // tm,),
        in_specs=[pl.BlockSpec((tm, D), lambda i: (i, 0)), _whole(), _whole()],
        out_specs=pl.BlockSpec((tm, Z_W), lambda i: (i, 0)),
        out_shape=jax.ShapeDtypeStruct((t, Z_W), F32),
        compiler_params=_cp(("parallel",)),
    )(x, g, w)


def mla_proj_fwd(z, ca, sb, sc, gq, gkv, wuq, wukv):
    t = z.shape[0]
    tm = min(TM, t)

    def body(z_ref, ca_ref, sb_ref, sc_ref, gq_ref, gkv_ref, wuq_ref, wukv_ref, q_ref, k_ref, v_ref):
        z = z_ref[...]
        ca, sb, sc = ca_ref[...], sb_ref[...], sc_ref[...]
        cq, _, _ = _rms_fwd(z[:, :Q_RANK], gq_ref[...])
        ckv, _, _ = _rms_fwd(z[:, Q_RANK:Q_RANK + KV_RANK], gkv_ref[...])
        q = _mm_nt(cq.astype(BF16), wuq_ref[...])
        kv = _mm_nt(ckv.astype(BF16), wukv_ref[...])
        kr = _rope(pltpu.roll(z[:, Q_RANK + KV_RANK:], NOPE, 1), ca, sb, sc)
        for h in range(HEADS):
            lanes = slice(h * HEAD_PAD, (h + 1) * HEAD_PAD)
            q_ref[:, lanes] = _rope(q[:, lanes], ca, sb, sc).astype(BF16)
            k_ref[:, lanes] = (kv[:, lanes] + kr).astype(BF16)
        v_ref[...] = kv[:, HEADS * HEAD_PAD:].astype(BF16)

    tab = pl.BlockSpec((tm, HEAD_PAD), lambda i: (i, 0))
    return pl.pallas_call(
        body, name="mla_proj_fwd", grid=(t // tm,),
        in_specs=[pl.BlockSpec((tm, 768), lambda i: (i, 1)), tab, tab, tab, _whole(), _whole(), _whole(), _whole()],
        out_specs=[pl.BlockSpec((tm, HEADS * HEAD_PAD), lambda i: (i, 0)),
                   pl.BlockSpec((tm, HEADS * HEAD_PAD), lambda i: (i, 0)),
                   pl.BlockSpec((tm, HEADS * V_DIM), lambda i: (i, 0))],
        out_shape=[jax.ShapeDtypeStruct((t, HEADS * HEAD_PAD), BF16),
                   jax.ShapeDtypeStruct((t, HEADS * HEAD_PAD), BF16),
                   jax.ShapeDtypeStruct((t, HEADS * V_DIM), BF16)],
        compiler_params=_cp(("parallel",)),
    )(z, ca, sb, sc, gq, gkv, wuq, wukv)


def attn_fwd(q, k, v):
    t = q.shape[0]
    tq = min(TQ, t)
    nq = t // tq

    def body(q_ref, k_ref, v_ref, o_ref, lse_ref):
        i = pl.program_id(1)
        row = lax.broadcasted_iota(jnp.int32, (tq, tq), 0)
        col = lax.broadcasted_iota(jnp.int32, (tq, tq), 1)
        outs = []
        for h in range(2):
            lanes = slice(h * HEAD_PAD, (h + 1) * HEAD_PAD)
            qh = q_ref[:, lanes]

            def step(j, carry, masked):
                m, l, acc = carry
                start = pl.multiple_of(j * tq, tq)
                kb = k_ref[pl.ds(start, tq), lanes]
                vb = v_ref[pl.ds(start, tq), :]
                s = _mm_nt(qh, kb) * SCALE
                if masked:
                    s = jnp.where(col <= row, s, NEG)
                m_new = jnp.maximum(m, jnp.max(s, axis=-1, keepdims=True))
                p = jnp.exp(s - m_new)
                alpha = jnp.exp(m - m_new)
                l = alpha * l + jnp.sum(p, axis=-1, keepdims=True)
                acc = alpha * acc + _mm(p.astype(BF16), vb)
                return m_new, l, acc

            init = (jnp.full((tq, 1), NEG, F32), jnp.zeros((tq, 1), F32), jnp.zeros((tq, 2 * V_DIM), F32))
            carry = lax.fori_loop(0, i, lambda j, c: step(j, c, False), init)
            m, l, acc = step(i, carry, True)
            outs.append(acc / l)
            lse_ref[:, lanes] = jnp.broadcast_to(m + jnp.log(l), (tq, HEAD_PAD))
        lane = lax.broadcasted_iota(jnp.int32, (tq, 2 * V_DIM), 1)
        o_ref[...] = jnp.where(lane < V_DIM, outs[0], outs[1])

    return pl.pallas_call(
        body, name="attn_fwd", grid=(HEADS // 2, nq),
        in_specs=[pl.BlockSpec((tq, 2 * HEAD_PAD), lambda p, i: (i, p)),
                  pl.BlockSpec((t, 2 * HEAD_PAD), lambda p, i: (0, p)),
                  pl.BlockSpec((t, 2 * V_DIM), lambda p, i: (0, p))],
        out_specs=[pl.BlockSpec((tq, 2 * V_DIM), lambda p, i: (i, p)),
                   pl.BlockSpec((tq, 2 * HEAD_PAD), lambda p, i: (i, p))],
        out_shape=[jax.ShapeDtypeStruct((t, HEADS * V_DIM), F32), jax.ShapeDtypeStruct((t, HEADS * HEAD_PAD), F32)],
        compiler_params=_cp(("parallel", "parallel")),
    )(q, k, v)


def _sgu_fwd(zsg, gm, lng, lnb, wc_ref, bias, mixed_ref):
    uv, th = _gelu(zsg)
    u, v0 = uv[:, :SG_W], uv[:, SG_W:]
    vc = v0 - _gmean(v0, gm)
    r = lax.rsqrt(_gmean(vc * vc, gm) + EPS)
    vh = vc * r
    v = vh * lng + lnb
    lane = lax.broadcasted_iota(jnp.int32, (CHUNK, SG_W), 1)
    for c in range(zsg.shape[0] // CHUNK):
        rows = slice(c * CHUNK, (c + 1) * CHUNK)
        vb = v[rows].astype(BF16)
        mixed = bias
        for g in range(SG_W // GROUP):
            mixed = mixed + jnp.where(lane // GROUP == g, _mm(wc_ref[g], vb), 0.0)
        mixed_ref[rows, :] = mixed
    return u, v, vh, r, th


def _conv_fwd(zcv, halo, first, cw):
    gb, gc, hh = zcv[:, :CV_W], zcv[:, CV_W:2 * CV_W], zcv[:, 2 * CV_W:]
    y = gc * hh
    yh = jnp.where(first, 0.0, halo[:, CV_W:2 * CV_W] * halo[:, 2 * CV_W:])
    y1 = _shift_down(y, 1, yh)
    y2 = _shift_down(y, 2, yh)
    conv = y2 * cw[0:1, :] + y1 * cw[1:2, :] + y * cw[2:3, :]
    return gb * conv, conv, y, y1, y2


def _tril_bf16(w_ref, g):
    row = lax.broadcasted_iota(jnp.int32, (CHUNK, CHUNK), 0)
    col = lax.broadcasted_iota(jnp.int32, (CHUNK, CHUNK), 1)
    return jnp.where(col <= row, w_ref[g], 0.0).astype(BF16)


def mix_fwd(x, z, ya, gm, lng, lnb, wsp, bias, cw, gout, wout, gpost):
    t = x.shape[0]
    tm = min(TM, t)

    def body(x_ref, zcv_ref, halo_ref, zsg_ref, ya_ref, gm_ref, lng_ref, lnb_ref, wsp_ref, bias_ref, cw_ref,
             gout_ref, wout_ref, gpost_ref, x1_ref, wc_ref, mixed_ref):
        i = pl.program_id(0)
        for g in range(SG_W // GROUP):
            wc_ref[g] = _tril_bf16(wsp_ref, g)
        u, _, _, _, _ = _sgu_fwd(zsg_ref[...], gm_ref[...], lng_ref[...], lnb_ref[...], wc_ref, bias_ref[...], mixed_ref)
        yb = u * mixed_ref[...]
        yc, _, _, _, _ = _conv_fwd(zcv_ref[...], halo_ref[...], i == 0, cw_ref[...])
        gout = gout_ref[...]
        na, _, _ = _rms_fwd(ya_ref[...], gout[:, :512])
        nb, _, _ = _rms_fwd(yb, gout[:, 512:768])
        nc, _, _ = _rms_fwd(yc, gout[:, 768:])
        mix = jnp.concatenate([na, nb, nc], axis=1).astype(BF16)
        o, _, _ = _rms_fwd(_mm(mix, wout_ref[...]), gpost_ref[...])
        x1_ref[...] = x_ref[...] + o

    hb = tm // 8
    return pl.pallas_call(
        body, name="mix_fwd", grid=(t // tm,),
        in_specs=[pl.BlockSpec((tm, D), lambda i: (i, 0)),
                  pl.BlockSpec((tm, 768), lambda i: (i, 0)),
                  pl.BlockSpec((8, 768), lambda i: (jnp.maximum(i * hb - 1, 0), 0)),
                  pl.BlockSpec((tm, 512), lambda i: (i, 3)),
                  pl.BlockSpec((tm, 512), lambda i: (i, 0)),
                  _whole(), _whole(), _whole(), _whole(), _whole(), _whole(), _whole(), _whole(), _whole()],
        out_specs=pl.BlockSpec((tm, D), lambda i: (i, 0)),
        out_shape=jax.ShapeDtypeStruct((t, D), F32),
        scratch_shapes=[pltpu.VMEM((SG_W // GROUP, CHUNK, CHUNK), BF16), pltpu.VMEM((tm, SG_W), F32)],
        compiler_params=_cp(("arbitrary",)),
    )(x, z, z, z, ya, gm, lng, lnb, wsp, bias, cw, gout, wout, gpost)


def _sigmoid(a):
    return 1.0 / (1.0 + jnp.exp(-a))


def ffn_fwd(x1, gpre, wgu, wd, gpost):
    t = x1.shape[0]
    tm = min(TM_FFN, t)

    def body(x_ref, gpre_ref, wgu_ref, wd_ref, gpost_ref, x2_ref):
        x = x_ref[...]
        h, _, _ = _rms_fwd(x, gpre_ref[...])
        ab = _mm_nt(h.astype(BF16), wgu_ref[...])
        a, b = ab[:, :D_FF], ab[:, D_FF:]
        s = a * _sigmoid(a) * b
        f, _, _ = _rms_fwd(_mm(s.astype(BF16), wd_ref[...]), gpost_ref[...])
        x2_ref[...] = x + f

    return pl.pallas_call(
        body, name="ffn_fwd", grid=(t // tm,),
        in_specs=[pl.BlockSpec((tm, D), lambda i: (i, 0)), _whole(), _whole(), _whole(), _whole()],
        out_specs=pl.BlockSpec((tm, D), lambda i: (i, 0)),
        out_shape=jax.ShapeDtypeStruct((t, D), F32),
        compiler_params=_cp(("parallel",)),
    )(x1, gpre, wgu, wd, gpost)


def loss_head(y, target):
    t = y.shape[0]
    tm = min(TM, t)

    def body(y_ref, t_ref, loss_ref, dy_ref):
        @pl.when(pl.program_id(0) == 0)
        def _():
            loss_ref[...] = jnp.zeros_like(loss_ref)

        e = y_ref[...] - t_ref[...]
        dy_ref[...] = e * (1.0 / D)
        loss_ref[...] += jnp.sum(jnp.sum(e * e, axis=-1, keepdims=True), axis=0, keepdims=True)

    return pl.pallas_call(
        body, name="loss_head", grid=(t // tm,),
        in_specs=[pl.BlockSpec((tm, D), lambda i: (i, 0)), pl.BlockSpec((tm, D), lambda i: (i, 0))],
        out_specs=[pl.BlockSpec((1, 128), lambda i: (0, 0)), pl.BlockSpec((tm, D), lambda i: (i, 0))],
        out_shape=[jax.ShapeDtypeStruct((1, 128), F32), jax.ShapeDtypeStruct((t, D), F32)],
        compiler_params=_cp(("arbitrary",)),
    )(y, target)


def _acc(ref, first, val):
    @pl.when(first)
    def _():
        ref[...] = val

    @pl.when(jnp.logical_not(first))
    def _():
        ref[...] += val


def ffn_bwd(x1, dx2, gpre, wgu, wd, gpost):
    t = x1.shape[0]
    tm = min(TM_FFN, t)

    def body(x_ref, dx2_ref, gpre_ref, wgu_ref, wd_ref, gpost_ref,
             dx1_ref, h_ref, dab_ref, s_ref, df_ref, dgpre_ref, dgpost_ref, ab_ref, ds_ref):
        first = pl.program_id(0) == 0
        dx2 = dx2_ref[...]
        gpre, gpost = gpre_ref[...], gpost_ref[...]
        h, xh, rx = _rms_fwd(x_ref[...], gpre)
        h_ref[...] = h.astype(BF16)
        ab_ref[...] = _mm_nt(h_ref[...], wgu_ref[...])
        for c in range(0, D_FF, FFN_SLAB):
            a, b = ab_ref[:, c:c + FFN_SLAB], ab_ref[:, D_FF + c:D_FF + c + FFN_SLAB]
            s_ref[:, c:c + FFN_SLAB] = (a * _sigmoid(a) * b).astype(BF16)
        _, fh, rf = _rms_fwd(_mm(s_ref[...], wd_ref[...]), gpost)
        df, dgpost = _rms_bwd(fh, rf, gpost, dx2)
        df_ref[...] = df.astype(BF16)
        ds_ref[...] = _mm_nt(df_ref[...], wd_ref[...])
        for c in range(0, D_FF, FFN_SLAB):
            a, b = ab_ref[:, c:c + FFN_SLAB], ab_ref[:, D_FF + c:D_FF + c + FFN_SLAB]
            ds = ds_ref[:, c:c + FFN_SLAB]
            sg = _sigmoid(a)
            dab_ref[:, c:c + FFN_SLAB] = (ds * b * (sg * (1.0 + a * (1.0 - sg)))).astype(BF16)
            dab_ref[:, D_FF + c:D_FF + c + FFN_SLAB] = (ds * (a * sg)).astype(BF16)
        dx, dgpre = _rms_bwd(xh, rx, gpre, _mm(dab_ref[...], wgu_ref[...]))
        dx1_ref[...] = dx2 + dx
        _acc(dgpre_ref, first, dgpre)
        _acc(dgpost_ref, first, dgpost)

    row = lambda w: pl.BlockSpec((tm, w), lambda i: (i, 0))
    vec = pl.BlockSpec((1, D), lambda i: (0, 0))
    return pl.pallas_call(
        body, name="ffn_bwd", grid=(t // tm,),
        in_specs=[row(D), row(D), _whole(), _whole(), _whole(), _whole()],
        out_specs=[row(D), row(D), row(2 * D_FF), row(D_FF), row(D), vec, vec],
        out_shape=[jax.ShapeDtypeStruct((t, D), F32), jax.ShapeDtypeStruct((t, D), BF16),
                   jax.ShapeDtypeStruct((t, 2 * D_FF), BF16), jax.ShapeDtypeStruct((t, D_FF), BF16),
                   jax.ShapeDtypeStruct((t, D), BF16), jax.ShapeDtypeStruct((1, D), F32),
                   jax.ShapeDtypeStruct((1, D), F32)],
        scratch_shapes=[pltpu.VMEM((tm, 2 * D_FF), F32), pltpu.VMEM((tm, D_FF), F32)],
        compiler_params=_cp(("arbitrary",)),
    )(x1, dx2, gpre, wgu, wd, gpost)


def atb(a, b, tk):
    t, k = a.shape
    n = b.shape[1]
    tt = min(TT, t)
    tk = min(tk, k)
    steps = t // tt

    def body(a_ref, b_ref, o_ref, acc_ref):
        i = pl.program_id(1)
        _acc(acc_ref, i == 0, _mm_tn(a_ref[...], b_ref[...]))

        @pl.when(i == steps - 1)
        def _():
            o_ref[...] = acc_ref[...].astype(BF16)

    return pl.pallas_call(
        body, name="atb", grid=(k // tk, steps),
        in_specs=[pl.BlockSpec((tt, tk), lambda j, i: (i, j)), pl.BlockSpec((tt, n), lambda j, i: (i, 0))],
        out_specs=pl.BlockSpec((tk, n), lambda j, i: (j, 0)),
        out_shape=jax.ShapeDtypeStruct((k, n), BF16),
        scratch_shapes=[pltpu.VMEM((tk, n), F32)],
        compiler_params=_cp(("parallel", "arbitrary")),
    )(a, b)


def mix_bwd(dx1, z, ya, gm, lng, lnb, wsp, wspt, bias, cw, gout, wout, gpost):
    t = dx1.shape[0]
    tm = min(TM, t)
    ng = SG_W // GROUP

    def body(dx1_ref, zcv_ref, halo_ref, zsg_ref, ya_ref, gm_ref, lng_ref, lnb_ref, wsp_ref, wspt_ref, bias_ref,
             cw_ref, gout_ref, wout_ref, gpost_ref,
             dya_ref, dyc_ref, dzsg_ref, mix_ref, do_ref, dgpost_ref, dgout_ref, dlng_ref, dlnb_ref, dwsp_ref,
             dbias_ref, wc_ref, wct_ref, mixed_ref, dv_ref):
        i = pl.program_id(0)
        first = i == 0
        gm = gm_ref[...]
        for g in range(ng):
            wc_ref[g] = _tril_bf16(wsp_ref, g)
            wct_ref[g] = jnp.where(
                lax.broadcasted_iota(jnp.int32, (CHUNK, CHUNK), 0) <= lax.broadcasted_iota(jnp.int32, (CHUNK, CHUNK), 1),
                wspt_ref[g], 0.0).astype(BF16)
        zsg = zsg_ref[...]
        lng = lng_ref[...]
        u, v, vh, r, th = _sgu_fwd(zsg, gm, lng, lnb_ref[...], wc_ref, bias_ref[...], mixed_ref)
        mixed = mixed_ref[...]
        yb = u * mixed
        yc, _, _, _, _ = _conv_fwd(zcv_ref[...], halo_ref[...], first, cw_ref[...])
        gout, gpost = gout_ref[...], gpost_ref[...]
        ga, gb_, gc_ = gout[:, :512], gout[:, 512:768], gout[:, 768:]
        na, yah, ra = _rms_fwd(ya_ref[...], ga)
        nb, ybh, rb = _rms_fwd(yb, gb_)
        nc, ych, rc = _rms_fwd(yc, gc_)
        mix = jnp.concatenate([na, nb, nc], axis=1).astype(BF16)
        _, oh, ro = _rms_fwd(_mm(mix, wout_ref[...]), gpost)
        do, dgpost = _rms_bwd(oh, ro, gpost, dx1_ref[...])
        dob = do.astype(BF16)
        dmix = _mm_nt(dob, wout_ref[...])
        dya, dga = _rms_bwd(yah, ra, ga, dmix[:, :512])
        dyb, dgb = _rms_bwd(ybh, rb, gb_, dmix[:, 512:768])
        dyc, dgc = _rms_bwd(ych, rc, gc_, dmix[:, 768:])
        dya_ref[...] = dya
        dyc_ref[...] = dyc
        mix_ref[...] = mix
        do_ref[...] = dob
        _acc(dgpost_ref, first, dgpost)
        _acc(dgout_ref, first, jnp.concatenate([dga, dgb, dgc], axis=1))
        du = dyb * mixed
        dmixed = dyb * u
        lane = lax.broadcasted_iota(jnp.int32, (CHUNK, SG_W), 1)
        row = lax.broadcasted_iota(jnp.int32, (CHUNK, CHUNK), 0)
        col = lax.broadcasted_iota(jnp.int32, (CHUNK, CHUNK), 1)
        dbias = jnp.zeros((CHUNK, SG_W), F32)
        dw = [jnp.zeros((CHUNK, CHUNK), F32) for _ in range(ng)]
        for c in range(tm // CHUNK):
            rows = slice(c * CHUNK, (c + 1) * CHUNK)
            dm = dmixed[rows]
            dbias = dbias + dm
            dmb = dm.astype(BF16)
            vb = v[rows].astype(BF16)
            dvc = jnp.zeros((CHUNK, SG_W), F32)
            for g in range(ng):
                in_g = lane // GROUP == g
                dvc = dvc + jnp.where(in_g, _mm(wct_ref[g], dmb), 0.0)
                dw[g] = dw[g] + _mm_nt(jnp.where(in_g, dmb, jnp.zeros_like(dmb)), vb)
            dv_ref[rows, :] = dvc
        for g in range(ng):
            dwg = jnp.where(col <= row, dw[g], 0.0)

            @pl.when(first)
            def _():
                dwsp_ref[g] = dwg

            @pl.when(jnp.logical_not(first))
            def _():
                dwsp_ref[g] += dwg
        _acc(dbias_ref, first, _gmean(dbias, gm) * GROUP)
        dv = dv_ref[...]
        _acc(dlng_ref, first, jnp.sum(dv * vh, axis=0, keepdims=True))
        _acc(dlnb_ref, first, jnp.sum(dv, axis=0, keepdims=True))
        dvh = dv * lng
        dv0 = r * (dvh - _gmean(dvh, gm) - vh * _gmean(dvh * vh, gm))
        dzsg_ref[...] = (jnp.concatenate([du, dv0], axis=1) * _gelu_grad(zsg, th)).astype(BF16)

    hb = tm // 8
    row_ = lambda w: pl.BlockSpec((tm, w), lambda i: (i, 0))
    vec = lambda w: pl.BlockSpec((1, w), lambda i: (0, 0))
    return pl.pallas_call(
        body, name="mix_bwd", grid=(t // tm,),
        in_specs=[row_(D),
                  pl.BlockSpec((tm, 768), lambda i: (i, 0)),
                  pl.BlockSpec((8, 768), lambda i: (jnp.maximum(i * hb - 1, 0), 0)),
                  pl.BlockSpec((tm, 512), lambda i: (i, 3)),
                  row_(512),
                  _whole(), _whole(), _whole(), _whole(), _whole(), _whole(), _whole(), _whole(), _whole(), _whole()],
        out_specs=[row_(512), row_(CV_W), row_(512), row_(D), row_(D), vec(D), vec(D), vec(SG_W), vec(SG_W),
                   pl.BlockSpec((ng, CHUNK, CHUNK), lambda i: (0, 0, 0)),
                   pl.BlockSpec((CHUNK, SG_W), lambda i: (0, 0))],
        out_shape=[jax.ShapeDtypeStruct((t, 512), F32), jax.ShapeDtypeStruct((t, CV_W), F32),
                   jax.ShapeDtypeStruct((t, 512), BF16), jax.ShapeDtypeStruct((t, D), BF16),
                   jax.ShapeDtypeStruct((t, D), BF16), jax.ShapeDtypeStruct((1, D), F32),
                   jax.ShapeDtypeStruct((1, D), F32), jax.ShapeDtypeStruct((1, SG_W), F32),
                   jax.ShapeDtypeStruct((1, SG_W), F32), jax.ShapeDtypeStruct((ng, CHUNK, CHUNK), F32),
                   jax.ShapeDtypeStruct((CHUNK, SG_W), F32)],
        scratch_shapes=[pltpu.VMEM((ng, CHUNK, CHUNK), BF16), pltpu.VMEM((ng, CHUNK, CHUNK), BF16),
                        pltpu.VMEM((tm, SG_W), F32), pltpu.VMEM((tm, SG_W), F32)],
        compiler_params=_cp(("arbitrary",)),
    )(dx1, z, z, z, ya, gm, lng, lnb, wsp, wspt, bias, cw, gout, wout, gpost)


def conv_bwd(dyc, z, cw):
    t = dyc.shape[0]
    tm = min(TM, t)
    hb = tm // 8
    last_blk = t // 8 - 1

    def body(dyc_ref, dyct_ref, zcv_ref, head_ref, tail_ref, cw_ref, dz_ref, dcw_ref):
        i = pl.program_id(0)
        first = i == 0
        last = i == pl.num_programs(0) - 1
        cw = cw_ref[...]
        zcv = zcv_ref[...]
        gb, gc, hh = zcv[:, :CV_W], zcv[:, CV_W:2 * CV_W], zcv[:, 2 * CV_W:]
        _, conv, y, y1, y2 = _conv_fwd(zcv, head_ref[...], first, cw)
        dyc = dyc_ref[...]
        dconv = dyc * gb
        tail = jnp.where(last, 0.0, dyct_ref[...] * tail_ref[:, :CV_W])
        d1 = _shift_up(dconv, 1, tail)
        d2 = _shift_up(dconv, 2, tail)
        dy = dconv * cw[2:3, :] + d1 * cw[1:2, :] + d2 * cw[0:1, :]
        dz_ref[...] = jnp.concatenate([dyc * conv, dy * hh, dy * gc], axis=1).astype(BF16)
        tap = lax.broadcasted_iota(jnp.int32, (8, CV_W), 0)
        dcw = jnp.where(tap == 0, jnp.sum(dconv * y2, axis=0, keepdims=True),
                        jnp.where(tap == 1, jnp.sum(dconv * y1, axis=0, keepdims=True),
                                  jnp.where(tap == 2, jnp.sum(dconv * y, axis=0, keepdims=True), 0.0)))
        _acc(dcw_ref, first, dcw)

    return pl.pallas_call(
        body, name="conv_bwd", grid=(t // tm,),
        in_specs=[pl.BlockSpec((tm, CV_W), lambda i: (i, 0)),
                  pl.BlockSpec((8, CV_W), lambda i: (jnp.minimum((i + 1) * hb, last_blk), 0)),
                  pl.BlockSpec((tm, 768), lambda i: (i, 0)),
                  pl.BlockSpec((8, 768), lambda i: (jnp.maximum(i * hb - 1, 0), 0)),
                  pl.BlockSpec((8, 768), lambda i: (jnp.minimum((i + 1) * hb, last_blk), 0)),
                  _whole()],
        out_specs=[pl.BlockSpec((tm, 768), lambda i: (i, 0)), pl.BlockSpec((8, CV_W), lambda i: (0, 0))],
        out_shape=[jax.ShapeDtypeStruct((t, 768), BF16), jax.ShapeDtypeStruct((8, CV_W), F32)],
        compiler_params=_cp(("arbitrary",)),
    )(dyc, dyc, z, z, z, cw)


def attn_bwd(q, k, v, o, lse, do):
    t = q.shape[0]
    tq = min(TQ, t)
    nq = t // tq

    def body(q_ref, k_ref, v_ref, o_ref, lse_ref, do_ref, dq_ref, dk_ref, dv_ref):
        j = pl.program_id(1)

        @pl.when(j == 0)
        def _():
            dq_ref[...] = jnp.zeros_like(dq_ref)

        row = lax.broadcasted_iota(jnp.int32, (tq, tq), 0)
        col = lax.broadcasted_iota(jnp.int32, (tq, tq), 1)
        vlane = lax.broadcasted_iota(jnp.int32, (tq, 2 * V_DIM), 1)
        vb = v_ref[...]
        dv_acc = jnp.zeros((tq, 2 * V_DIM), F32)
        for h in range(2):
            lanes = slice(h * HEAD_PAD, (h + 1) * HEAD_PAD)
            kb = k_ref[:, lanes]
            in_h = (vlane // V_DIM) == h

            def step(i, carry, masked):
                dk_acc, dv_acc = carry
                start = pl.multiple_of(i * tq, tq)
                qb = q_ref[pl.ds(start, tq), lanes]
                dob = jnp.where(in_h, do_ref[pl.ds(start, tq), :], 0.0)
                delta = jnp.sum(dob * o_ref[pl.ds(start, tq), :], axis=-1, keepdims=True)
                lse = lse_ref[pl.ds(start, tq), lanes][:, 0:1]
                s = _mm_nt(qb, kb) * SCALE
                if masked:
                    s = jnp.where(col <= row, s, NEG)
                p = jnp.exp(s - lse)
                dob16 = dob.astype(BF16)
                dp = _mm_nt(dob16, vb)
                ds = (p * (dp - delta) * SCALE).astype(BF16)
                dv_acc = dv_acc + _mm_tn(p.astype(BF16), dob16)
                dk_acc = dk_acc + _mm_tn(ds, qb)
                dq_ref[pl.ds(start, tq), lanes] += _mm(ds, kb)
                return dk_acc, dv_acc

            carry = step(j, (jnp.zeros((tq, HEAD_PAD), F32), dv_acc), True)
            dk_acc, dv_acc = lax.fori_loop(j + 1, nq, lambda i, c: step(i, c, False), carry)
            dk_ref[:, lanes] = dk_acc
        dv_ref[...] = dv_acc

    return pl.pallas_call(
        body, name="attn_bwd", grid=(HEADS // 2, nq),
        in_specs=[pl.BlockSpec((t, 2 * HEAD_PAD), lambda p, j: (0, p)),
                  pl.BlockSpec((tq, 2 * HEAD_PAD), lambda p, j: (j, p)),
                  pl.BlockSpec((tq, 2 * V_DIM), lambda p, j: (j, p)),
                  pl.BlockSpec((t, 2 * V_DIM), lambda p, j: (0, p)),
                  pl.BlockSpec((t, 2 * HEAD_PAD), lambda p, j: (0, p)),
                  pl.BlockSpec((t, 2 * V_DIM), lambda p, j: (0, p))],
        out_specs=[pl.BlockSpec((t, 2 * HEAD_PAD), lambda p, j: (0, p)),
                   pl.BlockSpec((tq, 2 * HEAD_PAD), lambda p, j: (j, p)),
                   pl.BlockSpec((tq, 2 * V_DIM), lambda p, j: (j, p))],
        out_shape=[jax.ShapeDtypeStruct((t, HEADS * HEAD_PAD), F32), jax.ShapeDtypeStruct((t, HEADS * HEAD_PAD), F32),
                   jax.ShapeDtypeStruct((t, HEADS * V_DIM), F32)],
        compiler_params=_cp(("parallel", "arbitrary")),
    )(q, k, v, o, lse, do)


def mla_proj_bwd(dq, dk, dv, z, ca, sb, sc, gq, gkv, wuq, wukv):
    t = z.shape[0]
    tm = min(TM, t)

    def body(dq_ref, dk_ref, dv_ref, z_ref, ca_ref, sb_ref, sc_ref, gq_ref, gkv_ref, wuq_ref, wukv_ref,
             dz_ref, cq_ref, ckv_ref, dqp_ref, dkvp_ref, dgq_ref, dgkv_ref):
        first = pl.program_id(0) == 0
        z = z_ref[...]
        ca, sb, sc = ca_ref[...], sb_ref[...], sc_ref[...]
        gq, gkv = gq_ref[...], gkv_ref[...]
        cq, cqh, rq = _rms_fwd(z[:, :Q_RANK], gq)
        ckv, ckvh, rkv = _rms_fwd(z[:, Q_RANK:Q_RANK + KV_RANK], gkv)
        lane = lax.broadcasted_iota(jnp.int32, (tm, HEAD_PAD), 1)
        dkr = jnp.zeros((tm, HEAD_PAD), F32)
        for h in range(HEADS):
            lanes = slice(h * HEAD_PAD, (h + 1) * HEAD_PAD)
            dqp_ref[:, lanes] = _rope_t(dq_ref[:, lanes], ca, sb, sc).astype(BF16)
            dkh = dk_ref[:, lanes]
            dkr = dkr + dkh
            dkvp_ref[:, lanes] = jnp.where(lane < NOPE, dkh, 0.0).astype(BF16)
        dkvp_ref[:, HEADS * HEAD_PAD:] = dv_ref[...].astype(BF16)
        dkr = pltpu.roll(_rope_t(jnp.where(lane >= NOPE, dkr, 0.0), ca, sb, sc), HEAD_PAD - NOPE, 1)
        dkr = jnp.where(lane < ROPE, dkr, 0.0)
        dcq = _mm(dqp_ref[...], wuq_ref[...])
        dckv = _mm(dkvp_ref[...], wukv_ref[...])
        dzq, dgq = _rms_bwd(cqh, rq, gq, dcq)
        dzkv, dgkv = _rms_bwd(ckvh, rkv, gkv, dckv)
        dz_ref[...] = jnp.concatenate([dzq, dzkv, dkr], axis=1).astype(BF16)
        cq_ref[...] = cq.astype(BF16)
        ckv_ref[...] = ckv.astype(BF16)
        _acc(dgq_ref, first, dgq)
        _acc(dgkv_ref, first, dgkv)

    row = lambda w: pl.BlockSpec((tm, w), lambda i: (i, 0))
    vec = lambda w: pl.BlockSpec((1, w), lambda i: (0, 0))
    return pl.pallas_call(
        body, name="mla_proj_bwd", grid=(t // tm,),
        in_specs=[row(1024), row(1024), row(512), pl.BlockSpec((tm, 768), lambda i: (i, 1)),
                  row(HEAD_PAD), row(HEAD_PAD), row(HEAD_PAD), _whole(), _whole(), _whole(), _whole()],
        out_specs=[row(768), row(Q_RANK), row(KV_RANK), row(1024), row(1536), vec(Q_RANK), vec(KV_RANK)],
        out_shape=[jax.ShapeDtypeStruct((t, 768), BF16), jax.ShapeDtypeStruct((t, Q_RANK), BF16),
                   jax.ShapeDtypeStruct((t, KV_RANK), BF16), jax.ShapeDtypeStruct((t, 1024), BF16),
                   jax.ShapeDtypeStruct((t, 1536), BF16), jax.ShapeDtypeStruct((1, Q_RANK), F32),
                   jax.ShapeDtypeStruct((1, KV_RANK), F32)],
        compiler_params=_cp(("arbitrary",)),
    )(dq, dk, dv, z, ca, sb, sc, gq, gkv, wuq, wukv)


def pre_in_bwd(x, dx1, dzcv, dzmla, dzsg, g, w):
    t = x.shape[0]
    tm = min(TM, t)

    def body(x_ref, dx1_ref, dzcv_ref, dzmla_ref, dzsg_ref, g_ref, w_ref, dx_ref, h_ref, dz_ref, dg_ref):
        g = g_ref[...]
        h, xh, r = _rms_fwd(x_ref[...], g)
        dz = jnp.concatenate([dzcv_ref[...], dzmla_ref[...], dzsg_ref[...]], axis=1)
        dx, dg = _rms_bwd(xh, r, g, _mm(dz, w_ref[...]))
        dx_ref[...] = dx1_ref[...] + dx
        h_ref[...] = h.astype(BF16)
        dz_ref[...] = dz
        _acc(dg_ref, pl.program_id(0) == 0, dg)

    row = lambda w_: pl.BlockSpec((tm, w_), lambda i: (i, 0))
    return pl.pallas_call(
        body, name="pre_in_bwd", grid=(t // tm,),
        in_specs=[row(D), row(D), row(768), row(768), row(512), _whole(), _whole()],
        out_specs=[row(D), row(D), row(Z_W), pl.BlockSpec((1, D), lambda i: (0, 0))],
        out_shape=[jax.ShapeDtypeStruct((t, D), F32), jax.ShapeDtypeStruct((t, D), BF16),
                   jax.ShapeDtypeStruct((t, Z_W), BF16), jax.ShapeDtypeStruct((1, D), F32)],
        compiler_params=_cp(("arbitrary",)),
    )(x, dx1, dzcv, dzmla, dzsg, g, w)


MESH = pl.DeviceIdType.MESH


def _place():
    return lax.axis_index("x"), lax.axis_index("y"), lax.axis_index("c")


def all_gather(block):
    r, w = block.shape

    def body(x_ref, out_ref, send_sems, recv_sems, local_sem):
        x, y, c = _place()
        me, sibling = (x, y, c), (x, y, 1 - c)
        chips = [(1 - x, y), (x, 1 - y), (1 - x, 1 - y)]

        def slot(px, py, pc):
            return out_ref.at[4 * px + 2 * py + pc]

        def copy(k, blk, to, src=None):
            return pltpu.make_async_remote_copy(
                src_ref=slot(*blk) if src is None else src, dst_ref=slot(*blk),
                send_sem=send_sems.at[k], recv_sem=recv_sems.at[k], device_id=to, device_id_type=MESH)

        mine = pltpu.make_async_copy(x_ref, slot(*me), local_sem)
        mine.start()
        first = [copy(0, me, sibling, src=x_ref)]
        first += [copy(1 + j, me, (*chip, c), src=x_ref) for j, chip in enumerate(chips)]
        for cp in first:
            cp.start()
        passed = [copy(4 + j, (*chip, c), sibling) for j, chip in enumerate(chips)]
        for j, chip in enumerate(chips):
            copy(1 + j, (*chip, c), me).wait_recv()
            passed[j].start()
        copy(0, sibling, me).wait_recv()
        for j, chip in enumerate(chips):
            copy(4 + j, (*chip, 1 - c), me).wait_recv()
        for cp in first + passed:
            cp.wait_send()
        mine.wait()

    return pl.pallas_call(
        body, name="all_gather",
        in_specs=[pl.BlockSpec(memory_space=pl.ANY)],
        out_specs=pl.BlockSpec(memory_space=pl.ANY),
        out_shape=jax.ShapeDtypeStruct((N_DEV, r, w), block.dtype),
        scratch_shapes=[pltpu.SemaphoreType.DMA((7,)), pltpu.SemaphoreType.DMA((7,)), pltpu.SemaphoreType.DMA],
    )(block)


def peer_exchange(src, scatter, name):
    r, w = src.shape[-2:]

    def body(src_ref, out_ref, send_sems, recv_sems, local_sem):
        x, y, c = _place()
        me = 4 * x + 2 * y + c
        own = pltpu.make_async_copy(src_ref.at[me] if scatter else src_ref, out_ref.at[me], local_sem)
        own.start()
        copies = []
        for k in range(1, N_DEV):
            px = 1 - x if k & 4 else x
            py = 1 - y if k & 2 else y
            pc = 1 - c if k & 1 else c
            copies.append(pltpu.make_async_remote_copy(
                src_ref=src_ref.at[4 * px + 2 * py + pc] if scatter else src_ref, dst_ref=out_ref.at[me],
                send_sem=send_sems.at[k - 1], recv_sem=recv_sems.at[k - 1], device_id=(px, py, pc), device_id_type=MESH))
        for cp in copies:
            cp.start()
        for cp in copies:
            cp.wait_recv()
        for cp in copies:
            cp.wait_send()
        own.wait()

    return pl.pallas_call(
        body, name=name,
        in_specs=[pl.BlockSpec(memory_space=pl.ANY)],
        out_specs=pl.BlockSpec(memory_space=pl.ANY),
        out_shape=jax.ShapeDtypeStruct((N_DEV, r, w), src.dtype),
        scratch_shapes=[pltpu.SemaphoreType.DMA((7,)), pltpu.SemaphoreType.DMA((7,)), pltpu.SemaphoreType.DMA],
    )(src)


def _row_tile(r, cap):
    return max(d for d in range(16, cap + 1, 16) if r % d == 0)


def sum_adamw(parts, w, m, v, cap):
    r, c = w.shape
    tr = _row_tile(r, cap)
    c1 = 1.0 / (1.0 - ADAM_B1 ** ADAM_STEP)
    c2 = 1.0 / (1.0 - ADAM_B2 ** ADAM_STEP)

    def body(p_ref, w_ref, m_ref, v_ref, g_ref, d_ref, nm_ref, nv_ref):
        g = p_ref[0].astype(F32)
        for k in range(1, N_DEV):
            g = g + p_ref[k].astype(F32)
        m = ADAM_B1 * m_ref[...] + (1.0 - ADAM_B1) * g
        v = ADAM_B2 * v_ref[...] + (1.0 - ADAM_B2) * (g * g)
        g_ref[...] = g
        nm_ref[...] = m
        nv_ref[...] = v
        d_ref[...] = -ADAM_LR * ((m * c1) / (jnp.sqrt(v * c2) + ADAM_EPS) + ADAM_WD * w_ref[...])

    blk = pl.BlockSpec((tr, c), lambda i: (i, 0))
    out = jax.ShapeDtypeStruct((r, c), F32)
    return pl.pallas_call(
        body, name="sum_adamw", grid=(r // tr,),
        in_specs=[pl.BlockSpec((N_DEV, tr, c), lambda i: (0, i, 0)), blk, blk, blk],
        out_specs=[blk, blk, blk, blk], out_shape=[out, out, out, out],
        compiler_params=_cp(("parallel",)),
    )(parts, w, m, v)


PACK_W = 1024
PIECES = (("w_out", D // N_DEV, D, False), ("w_gate", D_FF // N_DEV, D, True), ("w_up", D_FF // N_DEV, D, True),
          ("w_down", D_FF // N_DEV, D, False), ("w_uq", HEADS * (NOPE + ROPE) // N_DEV, Q_RANK, True),
          ("w_ukv", HEADS * (NOPE + V_DIM) // N_DEV, KV_RANK, True), ("conv", 16, PACK_W, False),
          ("w_in", IN_W // N_DEV, D, True))
PACK_ROWS = 1680
OFFSET = {}
_off = 0
for _name, _rows, _, _ in PIECES:
    OFFSET[_name] = _off
    _off += _rows + -_rows % 16
assert _off == PACK_ROWS and all(o % 16 == 0 for o in OFFSET.values())
CONV_BITS = 3 * (CV_W // N_DEV) * 2


def _to_pack(shards, dtype, conv=None):
    nl = shards["w_in"].shape[0]
    parts = []
    for name, rows, cols, transposed in PIECES:
        if name == "conv":
            if conv is None:
                a = jnp.zeros((nl, rows, PACK_W), dtype)
            else:
                bits = lax.bitcast_convert_type(conv.astype(F32), BF16).reshape(nl, CONV_BITS)
                a = jnp.pad(bits, ((0, 0), (0, rows * PACK_W - CONV_BITS))).reshape(nl, rows, PACK_W)
        else:
            a = shards[name].astype(dtype)
            a = jnp.swapaxes(a, 1, 2) if transposed else a
            a = jnp.pad(a, ((0, 0), (0, -rows % 16), (0, PACK_W - cols)))
        parts.append(a)
    return jnp.concatenate(parts, axis=1)


def _from_pack(pack):
    out = {}
    for name, rows, cols, transposed in PIECES:
        if name != "conv":
            a = pack[:, OFFSET[name]:OFFSET[name] + rows, :cols]
            out[name] = jnp.swapaxes(a, 1, 2) if transposed else a
    return out


def _kernel_weights(g):
    def rows(name):
        _, n, cols, _ = next(p for p in PIECES if p[0] == name)
        return g[:, OFFSET[name]:OFFSET[name] + n, :cols]

    w_in_t = rows("w_in").reshape(IN_W, D)
    w_in_p = jnp.concatenate([w_in_t[1184:], w_in_t[:672], jnp.zeros((96, D), BF16), w_in_t[672:1184]], axis=0)
    w_uq_p = jnp.pad(rows("w_uq"), ((0, 0), (0, HEAD_PAD - NOPE - ROPE), (0, 0))).reshape(HEADS * HEAD_PAD, Q_RANK)
    kv = rows("w_ukv")
    w_k = jnp.pad(kv[:, :NOPE], ((0, 0), (0, HEAD_PAD - NOPE), (0, 0))).reshape(HEADS * HEAD_PAD, KV_RANK)
    w_ukv_p = jnp.concatenate([w_k, kv[:, NOPE:].reshape(HEADS * V_DIM, KV_RANK)], axis=0)
    bits = rows("conv").reshape(N_DEV, -1)[:, :CONV_BITS].reshape(N_DEV, 3, CV_W // N_DEV, 2)
    conv_w = jnp.moveaxis(lax.bitcast_convert_type(bits, F32), 0, 1).reshape(3, CV_W)
    return dict(w_in=w_in_p, w_uq=w_uq_p, w_ukv=w_ukv_p, w_out=rows("w_out").reshape(D, D),
                w_gu=jnp.concatenate([rows("w_gate").reshape(D_FF, D), rows("w_up").reshape(D_FF, D)], axis=0),
                w_down=rows("w_down").reshape(D_FF, D), conv_w=conv_w)


def _grad_chunks(full):
    d_in = full["w_in"]
    d_in = jnp.concatenate([d_in[768:768 + 672], d_in[1536:], d_in[:768]], axis=0)
    d_uq = full["w_uq"].reshape(HEADS, HEAD_PAD, Q_RANK)[:, :NOPE + ROPE]
    d_k = full["w_ukv"][:HEADS * HEAD_PAD].reshape(HEADS, HEAD_PAD, KV_RANK)[:, :NOPE]
    d_v = full["w_ukv"][HEADS * HEAD_PAD:].reshape(HEADS, V_DIM, KV_RANK)
    mats = dict(w_in=d_in, w_uq=d_uq, w_ukv=jnp.concatenate([d_k, d_v], axis=1), w_out=full["w_out"],
                w_gate=full["w_gu"][:D_FF], w_up=full["w_gu"][D_FF:], w_down=full["w_down"])
    parts = []
    for name, rows, cols, _ in PIECES:
        if name == "conv":
            parts.append(jnp.zeros((N_DEV, rows, PACK_W), BF16))
        else:
            parts.append(jnp.pad(mats[name].reshape(N_DEV, rows, cols), ((0, 0), (0, -rows % 16), (0, PACK_W - cols))))
    return jnp.concatenate(parts, axis=1)


SMALL = (("mix_pre_g", (D,)), ("mix_post_g", (D,)), ("ffn_pre_g", (D,)), ("ffn_post_g", (D,)), ("q_norm_g", (Q_RANK,)),
         ("kv_norm_g", (KV_RANK,)), ("sg_ln_g", (SG_W,)), ("sg_ln_b", (SG_W,)), ("w_sp", (4, CHUNK, CHUNK)),
         ("b_sp", (4, CHUNK)), ("out_norm_g", (D,)))
SMALL_ROWS = 576


def _pack_small(vals, nl):
    flat = jnp.concatenate([vals[name].reshape(nl, -1) for name, _ in SMALL] + [vals["conv_w"].reshape(nl, -1)], axis=1)
    return jnp.pad(flat, ((0, 0), (0, SMALL_ROWS * 128 - flat.shape[1]))).reshape(nl * SMALL_ROWS, 128)


def _unpack_small(pack, nl):
    flat = pack.reshape(nl, SMALL_ROWS * 128)
    out, off = {}, 0
    for name, shape in SMALL + (("conv_w", (3, CV_W)),):
        n = int(np.prod(shape))
        out[name] = flat[:, off:off + n].reshape((nl,) + shape)
        off += n
    return out


def _layer_fwd(x, lw, sp, tabs, consts):
    ca, sb, sc = tabs
    z = pre_in_fwd(x, sp["mix_pre_g"], lw["w_in"])
    q, k, v = mla_proj_fwd(z, ca, sb, sc, sp["q_norm_g"], sp["kv_norm_g"], lw["w_uq"], lw["w_ukv"])
    ya, lse = attn_fwd(q, k, v)
    x1 = mix_fwd(x, z, ya, consts["gm"], sp["sg_ln_g"], sp["sg_ln_b"], sp["w_sp"], sp["bias"], lw["conv_w"],
                 sp["out_norm_g"], lw["w_out"], sp["mix_post_g"])
    x2 = ffn_fwd(x1, sp["ffn_pre_g"], lw["w_gu"], lw["w_down"], sp["ffn_post_g"])
    return x2, (x, z, q, k, v, ya, lse, x1)


def _layer_bwd(dx2, saved, lw, sp, tabs, consts):
    ca, sb, sc = tabs
    x, z, q, k, v, ya, lse, x1 = saved
    dx1, h2, dab, s, df, d_ffn_pre, d_ffn_post = ffn_bwd(x1, dx2, sp["ffn_pre_g"], lw["w_gu"], lw["w_down"], sp["ffn_post_g"])
    d_w_gu = atb(dab, h2, 1408)
    d_w_down = atb(s, df, 1408)
    dya, dyc, dzsg, mix, do, d_mix_post, d_out_norm, d_lng, d_lnb, d_wsp, d_bias = mix_bwd(
        dx1, z, ya, consts["gm"], sp["sg_ln_g"], sp["sg_ln_b"], sp["w_sp"], sp["w_sp_t"], sp["bias"], lw["conv_w"],
        sp["out_norm_g"], lw["w_out"], sp["mix_post_g"])
    d_w_out = atb(mix, do, 1024)
    dzcv, d_cw = conv_bwd(dyc, z, lw["conv_w"])
    dq, dk, dv = attn_bwd(q, k, v, ya, lse, dya)
    dzmla, cq, ckv, dqp, dkvp, d_gq, d_gkv = mla_proj_bwd(dq, dk, dv, z, ca, sb, sc, sp["q_norm_g"], sp["kv_norm_g"],
                                                          lw["w_uq"], lw["w_ukv"])
    d_w_uq = atb(dqp, cq, 1024)
    d_w_ukv = atb(dkvp, ckv, 1536)
    dx, h1, dz, d_mix_pre = pre_in_bwd(x, dx1, dzcv, dzmla, dzsg, sp["mix_pre_g"], lw["w_in"])
    d_w_in = atb(dz, h1, 2048)
    big = dict(w_in=d_w_in, w_uq=d_w_uq, w_ukv=d_w_ukv, w_out=d_w_out, w_gu=d_w_gu, w_down=d_w_down)
    d_bsp = d_bias[:, ::GROUP].T
    small = dict(mix_pre_g=d_mix_pre[0], mix_post_g=d_mix_post[0], ffn_pre_g=d_ffn_pre[0], ffn_post_g=d_ffn_post[0],
                 q_norm_g=d_gq[0], kv_norm_g=d_gkv[0], sg_ln_g=d_lng[0], sg_ln_b=d_lnb[0], w_sp=d_wsp, b_sp=d_bsp,
                 out_norm_g=d_out_norm[0], conv_w=d_cw[:3])
    return dx, big, small


def kernel(x, positions, mix_pre_g, mix_post_g, ffn_pre_g, ffn_post_g, w_in, q_norm_g, w_uq, kv_norm_g, w_ukv, sg_ln_g, sg_ln_b, w_sp, b_sp, conv_w, out_norm_g, w_out, w_gate, w_up, w_down, loss_target, m_mix_pre_g, m_mix_post_g, m_ffn_pre_g, m_ffn_post_g, m_w_in, m_q_norm_g, m_w_uq, m_kv_norm_g, m_w_ukv, m_sg_ln_g, m_sg_ln_b, m_w_sp, m_b_sp, m_conv_w, m_out_norm_g, m_w_out, m_w_gate, m_w_up, m_w_down, v_mix_pre_g, v_mix_post_g, v_ffn_pre_g, v_ffn_post_g, v_w_in, v_q_norm_g, v_w_uq, v_kv_norm_g, v_w_ukv, v_sg_ln_g, v_sg_ln_b, v_w_sp, v_b_sp, v_conv_w, v_out_norm_g, v_w_out, v_w_gate, v_w_up, v_w_down):
    nl = w_in.shape[0]
    t = x.shape[1]
    w = dict(mix_pre_g=mix_pre_g, mix_post_g=mix_post_g, ffn_pre_g=ffn_pre_g, ffn_post_g=ffn_post_g, w_in=w_in,
             q_norm_g=q_norm_g, w_uq=w_uq, kv_norm_g=kv_norm_g, w_ukv=w_ukv, sg_ln_g=sg_ln_g, sg_ln_b=sg_ln_b, w_sp=w_sp,
             b_sp=b_sp, conv_w=conv_w, out_norm_g=out_norm_g, w_out=w_out, w_gate=w_gate, w_up=w_up, w_down=w_down)
    m = dict(mix_pre_g=m_mix_pre_g, mix_post_g=m_mix_post_g, ffn_pre_g=m_ffn_pre_g, ffn_post_g=m_ffn_post_g, w_in=m_w_in,
             q_norm_g=m_q_norm_g, w_uq=m_w_uq, kv_norm_g=m_kv_norm_g, w_ukv=m_w_ukv, sg_ln_g=m_sg_ln_g, sg_ln_b=m_sg_ln_b,
             w_sp=m_w_sp, b_sp=m_b_sp, conv_w=m_conv_w, out_norm_g=m_out_norm_g, w_out=m_w_out, w_gate=m_w_gate,
             w_up=m_w_up, w_down=m_w_down)
    v = dict(mix_pre_g=v_mix_pre_g, mix_post_g=v_mix_post_g, ffn_pre_g=v_ffn_pre_g, ffn_post_g=v_ffn_post_g, w_in=v_w_in,
             q_norm_g=v_q_norm_g, w_uq=v_w_uq, kv_norm_g=v_kv_norm_g, w_ukv=v_w_ukv, sg_ln_g=v_sg_ln_g, sg_ln_b=v_sg_ln_b,
             w_sp=v_w_sp, b_sp=v_b_sp, conv_w=v_conv_w, out_norm_g=v_out_norm_g, w_out=v_w_out, w_gate=v_w_gate,
             w_up=v_w_up, w_down=v_w_down)

    w_pack = _to_pack(w, BF16, conv=w["conv_w"])
    layers = [_kernel_weights(all_gather(w_pack[l])) for l in range(nl)]
    consts = dict(gm=jnp.asarray(np.kron(np.eye(SG_W // GROUP), np.full((GROUP, GROUP), 1.0 / GROUP)), F32))
    smalls = []
    for l in range(nl):
        sp = {name: w[name][l].reshape(1, -1) for name, shape in SMALL if len(shape) == 1}
        sp["w_sp"] = w["w_sp"][l]
        sp["w_sp_t"] = jnp.swapaxes(w["w_sp"][l], 1, 2)
        sp["bias"] = jnp.repeat(w["b_sp"][l].T, GROUP, axis=1)
        smalls.append(sp)
    inv_freq = 1.0 / (ROPE_THETA ** (jnp.arange(0, ROPE // 2, dtype=F32) / (ROPE // 2)))
    inv = jnp.zeros((1, HEAD_PAD), F32).at[0, NOPE:NOPE + ROPE].set(jnp.concatenate([inv_freq, inv_freq]))
    tabs = rope_tables(positions.reshape(t, 1).astype(F32), inv)

    h = x[0]
    saved = []
    for l in range(nl):
        h, s = _layer_fwd(h, layers[l], smalls[l], tabs, consts)
        saved.append(s)
    sq, dh = loss_head(h, loss_target[0])
    loss = lax.psum(0.5 * sq[0, 0] / D, ("x", "y", "c"))
    big_grads, small_grads = [None] * nl, [None] * nl
    for l in reversed(range(nl)):
        dh, big_grads[l], small_grads[l] = _layer_bwd(dh, saved[l], layers[l], smalls[l], tabs, consts)

    got_big = [peer_exchange(_grad_chunks(big_grads[l]), True, "grad_exchange") for l in range(nl)]
    send_small = _pack_small({name: jnp.stack([g[name] for g in small_grads]) for name in small_grads[0]}, nl)
    got_small = peer_exchange(send_small, False, "small_exchange")

    me = 4 * lax.axis_index("x") + 2 * lax.axis_index("y") + lax.axis_index("c")
    w_f32, m_f32, v_f32 = _to_pack(w, F32), _to_pack(m, F32), _to_pack(v, F32)
    per_layer = [sum_adamw(got_big[l], w_f32[l], m_f32[l], v_f32[l], 240) for l in range(nl)]
    g_big, d_big, m_big, v_big = [_from_pack(jnp.stack([per_layer[l][i] for l in range(nl)])) for i in range(4)]

    def full_conv(a):
        return lax.dynamic_update_slice(jnp.zeros((nl, 3, CV_W), F32), a, (0, 0, me * (CV_W // N_DEV)))

    def small_pack(d):
        return _pack_small({**{name: d[name] for name, _ in SMALL}, "conv_w": full_conv(d["conv_w"])}, nl)

    g_small, d_small, m_small, v_small = [_unpack_small(p, nl) for p in
                                          sum_adamw(got_small, small_pack(w), small_pack(m), small_pack(v), 1152)]
    outs = []
    for big, small in ((g_big, g_small), (d_big, d_small), (m_big, m_small), (v_big, v_small)):
        for name in w:
            if name == "conv_w":
                outs.append(lax.dynamic_slice(small[name], (0, 0, me * (CV_W // N_DEV)), (nl, 3, CV_W // N_DEV)))
            elif name in small:
                outs.append(small[name])
            else:
                outs.append(big[name])
    return (loss, dh[None], *outs)
```

```python
import functools

import jax
import jax.numpy as jnp
import numpy as np
from jax import lax
from jax.experimental import pallas as pl
from jax.experimental.pallas import tpu as pltpu

F32 = jnp.float32
BF16 = jnp.bfloat16

D = 1024
Q_RANK = 384
KV_RANK = 256
ROPE = 32
HEADS = 8
NOPE = 64
V_DIM = 64
HEAD_PAD = 128
SG_W = 256
CV_W = 256
CHUNK = 128
GROUP = 64
D_FF = 2816
IN_W = 1952
Z_W = 2048
Z_CV, Z_MLA, Z_SG = 0, 768, 1536
EPS = 1e-6
ROPE_THETA = 10000.0
SCALE = (NOPE + ROPE) ** -0.5
NEG = -1e30
N_DEV = 8

ADAM_LR, ADAM_B1, ADAM_B2, ADAM_EPS, ADAM_WD, ADAM_STEP = 0.001, 0.9, 0.999, 1e-08, 0.01, 10

VMEM_LIMIT = 56 * 1024 * 1024

TM = 512
TM_FFN = 256
FFN_SLAB = 256
TQ = 512
TT = 512


def _cp(sem, vmem=VMEM_LIMIT):
    return pltpu.CompilerParams(dimension_semantics=sem, vmem_limit_bytes=vmem)


def _whole():
    return pl.BlockSpec(memory_space=pltpu.VMEM)


def _mm(a, b):
    return jnp.dot(a, b, preferred_element_type=F32)


def _mm_nt(a, b):
    return lax.dot_general(a, b, (((1,), (1,)), ((), ())), preferred_element_type=F32)


def _mm_tn(a, b):
    return lax.dot_general(a, b, (((0,), (0,)), ((), ())), preferred_element_type=F32)


def _rms_fwd(x, g):
    r = lax.rsqrt(jnp.mean(x * x, axis=-1, keepdims=True) + EPS)
    xh = x * r
    return xh * g, xh, r


def _rms_bwd(xh, r, g, dy):
    dxh = dy * g
    dx = r * (dxh - xh * jnp.mean(dxh * xh, axis=-1, keepdims=True))
    dg = jnp.sum(dy * xh, axis=0, keepdims=True)
    return dx, dg


def _gmean(v, gm):
    return jnp.dot(v, gm, precision=lax.Precision.HIGHEST, preferred_element_type=F32)


def _gelu(x):
    c = np.float32(np.sqrt(2.0 / np.pi))
    u = c * (x + 0.044715 * x * x * x)
    t = jnp.tanh(u)
    return 0.5 * x * (1.0 + t), t


def _gelu_grad(x, t):
    c = np.float32(np.sqrt(2.0 / np.pi))
    return 0.5 * (1.0 + t) + 0.5 * x * (1.0 - t * t) * c * (1.0 + 3.0 * 0.044715 * x * x)


def _rope(t, ca, sb, sc):
    return t * ca + pltpu.roll(t, HEAD_PAD - 16, 1) * sb + pltpu.roll(t, 16, 1) * sc


def _rope_t(dt, ca, sb, sc):
    return dt * ca + pltpu.roll(dt * sb, 16, 1) + pltpu.roll(dt * sc, HEAD_PAD - 16, 1)


def _shift_down(y, k, head):
    n = y.shape[0]
    out = pltpu.roll(y, k, 0)
    row = lax.broadcasted_iota(jnp.int32, y.shape, 0)
    for j in range(k):
        out = jnp.where(row == j, head[8 - k + j:8 - k + j + 1, :], out)
    return out


def _shift_up(y, k, tail):
    n = y.shape[0]
    out = pltpu.roll(y, n - k, 0)
    row = lax.broadcasted_iota(jnp.int32, y.shape, 0)
    for j in range(k):
        out = jnp.where(row == n - k + j, tail[j:j + 1, :], out)
    return out


def rope_tables(pos, inv):
    t = pos.shape[0]
    tm = min(TM, t)

    def body(pos_ref, inv_ref, ca_ref, sb_ref, sc_ref):
        ang = pos_ref[...] * inv_ref[...]
        c = jnp.cos(ang)
        s = jnp.sin(ang)
        lane = lax.broadcasted_iota(jnp.int32, ang.shape, 1)
        ca_ref[...] = jnp.where(lane < NOPE, 1.0, jnp.where(lane < NOPE + ROPE, c, 0.0))
        sb_ref[...] = jnp.where((lane >= NOPE) & (lane < NOPE + 16), -s, 0.0)
        sc_ref[...] = jnp.where((lane >= NOPE + 16) & (lane < NOPE + ROPE), s, 0.0)

    out = jax.ShapeDtypeStruct((t, HEAD_PAD), F32)
    blk = pl.BlockSpec((tm, HEAD_PAD), lambda i: (i, 0))
    return pl.pallas_call(
        body, name="rope_tables", grid=(t // tm,),
        in_specs=[pl.BlockSpec((tm, 1), lambda i: (i, 0)), pl.BlockSpec((1, HEAD_PAD), lambda i: (0, 0))],
        out_specs=[blk, blk, blk], out_shape=[out, out, out],
        compiler_params=_cp(("parallel",)),
    )(pos, inv)


def pre_in_fwd(x, g, w):
    t = x.shape[0]
    tm = min(TM, t)

    def body(x_ref, g_ref, w_ref, z_ref):
        h, _, _ = _rms_fwd(x_ref[...], g_ref[...])
        z_ref[...] = _mm_nt(h.astype(BF16), w_ref[...])

    return pl.pallas_call(
        body, name="pre_in_fwd", grid=(t // tm,),
        in_specs=[pl.BlockSpec((tm, D), lambda i: (i, 0)), _whole(), _whole()],
        out_specs=pl.BlockSpec((tm, Z_W), lambda i: (i, 0)),
        out_shape=jax.ShapeDtypeStruct((t, Z_W), F32),
        compiler_params=_cp(("parallel",)),
    )(x, g, w)


def mla_proj_fwd(z, ca, sb, sc, gq, gkv, wuq, wukv):
    t = z.shape[0]
    tm = min(TM, t)

    def body(z_ref, ca_ref, sb_ref, sc_ref, gq_ref, gkv_ref, wuq_ref, wukv_ref, q_ref, k_ref, v_ref):
        z = z_ref[...]
        ca, sb, sc = ca_ref[...], sb_ref[...], sc_ref[...]
        cq, _, _ = _rms_fwd(z[:, :Q_RANK], gq_ref[...])
        ckv, _, _ = _rms_fwd(z[:, Q_RANK:Q_RANK + KV_RANK], gkv_ref[...])
        q = _mm_nt(cq.astype(BF16), wuq_ref[...])
        kv = _mm_nt(ckv.astype(BF16), wukv_ref[...])
        kr = _rope(pltpu.roll(z[:, Q_RANK + KV_RANK:], NOPE, 1), ca, sb, sc)
        for h in range(HEADS):
            lanes = slice(h * HEAD_PAD, (h + 1) * HEAD_PAD)
            q_ref[:, lanes] = _rope(q[:, lanes], ca, sb, sc).astype(BF16)
            k_ref[:, lanes] = (kv[:, lanes] + kr).astype(BF16)
        v_ref[...] = kv[:, HEADS * HEAD_PAD:].astype(BF16)

    tab = pl.BlockSpec((tm, HEAD_PAD), lambda i: (i, 0))
    return pl.pallas_call(
        body, name="mla_proj_fwd", grid=(t // tm,),
        in_specs=[pl.BlockSpec((tm, 768), lambda i: (i, 1)), tab, tab, tab, _whole(), _whole(), _whole(), _whole()],
        out_specs=[pl.BlockSpec((tm, HEADS * HEAD_PAD), lambda i: (i, 0)),
                   pl.BlockSpec((tm, HEADS * HEAD_PAD), lambda i: (i, 0)),
                   pl.BlockSpec((tm, HEADS * V_DIM), lambda i: (i, 0))],
        out_shape=[jax.ShapeDtypeStruct((t, HEADS * HEAD_PAD), BF16),
                   jax.ShapeDtypeStruct((t, HEADS * HEAD_PAD), BF16),
                   jax.ShapeDtypeStruct((t, HEADS * V_DIM), BF16)],
        compiler_params=_cp(("parallel",)),
    )(z, ca, sb, sc, gq, gkv, wuq, wukv)


def attn_fwd(q, k, v, carried=None):
    t = q.shape[0]
    tq = min(TQ, t)
    nq = t // tq
    last_pair = HEADS // 2 - 1

    def body(*refs):
        if carried is None:
            q_ref, k_ref, v_ref, o_ref, lse_ref = refs
        else:
            q_ref, k_ref, v_ref, blk_ref, o_ref, lse_ref, gat_ref = refs[:7]
            start, forward, finish = _gather_steps(blk_ref, gat_ref, *refs[7:])
            pair = pl.program_id(0)
            pl.when((pair == 0) & (pl.program_id(1) == 0))(start)
            pl.when((pair == last_pair) & (pl.program_id(1) == 0))(forward)
        i = pl.program_id(1)
        row = lax.broadcasted_iota(jnp.int32, (tq, tq), 0)
        col = lax.broadcasted_iota(jnp.int32, (tq, tq), 1)
        outs = []
        for h in range(2):
            lanes = slice(h * HEAD_PAD, (h + 1) * HEAD_PAD)
            qh = q_ref[:, lanes]

            def step(j, carry, masked):
                m, l, acc = carry
                start = pl.multiple_of(j * tq, tq)
                kb = k_ref[pl.ds(start, tq), lanes]
                vb = v_ref[pl.ds(start, tq), :]
                s = _mm_nt(qh, kb) * SCALE
                if masked:
                    s = jnp.where(col <= row, s, NEG)
                m_new = jnp.maximum(m, jnp.max(s, axis=-1, keepdims=True))
                p = jnp.exp(s - m_new)
                alpha = jnp.exp(m - m_new)
                l = alpha * l + jnp.sum(p, axis=-1, keepdims=True)
                acc = alpha * acc + _mm(p.astype(BF16), vb)
                return m_new, l, acc

            init = (jnp.full((tq, 1), NEG, F32), jnp.zeros((tq, 1), F32), jnp.zeros((tq, 2 * V_DIM), F32))
            carry = lax.fori_loop(0, i, lambda j, c: step(j, c, False), init)
            m, l, acc = step(i, carry, True)
            outs.append(acc / l)
            lse_ref[:, lanes] = jnp.broadcast_to(m + jnp.log(l), (tq, HEAD_PAD))
        lane = lax.broadcasted_iota(jnp.int32, (tq, 2 * V_DIM), 1)
        o_ref[...] = jnp.where(lane < V_DIM, outs[0], outs[1])
        if carried is not None:
            pl.when((pl.program_id(0) == last_pair) & (i == nq - 1))(finish)

    in_specs = [pl.BlockSpec((tq, 2 * HEAD_PAD), lambda p, i: (i, p)),
                pl.BlockSpec((t, 2 * HEAD_PAD), lambda p, i: (0, p)),
                pl.BlockSpec((t, 2 * V_DIM), lambda p, i: (0, p))]
    out_specs = [pl.BlockSpec((tq, 2 * V_DIM), lambda p, i: (i, p)),
                 pl.BlockSpec((tq, 2 * HEAD_PAD), lambda p, i: (i, p))]
    out_shape = [jax.ShapeDtypeStruct((t, HEADS * V_DIM), F32), jax.ShapeDtypeStruct((t, HEADS * HEAD_PAD), F32)]
    if carried is None:
        return pl.pallas_call(
            body, name="attn_fwd", grid=(HEADS // 2, nq), in_specs=in_specs, out_specs=out_specs, out_shape=out_shape,
            compiler_params=_cp(("parallel", "parallel")),
        )(q, k, v)
    return pl.pallas_call(
        body, name="attn_fwd_gather", grid=(HEADS // 2, nq),
        in_specs=in_specs + [pl.BlockSpec(memory_space=pl.ANY)],
        out_specs=out_specs + [pl.BlockSpec(memory_space=pl.ANY)],
        out_shape=out_shape + [jax.ShapeDtypeStruct((N_DEV,) + carried.shape, carried.dtype)],
        scratch_shapes=_comm_sems(),
        compiler_params=_cp(("arbitrary", "arbitrary")),
    )(q, k, v, carried)


def _sgu_fwd(zsg, gm, lng, lnb, wc_ref, bias, mixed_ref):
    uv, th = _gelu(zsg)
    u, v0 = uv[:, :SG_W], uv[:, SG_W:]
    vc = v0 - _gmean(v0, gm)
    r = lax.rsqrt(_gmean(vc * vc, gm) + EPS)
    vh = vc * r
    v = vh * lng + lnb
    lane = lax.broadcasted_iota(jnp.int32, (CHUNK, SG_W), 1)
    for c in range(zsg.shape[0] // CHUNK):
        rows = slice(c * CHUNK, (c + 1) * CHUNK)
        vb = v[rows].astype(BF16)
        mixed = bias
        for g in range(SG_W // GROUP):
            mixed = mixed + jnp.where(lane // GROUP == g, _mm(wc_ref[g], vb), 0.0)
        mixed_ref[rows, :] = mixed
    return u, v, vh, r, th


def _conv_fwd(zcv, halo, first, cw):
    gb, gc, hh = zcv[:, :CV_W], zcv[:, CV_W:2 * CV_W], zcv[:, 2 * CV_W:]
    y = gc * hh
    yh = jnp.where(first, 0.0, halo[:, CV_W:2 * CV_W] * halo[:, 2 * CV_W:])
    y1 = _shift_down(y, 1, yh)
    y2 = _shift_down(y, 2, yh)
    conv = y2 * cw[0:1, :] + y1 * cw[1:2, :] + y * cw[2:3, :]
    return gb * conv, conv, y, y1, y2


def _tril_bf16(w_ref, g):
    row = lax.broadcasted_iota(jnp.int32, (CHUNK, CHUNK), 0)
    col = lax.broadcasted_iota(jnp.int32, (CHUNK, CHUNK), 1)
    return jnp.where(col <= row, w_ref[g], 0.0).astype(BF16)


def mix_fwd(x, z, ya, gm, lng, lnb, wsp, bias, cw, gout, wout, gpost):
    t = x.shape[0]
    tm = min(TM, t)

    def body(x_ref, zcv_ref, halo_ref, zsg_ref, ya_ref, gm_ref, lng_ref, lnb_ref, wsp_ref, bias_ref, cw_ref,
             gout_ref, wout_ref, gpost_ref, x1_ref, wc_ref, mixed_ref):
        i = pl.program_id(0)
        for g in range(SG_W // GROUP):
            wc_ref[g] = _tril_bf16(wsp_ref, g)
        u, _, _, _, _ = _sgu_fwd(zsg_ref[...], gm_ref[...], lng_ref[...], lnb_ref[...], wc_ref, bias_ref[...], mixed_ref)
        yb = u * mixed_ref[...]
        yc, _, _, _, _ = _conv_fwd(zcv_ref[...], halo_ref[...], i == 0, cw_ref[...])
        gout = gout_ref[...]
        na, _, _ = _rms_fwd(ya_ref[...], gout[:, :512])
        nb, _, _ = _rms_fwd(yb, gout[:, 512:768])
        nc, _, _ = _rms_fwd(yc, gout[:, 768:])
        mix = jnp.concatenate([na, nb, nc], axis=1).astype(BF16)
        o, _, _ = _rms_fwd(_mm(mix, wout_ref[...]), gpost_ref[...])
        x1_ref[...] = x_ref[...] + o

    hb = tm // 8
    return pl.pallas_call(
        body, name="mix_fwd", grid=(t // tm,),
        in_specs=[pl.BlockSpec((tm, D), lambda i: (i, 0)),
                  pl.BlockSpec((tm, 768), lambda i: (i, 0)),
                  pl.BlockSpec((8, 768), lambda i: (jnp.maximum(i * hb - 1, 0), 0)),
                  pl.BlockSpec((tm, 512), lambda i: (i, 3)),
                  pl.BlockSpec((tm, 512), lambda i: (i, 0)),
                  _whole(), _whole(), _whole(), _whole(), _whole(), _whole(), _whole(), _whole(), _whole()],
        out_specs=pl.BlockSpec((tm, D), lambda i: (i, 0)),
        out_shape=jax.ShapeDtypeStruct((t, D), F32),
        scratch_shapes=[pltpu.VMEM((SG_W // GROUP, CHUNK, CHUNK), BF16), pltpu.VMEM((tm, SG_W), F32)],
        compiler_params=_cp(("arbitrary",)),
    )(x, z, z, z, ya, gm, lng, lnb, wsp, bias, cw, gout, wout, gpost)


def _sigmoid(a):
    return 1.0 / (1.0 + jnp.exp(-a))


def ffn_fwd(x1, gpre, wgu, wd, gpost):
    t = x1.shape[0]
    tm = min(TM_FFN, t)

    def body(x_ref, gpre_ref, wgu_ref, wd_ref, gpost_ref, x2_ref):
        x = x_ref[...]
        h, _, _ = _rms_fwd(x, gpre_ref[...])
        ab = _mm_nt(h.astype(BF16), wgu_ref[...])
        a, b = ab[:, :D_FF], ab[:, D_FF:]
        s = a * _sigmoid(a) * b
        f, _, _ = _rms_fwd(_mm(s.astype(BF16), wd_ref[...]), gpost_ref[...])
        x2_ref[...] = x + f

    return pl.pallas_call(
        body, name="ffn_fwd", grid=(t // tm,),
        in_specs=[pl.BlockSpec((tm, D), lambda i: (i, 0)), _whole(), _whole(), _whole(), _whole()],
        out_specs=pl.BlockSpec((tm, D), lambda i: (i, 0)),
        out_shape=jax.ShapeDtypeStruct((t, D), F32),
        compiler_params=_cp(("parallel",)),
    )(x1, gpre, wgu, wd, gpost)


def loss_head(y, target):
    t = y.shape[0]
    tm = min(TM, t)

    def body(y_ref, t_ref, loss_ref, dy_ref):
        @pl.when(pl.program_id(0) == 0)
        def _():
            loss_ref[...] = jnp.zeros_like(loss_ref)

        e = y_ref[...] - t_ref[...]
        dy_ref[...] = e * (1.0 / D)
        loss_ref[...] += jnp.sum(jnp.sum(e * e, axis=-1, keepdims=True), axis=0, keepdims=True)

    return pl.pallas_call(
        body, name="loss_head", grid=(t // tm,),
        in_specs=[pl.BlockSpec((tm, D), lambda i: (i, 0)), pl.BlockSpec((tm, D), lambda i: (i, 0))],
        out_specs=[pl.BlockSpec((1, 128), lambda i: (0, 0)), pl.BlockSpec((tm, D), lambda i: (i, 0))],
        out_shape=[jax.ShapeDtypeStruct((1, 128), F32), jax.ShapeDtypeStruct((t, D), F32)],
        compiler_params=_cp(("arbitrary",)),
    )(y, target)


def _acc(ref, first, val):
    @pl.when(first)
    def _():
        ref[...] = val

    @pl.when(jnp.logical_not(first))
    def _():
        ref[...] += val


def ffn_bwd(x1, dx2, gpre, wgu, wd, gpost):
    t = x1.shape[0]
    tm = min(TM_FFN, t)

    def body(x_ref, dx2_ref, gpre_ref, wgu_ref, wd_ref, gpost_ref,
             dx1_ref, h_ref, dab_ref, s_ref, df_ref, dgpre_ref, dgpost_ref, ab_ref, ds_ref):
        first = pl.program_id(0) == 0
        dx2 = dx2_ref[...]
        gpre, gpost = gpre_ref[...], gpost_ref[...]
        h, xh, rx = _rms_fwd(x_ref[...], gpre)
        h_ref[...] = h.astype(BF16)
        ab_ref[...] = _mm_nt(h_ref[...], wgu_ref[...])
        for c in range(0, D_FF, FFN_SLAB):
            a, b = ab_ref[:, c:c + FFN_SLAB], ab_ref[:, D_FF + c:D_FF + c + FFN_SLAB]
            s_ref[:, c:c + FFN_SLAB] = (a * _sigmoid(a) * b).astype(BF16)
        _, fh, rf = _rms_fwd(_mm(s_ref[...], wd_ref[...]), gpost)
        df, dgpost = _rms_bwd(fh, rf, gpost, dx2)
        df_ref[...] = df.astype(BF16)
        ds_ref[...] = _mm_nt(df_ref[...], wd_ref[...])
        for c in range(0, D_FF, FFN_SLAB):
            a, b = ab_ref[:, c:c + FFN_SLAB], ab_ref[:, D_FF + c:D_FF + c + FFN_SLAB]
            ds = ds_ref[:, c:c + FFN_SLAB]
            sg = _sigmoid(a)
            dab_ref[:, c:c + FFN_SLAB] = (ds * b * (sg * (1.0 + a * (1.0 - sg)))).astype(BF16)
            dab_ref[:, D_FF + c:D_FF + c + FFN_SLAB] = (ds * (a * sg)).astype(BF16)
        dx, dgpre = _rms_bwd(xh, rx, gpre, _mm(dab_ref[...], wgu_ref[...]))
        dx1_ref[...] = dx2 + dx
        _acc(dgpre_ref, first, dgpre)
        _acc(dgpost_ref, first, dgpost)

    row = lambda w: pl.BlockSpec((tm, w), lambda i: (i, 0))
    vec = pl.BlockSpec((1, D), lambda i: (0, 0))
    return pl.pallas_call(
        body, name="ffn_bwd", grid=(t // tm,),
        in_specs=[row(D), row(D), _whole(), _whole(), _whole(), _whole()],
        out_specs=[row(D), row(D), row(2 * D_FF), row(D_FF), row(D), vec, vec],
        out_shape=[jax.ShapeDtypeStruct((t, D), F32), jax.ShapeDtypeStruct((t, D), BF16),
                   jax.ShapeDtypeStruct((t, 2 * D_FF), BF16), jax.ShapeDtypeStruct((t, D_FF), BF16),
                   jax.ShapeDtypeStruct((t, D), BF16), jax.ShapeDtypeStruct((1, D), F32),
                   jax.ShapeDtypeStruct((1, D), F32)],
        scratch_shapes=[pltpu.VMEM((tm, 2 * D_FF), F32), pltpu.VMEM((tm, D_FF), F32)],
        compiler_params=_cp(("arbitrary",)),
    )(x1, dx2, gpre, wgu, wd, gpost)


def atb(a, b, tk):
    t, k = a.shape
    n = b.shape[1]
    tt = min(TT, t)
    tk = min(tk, k)
    steps = t // tt

    def body(a_ref, b_ref, o_ref, acc_ref):
        i = pl.program_id(1)
        _acc(acc_ref, i == 0, _mm_tn(a_ref[...], b_ref[...]))

        @pl.when(i == steps - 1)
        def _():
            o_ref[...] = acc_ref[...].astype(BF16)

    return pl.pallas_call(
        body, name="atb", grid=(k // tk, steps),
        in_specs=[pl.BlockSpec((tt, tk), lambda j, i: (i, j)), pl.BlockSpec((tt, n), lambda j, i: (i, 0))],
        out_specs=pl.BlockSpec((tk, n), lambda j, i: (j, 0)),
        out_shape=jax.ShapeDtypeStruct((k, n), BF16),
        scratch_shapes=[pltpu.VMEM((tk, n), F32)],
        compiler_params=_cp(("parallel", "arbitrary")),
    )(a, b)


def mix_bwd(dx1, z, ya, gm, lng, lnb, wsp, wspt, bias, cw, gout, wout, gpost):
    t = dx1.shape[0]
    tm = min(TM, t)
    ng = SG_W // GROUP

    def body(dx1_ref, zcv_ref, halo_ref, zsg_ref, ya_ref, gm_ref, lng_ref, lnb_ref, wsp_ref, wspt_ref, bias_ref,
             cw_ref, gout_ref, wout_ref, gpost_ref,
             dya_ref, dyc_ref, dzsg_ref, mix_ref, do_ref, dgpost_ref, dgout_ref, dlng_ref, dlnb_ref, dwsp_ref,
             dbias_ref, wc_ref, wct_ref, mixed_ref, dv_ref):
        i = pl.program_id(0)
        first = i == 0
        gm = gm_ref[...]
        for g in range(ng):
            wc_ref[g] = _tril_bf16(wsp_ref, g)
            wct_ref[g] = jnp.where(
                lax.broadcasted_iota(jnp.int32, (CHUNK, CHUNK), 0) <= lax.broadcasted_iota(jnp.int32, (CHUNK, CHUNK), 1),
                wspt_ref[g], 0.0).astype(BF16)
        zsg = zsg_ref[...]
        lng = lng_ref[...]
        u, v, vh, r, th = _sgu_fwd(zsg, gm, lng, lnb_ref[...], wc_ref, bias_ref[...], mixed_ref)
        mixed = mixed_ref[...]
        yb = u * mixed
        yc, _, _, _, _ = _conv_fwd(zcv_ref[...], halo_ref[...], first, cw_ref[...])
        gout, gpost = gout_ref[...], gpost_ref[...]
        ga, gb_, gc_ = gout[:, :512], gout[:, 512:768], gout[:, 768:]
        na, yah, ra = _rms_fwd(ya_ref[...], ga)
        nb, ybh, rb = _rms_fwd(yb, gb_)
        nc, ych, rc = _rms_fwd(yc, gc_)
        mix = jnp.concatenate([na, nb, nc], axis=1).astype(BF16)
        _, oh, ro = _rms_fwd(_mm(mix, wout_ref[...]), gpost)
        do, dgpost = _rms_bwd(oh, ro, gpost, dx1_ref[...])
        dob = do.astype(BF16)
        dmix = _mm_nt(dob, wout_ref[...])
        dya, dga = _rms_bwd(yah, ra, ga, dmix[:, :512])
        dyb, dgb = _rms_bwd(ybh, rb, gb_, dmix[:, 512:768])
        dyc, dgc = _rms_bwd(ych, rc, gc_, dmix[:, 768:])
        dya_ref[...] = dya
        dyc_ref[...] = dyc
        mix_ref[...] = mix
        do_ref[...] = dob
        _acc(dgpost_ref, first, dgpost)
        _acc(dgout_ref, first, jnp.concatenate([dga, dgb, dgc], axis=1))
        du = dyb * mixed
        dmixed = dyb * u
        lane = lax.broadcasted_iota(jnp.int32, (CHUNK, SG_W), 1)
        row = lax.broadcasted_iota(jnp.int32, (CHUNK, CHUNK), 0)
        col = lax.broadcasted_iota(jnp.int32, (CHUNK, CHUNK), 1)
        dbias = jnp.zeros((CHUNK, SG_W), F32)
        dw = [jnp.zeros((CHUNK, CHUNK), F32) for _ in range(ng)]
        for c in range(tm // CHUNK):
            rows = slice(c * CHUNK, (c + 1) * CHUNK)
            dm = dmixed[rows]
            dbias = dbias + dm
            dmb = dm.astype(BF16)
            vb = v[rows].astype(BF16)
            dvc = jnp.zeros((CHUNK, SG_W), F32)
            for g in range(ng):
                in_g = lane // GROUP == g
                dvc = dvc + jnp.where(in_g, _mm(wct_ref[g], dmb), 0.0)
                dw[g] = dw[g] + _mm_nt(jnp.where(in_g, dmb, jnp.zeros_like(dmb)), vb)
            dv_ref[rows, :] = dvc
        for g in range(ng):
            dwg = jnp.where(col <= row, dw[g], 0.0)

            @pl.when(first)
            def _():
                dwsp_ref[g] = dwg

            @pl.when(jnp.logical_not(first))
            def _():
                dwsp_ref[g] += dwg
        _acc(dbias_ref, first, _gmean(dbias, gm) * GROUP)
        dv = dv_ref[...]
        _acc(dlng_ref, first, jnp.sum(dv * vh, axis=0, keepdims=True))
        _acc(dlnb_ref, first, jnp.sum(dv, axis=0, keepdims=True))
        dvh = dv * lng
        dv0 = r * (dvh - _gmean(dvh, gm) - vh * _gmean(dvh * vh, gm))
        dzsg_ref[...] = (jnp.concatenate([du, dv0], axis=1) * _gelu_grad(zsg, th)).astype(BF16)

    hb = tm // 8
    row_ = lambda w: pl.BlockSpec((tm, w), lambda i: (i, 0))
    vec = lambda w: pl.BlockSpec((1, w), lambda i: (0, 0))
    return pl.pallas_call(
        body, name="mix_bwd", grid=(t // tm,),
        in_specs=[row_(D),
                  pl.BlockSpec((tm, 768), lambda i: (i, 0)),
                  pl.BlockSpec((8, 768), lambda i: (jnp.maximum(i * hb - 1, 0), 0)),
                  pl.BlockSpec((tm, 512), lambda i: (i, 3)),
                  row_(512),
                  _whole(), _whole(), _whole(), _whole(), _whole(), _whole(), _whole(), _whole(), _whole(), _whole()],
        out_specs=[row_(512), row_(CV_W), row_(512), row_(D), row_(D), vec(D), vec(D), vec(SG_W), vec(SG_W),
                   pl.BlockSpec((ng, CHUNK, CHUNK), lambda i: (0, 0, 0)),
                   pl.BlockSpec((CHUNK, SG_W), lambda i: (0, 0))],
        out_shape=[jax.ShapeDtypeStruct((t, 512), F32), jax.ShapeDtypeStruct((t, CV_W), F32),
                   jax.ShapeDtypeStruct((t, 512), BF16), jax.ShapeDtypeStruct((t, D), BF16),
                   jax.ShapeDtypeStruct((t, D), BF16), jax.ShapeDtypeStruct((1, D), F32),
                   jax.ShapeDtypeStruct((1, D), F32), jax.ShapeDtypeStruct((1, SG_W), F32),
                   jax.ShapeDtypeStruct((1, SG_W), F32), jax.ShapeDtypeStruct((ng, CHUNK, CHUNK), F32),
                   jax.ShapeDtypeStruct((CHUNK, SG_W), F32)],
        scratch_shapes=[pltpu.VMEM((ng, CHUNK, CHUNK), BF16), pltpu.VMEM((ng, CHUNK, CHUNK), BF16),
                        pltpu.VMEM((tm, SG_W), F32), pltpu.VMEM((tm, SG_W), F32)],
        compiler_params=_cp(("arbitrary",)),
    )(dx1, z, z, z, ya, gm, lng, lnb, wsp, wspt, bias, cw, gout, wout, gpost)


def conv_bwd(dyc, z, cw):
    t = dyc.shape[0]
    tm = min(TM, t)
    hb = tm // 8
    last_blk = t // 8 - 1

    def body(dyc_ref, dyct_ref, zcv_ref, head_ref, tail_ref, cw_ref, dz_ref, dcw_ref):
        i = pl.program_id(0)
        first = i == 0
        last = i == pl.num_programs(0) - 1
        cw = cw_ref[...]
        zcv = zcv_ref[...]
        gb, gc, hh = zcv[:, :CV_W], zcv[:, CV_W:2 * CV_W], zcv[:, 2 * CV_W:]
        _, conv, y, y1, y2 = _conv_fwd(zcv, head_ref[...], first, cw)
        dyc = dyc_ref[...]
        dconv = dyc * gb
        tail = jnp.where(last, 0.0, dyct_ref[...] * tail_ref[:, :CV_W])
        d1 = _shift_up(dconv, 1, tail)
        d2 = _shift_up(dconv, 2, tail)
        dy = dconv * cw[2:3, :] + d1 * cw[1:2, :] + d2 * cw[0:1, :]
        dz_ref[...] = jnp.concatenate([dyc * conv, dy * hh, dy * gc], axis=1).astype(BF16)
        tap = lax.broadcasted_iota(jnp.int32, (8, CV_W), 0)
        dcw = jnp.where(tap == 0, jnp.sum(dconv * y2, axis=0, keepdims=True),
                        jnp.where(tap == 1, jnp.sum(dconv * y1, axis=0, keepdims=True),
                                  jnp.where(tap == 2, jnp.sum(dconv * y, axis=0, keepdims=True), 0.0)))
        _acc(dcw_ref, first, dcw)

    return pl.pallas_call(
        body, name="conv_bwd", grid=(t // tm,),
        in_specs=[pl.BlockSpec((tm, CV_W), lambda i: (i, 0)),
                  pl.BlockSpec((8, CV_W), lambda i: (jnp.minimum((i + 1) * hb, last_blk), 0)),
                  pl.BlockSpec((tm, 768), lambda i: (i, 0)),
                  pl.BlockSpec((8, 768), lambda i: (jnp.maximum(i * hb - 1, 0), 0)),
                  pl.BlockSpec((8, 768), lambda i: (jnp.minimum((i + 1) * hb, last_blk), 0)),
                  _whole()],
        out_specs=[pl.BlockSpec((tm, 768), lambda i: (i, 0)), pl.BlockSpec((8, CV_W), lambda i: (0, 0))],
        out_shape=[jax.ShapeDtypeStruct((t, 768), BF16), jax.ShapeDtypeStruct((8, CV_W), F32)],
        compiler_params=_cp(("arbitrary",)),
    )(dyc, dyc, z, z, z, cw)


def attn_bwd(q, k, v, o, lse, do, carried=None):
    t = q.shape[0]
    tq = min(TQ, t)
    nq = t // tq
    last_pair = HEADS // 2 - 1

    def body(*refs):
        j = pl.program_id(1)
        if carried is None:
            q_ref, k_ref, v_ref, o_ref, lse_ref, do_ref, dq_ref, dk_ref, dv_ref = refs
        else:
            q_ref, k_ref, v_ref, o_ref, lse_ref, do_ref, chunks_ref, small_ref = refs[:8]
            dq_ref, dk_ref, dv_ref, got_ref, gots_ref = refs[8:13]
            start_c, finish_c = _exchange_steps(chunks_ref, got_ref, True, *refs[13:16])
            start_s, finish_s = _exchange_steps(small_ref, gots_ref, False, *refs[16:19])

            @pl.when((pl.program_id(0) == 0) & (j == 0))
            def _():
                start_c()
                start_s()

        @pl.when(j == 0)
        def _():
            dq_ref[...] = jnp.zeros_like(dq_ref)

        row = lax.broadcasted_iota(jnp.int32, (tq, tq), 0)
        col = lax.broadcasted_iota(jnp.int32, (tq, tq), 1)
        vlane = lax.broadcasted_iota(jnp.int32, (tq, 2 * V_DIM), 1)
        vb = v_ref[...]
        dv_acc = jnp.zeros((tq, 2 * V_DIM), F32)
        for h in range(2):
            lanes = slice(h * HEAD_PAD, (h + 1) * HEAD_PAD)
            kb = k_ref[:, lanes]
            in_h = (vlane // V_DIM) == h

            def step(i, carry, masked):
                dk_acc, dv_acc = carry
                start = pl.multiple_of(i * tq, tq)
                qb = q_ref[pl.ds(start, tq), lanes]
                dob = jnp.where(in_h, do_ref[pl.ds(start, tq), :], 0.0)
                delta = jnp.sum(dob * o_ref[pl.ds(start, tq), :], axis=-1, keepdims=True)
                lse = lse_ref[pl.ds(start, tq), lanes][:, 0:1]
                s = _mm_nt(qb, kb) * SCALE
                if masked:
                    s = jnp.where(col <= row, s, NEG)
                p = jnp.exp(s - lse)
                dob16 = dob.astype(BF16)
                dp = _mm_nt(dob16, vb)
                ds = (p * (dp - delta) * SCALE).astype(BF16)
                dv_acc = dv_acc + _mm_tn(p.astype(BF16), dob16)
                dk_acc = dk_acc + _mm_tn(ds, qb)
                dq_ref[pl.ds(start, tq), lanes] += _mm(ds, kb)
                return dk_acc, dv_acc

            carry = step(j, (jnp.zeros((tq, HEAD_PAD), F32), dv_acc), True)
            dk_acc, dv_acc = lax.fori_loop(j + 1, nq, lambda i, c: step(i, c, False), carry)
            dk_ref[:, lanes] = dk_acc
        dv_ref[...] = dv_acc
        if carried is not None:
            @pl.when((pl.program_id(0) == last_pair) & (j == nq - 1))
            def _():
                finish_c()
                finish_s()

    in_specs = [pl.BlockSpec((t, 2 * HEAD_PAD), lambda p, j: (0, p)),
                pl.BlockSpec((tq, 2 * HEAD_PAD), lambda p, j: (j, p)),
                pl.BlockSpec((tq, 2 * V_DIM), lambda p, j: (j, p)),
                pl.BlockSpec((t, 2 * V_DIM), lambda p, j: (0, p)),
                pl.BlockSpec((t, 2 * HEAD_PAD), lambda p, j: (0, p)),
                pl.BlockSpec((t, 2 * V_DIM), lambda p, j: (0, p))]
    out_specs = [pl.BlockSpec((t, 2 * HEAD_PAD), lambda p, j: (0, p)),
                 pl.BlockSpec((tq, 2 * HEAD_PAD), lambda p, j: (j, p)),
                 pl.BlockSpec((tq, 2 * V_DIM), lambda p, j: (j, p))]
    out_shape = [jax.ShapeDtypeStruct((t, HEADS * HEAD_PAD), F32), jax.ShapeDtypeStruct((t, HEADS * HEAD_PAD), F32),
                 jax.ShapeDtypeStruct((t, HEADS * V_DIM), F32)]
    if carried is None:
        return pl.pallas_call(
            body, name="attn_bwd", grid=(HEADS // 2, nq), in_specs=in_specs, out_specs=out_specs, out_shape=out_shape,
            compiler_params=_cp(("parallel", "arbitrary")),
        )(q, k, v, o, lse, do)
    chunks, small = carried
    hbm = pl.BlockSpec(memory_space=pl.ANY)
    return pl.pallas_call(
        body, name="attn_bwd_exchange", grid=(HEADS // 2, nq),
        in_specs=in_specs + [hbm, hbm], out_specs=out_specs + [hbm, hbm],
        out_shape=out_shape + [jax.ShapeDtypeStruct(chunks.shape, chunks.dtype),
                               jax.ShapeDtypeStruct((N_DEV,) + small.shape, small.dtype)],
        scratch_shapes=_comm_sems() + _comm_sems(),
        compiler_params=_cp(("arbitrary", "arbitrary")),
    )(q, k, v, o, lse, do, chunks, small)


def mla_proj_bwd(dq, dk, dv, z, ca, sb, sc, gq, gkv, wuq, wukv):
    t = z.shape[0]
    tm = min(TM, t)

    def body(dq_ref, dk_ref, dv_ref, z_ref, ca_ref, sb_ref, sc_ref, gq_ref, gkv_ref, wuq_ref, wukv_ref,
             dz_ref, cq_ref, ckv_ref, dqp_ref, dkvp_ref, dgq_ref, dgkv_ref):
        first = pl.program_id(0) == 0
        z = z_ref[...]
        ca, sb, sc = ca_ref[...], sb_ref[...], sc_ref[...]
        gq, gkv = gq_ref[...], gkv_ref[...]
        cq, cqh, rq = _rms_fwd(z[:, :Q_RANK], gq)
        ckv, ckvh, rkv = _rms_fwd(z[:, Q_RANK:Q_RANK + KV_RANK], gkv)
        lane = lax.broadcasted_iota(jnp.int32, (tm, HEAD_PAD), 1)
        dkr = jnp.zeros((tm, HEAD_PAD), F32)
        for h in range(HEADS):
            lanes = slice(h * HEAD_PAD, (h + 1) * HEAD_PAD)
            dqp_ref[:, lanes] = _rope_t(dq_ref[:, lanes], ca, sb, sc).astype(BF16)
            dkh = dk_ref[:, lanes]
            dkr = dkr + dkh
            dkvp_ref[:, lanes] = jnp.where(lane < NOPE, dkh, 0.0).astype(BF16)
        dkvp_ref[:, HEADS * HEAD_PAD:] = dv_ref[...].astype(BF16)
        dkr = pltpu.roll(_rope_t(jnp.where(lane >= NOPE, dkr, 0.0), ca, sb, sc), HEAD_PAD - NOPE, 1)
        dkr = jnp.where(lane < ROPE, dkr, 0.0)
        dcq = _mm(dqp_ref[...], wuq_ref[...])
        dckv = _mm(dkvp_ref[...], wukv_ref[...])
        dzq, dgq = _rms_bwd(cqh, rq, gq, dcq)
        dzkv, dgkv = _rms_bwd(ckvh, rkv, gkv, dckv)
        dz_ref[...] = jnp.concatenate([dzq, dzkv, dkr], axis=1).astype(BF16)
        cq_ref[...] = cq.astype(BF16)
        ckv_ref[...] = ckv.astype(BF16)
        _acc(dgq_ref, first, dgq)
        _acc(dgkv_ref, first, dgkv)

    row = lambda w: pl.BlockSpec((tm, w), lambda i: (i, 0))
    vec = lambda w: pl.BlockSpec((1, w), lambda i: (0, 0))
    return pl.pallas_call(
        body, name="mla_proj_bwd", grid=(t // tm,),
        in_specs=[row(1024), row(1024), row(512), pl.BlockSpec((tm, 768), lambda i: (i, 1)),
                  row(HEAD_PAD), row(HEAD_PAD), row(HEAD_PAD), _whole(), _whole(), _whole(), _whole()],
        out_specs=[row(768), row(Q_RANK), row(KV_RANK), row(1024), row(1536), vec(Q_RANK), vec(KV_RANK)],
        out_shape=[jax.ShapeDtypeStruct((t, 768), BF16), jax.ShapeDtypeStruct((t, Q_RANK), BF16),
                   jax.ShapeDtypeStruct((t, KV_RANK), BF16), jax.ShapeDtypeStruct((t, 1024), BF16),
                   jax.ShapeDtypeStruct((t, 1536), BF16), jax.ShapeDtypeStruct((1, Q_RANK), F32),
                   jax.ShapeDtypeStruct((1, KV_RANK), F32)],
        compiler_params=_cp(("arbitrary",)),
    )(dq, dk, dv, z, ca, sb, sc, gq, gkv, wuq, wukv)


def pre_in_bwd(x, dx1, dzcv, dzmla, dzsg, g, w):
    t = x.shape[0]
    tm = min(TM, t)

    def body(x_ref, dx1_ref, dzcv_ref, dzmla_ref, dzsg_ref, g_ref, w_ref, dx_ref, h_ref, dz_ref, dg_ref):
        g = g_ref[...]
        h, xh, r = _rms_fwd(x_ref[...], g)
        dz = jnp.concatenate([dzcv_ref[...], dzmla_ref[...], dzsg_ref[...]], axis=1)
        dx, dg = _rms_bwd(xh, r, g, _mm(dz, w_ref[...]))
        dx_ref[...] = dx1_ref[...] + dx
        h_ref[...] = h.astype(BF16)
        dz_ref[...] = dz
        _acc(dg_ref, pl.program_id(0) == 0, dg)

    row = lambda w_: pl.BlockSpec((tm, w_), lambda i: (i, 0))
    return pl.pallas_call(
        body, name="pre_in_bwd", grid=(t // tm,),
        in_specs=[row(D), row(D), row(768), row(768), row(512), _whole(), _whole()],
        out_specs=[row(D), row(D), row(Z_W), pl.BlockSpec((1, D), lambda i: (0, 0))],
        out_shape=[jax.ShapeDtypeStruct((t, D), F32), jax.ShapeDtypeStruct((t, D), BF16),
                   jax.ShapeDtypeStruct((t, Z_W), BF16), jax.ShapeDtypeStruct((1, D), F32)],
        compiler_params=_cp(("arbitrary",)),
    )(x, dx1, dzcv, dzmla, dzsg, g, w)


MESH = pl.DeviceIdType.MESH


def _place():
    return lax.axis_index("x"), lax.axis_index("y"), lax.axis_index("c")


def _comm_sems():
    return [pltpu.SemaphoreType.DMA((7,)), pltpu.SemaphoreType.DMA((7,)), pltpu.SemaphoreType.DMA]


def _gather_steps(x_ref, out_ref, send_sems, recv_sems, local_sem):
    x, y, c = _place()
    me, sibling = (x, y, c), (x, y, 1 - c)
    chips = [(1 - x, y), (x, 1 - y), (1 - x, 1 - y)]

    def slot(px, py, pc):
        return out_ref.at[4 * px + 2 * py + pc]

    def copy(k, blk, to, src=None):
        return pltpu.make_async_remote_copy(
            src_ref=slot(*blk) if src is None else src, dst_ref=slot(*blk),
            send_sem=send_sems.at[k], recv_sem=recv_sems.at[k], device_id=to, device_id_type=MESH)

    mine = pltpu.make_async_copy(x_ref, slot(*me), local_sem)
    first = [copy(0, me, sibling, src=x_ref)] + [copy(1 + j, me, (*chip, c), src=x_ref) for j, chip in enumerate(chips)]
    passed = [copy(4 + j, (*chip, c), sibling) for j, chip in enumerate(chips)]

    def start():
        mine.start()
        for cp in first:
            cp.start()

    def forward():
        for j, chip in enumerate(chips):
            copy(1 + j, (*chip, c), me).wait_recv()
            passed[j].start()

    def finish():
        copy(0, sibling, me).wait_recv()
        for j, chip in enumerate(chips):
            copy(4 + j, (*chip, 1 - c), me).wait_recv()
        for cp in first + passed:
            cp.wait_send()
        mine.wait()

    return start, forward, finish


def _exchange_steps(src_ref, out_ref, scatter, send_sems, recv_sems, local_sem):
    x, y, c = _place()
    me = 4 * x + 2 * y + c
    own = pltpu.make_async_copy(src_ref.at[me] if scatter else src_ref, out_ref.at[me], local_sem)
    copies = []
    for k in range(1, N_DEV):
        px = 1 - x if k & 4 else x
        py = 1 - y if k & 2 else y
        pc = 1 - c if k & 1 else c
        copies.append(pltpu.make_async_remote_copy(
            src_ref=src_ref.at[4 * px + 2 * py + pc] if scatter else src_ref, dst_ref=out_ref.at[me],
            send_sem=send_sems.at[k - 1], recv_sem=recv_sems.at[k - 1], device_id=(px, py, pc), device_id_type=MESH))

    def start():
        own.start()
        for cp in copies:
            cp.start()

    def finish():
        for cp in copies:
            cp.wait_recv()
        for cp in copies:
            cp.wait_send()
        own.wait()

    return start, finish


def all_gather(block):
    def body(x_ref, out_ref, *sems):
        for stage in _gather_steps(x_ref, out_ref, *sems):
            stage()

    return pl.pallas_call(
        body, name="all_gather",
        in_specs=[pl.BlockSpec(memory_space=pl.ANY)],
        out_specs=pl.BlockSpec(memory_space=pl.ANY),
        out_shape=jax.ShapeDtypeStruct((N_DEV,) + block.shape, block.dtype),
        scratch_shapes=_comm_sems(),
    )(block)


def grad_exchange(chunks, small):
    def body(chunks_ref, small_ref, got_ref, gots_ref, *sems):
        start_c, finish_c = _exchange_steps(chunks_ref, got_ref, True, *sems[:3])
        start_s, finish_s = _exchange_steps(small_ref, gots_ref, False, *sems[3:])
        start_c()
        start_s()
        finish_c()
        finish_s()

    return pl.pallas_call(
        body, name="grad_exchange",
        in_specs=[pl.BlockSpec(memory_space=pl.ANY), pl.BlockSpec(memory_space=pl.ANY)],
        out_specs=[pl.BlockSpec(memory_space=pl.ANY), pl.BlockSpec(memory_space=pl.ANY)],
        out_shape=[jax.ShapeDtypeStruct(chunks.shape, chunks.dtype),
                   jax.ShapeDtypeStruct((N_DEV,) + small.shape, small.dtype)],
        scratch_shapes=_comm_sems() + _comm_sems(),
    )(chunks, small)


def _row_tile(r, cap):
    return max(d for d in range(16, cap + 1, 16) if r % d == 0)


def sum_adamw(parts, w, m, v, cap):
    r, c = w.shape
    tr = _row_tile(r, cap)
    c1 = 1.0 / (1.0 - ADAM_B1 ** ADAM_STEP)
    c2 = 1.0 / (1.0 - ADAM_B2 ** ADAM_STEP)

    def body(p_ref, w_ref, m_ref, v_ref, g_ref, d_ref, nm_ref, nv_ref):
        g = p_ref[0].astype(F32)
        for k in range(1, N_DEV):
            g = g + p_ref[k].astype(F32)
        m = ADAM_B1 * m_ref[...] + (1.0 - ADAM_B1) * g
        v = ADAM_B2 * v_ref[...] + (1.0 - ADAM_B2) * (g * g)
        g_ref[...] = g
        nm_ref[...] = m
        nv_ref[...] = v
        d_ref[...] = -ADAM_LR * ((m * c1) / (jnp.sqrt(v * c2) + ADAM_EPS) + ADAM_WD * w_ref[...])

    blk = pl.BlockSpec((tr, c), lambda i: (i, 0))
    out = jax.ShapeDtypeStruct((r, c), F32)
    return pl.pallas_call(
        body, name="sum_adamw", grid=(r // tr,),
        in_specs=[pl.BlockSpec((N_DEV, tr, c), lambda i: (0, i, 0)), blk, blk, blk],
        out_specs=[blk, blk, blk, blk], out_shape=[out, out, out, out],
        compiler_params=_cp(("parallel",)),
    )(parts, w, m, v)


PACK_W = 1024
PIECES = (("w_out", D // N_DEV, D, False), ("w_gate", D_FF // N_DEV, D, True), ("w_up", D_FF // N_DEV, D, True),
          ("w_down", D_FF // N_DEV, D, False), ("w_uq", HEADS * (NOPE + ROPE) // N_DEV, Q_RANK, True),
          ("w_ukv", HEADS * (NOPE + V_DIM) // N_DEV, KV_RANK, True), ("conv", 16, PACK_W, False),
          ("w_in", IN_W // N_DEV, D, True))
PACK_ROWS = 1680
OFFSET = {}
_off = 0
for _name, _rows, _, _ in PIECES:
    OFFSET[_name] = _off
    _off += _rows + -_rows % 16
assert _off == PACK_ROWS and all(o % 16 == 0 for o in OFFSET.values())
CONV_BITS = 3 * (CV_W // N_DEV) * 2


def _to_pack(shards, dtype, conv=None):
    nl = shards["w_in"].shape[0]
    parts = []
    for name, rows, cols, transposed in PIECES:
        if name == "conv":
            if conv is None:
                a = jnp.zeros((nl, rows, PACK_W), dtype)
            else:
                bits = lax.bitcast_convert_type(conv.astype(F32), BF16).reshape(nl, CONV_BITS)
                a = jnp.pad(bits, ((0, 0), (0, rows * PACK_W - CONV_BITS))).reshape(nl, rows, PACK_W)
        else:
            a = shards[name].astype(dtype)
            a = jnp.swapaxes(a, 1, 2) if transposed else a
            a = jnp.pad(a, ((0, 0), (0, -rows % 16), (0, PACK_W - cols)))
        parts.append(a)
    return jnp.concatenate(parts, axis=1)


def _from_pack(pack):
    out = {}
    for name, rows, cols, transposed in PIECES:
        if name != "conv":
            a = pack[:, OFFSET[name]:OFFSET[name] + rows, :cols]
            out[name] = jnp.swapaxes(a, 1, 2) if transposed else a
    return out


def _kernel_weights(g):
    def rows(name):
        _, n, cols, _ = next(p for p in PIECES if p[0] == name)
        return g[:, OFFSET[name]:OFFSET[name] + n, :cols]

    w_in_t = rows("w_in").reshape(IN_W, D)
    w_in_p = jnp.concatenate([w_in_t[1184:], w_in_t[:672], jnp.zeros((96, D), BF16), w_in_t[672:1184]], axis=0)
    w_uq_p = jnp.pad(rows("w_uq"), ((0, 0), (0, HEAD_PAD - NOPE - ROPE), (0, 0))).reshape(HEADS * HEAD_PAD, Q_RANK)
    kv = rows("w_ukv")
    w_k = jnp.pad(kv[:, :NOPE], ((0, 0), (0, HEAD_PAD - NOPE), (0, 0))).reshape(HEADS * HEAD_PAD, KV_RANK)
    w_ukv_p = jnp.concatenate([w_k, kv[:, NOPE:].reshape(HEADS * V_DIM, KV_RANK)], axis=0)
    bits = rows("conv").reshape(N_DEV, -1)[:, :CONV_BITS].reshape(N_DEV, 3, CV_W // N_DEV, 2)
    conv_w = jnp.moveaxis(lax.bitcast_convert_type(bits, F32), 0, 1).reshape(3, CV_W)
    return dict(w_in=w_in_p, w_uq=w_uq_p, w_ukv=w_ukv_p, w_out=rows("w_out").reshape(D, D),
                w_gu=jnp.concatenate([rows("w_gate").reshape(D_FF, D), rows("w_up").reshape(D_FF, D)], axis=0),
                w_down=rows("w_down").reshape(D_FF, D), conv_w=conv_w)


def _grad_chunks(full):
    d_in = full["w_in"]
    d_in = jnp.concatenate([d_in[768:768 + 672], d_in[1536:], d_in[:768]], axis=0)
    d_uq = full["w_uq"].reshape(HEADS, HEAD_PAD, Q_RANK)[:, :NOPE + ROPE]
    d_k = full["w_ukv"][:HEADS * HEAD_PAD].reshape(HEADS, HEAD_PAD, KV_RANK)[:, :NOPE]
    d_v = full["w_ukv"][HEADS * HEAD_PAD:].reshape(HEADS, V_DIM, KV_RANK)
    mats = dict(w_in=d_in, w_uq=d_uq, w_ukv=jnp.concatenate([d_k, d_v], axis=1), w_out=full["w_out"],
                w_gate=full["w_gu"][:D_FF], w_up=full["w_gu"][D_FF:], w_down=full["w_down"])
    parts = []
    for name, rows, cols, _ in PIECES:
        if name == "conv":
            parts.append(jnp.zeros((N_DEV, rows, PACK_W), BF16))
        else:
            parts.append(jnp.pad(mats[name].reshape(N_DEV, rows, cols), ((0, 0), (0, -rows % 16), (0, PACK_W - cols))))
    return jnp.concatenate(parts, axis=1)


SMALL = (("mix_pre_g", (D,)), ("mix_post_g", (D,)), ("ffn_pre_g", (D,)), ("ffn_post_g", (D,)), ("q_norm_g", (Q_RANK,)),
         ("kv_norm_g", (KV_RANK,)), ("sg_ln_g", (SG_W,)), ("sg_ln_b", (SG_W,)), ("w_sp", (4, CHUNK, CHUNK)),
         ("b_sp", (4, CHUNK)), ("out_norm_g", (D,)))
SMALL_ROWS = 576


def _pack_small(vals, nl):
    flat = jnp.concatenate([vals[name].reshape(nl, -1) for name, _ in SMALL] + [vals["conv_w"].reshape(nl, -1)], axis=1)
    return jnp.pad(flat, ((0, 0), (0, SMALL_ROWS * 128 - flat.shape[1]))).reshape(nl * SMALL_ROWS, 128)


def _unpack_small(pack, nl):
    flat = pack.reshape(nl, SMALL_ROWS * 128)
    out, off = {}, 0
    for name, shape in SMALL + (("conv_w", (3, CV_W)),):
        n = int(np.prod(shape))
        out[name] = flat[:, off:off + n].reshape((nl,) + shape)
        off += n
    return out


def _layer_fwd(x, lw, sp, tabs, consts, next_pack):
    ca, sb, sc = tabs
    z = pre_in_fwd(x, sp["mix_pre_g"], lw["w_in"])
    q, k, v = mla_proj_fwd(z, ca, sb, sc, sp["q_norm_g"], sp["kv_norm_g"], lw["w_uq"], lw["w_ukv"])
    ya, lse, *gathered = attn_fwd(q, k, v, next_pack)
    x1 = mix_fwd(x, z, ya, consts["gm"], sp["sg_ln_g"], sp["sg_ln_b"], sp["w_sp"], sp["bias"], lw["conv_w"],
                 sp["out_norm_g"], lw["w_out"], sp["mix_post_g"])
    x2 = ffn_fwd(x1, sp["ffn_pre_g"], lw["w_gu"], lw["w_down"], sp["ffn_post_g"])
    return x2, (x, z, q, k, v, ya, lse, x1), gathered


def _layer_bwd(dx2, saved, lw, sp, tabs, consts, pending):
    ca, sb, sc = tabs
    x, z, q, k, v, ya, lse, x1 = saved
    dx1, h2, dab, s, df, d_ffn_pre, d_ffn_post = ffn_bwd(x1, dx2, sp["ffn_pre_g"], lw["w_gu"], lw["w_down"], sp["ffn_post_g"])
    d_w_gu = atb(dab, h2, 1408)
    d_w_down = atb(s, df, 1408)
    dya, dyc, dzsg, mix, do, d_mix_post, d_out_norm, d_lng, d_lnb, d_wsp, d_bias = mix_bwd(
        dx1, z, ya, consts["gm"], sp["sg_ln_g"], sp["sg_ln_b"], sp["w_sp"], sp["w_sp_t"], sp["bias"], lw["conv_w"],
        sp["out_norm_g"], lw["w_out"], sp["mix_post_g"])
    d_w_out = atb(mix, do, 1024)
    dzcv, d_cw = conv_bwd(dyc, z, lw["conv_w"])
    dq, dk, dv, *received = attn_bwd(q, k, v, ya, lse, dya, pending)
    dzmla, cq, ckv, dqp, dkvp, d_gq, d_gkv = mla_proj_bwd(dq, dk, dv, z, ca, sb, sc, sp["q_norm_g"], sp["kv_norm_g"],
                                                          lw["w_uq"], lw["w_ukv"])
    d_w_uq = atb(dqp, cq, 1024)
    d_w_ukv = atb(dkvp, ckv, 1536)
    dx, h1, dz, d_mix_pre = pre_in_bwd(x, dx1, dzcv, dzmla, dzsg, sp["mix_pre_g"], lw["w_in"])
    d_w_in = atb(dz, h1, 2048)
    big = dict(w_in=d_w_in, w_uq=d_w_uq, w_ukv=d_w_ukv, w_out=d_w_out, w_gu=d_w_gu, w_down=d_w_down)
    d_bsp = d_bias[:, ::GROUP].T
    small = dict(mix_pre_g=d_mix_pre[0], mix_post_g=d_mix_post[0], ffn_pre_g=d_ffn_pre[0], ffn_post_g=d_ffn_post[0],
                 q_norm_g=d_gq[0], kv_norm_g=d_gkv[0], sg_ln_g=d_lng[0], sg_ln_b=d_lnb[0], w_sp=d_wsp, b_sp=d_bsp,
                 out_norm_g=d_out_norm[0], conv_w=d_cw[:3])
    return dx, (_grad_chunks(big), _pack_small({name: a[None] for name, a in small.items()}, 1)), received


def kernel(x, positions, mix_pre_g, mix_post_g, ffn_pre_g, ffn_post_g, w_in, q_norm_g, w_uq, kv_norm_g, w_ukv, sg_ln_g, sg_ln_b, w_sp, b_sp, conv_w, out_norm_g, w_out, w_gate, w_up, w_down, loss_target, m_mix_pre_g, m_mix_post_g, m_ffn_pre_g, m_ffn_post_g, m_w_in, m_q_norm_g, m_w_uq, m_kv_norm_g, m_w_ukv, m_sg_ln_g, m_sg_ln_b, m_w_sp, m_b_sp, m_conv_w, m_out_norm_g, m_w_out, m_w_gate, m_w_up, m_w_down, v_mix_pre_g, v_mix_post_g, v_ffn_pre_g, v_ffn_post_g, v_w_in, v_q_norm_g, v_w_uq, v_kv_norm_g, v_w_ukv, v_sg_ln_g, v_sg_ln_b, v_w_sp, v_b_sp, v_conv_w, v_out_norm_g, v_w_out, v_w_gate, v_w_up, v_w_down):
    nl = w_in.shape[0]
    t = x.shape[1]
    w = dict(mix_pre_g=mix_pre_g, mix_post_g=mix_post_g, ffn_pre_g=ffn_pre_g, ffn_post_g=ffn_post_g, w_in=w_in,
             q_norm_g=q_norm_g, w_uq=w_uq, kv_norm_g=kv_norm_g, w_ukv=w_ukv, sg_ln_g=sg_ln_g, sg_ln_b=sg_ln_b, w_sp=w_sp,
             b_sp=b_sp, conv_w=conv_w, out_norm_g=out_norm_g, w_out=w_out, w_gate=w_gate, w_up=w_up, w_down=w_down)
    m = dict(mix_pre_g=m_mix_pre_g, mix_post_g=m_mix_post_g, ffn_pre_g=m_ffn_pre_g, ffn_post_g=m_ffn_post_g, w_in=m_w_in,
             q_norm_g=m_q_norm_g, w_uq=m_w_uq, kv_norm_g=m_kv_norm_g, w_ukv=m_w_ukv, sg_ln_g=m_sg_ln_g, sg_ln_b=m_sg_ln_b,
             w_sp=m_w_sp, b_sp=m_b_sp, conv_w=m_conv_w, out_norm_g=m_out_norm_g, w_out=m_w_out, w_gate=m_w_gate,
             w_up=m_w_up, w_down=m_w_down)
    v = dict(mix_pre_g=v_mix_pre_g, mix_post_g=v_mix_post_g, ffn_pre_g=v_ffn_pre_g, ffn_post_g=v_ffn_post_g, w_in=v_w_in,
             q_norm_g=v_q_norm_g, w_uq=v_w_uq, kv_norm_g=v_kv_norm_g, w_ukv=v_w_ukv, sg_ln_g=v_sg_ln_g, sg_ln_b=v_sg_ln_b,
             w_sp=v_w_sp, b_sp=v_b_sp, conv_w=v_conv_w, out_norm_g=v_out_norm_g, w_out=v_w_out, w_gate=v_w_gate,
             w_up=v_w_up, w_down=v_w_down)

    w_pack = _to_pack(w, BF16, conv=w["conv_w"])
    consts = dict(gm=jnp.asarray(np.kron(np.eye(SG_W // GROUP), np.full((GROUP, GROUP), 1.0 / GROUP)), F32))
    smalls = []
    for l in range(nl):
        sp = {name: w[name][l].reshape(1, -1) for name, shape in SMALL if len(shape) == 1}
        sp["w_sp"] = w["w_sp"][l]
        sp["w_sp_t"] = jnp.swapaxes(w["w_sp"][l], 1, 2)
        sp["bias"] = jnp.repeat(w["b_sp"][l].T, GROUP, axis=1)
        smalls.append(sp)
    inv_freq = 1.0 / (ROPE_THETA ** (jnp.arange(0, ROPE // 2, dtype=F32) / (ROPE // 2)))
    inv = jnp.zeros((1, HEAD_PAD), F32).at[0, NOPE:NOPE + ROPE].set(jnp.concatenate([inv_freq, inv_freq]))
    tabs = rope_tables(positions.reshape(t, 1).astype(F32), inv)

    h = x[0]
    saved, layers = [], []
    gathered = [all_gather(w_pack[0])]
    for l in range(nl):
        layers.append(_kernel_weights(gathered[0]))
        h, s, gathered = _layer_fwd(h, layers[l], smalls[l], tabs, consts, w_pack[l + 1] if l + 1 < nl else None)
        saved.append(s)
    sq, dh = loss_head(h, loss_target[0])
    loss = lax.psum(0.5 * sq[0, 0] / D, ("x", "y", "c"))

    got = [None] * nl
    pending = None
    for l in reversed(range(nl)):
        dh, pending, received = _layer_bwd(dh, saved[l], layers[l], smalls[l], tabs, consts, pending)
        if received:
            got[l + 1] = received
    got[0] = grad_exchange(*pending)
    got_big = [g[0] for g in got]
    got_small = jnp.concatenate([g[1] for g in got], axis=1)

    me = 4 * lax.axis_index("x") + 2 * lax.axis_index("y") + lax.axis_index("c")
    w_f32, m_f32, v_f32 = _to_pack(w, F32), _to_pack(m, F32), _to_pack(v, F32)
    per_layer = [sum_adamw(got_big[l], w_f32[l], m_f32[l], v_f32[l], 240) for l in range(nl)]
    g_big, d_big, m_big, v_big = [_from_pack(jnp.stack([per_layer[l][i] for l in range(nl)])) for i in range(4)]

    def full_conv(a):
        return lax.dynamic_update_slice(jnp.zeros((nl, 3, CV_W), F32), a, (0, 0, me * (CV_W // N_DEV)))

    def small_pack(d):
        return _pack_small({**{name: d[name] for name, _ in SMALL}, "conv_w": full_conv(d["conv_w"])}, nl)

    g_small, d_small, m_small, v_small = [_unpack_small(p, nl) for p in
                                          sum_adamw(got_small, small_pack(w), small_pack(m), small_pack(v), 1152)]
    outs = []
    for big, small in ((g_big, g_small), (d_big, d_small), (m_big, m_small), (v_big, v_small)):
        for name in w:
            if name == "conv_w":
                outs.append(lax.dynamic_slice(small[name], (0, 0, me * (CV_W // N_DEV)), (nl, 3, CV_W // N_DEV)))
            elif name in small:
                outs.append(small[name])
            else:
                outs.append(big[name])
    return (loss, dh[None], *outs)
```

```python
import functools

import jax
import jax.numpy as jnp
import numpy as np
from jax import lax
from jax.experimental import pallas as pl
from jax.experimental.pallas import tpu as pltpu

F32 = jnp.float32
BF16 = jnp.bfloat16

D = 1024
Q_RANK = 384
KV_RANK = 256
ROPE = 32
HEADS = 8
NOPE = 64
V_DIM = 64
HEAD_PAD = 128
SG_W = 256
CV_W = 256
CHUNK = 128
GROUP = 64
D_FF = 2816
IN_W = 1952
Z_W = 2048
Z_CV, Z_MLA, Z_SG = 0, 768, 1536
EPS = 1e-6
ROPE_THETA = 10000.0
SCALE = (NOPE + ROPE) ** -0.5
LOG2E = 1.4426950408889634
SCALE_LOG2E = SCALE * LOG2E
NEG = -1e30
N_DEV = 8

ADAM_LR, ADAM_B1, ADAM_B2, ADAM_EPS, ADAM_WD, ADAM_STEP = 0.001, 0.9, 0.999, 1e-08, 0.01, 10

VMEM_LIMIT = 56 * 1024 * 1024

TM = 512
TM_FFN = 256
FFN_SLAB = 256
TQ = 512
TT = 512


def _cp(sem, vmem=VMEM_LIMIT):
    return pltpu.CompilerParams(dimension_semantics=sem, vmem_limit_bytes=vmem)


def _whole():
    return pl.BlockSpec(memory_space=pltpu.VMEM)


def _mm(a, b):
    return jnp.dot(a, b, preferred_element_type=F32)


def _mm_nt(a, b):
    return lax.dot_general(a, b, (((1,), (1,)), ((), ())), preferred_element_type=F32)


def _mm_tn(a, b):
    return lax.dot_general(a, b, (((0,), (0,)), ((), ())), preferred_element_type=F32)


def _rms_fwd(x, g):
    r = lax.rsqrt(jnp.mean(x * x, axis=-1, keepdims=True) + EPS)
    xh = x * r
    return xh * g, xh, r


def _rms_bwd(xh, r, g, dy):
    dxh = dy * g
    dx = r * (dxh - xh * jnp.mean(dxh * xh, axis=-1, keepdims=True))
    dg = jnp.sum(dy * xh, axis=0, keepdims=True)
    return dx, dg


def _gmean(v, gm):
    return jnp.dot(v, gm, precision=lax.Precision.HIGHEST, preferred_element_type=F32)


def _gelu(x):
    c = np.float32(np.sqrt(2.0 / np.pi))
    u = c * (x + 0.044715 * x * x * x)
    t = jnp.tanh(u)
    return 0.5 * x * (1.0 + t), t


def _gelu_grad(x, t):
    c = np.float32(np.sqrt(2.0 / np.pi))
    return 0.5 * (1.0 + t) + 0.5 * x * (1.0 - t * t) * c * (1.0 + 3.0 * 0.044715 * x * x)


def _rope(t, ca, sb, sc):
    return t * ca + pltpu.roll(t, HEAD_PAD - 16, 1) * sb + pltpu.roll(t, 16, 1) * sc


def _rope_t(dt, ca, sb, sc):
    return dt * ca + pltpu.roll(dt * sb, 16, 1) + pltpu.roll(dt * sc, HEAD_PAD - 16, 1)


def _shift_down(y, k, head):
    n = y.shape[0]
    out = pltpu.roll(y, k, 0)
    row = lax.broadcasted_iota(jnp.int32, y.shape, 0)
    for j in range(k):
        out = jnp.where(row == j, head[8 - k + j:8 - k + j + 1, :], out)
    return out


def _shift_up(y, k, tail):
    n = y.shape[0]
    out = pltpu.roll(y, n - k, 0)
    row = lax.broadcasted_iota(jnp.int32, y.shape, 0)
    for j in range(k):
        out = jnp.where(row == n - k + j, tail[j:j + 1, :], out)
    return out


def rope_tables(pos, inv):
    t = pos.shape[0]
    tm = min(TM, t)

    def body(pos_ref, inv_ref, ca_ref, sb_ref, sc_ref):
        ang = pos_ref[...] * inv_ref[...]
        c = jnp.cos(ang)
        s = jnp.sin(ang)
        lane = lax.broadcasted_iota(jnp.int32, ang.shape, 1)
        ca_ref[...] = jnp.where(lane < NOPE, 1.0, jnp.where(lane < NOPE + ROPE, c, 0.0))
        sb_ref[...] = jnp.where((lane >= NOPE) & (lane < NOPE + 16), -s, 0.0)
        sc_ref[...] = jnp.where((lane >= NOPE + 16) & (lane < NOPE + ROPE), s, 0.0)

    out = jax.ShapeDtypeStruct((t, HEAD_PAD), F32)
    blk = pl.BlockSpec((tm, HEAD_PAD), lambda i: (i, 0))
    return pl.pallas_call(
        body, name="rope_tables", grid=(t // tm,),
        in_specs=[pl.BlockSpec((tm, 1), lambda i: (i, 0)), pl.BlockSpec((1, HEAD_PAD), lambda i: (0, 0))],
        out_specs=[blk, blk, blk], out_shape=[out, out, out],
        compiler_params=_cp(("parallel",)),
    )(pos, inv)


def pre_in_fwd(x, g, w):
    t = x.shape[0]
    tm = min(TM, t)

    def body(x_ref, g_ref, w_ref, z_ref):
        h, _, _ = _rms_fwd(x_ref[...], g_ref[...])
        z_ref[...] = _mm_nt(h.astype(BF16), w_ref[...])

    return pl.pallas_call(
        body, name="pre_in_fwd", grid=(t // tm,),
        in_specs=[pl.BlockSpec((tm, D), lambda i: (i, 0)), _whole(), _whole()],
        out_specs=pl.BlockSpec((tm, Z_W), lambda i: (i, 0)),
        out_shape=jax.ShapeDtypeStruct((t, Z_W), F32),
        compiler_params=_cp(("parallel",)),
    )(x, g, w)


def mla_proj_fwd(z, ca, sb, sc, gq, gkv, wuq, wukv):
    t = z.shape[0]
    tm = min(TM, t)

    def body(z_ref, ca_ref, sb_ref, sc_ref, gq_ref, gkv_ref, wuq_ref, wukv_ref, q_ref, k_ref, v_ref):
        z = z_ref[...]
        ca, sb, sc = ca_ref[...], sb_ref[...], sc_ref[...]
        cq, _, _ = _rms_fwd(z[:, :Q_RANK], gq_ref[...])
        ckv, _, _ = _rms_fwd(z[:, Q_RANK:Q_RANK + KV_RANK], gkv_ref[...])
        q = _mm_nt(cq.astype(BF16), wuq_ref[...])
        kv = _mm_nt(ckv.astype(BF16), wukv_ref[...])
        kr = _rope(pltpu.roll(z[:, Q_RANK + KV_RANK:], NOPE, 1), ca, sb, sc)
        for h in range(HEADS):
            lanes = slice(h * HEAD_PAD, (h + 1) * HEAD_PAD)
            q_ref[:, lanes] = _rope(q[:, lanes], ca, sb, sc).astype(BF16)
            k_ref[:, lanes] = (kv[:, lanes] + kr).astype(BF16)
        v_ref[...] = kv[:, HEADS * HEAD_PAD:].astype(BF16)

    tab = pl.BlockSpec((tm, HEAD_PAD), lambda i: (i, 0))
    return pl.pallas_call(
        body, name="mla_proj_fwd", grid=(t // tm,),
        in_specs=[pl.BlockSpec((tm, 768), lambda i: (i, 1)), tab, tab, tab, _whole(), _whole(), _whole(), _whole()],
        out_specs=[pl.BlockSpec((tm, HEADS * HEAD_PAD), lambda i: (i, 0)),
                   pl.BlockSpec((tm, HEADS * HEAD_PAD), lambda i: (i, 0)),
                   pl.BlockSpec((tm, HEADS * V_DIM), lambda i: (i, 0))],
        out_shape=[jax.ShapeDtypeStruct((t, HEADS * HEAD_PAD), BF16),
                   jax.ShapeDtypeStruct((t, HEADS * HEAD_PAD), BF16),
                   jax.ShapeDtypeStruct((t, HEADS * V_DIM), BF16)],
        compiler_params=_cp(("parallel",)),
    )(z, ca, sb, sc, gq, gkv, wuq, wukv)


def _each(stages, k):
    def run():
        for stage in stages:
            stage[k]()
    return run


def attn_fwd(q, k, v, carried=()):
    t = q.shape[0]
    tq = min(TQ, t)
    nq = t // tq
    last_pair = HEADS // 2 - 1
    n = len(carried)

    def body(*refs):
        q_ref, k_ref, v_ref = refs[:3]
        o_ref, lse_ref = refs[3 + n:5 + n]
        sems = refs[5 + 2 * n:]
        stages = [_gather_steps(refs[3 + a], refs[5 + n + a], *sems[3 * a:3 * a + 3]) for a in range(n)]
        if n:
            pair = pl.program_id(0)
            pl.when((pair == 0) & (pl.program_id(1) == 0))(_each(stages, 0))
            pl.when((pair == last_pair) & (pl.program_id(1) == 0))(_each(stages, 1))
        i = pl.program_id(1)
        row = lax.broadcasted_iota(jnp.int32, (tq, tq), 0)
        col = lax.broadcasted_iota(jnp.int32, (tq, tq), 1)
        head_lanes = [slice(h * HEAD_PAD, (h + 1) * HEAD_PAD) for h in range(2)]

        def step(j, carry, masked):
            start = pl.multiple_of(j * tq, tq)
            vb = v_ref[pl.ds(start, tq), :]
            out = []
            for h in range(2):
                m, l, acc = carry[h]
                s = _mm_nt(q_ref[:, head_lanes[h]], k_ref[pl.ds(start, tq), head_lanes[h]])
                if masked:
                    s = jnp.where(col <= row, s, NEG)
                m_new = jnp.maximum(m, jnp.max(s, axis=-1, keepdims=True))
                p = jnp.exp2((s - m_new) * SCALE_LOG2E)
                alpha = jnp.exp2((m - m_new) * SCALE_LOG2E)
                l = alpha * l + jnp.sum(p, axis=-1, keepdims=True)
                acc = alpha * acc + _mm(p.astype(BF16), vb)
                out.append((m_new, l, acc))
            return tuple(out)

        init = (jnp.full((tq, 1), NEG, F32), jnp.zeros((tq, 1), F32), jnp.zeros((tq, 2 * V_DIM), F32))
        carry = lax.fori_loop(0, i, lambda j, c: step(j, c, False), (init, init))
        outs = []
        for h, (m, l, acc) in enumerate(step(i, carry, True)):
            outs.append(acc / l)
            lse_ref[:, head_lanes[h]] = jnp.broadcast_to(m * SCALE + jnp.log(l), (tq, HEAD_PAD))
        lane = lax.broadcasted_iota(jnp.int32, (tq, 2 * V_DIM), 1)
        o_ref[...] = jnp.where(lane < V_DIM, outs[0], outs[1])
        if n:
            pl.when((pl.program_id(0) == last_pair) & (i == nq - 1))(_each(stages, 2))

    hbm = pl.BlockSpec(memory_space=pl.ANY)
    return pl.pallas_call(
        body, name=f"attn_fwd_gather{n}" if n else "attn_fwd", grid=(HEADS // 2, nq),
        in_specs=[pl.BlockSpec((tq, 2 * HEAD_PAD), lambda p, i: (i, p)),
                  pl.BlockSpec((t, 2 * HEAD_PAD), lambda p, i: (0, p)),
                  pl.BlockSpec((t, 2 * V_DIM), lambda p, i: (0, p))] + [hbm] * n,
        out_specs=[pl.BlockSpec((tq, 2 * V_DIM), lambda p, i: (i, p)),
                   pl.BlockSpec((tq, 2 * HEAD_PAD), lambda p, i: (i, p))] + [hbm] * n,
        out_shape=[jax.ShapeDtypeStruct((t, HEADS * V_DIM), F32), jax.ShapeDtypeStruct((t, HEADS * HEAD_PAD), F32)]
        + [jax.ShapeDtypeStruct((N_DEV,) + c.shape, c.dtype) for c in carried],
        scratch_shapes=_comm_sems() * n,
        compiler_params=_cp(("arbitrary", "arbitrary") if n else ("parallel", "parallel")),
    )(q, k, v, *carried)


def _sgu_fwd(zsg, gm, lng, lnb, wc_ref, bias, mixed_ref):
    uv, th = _gelu(zsg)
    u, v0 = uv[:, :SG_W], uv[:, SG_W:]
    vc = v0 - _gmean(v0, gm)
    r = lax.rsqrt(_gmean(vc * vc, gm) + EPS)
    vh = vc * r
    v = vh * lng + lnb
    lane = lax.broadcasted_iota(jnp.int32, (CHUNK, SG_W), 1)
    for c in range(zsg.shape[0] // CHUNK):
        rows = slice(c * CHUNK, (c + 1) * CHUNK)
        vb = v[rows].astype(BF16)
        mixed = bias
        for g in range(SG_W // GROUP):
            mixed = mixed + jnp.where(lane // GROUP == g, _mm(wc_ref[g], vb), 0.0)
        mixed_ref[rows, :] = mixed
    return u, v, vh, r, th


def _conv_fwd(zcv, halo, first, cw):
    gb, gc, hh = zcv[:, :CV_W], zcv[:, CV_W:2 * CV_W], zcv[:, 2 * CV_W:]
    y = gc * hh
    yh = jnp.where(first, 0.0, halo[:, CV_W:2 * CV_W] * halo[:, 2 * CV_W:])
    y1 = _shift_down(y, 1, yh)
    y2 = _shift_down(y, 2, yh)
    conv = y2 * cw[0:1, :] + y1 * cw[1:2, :] + y * cw[2:3, :]
    return gb * conv, conv, y, y1, y2


def _tril_bf16(w_ref, g):
    row = lax.broadcasted_iota(jnp.int32, (CHUNK, CHUNK), 0)
    col = lax.broadcasted_iota(jnp.int32, (CHUNK, CHUNK), 1)
    return jnp.where(col <= row, w_ref[g], 0.0).astype(BF16)


def mix_fwd(x, z, ya, gm, lng, lnb, wsp, bias, cw, gout, wout, gpost):
    t = x.shape[0]
    tm = min(TM, t)

    def body(x_ref, zcv_ref, halo_ref, zsg_ref, ya_ref, gm_ref, lng_ref, lnb_ref, wsp_ref, bias_ref, cw_ref,
             gout_ref, wout_ref, gpost_ref, x1_ref, wc_ref, mixed_ref):
        i = pl.program_id(0)
        for g in range(SG_W // GROUP):
            wc_ref[g] = _tril_bf16(wsp_ref, g)
        u, _, _, _, _ = _sgu_fwd(zsg_ref[...], gm_ref[...], lng_ref[...], lnb_ref[...], wc_ref, bias_ref[...], mixed_ref)
        yb = u * mixed_ref[...]
        yc, _, _, _, _ = _conv_fwd(zcv_ref[...], halo_ref[...], i == 0, cw_ref[...])
        gout = gout_ref[...]
        na, _, _ = _rms_fwd(ya_ref[...], gout[:, :512])
        nb, _, _ = _rms_fwd(yb, gout[:, 512:768])
        nc, _, _ = _rms_fwd(yc, gout[:, 768:])
        mix = jnp.concatenate([na, nb, nc], axis=1).astype(BF16)
        o, _, _ = _rms_fwd(_mm(mix, wout_ref[...]), gpost_ref[...])
        x1_ref[...] = x_ref[...] + o

    hb = tm // 8
    return pl.pallas_call(
        body, name="mix_fwd", grid=(t // tm,),
        in_specs=[pl.BlockSpec((tm, D), lambda i: (i, 0)),
                  pl.BlockSpec((tm, 768), lambda i: (i, 0)),
                  pl.BlockSpec((8, 768), lambda i: (jnp.maximum(i * hb - 1, 0), 0)),
                  pl.BlockSpec((tm, 512), lambda i: (i, 3)),
                  pl.BlockSpec((tm, 512), lambda i: (i, 0)),
                  _whole(), _whole(), _whole(), _whole(), _whole(), _whole(), _whole(), _whole(), _whole()],
        out_specs=pl.BlockSpec((tm, D), lambda i: (i, 0)),
        out_shape=jax.ShapeDtypeStruct((t, D), F32),
        scratch_shapes=[pltpu.VMEM((SG_W // GROUP, CHUNK, CHUNK), BF16), pltpu.VMEM((tm, SG_W), F32)],
        compiler_params=_cp(("arbitrary",)),
    )(x, z, z, z, ya, gm, lng, lnb, wsp, bias, cw, gout, wout, gpost)


def _sigmoid(a):
    return 1.0 / (1.0 + jnp.exp(-a))


def ffn_fwd(x1, gpre, wgu, wd, gpost):
    t = x1.shape[0]
    tm = min(TM_FFN, t)

    def body(x_ref, gpre_ref, wgu_ref, wd_ref, gpost_ref, x2_ref):
        x = x_ref[...]
        h, _, _ = _rms_fwd(x, gpre_ref[...])
        ab = _mm_nt(h.astype(BF16), wgu_ref[...])
        a, b = ab[:, :D_FF], ab[:, D_FF:]
        s = a * _sigmoid(a) * b
        f, _, _ = _rms_fwd(_mm(s.astype(BF16), wd_ref[...]), gpost_ref[...])
        x2_ref[...] = x + f

    return pl.pallas_call(
        body, name="ffn_fwd", grid=(t // tm,),
        in_specs=[pl.BlockSpec((tm, D), lambda i: (i, 0)), _whole(), _whole(), _whole(), _whole()],
        out_specs=pl.BlockSpec((tm, D), lambda i: (i, 0)),
        out_shape=jax.ShapeDtypeStruct((t, D), F32),
        compiler_params=_cp(("parallel",)),
    )(x1, gpre, wgu, wd, gpost)


def loss_head(y, target):
    t = y.shape[0]
    tm = min(TM, t)

    def body(y_ref, t_ref, loss_ref, dy_ref):
        @pl.when(pl.program_id(0) == 0)
        def _():
            loss_ref[...] = jnp.zeros_like(loss_ref)

        e = y_ref[...] - t_ref[...]
        dy_ref[...] = e * (1.0 / D)
        loss_ref[...] += jnp.sum(jnp.sum(e * e, axis=-1, keepdims=True), axis=0, keepdims=True)

    return pl.pallas_call(
        body, name="loss_head", grid=(t // tm,),
        in_specs=[pl.BlockSpec((tm, D), lambda i: (i, 0)), pl.BlockSpec((tm, D), lambda i: (i, 0))],
        out_specs=[pl.BlockSpec((1, 128), lambda i: (0, 0)), pl.BlockSpec((tm, D), lambda i: (i, 0))],
        out_shape=[jax.ShapeDtypeStruct((1, 128), F32), jax.ShapeDtypeStruct((t, D), F32)],
        compiler_params=_cp(("arbitrary",)),
    )(y, target)


def _acc(ref, first, val):
    @pl.when(first)
    def _():
        ref[...] = val

    @pl.when(jnp.logical_not(first))
    def _():
        ref[...] += val


def ffn_bwd(x1, dx2, gpre, wgu, wd, gpost):
    t = x1.shape[0]
    tm = min(TM_FFN, t)

    def body(x_ref, dx2_ref, gpre_ref, wgu_ref, wd_ref, gpost_ref,
             dx1_ref, h_ref, dab_ref, s_ref, df_ref, dgpre_ref, dgpost_ref, ab_ref, ds_ref):
        first = pl.program_id(0) == 0
        dx2 = dx2_ref[...]
        gpre, gpost = gpre_ref[...], gpost_ref[...]
        h, xh, rx = _rms_fwd(x_ref[...], gpre)
        h_ref[...] = h.astype(BF16)
        ab_ref[...] = _mm_nt(h_ref[...], wgu_ref[...])
        for c in range(0, D_FF, FFN_SLAB):
            a, b = ab_ref[:, c:c + FFN_SLAB], ab_ref[:, D_FF + c:D_FF + c + FFN_SLAB]
            s_ref[:, c:c + FFN_SLAB] = (a * _sigmoid(a) * b).astype(BF16)
        _, fh, rf = _rms_fwd(_mm(s_ref[...], wd_ref[...]), gpost)
        df, dgpost = _rms_bwd(fh, rf, gpost, dx2)
        df_ref[...] = df.astype(BF16)
        ds_ref[...] = _mm_nt(df_ref[...], wd_ref[...])
        for c in range(0, D_FF, FFN_SLAB):
            a, b = ab_ref[:, c:c + FFN_SLAB], ab_ref[:, D_FF + c:D_FF + c + FFN_SLAB]
            ds = ds_ref[:, c:c + FFN_SLAB]
            sg = _sigmoid(a)
            dab_ref[:, c:c + FFN_SLAB] = (ds * b * (sg * (1.0 + a * (1.0 - sg)))).astype(BF16)
            dab_ref[:, D_FF + c:D_FF + c + FFN_SLAB] = (ds * (a * sg)).astype(BF16)
        dx, dgpre = _rms_bwd(xh, rx, gpre, _mm(dab_ref[...], wgu_ref[...]))
        dx1_ref[...] = dx2 + dx
        _acc(dgpre_ref, first, dgpre)
        _acc(dgpost_ref, first, dgpost)

    row = lambda w: pl.BlockSpec((tm, w), lambda i: (i, 0))
    vec = pl.BlockSpec((1, D), lambda i: (0, 0))
    return pl.pallas_call(
        body, name="ffn_bwd", grid=(t // tm,),
        in_specs=[row(D), row(D), _whole(), _whole(), _whole(), _whole()],
        out_specs=[row(D), row(D), row(2 * D_FF), row(D_FF), row(D), vec, vec],
        out_shape=[jax.ShapeDtypeStruct((t, D), F32), jax.ShapeDtypeStruct((t, D), BF16),
                   jax.ShapeDtypeStruct((t, 2 * D_FF), BF16), jax.ShapeDtypeStruct((t, D_FF), BF16),
                   jax.ShapeDtypeStruct((t, D), BF16), jax.ShapeDtypeStruct((1, D), F32),
                   jax.ShapeDtypeStruct((1, D), F32)],
        scratch_shapes=[pltpu.VMEM((tm, 2 * D_FF), F32), pltpu.VMEM((tm, D_FF), F32)],
        compiler_params=_cp(("arbitrary",)),
    )(x1, dx2, gpre, wgu, wd, gpost)


def atb(a, b, tk):
    t, k = a.shape
    n = b.shape[1]
    tt = min(TT, t)
    tk = min(tk, k)
    steps = t // tt

    def body(a_ref, b_ref, o_ref, acc_ref):
        i = pl.program_id(1)
        _acc(acc_ref, i == 0, _mm_tn(a_ref[...], b_ref[...]))

        @pl.when(i == steps - 1)
        def _():
            o_ref[...] = acc_ref[...].astype(BF16)

    return pl.pallas_call(
        body, name="atb", grid=(k // tk, steps),
        in_specs=[pl.BlockSpec((tt, tk), lambda j, i: (i, j)), pl.BlockSpec((tt, n), lambda j, i: (i, 0))],
        out_specs=pl.BlockSpec((tk, n), lambda j, i: (j, 0)),
        out_shape=jax.ShapeDtypeStruct((k, n), BF16),
        scratch_shapes=[pltpu.VMEM((tk, n), F32)],
        compiler_params=_cp(("parallel", "arbitrary")),
    )(a, b)


def mix_bwd(dx1, z, ya, gm, lng, lnb, wsp, wspt, bias, cw, gout, wout, gpost):
    t = dx1.shape[0]
    tm = min(TM, t)
    ng = SG_W // GROUP

    def body(dx1_ref, zcv_ref, halo_ref, zsg_ref, ya_ref, gm_ref, lng_ref, lnb_ref, wsp_ref, wspt_ref, bias_ref,
             cw_ref, gout_ref, wout_ref, gpost_ref,
             dya_ref, dyc_ref, dzsg_ref, mix_ref, do_ref, dgpost_ref, dgout_ref, dlng_ref, dlnb_ref, dwsp_ref,
             dbias_ref, wc_ref, wct_ref, mixed_ref, dv_ref):
        i = pl.program_id(0)
        first = i == 0
        gm = gm_ref[...]
        for g in range(ng):
            wc_ref[g] = _tril_bf16(wsp_ref, g)
            wct_ref[g] = jnp.where(
                lax.broadcasted_iota(jnp.int32, (CHUNK, CHUNK), 0) <= lax.broadcasted_iota(jnp.int32, (CHUNK, CHUNK), 1),
                wspt_ref[g], 0.0).astype(BF16)
        zsg = zsg_ref[...]
        lng = lng_ref[...]
        u, v, vh, r, th = _sgu_fwd(zsg, gm, lng, lnb_ref[...], wc_ref, bias_ref[...], mixed_ref)
        mixed = mixed_ref[...]
        yb = u * mixed
        yc, _, _, _, _ = _conv_fwd(zcv_ref[...], halo_ref[...], first, cw_ref[...])
        gout, gpost = gout_ref[...], gpost_ref[...]
        ga, gb_, gc_ = gout[:, :512], gout[:, 512:768], gout[:, 768:]
        na, yah, ra = _rms_fwd(ya_ref[...], ga)
        nb, ybh, rb = _rms_fwd(yb, gb_)
        nc, ych, rc = _rms_fwd(yc, gc_)
        mix = jnp.concatenate([na, nb, nc], axis=1).astype(BF16)
        _, oh, ro = _rms_fwd(_mm(mix, wout_ref[...]), gpost)
        do, dgpost = _rms_bwd(oh, ro, gpost, dx1_ref[...])
        dob = do.astype(BF16)
        dmix = _mm_nt(dob, wout_ref[...])
        dya, dga = _rms_bwd(yah, ra, ga, dmix[:, :512])
        dyb, dgb = _rms_bwd(ybh, rb, gb_, dmix[:, 512:768])
        dyc, dgc = _rms_bwd(ych, rc, gc_, dmix[:, 768:])
        dya_ref[...] = dya
        dyc_ref[...] = dyc
        mix_ref[...] = mix
        do_ref[...] = dob
        _acc(dgpost_ref, first, dgpost)
        _acc(dgout_ref, first, jnp.concatenate([dga, dgb, dgc], axis=1))
        du = dyb * mixed
        dmixed = dyb * u
        lane = lax.broadcasted_iota(jnp.int32, (CHUNK, SG_W), 1)
        row = lax.broadcasted_iota(jnp.int32, (CHUNK, CHUNK), 0)
        col = lax.broadcasted_iota(jnp.int32, (CHUNK, CHUNK), 1)
        dbias = jnp.zeros((CHUNK, SG_W), F32)
        dw = [jnp.zeros((CHUNK, CHUNK), F32) for _ in range(ng)]
        for c in range(tm // CHUNK):
            rows = slice(c * CHUNK, (c + 1) * CHUNK)
            dm = dmixed[rows]
            dbias = dbias + dm
            dmb = dm.astype(BF16)
            vb = v[rows].astype(BF16)
            dvc = jnp.zeros((CHUNK, SG_W), F32)
            for g in range(ng):
                in_g = lane // GROUP == g
                dvc = dvc + jnp.where(in_g, _mm(wct_ref[g], dmb), 0.0)
                dw[g] = dw[g] + _mm_nt(jnp.where(in_g, dmb, jnp.zeros_like(dmb)), vb)
            dv_ref[rows, :] = dvc
        for g in range(ng):
            dwg = jnp.where(col <= row, dw[g], 0.0)

            @pl.when(first)
            def _():
                dwsp_ref[g] = dwg

            @pl.when(jnp.logical_not(first))
            def _():
                dwsp_ref[g] += dwg
        _acc(dbias_ref, first, _gmean(dbias, gm) * GROUP)
        dv = dv_ref[...]
        _acc(dlng_ref, first, jnp.sum(dv * vh, axis=0, keepdims=True))
        _acc(dlnb_ref, first, jnp.sum(dv, axis=0, keepdims=True))
        dvh = dv * lng
        dv0 = r * (dvh - _gmean(dvh, gm) - vh * _gmean(dvh * vh, gm))
        dzsg_ref[...] = (jnp.concatenate([du, dv0], axis=1) * _gelu_grad(zsg, th)).astype(BF16)

    hb = tm // 8
    row_ = lambda w: pl.BlockSpec((tm, w), lambda i: (i, 0))
    vec = lambda w: pl.BlockSpec((1, w), lambda i: (0, 0))
    return pl.pallas_call(
        body, name="mix_bwd", grid=(t // tm,),
        in_specs=[row_(D),
                  pl.BlockSpec((tm, 768), lambda i: (i, 0)),
                  pl.BlockSpec((8, 768), lambda i: (jnp.maximum(i * hb - 1, 0), 0)),
                  pl.BlockSpec((tm, 512), lambda i: (i, 3)),
                  row_(512),
                  _whole(), _whole(), _whole(), _whole(), _whole(), _whole(), _whole(), _whole(), _whole(), _whole()],
        out_specs=[row_(512), row_(CV_W), row_(512), row_(D), row_(D), vec(D), vec(D), vec(SG_W), vec(SG_W),
                   pl.BlockSpec((ng, CHUNK, CHUNK), lambda i: (0, 0, 0)),
                   pl.BlockSpec((CHUNK, SG_W), lambda i: (0, 0))],
        out_shape=[jax.ShapeDtypeStruct((t, 512), F32), jax.ShapeDtypeStruct((t, CV_W), F32),
                   jax.ShapeDtypeStruct((t, 512), BF16), jax.ShapeDtypeStruct((t, D), BF16),
                   jax.ShapeDtypeStruct((t, D), BF16), jax.ShapeDtypeStruct((1, D), F32),
                   jax.ShapeDtypeStruct((1, D), F32), jax.ShapeDtypeStruct((1, SG_W), F32),
                   jax.ShapeDtypeStruct((1, SG_W), F32), jax.ShapeDtypeStruct((ng, CHUNK, CHUNK), F32),
                   jax.ShapeDtypeStruct((CHUNK, SG_W), F32)],
        scratch_shapes=[pltpu.VMEM((ng, CHUNK, CHUNK), BF16), pltpu.VMEM((ng, CHUNK, CHUNK), BF16),
                        pltpu.VMEM((tm, SG_W), F32), pltpu.VMEM((tm, SG_W), F32)],
        compiler_params=_cp(("arbitrary",)),
    )(dx1, z, z, z, ya, gm, lng, lnb, wsp, wspt, bias, cw, gout, wout, gpost)


def conv_bwd(dyc, z, cw):
    t = dyc.shape[0]
    tm = min(TM, t)
    hb = tm // 8
    last_blk = t // 8 - 1

    def body(dyc_ref, dyct_ref, zcv_ref, head_ref, tail_ref, cw_ref, dz_ref, dcw_ref):
        i = pl.program_id(0)
        first = i == 0
        last = i == pl.num_programs(0) - 1
        cw = cw_ref[...]
        zcv = zcv_ref[...]
        gb, gc, hh = zcv[:, :CV_W], zcv[:, CV_W:2 * CV_W], zcv[:, 2 * CV_W:]
        _, conv, y, y1, y2 = _conv_fwd(zcv, head_ref[...], first, cw)
        dyc = dyc_ref[...]
        dconv = dyc * gb
        tail = jnp.where(last, 0.0, dyct_ref[...] * tail_ref[:, :CV_W])
        d1 = _shift_up(dconv, 1, tail)
        d2 = _shift_up(dconv, 2, tail)
        dy = dconv * cw[2:3, :] + d1 * cw[1:2, :] + d2 * cw[0:1, :]
        dz_ref[...] = jnp.concatenate([dyc * conv, dy * hh, dy * gc], axis=1).astype(BF16)
        tap = lax.broadcasted_iota(jnp.int32, (8, CV_W), 0)
        dcw = jnp.where(tap == 0, jnp.sum(dconv * y2, axis=0, keepdims=True),
                        jnp.where(tap == 1, jnp.sum(dconv * y1, axis=0, keepdims=True),
                                  jnp.where(tap == 2, jnp.sum(dconv * y, axis=0, keepdims=True), 0.0)))
        _acc(dcw_ref, first, dcw)

    return pl.pallas_call(
        body, name="conv_bwd", grid=(t // tm,),
        in_specs=[pl.BlockSpec((tm, CV_W), lambda i: (i, 0)),
                  pl.BlockSpec((8, CV_W), lambda i: (jnp.minimum((i + 1) * hb, last_blk), 0)),
                  pl.BlockSpec((tm, 768), lambda i: (i, 0)),
                  pl.BlockSpec((8, 768), lambda i: (jnp.maximum(i * hb - 1, 0), 0)),
                  pl.BlockSpec((8, 768), lambda i: (jnp.minimum((i + 1) * hb, last_blk), 0)),
                  _whole()],
        out_specs=[pl.BlockSpec((tm, 768), lambda i: (i, 0)), pl.BlockSpec((8, CV_W), lambda i: (0, 0))],
        out_shape=[jax.ShapeDtypeStruct((t, 768), BF16), jax.ShapeDtypeStruct((8, CV_W), F32)],
        compiler_params=_cp(("arbitrary",)),
    )(dyc, dyc, z, z, z, cw)


def attn_bwd(q, k, v, o, lse, do, carried=()):
    t = q.shape[0]
    tq = min(TQ, t)
    nq = t // tq
    last_pair = HEADS // 2 - 1
    n = len(carried)

    def body(*refs):
        j = pl.program_id(1)
        q_ref, k_ref, v_ref, o_ref, lse_ref, do_ref = refs[:6]
        dq_ref, dk_ref, dv_ref = refs[6 + n:9 + n]
        sems = refs[9 + 2 * n:]
        stages = [_exchange_steps(refs[6 + a], refs[9 + n + a], carried[a][1], *sems[3 * a:3 * a + 3]) for a in range(n)]
        if n:
            pl.when((pl.program_id(0) == 0) & (j == 0))(_each(stages, 0))

        @pl.when(j == 0)
        def _():
            dq_ref[...] = jnp.zeros_like(dq_ref)

        row = lax.broadcasted_iota(jnp.int32, (tq, tq), 0)
        col = lax.broadcasted_iota(jnp.int32, (tq, tq), 1)
        vlane = lax.broadcasted_iota(jnp.int32, (tq, 2 * V_DIM), 1)
        head_lanes = [slice(h * HEAD_PAD, (h + 1) * HEAD_PAD) for h in range(2)]

        def step(i, carry, masked):
            start = pl.multiple_of(i * tq, tq)
            do_blk = do_ref[pl.ds(start, tq), :]
            o_blk = o_ref[pl.ds(start, tq), :]
            vb = v_ref[...]
            dks, dv_acc = [], carry[2]
            for h in range(2):
                lanes = head_lanes[h]
                qb = q_ref[pl.ds(start, tq), lanes]
                kb = k_ref[:, lanes]
                dob = jnp.where((vlane // V_DIM) == h, do_blk, 0.0)
                delta = jnp.sum(dob * o_blk, axis=-1, keepdims=True)
                lse2 = lse_ref[pl.ds(start, tq), lanes][:, 0:1] * LOG2E
                s = _mm_nt(qb, kb)
                if masked:
                    s = jnp.where(col <= row, s, NEG)
                p = jnp.exp2(s * SCALE_LOG2E - lse2)
                dob16 = dob.astype(BF16)
                dp = _mm_nt(dob16, vb)
                ds = (p * (dp - delta) * SCALE).astype(BF16)
                dv_acc = dv_acc + _mm_tn(p.astype(BF16), dob16)
                dks.append(carry[h] + _mm_tn(ds, qb))
                dq_ref[pl.ds(start, tq), lanes] += _mm(ds, kb)
            return dks[0], dks[1], dv_acc

        zero = jnp.zeros((tq, HEAD_PAD), F32)
        carry = step(j, (zero, zero, jnp.zeros((tq, 2 * V_DIM), F32)), True)
        dk0, dk1, dv_acc = lax.fori_loop(j + 1, nq, lambda i, c: step(i, c, False), carry)
        dk_ref[:, head_lanes[0]] = dk0
        dk_ref[:, head_lanes[1]] = dk1
        dv_ref[...] = dv_acc
        if n:
            pl.when((pl.program_id(0) == last_pair) & (j == nq - 1))(_each(stages, 1))

    hbm = pl.BlockSpec(memory_space=pl.ANY)
    return pl.pallas_call(
        body, name=f"attn_bwd_exchange{n}" if n else "attn_bwd", grid=(HEADS // 2, nq),
        in_specs=[pl.BlockSpec((t, 2 * HEAD_PAD), lambda p, j: (0, p)),
                  pl.BlockSpec((tq, 2 * HEAD_PAD), lambda p, j: (j, p)),
                  pl.BlockSpec((tq, 2 * V_DIM), lambda p, j: (j, p)),
                  pl.BlockSpec((t, 2 * V_DIM), lambda p, j: (0, p)),
                  pl.BlockSpec((t, 2 * HEAD_PAD), lambda p, j: (0, p)),
                  pl.BlockSpec((t, 2 * V_DIM), lambda p, j: (0, p))] + [hbm] * n,
        out_specs=[pl.BlockSpec((t, 2 * HEAD_PAD), lambda p, j: (0, p)),
                   pl.BlockSpec((tq, 2 * HEAD_PAD), lambda p, j: (j, p)),
                   pl.BlockSpec((tq, 2 * V_DIM), lambda p, j: (j, p))] + [hbm] * n,
        out_shape=[jax.ShapeDtypeStruct((t, HEADS * HEAD_PAD), F32), jax.ShapeDtypeStruct((t, HEADS * HEAD_PAD), F32),
                   jax.ShapeDtypeStruct((t, HEADS * V_DIM), F32)]
        + [jax.ShapeDtypeStruct(src.shape if scatter else (N_DEV,) + src.shape, src.dtype) for src, scatter in carried],
        scratch_shapes=_comm_sems() * n,
        compiler_params=_cp(("arbitrary", "arbitrary") if n else ("parallel", "arbitrary")),
    )(q, k, v, o, lse, do, *[src for src, _ in carried])


def mla_proj_bwd(dq, dk, dv, z, ca, sb, sc, gq, gkv, wuq, wukv):
    t = z.shape[0]
    tm = min(TM, t)

    def body(dq_ref, dk_ref, dv_ref, z_ref, ca_ref, sb_ref, sc_ref, gq_ref, gkv_ref, wuq_ref, wukv_ref,
             dz_ref, cq_ref, ckv_ref, dqp_ref, dkvp_ref, dgq_ref, dgkv_ref):
        first = pl.program_id(0) == 0
        z = z_ref[...]
        ca, sb, sc = ca_ref[...], sb_ref[...], sc_ref[...]
        gq, gkv = gq_ref[...], gkv_ref[...]
        cq, cqh, rq = _rms_fwd(z[:, :Q_RANK], gq)
        ckv, ckvh, rkv = _rms_fwd(z[:, Q_RANK:Q_RANK + KV_RANK], gkv)
        lane = lax.broadcasted_iota(jnp.int32, (tm, HEAD_PAD), 1)
        dkr = jnp.zeros((tm, HEAD_PAD), F32)
        for h in range(HEADS):
            lanes = slice(h * HEAD_PAD, (h + 1) * HEAD_PAD)
            dqp_ref[:, lanes] = _rope_t(dq_ref[:, lanes], ca, sb, sc).astype(BF16)
            dkh = dk_ref[:, lanes]
            dkr = dkr + dkh
            dkvp_ref[:, lanes] = jnp.where(lane < NOPE, dkh, 0.0).astype(BF16)
        dkvp_ref[:, HEADS * HEAD_PAD:] = dv_ref[...].astype(BF16)
        dkr = pltpu.roll(_rope_t(jnp.where(lane >= NOPE, dkr, 0.0), ca, sb, sc), HEAD_PAD - NOPE, 1)
        dkr = jnp.where(lane < ROPE, dkr, 0.0)
        dcq = _mm(dqp_ref[...], wuq_ref[...])
        dckv = _mm(dkvp_ref[...], wukv_ref[...])
        dzq, dgq = _rms_bwd(cqh, rq, gq, dcq)
        dzkv, dgkv = _rms_bwd(ckvh, rkv, gkv, dckv)
        dz_ref[...] = jnp.concatenate([dzq, dzkv, dkr], axis=1).astype(BF16)
        cq_ref[...] = cq.astype(BF16)
        ckv_ref[...] = ckv.astype(BF16)
        _acc(dgq_ref, first, dgq)
        _acc(dgkv_ref, first, dgkv)

    row = lambda w: pl.BlockSpec((tm, w), lambda i: (i, 0))
    vec = lambda w: pl.BlockSpec((1, w), lambda i: (0, 0))
    return pl.pallas_call(
        body, name="mla_proj_bwd", grid=(t // tm,),
        in_specs=[row(1024), row(1024), row(512), pl.BlockSpec((tm, 768), lambda i: (i, 1)),
                  row(HEAD_PAD), row(HEAD_PAD), row(HEAD_PAD), _whole(), _whole(), _whole(), _whole()],
        out_specs=[row(768), row(Q_RANK), row(KV_RANK), row(1024), row(1536), vec(Q_RANK), vec(KV_RANK)],
        out_shape=[jax.ShapeDtypeStruct((t, 768), BF16), jax.ShapeDtypeStruct((t, Q_RANK), BF16),
                   jax.ShapeDtypeStruct((t, KV_RANK), BF16), jax.ShapeDtypeStruct((t, 1024), BF16),
                   jax.ShapeDtypeStruct((t, 1536), BF16), jax.ShapeDtypeStruct((1, Q_RANK), F32),
                   jax.ShapeDtypeStruct((1, KV_RANK), F32)],
        compiler_params=_cp(("arbitrary",)),
    )(dq, dk, dv, z, ca, sb, sc, gq, gkv, wuq, wukv)


def pre_in_bwd(x, dx1, dzcv, dzmla, dzsg, g, w):
    t = x.shape[0]
    tm = min(TM, t)

    def body(x_ref, dx1_ref, dzcv_ref, dzmla_ref, dzsg_ref, g_ref, w_ref, dx_ref, h_ref, dz_ref, dg_ref):
        g = g_ref[...]
        h, xh, r = _rms_fwd(x_ref[...], g)
        dz = jnp.concatenate([dzcv_ref[...], dzmla_ref[...], dzsg_ref[...]], axis=1)
        dx, dg = _rms_bwd(xh, r, g, _mm(dz, w_ref[...]))
        dx_ref[...] = dx1_ref[...] + dx
        h_ref[...] = h.astype(BF16)
        dz_ref[...] = dz
        _acc(dg_ref, pl.program_id(0) == 0, dg)

    row = lambda w_: pl.BlockSpec((tm, w_), lambda i: (i, 0))
    return pl.pallas_call(
        body, name="pre_in_bwd", grid=(t // tm,),
        in_specs=[row(D), row(D), row(768), row(768), row(512), _whole(), _whole()],
        out_specs=[row(D), row(D), row(Z_W), pl.BlockSpec((1, D), lambda i: (0, 0))],
        out_shape=[jax.ShapeDtypeStruct((t, D), F32), jax.ShapeDtypeStruct((t, D), BF16),
                   jax.ShapeDtypeStruct((t, Z_W), BF16), jax.ShapeDtypeStruct((1, D), F32)],
        compiler_params=_cp(("arbitrary",)),
    )(x, dx1, dzcv, dzmla, dzsg, g, w)


MESH = pl.DeviceIdType.MESH


def _place():
    return lax.axis_index("x"), lax.axis_index("y"), lax.axis_index("c")


def _comm_sems():
    return [pltpu.SemaphoreType.DMA((7,)), pltpu.SemaphoreType.DMA((7,)), pltpu.SemaphoreType.DMA]


def _gather_steps(x_ref, out_ref, send_sems, recv_sems, local_sem):
    x, y, c = _place()
    me, sibling = (x, y, c), (x, y, 1 - c)
    chips = [(1 - x, y), (x, 1 - y), (1 - x, 1 - y)]

    def slot(px, py, pc):
        return out_ref.at[4 * px + 2 * py + pc]

    def copy(k, blk, to, src=None):
        return pltpu.make_async_remote_copy(
            src_ref=slot(*blk) if src is None else src, dst_ref=slot(*blk),
            send_sem=send_sems.at[k], recv_sem=recv_sems.at[k], device_id=to, device_id_type=MESH)

    mine = pltpu.make_async_copy(x_ref, slot(*me), local_sem)
    first = [copy(0, me, sibling, src=x_ref)] + [copy(1 + j, me, (*chip, c), src=x_ref) for j, chip in enumerate(chips)]
    passed = [copy(4 + j, (*chip, c), sibling) for j, chip in enumerate(chips)]

    def start():
        mine.start()
        for cp in first:
            cp.start()

    def forward():
        for j, chip in enumerate(chips):
            copy(1 + j, (*chip, c), me).wait_recv()
            passed[j].start()

    def finish():
        copy(0, sibling, me).wait_recv()
        for j, chip in enumerate(chips):
            copy(4 + j, (*chip, 1 - c), me).wait_recv()
        for cp in first + passed:
            cp.wait_send()
        mine.wait()

    return start, forward, finish


def _exchange_steps(src_ref, out_ref, scatter, send_sems, recv_sems, local_sem):
    x, y, c = _place()
    me = 4 * x + 2 * y + c
    own = pltpu.make_async_copy(src_ref.at[me] if scatter else src_ref, out_ref.at[me], local_sem)
    copies = []
    for k in range(1, N_DEV):
        px = 1 - x if k & 4 else x
        py = 1 - y if k & 2 else y
        pc = 1 - c if k & 1 else c
        copies.append(pltpu.make_async_remote_copy(
            src_ref=src_ref.at[4 * px + 2 * py + pc] if scatter else src_ref, dst_ref=out_ref.at[me],
            send_sem=send_sems.at[k - 1], recv_sem=recv_sems.at[k - 1], device_id=(px, py, pc), device_id_type=MESH))

    def start():
        own.start()
        for cp in copies:
            cp.start()

    def finish():
        for cp in copies:
            cp.wait_recv()
        for cp in copies:
            cp.wait_send()
        own.wait()

    return start, finish


def all_gather(block):
    def body(x_ref, out_ref, *sems):
        for stage in _gather_steps(x_ref, out_ref, *sems):
            stage()

    return pl.pallas_call(
        body, name="all_gather",
        in_specs=[pl.BlockSpec(memory_space=pl.ANY)],
        out_specs=pl.BlockSpec(memory_space=pl.ANY),
        out_shape=jax.ShapeDtypeStruct((N_DEV,) + block.shape, block.dtype),
        scratch_shapes=_comm_sems(),
    )(block)


def grad_exchange(chunks, small):
    def body(chunks_ref, small_ref, got_ref, gots_ref, *sems):
        start_c, finish_c = _exchange_steps(chunks_ref, got_ref, True, *sems[:3])
        start_s, finish_s = _exchange_steps(small_ref, gots_ref, False, *sems[3:])
        start_c()
        start_s()
        finish_c()
        finish_s()

    return pl.pallas_call(
        body, name="grad_exchange",
        in_specs=[pl.BlockSpec(memory_space=pl.ANY), pl.BlockSpec(memory_space=pl.ANY)],
        out_specs=[pl.BlockSpec(memory_space=pl.ANY), pl.BlockSpec(memory_space=pl.ANY)],
        out_shape=[jax.ShapeDtypeStruct(chunks.shape, chunks.dtype),
                   jax.ShapeDtypeStruct((N_DEV,) + small.shape, small.dtype)],
        scratch_shapes=_comm_sems() + _comm_sems(),
    )(chunks, small)


def _row_tile(r, cap):
    return max(d for d in range(16, cap + 1, 16) if r % d == 0)


def sum_adamw(parts, w, m, v, cap):
    r, c = w.shape
    tr = _row_tile(r, cap)
    c1 = 1.0 / (1.0 - ADAM_B1 ** ADAM_STEP)
    c2 = 1.0 / (1.0 - ADAM_B2 ** ADAM_STEP)

    def body(p_ref, w_ref, m_ref, v_ref, g_ref, d_ref, nm_ref, nv_ref):
        g = p_ref[0].astype(F32)
        for k in range(1, N_DEV):
            g = g + p_ref[k].astype(F32)
        m = ADAM_B1 * m_ref[...] + (1.0 - ADAM_B1) * g
        v = ADAM_B2 * v_ref[...] + (1.0 - ADAM_B2) * (g * g)
        g_ref[...] = g
        nm_ref[...] = m
        nv_ref[...] = v
        d_ref[...] = -ADAM_LR * ((m * c1) / (jnp.sqrt(v * c2) + ADAM_EPS) + ADAM_WD * w_ref[...])

    blk = pl.BlockSpec((tr, c), lambda i: (i, 0))
    out = jax.ShapeDtypeStruct((r, c), F32)
    return pl.pallas_call(
        body, name="sum_adamw", grid=(r // tr,),
        in_specs=[pl.BlockSpec((N_DEV, tr, c), lambda i: (0, i, 0)), blk, blk, blk],
        out_specs=[blk, blk, blk, blk], out_shape=[out, out, out, out],
        compiler_params=_cp(("parallel",)),
    )(parts, w, m, v)


PACK_W = 1024
MIX_PIECES = (("w_out", D // N_DEV, D, False), ("w_uq", HEADS * (NOPE + ROPE) // N_DEV, Q_RANK, True),
              ("w_ukv", HEADS * (NOPE + V_DIM) // N_DEV, KV_RANK, True), ("conv", 16, PACK_W, False),
              ("w_in", IN_W // N_DEV, D, True))
FFN_PIECES = (("w_gate", D_FF // N_DEV, D, True), ("w_up", D_FF // N_DEV, D, True), ("w_down", D_FF // N_DEV, D, False))
OFFSET = {}
for _pieces in (MIX_PIECES, FFN_PIECES):
    _off = 0
    for _name, _rows, _, _ in _pieces:
        OFFSET[_name] = _off
        _off += _rows + -_rows % 16
assert all(o % 16 == 0 for o in OFFSET.values())
CONV_BITS = 3 * (CV_W // N_DEV) * 2


def _to_pack(shards, dtype, pieces, conv=None):
    nl = shards["w_in"].shape[0]
    parts = []
    for name, rows, cols, transposed in pieces:
        if name == "conv":
            if conv is None:
                a = jnp.zeros((nl, rows, PACK_W), dtype)
            else:
                bits = lax.bitcast_convert_type(conv.astype(F32), BF16).reshape(nl, CONV_BITS)
                a = jnp.pad(bits, ((0, 0), (0, rows * PACK_W - CONV_BITS))).reshape(nl, rows, PACK_W)
        else:
            a = shards[name].astype(dtype)
            a = jnp.swapaxes(a, 1, 2) if transposed else a
            a = jnp.pad(a, ((0, 0), (0, -rows % 16), (0, PACK_W - cols)))
        parts.append(a)
    return jnp.concatenate(parts, axis=1)


def _from_pack(pack, pieces):
    out = {}
    for name, rows, cols, transposed in pieces:
        if name != "conv":
            a = pack[:, OFFSET[name]:OFFSET[name] + rows, :cols]
            out[name] = jnp.swapaxes(a, 1, 2) if transposed else a
    return out


def _rows_of(g, pieces, name):
    _, n, cols, _ = next(p for p in pieces if p[0] == name)
    return g[:, OFFSET[name]:OFFSET[name] + n, :cols]


def _ffn_weights(g):
    return dict(w_gu=jnp.concatenate([_rows_of(g, FFN_PIECES, "w_gate").reshape(D_FF, D),
                                      _rows_of(g, FFN_PIECES, "w_up").reshape(D_FF, D)], axis=0),
                w_down=_rows_of(g, FFN_PIECES, "w_down").reshape(D_FF, D))


def _mix_weights(g):
    def rows(name):
        return _rows_of(g, MIX_PIECES, name)

    w_in_t = rows("w_in").reshape(IN_W, D)
    w_in_p = jnp.concatenate([w_in_t[1184:], w_in_t[:672], jnp.zeros((96, D), BF16), w_in_t[672:1184]], axis=0)
    w_uq_p = jnp.pad(rows("w_uq"), ((0, 0), (0, HEAD_PAD - NOPE - ROPE), (0, 0))).reshape(HEADS * HEAD_PAD, Q_RANK)
    kv = rows("w_ukv")
    w_k = jnp.pad(kv[:, :NOPE], ((0, 0), (0, HEAD_PAD - NOPE), (0, 0))).reshape(HEADS * HEAD_PAD, KV_RANK)
    w_ukv_p = jnp.concatenate([w_k, kv[:, NOPE:].reshape(HEADS * V_DIM, KV_RANK)], axis=0)
    bits = rows("conv").reshape(N_DEV, -1)[:, :CONV_BITS].reshape(N_DEV, 3, CV_W // N_DEV, 2)
    conv_w = jnp.moveaxis(lax.bitcast_convert_type(bits, F32), 0, 1).reshape(3, CV_W)
    return dict(w_in=w_in_p, w_uq=w_uq_p, w_ukv=w_ukv_p, w_out=rows("w_out").reshape(D, D), conv_w=conv_w)


def _grad_chunks(full):
    if "w_in" in full:
        pieces = MIX_PIECES
        d_in = full["w_in"]
        d_in = jnp.concatenate([d_in[768:768 + 672], d_in[1536:], d_in[:768]], axis=0)
        d_uq = full["w_uq"].reshape(HEADS, HEAD_PAD, Q_RANK)[:, :NOPE + ROPE]
        d_k = full["w_ukv"][:HEADS * HEAD_PAD].reshape(HEADS, HEAD_PAD, KV_RANK)[:, :NOPE]
        d_v = full["w_ukv"][HEADS * HEAD_PAD:].reshape(HEADS, V_DIM, KV_RANK)
        mats = dict(w_in=d_in, w_uq=d_uq, w_ukv=jnp.concatenate([d_k, d_v], axis=1), w_out=full["w_out"])
    else:
        pieces = FFN_PIECES
        mats = dict(w_gate=full["w_gu"][:D_FF], w_up=full["w_gu"][D_FF:], w_down=full["w_down"])
    parts = []
    for name, rows, cols, _ in pieces:
        if name == "conv":
            parts.append(jnp.zeros((N_DEV, rows, PACK_W), BF16))
        else:
            parts.append(jnp.pad(mats[name].reshape(N_DEV, rows, cols), ((0, 0), (0, -rows % 16), (0, PACK_W - cols))))
    return jnp.concatenate(parts, axis=1)


SMALL = (("mix_pre_g", (D,)), ("mix_post_g", (D,)), ("ffn_pre_g", (D,)), ("ffn_post_g", (D,)), ("q_norm_g", (Q_RANK,)),
         ("kv_norm_g", (KV_RANK,)), ("sg_ln_g", (SG_W,)), ("sg_ln_b", (SG_W,)), ("w_sp", (4, CHUNK, CHUNK)),
         ("b_sp", (4, CHUNK)), ("out_norm_g", (D,)))
SMALL_ROWS = 576


def _pack_small(vals, nl):
    flat = jnp.concatenate([vals[name].reshape(nl, -1) for name, _ in SMALL] + [vals["conv_w"].reshape(nl, -1)], axis=1)
    return jnp.pad(flat, ((0, 0), (0, SMALL_ROWS * 128 - flat.shape[1]))).reshape(nl * SMALL_ROWS, 128)


def _unpack_small(pack, nl):
    flat = pack.reshape(nl, SMALL_ROWS * 128)
    out, off = {}, 0
    for name, shape in SMALL + (("conv_w", (3, CV_W)),):
        n = int(np.prod(shape))
        out[name] = flat[:, off:off + n].reshape((nl,) + shape)
        off += n
    return out


def _layer_fwd(x, lw, sp, tabs, consts, next_pack):
    ca, sb, sc = tabs
    z = pre_in_fwd(x, sp["mix_pre_g"], lw["w_in"])
    q, k, v = mla_proj_fwd(z, ca, sb, sc, sp["q_norm_g"], sp["kv_norm_g"], lw["w_uq"], lw["w_ukv"])
    ya, lse, ffn_gathered, *mix_gathered = attn_fwd(q, k, v, next_pack)
    lw.update(_ffn_weights(ffn_gathered))
    x1 = mix_fwd(x, z, ya, consts["gm"], sp["sg_ln_g"], sp["sg_ln_b"], sp["w_sp"], sp["bias"], lw["conv_w"],
                 sp["out_norm_g"], lw["w_out"], sp["mix_post_g"])
    x2 = ffn_fwd(x1, sp["ffn_pre_g"], lw["w_gu"], lw["w_down"], sp["ffn_post_g"])
    return x2, (x, z, q, k, v, ya, lse, x1), mix_gathered


def _layer_bwd(dx2, saved, lw, sp, tabs, consts, pending):
    ca, sb, sc = tabs
    x, z, q, k, v, ya, lse, x1 = saved
    dx1, h2, dab, s, df, d_ffn_pre, d_ffn_post = ffn_bwd(x1, dx2, sp["ffn_pre_g"], lw["w_gu"], lw["w_down"], sp["ffn_post_g"])
    ffn_chunks = _grad_chunks(dict(w_gu=atb(dab, h2, 1408), w_down=atb(s, df, 1408)))
    dya, dyc, dzsg, mix, do, d_mix_post, d_out_norm, d_lng, d_lnb, d_wsp, d_bias = mix_bwd(
        dx1, z, ya, consts["gm"], sp["sg_ln_g"], sp["sg_ln_b"], sp["w_sp"], sp["w_sp_t"], sp["bias"], lw["conv_w"],
        sp["out_norm_g"], lw["w_out"], sp["mix_post_g"])
    d_w_out = atb(mix, do, 1024)
    dzcv, d_cw = conv_bwd(dyc, z, lw["conv_w"])
    dq, dk, dv, *received = attn_bwd(q, k, v, ya, lse, dya, ((ffn_chunks, True),) + pending)
    dzmla, cq, ckv, dqp, dkvp, d_gq, d_gkv = mla_proj_bwd(dq, dk, dv, z, ca, sb, sc, sp["q_norm_g"], sp["kv_norm_g"],
                                                          lw["w_uq"], lw["w_ukv"])
    d_w_uq = atb(dqp, cq, 1024)
    d_w_ukv = atb(dkvp, ckv, 1536)
    dx, h1, dz, d_mix_pre = pre_in_bwd(x, dx1, dzcv, dzmla, dzsg, sp["mix_pre_g"], lw["w_in"])
    d_w_in = atb(dz, h1, 2048)
    mix_chunks = _grad_chunks(dict(w_in=d_w_in, w_uq=d_w_uq, w_ukv=d_w_ukv, w_out=d_w_out))
    d_bsp = d_bias[:, ::GROUP].T
    small = dict(mix_pre_g=d_mix_pre[0], mix_post_g=d_mix_post[0], ffn_pre_g=d_ffn_pre[0], ffn_post_g=d_ffn_post[0],
                 q_norm_g=d_gq[0], kv_norm_g=d_gkv[0], sg_ln_g=d_lng[0], sg_ln_b=d_lnb[0], w_sp=d_wsp, b_sp=d_bsp,
                 out_norm_g=d_out_norm[0], conv_w=d_cw[:3])
    small_pack = _pack_small({name: a[None] for name, a in small.items()}, 1)
    return dx, ((mix_chunks, True), (small_pack, False)), received


def kernel(x, positions, mix_pre_g, mix_post_g, ffn_pre_g, ffn_post_g, w_in, q_norm_g, w_uq, kv_norm_g, w_ukv, sg_ln_g, sg_ln_b, w_sp, b_sp, conv_w, out_norm_g, w_out, w_gate, w_up, w_down, loss_target, m_mix_pre_g, m_mix_post_g, m_ffn_pre_g, m_ffn_post_g, m_w_in, m_q_norm_g, m_w_uq, m_kv_norm_g, m_w_ukv, m_sg_ln_g, m_sg_ln_b, m_w_sp, m_b_sp, m_conv_w, m_out_norm_g, m_w_out, m_w_gate, m_w_up, m_w_down, v_mix_pre_g, v_mix_post_g, v_ffn_pre_g, v_ffn_post_g, v_w_in, v_q_norm_g, v_w_uq, v_kv_norm_g, v_w_ukv, v_sg_ln_g, v_sg_ln_b, v_w_sp, v_b_sp, v_conv_w, v_out_norm_g, v_w_out, v_w_gate, v_w_up, v_w_down):
    nl = w_in.shape[0]
    t = x.shape[1]
    w = dict(mix_pre_g=mix_pre_g, mix_post_g=mix_post_g, ffn_pre_g=ffn_pre_g, ffn_post_g=ffn_post_g, w_in=w_in,
             q_norm_g=q_norm_g, w_uq=w_uq, kv_norm_g=kv_norm_g, w_ukv=w_ukv, sg_ln_g=sg_ln_g, sg_ln_b=sg_ln_b, w_sp=w_sp,
             b_sp=b_sp, conv_w=conv_w, out_norm_g=out_norm_g, w_out=w_out, w_gate=w_gate, w_up=w_up, w_down=w_down)
    m = dict(mix_pre_g=m_mix_pre_g, mix_post_g=m_mix_post_g, ffn_pre_g=m_ffn_pre_g, ffn_post_g=m_ffn_post_g, w_in=m_w_in,
             q_norm_g=m_q_norm_g, w_uq=m_w_uq, kv_norm_g=m_kv_norm_g, w_ukv=m_w_ukv, sg_ln_g=m_sg_ln_g, sg_ln_b=m_sg_ln_b,
             w_sp=m_w_sp, b_sp=m_b_sp, conv_w=m_conv_w, out_norm_g=m_out_norm_g, w_out=m_w_out, w_gate=m_w_gate,
             w_up=m_w_up, w_down=m_w_down)
    v = dict(mix_pre_g=v_mix_pre_g, mix_post_g=v_mix_post_g, ffn_pre_g=v_ffn_pre_g, ffn_post_g=v_ffn_post_g, w_in=v_w_in,
             q_norm_g=v_q_norm_g, w_uq=v_w_uq, kv_norm_g=v_kv_norm_g, w_ukv=v_w_ukv, sg_ln_g=v_sg_ln_g, sg_ln_b=v_sg_ln_b,
             w_sp=v_w_sp, b_sp=v_b_sp, conv_w=v_conv_w, out_norm_g=v_out_norm_g, w_out=v_w_out, w_gate=v_w_gate,
             w_up=v_w_up, w_down=v_w_down)

    mix_pack = _to_pack(w, BF16, MIX_PIECES, conv=w["conv_w"])
    ffn_pack = _to_pack(w, BF16, FFN_PIECES)
    consts = dict(gm=jnp.asarray(np.kron(np.eye(SG_W // GROUP), np.full((GROUP, GROUP), 1.0 / GROUP)), F32))
    smalls = []
    for l in range(nl):
        sp = {name: w[name][l].reshape(1, -1) for name, shape in SMALL if len(shape) == 1}
        sp["w_sp"] = w["w_sp"][l]
        sp["w_sp_t"] = jnp.swapaxes(w["w_sp"][l], 1, 2)
        sp["bias"] = jnp.repeat(w["b_sp"][l].T, GROUP, axis=1)
        smalls.append(sp)
    inv_freq = 1.0 / (ROPE_THETA ** (jnp.arange(0, ROPE // 2, dtype=F32) / (ROPE // 2)))
    inv = jnp.zeros((1, HEAD_PAD), F32).at[0, NOPE:NOPE + ROPE].set(jnp.concatenate([inv_freq, inv_freq]))
    tabs = rope_tables(positions.reshape(t, 1).astype(F32), inv)

    h = x[0]
    saved, layers = [], []
    mix_gathered = [all_gather(mix_pack[0])]
    for l in range(nl):
        layers.append(_mix_weights(mix_gathered[0]))
        carried = (ffn_pack[l],) + ((mix_pack[l + 1],) if l + 1 < nl else ())
        h, s, mix_gathered = _layer_fwd(h, layers[l], smalls[l], tabs, consts, carried)
        saved.append(s)
    sq, dh = loss_head(h, loss_target[0])
    loss = lax.psum(0.5 * sq[0, 0] / D, ("x", "y", "c"))

    got_ffn, got_mix, got_small = [None] * nl, [None] * nl, [None] * nl
    pending = ()
    for l in reversed(range(nl)):
        dh, new_pending, received = _layer_bwd(dh, saved[l], layers[l], smalls[l], tabs, consts, pending)
        got_ffn[l] = received[0]
        if pending:
            got_mix[l + 1], got_small[l + 1] = received[1:]
        pending = new_pending
    got_mix[0], got_small[0] = grad_exchange(pending[0][0], pending[1][0])
    got_small = jnp.concatenate(got_small, axis=1)

    me = 4 * lax.axis_index("x") + 2 * lax.axis_index("y") + lax.axis_index("c")
    g_big, d_big, m_big, v_big = {}, {}, {}, {}
    for pieces, got in ((MIX_PIECES, got_mix), (FFN_PIECES, got_ffn)):
        w_f32, m_f32, v_f32 = [_to_pack(d, F32, pieces) for d in (w, m, v)]
        per_layer = [sum_adamw(got[l], w_f32[l], m_f32[l], v_f32[l], 352) for l in range(nl)]
        for i, out in enumerate((g_big, d_big, m_big, v_big)):
            out.update(_from_pack(jnp.stack([per_layer[l][i] for l in range(nl)]), pieces))

    def full_conv(a):
        return lax.dynamic_update_slice(jnp.zeros((nl, 3, CV_W), F32), a, (0, 0, me * (CV_W // N_DEV)))

    def small_pack(d):
        return _pack_small({**{name: d[name] for name, _ in SMALL}, "conv_w": full_conv(d["conv_w"])}, nl)

    g_small, d_small, m_small, v_small = [_unpack_small(p, nl) for p in
                                          sum_adamw(got_small, small_pack(w), small_pack(m), small_pack(v), 1152)]
    outs = []
    for big, small in ((g_big, g_small), (d_big, d_small), (m_big, m_small), (v_big, v_small)):
        for name in w:
            if name == "conv_w":
                outs.append(lax.dynamic_slice(small[name], (0, 0, me * (CV_W // N_DEV)), (nl, 3, CV_W // N_DEV)))
            elif name in small:
                outs.append(small[name])
            else:
                outs.append(big[name])
    return (loss, dh[None], *outs)
```

```python
import functools

import jax
import jax.numpy as jnp
import numpy as np
from jax import lax
from jax.experimental import pallas as pl
from jax.experimental.pallas import tpu as pltpu

F32 = jnp.float32
BF16 = jnp.bfloat16

D = 1024
Q_RANK = 384
KV_RANK = 256
ROPE = 32
HEADS = 8
NOPE = 64
V_DIM = 64
HEAD_PAD = 128
SG_W = 256
CV_W = 256
CHUNK = 128
GROUP = 64
D_FF = 2816
IN_W = 1952
Z_W = 2048
Z_CV, Z_MLA, Z_SG = 0, 768, 1536
EPS = 1e-6
ROPE_THETA = 10000.0
SCALE = (NOPE + ROPE) ** -0.5
LOG2E = 1.4426950408889634
SCALE_LOG2E = SCALE * LOG2E
NEG = -1e30
N_DEV = 8

ADAM_LR, ADAM_B1, ADAM_B2, ADAM_EPS, ADAM_WD, ADAM_STEP = 0.001, 0.9, 0.999, 1e-08, 0.01, 10

VMEM_LIMIT = 56 * 1024 * 1024

TM = 512
TM_FFN = 256
FFN_SLAB = 256
TQ = 512
TT = 2048


def _cp(sem, vmem=VMEM_LIMIT):
    return pltpu.CompilerParams(dimension_semantics=sem, vmem_limit_bytes=vmem)


def _whole():
    return pl.BlockSpec(memory_space=pltpu.VMEM)


def _mm(a, b):
    return jnp.dot(a, b, preferred_element_type=F32)


def _mm_nt(a, b):
    return lax.dot_general(a, b, (((1,), (1,)), ((), ())), preferred_element_type=F32)


def _mm_tn(a, b):
    return lax.dot_general(a, b, (((0,), (0,)), ((), ())), preferred_element_type=F32)


def _rms_fwd(x, g):
    r = lax.rsqrt(jnp.mean(x * x, axis=-1, keepdims=True) + EPS)
    xh = x * r
    return xh * g, xh, r


def _rms_bwd(xh, r, g, dy):
    dxh = dy * g
    dx = r * (dxh - xh * jnp.mean(dxh * xh, axis=-1, keepdims=True))
    dg = jnp.sum(dy * xh, axis=0, keepdims=True)
    return dx, dg


def _gmean(v, gm):
    hi = v.astype(BF16)
    lo = (v - hi.astype(F32)).astype(BF16)
    return _mm(hi, gm) + _mm(lo, gm)


def _gelu(x):
    c = np.float32(np.sqrt(2.0 / np.pi))
    u = c * (x + 0.044715 * x * x * x)
    t = jnp.tanh(u)
    return 0.5 * x * (1.0 + t), t


def _gelu_grad(x, t):
    c = np.float32(np.sqrt(2.0 / np.pi))
    return 0.5 * (1.0 + t) + 0.5 * x * (1.0 - t * t) * c * (1.0 + 3.0 * 0.044715 * x * x)


def _rope(t, ca, sb, sc):
    return t * ca + pltpu.roll(t, HEAD_PAD - 16, 1) * sb + pltpu.roll(t, 16, 1) * sc


def _rope_t(dt, ca, sb, sc):
    return dt * ca + pltpu.roll(dt * sb, 16, 1) + pltpu.roll(dt * sc, HEAD_PAD - 16, 1)


def _shift_down(y, k, head):
    n = y.shape[0]
    out = pltpu.roll(y, k, 0)
    row = lax.broadcasted_iota(jnp.int32, y.shape, 0)
    for j in range(k):
        out = jnp.where(row == j, head[8 - k + j:8 - k + j + 1, :], out)
    return out


def _shift_up(y, k, tail):
    n = y.shape[0]
    out = pltpu.roll(y, n - k, 0)
    row = lax.broadcasted_iota(jnp.int32, y.shape, 0)
    for j in range(k):
        out = jnp.where(row == n - k + j, tail[j:j + 1, :], out)
    return out


def rope_tables(pos, inv):
    t = pos.shape[0]
    tm = min(TM, t)

    def body(pos_ref, inv_ref, ca_ref, sb_ref, sc_ref):
        ang = pos_ref[...] * inv_ref[...]
        c = jnp.cos(ang)
        s = jnp.sin(ang)
        lane = lax.broadcasted_iota(jnp.int32, ang.shape, 1)
        ca_ref[...] = jnp.where(lane < NOPE, 1.0, jnp.where(lane < NOPE + ROPE, c, 0.0))
        sb_ref[...] = jnp.where((lane >= NOPE) & (lane < NOPE + 16), -s, 0.0)
        sc_ref[...] = jnp.where((lane >= NOPE + 16) & (lane < NOPE + ROPE), s, 0.0)

    out = jax.ShapeDtypeStruct((t, HEAD_PAD), F32)
    blk = pl.BlockSpec((tm, HEAD_PAD), lambda i: (i, 0))
    return pl.pallas_call(
        body, name="rope_tables", grid=(t // tm,),
        in_specs=[pl.BlockSpec((tm, 1), lambda i: (i, 0)), pl.BlockSpec((1, HEAD_PAD), lambda i: (0, 0))],
        out_specs=[blk, blk, blk], out_shape=[out, out, out],
        compiler_params=_cp(("parallel",)),
    )(pos, inv)


def pre_in_fwd(x, g, w):
    t = x.shape[0]
    tm = min(TM, t)

    def body(x_ref, g_ref, w_ref, z_ref):
        h, _, _ = _rms_fwd(x_ref[...], g_ref[...])
        z_ref[...] = _mm_nt(h.astype(BF16), w_ref[...])

    return pl.pallas_call(
        body, name="pre_in_fwd", grid=(t // tm,),
        in_specs=[pl.BlockSpec((tm, D), lambda i: (i, 0)), _whole(), _whole()],
        out_specs=pl.BlockSpec((tm, Z_W), lambda i: (i, 0)),
        out_shape=jax.ShapeDtypeStruct((t, Z_W), F32),
        compiler_params=_cp(("parallel",)),
    )(x, g, w)


def mla_proj_fwd(z, ca, sb, sc, gq, gkv, wuq, wukv):
    t = z.shape[0]
    tm = min(TM, t)

    def body(z_ref, ca_ref, sb_ref, sc_ref, gq_ref, gkv_ref, wuq_ref, wukv_ref, q_ref, k_ref, v_ref):
        z = z_ref[...]
        ca, sb, sc = ca_ref[...], sb_ref[...], sc_ref[...]
        cq, _, _ = _rms_fwd(z[:, :Q_RANK], gq_ref[...])
        ckv, _, _ = _rms_fwd(z[:, Q_RANK:Q_RANK + KV_RANK], gkv_ref[...])
        q = _mm_nt(cq.astype(BF16), wuq_ref[...])
        kv = _mm_nt(ckv.astype(BF16), wukv_ref[...])
        kr = _rope(pltpu.roll(z[:, Q_RANK + KV_RANK:], NOPE, 1), ca, sb, sc)
        for h in range(HEADS):
            lanes = slice(h * HEAD_PAD, (h + 1) * HEAD_PAD)
            q_ref[:, lanes] = _rope(q[:, lanes], ca, sb, sc).astype(BF16)
            k_ref[:, lanes] = (kv[:, lanes] + kr).astype(BF16)
        v_ref[...] = kv[:, HEADS * HEAD_PAD:].astype(BF16)

    tab = pl.BlockSpec((tm, HEAD_PAD), lambda i: (i, 0))
    return pl.pallas_call(
        body, name="mla_proj_fwd", grid=(t // tm,),
        in_specs=[pl.BlockSpec((tm, 768), lambda i: (i, 1)), tab, tab, tab, _whole(), _whole(), _whole(), _whole()],
        out_specs=[pl.BlockSpec((tm, HEADS * HEAD_PAD), lambda i: (i, 0)),
                   pl.BlockSpec((tm, HEADS * HEAD_PAD), lambda i: (i, 0)),
                   pl.BlockSpec((tm, HEADS * V_DIM), lambda i: (i, 0))],
        out_shape=[jax.ShapeDtypeStruct((t, HEADS * HEAD_PAD), BF16),
                   jax.ShapeDtypeStruct((t, HEADS * HEAD_PAD), BF16),
                   jax.ShapeDtypeStruct((t, HEADS * V_DIM), BF16)],
        compiler_params=_cp(("parallel",)),
    )(z, ca, sb, sc, gq, gkv, wuq, wukv)


def _each(stages, k):
    def run():
        for stage in stages:
            stage[k]()
    return run


def attn_fwd(q, k, v, carried=()):
    t = q.shape[0]
    tq = min(TQ, t)
    nq = t // tq
    last_pair = HEADS // 2 - 1
    n = len(carried)

    def body(*refs):
        q_ref, k_ref, v_ref = refs[:3]
        o_ref, lse_ref = refs[3 + n:5 + n]
        sems = refs[5 + 2 * n:]
        stages = [_gather_steps(refs[3 + a], refs[5 + n + a], *sems[3 * a:3 * a + 3]) for a in range(n)]
        if n:
            pair = pl.program_id(0)
            pl.when((pair == 0) & (pl.program_id(1) == 0))(_each(stages, 0))
            pl.when((pair == last_pair) & (pl.program_id(1) == 0))(_each(stages, 1))
        i = pl.program_id(1)
        row = lax.broadcasted_iota(jnp.int32, (tq, tq), 0)
        col = lax.broadcasted_iota(jnp.int32, (tq, tq), 1)
        head_lanes = [slice(h * HEAD_PAD, (h + 1) * HEAD_PAD) for h in range(2)]

        def step(j, carry, masked):
            start = pl.multiple_of(j * tq, tq)
            vb = v_ref[pl.ds(start, tq), :]
            out = []
            for h in range(2):
                m, l, acc = carry[h]
                s = _mm_nt(q_ref[:, head_lanes[h]], k_ref[pl.ds(start, tq), head_lanes[h]])
                if masked:
                    s = jnp.where(col <= row, s, NEG)
                m_new = jnp.maximum(m, jnp.max(s, axis=-1, keepdims=True))
                p = jnp.exp2((s - m_new) * SCALE_LOG2E)
                alpha = jnp.exp2((m - m_new) * SCALE_LOG2E)
                l = alpha * l + jnp.sum(p, axis=-1, keepdims=True)
                acc = alpha * acc + _mm(p.astype(BF16), vb)
                out.append((m_new, l, acc))
            return tuple(out)

        init = (jnp.full((tq, 1), NEG, F32), jnp.zeros((tq, 1), F32), jnp.zeros((tq, 2 * V_DIM), F32))
        carry = lax.fori_loop(0, i, lambda j, c: step(j, c, False), (init, init))
        outs = []
        for h, (m, l, acc) in enumerate(step(i, carry, True)):
            outs.append(acc / l)
            lse_ref[:, head_lanes[h]] = jnp.broadcast_to(m * SCALE + jnp.log(l), (tq, HEAD_PAD))
        lane = lax.broadcasted_iota(jnp.int32, (tq, 2 * V_DIM), 1)
        o_ref[...] = jnp.where(lane < V_DIM, outs[0], outs[1])
        if n:
            pl.when((pl.program_id(0) == last_pair) & (i == nq - 1))(_each(stages, 2))

    hbm = pl.BlockSpec(memory_space=pl.ANY)
    return pl.pallas_call(
        body, name=f"attn_fwd_gather{n}" if n else "attn_fwd", grid=(HEADS // 2, nq),
        in_specs=[pl.BlockSpec((tq, 2 * HEAD_PAD), lambda p, i: (i, p)),
                  pl.BlockSpec((t, 2 * HEAD_PAD), lambda p, i: (0, p)),
                  pl.BlockSpec((t, 2 * V_DIM), lambda p, i: (0, p))] + [hbm] * n,
        out_specs=[pl.BlockSpec((tq, 2 * V_DIM), lambda p, i: (i, p)),
                   pl.BlockSpec((tq, 2 * HEAD_PAD), lambda p, i: (i, p))] + [hbm] * n,
        out_shape=[jax.ShapeDtypeStruct((t, HEADS * V_DIM), F32), jax.ShapeDtypeStruct((t, HEADS * HEAD_PAD), F32)]
        + [jax.ShapeDtypeStruct((N_DEV,) + c.shape, c.dtype) for c in carried],
        scratch_shapes=_comm_sems() * n,
        compiler_params=_cp(("arbitrary", "arbitrary") if n else ("parallel", "parallel")),
    )(q, k, v, *carried)


def _sgu_fwd(zsg, gm, lng, lnb, wc_ref, bias, mixed_ref):
    uv, th = _gelu(zsg)
    u, v0 = uv[:, :SG_W], uv[:, SG_W:]
    vc = v0 - _gmean(v0, gm)
    r = lax.rsqrt(_gmean(vc * vc, gm) + EPS)
    vh = vc * r
    v = vh * lng + lnb
    lane = lax.broadcasted_iota(jnp.int32, (CHUNK, SG_W), 1)
    for c in range(zsg.shape[0] // CHUNK):
        rows = slice(c * CHUNK, (c + 1) * CHUNK)
        vb = v[rows].astype(BF16)
        mixed = bias
        for g in range(SG_W // GROUP):
            mixed = mixed + jnp.where(lane // GROUP == g, _mm(wc_ref[g], vb), 0.0)
        mixed_ref[rows, :] = mixed
    return u, v, vh, r, th


def _conv_fwd(zcv, halo, first, cw):
    gb, gc, hh = zcv[:, :CV_W], zcv[:, CV_W:2 * CV_W], zcv[:, 2 * CV_W:]
    y = gc * hh
    yh = jnp.where(first, 0.0, halo[:, CV_W:2 * CV_W] * halo[:, 2 * CV_W:])
    y1 = _shift_down(y, 1, yh)
    y2 = _shift_down(y, 2, yh)
    conv = y2 * cw[0:1, :] + y1 * cw[1:2, :] + y * cw[2:3, :]
    return gb * conv, conv, y, y1, y2


def _tril_bf16(w_ref, g):
    row = lax.broadcasted_iota(jnp.int32, (CHUNK, CHUNK), 0)
    col = lax.broadcasted_iota(jnp.int32, (CHUNK, CHUNK), 1)
    return jnp.where(col <= row, w_ref[g], 0.0).astype(BF16)


def mix_fwd(x, z, ya, gm, lng, lnb, wsp, bias, cw, gout, wout, gpost):
    t = x.shape[0]
    tm = min(TM, t)

    def body(x_ref, zcv_ref, halo_ref, zsg_ref, ya_ref, gm_ref, lng_ref, lnb_ref, wsp_ref, bias_ref, cw_ref,
             gout_ref, wout_ref, gpost_ref, x1_ref, wc_ref, mixed_ref):
        i = pl.program_id(0)
        for g in range(SG_W // GROUP):
            wc_ref[g] = _tril_bf16(wsp_ref, g)
        u, _, _, _, _ = _sgu_fwd(zsg_ref[...], gm_ref[...], lng_ref[...], lnb_ref[...], wc_ref, bias_ref[...], mixed_ref)
        yb = u * mixed_ref[...]
        yc, _, _, _, _ = _conv_fwd(zcv_ref[...], halo_ref[...], i == 0, cw_ref[...])
        gout = gout_ref[...]
        na, _, _ = _rms_fwd(ya_ref[...], gout[:, :512])
        nb, _, _ = _rms_fwd(yb, gout[:, 512:768])
        nc, _, _ = _rms_fwd(yc, gout[:, 768:])
        mix = jnp.concatenate([na, nb, nc], axis=1).astype(BF16)
        o, _, _ = _rms_fwd(_mm(mix, wout_ref[...]), gpost_ref[...])
        x1_ref[...] = x_ref[...] + o

    hb = tm // 8
    return pl.pallas_call(
        body, name="mix_fwd", grid=(t // tm,),
        in_specs=[pl.BlockSpec((tm, D), lambda i: (i, 0)),
                  pl.BlockSpec((tm, 768), lambda i: (i, 0)),
                  pl.BlockSpec((8, 768), lambda i: (jnp.maximum(i * hb - 1, 0), 0)),
                  pl.BlockSpec((tm, 512), lambda i: (i, 3)),
                  pl.BlockSpec((tm, 512), lambda i: (i, 0)),
                  _whole(), _whole(), _whole(), _whole(), _whole(), _whole(), _whole(), _whole(), _whole()],
        out_specs=pl.BlockSpec((tm, D), lambda i: (i, 0)),
        out_shape=jax.ShapeDtypeStruct((t, D), F32),
        scratch_shapes=[pltpu.VMEM((SG_W // GROUP, CHUNK, CHUNK), BF16), pltpu.VMEM((tm, SG_W), F32)],
        compiler_params=_cp(("arbitrary",)),
    )(x, z, z, z, ya, gm, lng, lnb, wsp, bias, cw, gout, wout, gpost)


def _sigmoid(a):
    return 1.0 / (1.0 + jnp.exp(-a))


FFN_SHARD = D_FF // N_DEV


def _load_ffn_weights(g_ref, wgu_ref, wd_ref, sems):
    copies = []
    for j in range(N_DEV):
        for p, (dst, base) in enumerate(((wgu_ref, 0), (wgu_ref, D_FF), (wd_ref, 0))):
            copies.append(pltpu.make_async_copy(g_ref.at[j, pl.ds(p * FFN_SHARD, FFN_SHARD)],
                                                dst.at[pl.ds(base + j * FFN_SHARD, FFN_SHARD)], sems.at[3 * j + p]))
    for cp in copies:
        cp.start()
    for cp in copies:
        cp.wait()


def _ffn_weight_scratch():
    return [pltpu.VMEM((2 * D_FF, D), BF16), pltpu.VMEM((D_FF, D), BF16), pltpu.SemaphoreType.DMA((3 * N_DEV,))]


def ffn_fwd(x1, gpre, gathered, gpost):
    t = x1.shape[0]
    tm = min(TM_FFN, t)

    def body(x_ref, gpre_ref, g_ref, gpost_ref, x2_ref, wgu_ref, wd_ref, sems):
        @pl.when(pl.program_id(0) == 0)
        def _():
            _load_ffn_weights(g_ref, wgu_ref, wd_ref, sems)

        x = x_ref[...]
        h, _, _ = _rms_fwd(x, gpre_ref[...])
        ab = _mm_nt(h.astype(BF16), wgu_ref[...])
        a, b = ab[:, :D_FF], ab[:, D_FF:]
        s = a * _sigmoid(a) * b
        f, _, _ = _rms_fwd(_mm(s.astype(BF16), wd_ref[...]), gpost_ref[...])
        x2_ref[...] = x + f

    return pl.pallas_call(
        body, name="ffn_fwd", grid=(t // tm,),
        in_specs=[pl.BlockSpec((tm, D), lambda i: (i, 0)), _whole(), pl.BlockSpec(memory_space=pl.ANY), _whole()],
        out_specs=pl.BlockSpec((tm, D), lambda i: (i, 0)),
        out_shape=jax.ShapeDtypeStruct((t, D), F32),
        scratch_shapes=_ffn_weight_scratch(),
        compiler_params=_cp(("arbitrary",)),
    )(x1, gpre, gathered, gpost)


def loss_head(y, target):
    t = y.shape[0]
    tm = min(TM, t)

    def body(y_ref, t_ref, loss_ref, dy_ref):
        @pl.when(pl.program_id(0) == 0)
        def _():
            loss_ref[...] = jnp.zeros_like(loss_ref)

        e = y_ref[...] - t_ref[...]
        dy_ref[...] = e * (1.0 / D)
        loss_ref[...] += jnp.sum(jnp.sum(e * e, axis=-1, keepdims=True), axis=0, keepdims=True)

    return pl.pallas_call(
        body, name="loss_head", grid=(t // tm,),
        in_specs=[pl.BlockSpec((tm, D), lambda i: (i, 0)), pl.BlockSpec((tm, D), lambda i: (i, 0))],
        out_specs=[pl.BlockSpec((1, 128), lambda i: (0, 0)), pl.BlockSpec((tm, D), lambda i: (i, 0))],
        out_shape=[jax.ShapeDtypeStruct((1, 128), F32), jax.ShapeDtypeStruct((t, D), F32)],
        compiler_params=_cp(("arbitrary",)),
    )(y, target)


def _acc(ref, first, val):
    @pl.when(first)
    def _():
        ref[...] = val

    @pl.when(jnp.logical_not(first))
    def _():
        ref[...] += val


def ffn_bwd(x1, dx2, gpre, gathered, gpost):
    t = x1.shape[0]
    tm = min(TM_FFN, t)

    def body(x_ref, dx2_ref, gpre_ref, g_ref, gpost_ref,
             dx1_ref, h_ref, dab_ref, s_ref, df_ref, dgpre_ref, dgpost_ref, ab_ref, ds_ref, wgu_ref, wd_ref, sems):
        first = pl.program_id(0) == 0

        @pl.when(first)
        def _():
            _load_ffn_weights(g_ref, wgu_ref, wd_ref, sems)

        dx2 = dx2_ref[...]
        gpre, gpost = gpre_ref[...], gpost_ref[...]
        h, xh, rx = _rms_fwd(x_ref[...], gpre)
        h_ref[...] = h.astype(BF16)
        ab_ref[...] = _mm_nt(h_ref[...], wgu_ref[...])
        for c in range(0, D_FF, FFN_SLAB):
            a, b = ab_ref[:, c:c + FFN_SLAB], ab_ref[:, D_FF + c:D_FF + c + FFN_SLAB]
            s_ref[:, c:c + FFN_SLAB] = (a * _sigmoid(a) * b).astype(BF16)
        _, fh, rf = _rms_fwd(_mm(s_ref[...], wd_ref[...]), gpost)
        df, dgpost = _rms_bwd(fh, rf, gpost, dx2)
        df_ref[...] = df.astype(BF16)
        ds_ref[...] = _mm_nt(df_ref[...], wd_ref[...])
        for c in range(0, D_FF, FFN_SLAB):
            a, b = ab_ref[:, c:c + FFN_SLAB], ab_ref[:, D_FF + c:D_FF + c + FFN_SLAB]
            ds = ds_ref[:, c:c + FFN_SLAB]
            sg = _sigmoid(a)
            dab_ref[:, c:c + FFN_SLAB] = (ds * b * (sg * (1.0 + a * (1.0 - sg)))).astype(BF16)
            dab_ref[:, D_FF + c:D_FF + c + FFN_SLAB] = (ds * (a * sg)).astype(BF16)
        dx, dgpre = _rms_bwd(xh, rx, gpre, _mm(dab_ref[...], wgu_ref[...]))
        dx1_ref[...] = dx2 + dx
        _acc(dgpre_ref, first, dgpre)
        _acc(dgpost_ref, first, dgpost)

    row = lambda w: pl.BlockSpec((tm, w), lambda i: (i, 0))
    vec = pl.BlockSpec((1, D), lambda i: (0, 0))
    return pl.pallas_call(
        body, name="ffn_bwd", grid=(t // tm,),
        in_specs=[row(D), row(D), _whole(), pl.BlockSpec(memory_space=pl.ANY), _whole()],
        out_specs=[row(D), row(D), row(2 * D_FF), row(D_FF), row(D), vec, vec],
        out_shape=[jax.ShapeDtypeStruct((t, D), F32), jax.ShapeDtypeStruct((t, D), BF16),
                   jax.ShapeDtypeStruct((t, 2 * D_FF), BF16), jax.ShapeDtypeStruct((t, D_FF), BF16),
                   jax.ShapeDtypeStruct((t, D), BF16), jax.ShapeDtypeStruct((1, D), F32),
                   jax.ShapeDtypeStruct((1, D), F32)],
        scratch_shapes=[pltpu.VMEM((tm, 2 * D_FF), F32), pltpu.VMEM((tm, D_FF), F32)] + _ffn_weight_scratch(),
        compiler_params=_cp(("arbitrary",)),
    )(x1, dx2, gpre, gathered, gpost)


def atb(a, b, tk):
    t, k = a.shape
    n = b.shape[1]
    tt = min(TT, t)
    tk = min(tk, k)
    steps = t // tt

    def body(a_ref, b_ref, o_ref, acc_ref):
        i = pl.program_id(1)
        _acc(acc_ref, i == 0, _mm_tn(a_ref[...], b_ref[...]))

        @pl.when(i == steps - 1)
        def _():
            o_ref[...] = acc_ref[...].astype(BF16)

    return pl.pallas_call(
        body, name="atb", grid=(k // tk, steps),
        in_specs=[pl.BlockSpec((tt, tk), lambda j, i: (i, j)), pl.BlockSpec((tt, n), lambda j, i: (i, 0))],
        out_specs=pl.BlockSpec((tk, n), lambda j, i: (j, 0)),
        out_shape=jax.ShapeDtypeStruct((k, n), BF16),
        scratch_shapes=[pltpu.VMEM((tk, n), F32)],
        compiler_params=_cp(("parallel", "arbitrary")),
    )(a, b)


def mix_bwd(dx1, z, ya, gm, lng, lnb, wsp, wspt, bias, cw, gout, wout, gpost):
    t = dx1.shape[0]
    tm = min(TM, t)
    ng = SG_W // GROUP

    def body(dx1_ref, zcv_ref, halo_ref, zsg_ref, ya_ref, gm_ref, lng_ref, lnb_ref, wsp_ref, wspt_ref, bias_ref,
             cw_ref, gout_ref, wout_ref, gpost_ref,
             dya_ref, dyc_ref, dzsg_ref, mix_ref, do_ref, dgpost_ref, dgout_ref, dlng_ref, dlnb_ref, dwsp_ref,
             dbias_ref, wc_ref, wct_ref, mixed_ref, dv_ref):
        i = pl.program_id(0)
        first = i == 0
        gm = gm_ref[...]
        for g in range(ng):
            wc_ref[g] = _tril_bf16(wsp_ref, g)
            wct_ref[g] = jnp.where(
                lax.broadcasted_iota(jnp.int32, (CHUNK, CHUNK), 0) <= lax.broadcasted_iota(jnp.int32, (CHUNK, CHUNK), 1),
                wspt_ref[g], 0.0).astype(BF16)
        zsg = zsg_ref[...]
        lng = lng_ref[...]
        u, v, vh, r, th = _sgu_fwd(zsg, gm, lng, lnb_ref[...], wc_ref, bias_ref[...], mixed_ref)
        mixed = mixed_ref[...]
        yb = u * mixed
        yc, _, _, _, _ = _conv_fwd(zcv_ref[...], halo_ref[...], first, cw_ref[...])
        gout, gpost = gout_ref[...], gpost_ref[...]
        ga, gb_, gc_ = gout[:, :512], gout[:, 512:768], gout[:, 768:]
        na, yah, ra = _rms_fwd(ya_ref[...], ga)
        nb, ybh, rb = _rms_fwd(yb, gb_)
        nc, ych, rc = _rms_fwd(yc, gc_)
        mix = jnp.concatenate([na, nb, nc], axis=1).astype(BF16)
        _, oh, ro = _rms_fwd(_mm(mix, wout_ref[...]), gpost)
        do, dgpost = _rms_bwd(oh, ro, gpost, dx1_ref[...])
        dob = do.astype(BF16)
        dmix = _mm_nt(dob, wout_ref[...])
        dya, dga = _rms_bwd(yah, ra, ga, dmix[:, :512])
        dyb, dgb = _rms_bwd(ybh, rb, gb_, dmix[:, 512:768])
        dyc, dgc = _rms_bwd(ych, rc, gc_, dmix[:, 768:])
        dya_ref[...] = dya
        dyc_ref[...] = dyc
        mix_ref[...] = mix
        do_ref[...] = dob
        _acc(dgpost_ref, first, dgpost)
        _acc(dgout_ref, first, jnp.concatenate([dga, dgb, dgc], axis=1))
        du = dyb * mixed
        dmixed = dyb * u
        lane = lax.broadcasted_iota(jnp.int32, (CHUNK, SG_W), 1)
        row = lax.broadcasted_iota(jnp.int32, (CHUNK, CHUNK), 0)
        col = lax.broadcasted_iota(jnp.int32, (CHUNK, CHUNK), 1)
        dbias = jnp.zeros((CHUNK, SG_W), F32)
        dw = [jnp.zeros((CHUNK, CHUNK), F32) for _ in range(ng)]
        for c in range(tm // CHUNK):
            rows = slice(c * CHUNK, (c + 1) * CHUNK)
            dm = dmixed[rows]
            dbias = dbias + dm
            dmb = dm.astype(BF16)
            vb = v[rows].astype(BF16)
            dvc = jnp.zeros((CHUNK, SG_W), F32)
            for g in range(ng):
                in_g = lane // GROUP == g
                dvc = dvc + jnp.where(in_g, _mm(wct_ref[g], dmb), 0.0)
                dw[g] = dw[g] + _mm_nt(jnp.where(in_g, dmb, jnp.zeros_like(dmb)), vb)
            dv_ref[rows, :] = dvc
        for g in range(ng):
            dwg = jnp.where(col <= row, dw[g], 0.0)

            @pl.when(first)
            def _():
                dwsp_ref[g] = dwg

            @pl.when(jnp.logical_not(first))
            def _():
                dwsp_ref[g] += dwg
        _acc(dbias_ref, first, _gmean(dbias, gm) * GROUP)
        dv = dv_ref[...]
        _acc(dlng_ref, first, jnp.sum(dv * vh, axis=0, keepdims=True))
        _acc(dlnb_ref, first, jnp.sum(dv, axis=0, keepdims=True))
        dvh = dv * lng
        dv0 = r * (dvh - _gmean(dvh, gm) - vh * _gmean(dvh * vh, gm))
        dzsg_ref[...] = (jnp.concatenate([du, dv0], axis=1) * _gelu_grad(zsg, th)).astype(BF16)

    hb = tm // 8
    row_ = lambda w: pl.BlockSpec((tm, w), lambda i: (i, 0))
    vec = lambda w: pl.BlockSpec((1, w), lambda i: (0, 0))
    return pl.pallas_call(
        body, name="mix_bwd", grid=(t // tm,),
        in_specs=[row_(D),
                  pl.BlockSpec((tm, 768), lambda i: (i, 0)),
                  pl.BlockSpec((8, 768), lambda i: (jnp.maximum(i * hb - 1, 0), 0)),
                  pl.BlockSpec((tm, 512), lambda i: (i, 3)),
                  row_(512),
                  _whole(), _whole(), _whole(), _whole(), _whole(), _whole(), _whole(), _whole(), _whole(), _whole()],
        out_specs=[row_(512), row_(CV_W), row_(512), row_(D), row_(D), vec(D), vec(D), vec(SG_W), vec(SG_W),
                   pl.BlockSpec((ng, CHUNK, CHUNK), lambda i: (0, 0, 0)),
                   pl.BlockSpec((CHUNK, SG_W), lambda i: (0, 0))],
        out_shape=[jax.ShapeDtypeStruct((t, 512), F32), jax.ShapeDtypeStruct((t, CV_W), F32),
                   jax.ShapeDtypeStruct((t, 512), BF16), jax.ShapeDtypeStruct((t, D), BF16),
                   jax.ShapeDtypeStruct((t, D), BF16), jax.ShapeDtypeStruct((1, D), F32),
                   jax.ShapeDtypeStruct((1, D), F32), jax.ShapeDtypeStruct((1, SG_W), F32),
                   jax.ShapeDtypeStruct((1, SG_W), F32), jax.ShapeDtypeStruct((ng, CHUNK, CHUNK), F32),
                   jax.ShapeDtypeStruct((CHUNK, SG_W), F32)],
        scratch_shapes=[pltpu.VMEM((ng, CHUNK, CHUNK), BF16), pltpu.VMEM((ng, CHUNK, CHUNK), BF16),
                        pltpu.VMEM((tm, SG_W), F32), pltpu.VMEM((tm, SG_W), F32)],
        compiler_params=_cp(("arbitrary",)),
    )(dx1, z, z, z, ya, gm, lng, lnb, wsp, wspt, bias, cw, gout, wout, gpost)


def conv_bwd(dyc, z, cw):
    t = dyc.shape[0]
    tm = min(TM, t)
    hb = tm // 8
    last_blk = t // 8 - 1

    def body(dyc_ref, dyct_ref, zcv_ref, head_ref, tail_ref, cw_ref, dz_ref, dcw_ref):
        i = pl.program_id(0)
        first = i == 0
        last = i == pl.num_programs(0) - 1
        cw = cw_ref[...]
        zcv = zcv_ref[...]
        gb, gc, hh = zcv[:, :CV_W], zcv[:, CV_W:2 * CV_W], zcv[:, 2 * CV_W:]
        _, conv, y, y1, y2 = _conv_fwd(zcv, head_ref[...], first, cw)
        dyc = dyc_ref[...]
        dconv = dyc * gb
        tail = jnp.where(last, 0.0, dyct_ref[...] * tail_ref[:, :CV_W])
        d1 = _shift_up(dconv, 1, tail)
        d2 = _shift_up(dconv, 2, tail)
        dy = dconv * cw[2:3, :] + d1 * cw[1:2, :] + d2 * cw[0:1, :]
        dz_ref[...] = jnp.concatenate([dyc * conv, dy * hh, dy * gc], axis=1).astype(BF16)
        tap = lax.broadcasted_iota(jnp.int32, (8, CV_W), 0)
        dcw = jnp.where(tap == 0, jnp.sum(dconv * y2, axis=0, keepdims=True),
                        jnp.where(tap == 1, jnp.sum(dconv * y1, axis=0, keepdims=True),
                                  jnp.where(tap == 2, jnp.sum(dconv * y, axis=0, keepdims=True), 0.0)))
        _acc(dcw_ref, first, dcw)

    return pl.pallas_call(
        body, name="conv_bwd", grid=(t // tm,),
        in_specs=[pl.BlockSpec((tm, CV_W), lambda i: (i, 0)),
                  pl.BlockSpec((8, CV_W), lambda i: (jnp.minimum((i + 1) * hb, last_blk), 0)),
                  pl.BlockSpec((tm, 768), lambda i: (i, 0)),
                  pl.BlockSpec((8, 768), lambda i: (jnp.maximum(i * hb - 1, 0), 0)),
                  pl.BlockSpec((8, 768), lambda i: (jnp.minimum((i + 1) * hb, last_blk), 0)),
                  _whole()],
        out_specs=[pl.BlockSpec((tm, 768), lambda i: (i, 0)), pl.BlockSpec((8, CV_W), lambda i: (0, 0))],
        out_shape=[jax.ShapeDtypeStruct((t, 768), BF16), jax.ShapeDtypeStruct((8, CV_W), F32)],
        compiler_params=_cp(("arbitrary",)),
    )(dyc, dyc, z, z, z, cw)


def attn_bwd(q, k, v, o, lse, do, carried=()):
    t = q.shape[0]
    tq = min(TQ, t)
    nq = t // tq
    last_pair = HEADS // 2 - 1
    n = len(carried)

    def body(*refs):
        j = pl.program_id(1)
        q_ref, k_ref, v_ref, o_ref, lse_ref, do_ref = refs[:6]
        dq_ref, dk_ref, dv_ref = refs[6 + n:9 + n]
        sems = refs[9 + 2 * n:]
        stages = [_exchange_steps(refs[6 + a], refs[9 + n + a], carried[a][1], *sems[3 * a:3 * a + 3]) for a in range(n)]
        if n:
            pl.when((pl.program_id(0) == 0) & (j == 0))(_each(stages, 0))

        @pl.when(j == 0)
        def _():
            dq_ref[...] = jnp.zeros_like(dq_ref)

        row = lax.broadcasted_iota(jnp.int32, (tq, tq), 0)
        col = lax.broadcasted_iota(jnp.int32, (tq, tq), 1)
        vlane = lax.broadcasted_iota(jnp.int32, (tq, 2 * V_DIM), 1)
        head_lanes = [slice(h * HEAD_PAD, (h + 1) * HEAD_PAD) for h in range(2)]

        def step(i, carry, masked):
            start = pl.multiple_of(i * tq, tq)
            do_blk = do_ref[pl.ds(start, tq), :]
            o_blk = o_ref[pl.ds(start, tq), :]
            vb = v_ref[...]
            dks, dv_acc = [], carry[2]
            for h in range(2):
                lanes = head_lanes[h]
                qb = q_ref[pl.ds(start, tq), lanes]
                kb = k_ref[:, lanes]
                dob = jnp.where((vlane // V_DIM) == h, do_blk, 0.0)
                delta = jnp.sum(dob * o_blk, axis=-1, keepdims=True)
                lse2 = lse_ref[pl.ds(start, tq), lanes][:, 0:1] * LOG2E
                s = _mm_nt(qb, kb)
                if masked:
                    s = jnp.where(col <= row, s, NEG)
                p = jnp.exp2(s * SCALE_LOG2E - lse2)
                dob16 = dob.astype(BF16)
                dp = _mm_nt(dob16, vb)
                ds = (p * (dp - delta) * SCALE).astype(BF16)
                dv_acc = dv_acc + _mm_tn(p.astype(BF16), dob16)
                dks.append(carry[h] + _mm_tn(ds, qb))
                dq_ref[pl.ds(start, tq), lanes] += _mm(ds, kb)
            return dks[0], dks[1], dv_acc

        zero = jnp.zeros((tq, HEAD_PAD), F32)
        carry = step(j, (zero, zero, jnp.zeros((tq, 2 * V_DIM), F32)), True)
        dk0, dk1, dv_acc = lax.fori_loop(j + 1, nq, lambda i, c: step(i, c, False), carry)
        dk_ref[:, head_lanes[0]] = dk0
        dk_ref[:, head_lanes[1]] = dk1
        dv_ref[...] = dv_acc
        if n:
            pl.when((pl.program_id(0) == last_pair) & (j == nq - 1))(_each(stages, 1))

    hbm = pl.BlockSpec(memory_space=pl.ANY)
    return pl.pallas_call(
        body, name=f"attn_bwd_exchange{n}" if n else "attn_bwd", grid=(HEADS // 2, nq),
        in_specs=[pl.BlockSpec((t, 2 * HEAD_PAD), lambda p, j: (0, p)),
                  pl.BlockSpec((tq, 2 * HEAD_PAD), lambda p, j: (j, p)),
                  pl.BlockSpec((tq, 2 * V_DIM), lambda p, j: (j, p)),
                  pl.BlockSpec((t, 2 * V_DIM), lambda p, j: (0, p)),
                  pl.BlockSpec((t, 2 * HEAD_PAD), lambda p, j: (0, p)),
                  pl.BlockSpec((t, 2 * V_DIM), lambda p, j: (0, p))] + [hbm] * n,
        out_specs=[pl.BlockSpec((t, 2 * HEAD_PAD), lambda p, j: (0, p)),
                   pl.BlockSpec((tq, 2 * HEAD_PAD), lambda p, j: (j, p)),
                   pl.BlockSpec((tq, 2 * V_DIM), lambda p, j: (j, p))] + [hbm] * n,
        out_shape=[jax.ShapeDtypeStruct((t, HEADS * HEAD_PAD), F32), jax.ShapeDtypeStruct((t, HEADS * HEAD_PAD), F32),
                   jax.ShapeDtypeStruct((t, HEADS * V_DIM), F32)]
        + [jax.ShapeDtypeStruct(src.shape if scatter else (N_DEV,) + src.shape, src.dtype) for src, scatter in carried],
        scratch_shapes=_comm_sems() * n,
        compiler_params=_cp(("arbitrary", "arbitrary") if n else ("parallel", "arbitrary")),
    )(q, k, v, o, lse, do, *[src for src, _ in carried])


def mla_proj_bwd(dq, dk, dv, z, ca, sb, sc, gq, gkv, wuq, wukv):
    t = z.shape[0]
    tm = min(TM, t)

    def body(dq_ref, dk_ref, dv_ref, z_ref, ca_ref, sb_ref, sc_ref, gq_ref, gkv_ref, wuq_ref, wukv_ref,
             dz_ref, cq_ref, ckv_ref, dqp_ref, dkvp_ref, dgq_ref, dgkv_ref):
        first = pl.program_id(0) == 0
        z = z_ref[...]
        ca, sb, sc = ca_ref[...], sb_ref[...], sc_ref[...]
        gq, gkv = gq_ref[...], gkv_ref[...]
        cq, cqh, rq = _rms_fwd(z[:, :Q_RANK], gq)
        ckv, ckvh, rkv = _rms_fwd(z[:, Q_RANK:Q_RANK + KV_RANK], gkv)
        lane = lax.broadcasted_iota(jnp.int32, (tm, HEAD_PAD), 1)
        dkr = jnp.zeros((tm, HEAD_PAD), F32)
        for h in range(HEADS):
            lanes = slice(h * HEAD_PAD, (h + 1) * HEAD_PAD)
            dqp_ref[:, lanes] = _rope_t(dq_ref[:, lanes], ca, sb, sc).astype(BF16)
            dkh = dk_ref[:, lanes]
            dkr = dkr + dkh
            dkvp_ref[:, lanes] = jnp.where(lane < NOPE, dkh, 0.0).astype(BF16)
        dkvp_ref[:, HEADS * HEAD_PAD:] = dv_ref[...].astype(BF16)
        dkr = pltpu.roll(_rope_t(jnp.where(lane >= NOPE, dkr, 0.0), ca, sb, sc), HEAD_PAD - NOPE, 1)
        dkr = jnp.where(lane < ROPE, dkr, 0.0)
        dcq = _mm(dqp_ref[...], wuq_ref[...])
        dckv = _mm(dkvp_ref[...], wukv_ref[...])
        dzq, dgq = _rms_bwd(cqh, rq, gq, dcq)
        dzkv, dgkv = _rms_bwd(ckvh, rkv, gkv, dckv)
        dz_ref[...] = jnp.concatenate([dzq, dzkv, dkr], axis=1).astype(BF16)
        cq_ref[...] = cq.astype(BF16)
        ckv_ref[...] = ckv.astype(BF16)
        _acc(dgq_ref, first, dgq)
        _acc(dgkv_ref, first, dgkv)

    row = lambda w: pl.BlockSpec((tm, w), lambda i: (i, 0))
    vec = lambda w: pl.BlockSpec((1, w), lambda i: (0, 0))
    return pl.pallas_call(
        body, name="mla_proj_bwd", grid=(t // tm,),
        in_specs=[row(1024), row(1024), row(512), pl.BlockSpec((tm, 768), lambda i: (i, 1)),
                  row(HEAD_PAD), row(HEAD_PAD), row(HEAD_PAD), _whole(), _whole(), _whole(), _whole()],
        out_specs=[row(768), row(Q_RANK), row(KV_RANK), row(1024), row(1536), vec(Q_RANK), vec(KV_RANK)],
        out_shape=[jax.ShapeDtypeStruct((t, 768), BF16), jax.ShapeDtypeStruct((t, Q_RANK), BF16),
                   jax.ShapeDtypeStruct((t, KV_RANK), BF16), jax.ShapeDtypeStruct((t, 1024), BF16),
                   jax.ShapeDtypeStruct((t, 1536), BF16), jax.ShapeDtypeStruct((1, Q_RANK), F32),
                   jax.ShapeDtypeStruct((1, KV_RANK), F32)],
        compiler_params=_cp(("arbitrary",)),
    )(dq, dk, dv, z, ca, sb, sc, gq, gkv, wuq, wukv)


def pre_in_bwd(x, dx1, dzcv, dzmla, dzsg, g, w):
    t = x.shape[0]
    tm = min(TM, t)

    def body(x_ref, dx1_ref, dzcv_ref, dzmla_ref, dzsg_ref, g_ref, w_ref, dx_ref, h_ref, dz_ref, dg_ref):
        g = g_ref[...]
        h, xh, r = _rms_fwd(x_ref[...], g)
        dz = jnp.concatenate([dzcv_ref[...], dzmla_ref[...], dzsg_ref[...]], axis=1)
        dx, dg = _rms_bwd(xh, r, g, _mm(dz, w_ref[...]))
        dx_ref[...] = dx1_ref[...] + dx
        h_ref[...] = h.astype(BF16)
        dz_ref[...] = dz
        _acc(dg_ref, pl.program_id(0) == 0, dg)

    row = lambda w_: pl.BlockSpec((tm, w_), lambda i: (i, 0))
    return pl.pallas_call(
        body, name="pre_in_bwd", grid=(t // tm,),
        in_specs=[row(D), row(D), row(768), row(768), row(512), _whole(), _whole()],
        out_specs=[row(D), row(D), row(Z_W), pl.BlockSpec((1, D), lambda i: (0, 0))],
        out_shape=[jax.ShapeDtypeStruct((t, D), F32), jax.ShapeDtypeStruct((t, D), BF16),
                   jax.ShapeDtypeStruct((t, Z_W), BF16), jax.ShapeDtypeStruct((1, D), F32)],
        compiler_params=_cp(("arbitrary",)),
    )(x, dx1, dzcv, dzmla, dzsg, g, w)


MESH = pl.DeviceIdType.MESH


def _place():
    return lax.axis_index("x"), lax.axis_index("y"), lax.axis_index("c")


def _comm_sems():
    return [pltpu.SemaphoreType.DMA((7,)), pltpu.SemaphoreType.DMA((7,)), pltpu.SemaphoreType.DMA]


def _gather_steps(x_ref, out_ref, send_sems, recv_sems, local_sem):
    x, y, c = _place()
    me, sibling = (x, y, c), (x, y, 1 - c)
    chips = [(1 - x, y), (x, 1 - y), (1 - x, 1 - y)]

    def slot(px, py, pc):
        return out_ref.at[4 * px + 2 * py + pc]

    def copy(k, blk, to, src=None):
        return pltpu.make_async_remote_copy(
            src_ref=slot(*blk) if src is None else src, dst_ref=slot(*blk),
            send_sem=send_sems.at[k], recv_sem=recv_sems.at[k], device_id=to, device_id_type=MESH)

    mine = pltpu.make_async_copy(x_ref, slot(*me), local_sem)
    first = [copy(0, me, sibling, src=x_ref)] + [copy(1 + j, me, (*chip, c), src=x_ref) for j, chip in enumerate(chips)]
    passed = [copy(4 + j, (*chip, c), sibling) for j, chip in enumerate(chips)]

    def start():
        mine.start()
        for cp in first:
            cp.start()

    def forward():
        for j, chip in enumerate(chips):
            copy(1 + j, (*chip, c), me).wait_recv()
            passed[j].start()

    def finish():
        copy(0, sibling, me).wait_recv()
        for j, chip in enumerate(chips):
            copy(4 + j, (*chip, 1 - c), me).wait_recv()
        for cp in first + passed:
            cp.wait_send()
        mine.wait()

    return start, forward, finish


def _exchange_steps(src_ref, out_ref, scatter, send_sems, recv_sems, local_sem):
    x, y, c = _place()
    me = 4 * x + 2 * y + c
    own = pltpu.make_async_copy(src_ref.at[me] if scatter else src_ref, out_ref.at[me], local_sem)
    copies = []
    for k in range(1, N_DEV):
        px = 1 - x if k & 4 else x
        py = 1 - y if k & 2 else y
        pc = 1 - c if k & 1 else c
        copies.append(pltpu.make_async_remote_copy(
            src_ref=src_ref.at[4 * px + 2 * py + pc] if scatter else src_ref, dst_ref=out_ref.at[me],
            send_sem=send_sems.at[k - 1], recv_sem=recv_sems.at[k - 1], device_id=(px, py, pc), device_id_type=MESH))

    def start():
        own.start()
        for cp in copies:
            cp.start()

    def finish():
        for cp in copies:
            cp.wait_recv()
        for cp in copies:
            cp.wait_send()
        own.wait()

    return start, finish


def all_gather(block):
    def body(x_ref, out_ref, *sems):
        for stage in _gather_steps(x_ref, out_ref, *sems):
            stage()

    return pl.pallas_call(
        body, name="all_gather",
        in_specs=[pl.BlockSpec(memory_space=pl.ANY)],
        out_specs=pl.BlockSpec(memory_space=pl.ANY),
        out_shape=jax.ShapeDtypeStruct((N_DEV,) + block.shape, block.dtype),
        scratch_shapes=_comm_sems(),
    )(block)


def grad_exchange(chunks, small):
    def body(chunks_ref, small_ref, got_ref, gots_ref, *sems):
        start_c, finish_c = _exchange_steps(chunks_ref, got_ref, True, *sems[:3])
        start_s, finish_s = _exchange_steps(small_ref, gots_ref, False, *sems[3:])
        start_c()
        start_s()
        finish_c()
        finish_s()

    return pl.pallas_call(
        body, name="grad_exchange",
        in_specs=[pl.BlockSpec(memory_space=pl.ANY), pl.BlockSpec(memory_space=pl.ANY)],
        out_specs=[pl.BlockSpec(memory_space=pl.ANY), pl.BlockSpec(memory_space=pl.ANY)],
        out_shape=[jax.ShapeDtypeStruct(chunks.shape, chunks.dtype),
                   jax.ShapeDtypeStruct((N_DEV,) + small.shape, small.dtype)],
        scratch_shapes=_comm_sems() + _comm_sems(),
    )(chunks, small)


def _row_tile(r, cap):
    return max(d for d in range(16, cap + 1, 16) if r % d == 0)


def sum_adamw(parts, w, m, v, cap):
    r, c = w.shape
    tr = _row_tile(r, cap)
    c1 = 1.0 / (1.0 - ADAM_B1 ** ADAM_STEP)
    c2 = 1.0 / (1.0 - ADAM_B2 ** ADAM_STEP)

    def body(p_ref, w_ref, m_ref, v_ref, g_ref, d_ref, nm_ref, nv_ref):
        g = p_ref[0].astype(F32)
        for k in range(1, N_DEV):
            g = g + p_ref[k].astype(F32)
        m = ADAM_B1 * m_ref[...] + (1.0 - ADAM_B1) * g
        v = ADAM_B2 * v_ref[...] + (1.0 - ADAM_B2) * (g * g)
        g_ref[...] = g
        nm_ref[...] = m
        nv_ref[...] = v
        d_ref[...] = -ADAM_LR * ((m * c1) / (jnp.sqrt(v * c2) + ADAM_EPS) + ADAM_WD * w_ref[...])

    blk = pl.BlockSpec((tr, c), lambda i: (i, 0))
    out = jax.ShapeDtypeStruct((r, c), F32)
    return pl.pallas_call(
        body, name="sum_adamw", grid=(r // tr,),
        in_specs=[pl.BlockSpec((N_DEV, tr, c), lambda i: (0, i, 0)), blk, blk, blk],
        out_specs=[blk, blk, blk, blk], out_shape=[out, out, out, out],
        compiler_params=_cp(("parallel",)),
    )(parts, w, m, v)


PACK_W = 1024
MIX_PIECES = (("w_out", D // N_DEV, D, False), ("w_uq", HEADS * (NOPE + ROPE) // N_DEV, Q_RANK, True),
              ("w_ukv", HEADS * (NOPE + V_DIM) // N_DEV, KV_RANK, True), ("conv", 16, PACK_W, False),
              ("w_in", IN_W // N_DEV, D, True))
FFN_PIECES = (("w_gate", D_FF // N_DEV, D, True), ("w_up", D_FF // N_DEV, D, True), ("w_down", D_FF // N_DEV, D, False))
OFFSET = {}
for _pieces in (MIX_PIECES, FFN_PIECES):
    _off = 0
    for _name, _rows, _, _ in _pieces:
        OFFSET[_name] = _off
        _off += _rows + -_rows % 16
assert all(o % 16 == 0 for o in OFFSET.values())
assert [OFFSET[n] for n in ("w_gate", "w_up", "w_down")] == [0, FFN_SHARD, 2 * FFN_SHARD]
CONV_BITS = 3 * (CV_W // N_DEV) * 2


def _to_pack(shards, dtype, pieces, conv=None):
    nl = shards["w_in"].shape[0]
    parts = []
    for name, rows, cols, transposed in pieces:
        if name == "conv":
            if conv is None:
                a = jnp.zeros((nl, rows, PACK_W), dtype)
            else:
                bits = lax.bitcast_convert_type(conv.astype(F32), BF16).reshape(nl, CONV_BITS)
                a = jnp.pad(bits, ((0, 0), (0, rows * PACK_W - CONV_BITS))).reshape(nl, rows, PACK_W)
        else:
            a = shards[name].astype(dtype)
            a = jnp.swapaxes(a, 1, 2) if transposed else a
            a = jnp.pad(a, ((0, 0), (0, -rows % 16), (0, PACK_W - cols)))
        parts.append(a)
    return jnp.concatenate(parts, axis=1)


def _from_pack(pack, pieces):
    out = {}
    for name, rows, cols, transposed in pieces:
        if name != "conv":
            a = pack[:, OFFSET[name]:OFFSET[name] + rows, :cols]
            out[name] = jnp.swapaxes(a, 1, 2) if transposed else a
    return out


def _mix_weights(g):
    def rows(name):
        _, n, cols, _ = next(p for p in MIX_PIECES if p[0] == name)
        return g[:, OFFSET[name]:OFFSET[name] + n, :cols]

    w_in_t = rows("w_in").reshape(IN_W, D)
    w_in_p = jnp.concatenate([w_in_t[1184:], w_in_t[:672], jnp.zeros((96, D), BF16), w_in_t[672:1184]], axis=0)
    w_uq_p = jnp.pad(rows("w_uq"), ((0, 0), (0, HEAD_PAD - NOPE - ROPE), (0, 0))).reshape(HEADS * HEAD_PAD, Q_RANK)
    kv = rows("w_ukv")
    w_k = jnp.pad(kv[:, :NOPE], ((0, 0), (0, HEAD_PAD - NOPE), (0, 0))).reshape(HEADS * HEAD_PAD, KV_RANK)
    w_ukv_p = jnp.concatenate([w_k, kv[:, NOPE:].reshape(HEADS * V_DIM, KV_RANK)], axis=0)
    bits = rows("conv").reshape(N_DEV, -1)[:, :CONV_BITS].reshape(N_DEV, 3, CV_W // N_DEV, 2)
    conv_w = jnp.moveaxis(lax.bitcast_convert_type(bits, F32), 0, 1).reshape(3, CV_W)
    return dict(w_in=w_in_p, w_uq=w_uq_p, w_ukv=w_ukv_p, w_out=rows("w_out").reshape(D, D), conv_w=conv_w)


def _grad_chunks(full):
    if "w_in" in full:
        pieces = MIX_PIECES
        d_in = full["w_in"]
        d_in = jnp.concatenate([d_in[768:768 + 672], d_in[1536:], d_in[:768]], axis=0)
        d_uq = full["w_uq"].reshape(HEADS, HEAD_PAD, Q_RANK)[:, :NOPE + ROPE]
        d_k = full["w_ukv"][:HEADS * HEAD_PAD].reshape(HEADS, HEAD_PAD, KV_RANK)[:, :NOPE]
        d_v = full["w_ukv"][HEADS * HEAD_PAD:].reshape(HEADS, V_DIM, KV_RANK)
        mats = dict(w_in=d_in, w_uq=d_uq, w_ukv=jnp.concatenate([d_k, d_v], axis=1), w_out=full["w_out"])
    else:
        pieces = FFN_PIECES
        mats = dict(w_gate=full["w_gu"][:D_FF], w_up=full["w_gu"][D_FF:], w_down=full["w_down"])
    parts = []
    for name, rows, cols, _ in pieces:
        if name == "conv":
            parts.append(jnp.zeros((N_DEV, rows, PACK_W), BF16))
        else:
            parts.append(jnp.pad(mats[name].reshape(N_DEV, rows, cols), ((0, 0), (0, -rows % 16), (0, PACK_W - cols))))
    return jnp.concatenate(parts, axis=1)


SMALL = (("mix_pre_g", (D,)), ("mix_post_g", (D,)), ("ffn_pre_g", (D,)), ("ffn_post_g", (D,)), ("q_norm_g", (Q_RANK,)),
         ("kv_norm_g", (KV_RANK,)), ("sg_ln_g", (SG_W,)), ("sg_ln_b", (SG_W,)), ("w_sp", (4, CHUNK, CHUNK)),
         ("b_sp", (4, CHUNK)), ("out_norm_g", (D,)))
SMALL_ROWS = 576


def _pack_small(vals, nl):
    flat = jnp.concatenate([vals[name].reshape(nl, -1) for name, _ in SMALL] + [vals["conv_w"].reshape(nl, -1)], axis=1)
    return jnp.pad(flat, ((0, 0), (0, SMALL_ROWS * 128 - flat.shape[1]))).reshape(nl * SMALL_ROWS, 128)


def _unpack_small(pack, nl):
    flat = pack.reshape(nl, SMALL_ROWS * 128)
    out, off = {}, 0
    for name, shape in SMALL + (("conv_w", (3, CV_W)),):
        n = int(np.prod(shape))
        out[name] = flat[:, off:off + n].reshape((nl,) + shape)
        off += n
    return out


def _layer_fwd(x, lw, sp, tabs, consts, next_pack):
    ca, sb, sc = tabs
    z = pre_in_fwd(x, sp["mix_pre_g"], lw["w_in"])
    q, k, v = mla_proj_fwd(z, ca, sb, sc, sp["q_norm_g"], sp["kv_norm_g"], lw["w_uq"], lw["w_ukv"])
    ya, lse, ffn_gathered, *mix_gathered = attn_fwd(q, k, v, next_pack)
    lw["ffn"] = ffn_gathered
    x1 = mix_fwd(x, z, ya, consts["gm"], sp["sg_ln_g"], sp["sg_ln_b"], sp["w_sp"], sp["bias"], lw["conv_w"],
                 sp["out_norm_g"], lw["w_out"], sp["mix_post_g"])
    x2 = ffn_fwd(x1, sp["ffn_pre_g"], lw["ffn"], sp["ffn_post_g"])
    return x2, (x, z, q, k, v, ya, lse, x1), mix_gathered


def _layer_bwd(dx2, saved, lw, sp, tabs, consts, pending):
    ca, sb, sc = tabs
    x, z, q, k, v, ya, lse, x1 = saved
    dx1, h2, dab, s, df, d_ffn_pre, d_ffn_post = ffn_bwd(x1, dx2, sp["ffn_pre_g"], lw["ffn"], sp["ffn_post_g"])
    ffn_chunks = _grad_chunks(dict(w_gu=atb(dab, h2, 1408), w_down=atb(s, df, 1408)))
    dya, dyc, dzsg, mix, do, d_mix_post, d_out_norm, d_lng, d_lnb, d_wsp, d_bias = mix_bwd(
        dx1, z, ya, consts["gm"], sp["sg_ln_g"], sp["sg_ln_b"], sp["w_sp"], sp["w_sp_t"], sp["bias"], lw["conv_w"],
        sp["out_norm_g"], lw["w_out"], sp["mix_post_g"])
    d_w_out = atb(mix, do, 1024)
    dzcv, d_cw = conv_bwd(dyc, z, lw["conv_w"])
    dq, dk, dv, *received = attn_bwd(q, k, v, ya, lse, dya, ((ffn_chunks, True),) + pending)
    dzmla, cq, ckv, dqp, dkvp, d_gq, d_gkv = mla_proj_bwd(dq, dk, dv, z, ca, sb, sc, sp["q_norm_g"], sp["kv_norm_g"],
                                                          lw["w_uq"], lw["w_ukv"])
    d_w_uq = atb(dqp, cq, 1024)
    d_w_ukv = atb(dkvp, ckv, 1536)
    dx, h1, dz, d_mix_pre = pre_in_bwd(x, dx1, dzcv, dzmla, dzsg, sp["mix_pre_g"], lw["w_in"])
    d_w_in = atb(dz, h1, 2048)
    mix_chunks = _grad_chunks(dict(w_in=d_w_in, w_uq=d_w_uq, w_ukv=d_w_ukv, w_out=d_w_out))
    d_bsp = d_bias[:, ::GROUP].T
    small = dict(mix_pre_g=d_mix_pre[0], mix_post_g=d_mix_post[0], ffn_pre_g=d_ffn_pre[0], ffn_post_g=d_ffn_post[0],
                 q_norm_g=d_gq[0], kv_norm_g=d_gkv[0], sg_ln_g=d_lng[0], sg_ln_b=d_lnb[0], w_sp=d_wsp, b_sp=d_bsp,
                 out_norm_g=d_out_norm[0], conv_w=d_cw[:3])
    small_pack = _pack_small({name: a[None] for name, a in small.items()}, 1)
    return dx, ((mix_chunks, True), (small_pack, False)), received


def kernel(x, positions, mix_pre_g, mix_post_g, ffn_pre_g, ffn_post_g, w_in, q_norm_g, w_uq, kv_norm_g, w_ukv, sg_ln_g, sg_ln_b, w_sp, b_sp, conv_w, out_norm_g, w_out, w_gate, w_up, w_down, loss_target, m_mix_pre_g, m_mix_post_g, m_ffn_pre_g, m_ffn_post_g, m_w_in, m_q_norm_g, m_w_uq, m_kv_norm_g, m_w_ukv, m_sg_ln_g, m_sg_ln_b, m_w_sp, m_b_sp, m_conv_w, m_out_norm_g, m_w_out, m_w_gate, m_w_up, m_w_down, v_mix_pre_g, v_mix_post_g, v_ffn_pre_g, v_ffn_post_g, v_w_in, v_q_norm_g, v_w_uq, v_kv_norm_g, v_w_ukv, v_sg_ln_g, v_sg_ln_b, v_w_sp, v_b_sp, v_conv_w, v_out_norm_g, v_w_out, v_w_gate, v_w_up, v_w_down):
    nl = w_in.shape[0]
    t = x.shape[1]
    w = dict(mix_pre_g=mix_pre_g, mix_post_g=mix_post_g, ffn_pre_g=ffn_pre_g, ffn_post_g=ffn_post_g, w_in=w_in,
             q_norm_g=q_norm_g, w_uq=w_uq, kv_norm_g=kv_norm_g, w_ukv=w_ukv, sg_ln_g=sg_ln_g, sg_ln_b=sg_ln_b, w_sp=w_sp,
             b_sp=b_sp, conv_w=conv_w, out_norm_g=out_norm_g, w_out=w_out, w_gate=w_gate, w_up=w_up, w_down=w_down)
    m = dict(mix_pre_g=m_mix_pre_g, mix_post_g=m_mix_post_g, ffn_pre_g=m_ffn_pre_g, ffn_post_g=m_ffn_post_g, w_in=m_w_in,
             q_norm_g=m_q_norm_g, w_uq=m_w_uq, kv_norm_g=m_kv_norm_g, w_ukv=m_w_ukv, sg_ln_g=m_sg_ln_g, sg_ln_b=m_sg_ln_b,
             w_sp=m_w_sp, b_sp=m_b_sp, conv_w=m_conv_w, out_norm_g=m_out_norm_g, w_out=m_w_out, w_gate=m_w_gate,
             w_up=m_w_up, w_down=m_w_down)
    v = dict(mix_pre_g=v_mix_pre_g, mix_post_g=v_mix_post_g, ffn_pre_g=v_ffn_pre_g, ffn_post_g=v_ffn_post_g, w_in=v_w_in,
             q_norm_g=v_q_norm_g, w_uq=v_w_uq, kv_norm_g=v_kv_norm_g, w_ukv=v_w_ukv, sg_ln_g=v_sg_ln_g, sg_ln_b=v_sg_ln_b,
             w_sp=v_w_sp, b_sp=v_b_sp, conv_w=v_conv_w, out_norm_g=v_out_norm_g, w_out=v_w_out, w_gate=v_w_gate,
             w_up=v_w_up, w_down=v_w_down)

    mix_pack = _to_pack(w, BF16, MIX_PIECES, conv=w["conv_w"])
    ffn_pack = _to_pack(w, BF16, FFN_PIECES)
    consts = dict(gm=jnp.asarray(np.kron(np.eye(SG_W // GROUP), np.full((GROUP, GROUP), 1.0 / GROUP)), BF16))
    smalls = []
    for l in range(nl):
        sp = {name: w[name][l].reshape(1, -1) for name, shape in SMALL if len(shape) == 1}
        sp["w_sp"] = w["w_sp"][l]
        sp["w_sp_t"] = jnp.swapaxes(w["w_sp"][l], 1, 2)
        sp["bias"] = jnp.repeat(w["b_sp"][l].T, GROUP, axis=1)
        smalls.append(sp)
    inv_freq = 1.0 / (ROPE_THETA ** (jnp.arange(0, ROPE // 2, dtype=F32) / (ROPE // 2)))
    inv = jnp.zeros((1, HEAD_PAD), F32).at[0, NOPE:NOPE + ROPE].set(jnp.concatenate([inv_freq, inv_freq]))
    tabs = rope_tables(positions.reshape(t, 1).astype(F32), inv)

    h = x[0]
    saved, layers = [], []
    mix_gathered = [all_gather(mix_pack[0])]
    for l in range(nl):
        layers.append(_mix_weights(mix_gathered[0]))
        carried = (ffn_pack[l],) + ((mix_pack[l + 1],) if l + 1 < nl else ())
        h, s, mix_gathered = _layer_fwd(h, layers[l], smalls[l], tabs, consts, carried)
        saved.append(s)
    sq, dh = loss_head(h, loss_target[0])
    loss = lax.psum(0.5 * sq[0, 0] / D, ("x", "y", "c"))

    got_ffn, got_mix, got_small = [None] * nl, [None] * nl, [None] * nl
    pending = ()
    for l in reversed(range(nl)):
        dh, new_pending, received = _layer_bwd(dh, saved[l], layers[l], smalls[l], tabs, consts, pending)
        got_ffn[l] = received[0]
        if pending:
            got_mix[l + 1], got_small[l + 1] = received[1:]
        pending = new_pending
    got_mix[0], got_small[0] = grad_exchange(pending[0][0], pending[1][0])
    got_small = jnp.concatenate(got_small, axis=1)

    me = 4 * lax.axis_index("x") + 2 * lax.axis_index("y") + lax.axis_index("c")
    g_big, d_big, m_big, v_big = {}, {}, {}, {}
    for pieces, got in ((MIX_PIECES, got_mix), (FFN_PIECES, got_ffn)):
        w_f32, m_f32, v_f32 = [_to_pack(d, F32, pieces) for d in (w, m, v)]
        per_layer = [sum_adamw(got[l], w_f32[l], m_f32[l], v_f32[l], 352) for l in range(nl)]
        for i, out in enumerate((g_big, d_big, m_big, v_big)):
            out.update(_from_pack(jnp.stack([per_layer[l][i] for l in range(nl)]), pieces))

    def full_conv(a):
        return lax.dynamic_update_slice(jnp.zeros((nl, 3, CV_W), F32), a, (0, 0, me * (CV_W // N_DEV)))

    def small_pack(d):
        return _pack_small({**{name: d[name] for name, _ in SMALL}, "conv_w": full_conv(d["conv_w"])}, nl)

    g_small, d_small, m_small, v_small = [_unpack_small(p, nl) for p in
                                          sum_adamw(got_small, small_pack(w), small_pack(m), small_pack(v), 1152)]
    outs = []
    for big, small in ((g_big, g_small), (d_big, d_small), (m_big, m_small), (v_big, v_small)):
        for name in w:
            if name == "conv_w":
                outs.append(lax.dynamic_slice(small[name], (0, 0, me * (CV_W // N_DEV)), (nl, 3, CV_W // N_DEV)))
            elif name in small:
                outs.append(small[name])
            else:
                outs.append(big[name])
    return (loss, dh[None], *outs)
```

```python
import functools

import jax
import jax.numpy as jnp
import numpy as np
from jax import lax
from jax.experimental import pallas as pl
from jax.experimental.pallas import tpu as pltpu

F32 = jnp.float32
BF16 = jnp.bfloat16

D = 1024
Q_RANK = 384
KV_RANK = 256
ROPE = 32
HEADS = 8
NOPE = 64
V_DIM = 64
HEAD_PAD = 128
SG_W = 256
CV_W = 256
CHUNK = 128
GROUP = 64
D_FF = 2816
IN_W = 1952
Z_W = 2048
Z_CV, Z_MLA, Z_SG = 0, 768, 1536
EPS = 1e-6
ROPE_THETA = 10000.0
SCALE = (NOPE + ROPE) ** -0.5
LOG2E = 1.4426950408889634
SCALE_LOG2E = SCALE * LOG2E
NEG = -1e30
N_DEV = 8

ADAM_LR, ADAM_B1, ADAM_B2, ADAM_EPS, ADAM_WD, ADAM_STEP = 0.001, 0.9, 0.999, 1e-08, 0.01, 10

VMEM_LIMIT = 56 * 1024 * 1024

TM = 512
TM_FFN = 256
FFN_SLAB = 256
TQ = 512
TT = 2048


def _cp(sem, vmem=VMEM_LIMIT):
    return pltpu.CompilerParams(dimension_semantics=sem, vmem_limit_bytes=vmem)


def _whole():
    return pl.BlockSpec(memory_space=pltpu.VMEM)


def _mm(a, b):
    return jnp.dot(a, b, preferred_element_type=F32)


def _mm_nt(a, b):
    return lax.dot_general(a, b, (((1,), (1,)), ((), ())), preferred_element_type=F32)


def _mm_tn(a, b):
    return lax.dot_general(a, b, (((0,), (0,)), ((), ())), preferred_element_type=F32)


def _rms_fwd(x, g):
    r = lax.rsqrt(jnp.mean(x * x, axis=-1, keepdims=True) + EPS)
    xh = x * r
    return xh * g, xh, r


def _rms_bwd(xh, r, g, dy):
    dxh = dy * g
    dx = r * (dxh - xh * jnp.mean(dxh * xh, axis=-1, keepdims=True))
    dg = jnp.sum(dy * xh, axis=0, keepdims=True)
    return dx, dg


def _gmean(v, gm):
    hi = v.astype(BF16)
    lo = (v - hi.astype(F32)).astype(BF16)
    return _mm(hi, gm) + _mm(lo, gm)


def _gelu(x):
    c = np.float32(np.sqrt(2.0 / np.pi))
    u = c * (x + 0.044715 * x * x * x)
    t = jnp.tanh(u)
    return 0.5 * x * (1.0 + t), t


def _gelu_grad(x, t):
    c = np.float32(np.sqrt(2.0 / np.pi))
    return 0.5 * (1.0 + t) + 0.5 * x * (1.0 - t * t) * c * (1.0 + 3.0 * 0.044715 * x * x)


def _rope(t, ca, sb, sc):
    return t * ca + pltpu.roll(t, HEAD_PAD - 16, 1) * sb + pltpu.roll(t, 16, 1) * sc


def _rope_t(dt, ca, sb, sc):
    return dt * ca + pltpu.roll(dt * sb, 16, 1) + pltpu.roll(dt * sc, HEAD_PAD - 16, 1)


def _shift_down(y, k, head):
    n = y.shape[0]
    out = pltpu.roll(y, k, 0)
    row = lax.broadcasted_iota(jnp.int32, y.shape, 0)
    for j in range(k):
        out = jnp.where(row == j, head[8 - k + j:8 - k + j + 1, :], out)
    return out


def _shift_up(y, k, tail):
    n = y.shape[0]
    out = pltpu.roll(y, n - k, 0)
    row = lax.broadcasted_iota(jnp.int32, y.shape, 0)
    for j in range(k):
        out = jnp.where(row == n - k + j, tail[j:j + 1, :], out)
    return out


def rope_tables(pos, inv):
    t = pos.shape[0]
    tm = min(TM, t)

    def body(pos_ref, inv_ref, ca_ref, sb_ref, sc_ref):
        ang = pos_ref[...] * inv_ref[...]
        c = jnp.cos(ang)
        s = jnp.sin(ang)
        lane = lax.broadcasted_iota(jnp.int32, ang.shape, 1)
        ca_ref[...] = jnp.where(lane < NOPE, 1.0, jnp.where(lane < NOPE + ROPE, c, 0.0))
        sb_ref[...] = jnp.where((lane >= NOPE) & (lane < NOPE + 16), -s, 0.0)
        sc_ref[...] = jnp.where((lane >= NOPE + 16) & (lane < NOPE + ROPE), s, 0.0)

    out = jax.ShapeDtypeStruct((t, HEAD_PAD), F32)
    blk = pl.BlockSpec((tm, HEAD_PAD), lambda i: (i, 0))
    return pl.pallas_call(
        body, name="rope_tables", grid=(t // tm,),
        in_specs=[pl.BlockSpec((tm, 1), lambda i: (i, 0)), pl.BlockSpec((1, HEAD_PAD), lambda i: (0, 0))],
        out_specs=[blk, blk, blk], out_shape=[out, out, out],
        compiler_params=_cp(("parallel",)),
    )(pos, inv)


def pre_in_fwd(x, g, w):
    t = x.shape[0]
    tm = min(TM, t)

    def body(x_ref, g_ref, w_ref, z_ref):
        h, _, _ = _rms_fwd(x_ref[...], g_ref[...])
        z_ref[...] = _mm_nt(h.astype(BF16), w_ref[...])

    return pl.pallas_call(
        body, name="pre_in_fwd", grid=(t // tm,),
        in_specs=[pl.BlockSpec((tm, D), lambda i: (i, 0)), _whole(), _whole()],
        out_specs=pl.BlockSpec((tm, Z_W), lambda i: (i, 0)),
        out_shape=jax.ShapeDtypeStruct((t, Z_W), F32),
        compiler_params=_cp(("parallel",)),
    )(x, g, w)


def mla_proj_fwd(z, ca, sb, sc, gq, gkv, wuq, wukv):
    t = z.shape[0]
    tm = min(TM, t)

    def body(z_ref, ca_ref, sb_ref, sc_ref, gq_ref, gkv_ref, wuq_ref, wukv_ref, q_ref, k_ref, v_ref):
        z = z_ref[...]
        ca, sb, sc = ca_ref[...], sb_ref[...], sc_ref[...]
        cq, _, _ = _rms_fwd(z[:, :Q_RANK], gq_ref[...])
        ckv, _, _ = _rms_fwd(z[:, Q_RANK:Q_RANK + KV_RANK], gkv_ref[...])
        q = _mm_nt(cq.astype(BF16), wuq_ref[...])
        kv = _mm_nt(ckv.astype(BF16), wukv_ref[...])
        kr = _rope(pltpu.roll(z[:, Q_RANK + KV_RANK:], NOPE, 1), ca, sb, sc)
        for h in range(HEADS):
            lanes = slice(h * HEAD_PAD, (h + 1) * HEAD_PAD)
            q_ref[:, lanes] = _rope(q[:, lanes], ca, sb, sc).astype(BF16)
            k_ref[:, lanes] = (kv[:, lanes] + kr).astype(BF16)
        v_ref[...] = kv[:, HEADS * HEAD_PAD:].astype(BF16)

    tab = pl.BlockSpec((tm, HEAD_PAD), lambda i: (i, 0))
    return pl.pallas_call(
        body, name="mla_proj_fwd", grid=(t // tm,),
        in_specs=[pl.BlockSpec((tm, 768), lambda i: (i, 1)), tab, tab, tab, _whole(), _whole(), _whole(), _whole()],
        out_specs=[pl.BlockSpec((tm, HEADS * HEAD_PAD), lambda i: (i, 0)),
                   pl.BlockSpec((tm, HEADS * HEAD_PAD), lambda i: (i, 0)),
                   pl.BlockSpec((tm, HEADS * V_DIM), lambda i: (i, 0))],
        out_shape=[jax.ShapeDtypeStruct((t, HEADS * HEAD_PAD), BF16),
                   jax.ShapeDtypeStruct((t, HEADS * HEAD_PAD), BF16),
                   jax.ShapeDtypeStruct((t, HEADS * V_DIM), BF16)],
        compiler_params=_cp(("parallel",)),
    )(z, ca, sb, sc, gq, gkv, wuq, wukv)


def _each(stages, k):
    def run():
        for stage in stages:
            stage[k]()
    return run


def attn_fwd(q, k, v, carried=()):
    t = q.shape[0]
    tq = min(TQ, t)
    nq = t // tq
    last_pair = HEADS // 2 - 1
    n = len(carried)

    def body(*refs):
        q_ref, k_ref, v_ref = refs[:3]
        o_ref, lse_ref = refs[3 + n:5 + n]
        sems = refs[5 + 2 * n:]
        stages = [_gather_steps(refs[3 + a], refs[5 + n + a], *sems[3 * a:3 * a + 3]) for a in range(n)]
        if n:
            pair = pl.program_id(0)
            pl.when((pair == 0) & (pl.program_id(1) == 0))(_each(stages, 0))
            pl.when((pair == last_pair) & (pl.program_id(1) == 0))(_each(stages, 1))
        i = pl.program_id(1)
        row = lax.broadcasted_iota(jnp.int32, (tq, tq), 0)
        col = lax.broadcasted_iota(jnp.int32, (tq, tq), 1)
        head_lanes = [slice(h * HEAD_PAD, (h + 1) * HEAD_PAD) for h in range(2)]

        def step(j, carry, masked):
            start = pl.multiple_of(j * tq, tq)
            vb = v_ref[pl.ds(start, tq), :]
            out = []
            for h in range(2):
                m, l, acc = carry[h]
                s = _mm_nt(q_ref[:, head_lanes[h]], k_ref[pl.ds(start, tq), head_lanes[h]])
                if masked:
                    s = jnp.where(col <= row, s, NEG)
                m_new = jnp.maximum(m, jnp.max(s, axis=-1, keepdims=True))
                p = jnp.exp2((s - m_new) * SCALE_LOG2E)
                alpha = jnp.exp2((m - m_new) * SCALE_LOG2E)
                l = alpha * l + jnp.sum(p, axis=-1, keepdims=True)
                acc = alpha * acc + _mm(p.astype(BF16), vb)
                out.append((m_new, l, acc))
            return tuple(out)

        init = (jnp.full((tq, 1), NEG, F32), jnp.zeros((tq, 1), F32), jnp.zeros((tq, 2 * V_DIM), F32))
        carry = lax.fori_loop(0, i // 2, lambda j, c: step(2 * j + 1, step(2 * j, c, False), False), (init, init))
        carry = lax.fori_loop(0, i % 2, lambda _, c: step(i - 1, c, False), carry)
        outs = []
        for h, (m, l, acc) in enumerate(step(i, carry, True)):
            outs.append(acc / l)
            lse_ref[:, head_lanes[h]] = jnp.broadcast_to(m * SCALE + jnp.log(l), (tq, HEAD_PAD))
        lane = lax.broadcasted_iota(jnp.int32, (tq, 2 * V_DIM), 1)
        o_ref[...] = jnp.where(lane < V_DIM, outs[0], outs[1])
        if n:
            pl.when((pl.program_id(0) == last_pair) & (i == nq - 1))(_each(stages, 2))

    hbm = pl.BlockSpec(memory_space=pl.ANY)
    return pl.pallas_call(
        body, name=f"attn_fwd_gather{n}" if n else "attn_fwd", grid=(HEADS // 2, nq),
        in_specs=[pl.BlockSpec((tq, 2 * HEAD_PAD), lambda p, i: (i, p)),
                  pl.BlockSpec((t, 2 * HEAD_PAD), lambda p, i: (0, p)),
                  pl.BlockSpec((t, 2 * V_DIM), lambda p, i: (0, p))] + [hbm] * n,
        out_specs=[pl.BlockSpec((tq, 2 * V_DIM), lambda p, i: (i, p)),
                   pl.BlockSpec((tq, 2 * HEAD_PAD), lambda p, i: (i, p))] + [hbm] * n,
        out_shape=[jax.ShapeDtypeStruct((t, HEADS * V_DIM), F32), jax.ShapeDtypeStruct((t, HEADS * HEAD_PAD), F32)]
        + [jax.ShapeDtypeStruct((N_DEV,) + c.shape, c.dtype) for c in carried],
        scratch_shapes=_comm_sems() * n,
        compiler_params=_cp(("arbitrary", "arbitrary") if n else ("parallel", "parallel")),
    )(q, k, v, *carried)


def _sgu_fwd(zsg, gm, lng, lnb, wc_ref, bias, mixed_ref):
    uv, th = _gelu(zsg)
    u, v0 = uv[:, :SG_W], uv[:, SG_W:]
    vc = v0 - _gmean(v0, gm)
    r = lax.rsqrt(_gmean(vc * vc, gm) + EPS)
    vh = vc * r
    v = vh * lng + lnb
    lane = lax.broadcasted_iota(jnp.int32, (CHUNK, SG_W), 1)
    for c in range(zsg.shape[0] // CHUNK):
        rows = slice(c * CHUNK, (c + 1) * CHUNK)
        vb = v[rows].astype(BF16)
        mixed = bias
        for g in range(SG_W // GROUP):
            mixed = mixed + jnp.where(lane // GROUP == g, _mm(wc_ref[g], vb), 0.0)
        mixed_ref[rows, :] = mixed
    return u, v, vh, r, th


def _conv_fwd(zcv, halo, first, cw):
    gb, gc, hh = zcv[:, :CV_W], zcv[:, CV_W:2 * CV_W], zcv[:, 2 * CV_W:]
    y = gc * hh
    yh = jnp.where(first, 0.0, halo[:, CV_W:2 * CV_W] * halo[:, 2 * CV_W:])
    y1 = _shift_down(y, 1, yh)
    y2 = _shift_down(y, 2, yh)
    conv = y2 * cw[0:1, :] + y1 * cw[1:2, :] + y * cw[2:3, :]
    return gb * conv, conv, y, y1, y2


def _tril_bf16(w_ref, g):
    row = lax.broadcasted_iota(jnp.int32, (CHUNK, CHUNK), 0)
    col = lax.broadcasted_iota(jnp.int32, (CHUNK, CHUNK), 1)
    return jnp.where(col <= row, w_ref[g], 0.0).astype(BF16)


def mix_fwd(x, z, ya, gm, lng, lnb, wsp, bias, cw, gout, wout, gpost):
    t = x.shape[0]
    tm = min(TM, t)

    def body(x_ref, zcv_ref, halo_ref, zsg_ref, ya_ref, gm_ref, lng_ref, lnb_ref, wsp_ref, bias_ref, cw_ref,
             gout_ref, wout_ref, gpost_ref, x1_ref, wc_ref, mixed_ref):
        i = pl.program_id(0)
        for g in range(SG_W // GROUP):
            wc_ref[g] = _tril_bf16(wsp_ref, g)
        u, _, _, _, _ = _sgu_fwd(zsg_ref[...], gm_ref[...], lng_ref[...], lnb_ref[...], wc_ref, bias_ref[...], mixed_ref)
        yb = u * mixed_ref[...]
        yc, _, _, _, _ = _conv_fwd(zcv_ref[...], halo_ref[...], i == 0, cw_ref[...])
        gout = gout_ref[...]
        na, _, _ = _rms_fwd(ya_ref[...], gout[:, :512])
        nb, _, _ = _rms_fwd(yb, gout[:, 512:768])
        nc, _, _ = _rms_fwd(yc, gout[:, 768:])
        mix = jnp.concatenate([na, nb, nc], axis=1).astype(BF16)
        o, _, _ = _rms_fwd(_mm(mix, wout_ref[...]), gpost_ref[...])
        x1_ref[...] = x_ref[...] + o

    hb = tm // 8
    return pl.pallas_call(
        body, name="mix_fwd", grid=(t // tm,),
        in_specs=[pl.BlockSpec((tm, D), lambda i: (i, 0)),
                  pl.BlockSpec((tm, 768), lambda i: (i, 0)),
                  pl.BlockSpec((8, 768), lambda i: (jnp.maximum(i * hb - 1, 0), 0)),
                  pl.BlockSpec((tm, 512), lambda i: (i, 3)),
                  pl.BlockSpec((tm, 512), lambda i: (i, 0)),
                  _whole(), _whole(), _whole(), _whole(), _whole(), _whole(), _whole(), _whole(), _whole()],
        out_specs=pl.BlockSpec((tm, D), lambda i: (i, 0)),
        out_shape=jax.ShapeDtypeStruct((t, D), F32),
        scratch_shapes=[pltpu.VMEM((SG_W // GROUP, CHUNK, CHUNK), BF16), pltpu.VMEM((tm, SG_W), F32)],
        compiler_params=_cp(("arbitrary",)),
    )(x, z, z, z, ya, gm, lng, lnb, wsp, bias, cw, gout, wout, gpost)


def _sigmoid(a):
    return 1.0 / (1.0 + jnp.exp(-a))


FFN_SHARD = D_FF // N_DEV


def _load_ffn_weights(g_ref, wgu_ref, wd_ref, sems):
    copies = []
    for j in range(N_DEV):
        for p, (dst, base) in enumerate(((wgu_ref, 0), (wgu_ref, D_FF), (wd_ref, 0))):
            copies.append(pltpu.make_async_copy(g_ref.at[j, pl.ds(p * FFN_SHARD, FFN_SHARD)],
                                                dst.at[pl.ds(base + j * FFN_SHARD, FFN_SHARD)], sems.at[3 * j + p]))
    for cp in copies:
        cp.start()
    for cp in copies:
        cp.wait()


def _ffn_weight_scratch():
    return [pltpu.VMEM((2 * D_FF, D), BF16), pltpu.VMEM((D_FF, D), BF16), pltpu.SemaphoreType.DMA((3 * N_DEV,))]


def ffn_fwd(x1, gpre, gathered, gpost):
    t = x1.shape[0]
    tm = min(TM_FFN, t)

    def body(x_ref, gpre_ref, g_ref, gpost_ref, x2_ref, f_ref, wgu_ref, wd_ref, sems):
        @pl.when(pl.program_id(0) == 0)
        def _():
            _load_ffn_weights(g_ref, wgu_ref, wd_ref, sems)

        x = x_ref[...]
        h, _, _ = _rms_fwd(x, gpre_ref[...])
        ab = _mm_nt(h.astype(BF16), wgu_ref[...])
        a, b = ab[:, :D_FF], ab[:, D_FF:]
        s = a * _sigmoid(a) * b
        f = _mm(s.astype(BF16), wd_ref[...])
        f_ref[...] = f
        x2_ref[...] = x + _rms_fwd(f, gpost_ref[...])[0]

    row = pl.BlockSpec((tm, D), lambda i: (i, 0))
    return pl.pallas_call(
        body, name="ffn_fwd", grid=(t // tm,),
        in_specs=[row, _whole(), pl.BlockSpec(memory_space=pl.ANY), _whole()],
        out_specs=[row, row],
        out_shape=[jax.ShapeDtypeStruct((t, D), F32), jax.ShapeDtypeStruct((t, D), F32)],
        scratch_shapes=_ffn_weight_scratch(),
        compiler_params=_cp(("arbitrary",)),
    )(x1, gpre, gathered, gpost)


def loss_head(y, target):
    t = y.shape[0]
    tm = min(TM, t)

    def body(y_ref, t_ref, loss_ref, dy_ref):
        @pl.when(pl.program_id(0) == 0)
        def _():
            loss_ref[...] = jnp.zeros_like(loss_ref)

        e = y_ref[...] - t_ref[...]
        dy_ref[...] = e * (1.0 / D)
        loss_ref[...] += jnp.sum(jnp.sum(e * e, axis=-1, keepdims=True), axis=0, keepdims=True)

    return pl.pallas_call(
        body, name="loss_head", grid=(t // tm,),
        in_specs=[pl.BlockSpec((tm, D), lambda i: (i, 0)), pl.BlockSpec((tm, D), lambda i: (i, 0))],
        out_specs=[pl.BlockSpec((1, 128), lambda i: (0, 0)), pl.BlockSpec((tm, D), lambda i: (i, 0))],
        out_shape=[jax.ShapeDtypeStruct((1, 128), F32), jax.ShapeDtypeStruct((t, D), F32)],
        compiler_params=_cp(("arbitrary",)),
    )(y, target)


def _acc(ref, first, val):
    @pl.when(first)
    def _():
        ref[...] = val

    @pl.when(jnp.logical_not(first))
    def _():
        ref[...] += val


def ffn_bwd(x1, f, dx2, gpre, gathered, gpost):
    t = x1.shape[0]
    tm = min(TM_FFN, t)

    def body(x_ref, f_ref, dx2_ref, gpre_ref, g_ref, gpost_ref,
             dx1_ref, h_ref, dab_ref, s_ref, df_ref, dgpre_ref, dgpost_ref, ab_ref, ds_ref, wgu_ref, wd_ref, sems):
        first = pl.program_id(0) == 0

        @pl.when(first)
        def _():
            _load_ffn_weights(g_ref, wgu_ref, wd_ref, sems)

        dx2 = dx2_ref[...]
        gpre, gpost = gpre_ref[...], gpost_ref[...]
        h, xh, rx = _rms_fwd(x_ref[...], gpre)
        h_ref[...] = h.astype(BF16)
        ab_ref[...] = _mm_nt(h_ref[...], wgu_ref[...])
        for c in range(0, D_FF, FFN_SLAB):
            a, b = ab_ref[:, c:c + FFN_SLAB], ab_ref[:, D_FF + c:D_FF + c + FFN_SLAB]
            s_ref[:, c:c + FFN_SLAB] = (a * _sigmoid(a) * b).astype(BF16)
        _, fh, rf = _rms_fwd(f_ref[...], gpost)
        df, dgpost = _rms_bwd(fh, rf, gpost, dx2)
        df_ref[...] = df.astype(BF16)
        ds_ref[...] = _mm_nt(df_ref[...], wd_ref[...])
        for c in range(0, D_FF, FFN_SLAB):
            a, b = ab_ref[:, c:c + FFN_SLAB], ab_ref[:, D_FF + c:D_FF + c + FFN_SLAB]
            ds = ds_ref[:, c:c + FFN_SLAB]
            sg = _sigmoid(a)
            dab_ref[:, c:c + FFN_SLAB] = (ds * b * (sg * (1.0 + a * (1.0 - sg)))).astype(BF16)
            dab_ref[:, D_FF + c:D_FF + c + FFN_SLAB] = (ds * (a * sg)).astype(BF16)
        dx, dgpre = _rms_bwd(xh, rx, gpre, _mm(dab_ref[...], wgu_ref[...]))
        dx1_ref[...] = dx2 + dx
        _acc(dgpre_ref, first, dgpre)
        _acc(dgpost_ref, first, dgpost)

    row = lambda w: pl.BlockSpec((tm, w), lambda i: (i, 0))
    vec = pl.BlockSpec((1, D), lambda i: (0, 0))
    return pl.pallas_call(
        body, name="ffn_bwd", grid=(t // tm,),
        in_specs=[row(D), row(D), row(D), _whole(), pl.BlockSpec(memory_space=pl.ANY), _whole()],
        out_specs=[row(D), row(D), row(2 * D_FF), row(D_FF), row(D), vec, vec],
        out_shape=[jax.ShapeDtypeStruct((t, D), F32), jax.ShapeDtypeStruct((t, D), BF16),
                   jax.ShapeDtypeStruct((t, 2 * D_FF), BF16), jax.ShapeDtypeStruct((t, D_FF), BF16),
                   jax.ShapeDtypeStruct((t, D), BF16), jax.ShapeDtypeStruct((1, D), F32),
                   jax.ShapeDtypeStruct((1, D), F32)],
        scratch_shapes=[pltpu.VMEM((tm, 2 * D_FF), F32), pltpu.VMEM((tm, D_FF), F32)] + _ffn_weight_scratch(),
        compiler_params=_cp(("arbitrary",)),
    )(x1, f, dx2, gpre, gathered, gpost)


def atb(a, b, tk):
    t, k = a.shape
    n = b.shape[1]
    tt = min(TT, t)
    tk = min(tk, k)
    steps = t // tt

    def body(a_ref, b_ref, o_ref, acc_ref):
        i = pl.program_id(1)
        _acc(acc_ref, i == 0, _mm_tn(a_ref[...], b_ref[...]))

        @pl.when(i == steps - 1)
        def _():
            o_ref[...] = acc_ref[...].astype(BF16)

    return pl.pallas_call(
        body, name="atb", grid=(k // tk, steps),
        in_specs=[pl.BlockSpec((tt, tk), lambda j, i: (i, j)), pl.BlockSpec((tt, n), lambda j, i: (i, 0))],
        out_specs=pl.BlockSpec((tk, n), lambda j, i: (j, 0)),
        out_shape=jax.ShapeDtypeStruct((k, n), BF16),
        scratch_shapes=[pltpu.VMEM((tk, n), F32)],
        compiler_params=_cp(("parallel", "arbitrary")),
    )(a, b)


def mix_bwd(dx1, z, ya, gm, lng, lnb, wsp, wspt, bias, cw, gout, wout, gpost):
    t = dx1.shape[0]
    tm = min(TM, t)
    ng = SG_W // GROUP

    def body(dx1_ref, zcv_ref, halo_ref, zsg_ref, ya_ref, gm_ref, lng_ref, lnb_ref, wsp_ref, wspt_ref, bias_ref,
             cw_ref, gout_ref, wout_ref, gpost_ref,
             dya_ref, dyc_ref, dzsg_ref, mix_ref, do_ref, dgpost_ref, dgout_ref, dlng_ref, dlnb_ref, dwsp_ref,
             dbias_ref, wc_ref, wct_ref, mixed_ref, dv_ref):
        i = pl.program_id(0)
        first = i == 0
        gm = gm_ref[...]
        for g in range(ng):
            wc_ref[g] = _tril_bf16(wsp_ref, g)
            wct_ref[g] = jnp.where(
                lax.broadcasted_iota(jnp.int32, (CHUNK, CHUNK), 0) <= lax.broadcasted_iota(jnp.int32, (CHUNK, CHUNK), 1),
                wspt_ref[g], 0.0).astype(BF16)
        zsg = zsg_ref[...]
        lng = lng_ref[...]
        u, v, vh, r, th = _sgu_fwd(zsg, gm, lng, lnb_ref[...], wc_ref, bias_ref[...], mixed_ref)
        mixed = mixed_ref[...]
        yb = u * mixed
        yc, _, _, _, _ = _conv_fwd(zcv_ref[...], halo_ref[...], first, cw_ref[...])
        gout, gpost = gout_ref[...], gpost_ref[...]
        ga, gb_, gc_ = gout[:, :512], gout[:, 512:768], gout[:, 768:]
        na, yah, ra = _rms_fwd(ya_ref[...], ga)
        nb, ybh, rb = _rms_fwd(yb, gb_)
        nc, ych, rc = _rms_fwd(yc, gc_)
        mix = jnp.concatenate([na, nb, nc], axis=1).astype(BF16)
        _, oh, ro = _rms_fwd(_mm(mix, wout_ref[...]), gpost)
        do, dgpost = _rms_bwd(oh, ro, gpost, dx1_ref[...])
        dob = do.astype(BF16)
        dmix = _mm_nt(dob, wout_ref[...])
        dya, dga = _rms_bwd(yah, ra, ga, dmix[:, :512])
        dyb, dgb = _rms_bwd(ybh, rb, gb_, dmix[:, 512:768])
        dyc, dgc = _rms_bwd(ych, rc, gc_, dmix[:, 768:])
        dya_ref[...] = dya
        dyc_ref[...] = dyc
        mix_ref[...] = mix
        do_ref[...] = dob
        _acc(dgpost_ref, first, dgpost)
        _acc(dgout_ref, first, jnp.concatenate([dga, dgb, dgc], axis=1))
        du = dyb * mixed
        dmixed = dyb * u
        lane = lax.broadcasted_iota(jnp.int32, (CHUNK, SG_W), 1)
        row = lax.broadcasted_iota(jnp.int32, (CHUNK, CHUNK), 0)
        col = lax.broadcasted_iota(jnp.int32, (CHUNK, CHUNK), 1)
        dbias = jnp.zeros((CHUNK, SG_W), F32)
        dw = [jnp.zeros((CHUNK, CHUNK), F32) for _ in range(ng)]
        for c in range(tm // CHUNK):
            rows = slice(c * CHUNK, (c + 1) * CHUNK)
            dm = dmixed[rows]
            dbias = dbias + dm
            dmb = dm.astype(BF16)
            vb = v[rows].astype(BF16)
            dvc = jnp.zeros((CHUNK, SG_W), F32)
            for g in range(ng):
                in_g = lane // GROUP == g
                dvc = dvc + jnp.where(in_g, _mm(wct_ref[g], dmb), 0.0)
                dw[g] = dw[g] + _mm_nt(jnp.where(in_g, dmb, jnp.zeros_like(dmb)), vb)
            dv_ref[rows, :] = dvc
        for g in range(ng):
            dwg = jnp.where(col <= row, dw[g], 0.0)

            @pl.when(first)
            def _():
                dwsp_ref[g] = dwg

            @pl.when(jnp.logical_not(first))
            def _():
                dwsp_ref[g] += dwg
        _acc(dbias_ref, first, _gmean(dbias, gm) * GROUP)
        dv = dv_ref[...]
        _acc(dlng_ref, first, jnp.sum(dv * vh, axis=0, keepdims=True))
        _acc(dlnb_ref, first, jnp.sum(dv, axis=0, keepdims=True))
        dvh = dv * lng
        dv0 = r * (dvh - _gmean(dvh, gm) - vh * _gmean(dvh * vh, gm))
        dzsg_ref[...] = (jnp.concatenate([du, dv0], axis=1) * _gelu_grad(zsg, th)).astype(BF16)

    hb = tm // 8
    row_ = lambda w: pl.BlockSpec((tm, w), lambda i: (i, 0))
    vec = lambda w: pl.BlockSpec((1, w), lambda i: (0, 0))
    return pl.pallas_call(
        body, name="mix_bwd", grid=(t // tm,),
        in_specs=[row_(D),
                  pl.BlockSpec((tm, 768), lambda i: (i, 0)),
                  pl.BlockSpec((8, 768), lambda i: (jnp.maximum(i * hb - 1, 0), 0)),
                  pl.BlockSpec((tm, 512), lambda i: (i, 3)),
                  row_(512),
                  _whole(), _whole(), _whole(), _whole(), _whole(), _whole(), _whole(), _whole(), _whole(), _whole()],
        out_specs=[row_(512), row_(CV_W), row_(512), row_(D), row_(D), vec(D), vec(D), vec(SG_W), vec(SG_W),
                   pl.BlockSpec((ng, CHUNK, CHUNK), lambda i: (0, 0, 0)),
                   pl.BlockSpec((CHUNK, SG_W), lambda i: (0, 0))],
        out_shape=[jax.ShapeDtypeStruct((t, 512), F32), jax.ShapeDtypeStruct((t, CV_W), F32),
                   jax.ShapeDtypeStruct((t, 512), BF16), jax.ShapeDtypeStruct((t, D), BF16),
                   jax.ShapeDtypeStruct((t, D), BF16), jax.ShapeDtypeStruct((1, D), F32),
                   jax.ShapeDtypeStruct((1, D), F32), jax.ShapeDtypeStruct((1, SG_W), F32),
                   jax.ShapeDtypeStruct((1, SG_W), F32), jax.ShapeDtypeStruct((ng, CHUNK, CHUNK), F32),
                   jax.ShapeDtypeStruct((CHUNK, SG_W), F32)],
        scratch_shapes=[pltpu.VMEM((ng, CHUNK, CHUNK), BF16), pltpu.VMEM((ng, CHUNK, CHUNK), BF16),
                        pltpu.VMEM((tm, SG_W), F32), pltpu.VMEM((tm, SG_W), F32)],
        compiler_params=_cp(("arbitrary",)),
    )(dx1, z, z, z, ya, gm, lng, lnb, wsp, wspt, bias, cw, gout, wout, gpost)


def conv_bwd(dyc, z, cw):
    t = dyc.shape[0]
    tm = min(TM, t)
    hb = tm // 8
    last_blk = t // 8 - 1

    def body(dyc_ref, dyct_ref, zcv_ref, head_ref, tail_ref, cw_ref, dz_ref, dcw_ref):
        i = pl.program_id(0)
        first = i == 0
        last = i == pl.num_programs(0) - 1
        cw = cw_ref[...]
        zcv = zcv_ref[...]
        gb, gc, hh = zcv[:, :CV_W], zcv[:, CV_W:2 * CV_W], zcv[:, 2 * CV_W:]
        _, conv, y, y1, y2 = _conv_fwd(zcv, head_ref[...], first, cw)
        dyc = dyc_ref[...]
        dconv = dyc * gb
        tail = jnp.where(last, 0.0, dyct_ref[...] * tail_ref[:, :CV_W])
        d1 = _shift_up(dconv, 1, tail)
        d2 = _shift_up(dconv, 2, tail)
        dy = dconv * cw[2:3, :] + d1 * cw[1:2, :] + d2 * cw[0:1, :]
        dz_ref[...] = jnp.concatenate([dyc * conv, dy * hh, dy * gc], axis=1).astype(BF16)
        tap = lax.broadcasted_iota(jnp.int32, (8, CV_W), 0)
        dcw = jnp.where(tap == 0, jnp.sum(dconv * y2, axis=0, keepdims=True),
                        jnp.where(tap == 1, jnp.sum(dconv * y1, axis=0, keepdims=True),
                                  jnp.where(tap == 2, jnp.sum(dconv * y, axis=0, keepdims=True), 0.0)))
        _acc(dcw_ref, first, dcw)

    return pl.pallas_call(
        body, name="conv_bwd", grid=(t // tm,),
        in_specs=[pl.BlockSpec((tm, CV_W), lambda i: (i, 0)),
                  pl.BlockSpec((8, CV_W), lambda i: (jnp.minimum((i + 1) * hb, last_blk), 0)),
                  pl.BlockSpec((tm, 768), lambda i: (i, 0)),
                  pl.BlockSpec((8, 768), lambda i: (jnp.maximum(i * hb - 1, 0), 0)),
                  pl.BlockSpec((8, 768), lambda i: (jnp.minimum((i + 1) * hb, last_blk), 0)),
                  _whole()],
        out_specs=[pl.BlockSpec((tm, 768), lambda i: (i, 0)), pl.BlockSpec((8, CV_W), lambda i: (0, 0))],
        out_shape=[jax.ShapeDtypeStruct((t, 768), BF16), jax.ShapeDtypeStruct((8, CV_W), F32)],
        compiler_params=_cp(("arbitrary",)),
    )(dyc, dyc, z, z, z, cw)


def attn_bwd(q, k, v, o, lse, do, carried=()):
    t = q.shape[0]
    tq = min(TQ, t)
    nq = t // tq
    last_pair = HEADS // 2 - 1
    n = len(carried)

    def body(*refs):
        j = pl.program_id(1)
        q_ref, k_ref, v_ref, o_ref, lse_ref, do_ref = refs[:6]
        dq_ref, dk_ref, dv_ref = refs[6 + n:9 + n]
        sems = refs[9 + 2 * n:]
        stages = [_exchange_steps(refs[6 + a], refs[9 + n + a], carried[a][1], *sems[3 * a:3 * a + 3]) for a in range(n)]
        if n:
            pl.when((pl.program_id(0) == 0) & (j == 0))(_each(stages, 0))

        @pl.when(j == 0)
        def _():
            dq_ref[...] = jnp.zeros_like(dq_ref)

        row = lax.broadcasted_iota(jnp.int32, (tq, tq), 0)
        col = lax.broadcasted_iota(jnp.int32, (tq, tq), 1)
        vlane = lax.broadcasted_iota(jnp.int32, (tq, 2 * V_DIM), 1)
        head_lanes = [slice(h * HEAD_PAD, (h + 1) * HEAD_PAD) for h in range(2)]

        def step(i, carry, masked):
            start = pl.multiple_of(i * tq, tq)
            do_blk = do_ref[pl.ds(start, tq), :]
            o_blk = o_ref[pl.ds(start, tq), :]
            vb = v_ref[...]
            dks, dv_acc = [], carry[2]
            for h in range(2):
                lanes = head_lanes[h]
                qb = q_ref[pl.ds(start, tq), lanes]
                kb = k_ref[:, lanes]
                dob = jnp.where((vlane // V_DIM) == h, do_blk, 0.0)
                delta = jnp.sum(dob * o_blk, axis=-1, keepdims=True)
                lse2 = lse_ref[pl.ds(start, tq), lanes][:, 0:1] * LOG2E
                s = _mm_nt(qb, kb)
                if masked:
                    s = jnp.where(col <= row, s, NEG)
                p = jnp.exp2(s * SCALE_LOG2E - lse2)
                dob16 = dob.astype(BF16)
                dp = _mm_nt(dob16, vb)
                ds = (p * (dp - delta) * SCALE).astype(BF16)
                dv_acc = dv_acc + _mm_tn(p.astype(BF16), dob16)
                dks.append(carry[h] + _mm_tn(ds, qb))
                dq_ref[pl.ds(start, tq), lanes] += _mm(ds, kb)
            return dks[0], dks[1], dv_acc

        zero = jnp.zeros((tq, HEAD_PAD), F32)
        carry = step(j, (zero, zero, jnp.zeros((tq, 2 * V_DIM), F32)), True)
        dk0, dk1, dv_acc = lax.fori_loop(j + 1, nq, lambda i, c: step(i, c, False), carry)
        dk_ref[:, head_lanes[0]] = dk0
        dk_ref[:, head_lanes[1]] = dk1
        dv_ref[...] = dv_acc
        if n:
            pl.when((pl.program_id(0) == last_pair) & (j == nq - 1))(_each(stages, 1))

    hbm = pl.BlockSpec(memory_space=pl.ANY)
    return pl.pallas_call(
        body, name=f"attn_bwd_exchange{n}" if n else "attn_bwd", grid=(HEADS // 2, nq),
        in_specs=[pl.BlockSpec((t, 2 * HEAD_PAD), lambda p, j: (0, p)),
                  pl.BlockSpec((tq, 2 * HEAD_PAD), lambda p, j: (j, p)),
                  pl.BlockSpec((tq, 2 * V_DIM), lambda p, j: (j, p)),
                  pl.BlockSpec((t, 2 * V_DIM), lambda p, j: (0, p)),
                  pl.BlockSpec((t, 2 * HEAD_PAD), lambda p, j: (0, p)),
                  pl.BlockSpec((t, 2 * V_DIM), lambda p, j: (0, p))] + [hbm] * n,
        out_specs=[pl.BlockSpec((t, 2 * HEAD_PAD), lambda p, j: (0, p)),
                   pl.BlockSpec((tq, 2 * HEAD_PAD), lambda p, j: (j, p)),
                   pl.BlockSpec((tq, 2 * V_DIM), lambda p, j: (j, p))] + [hbm] * n,
        out_shape=[jax.ShapeDtypeStruct((t, HEADS * HEAD_PAD), F32), jax.ShapeDtypeStruct((t, HEADS * HEAD_PAD), F32),
                   jax.ShapeDtypeStruct((t, HEADS * V_DIM), F32)]
        + [jax.ShapeDtypeStruct(src.shape if scatter else (N_DEV,) + src.shape, src.dtype) for src, scatter in carried],
        scratch_shapes=_comm_sems() * n,
        compiler_params=_cp(("arbitrary", "arbitrary") if n else ("parallel", "arbitrary")),
    )(q, k, v, o, lse, do, *[src for src, _ in carried])


def mla_proj_bwd(dq, dk, dv, z, ca, sb, sc, gq, gkv, wuq, wukv):
    t = z.shape[0]
    tm = min(TM, t)

    def body(dq_ref, dk_ref, dv_ref, z_ref, ca_ref, sb_ref, sc_ref, gq_ref, gkv_ref, wuq_ref, wukv_ref,
             dz_ref, cq_ref, ckv_ref, dqp_ref, dkvp_ref, dgq_ref, dgkv_ref):
        first = pl.program_id(0) == 0
        z = z_ref[...]
        ca, sb, sc = ca_ref[...], sb_ref[...], sc_ref[...]
        gq, gkv = gq_ref[...], gkv_ref[...]
        cq, cqh, rq = _rms_fwd(z[:, :Q_RANK], gq)
        ckv, ckvh, rkv = _rms_fwd(z[:, Q_RANK:Q_RANK + KV_RANK], gkv)
        lane = lax.broadcasted_iota(jnp.int32, (tm, HEAD_PAD), 1)
        dkr = jnp.zeros((tm, HEAD_PAD), F32)
        for h in range(HEADS):
            lanes = slice(h * HEAD_PAD, (h + 1) * HEAD_PAD)
            dqp_ref[:, lanes] = _rope_t(dq_ref[:, lanes], ca, sb, sc).astype(BF16)
            dkh = dk_ref[:, lanes]
            dkr = dkr + dkh
            dkvp_ref[:, lanes] = jnp.where(lane < NOPE, dkh, 0.0).astype(BF16)
        dkvp_ref[:, HEADS * HEAD_PAD:] = dv_ref[...].astype(BF16)
        dkr = pltpu.roll(_rope_t(jnp.where(lane >= NOPE, dkr, 0.0), ca, sb, sc), HEAD_PAD - NOPE, 1)
        dkr = jnp.where(lane < ROPE, dkr, 0.0)
        dcq = _mm(dqp_ref[...], wuq_ref[...])
        dckv = _mm(dkvp_ref[...], wukv_ref[...])
        dzq, dgq = _rms_bwd(cqh, rq, gq, dcq)
        dzkv, dgkv = _rms_bwd(ckvh, rkv, gkv, dckv)
        dz_ref[...] = jnp.concatenate([dzq, dzkv, dkr], axis=1).astype(BF16)
        cq_ref[...] = cq.astype(BF16)
        ckv_ref[...] = ckv.astype(BF16)
        _acc(dgq_ref, first, dgq)
        _acc(dgkv_ref, first, dgkv)

    row = lambda w: pl.BlockSpec((tm, w), lambda i: (i, 0))
    vec = lambda w: pl.BlockSpec((1, w), lambda i: (0, 0))
    return pl.pallas_call(
        body, name="mla_proj_bwd", grid=(t // tm,),
        in_specs=[row(1024), row(1024), row(512), pl.BlockSpec((tm, 768), lambda i: (i, 1)),
                  row(HEAD_PAD), row(HEAD_PAD), row(HEAD_PAD), _whole(), _whole(), _whole(), _whole()],
        out_specs=[row(768), row(Q_RANK), row(KV_RANK), row(1024), row(1536), vec(Q_RANK), vec(KV_RANK)],
        out_shape=[jax.ShapeDtypeStruct((t, 768), BF16), jax.ShapeDtypeStruct((t, Q_RANK), BF16),
                   jax.ShapeDtypeStruct((t, KV_RANK), BF16), jax.ShapeDtypeStruct((t, 1024), BF16),
                   jax.ShapeDtypeStruct((t, 1536), BF16), jax.ShapeDtypeStruct((1, Q_RANK), F32),
                   jax.ShapeDtypeStruct((1, KV_RANK), F32)],
        compiler_params=_cp(("arbitrary",)),
    )(dq, dk, dv, z, ca, sb, sc, gq, gkv, wuq, wukv)


def pre_in_bwd(x, dx1, dzcv, dzmla, dzsg, g, w):
    t = x.shape[0]
    tm = min(TM, t)

    def body(x_ref, dx1_ref, dzcv_ref, dzmla_ref, dzsg_ref, g_ref, w_ref, dx_ref, h_ref, dz_ref, dg_ref):
        g = g_ref[...]
        h, xh, r = _rms_fwd(x_ref[...], g)
        dz = jnp.concatenate([dzcv_ref[...], dzmla_ref[...], dzsg_ref[...]], axis=1)
        dx, dg = _rms_bwd(xh, r, g, _mm(dz, w_ref[...]))
        dx_ref[...] = dx1_ref[...] + dx
        h_ref[...] = h.astype(BF16)
        dz_ref[...] = dz
        _acc(dg_ref, pl.program_id(0) == 0, dg)

    row = lambda w_: pl.BlockSpec((tm, w_), lambda i: (i, 0))
    return pl.pallas_call(
        body, name="pre_in_bwd", grid=(t // tm,),
        in_specs=[row(D), row(D), row(768), row(768), row(512), _whole(), _whole()],
        out_specs=[row(D), row(D), row(Z_W), pl.BlockSpec((1, D), lambda i: (0, 0))],
        out_shape=[jax.ShapeDtypeStruct((t, D), F32), jax.ShapeDtypeStruct((t, D), BF16),
                   jax.ShapeDtypeStruct((t, Z_W), BF16), jax.ShapeDtypeStruct((1, D), F32)],
        compiler_params=_cp(("arbitrary",)),
    )(x, dx1, dzcv, dzmla, dzsg, g, w)


MESH = pl.DeviceIdType.MESH


def _place():
    return lax.axis_index("x"), lax.axis_index("y"), lax.axis_index("c")


def _comm_sems():
    return [pltpu.SemaphoreType.DMA((7,)), pltpu.SemaphoreType.DMA((7,)), pltpu.SemaphoreType.DMA]


def _gather_steps(x_ref, out_ref, send_sems, recv_sems, local_sem):
    x, y, c = _place()
    me, sibling = (x, y, c), (x, y, 1 - c)
    chips = [(1 - x, y), (x, 1 - y), (1 - x, 1 - y)]

    def slot(px, py, pc):
        return out_ref.at[4 * px + 2 * py + pc]

    def copy(k, blk, to, src=None):
        return pltpu.make_async_remote_copy(
            src_ref=slot(*blk) if src is None else src, dst_ref=slot(*blk),
            send_sem=send_sems.at[k], recv_sem=recv_sems.at[k], device_id=to, device_id_type=MESH)

    mine = pltpu.make_async_copy(x_ref, slot(*me), local_sem)
    first = [copy(0, me, sibling, src=x_ref)] + [copy(1 + j, me, (*chip, c), src=x_ref) for j, chip in enumerate(chips)]
    passed = [copy(4 + j, (*chip, c), sibling) for j, chip in enumerate(chips)]

    def start():
        mine.start()
        for cp in first:
            cp.start()

    def forward():
        for j, chip in enumerate(chips):
            copy(1 + j, (*chip, c), me).wait_recv()
            passed[j].start()

    def finish():
        copy(0, sibling, me).wait_recv()
        for j, chip in enumerate(chips):
            copy(4 + j, (*chip, 1 - c), me).wait_recv()
        for cp in first + passed:
            cp.wait_send()
        mine.wait()

    return start, forward, finish


def _exchange_steps(src_ref, out_ref, scatter, send_sems, recv_sems, local_sem):
    x, y, c = _place()
    me = 4 * x + 2 * y + c
    own = pltpu.make_async_copy(src_ref.at[me] if scatter else src_ref, out_ref.at[me], local_sem)
    copies = []
    for k in range(1, N_DEV):
        px = 1 - x if k & 4 else x
        py = 1 - y if k & 2 else y
        pc = 1 - c if k & 1 else c
        copies.append(pltpu.make_async_remote_copy(
            src_ref=src_ref.at[4 * px + 2 * py + pc] if scatter else src_ref, dst_ref=out_ref.at[me],
            send_sem=send_sems.at[k - 1], recv_sem=recv_sems.at[k - 1], device_id=(px, py, pc), device_id_type=MESH))

    def start():
        own.start()
        for cp in copies:
            cp.start()

    def finish():
        for cp in copies:
            cp.wait_recv()
        for cp in copies:
            cp.wait_send()
        own.wait()

    return start, finish


def all_gather(block):
    def body(x_ref, out_ref, *sems):
        for stage in _gather_steps(x_ref, out_ref, *sems):
            stage()

    return pl.pallas_call(
        body, name="all_gather",
        in_specs=[pl.BlockSpec(memory_space=pl.ANY)],
        out_specs=pl.BlockSpec(memory_space=pl.ANY),
        out_shape=jax.ShapeDtypeStruct((N_DEV,) + block.shape, block.dtype),
        scratch_shapes=_comm_sems(),
    )(block)


def grad_exchange(chunks, small):
    def body(chunks_ref, small_ref, got_ref, gots_ref, *sems):
        start_c, finish_c = _exchange_steps(chunks_ref, got_ref, True, *sems[:3])
        start_s, finish_s = _exchange_steps(small_ref, gots_ref, False, *sems[3:])
        start_c()
        start_s()
        finish_c()
        finish_s()

    return pl.pallas_call(
        body, name="grad_exchange",
        in_specs=[pl.BlockSpec(memory_space=pl.ANY), pl.BlockSpec(memory_space=pl.ANY)],
        out_specs=[pl.BlockSpec(memory_space=pl.ANY), pl.BlockSpec(memory_space=pl.ANY)],
        out_shape=[jax.ShapeDtypeStruct(chunks.shape, chunks.dtype),
                   jax.ShapeDtypeStruct((N_DEV,) + small.shape, small.dtype)],
        scratch_shapes=_comm_sems() + _comm_sems(),
    )(chunks, small)


def _row_tile(r, cap):
    return max(d for d in range(16, cap + 1, 16) if r % d == 0)


def sum_adamw(parts, w, m, v, cap):
    r, c = w.shape
    tr = _row_tile(r, cap)
    c1 = 1.0 / (1.0 - ADAM_B1 ** ADAM_STEP)
    c2 = 1.0 / (1.0 - ADAM_B2 ** ADAM_STEP)

    def body(p_ref, w_ref, m_ref, v_ref, g_ref, d_ref, nm_ref, nv_ref):
        g = p_ref[0].astype(F32)
        for k in range(1, N_DEV):
            g = g + p_ref[k].astype(F32)
        m = ADAM_B1 * m_ref[...] + (1.0 - ADAM_B1) * g
        v = ADAM_B2 * v_ref[...] + (1.0 - ADAM_B2) * (g * g)
        g_ref[...] = g
        nm_ref[...] = m
        nv_ref[...] = v
        d_ref[...] = -ADAM_LR * ((m * c1) / (jnp.sqrt(v * c2) + ADAM_EPS) + ADAM_WD * w_ref[...])

    blk = pl.BlockSpec((tr, c), lambda i: (i, 0))
    out = jax.ShapeDtypeStruct((r, c), F32)
    return pl.pallas_call(
        body, name="sum_adamw", grid=(r // tr,),
        in_specs=[pl.BlockSpec((N_DEV, tr, c), lambda i: (0, i, 0)), blk, blk, blk],
        out_specs=[blk, blk, blk, blk], out_shape=[out, out, out, out],
        compiler_params=_cp(("parallel",)),
    )(parts, w, m, v)


PACK_W = 1024
MIX_PIECES = (("w_out", D // N_DEV, D, False), ("w_uq", HEADS * (NOPE + ROPE) // N_DEV, Q_RANK, True),
              ("w_ukv", HEADS * (NOPE + V_DIM) // N_DEV, KV_RANK, True), ("conv", 16, PACK_W, False),
              ("w_in", IN_W // N_DEV, D, True))
FFN_PIECES = (("w_gate", D_FF // N_DEV, D, True), ("w_up", D_FF // N_DEV, D, True), ("w_down", D_FF // N_DEV, D, False))
OFFSET = {}
for _pieces in (MIX_PIECES, FFN_PIECES):
    _off = 0
    for _name, _rows, _, _ in _pieces:
        OFFSET[_name] = _off
        _off += _rows + -_rows % 16
assert all(o % 16 == 0 for o in OFFSET.values())
assert [OFFSET[n] for n in ("w_gate", "w_up", "w_down")] == [0, FFN_SHARD, 2 * FFN_SHARD]
CONV_BITS = 3 * (CV_W // N_DEV) * 2


def _to_pack(shards, dtype, pieces, conv=None):
    nl = shards["w_in"].shape[0]
    parts = []
    for name, rows, cols, transposed in pieces:
        if name == "conv":
            if conv is None:
                a = jnp.zeros((nl, rows, PACK_W), dtype)
            else:
                bits = lax.bitcast_convert_type(conv.astype(F32), BF16).reshape(nl, CONV_BITS)
                a = jnp.pad(bits, ((0, 0), (0, rows * PACK_W - CONV_BITS))).reshape(nl, rows, PACK_W)
        else:
            a = shards[name].astype(dtype)
            a = jnp.swapaxes(a, 1, 2) if transposed else a
            a = jnp.pad(a, ((0, 0), (0, -rows % 16), (0, PACK_W - cols)))
        parts.append(a)
    return jnp.concatenate(parts, axis=1)


def _from_pack(pack, pieces):
    out = {}
    for name, rows, cols, transposed in pieces:
        if name != "conv":
            a = pack[:, OFFSET[name]:OFFSET[name] + rows, :cols]
            out[name] = jnp.swapaxes(a, 1, 2) if transposed else a
    return out


def _mix_weights(g):
    def rows(name):
        _, n, cols, _ = next(p for p in MIX_PIECES if p[0] == name)
        return g[:, OFFSET[name]:OFFSET[name] + n, :cols]

    w_in_t = rows("w_in").reshape(IN_W, D)
    w_in_p = jnp.concatenate([w_in_t[1184:], w_in_t[:672], jnp.zeros((96, D), BF16), w_in_t[672:1184]], axis=0)
    w_uq_p = jnp.pad(rows("w_uq"), ((0, 0), (0, HEAD_PAD - NOPE - ROPE), (0, 0))).reshape(HEADS * HEAD_PAD, Q_RANK)
    kv = rows("w_ukv")
    w_k = jnp.pad(kv[:, :NOPE], ((0, 0), (0, HEAD_PAD - NOPE), (0, 0))).reshape(HEADS * HEAD_PAD, KV_RANK)
    w_ukv_p = jnp.concatenate([w_k, kv[:, NOPE:].reshape(HEADS * V_DIM, KV_RANK)], axis=0)
    bits = rows("conv").reshape(N_DEV, -1)[:, :CONV_BITS].reshape(N_DEV, 3, CV_W // N_DEV, 2)
    conv_w = jnp.moveaxis(lax.bitcast_convert_type(bits, F32), 0, 1).reshape(3, CV_W)
    return dict(w_in=w_in_p, w_uq=w_uq_p, w_ukv=w_ukv_p, w_out=rows("w_out").reshape(D, D), conv_w=conv_w)


def _grad_chunks(full):
    if "w_in" in full:
        pieces = MIX_PIECES
        d_in = full["w_in"]
        d_in = jnp.concatenate([d_in[768:768 + 672], d_in[1536:], d_in[:768]], axis=0)
        d_uq = full["w_uq"].reshape(HEADS, HEAD_PAD, Q_RANK)[:, :NOPE + ROPE]
        d_k = full["w_ukv"][:HEADS * HEAD_PAD].reshape(HEADS, HEAD_PAD, KV_RANK)[:, :NOPE]
        d_v = full["w_ukv"][HEADS * HEAD_PAD:].reshape(HEADS, V_DIM, KV_RANK)
        mats = dict(w_in=d_in, w_uq=d_uq, w_ukv=jnp.concatenate([d_k, d_v], axis=1), w_out=full["w_out"])
    else:
        pieces = FFN_PIECES
        mats = dict(w_gate=full["w_gu"][:D_FF], w_up=full["w_gu"][D_FF:], w_down=full["w_down"])
    parts = []
    for name, rows, cols, _ in pieces:
        if name == "conv":
            parts.append(jnp.zeros((N_DEV, rows, PACK_W), BF16))
        else:
            parts.append(jnp.pad(mats[name].reshape(N_DEV, rows, cols), ((0, 0), (0, -rows % 16), (0, PACK_W - cols))))
    return jnp.concatenate(parts, axis=1)


SMALL = (("mix_pre_g", (D,)), ("mix_post_g", (D,)), ("ffn_pre_g", (D,)), ("ffn_post_g", (D,)), ("q_norm_g", (Q_RANK,)),
         ("kv_norm_g", (KV_RANK,)), ("sg_ln_g", (SG_W,)), ("sg_ln_b", (SG_W,)), ("w_sp", (4, CHUNK, CHUNK)),
         ("b_sp", (4, CHUNK)), ("out_norm_g", (D,)))
SMALL_ROWS = 576


def _pack_small(vals, nl):
    flat = jnp.concatenate([vals[name].reshape(nl, -1) for name, _ in SMALL] + [vals["conv_w"].reshape(nl, -1)], axis=1)
    return jnp.pad(flat, ((0, 0), (0, SMALL_ROWS * 128 - flat.shape[1]))).reshape(nl * SMALL_ROWS, 128)


def _unpack_small(pack, nl):
    flat = pack.reshape(nl, SMALL_ROWS * 128)
    out, off = {}, 0
    for name, shape in SMALL + (("conv_w", (3, CV_W)),):
        n = int(np.prod(shape))
        out[name] = flat[:, off:off + n].reshape((nl,) + shape)
        off += n
    return out


def _layer_fwd(x, lw, sp, tabs, consts, next_pack):
    ca, sb, sc = tabs
    z = pre_in_fwd(x, sp["mix_pre_g"], lw["w_in"])
    q, k, v = mla_proj_fwd(z, ca, sb, sc, sp["q_norm_g"], sp["kv_norm_g"], lw["w_uq"], lw["w_ukv"])
    ya, lse, ffn_gathered, *mix_gathered = attn_fwd(q, k, v, next_pack)
    lw["ffn"] = ffn_gathered
    x1 = mix_fwd(x, z, ya, consts["gm"], sp["sg_ln_g"], sp["sg_ln_b"], sp["w_sp"], sp["bias"], lw["conv_w"],
                 sp["out_norm_g"], lw["w_out"], sp["mix_post_g"])
    x2, f = ffn_fwd(x1, sp["ffn_pre_g"], lw["ffn"], sp["ffn_post_g"])
    return x2, (x, z, q, k, v, ya, lse, x1, f), mix_gathered


def _layer_bwd(dx2, saved, lw, sp, tabs, consts, pending):
    ca, sb, sc = tabs
    x, z, q, k, v, ya, lse, x1, f = saved
    dx1, h2, dab, s, df, d_ffn_pre, d_ffn_post = ffn_bwd(x1, f, dx2, sp["ffn_pre_g"], lw["ffn"], sp["ffn_post_g"])
    ffn_chunks = _grad_chunks(dict(w_gu=atb(dab, h2, 1408), w_down=atb(s, df, 1408)))
    dya, dyc, dzsg, mix, do, d_mix_post, d_out_norm, d_lng, d_lnb, d_wsp, d_bias = mix_bwd(
        dx1, z, ya, consts["gm"], sp["sg_ln_g"], sp["sg_ln_b"], sp["w_sp"], sp["w_sp_t"], sp["bias"], lw["conv_w"],
        sp["out_norm_g"], lw["w_out"], sp["mix_post_g"])
    d_w_out = atb(mix, do, 1024)
    dzcv, d_cw = conv_bwd(dyc, z, lw["conv_w"])
    dq, dk, dv, *received = attn_bwd(q, k, v, ya, lse, dya, ((ffn_chunks, True),) + pending)
    dzmla, cq, ckv, dqp, dkvp, d_gq, d_gkv = mla_proj_bwd(dq, dk, dv, z, ca, sb, sc, sp["q_norm_g"], sp["kv_norm_g"],
                                                          lw["w_uq"], lw["w_ukv"])
    d_w_uq = atb(dqp, cq, 1024)
    d_w_ukv = atb(dkvp, ckv, 1536)
    dx, h1, dz, d_mix_pre = pre_in_bwd(x, dx1, dzcv, dzmla, dzsg, sp["mix_pre_g"], lw["w_in"])
    d_w_in = atb(dz, h1, 2048)
    mix_chunks = _grad_chunks(dict(w_in=d_w_in, w_uq=d_w_uq, w_ukv=d_w_ukv, w_out=d_w_out))
    d_bsp = d_bias[:, ::GROUP].T
    small = dict(mix_pre_g=d_mix_pre[0], mix_post_g=d_mix_post[0], ffn_pre_g=d_ffn_pre[0], ffn_post_g=d_ffn_post[0],
                 q_norm_g=d_gq[0], kv_norm_g=d_gkv[0], sg_ln_g=d_lng[0], sg_ln_b=d_lnb[0], w_sp=d_wsp, b_sp=d_bsp,
                 out_norm_g=d_out_norm[0], conv_w=d_cw[:3])
    small_pack = _pack_small({name: a[None] for name, a in small.items()}, 1)
    return dx, ((mix_chunks, True), (small_pack, False)), received


def kernel(x, positions, mix_pre_g, mix_post_g, ffn_pre_g, ffn_post_g, w_in, q_norm_g, w_uq, kv_norm_g, w_ukv, sg_ln_g, sg_ln_b, w_sp, b_sp, conv_w, out_norm_g, w_out, w_gate, w_up, w_down, loss_target, m_mix_pre_g, m_mix_post_g, m_ffn_pre_g, m_ffn_post_g, m_w_in, m_q_norm_g, m_w_uq, m_kv_norm_g, m_w_ukv, m_sg_ln_g, m_sg_ln_b, m_w_sp, m_b_sp, m_conv_w, m_out_norm_g, m_w_out, m_w_gate, m_w_up, m_w_down, v_mix_pre_g, v_mix_post_g, v_ffn_pre_g, v_ffn_post_g, v_w_in, v_q_norm_g, v_w_uq, v_kv_norm_g, v_w_ukv, v_sg_ln_g, v_sg_ln_b, v_w_sp, v_b_sp, v_conv_w, v_out_norm_g, v_w_out, v_w_gate, v_w_up, v_w_down):
    nl = w_in.shape[0]
    t = x.shape[1]
    w = dict(mix_pre_g=mix_pre_g, mix_post_g=mix_post_g, ffn_pre_g=ffn_pre_g, ffn_post_g=ffn_post_g, w_in=w_in,
             q_norm_g=q_norm_g, w_uq=w_uq, kv_norm_g=kv_norm_g, w_ukv=w_ukv, sg_ln_g=sg_ln_g, sg_ln_b=sg_ln_b, w_sp=w_sp,
             b_sp=b_sp, conv_w=conv_w, out_norm_g=out_norm_g, w_out=w_out, w_gate=w_gate, w_up=w_up, w_down=w_down)
    m = dict(mix_pre_g=m_mix_pre_g, mix_post_g=m_mix_post_g, ffn_pre_g=m_ffn_pre_g, ffn_post_g=m_ffn_post_g, w_in=m_w_in,
             q_norm_g=m_q_norm_g, w_uq=m_w_uq, kv_norm_g=m_kv_norm_g, w_ukv=m_w_ukv, sg_ln_g=m_sg_ln_g, sg_ln_b=m_sg_ln_b,
             w_sp=m_w_sp, b_sp=m_b_sp, conv_w=m_conv_w, out_norm_g=m_out_norm_g, w_out=m_w_out, w_gate=m_w_gate,
             w_up=m_w_up, w_down=m_w_down)
    v = dict(mix_pre_g=v_mix_pre_g, mix_post_g=v_mix_post_g, ffn_pre_g=v_ffn_pre_g, ffn_post_g=v_ffn_post_g, w_in=v_w_in,
             q_norm_g=v_q_norm_g, w_uq=v_w_uq, kv_norm_g=v_kv_norm_g, w_ukv=v_w_ukv, sg_ln_g=v_sg_ln_g, sg_ln_b=v_sg_ln_b,
             w_sp=v_w_sp, b_sp=v_b_sp, conv_w=v_conv_w, out_norm_g=v_out_norm_g, w_out=v_w_out, w_gate=v_w_gate,
             w_up=v_w_up, w_down=v_w_down)

    mix_pack = _to_pack(w, BF16, MIX_PIECES, conv=w["conv_w"])
    ffn_pack = _to_pack(w, BF16, FFN_PIECES)
    consts = dict(gm=jnp.asarray(np.kron(np.eye(SG_W // GROUP), np.full((GROUP, GROUP), 1.0 / GROUP)), BF16))
    smalls = []
    for l in range(nl):
        sp = {name: w[name][l].reshape(1, -1) for name, shape in SMALL if len(shape) == 1}
        sp["w_sp"] = w["w_sp"][l]
        sp["w_sp_t"] = jnp.swapaxes(w["w_sp"][l], 1, 2)
        sp["bias"] = jnp.repeat(w["b_sp"][l].T, GROUP, axis=1)
        smalls.append(sp)
    inv_freq = 1.0 / (ROPE_THETA ** (jnp.arange(0, ROPE // 2, dtype=F32) / (ROPE // 2)))
    inv = jnp.zeros((1, HEAD_PAD), F32).at[0, NOPE:NOPE + ROPE].set(jnp.concatenate([inv_freq, inv_freq]))
    tabs = rope_tables(positions.reshape(t, 1).astype(F32), inv)

    h = x[0]
    saved, layers = [], []
    mix_gathered = [all_gather(mix_pack[0])]
    for l in range(nl):
        layers.append(_mix_weights(mix_gathered[0]))
        carried = (ffn_pack[l],) + ((mix_pack[l + 1],) if l + 1 < nl else ())
        h, s, mix_gathered = _layer_fwd(h, layers[l], smalls[l], tabs, consts, carried)
        saved.append(s)
    sq, dh = loss_head(h, loss_target[0])
    loss = lax.psum(0.5 * sq[0, 0] / D, ("x", "y", "c"))

    got_ffn, got_mix, got_small = [None] * nl, [None] * nl, [None] * nl
    pending = ()
    for l in reversed(range(nl)):
        dh, new_pending, received = _layer_bwd(dh, saved[l], layers[l], smalls[l], tabs, consts, pending)
        got_ffn[l] = received[0]
        if pending:
            got_mix[l + 1], got_small[l + 1] = received[1:]
        pending = new_pending
    got_mix[0], got_small[0] = grad_exchange(pending[0][0], pending[1][0])
    got_small = jnp.concatenate(got_small, axis=1)

    me = 4 * lax.axis_index("x") + 2 * lax.axis_index("y") + lax.axis_index("c")
    g_big, d_big, m_big, v_big = {}, {}, {}, {}
    for pieces, got in ((MIX_PIECES, got_mix), (FFN_PIECES, got_ffn)):
        w_f32, m_f32, v_f32 = [_to_pack(d, F32, pieces) for d in (w, m, v)]
        per_layer = [sum_adamw(got[l], w_f32[l], m_f32[l], v_f32[l], 352) for l in range(nl)]
        for i, out in enumerate((g_big, d_big, m_big, v_big)):
            out.update(_from_pack(jnp.stack([per_layer[l][i] for l in range(nl)]), pieces))

    def full_conv(a):
        return lax.dynamic_update_slice(jnp.zeros((nl, 3, CV_W), F32), a, (0, 0, me * (CV_W // N_DEV)))

    def small_pack(d):
        return _pack_small({**{name: d[name] for name, _ in SMALL}, "conv_w": full_conv(d["conv_w"])}, nl)

    g_small, d_small, m_small, v_small = [_unpack_small(p, nl) for p in
                                          sum_adamw(got_small, small_pack(w), small_pack(m), small_pack(v), 1152)]
    outs = []
    for big, small in ((g_big, g_small), (d_big, d_small), (m_big, m_small), (v_big, v_small)):
        for name in w:
            if name == "conv_w":
                outs.append(lax.dynamic_slice(small[name], (0, 0, me * (CV_W // N_DEV)), (nl, 3, CV_W // N_DEV)))
            elif name in small:
                outs.append(small[name])
            else:
                outs.append(big[name])
    return (loss, dh[None], *outs)
```

```python
import functools

import jax
import jax.numpy as jnp
import numpy as np
from jax import lax
from jax.experimental import pallas as pl
from jax.experimental.pallas import tpu as pltpu

F32 = jnp.float32
BF16 = jnp.bfloat16

D = 1024
Q_RANK = 384
KV_RANK = 256
ROPE = 32
HEADS = 8
NOPE = 64
V_DIM = 64
HEAD_PAD = 128
SG_W = 256
CV_W = 256
CHUNK = 128
GROUP = 64
D_FF = 2816
IN_W = 1952
Z_W = 2048
Z_CV, Z_MLA, Z_SG = 0, 768, 1536
EPS = 1e-6
ROPE_THETA = 10000.0
SCALE = (NOPE + ROPE) ** -0.5
LOG2E = 1.4426950408889634
SCALE_LOG2E = SCALE * LOG2E
NEG = -1e30
N_DEV = 8

ADAM_LR, ADAM_B1, ADAM_B2, ADAM_EPS, ADAM_WD, ADAM_STEP = 0.001, 0.9, 0.999, 1e-08, 0.01, 10

VMEM_LIMIT = 56 * 1024 * 1024

TM = 512
TM_FFN = 256
FFN_SLAB = 256
TQ = 512
TT = 2048


def _cp(sem, vmem=VMEM_LIMIT):
    return pltpu.CompilerParams(dimension_semantics=sem, vmem_limit_bytes=vmem)


def _whole():
    return pl.BlockSpec(memory_space=pltpu.VMEM)


def _mm(a, b):
    return jnp.dot(a, b, preferred_element_type=F32)


def _mm_nt(a, b):
    return lax.dot_general(a, b, (((1,), (1,)), ((), ())), preferred_element_type=F32)


def _mm_tn(a, b):
    return lax.dot_general(a, b, (((0,), (0,)), ((), ())), preferred_element_type=F32)


def _rms_fwd(x, g):
    r = lax.rsqrt(jnp.mean(x * x, axis=-1, keepdims=True) + EPS)
    xh = x * r
    return xh * g, xh, r


def _rms_bwd(xh, r, g, dy):
    dxh = dy * g
    dx = r * (dxh - xh * jnp.mean(dxh * xh, axis=-1, keepdims=True))
    dg = jnp.sum(dy * xh, axis=0, keepdims=True)
    return dx, dg


def _gmean(v, gm):
    hi = v.astype(BF16)
    lo = (v - hi.astype(F32)).astype(BF16)
    return _mm(hi, gm) + _mm(lo, gm)


def _gelu(x):
    c = np.float32(np.sqrt(2.0 / np.pi))
    u = c * (x + 0.044715 * x * x * x)
    t = jnp.tanh(u)
    return 0.5 * x * (1.0 + t), t


def _gelu_grad(x, t):
    c = np.float32(np.sqrt(2.0 / np.pi))
    return 0.5 * (1.0 + t) + 0.5 * x * (1.0 - t * t) * c * (1.0 + 3.0 * 0.044715 * x * x)


def _rope(t, ca, sb, sc):
    return t * ca + pltpu.roll(t, HEAD_PAD - 16, 1) * sb + pltpu.roll(t, 16, 1) * sc


def _rope_t(dt, ca, sb, sc):
    return dt * ca + pltpu.roll(dt * sb, 16, 1) + pltpu.roll(dt * sc, HEAD_PAD - 16, 1)


def _shift_down(y, k, head):
    n = y.shape[0]
    out = pltpu.roll(y, k, 0)
    row = lax.broadcasted_iota(jnp.int32, y.shape, 0)
    for j in range(k):
        out = jnp.where(row == j, head[8 - k + j:8 - k + j + 1, :], out)
    return out


def _shift_up(y, k, tail):
    n = y.shape[0]
    out = pltpu.roll(y, n - k, 0)
    row = lax.broadcasted_iota(jnp.int32, y.shape, 0)
    for j in range(k):
        out = jnp.where(row == n - k + j, tail[j:j + 1, :], out)
    return out


def rope_tables(pos, inv):
    t = pos.shape[0]
    tm = min(TM, t)

    def body(pos_ref, inv_ref, ca_ref, sb_ref, sc_ref):
        ang = pos_ref[...] * inv_ref[...]
        c = jnp.cos(ang)
        s = jnp.sin(ang)
        lane = lax.broadcasted_iota(jnp.int32, ang.shape, 1)
        ca_ref[...] = jnp.where(lane < NOPE, 1.0, jnp.where(lane < NOPE + ROPE, c, 0.0))
        sb_ref[...] = jnp.where((lane >= NOPE) & (lane < NOPE + 16), -s, 0.0)
        sc_ref[...] = jnp.where((lane >= NOPE + 16) & (lane < NOPE + ROPE), s, 0.0)

    out = jax.ShapeDtypeStruct((t, HEAD_PAD), F32)
    blk = pl.BlockSpec((tm, HEAD_PAD), lambda i: (i, 0))
    return pl.pallas_call(
        body, name="rope_tables", grid=(t // tm,),
        in_specs=[pl.BlockSpec((tm, 1), lambda i: (i, 0)), pl.BlockSpec((1, HEAD_PAD), lambda i: (0, 0))],
        out_specs=[blk, blk, blk], out_shape=[out, out, out],
        compiler_params=_cp(("parallel",)),
    )(pos, inv)


def pre_in_fwd(x, g, w):
    t = x.shape[0]
    tm = min(TM, t)

    def body(x_ref, g_ref, w_ref, z_ref):
        h, _, _ = _rms_fwd(x_ref[...], g_ref[...])
        z_ref[...] = _mm_nt(h.astype(BF16), w_ref[...])

    return pl.pallas_call(
        body, name="pre_in_fwd", grid=(t // tm,),
        in_specs=[pl.BlockSpec((tm, D), lambda i: (i, 0)), _whole(), _whole()],
        out_specs=pl.BlockSpec((tm, Z_W), lambda i: (i, 0)),
        out_shape=jax.ShapeDtypeStruct((t, Z_W), F32),
        compiler_params=_cp(("parallel",)),
    )(x, g, w)


def mla_proj_fwd(z, ca, sb, sc, gq, gkv, wuq, wukv):
    t = z.shape[0]
    tm = min(TM, t)

    def body(z_ref, ca_ref, sb_ref, sc_ref, gq_ref, gkv_ref, wuq_ref, wukv_ref, q_ref, k_ref, v_ref):
        z = z_ref[...]
        ca, sb, sc = ca_ref[...], sb_ref[...], sc_ref[...]
        cq, _, _ = _rms_fwd(z[:, :Q_RANK], gq_ref[...])
        ckv, _, _ = _rms_fwd(z[:, Q_RANK:Q_RANK + KV_RANK], gkv_ref[...])
        q = _mm_nt(cq.astype(BF16), wuq_ref[...])
        kv = _mm_nt(ckv.astype(BF16), wukv_ref[...])
        kr = _rope(pltpu.roll(z[:, Q_RANK + KV_RANK:], NOPE, 1), ca, sb, sc)
        for h in range(HEADS):
            lanes = slice(h * HEAD_PAD, (h + 1) * HEAD_PAD)
            q_ref[:, lanes] = _rope(q[:, lanes], ca, sb, sc).astype(BF16)
            k_ref[:, lanes] = (kv[:, lanes] + kr).astype(BF16)
        v_ref[...] = kv[:, HEADS * HEAD_PAD:].astype(BF16)

    tab = pl.BlockSpec((tm, HEAD_PAD), lambda i: (i, 0))
    return pl.pallas_call(
        body, name="mla_proj_fwd", grid=(t // tm,),
        in_specs=[pl.BlockSpec((tm, 768), lambda i: (i, 1)), tab, tab, tab, _whole(), _whole(), _whole(), _whole()],
        out_specs=[pl.BlockSpec((tm, HEADS * HEAD_PAD), lambda i: (i, 0)),
                   pl.BlockSpec((tm, HEADS * HEAD_PAD), lambda i: (i, 0)),
                   pl.BlockSpec((tm, HEADS * V_DIM), lambda i: (i, 0))],
        out_shape=[jax.ShapeDtypeStruct((t, HEADS * HEAD_PAD), BF16),
                   jax.ShapeDtypeStruct((t, HEADS * HEAD_PAD), BF16),
                   jax.ShapeDtypeStruct((t, HEADS * V_DIM), BF16)],
        compiler_params=_cp(("parallel",)),
    )(z, ca, sb, sc, gq, gkv, wuq, wukv)


def _each(stages, k):
    def run():
        for stage in stages:
            stage[k]()
    return run


def attn_fwd(q, k, v, carried=()):
    t = q.shape[0]
    tq = min(TQ, t)
    nq = t // tq
    last_pair = HEADS // 2 - 1
    n = len(carried)

    def body(*refs):
        q_ref, k_ref, v_ref = refs[:3]
        o_ref, lse_ref = refs[3 + n:5 + n]
        sems = refs[5 + 2 * n:]
        stages = [_gather_steps(refs[3 + a], refs[5 + n + a], *sems[3 * a:3 * a + 3]) for a in range(n)]
        if n:
            pair = pl.program_id(0)
            pl.when((pair == 0) & (pl.program_id(1) == 0))(_each(stages, 0))
            pl.when((pair == last_pair) & (pl.program_id(1) == 0))(_each(stages, 1))
        i = pl.program_id(1)
        row = lax.broadcasted_iota(jnp.int32, (tq, tq), 0)
        col = lax.broadcasted_iota(jnp.int32, (tq, tq), 1)
        head_lanes = [slice(h * HEAD_PAD, (h + 1) * HEAD_PAD) for h in range(2)]

        def step(j, carry, masked):
            start = pl.multiple_of(j * tq, tq)
            vb = v_ref[pl.ds(start, tq), :]
            out = []
            for h in range(2):
                m, l, acc = carry[h]
                s = _mm_nt(q_ref[:, head_lanes[h]], k_ref[pl.ds(start, tq), head_lanes[h]])
                if masked:
                    s = jnp.where(col <= row, s, NEG)
                m_new = jnp.maximum(m, jnp.max(s, axis=-1, keepdims=True))
                p = jnp.exp2((s - m_new) * SCALE_LOG2E)
                alpha = jnp.exp2((m - m_new) * SCALE_LOG2E)
                l = alpha * l + jnp.sum(p, axis=-1, keepdims=True)
                acc = alpha * acc + _mm(p.astype(BF16), vb)
                out.append((m_new, l, acc))
            return tuple(out)

        init = (jnp.full((tq, 1), NEG, F32), jnp.zeros((tq, 1), F32), jnp.zeros((tq, 2 * V_DIM), F32))
        carry = lax.fori_loop(0, i // 2, lambda j, c: step(2 * j + 1, step(2 * j, c, False), False), (init, init))
        carry = lax.fori_loop(0, i % 2, lambda _, c: step(i - 1, c, False), carry)
        outs = []
        for h, (m, l, acc) in enumerate(step(i, carry, True)):
            outs.append(acc / l)
            lse_ref[:, head_lanes[h]] = jnp.broadcast_to(m * SCALE + jnp.log(l), (tq, HEAD_PAD))
        lane = lax.broadcasted_iota(jnp.int32, (tq, 2 * V_DIM), 1)
        o_ref[...] = jnp.where(lane < V_DIM, outs[0], outs[1])
        if n:
            pl.when((pl.program_id(0) == last_pair) & (i == nq - 1))(_each(stages, 2))

    hbm = pl.BlockSpec(memory_space=pl.ANY)
    return pl.pallas_call(
        body, name=f"attn_fwd_gather{n}" if n else "attn_fwd", grid=(HEADS // 2, nq),
        in_specs=[pl.BlockSpec((tq, 2 * HEAD_PAD), lambda p, i: (i, p)),
                  pl.BlockSpec((t, 2 * HEAD_PAD), lambda p, i: (0, p)),
                  pl.BlockSpec((t, 2 * V_DIM), lambda p, i: (0, p))] + [hbm] * n,
        out_specs=[pl.BlockSpec((tq, 2 * V_DIM), lambda p, i: (i, p)),
                   pl.BlockSpec((tq, 2 * HEAD_PAD), lambda p, i: (i, p))] + [hbm] * n,
        out_shape=[jax.ShapeDtypeStruct((t, HEADS * V_DIM), F32), jax.ShapeDtypeStruct((t, HEADS * HEAD_PAD), F32)]
        + [jax.ShapeDtypeStruct((N_DEV,) + c.shape, c.dtype) for c in carried],
        scratch_shapes=_comm_sems() * n,
        compiler_params=_cp(("arbitrary", "arbitrary") if n else ("parallel", "parallel")),
    )(q, k, v, *carried)


def _sgu_fwd(zsg, gm, lng, lnb, wc_ref, bias, mixed_ref):
    uv, th = _gelu(zsg)
    u, v0 = uv[:, :SG_W], uv[:, SG_W:]
    vc = v0 - _gmean(v0, gm)
    r = lax.rsqrt(_gmean(vc * vc, gm) + EPS)
    vh = vc * r
    v = vh * lng + lnb
    lane = lax.broadcasted_iota(jnp.int32, (CHUNK, SG_W), 1)
    for c in range(zsg.shape[0] // CHUNK):
        rows = slice(c * CHUNK, (c + 1) * CHUNK)
        vb = v[rows].astype(BF16)
        mixed = bias
        for g in range(SG_W // GROUP):
            mixed = mixed + jnp.where(lane // GROUP == g, _mm(wc_ref[g], vb), 0.0)
        mixed_ref[rows, :] = mixed
    return u, v, vh, r, th


def _conv_fwd(zcv, halo, first, cw):
    gb, gc, hh = zcv[:, :CV_W], zcv[:, CV_W:2 * CV_W], zcv[:, 2 * CV_W:]
    y = gc * hh
    yh = jnp.where(first, 0.0, halo[:, CV_W:2 * CV_W] * halo[:, 2 * CV_W:])
    y1 = _shift_down(y, 1, yh)
    y2 = _shift_down(y, 2, yh)
    conv = y2 * cw[0:1, :] + y1 * cw[1:2, :] + y * cw[2:3, :]
    return gb * conv, conv, y, y1, y2


def _tril_bf16(w_ref, g):
    row = lax.broadcasted_iota(jnp.int32, (CHUNK, CHUNK), 0)
    col = lax.broadcasted_iota(jnp.int32, (CHUNK, CHUNK), 1)
    return jnp.where(col <= row, w_ref[g], 0.0).astype(BF16)


def mix_fwd(x, z, ya, gm, lng, lnb, wsp, bias, cw, gout, wout, gpost):
    t = x.shape[0]
    tm = min(TM, t)

    def body(x_ref, zcv_ref, halo_ref, zsg_ref, ya_ref, gm_ref, lng_ref, lnb_ref, wsp_ref, bias_ref, cw_ref,
             gout_ref, wout_ref, gpost_ref, x1_ref, wc_ref, mixed_ref):
        i = pl.program_id(0)
        for g in range(SG_W // GROUP):
            wc_ref[g] = _tril_bf16(wsp_ref, g)
        u, _, _, _, _ = _sgu_fwd(zsg_ref[...], gm_ref[...], lng_ref[...], lnb_ref[...], wc_ref, bias_ref[...], mixed_ref)
        yb = u * mixed_ref[...]
        yc, _, _, _, _ = _conv_fwd(zcv_ref[...], halo_ref[...], i == 0, cw_ref[...])
        gout = gout_ref[...]
        na, _, _ = _rms_fwd(ya_ref[...], gout[:, :512])
        nb, _, _ = _rms_fwd(yb, gout[:, 512:768])
        nc, _, _ = _rms_fwd(yc, gout[:, 768:])
        mix = jnp.concatenate([na, nb, nc], axis=1).astype(BF16)
        o, _, _ = _rms_fwd(_mm(mix, wout_ref[...]), gpost_ref[...])
        x1_ref[...] = x_ref[...] + o

    hb = tm // 8
    return pl.pallas_call(
        body, name="mix_fwd", grid=(t // tm,),
        in_specs=[pl.BlockSpec((tm, D), lambda i: (i, 0)),
                  pl.BlockSpec((tm, 768), lambda i: (i, 0)),
                  pl.BlockSpec((8, 768), lambda i: (jnp.maximum(i * hb - 1, 0), 0)),
                  pl.BlockSpec((tm, 512), lambda i: (i, 3)),
                  pl.BlockSpec((tm, 512), lambda i: (i, 0)),
                  _whole(), _whole(), _whole(), _whole(), _whole(), _whole(), _whole(), _whole(), _whole()],
        out_specs=pl.BlockSpec((tm, D), lambda i: (i, 0)),
        out_shape=jax.ShapeDtypeStruct((t, D), F32),
        scratch_shapes=[pltpu.VMEM((SG_W // GROUP, CHUNK, CHUNK), BF16), pltpu.VMEM((tm, SG_W), F32)],
        compiler_params=_cp(("arbitrary",)),
    )(x, z, z, z, ya, gm, lng, lnb, wsp, bias, cw, gout, wout, gpost)


def _sigmoid(a):
    return 1.0 / (1.0 + jnp.exp(-a))


FFN_SHARD = D_FF // N_DEV


def _load_ffn_weights(g_ref, wgu_ref, wd_ref, sems):
    copies = []
    for j in range(N_DEV):
        for p, (dst, base) in enumerate(((wgu_ref, 0), (wgu_ref, D_FF), (wd_ref, 0))):
            copies.append(pltpu.make_async_copy(g_ref.at[j, pl.ds(p * FFN_SHARD, FFN_SHARD)],
                                                dst.at[pl.ds(base + j * FFN_SHARD, FFN_SHARD)], sems.at[3 * j + p]))
    for cp in copies:
        cp.start()
    for cp in copies:
        cp.wait()


def _ffn_weight_scratch():
    return [pltpu.VMEM((2 * D_FF, D), BF16), pltpu.VMEM((D_FF, D), BF16), pltpu.SemaphoreType.DMA((3 * N_DEV,))]


def ffn_fwd(x1, gpre, gathered, gpost):
    t = x1.shape[0]
    tm = min(TM_FFN, t)

    def body(x_ref, gpre_ref, g_ref, gpost_ref, x2_ref, f_ref, wgu_ref, wd_ref, sems):
        @pl.when(pl.program_id(0) == 0)
        def _():
            _load_ffn_weights(g_ref, wgu_ref, wd_ref, sems)

        x = x_ref[...]
        h, _, _ = _rms_fwd(x, gpre_ref[...])
        ab = _mm_nt(h.astype(BF16), wgu_ref[...])
        a, b = ab[:, :D_FF], ab[:, D_FF:]
        s = a * _sigmoid(a) * b
        f = _mm(s.astype(BF16), wd_ref[...])
        f_ref[...] = f
        x2_ref[...] = x + _rms_fwd(f, gpost_ref[...])[0]

    row = pl.BlockSpec((tm, D), lambda i: (i, 0))
    return pl.pallas_call(
        body, name="ffn_fwd", grid=(t // tm,),
        in_specs=[row, _whole(), pl.BlockSpec(memory_space=pl.ANY), _whole()],
        out_specs=[row, row],
        out_shape=[jax.ShapeDtypeStruct((t, D), F32), jax.ShapeDtypeStruct((t, D), F32)],
        scratch_shapes=_ffn_weight_scratch(),
        compiler_params=_cp(("arbitrary",)),
    )(x1, gpre, gathered, gpost)


def loss_head(y, target):
    t = y.shape[0]
    tm = min(TM, t)

    def body(y_ref, t_ref, loss_ref, dy_ref):
        @pl.when(pl.program_id(0) == 0)
        def _():
            loss_ref[...] = jnp.zeros_like(loss_ref)

        e = y_ref[...] - t_ref[...]
        dy_ref[...] = e * (1.0 / D)
        loss_ref[...] += jnp.sum(jnp.sum(e * e, axis=-1, keepdims=True), axis=0, keepdims=True)

    return pl.pallas_call(
        body, name="loss_head", grid=(t // tm,),
        in_specs=[pl.BlockSpec((tm, D), lambda i: (i, 0)), pl.BlockSpec((tm, D), lambda i: (i, 0))],
        out_specs=[pl.BlockSpec((1, 128), lambda i: (0, 0)), pl.BlockSpec((tm, D), lambda i: (i, 0))],
        out_shape=[jax.ShapeDtypeStruct((1, 128), F32), jax.ShapeDtypeStruct((t, D), F32)],
        compiler_params=_cp(("arbitrary",)),
    )(y, target)


def _acc(ref, first, val):
    @pl.when(first)
    def _():
        ref[...] = val

    @pl.when(jnp.logical_not(first))
    def _():
        ref[...] += val


def ffn_bwd(x1, f, dx2, gpre, gathered, gpost):
    t = x1.shape[0]
    tm = min(TM_FFN, t)

    def body(x_ref, f_ref, dx2_ref, gpre_ref, g_ref, gpost_ref,
             dx1_ref, h_ref, dab_ref, s_ref, df_ref, dgpre_ref, dgpost_ref, ab_ref, ds_ref, wgu_ref, wd_ref, sems):
        first = pl.program_id(0) == 0

        @pl.when(first)
        def _():
            _load_ffn_weights(g_ref, wgu_ref, wd_ref, sems)

        dx2 = dx2_ref[...]
        gpre, gpost = gpre_ref[...], gpost_ref[...]
        h, xh, rx = _rms_fwd(x_ref[...], gpre)
        h_ref[...] = h.astype(BF16)
        ab_ref[...] = _mm_nt(h_ref[...], wgu_ref[...])
        for c in range(0, D_FF, FFN_SLAB):
            a, b = ab_ref[:, c:c + FFN_SLAB], ab_ref[:, D_FF + c:D_FF + c + FFN_SLAB]
            s_ref[:, c:c + FFN_SLAB] = (a * _sigmoid(a) * b).astype(BF16)
        _, fh, rf = _rms_fwd(f_ref[...], gpost)
        df, dgpost = _rms_bwd(fh, rf, gpost, dx2)
        df_ref[...] = df.astype(BF16)
        ds_ref[...] = _mm_nt(df_ref[...], wd_ref[...])
        for c in range(0, D_FF, FFN_SLAB):
            a, b = ab_ref[:, c:c + FFN_SLAB], ab_ref[:, D_FF + c:D_FF + c + FFN_SLAB]
            ds = ds_ref[:, c:c + FFN_SLAB]
            sg = _sigmoid(a)
            dab_ref[:, c:c + FFN_SLAB] = (ds * b * (sg * (1.0 + a * (1.0 - sg)))).astype(BF16)
            dab_ref[:, D_FF + c:D_FF + c + FFN_SLAB] = (ds * (a * sg)).astype(BF16)
        dx, dgpre = _rms_bwd(xh, rx, gpre, _mm(dab_ref[...], wgu_ref[...]))
        dx1_ref[...] = dx2 + dx
        _acc(dgpre_ref, first, dgpre)
        _acc(dgpost_ref, first, dgpost)

    row = lambda w: pl.BlockSpec((tm, w), lambda i: (i, 0))
    vec = pl.BlockSpec((1, D), lambda i: (0, 0))
    return pl.pallas_call(
        body, name="ffn_bwd", grid=(t // tm,),
        in_specs=[row(D), row(D), row(D), _whole(), pl.BlockSpec(memory_space=pl.ANY), _whole()],
        out_specs=[row(D), row(D), row(2 * D_FF), row(D_FF), row(D), vec, vec],
        out_shape=[jax.ShapeDtypeStruct((t, D), F32), jax.ShapeDtypeStruct((t, D), BF16),
                   jax.ShapeDtypeStruct((t, 2 * D_FF), BF16), jax.ShapeDtypeStruct((t, D_FF), BF16),
                   jax.ShapeDtypeStruct((t, D), BF16), jax.ShapeDtypeStruct((1, D), F32),
                   jax.ShapeDtypeStruct((1, D), F32)],
        scratch_shapes=[pltpu.VMEM((tm, 2 * D_FF), F32), pltpu.VMEM((tm, D_FF), F32)] + _ffn_weight_scratch(),
        compiler_params=_cp(("arbitrary",)),
    )(x1, f, dx2, gpre, gathered, gpost)


def atb(a, b, tk):
    t, k = a.shape
    n = b.shape[1]
    tt = min(TT, t)
    tk = min(tk, k)
    steps = t // tt

    def body(a_ref, b_ref, o_ref, acc_ref):
        i = pl.program_id(1)
        _acc(acc_ref, i == 0, _mm_tn(a_ref[...], b_ref[...]))

        @pl.when(i == steps - 1)
        def _():
            o_ref[...] = acc_ref[...].astype(BF16)

    return pl.pallas_call(
        body, name="atb", grid=(k // tk, steps),
        in_specs=[pl.BlockSpec((tt, tk), lambda j, i: (i, j)), pl.BlockSpec((tt, n), lambda j, i: (i, 0))],
        out_specs=pl.BlockSpec((tk, n), lambda j, i: (j, 0)),
        out_shape=jax.ShapeDtypeStruct((k, n), BF16),
        scratch_shapes=[pltpu.VMEM((tk, n), F32)],
        compiler_params=_cp(("parallel", "arbitrary")),
    )(a, b)


def mix_bwd(dx1, z, ya, gm, lng, lnb, wsp, wspt, bias, cw, gout, wout, gpost):
    t = dx1.shape[0]
    tm = min(TM, t)
    ng = SG_W // GROUP

    def body(dx1_ref, zcv_ref, halo_ref, zsg_ref, ya_ref, gm_ref, lng_ref, lnb_ref, wsp_ref, wspt_ref, bias_ref,
             cw_ref, gout_ref, wout_ref, gpost_ref,
             dya_ref, dyc_ref, dzsg_ref, mix_ref, do_ref, dgpost_ref, dgout_ref, dlng_ref, dlnb_ref, dwsp_ref,
             dbias_ref, wc_ref, wct_ref, mixed_ref, dv_ref):
        i = pl.program_id(0)
        first = i == 0
        gm = gm_ref[...]
        for g in range(ng):
            wc_ref[g] = _tril_bf16(wsp_ref, g)
            wct_ref[g] = jnp.where(
                lax.broadcasted_iota(jnp.int32, (CHUNK, CHUNK), 0) <= lax.broadcasted_iota(jnp.int32, (CHUNK, CHUNK), 1),
                wspt_ref[g], 0.0).astype(BF16)
        zsg = zsg_ref[...]
        lng = lng_ref[...]
        u, v, vh, r, th = _sgu_fwd(zsg, gm, lng, lnb_ref[...], wc_ref, bias_ref[...], mixed_ref)
        mixed = mixed_ref[...]
        yb = u * mixed
        yc, _, _, _, _ = _conv_fwd(zcv_ref[...], halo_ref[...], first, cw_ref[...])
        gout, gpost = gout_ref[...], gpost_ref[...]
        ga, gb_, gc_ = gout[:, :512], gout[:, 512:768], gout[:, 768:]
        na, yah, ra = _rms_fwd(ya_ref[...], ga)
        nb, ybh, rb = _rms_fwd(yb, gb_)
        nc, ych, rc = _rms_fwd(yc, gc_)
        mix = jnp.concatenate([na, nb, nc], axis=1).astype(BF16)
        _, oh, ro = _rms_fwd(_mm(mix, wout_ref[...]), gpost)
        do, dgpost = _rms_bwd(oh, ro, gpost, dx1_ref[...])
        dob = do.astype(BF16)
        dmix = _mm_nt(dob, wout_ref[...])
        dya, dga = _rms_bwd(yah, ra, ga, dmix[:, :512])
        dyb, dgb = _rms_bwd(ybh, rb, gb_, dmix[:, 512:768])
        dyc, dgc = _rms_bwd(ych, rc, gc_, dmix[:, 768:])
        dya_ref[...] = dya
        dyc_ref[...] = dyc
        mix_ref[...] = mix
        do_ref[...] = dob
        _acc(dgpost_ref, first, dgpost)
        _acc(dgout_ref, first, jnp.concatenate([dga, dgb, dgc], axis=1))
        du = dyb * mixed
        dmixed = dyb * u
        lane = lax.broadcasted_iota(jnp.int32, (CHUNK, SG_W), 1)
        row = lax.broadcasted_iota(jnp.int32, (CHUNK, CHUNK), 0)
        col = lax.broadcasted_iota(jnp.int32, (CHUNK, CHUNK), 1)
        dbias = jnp.zeros((CHUNK, SG_W), F32)
        dw = [jnp.zeros((CHUNK, CHUNK), F32) for _ in range(ng)]
        for c in range(tm // CHUNK):
            rows = slice(c * CHUNK, (c + 1) * CHUNK)
            dm = dmixed[rows]
            dbias = dbias + dm
            dmb = dm.astype(BF16)
            vb = v[rows].astype(BF16)
            dvc = jnp.zeros((CHUNK, SG_W), F32)
            for g in range(ng):
                in_g = lane // GROUP == g
                dvc = dvc + jnp.where(in_g, _mm(wct_ref[g], dmb), 0.0)
                dw[g] = dw[g] + _mm_nt(jnp.where(in_g, dmb, jnp.zeros_like(dmb)), vb)
            dv_ref[rows, :] = dvc
        for g in range(ng):
            dwg = jnp.where(col <= row, dw[g], 0.0)

            @pl.when(first)
            def _():
                dwsp_ref[g] = dwg

            @pl.when(jnp.logical_not(first))
            def _():
                dwsp_ref[g] += dwg
        _acc(dbias_ref, first, _gmean(dbias, gm) * GROUP)
        dv = dv_ref[...]
        _acc(dlng_ref, first, jnp.sum(dv * vh, axis=0, keepdims=True))
        _acc(dlnb_ref, first, jnp.sum(dv, axis=0, keepdims=True))
        dvh = dv * lng
        dv0 = r * (dvh - _gmean(dvh, gm) - vh * _gmean(dvh * vh, gm))
        dzsg_ref[...] = (jnp.concatenate([du, dv0], axis=1) * _gelu_grad(zsg, th)).astype(BF16)

    hb = tm // 8
    row_ = lambda w: pl.BlockSpec((tm, w), lambda i: (i, 0))
    vec = lambda w: pl.BlockSpec((1, w), lambda i: (0, 0))
    return pl.pallas_call(
        body, name="mix_bwd", grid=(t // tm,),
        in_specs=[row_(D),
                  pl.BlockSpec((tm, 768), lambda i: (i, 0)),
                  pl.BlockSpec((8, 768), lambda i: (jnp.maximum(i * hb - 1, 0), 0)),
                  pl.BlockSpec((tm, 512), lambda i: (i, 3)),
                  row_(512),
                  _whole(), _whole(), _whole(), _whole(), _whole(), _whole(), _whole(), _whole(), _whole(), _whole()],
        out_specs=[row_(512), row_(CV_W), row_(512), row_(D), row_(D), vec(D), vec(D), vec(SG_W), vec(SG_W),
                   pl.BlockSpec((ng, CHUNK, CHUNK), lambda i: (0, 0, 0)),
                   pl.BlockSpec((CHUNK, SG_W), lambda i: (0, 0))],
        out_shape=[jax.ShapeDtypeStruct((t, 512), F32), jax.ShapeDtypeStruct((t, CV_W), F32),
                   jax.ShapeDtypeStruct((t, 512), BF16), jax.ShapeDtypeStruct((t, D), BF16),
                   jax.ShapeDtypeStruct((t, D), BF16), jax.ShapeDtypeStruct((1, D), F32),
                   jax.ShapeDtypeStruct((1, D), F32), jax.ShapeDtypeStruct((1, SG_W), F32),
                   jax.ShapeDtypeStruct((1, SG_W), F32), jax.ShapeDtypeStruct((ng, CHUNK, CHUNK), F32),
                   jax.ShapeDtypeStruct((CHUNK, SG_W), F32)],
        scratch_shapes=[pltpu.VMEM((ng, CHUNK, CHUNK), BF16), pltpu.VMEM((ng, CHUNK, CHUNK), BF16),
                        pltpu.VMEM((tm, SG_W), F32), pltpu.VMEM((tm, SG_W), F32)],
        compiler_params=_cp(("arbitrary",)),
    )(dx1, z, z, z, ya, gm, lng, lnb, wsp, wspt, bias, cw, gout, wout, gpost)


def conv_bwd(dyc, z, cw):
    t = dyc.shape[0]
    tm = min(TM, t)
    hb = tm // 8
    last_blk = t // 8 - 1

    def body(dyc_ref, dyct_ref, zcv_ref, head_ref, tail_ref, cw_ref, dz_ref, dcw_ref):
        i = pl.program_id(0)
        first = i == 0
        last = i == pl.num_programs(0) - 1
        cw = cw_ref[...]
        zcv = zcv_ref[...]
        gb, gc, hh = zcv[:, :CV_W], zcv[:, CV_W:2 * CV_W], zcv[:, 2 * CV_W:]
        _, conv, y, y1, y2 = _conv_fwd(zcv, head_ref[...], first, cw)
        dyc = dyc_ref[...]
        dconv = dyc * gb
        tail = jnp.where(last, 0.0, dyct_ref[...] * tail_ref[:, :CV_W])
        d1 = _shift_up(dconv, 1, tail)
        d2 = _shift_up(dconv, 2, tail)
        dy = dconv * cw[2:3, :] + d1 * cw[1:2, :] + d2 * cw[0:1, :]
        dz_ref[...] = jnp.concatenate([dyc * conv, dy * hh, dy * gc], axis=1).astype(BF16)
        tap = lax.broadcasted_iota(jnp.int32, (8, CV_W), 0)
        dcw = jnp.where(tap == 0, jnp.sum(dconv * y2, axis=0, keepdims=True),
                        jnp.where(tap == 1, jnp.sum(dconv * y1, axis=0, keepdims=True),
                                  jnp.where(tap == 2, jnp.sum(dconv * y, axis=0, keepdims=True), 0.0)))
        _acc(dcw_ref, first, dcw)

    return pl.pallas_call(
        body, name="conv_bwd", grid=(t // tm,),
        in_specs=[pl.BlockSpec((tm, CV_W), lambda i: (i, 0)),
                  pl.BlockSpec((8, CV_W), lambda i: (jnp.minimum((i + 1) * hb, last_blk), 0)),
                  pl.BlockSpec((tm, 768), lambda i: (i, 0)),
                  pl.BlockSpec((8, 768), lambda i: (jnp.maximum(i * hb - 1, 0), 0)),
                  pl.BlockSpec((8, 768), lambda i: (jnp.minimum((i + 1) * hb, last_blk), 0)),
                  _whole()],
        out_specs=[pl.BlockSpec((tm, 768), lambda i: (i, 0)), pl.BlockSpec((8, CV_W), lambda i: (0, 0))],
        out_shape=[jax.ShapeDtypeStruct((t, 768), BF16), jax.ShapeDtypeStruct((8, CV_W), F32)],
        compiler_params=_cp(("arbitrary",)),
    )(dyc, dyc, z, z, z, cw)


def attn_bwd(q, k, v, o, lse, do, carried=()):
    t = q.shape[0]
    tq = min(TQ, t)
    nq = t // tq
    last_pair = HEADS // 2 - 1
    n = len(carried)

    def body(*refs):
        j = pl.program_id(1)
        q_ref, k_ref, v_ref, o_ref, lse_ref, do_ref = refs[:6]
        dq_ref, dk_ref, dv_ref = refs[6 + n:9 + n]
        sems = refs[9 + 2 * n:]
        stages = [_exchange_steps(refs[6 + a], refs[9 + n + a], carried[a][1], *sems[3 * a:3 * a + 3]) for a in range(n)]
        if n:
            pl.when((pl.program_id(0) == 0) & (j == 0))(_each(stages, 0))

        @pl.when(j == 0)
        def _():
            dq_ref[...] = jnp.zeros_like(dq_ref)

        row = lax.broadcasted_iota(jnp.int32, (tq, tq), 0)
        col = lax.broadcasted_iota(jnp.int32, (tq, tq), 1)
        vlane = lax.broadcasted_iota(jnp.int32, (tq, 2 * V_DIM), 1)
        head_lanes = [slice(h * HEAD_PAD, (h + 1) * HEAD_PAD) for h in range(2)]

        def step(i, carry, masked):
            start = pl.multiple_of(i * tq, tq)
            do_blk = do_ref[pl.ds(start, tq), :]
            o_blk = o_ref[pl.ds(start, tq), :]
            vb = v_ref[...]
            dks, dv_acc = [], carry[2]
            for h in range(2):
                lanes = head_lanes[h]
                qb = q_ref[pl.ds(start, tq), lanes]
                kb = k_ref[:, lanes]
                dob = jnp.where((vlane // V_DIM) == h, do_blk, 0.0)
                delta = jnp.sum(dob * o_blk, axis=-1, keepdims=True)
                lse2 = lse_ref[pl.ds(start, tq), lanes][:, 0:1] * LOG2E
                s = _mm_nt(qb, kb)
                if masked:
                    s = jnp.where(col <= row, s, NEG)
                p = jnp.exp2(s * SCALE_LOG2E - lse2)
                dob16 = dob.astype(BF16)
                dp = _mm_nt(dob16, vb)
                ds = (p * (dp - delta) * SCALE).astype(BF16)
                dv_acc = dv_acc + _mm_tn(p.astype(BF16), dob16)
                dks.append(carry[h] + _mm_tn(ds, qb))
                dq_ref[pl.ds(start, tq), lanes] += _mm(ds, kb)
            return dks[0], dks[1], dv_acc

        zero = jnp.zeros((tq, HEAD_PAD), F32)
        carry = step(j, (zero, zero, jnp.zeros((tq, 2 * V_DIM), F32)), True)
        dk0, dk1, dv_acc = lax.fori_loop(j + 1, nq, lambda i, c: step(i, c, False), carry)
        dk_ref[:, head_lanes[0]] = dk0
        dk_ref[:, head_lanes[1]] = dk1
        dv_ref[...] = dv_acc
        if n:
            pl.when((pl.program_id(0) == last_pair) & (j == nq - 1))(_each(stages, 1))

    hbm = pl.BlockSpec(memory_space=pl.ANY)
    return pl.pallas_call(
        body, name=f"attn_bwd_exchange{n}" if n else "attn_bwd", grid=(HEADS // 2, nq),
        in_specs=[pl.BlockSpec((t, 2 * HEAD_PAD), lambda p, j: (0, p)),
                  pl.BlockSpec((tq, 2 * HEAD_PAD), lambda p, j: (j, p)),
                  pl.BlockSpec((tq, 2 * V_DIM), lambda p, j: (j, p)),
                  pl.BlockSpec((t, 2 * V_DIM), lambda p, j: (0, p)),
                  pl.BlockSpec((t, 2 * HEAD_PAD), lambda p, j: (0, p)),
                  pl.BlockSpec((t, 2 * V_DIM), lambda p, j: (0, p))] + [hbm] * n,
        out_specs=[pl.BlockSpec((t, 2 * HEAD_PAD), lambda p, j: (0, p)),
                   pl.BlockSpec((tq, 2 * HEAD_PAD), lambda p, j: (j, p)),
                   pl.BlockSpec((tq, 2 * V_DIM), lambda p, j: (j, p))] + [hbm] * n,
        out_shape=[jax.ShapeDtypeStruct((t, HEADS * HEAD_PAD), F32), jax.ShapeDtypeStruct((t, HEADS * HEAD_PAD), F32),
                   jax.ShapeDtypeStruct((t, HEADS * V_DIM), F32)]
        + [jax.ShapeDtypeStruct(src.shape if scatter else (N_DEV,) + src.shape, src.dtype) for src, scatter in carried],
        scratch_shapes=_comm_sems() * n,
        compiler_params=_cp(("arbitrary", "arbitrary") if n else ("parallel", "arbitrary")),
    )(q, k, v, o, lse, do, *[src for src, _ in carried])


def mla_proj_bwd(dq, dk, dv, z, ca, sb, sc, gq, gkv, wuq, wukv):
    t = z.shape[0]
    tm = min(TM, t)

    def body(dq_ref, dk_ref, dv_ref, z_ref, ca_ref, sb_ref, sc_ref, gq_ref, gkv_ref, wuq_ref, wukv_ref,
             dz_ref, cq_ref, ckv_ref, dqp_ref, dkvp_ref, dgq_ref, dgkv_ref):
        first = pl.program_id(0) == 0
        z = z_ref[...]
        ca, sb, sc = ca_ref[...], sb_ref[...], sc_ref[...]
        gq, gkv = gq_ref[...], gkv_ref[...]
        cq, cqh, rq = _rms_fwd(z[:, :Q_RANK], gq)
        ckv, ckvh, rkv = _rms_fwd(z[:, Q_RANK:Q_RANK + KV_RANK], gkv)
        lane = lax.broadcasted_iota(jnp.int32, (tm, HEAD_PAD), 1)
        dkr = jnp.zeros((tm, HEAD_PAD), F32)
        for h in range(HEADS):
            lanes = slice(h * HEAD_PAD, (h + 1) * HEAD_PAD)
            dqp_ref[:, lanes] = _rope_t(dq_ref[:, lanes], ca, sb, sc).astype(BF16)
            dkh = dk_ref[:, lanes]
            dkr = dkr + dkh
            dkvp_ref[:, lanes] = jnp.where(lane < NOPE, dkh, 0.0).astype(BF16)
        dkvp_ref[:, HEADS * HEAD_PAD:] = dv_ref[...].astype(BF16)
        dkr = pltpu.roll(_rope_t(jnp.where(lane >= NOPE, dkr, 0.0), ca, sb, sc), HEAD_PAD - NOPE, 1)
        dkr = jnp.where(lane < ROPE, dkr, 0.0)
        dcq = _mm(dqp_ref[...], wuq_ref[...])
        dckv = _mm(dkvp_ref[...], wukv_ref[...])
        dzq, dgq = _rms_bwd(cqh, rq, gq, dcq)
        dzkv, dgkv = _rms_bwd(ckvh, rkv, gkv, dckv)
        dz_ref[...] = jnp.concatenate([dzq, dzkv, dkr], axis=1).astype(BF16)
        cq_ref[...] = cq.astype(BF16)
        ckv_ref[...] = ckv.astype(BF16)
        _acc(dgq_ref, first, dgq)
        _acc(dgkv_ref, first, dgkv)

    row = lambda w: pl.BlockSpec((tm, w), lambda i: (i, 0))
    vec = lambda w: pl.BlockSpec((1, w), lambda i: (0, 0))
    return pl.pallas_call(
        body, name="mla_proj_bwd", grid=(t // tm,),
        in_specs=[row(1024), row(1024), row(512), pl.BlockSpec((tm, 768), lambda i: (i, 1)),
                  row(HEAD_PAD), row(HEAD_PAD), row(HEAD_PAD), _whole(), _whole(), _whole(), _whole()],
        out_specs=[row(768), row(Q_RANK), row(KV_RANK), row(1024), row(1536), vec(Q_RANK), vec(KV_RANK)],
        out_shape=[jax.ShapeDtypeStruct((t, 768), BF16), jax.ShapeDtypeStruct((t, Q_RANK), BF16),
                   jax.ShapeDtypeStruct((t, KV_RANK), BF16), jax.ShapeDtypeStruct((t, 1024), BF16),
                   jax.ShapeDtypeStruct((t, 1536), BF16), jax.ShapeDtypeStruct((1, Q_RANK), F32),
                   jax.ShapeDtypeStruct((1, KV_RANK), F32)],
        compiler_params=_cp(("arbitrary",)),
    )(dq, dk, dv, z, ca, sb, sc, gq, gkv, wuq, wukv)


def pre_in_bwd(x, dx1, dzcv, dzmla, dzsg, g, w):
    t = x.shape[0]
    tm = min(TM, t)

    def body(x_ref, dx1_ref, dzcv_ref, dzmla_ref, dzsg_ref, g_ref, w_ref, dx_ref, h_ref, dz_ref, dg_ref):
        g = g_ref[...]
        h, xh, r = _rms_fwd(x_ref[...], g)
        dz = jnp.concatenate([dzcv_ref[...], dzmla_ref[...], dzsg_ref[...]], axis=1)
        dx, dg = _rms_bwd(xh, r, g, _mm(dz, w_ref[...]))
        dx_ref[...] = dx1_ref[...] + dx
        h_ref[...] = h.astype(BF16)
        dz_ref[...] = dz
        _acc(dg_ref, pl.program_id(0) == 0, dg)

    row = lambda w_: pl.BlockSpec((tm, w_), lambda i: (i, 0))
    return pl.pallas_call(
        body, name="pre_in_bwd", grid=(t // tm,),
        in_specs=[row(D), row(D), row(768), row(768), row(512), _whole(), _whole()],
        out_specs=[row(D), row(D), row(Z_W), pl.BlockSpec((1, D), lambda i: (0, 0))],
        out_shape=[jax.ShapeDtypeStruct((t, D), F32), jax.ShapeDtypeStruct((t, D), BF16),
                   jax.ShapeDtypeStruct((t, Z_W), BF16), jax.ShapeDtypeStruct((1, D), F32)],
        compiler_params=_cp(("arbitrary",)),
    )(x, dx1, dzcv, dzmla, dzsg, g, w)


MESH = pl.DeviceIdType.MESH


def _place():
    return lax.axis_index("x"), lax.axis_index("y"), lax.axis_index("c")


def _comm_sems():
    return [pltpu.SemaphoreType.DMA((7,)), pltpu.SemaphoreType.DMA((7,)), pltpu.SemaphoreType.DMA]


def _gather_steps(x_ref, out_ref, send_sems, recv_sems, local_sem):
    x, y, c = _place()
    me, sibling = (x, y, c), (x, y, 1 - c)
    chips = [(1 - x, y), (x, 1 - y), (1 - x, 1 - y)]

    def slot(px, py, pc):
        return out_ref.at[4 * px + 2 * py + pc]

    def copy(k, blk, to, src=None):
        return pltpu.make_async_remote_copy(
            src_ref=slot(*blk) if src is None else src, dst_ref=slot(*blk),
            send_sem=send_sems.at[k], recv_sem=recv_sems.at[k], device_id=to, device_id_type=MESH)

    mine = pltpu.make_async_copy(x_ref, slot(*me), local_sem)
    first = [copy(0, me, sibling, src=x_ref)] + [copy(1 + j, me, (*chip, c), src=x_ref) for j, chip in enumerate(chips)]
    passed = [copy(4 + j, (*chip, c), sibling) for j, chip in enumerate(chips)]

    def start():
        mine.start()
        for cp in first:
            cp.start()

    def forward():
        for j, chip in enumerate(chips):
            copy(1 + j, (*chip, c), me).wait_recv()
            passed[j].start()

    def finish():
        copy(0, sibling, me).wait_recv()
        for j, chip in enumerate(chips):
            copy(4 + j, (*chip, 1 - c), me).wait_recv()
        for cp in first + passed:
            cp.wait_send()
        mine.wait()

    return start, forward, finish


def _exchange_steps(src_ref, out_ref, scatter, send_sems, recv_sems, local_sem):
    x, y, c = _place()
    me = 4 * x + 2 * y + c
    own = pltpu.make_async_copy(src_ref.at[me] if scatter else src_ref, out_ref.at[me], local_sem)
    copies = []
    for k in range(1, N_DEV):
        px = 1 - x if k & 4 else x
        py = 1 - y if k & 2 else y
        pc = 1 - c if k & 1 else c
        copies.append(pltpu.make_async_remote_copy(
            src_ref=src_ref.at[4 * px + 2 * py + pc] if scatter else src_ref, dst_ref=out_ref.at[me],
            send_sem=send_sems.at[k - 1], recv_sem=recv_sems.at[k - 1], device_id=(px, py, pc), device_id_type=MESH))

    def start():
        own.start()
        for cp in copies:
            cp.start()

    def finish():
        for cp in copies:
            cp.wait_recv()
        for cp in copies:
            cp.wait_send()
        own.wait()

    return start, finish


def all_gather(block):
    def body(x_ref, out_ref, *sems):
        for stage in _gather_steps(x_ref, out_ref, *sems):
            stage()

    return pl.pallas_call(
        body, name="all_gather",
        in_specs=[pl.BlockSpec(memory_space=pl.ANY)],
        out_specs=pl.BlockSpec(memory_space=pl.ANY),
        out_shape=jax.ShapeDtypeStruct((N_DEV,) + block.shape, block.dtype),
        scratch_shapes=_comm_sems(),
    )(block)


def _row_tile(r, cap):
    return max(d for d in range(16, cap + 1, 16) if r % d == 0)


def sum_adamw(parts, w, m, v, cap, carried=()):
    nl, r, c = w.shape
    tr = _row_tile(r, cap)
    steps = r // tr
    n = len(carried)
    c1 = 1.0 / (1.0 - ADAM_B1 ** ADAM_STEP)
    c2 = 1.0 / (1.0 - ADAM_B2 ** ADAM_STEP)

    def body(*refs):
        p_refs = refs[:nl]
        w_ref, m_ref, v_ref = refs[nl:nl + 3]
        g_ref, d_ref, nm_ref, nv_ref = refs[nl + 3 + n:nl + 7 + n]
        sems = refs[nl + 7 + 2 * n:]
        stages = [_exchange_steps(refs[nl + 3 + a], refs[nl + 7 + n + a], carried[a][1], *sems[3 * a:3 * a + 3])
                  for a in range(n)]
        layer, i = pl.program_id(0), pl.program_id(1)
        if n:
            pl.when((layer == 0) & (i == 0))(_each(stages, 0))

        def update(p_ref):
            g = p_ref[0].astype(F32)
            for k in range(1, N_DEV):
                g = g + p_ref[k].astype(F32)
            m_new = ADAM_B1 * m_ref[...] + (1.0 - ADAM_B1) * g
            v_new = ADAM_B2 * v_ref[...] + (1.0 - ADAM_B2) * (g * g)
            g_ref[...] = g
            nm_ref[...] = m_new
            nv_ref[...] = v_new
            d_ref[...] = -ADAM_LR * ((m_new * c1) / (jnp.sqrt(v_new * c2) + ADAM_EPS) + ADAM_WD * w_ref[...])

        for k in range(nl):
            pl.when(layer == k)(functools.partial(update, p_refs[k]))
        if n:
            pl.when((layer == nl - 1) & (i == steps - 1))(_each(stages, 1))

    def parts_spec(k):
        return pl.BlockSpec((N_DEV, tr, c), lambda l, i: (0, jnp.where(l == k, i, jnp.where(l < k, 0, steps - 1)), 0))

    blk = pl.BlockSpec((None, tr, c), lambda l, i: (l, i, 0))
    out = jax.ShapeDtypeStruct((nl, r, c), F32)
    hbm = pl.BlockSpec(memory_space=pl.ANY)
    return pl.pallas_call(
        body, name=f"sum_adamw_exchange{n}" if n else "sum_adamw", grid=(nl, steps),
        in_specs=[parts_spec(k) for k in range(nl)] + [blk, blk, blk] + [hbm] * n,
        out_specs=[blk, blk, blk, blk] + [hbm] * n,
        out_shape=[out, out, out, out]
        + [jax.ShapeDtypeStruct(src.shape if scatter else (N_DEV,) + src.shape, src.dtype) for src, scatter in carried],
        scratch_shapes=_comm_sems() * n,
        compiler_params=_cp(("arbitrary", "arbitrary")),
    )(*parts, w, m, v, *[src for src, _ in carried])


PACK_W = 1024
MIX_PIECES = (("w_out", D // N_DEV, D, False), ("w_uq", HEADS * (NOPE + ROPE) // N_DEV, Q_RANK, True),
              ("w_ukv", HEADS * (NOPE + V_DIM) // N_DEV, KV_RANK, True), ("conv", 16, PACK_W, False),
              ("w_in", IN_W // N_DEV, D, True))
FFN_PIECES = (("w_gate", D_FF // N_DEV, D, True), ("w_up", D_FF // N_DEV, D, True), ("w_down", D_FF // N_DEV, D, False))
OFFSET = {}
for _pieces in (MIX_PIECES, FFN_PIECES):
    _off = 0
    for _name, _rows, _, _ in _pieces:
        OFFSET[_name] = _off
        _off += _rows + -_rows % 16
assert all(o % 16 == 0 for o in OFFSET.values())
assert [OFFSET[n] for n in ("w_gate", "w_up", "w_down")] == [0, FFN_SHARD, 2 * FFN_SHARD]
CONV_BITS = 3 * (CV_W // N_DEV) * 2


def _to_pack(shards, dtype, pieces, conv=None):
    nl = shards["w_in"].shape[0]
    parts = []
    for name, rows, cols, transposed in pieces:
        if name == "conv":
            if conv is None:
                a = jnp.zeros((nl, rows, PACK_W), dtype)
            else:
                bits = lax.bitcast_convert_type(conv.astype(F32), BF16).reshape(nl, CONV_BITS)
                a = jnp.pad(bits, ((0, 0), (0, rows * PACK_W - CONV_BITS))).reshape(nl, rows, PACK_W)
        else:
            a = shards[name].astype(dtype)
            a = jnp.swapaxes(a, 1, 2) if transposed else a
            a = jnp.pad(a, ((0, 0), (0, -rows % 16), (0, PACK_W - cols)))
        parts.append(a)
    return jnp.concatenate(parts, axis=1)


def _from_pack(pack, pieces):
    out = {}
    for name, rows, cols, transposed in pieces:
        if name != "conv":
            a = pack[:, OFFSET[name]:OFFSET[name] + rows, :cols]
            out[name] = jnp.swapaxes(a, 1, 2) if transposed else a
    return out


def _mix_weights(g):
    def rows(name):
        _, n, cols, _ = next(p for p in MIX_PIECES if p[0] == name)
        return g[:, OFFSET[name]:OFFSET[name] + n, :cols]

    w_in_t = rows("w_in").reshape(IN_W, D)
    w_in_p = jnp.concatenate([w_in_t[1184:], w_in_t[:672], jnp.zeros((96, D), BF16), w_in_t[672:1184]], axis=0)
    w_uq_p = jnp.pad(rows("w_uq"), ((0, 0), (0, HEAD_PAD - NOPE - ROPE), (0, 0))).reshape(HEADS * HEAD_PAD, Q_RANK)
    kv = rows("w_ukv")
    w_k = jnp.pad(kv[:, :NOPE], ((0, 0), (0, HEAD_PAD - NOPE), (0, 0))).reshape(HEADS * HEAD_PAD, KV_RANK)
    w_ukv_p = jnp.concatenate([w_k, kv[:, NOPE:].reshape(HEADS * V_DIM, KV_RANK)], axis=0)
    bits = rows("conv").reshape(N_DEV, -1)[:, :CONV_BITS].reshape(N_DEV, 3, CV_W // N_DEV, 2)
    conv_w = jnp.moveaxis(lax.bitcast_convert_type(bits, F32), 0, 1).reshape(3, CV_W)
    return dict(w_in=w_in_p, w_uq=w_uq_p, w_ukv=w_ukv_p, w_out=rows("w_out").reshape(D, D), conv_w=conv_w)


def _grad_chunks(full):
    if "w_in" in full:
        pieces = MIX_PIECES
        d_in = full["w_in"]
        d_in = jnp.concatenate([d_in[768:768 + 672], d_in[1536:], d_in[:768]], axis=0)
        d_uq = full["w_uq"].reshape(HEADS, HEAD_PAD, Q_RANK)[:, :NOPE + ROPE]
        d_k = full["w_ukv"][:HEADS * HEAD_PAD].reshape(HEADS, HEAD_PAD, KV_RANK)[:, :NOPE]
        d_v = full["w_ukv"][HEADS * HEAD_PAD:].reshape(HEADS, V_DIM, KV_RANK)
        mats = dict(w_in=d_in, w_uq=d_uq, w_ukv=jnp.concatenate([d_k, d_v], axis=1), w_out=full["w_out"])
    else:
        pieces = FFN_PIECES
        mats = dict(w_gate=full["w_gu"][:D_FF], w_up=full["w_gu"][D_FF:], w_down=full["w_down"])
    parts = []
    for name, rows, cols, _ in pieces:
        if name == "conv":
            parts.append(jnp.zeros((N_DEV, rows, PACK_W), BF16))
        else:
            parts.append(jnp.pad(mats[name].reshape(N_DEV, rows, cols), ((0, 0), (0, -rows % 16), (0, PACK_W - cols))))
    return jnp.concatenate(parts, axis=1)


SMALL = (("mix_pre_g", (D,)), ("mix_post_g", (D,)), ("ffn_pre_g", (D,)), ("ffn_post_g", (D,)), ("q_norm_g", (Q_RANK,)),
         ("kv_norm_g", (KV_RANK,)), ("sg_ln_g", (SG_W,)), ("sg_ln_b", (SG_W,)), ("w_sp", (4, CHUNK, CHUNK)),
         ("b_sp", (4, CHUNK)), ("out_norm_g", (D,)))
SMALL_ROWS = 576


def _pack_small(vals, nl):
    flat = jnp.concatenate([vals[name].reshape(nl, -1) for name, _ in SMALL] + [vals["conv_w"].reshape(nl, -1)], axis=1)
    return jnp.pad(flat, ((0, 0), (0, SMALL_ROWS * 128 - flat.shape[1]))).reshape(nl * SMALL_ROWS, 128)


def _unpack_small(pack, nl):
    flat = pack.reshape(nl, SMALL_ROWS * 128)
    out, off = {}, 0
    for name, shape in SMALL + (("conv_w", (3, CV_W)),):
        n = int(np.prod(shape))
        out[name] = flat[:, off:off + n].reshape((nl,) + shape)
        off += n
    return out


def _layer_fwd(x, lw, sp, tabs, consts, next_pack):
    ca, sb, sc = tabs
    z = pre_in_fwd(x, sp["mix_pre_g"], lw["w_in"])
    q, k, v = mla_proj_fwd(z, ca, sb, sc, sp["q_norm_g"], sp["kv_norm_g"], lw["w_uq"], lw["w_ukv"])
    ya, lse, ffn_gathered, *mix_gathered = attn_fwd(q, k, v, next_pack)
    lw["ffn"] = ffn_gathered
    x1 = mix_fwd(x, z, ya, consts["gm"], sp["sg_ln_g"], sp["sg_ln_b"], sp["w_sp"], sp["bias"], lw["conv_w"],
                 sp["out_norm_g"], lw["w_out"], sp["mix_post_g"])
    x2, f = ffn_fwd(x1, sp["ffn_pre_g"], lw["ffn"], sp["ffn_post_g"])
    return x2, (x, z, q, k, v, ya, lse, x1, f), mix_gathered


def _layer_bwd(dx2, saved, lw, sp, tabs, consts, pending):
    ca, sb, sc = tabs
    x, z, q, k, v, ya, lse, x1, f = saved
    dx1, h2, dab, s, df, d_ffn_pre, d_ffn_post = ffn_bwd(x1, f, dx2, sp["ffn_pre_g"], lw["ffn"], sp["ffn_post_g"])
    ffn_chunks = _grad_chunks(dict(w_gu=atb(dab, h2, 1408), w_down=atb(s, df, 1408)))
    dya, dyc, dzsg, mix, do, d_mix_post, d_out_norm, d_lng, d_lnb, d_wsp, d_bias = mix_bwd(
        dx1, z, ya, consts["gm"], sp["sg_ln_g"], sp["sg_ln_b"], sp["w_sp"], sp["w_sp_t"], sp["bias"], lw["conv_w"],
        sp["out_norm_g"], lw["w_out"], sp["mix_post_g"])
    d_w_out = atb(mix, do, 1024)
    dzcv, d_cw = conv_bwd(dyc, z, lw["conv_w"])
    dq, dk, dv, *received = attn_bwd(q, k, v, ya, lse, dya, ((ffn_chunks, True),) + pending)
    dzmla, cq, ckv, dqp, dkvp, d_gq, d_gkv = mla_proj_bwd(dq, dk, dv, z, ca, sb, sc, sp["q_norm_g"], sp["kv_norm_g"],
                                                          lw["w_uq"], lw["w_ukv"])
    d_w_uq = atb(dqp, cq, 1024)
    d_w_ukv = atb(dkvp, ckv, 1536)
    dx, h1, dz, d_mix_pre = pre_in_bwd(x, dx1, dzcv, dzmla, dzsg, sp["mix_pre_g"], lw["w_in"])
    d_w_in = atb(dz, h1, 2048)
    mix_chunks = _grad_chunks(dict(w_in=d_w_in, w_uq=d_w_uq, w_ukv=d_w_ukv, w_out=d_w_out))
    d_bsp = d_bias[:, ::GROUP].T
    small = dict(mix_pre_g=d_mix_pre[0], mix_post_g=d_mix_post[0], ffn_pre_g=d_ffn_pre[0], ffn_post_g=d_ffn_post[0],
                 q_norm_g=d_gq[0], kv_norm_g=d_gkv[0], sg_ln_g=d_lng[0], sg_ln_b=d_lnb[0], w_sp=d_wsp, b_sp=d_bsp,
                 out_norm_g=d_out_norm[0], conv_w=d_cw[:3])
    small_pack = _pack_small({name: a[None] for name, a in small.items()}, 1)
    return dx, ((mix_chunks, True), (small_pack, False)), received


def kernel(x, positions, mix_pre_g, mix_post_g, ffn_pre_g, ffn_post_g, w_in, q_norm_g, w_uq, kv_norm_g, w_ukv, sg_ln_g, sg_ln_b, w_sp, b_sp, conv_w, out_norm_g, w_out, w_gate, w_up, w_down, loss_target, m_mix_pre_g, m_mix_post_g, m_ffn_pre_g, m_ffn_post_g, m_w_in, m_q_norm_g, m_w_uq, m_kv_norm_g, m_w_ukv, m_sg_ln_g, m_sg_ln_b, m_w_sp, m_b_sp, m_conv_w, m_out_norm_g, m_w_out, m_w_gate, m_w_up, m_w_down, v_mix_pre_g, v_mix_post_g, v_ffn_pre_g, v_ffn_post_g, v_w_in, v_q_norm_g, v_w_uq, v_kv_norm_g, v_w_ukv, v_sg_ln_g, v_sg_ln_b, v_w_sp, v_b_sp, v_conv_w, v_out_norm_g, v_w_out, v_w_gate, v_w_up, v_w_down):
    nl = w_in.shape[0]
    t = x.shape[1]
    w = dict(mix_pre_g=mix_pre_g, mix_post_g=mix_post_g, ffn_pre_g=ffn_pre_g, ffn_post_g=ffn_post_g, w_in=w_in,
             q_norm_g=q_norm_g, w_uq=w_uq, kv_norm_g=kv_norm_g, w_ukv=w_ukv, sg_ln_g=sg_ln_g, sg_ln_b=sg_ln_b, w_sp=w_sp,
             b_sp=b_sp, conv_w=conv_w, out_norm_g=out_norm_g, w_out=w_out, w_gate=w_gate, w_up=w_up, w_down=w_down)
    m = dict(mix_pre_g=m_mix_pre_g, mix_post_g=m_mix_post_g, ffn_pre_g=m_ffn_pre_g, ffn_post_g=m_ffn_post_g, w_in=m_w_in,
             q_norm_g=m_q_norm_g, w_uq=m_w_uq, kv_norm_g=m_kv_norm_g, w_ukv=m_w_ukv, sg_ln_g=m_sg_ln_g, sg_ln_b=m_sg_ln_b,
             w_sp=m_w_sp, b_sp=m_b_sp, conv_w=m_conv_w, out_norm_g=m_out_norm_g, w_out=m_w_out, w_gate=m_w_gate,
             w_up=m_w_up, w_down=m_w_down)
    v = dict(mix_pre_g=v_mix_pre_g, mix_post_g=v_mix_post_g, ffn_pre_g=v_ffn_pre_g, ffn_post_g=v_ffn_post_g, w_in=v_w_in,
             q_norm_g=v_q_norm_g, w_uq=v_w_uq, kv_norm_g=v_kv_norm_g, w_ukv=v_w_ukv, sg_ln_g=v_sg_ln_g, sg_ln_b=v_sg_ln_b,
             w_sp=v_w_sp, b_sp=v_b_sp, conv_w=v_conv_w, out_norm_g=v_out_norm_g, w_out=v_w_out, w_gate=v_w_gate,
             w_up=v_w_up, w_down=v_w_down)

    mix_pack = _to_pack(w, BF16, MIX_PIECES, conv=w["conv_w"])
    ffn_pack = _to_pack(w, BF16, FFN_PIECES)
    consts = dict(gm=jnp.asarray(np.kron(np.eye(SG_W // GROUP), np.full((GROUP, GROUP), 1.0 / GROUP)), BF16))
    smalls = []
    for l in range(nl):
        sp = {name: w[name][l].reshape(1, -1) for name, shape in SMALL if len(shape) == 1}
        sp["w_sp"] = w["w_sp"][l]
        sp["w_sp_t"] = jnp.swapaxes(w["w_sp"][l], 1, 2)
        sp["bias"] = jnp.repeat(w["b_sp"][l].T, GROUP, axis=1)
        smalls.append(sp)
    inv_freq = 1.0 / (ROPE_THETA ** (jnp.arange(0, ROPE // 2, dtype=F32) / (ROPE // 2)))
    inv = jnp.zeros((1, HEAD_PAD), F32).at[0, NOPE:NOPE + ROPE].set(jnp.concatenate([inv_freq, inv_freq]))
    tabs = rope_tables(positions.reshape(t, 1).astype(F32), inv)

    h = x[0]
    saved, layers = [], []
    mix_gathered = [all_gather(mix_pack[0])]
    for l in range(nl):
        layers.append(_mix_weights(mix_gathered[0]))
        carried = (ffn_pack[l],) + ((mix_pack[l + 1],) if l + 1 < nl else ())
        h, s, mix_gathered = _layer_fwd(h, layers[l], smalls[l], tabs, consts, carried)
        saved.append(s)
    sq, dh = loss_head(h, loss_target[0])
    loss = lax.psum(0.5 * sq[0, 0] / D, ("x", "y", "c"))

    got_ffn, got_mix, got_small = [None] * nl, [None] * nl, [None] * nl
    pending = ()
    for l in reversed(range(nl)):
        dh, new_pending, received = _layer_bwd(dh, saved[l], layers[l], smalls[l], tabs, consts, pending)
        got_ffn[l] = received[0]
        if pending:
            got_mix[l + 1], got_small[l + 1] = received[1:]
        pending = new_pending

    me = 4 * lax.axis_index("x") + 2 * lax.axis_index("y") + lax.axis_index("c")
    *ffn_new, got_mix[0], got_small[0] = sum_adamw(got_ffn, *[_to_pack(d, F32, FFN_PIECES) for d in (w, m, v)], 176,
                                                   carried=pending)
    mix_new = sum_adamw(got_mix, *[_to_pack(d, F32, MIX_PIECES) for d in (w, m, v)], 208)
    got_small = jnp.concatenate(got_small, axis=1)
    g_big, d_big, m_big, v_big = [{**_from_pack(a, FFN_PIECES), **_from_pack(b, MIX_PIECES)}
                                  for a, b in zip(ffn_new, mix_new)]

    def full_conv(a):
        return lax.dynamic_update_slice(jnp.zeros((nl, 3, CV_W), F32), a, (0, 0, me * (CV_W // N_DEV)))

    def small_pack(d):
        return _pack_small({**{name: d[name] for name, _ in SMALL}, "conv_w": full_conv(d["conv_w"])}, nl)

    g_small, d_small, m_small, v_small = [_unpack_small(p[0], nl) for p in
                                          sum_adamw([got_small], small_pack(w)[None], small_pack(m)[None],
                                                    small_pack(v)[None], 1152)]
    outs = []
    for big, small in ((g_big, g_small), (d_big, d_small), (m_big, m_small), (v_big, v_small)):
        for name in w:
            if name == "conv_w":
                outs.append(lax.dynamic_slice(small[name], (0, 0, me * (CV_W // N_DEV)), (nl, 3, CV_W // N_DEV)))
            elif name in small:
                outs.append(small[name])
            else:
                outs.append(big[name])
    return (loss, dh[None], *outs)
```

```python
import functools

import jax
import jax.numpy as jnp
import numpy as np
from jax import lax
from jax.experimental import pallas as pl
from jax.experimental.pallas import tpu as pltpu

F32 = jnp.float32
BF16 = jnp.bfloat16

D = 1024
Q_RANK = 384
KV_RANK = 256
ROPE = 32
HEADS = 8
NOPE = 64
V_DIM = 64
HEAD_PAD = 128
SG_W = 256
CV_W = 256
CHUNK = 128
GROUP = 64
D_FF = 2816
IN_W = 1952
Z_W = 2048
Z_CV, Z_MLA, Z_SG = 0, 768, 1536
EPS = 1e-6
ROPE_THETA = 10000.0
SCALE = (NOPE + ROPE) ** -0.5
LOG2E = 1.4426950408889634
SCALE_LOG2E = SCALE * LOG2E
NEG = -1e30
N_DEV = 8

ADAM_LR, ADAM_B1, ADAM_B2, ADAM_EPS, ADAM_WD, ADAM_STEP = 0.001, 0.9, 0.999, 1e-08, 0.01, 10

VMEM_LIMIT = 56 * 1024 * 1024

TM = 512
TM_FFN = 256
FFN_SLAB = 256
TQ = 512
TT = 2048


def _cp(sem, vmem=VMEM_LIMIT):
    return pltpu.CompilerParams(dimension_semantics=sem, vmem_limit_bytes=vmem)


def _whole():
    return pl.BlockSpec(memory_space=pltpu.VMEM)


def _mm(a, b):
    return jnp.dot(a, b, preferred_element_type=F32)


def _mm_nt(a, b):
    return lax.dot_general(a, b, (((1,), (1,)), ((), ())), preferred_element_type=F32)


def _mm_tn(a, b):
    return lax.dot_general(a, b, (((0,), (0,)), ((), ())), preferred_element_type=F32)


def _rms_fwd(x, g):
    r = lax.rsqrt(jnp.mean(x * x, axis=-1, keepdims=True) + EPS)
    xh = x * r
    return xh * g, xh, r


def _rms_bwd(xh, r, g, dy):
    dxh = dy * g
    dx = r * (dxh - xh * jnp.mean(dxh * xh, axis=-1, keepdims=True))
    dg = jnp.sum(dy * xh, axis=0, keepdims=True)
    return dx, dg


def _gmean(v, gm):
    hi = v.astype(BF16)
    lo = (v - hi.astype(F32)).astype(BF16)
    return _mm(hi, gm) + _mm(lo, gm)


def _gelu(x):
    c = np.float32(np.sqrt(2.0 / np.pi))
    u = c * (x + 0.044715 * x * x * x)
    t = jnp.tanh(u)
    return 0.5 * x * (1.0 + t), t


def _gelu_grad(x, t):
    c = np.float32(np.sqrt(2.0 / np.pi))
    return 0.5 * (1.0 + t) + 0.5 * x * (1.0 - t * t) * c * (1.0 + 3.0 * 0.044715 * x * x)


def _rope(t, ca, sb, sc):
    return t * ca + pltpu.roll(t, HEAD_PAD - 16, 1) * sb + pltpu.roll(t, 16, 1) * sc


def _rope_t(dt, ca, sb, sc):
    return dt * ca + pltpu.roll(dt * sb, 16, 1) + pltpu.roll(dt * sc, HEAD_PAD - 16, 1)


def _shift_down(y, k, head):
    n = y.shape[0]
    out = pltpu.roll(y, k, 0)
    row = lax.broadcasted_iota(jnp.int32, y.shape, 0)
    for j in range(k):
        out = jnp.where(row == j, head[8 - k + j:8 - k + j + 1, :], out)
    return out


def _shift_up(y, k, tail):
    n = y.shape[0]
    out = pltpu.roll(y, n - k, 0)
    row = lax.broadcasted_iota(jnp.int32, y.shape, 0)
    for j in range(k):
        out = jnp.where(row == n - k + j, tail[j:j + 1, :], out)
    return out


def rope_tables(pos, inv):
    t = pos.shape[0]
    tm = min(TM, t)

    def body(pos_ref, inv_ref, ca_ref, sb_ref, sc_ref):
        ang = pos_ref[...] * inv_ref[...]
        c = jnp.cos(ang)
        s = jnp.sin(ang)
        lane = lax.broadcasted_iota(jnp.int32, ang.shape, 1)
        ca_ref[...] = jnp.where(lane < NOPE, 1.0, jnp.where(lane < NOPE + ROPE, c, 0.0))
        sb_ref[...] = jnp.where((lane >= NOPE) & (lane < NOPE + 16), -s, 0.0)
        sc_ref[...] = jnp.where((lane >= NOPE + 16) & (lane < NOPE + ROPE), s, 0.0)

    out = jax.ShapeDtypeStruct((t, HEAD_PAD), F32)
    blk = pl.BlockSpec((tm, HEAD_PAD), lambda i: (i, 0))
    return pl.pallas_call(
        body, name="rope_tables", grid=(t // tm,),
        in_specs=[pl.BlockSpec((tm, 1), lambda i: (i, 0)), pl.BlockSpec((1, HEAD_PAD), lambda i: (0, 0))],
        out_specs=[blk, blk, blk], out_shape=[out, out, out],
        compiler_params=_cp(("parallel",)),
    )(pos, inv)


def pre_in_fwd(x, g, w):
    t = x.shape[0]
    tm = min(TM, t)

    def body(x_ref, g_ref, w_ref, z_ref):
        h, _, _ = _rms_fwd(x_ref[...], g_ref[...])
        z_ref[...] = _mm_nt(h.astype(BF16), w_ref[...])

    return pl.pallas_call(
        body, name="pre_in_fwd", grid=(t // tm,),
        in_specs=[pl.BlockSpec((tm, D), lambda i: (i, 0)), _whole(), _whole()],
        out_specs=pl.BlockSpec((tm, Z_W), lambda i: (i, 0)),
        out_shape=jax.ShapeDtypeStruct((t, Z_W), F32),
        compiler_params=_cp(("parallel",)),
    )(x, g, w)


def mla_proj_fwd(z, ca, sb, sc, gq, gkv, wuq, wukv):
    t = z.shape[0]
    tm = min(TM, t)

    def body(z_ref, ca_ref, sb_ref, sc_ref, gq_ref, gkv_ref, wuq_ref, wukv_ref, q_ref, k_ref, v_ref):
        z = z_ref[...]
        ca, sb, sc = ca_ref[...], sb_ref[...], sc_ref[...]
        cq, _, _ = _rms_fwd(z[:, :Q_RANK], gq_ref[...])
        ckv, _, _ = _rms_fwd(z[:, Q_RANK:Q_RANK + KV_RANK], gkv_ref[...])
        q = _mm_nt(cq.astype(BF16), wuq_ref[...])
        kv = _mm_nt(ckv.astype(BF16), wukv_ref[...])
        kr = _rope(pltpu.roll(z[:, Q_RANK + KV_RANK:], NOPE, 1), ca, sb, sc)
        for h in range(HEADS):
            lanes = slice(h * HEAD_PAD, (h + 1) * HEAD_PAD)
            q_ref[:, lanes] = _rope(q[:, lanes], ca, sb, sc).astype(BF16)
            k_ref[:, lanes] = (kv[:, lanes] + kr).astype(BF16)
        v_ref[...] = kv[:, HEADS * HEAD_PAD:].astype(BF16)

    tab = pl.BlockSpec((tm, HEAD_PAD), lambda i: (i, 0))
    return pl.pallas_call(
        body, name="mla_proj_fwd", grid=(t // tm,),
        in_specs=[pl.BlockSpec((tm, 768), lambda i: (i, 1)), tab, tab, tab, _whole(), _whole(), _whole(), _whole()],
        out_specs=[pl.BlockSpec((tm, HEADS * HEAD_PAD), lambda i: (i, 0)),
                   pl.BlockSpec((tm, HEADS * HEAD_PAD), lambda i: (i, 0)),
                   pl.BlockSpec((tm, HEADS * V_DIM), lambda i: (i, 0))],
        out_shape=[jax.ShapeDtypeStruct((t, HEADS * HEAD_PAD), BF16),
                   jax.ShapeDtypeStruct((t, HEADS * HEAD_PAD), BF16),
                   jax.ShapeDtypeStruct((t, HEADS * V_DIM), BF16)],
        compiler_params=_cp(("parallel",)),
    )(z, ca, sb, sc, gq, gkv, wuq, wukv)


def _each(stages, k):
    def run():
        for stage in stages:
            stage[k]()
    return run


def attn_fwd(q, k, v, carried=()):
    t = q.shape[0]
    tq = min(TQ, t)
    nq = t // tq
    last_pair = HEADS // 2 - 1
    n = len(carried)

    def body(*refs):
        q_ref, k_ref, v_ref = refs[:3]
        o_ref, lse_ref = refs[3 + n:5 + n]
        sems = refs[5 + 2 * n:]
        stages = [_gather_steps(refs[3 + a], refs[5 + n + a], *sems[3 * a:3 * a + 3]) for a in range(n)]
        if n:
            pair = pl.program_id(0)
            pl.when((pair == 0) & (pl.program_id(1) == 0))(_each(stages, 0))
            pl.when((pair == last_pair) & (pl.program_id(1) == 0))(_each(stages, 1))
        i = pl.program_id(1)
        row = lax.broadcasted_iota(jnp.int32, (tq, tq), 0)
        col = lax.broadcasted_iota(jnp.int32, (tq, tq), 1)
        head_lanes = [slice(h * HEAD_PAD, (h + 1) * HEAD_PAD) for h in range(2)]

        def step(j, carry, masked):
            start = pl.multiple_of(j * tq, tq)
            vb = v_ref[pl.ds(start, tq), :]
            out = []
            for h in range(2):
                m, l, acc = carry[h]
                s = _mm_nt(q_ref[:, head_lanes[h]], k_ref[pl.ds(start, tq), head_lanes[h]])
                if masked:
                    s = jnp.where(col <= row, s, NEG)
                m_new = jnp.maximum(m, jnp.max(s, axis=-1, keepdims=True))
                p = jnp.exp2((s - m_new) * SCALE_LOG2E)
                alpha = jnp.exp2((m - m_new) * SCALE_LOG2E)
                l = alpha * l + jnp.sum(p, axis=-1, keepdims=True)
                acc = alpha * acc + _mm(p.astype(BF16), vb)
                out.append((m_new, l, acc))
            return tuple(out)

        init = (jnp.full((tq, 1), NEG, F32), jnp.zeros((tq, 1), F32), jnp.zeros((tq, 2 * V_DIM), F32))
        carry = lax.fori_loop(0, i // 2, lambda j, c: step(2 * j + 1, step(2 * j, c, False), False), (init, init))
        carry = lax.fori_loop(0, i % 2, lambda _, c: step(i - 1, c, False), carry)
        outs = []
        for h, (m, l, acc) in enumerate(step(i, carry, True)):
            outs.append(acc / l)
            lse_ref[:, head_lanes[h]] = jnp.broadcast_to(m * SCALE + jnp.log(l), (tq, HEAD_PAD))
        lane = lax.broadcasted_iota(jnp.int32, (tq, 2 * V_DIM), 1)
        o_ref[...] = jnp.where(lane < V_DIM, outs[0], outs[1])
        if n:
            pl.when((pl.program_id(0) == last_pair) & (i == nq - 1))(_each(stages, 2))

    hbm = pl.BlockSpec(memory_space=pl.ANY)
    return pl.pallas_call(
        body, name=f"attn_fwd_gather{n}" if n else "attn_fwd", grid=(HEADS // 2, nq),
        in_specs=[pl.BlockSpec((tq, 2 * HEAD_PAD), lambda p, i: (i, p)),
                  pl.BlockSpec((t, 2 * HEAD_PAD), lambda p, i: (0, p)),
                  pl.BlockSpec((t, 2 * V_DIM), lambda p, i: (0, p))] + [hbm] * n,
        out_specs=[pl.BlockSpec((tq, 2 * V_DIM), lambda p, i: (i, p)),
                   pl.BlockSpec((tq, 2 * HEAD_PAD), lambda p, i: (i, p))] + [hbm] * n,
        out_shape=[jax.ShapeDtypeStruct((t, HEADS * V_DIM), F32), jax.ShapeDtypeStruct((t, HEADS * HEAD_PAD), F32)]
        + [jax.ShapeDtypeStruct((N_DEV,) + c.shape, c.dtype) for c in carried],
        scratch_shapes=_comm_sems() * n,
        compiler_params=_cp(("arbitrary", "arbitrary") if n else ("parallel", "parallel")),
    )(q, k, v, *carried)


def _sgu_fwd(zsg, gm, lng, lnb, wc_ref, bias, mixed_ref):
    uv, th = _gelu(zsg)
    u, v0 = uv[:, :SG_W], uv[:, SG_W:]
    vc = v0 - _gmean(v0, gm)
    r = lax.rsqrt(_gmean(vc * vc, gm) + EPS)
    vh = vc * r
    v = vh * lng + lnb
    lane = lax.broadcasted_iota(jnp.int32, (CHUNK, SG_W), 1)
    for c in range(zsg.shape[0] // CHUNK):
        rows = slice(c * CHUNK, (c + 1) * CHUNK)
        vb = v[rows].astype(BF16)
        mixed = bias
        for g in range(SG_W // GROUP):
            mixed = mixed + jnp.where(lane // GROUP == g, _mm(wc_ref[g], vb), 0.0)
        mixed_ref[rows, :] = mixed
    return u, v, vh, r, th


def _conv_fwd(zcv, halo, first, cw):
    gb, gc, hh = zcv[:, :CV_W], zcv[:, CV_W:2 * CV_W], zcv[:, 2 * CV_W:]
    y = gc * hh
    yh = jnp.where(first, 0.0, halo[:, CV_W:2 * CV_W] * halo[:, 2 * CV_W:])
    y1 = _shift_down(y, 1, yh)
    y2 = _shift_down(y, 2, yh)
    conv = y2 * cw[0:1, :] + y1 * cw[1:2, :] + y * cw[2:3, :]
    return gb * conv, conv, y, y1, y2


def _tril_bf16(w_ref, g):
    row = lax.broadcasted_iota(jnp.int32, (CHUNK, CHUNK), 0)
    col = lax.broadcasted_iota(jnp.int32, (CHUNK, CHUNK), 1)
    return jnp.where(col <= row, w_ref[g], 0.0).astype(BF16)


def mix_fwd(x, z, ya, gm, lng, lnb, wsp, bias, cw, gout, wout, gpost):
    t = x.shape[0]
    tm = min(TM, t)

    def body(x_ref, zcv_ref, halo_ref, zsg_ref, ya_ref, gm_ref, lng_ref, lnb_ref, wsp_ref, bias_ref, cw_ref,
             gout_ref, wout_ref, gpost_ref, x1_ref, wc_ref, mixed_ref):
        i = pl.program_id(0)
        for g in range(SG_W // GROUP):
            wc_ref[g] = _tril_bf16(wsp_ref, g)
        u, _, _, _, _ = _sgu_fwd(zsg_ref[...], gm_ref[...], lng_ref[...], lnb_ref[...], wc_ref, bias_ref[...], mixed_ref)
        yb = u * mixed_ref[...]
        yc, _, _, _, _ = _conv_fwd(zcv_ref[...], halo_ref[...], i == 0, cw_ref[...])
        gout = gout_ref[...]
        na, _, _ = _rms_fwd(ya_ref[...], gout[:, :512])
        nb, _, _ = _rms_fwd(yb, gout[:, 512:768])
        nc, _, _ = _rms_fwd(yc, gout[:, 768:])
        mix = jnp.concatenate([na, nb, nc], axis=1).astype(BF16)
        o, _, _ = _rms_fwd(_mm(mix, wout_ref[...]), gpost_ref[...])
        x1_ref[...] = x_ref[...] + o

    hb = tm // 8
    return pl.pallas_call(
        body, name="mix_fwd", grid=(t // tm,),
        in_specs=[pl.BlockSpec((tm, D), lambda i: (i, 0)),
                  pl.BlockSpec((tm, 768), lambda i: (i, 0)),
                  pl.BlockSpec((8, 768), lambda i: (jnp.maximum(i * hb - 1, 0), 0)),
                  pl.BlockSpec((tm, 512), lambda i: (i, 3)),
                  pl.BlockSpec((tm, 512), lambda i: (i, 0)),
                  _whole(), _whole(), _whole(), _whole(), _whole(), _whole(), _whole(), _whole(), _whole()],
        out_specs=pl.BlockSpec((tm, D), lambda i: (i, 0)),
        out_shape=jax.ShapeDtypeStruct((t, D), F32),
        scratch_shapes=[pltpu.VMEM((SG_W // GROUP, CHUNK, CHUNK), BF16), pltpu.VMEM((tm, SG_W), F32)],
        compiler_params=_cp(("arbitrary",)),
    )(x, z, z, z, ya, gm, lng, lnb, wsp, bias, cw, gout, wout, gpost)


def _sigmoid(a):
    return 1.0 / (1.0 + jnp.exp(-a))


FFN_SHARD = D_FF // N_DEV


def _load_ffn_weights(g_ref, wgu_ref, wd_ref, sems):
    copies = []
    for j in range(N_DEV):
        for p, (dst, base) in enumerate(((wgu_ref, 0), (wgu_ref, D_FF), (wd_ref, 0))):
            copies.append(pltpu.make_async_copy(g_ref.at[j, pl.ds(p * FFN_SHARD, FFN_SHARD)],
                                                dst.at[pl.ds(base + j * FFN_SHARD, FFN_SHARD)], sems.at[3 * j + p]))
    for cp in copies:
        cp.start()
    for cp in copies:
        cp.wait()


def _ffn_weight_scratch():
    return [pltpu.VMEM((2 * D_FF, D), BF16), pltpu.VMEM((D_FF, D), BF16), pltpu.SemaphoreType.DMA((3 * N_DEV,))]


def ffn_fwd(x1, gpre, gathered, gpost):
    t = x1.shape[0]
    tm = min(TM_FFN, t)

    def body(x_ref, gpre_ref, g_ref, gpost_ref, x2_ref, f_ref, wgu_ref, wd_ref, sems):
        @pl.when(pl.program_id(0) == 0)
        def _():
            _load_ffn_weights(g_ref, wgu_ref, wd_ref, sems)

        x = x_ref[...]
        h, _, _ = _rms_fwd(x, gpre_ref[...])
        ab = _mm_nt(h.astype(BF16), wgu_ref[...])
        a, b = ab[:, :D_FF], ab[:, D_FF:]
        s = a * _sigmoid(a) * b
        f = _mm(s.astype(BF16), wd_ref[...])
        f_ref[...] = f
        x2_ref[...] = x + _rms_fwd(f, gpost_ref[...])[0]

    row = pl.BlockSpec((tm, D), lambda i: (i, 0))
    return pl.pallas_call(
        body, name="ffn_fwd", grid=(t // tm,),
        in_specs=[row, _whole(), pl.BlockSpec(memory_space=pl.ANY), _whole()],
        out_specs=[row, row],
        out_shape=[jax.ShapeDtypeStruct((t, D), F32), jax.ShapeDtypeStruct((t, D), F32)],
        scratch_shapes=_ffn_weight_scratch(),
        compiler_params=_cp(("arbitrary",)),
    )(x1, gpre, gathered, gpost)


def loss_head(y, target):
    t = y.shape[0]
    tm = min(TM, t)

    def body(y_ref, t_ref, loss_ref, dy_ref):
        @pl.when(pl.program_id(0) == 0)
        def _():
            loss_ref[...] = jnp.zeros_like(loss_ref)

        e = y_ref[...] - t_ref[...]
        dy_ref[...] = e * (1.0 / D)
        loss_ref[...] += jnp.sum(jnp.sum(e * e, axis=-1, keepdims=True), axis=0, keepdims=True)

    return pl.pallas_call(
        body, name="loss_head", grid=(t // tm,),
        in_specs=[pl.BlockSpec((tm, D), lambda i: (i, 0)), pl.BlockSpec((tm, D), lambda i: (i, 0))],
        out_specs=[pl.BlockSpec((1, 128), lambda i: (0, 0)), pl.BlockSpec((tm, D), lambda i: (i, 0))],
        out_shape=[jax.ShapeDtypeStruct((1, 128), F32), jax.ShapeDtypeStruct((t, D), F32)],
        compiler_params=_cp(("arbitrary",)),
    )(y, target)


def _acc(ref, first, val):
    @pl.when(first)
    def _():
        ref[...] = val

    @pl.when(jnp.logical_not(first))
    def _():
        ref[...] += val


def ffn_bwd(x1, f, dx2, gpre, gathered, gpost, carried=()):
    t = x1.shape[0]
    tm = min(TM_FFN, t)
    steps = t // tm
    n = len(carried)

    def body(*refs):
        x_ref, f_ref, dx2_ref, gpre_ref, g_ref, gpost_ref = refs[:6]
        dx1_ref, h_ref, dab_ref, s_ref, df_ref, dgpre_ref, dgpost_ref = refs[6 + n:13 + n]
        ab_ref, ds_ref, wgu_ref, wd_ref, sems = refs[13 + 2 * n:18 + 2 * n]
        comm_sems = refs[18 + 2 * n:]
        stages = [_exchange_steps(refs[6 + a], refs[13 + n + a], carried[a][1], *comm_sems[3 * a:3 * a + 3])
                  for a in range(n)]
        first = pl.program_id(0) == 0
        if n:
            pl.when(first)(_each(stages, 0))

        @pl.when(first)
        def _():
            _load_ffn_weights(g_ref, wgu_ref, wd_ref, sems)

        dx2 = dx2_ref[...]
        gpre, gpost = gpre_ref[...], gpost_ref[...]
        h, xh, rx = _rms_fwd(x_ref[...], gpre)
        h_ref[...] = h.astype(BF16)
        ab_ref[...] = _mm_nt(h_ref[...], wgu_ref[...])
        for c in range(0, D_FF, FFN_SLAB):
            a, b = ab_ref[:, c:c + FFN_SLAB], ab_ref[:, D_FF + c:D_FF + c + FFN_SLAB]
            s_ref[:, c:c + FFN_SLAB] = (a * _sigmoid(a) * b).astype(BF16)
        _, fh, rf = _rms_fwd(f_ref[...], gpost)
        df, dgpost = _rms_bwd(fh, rf, gpost, dx2)
        df_ref[...] = df.astype(BF16)
        ds_ref[...] = _mm_nt(df_ref[...], wd_ref[...])
        for c in range(0, D_FF, FFN_SLAB):
            a, b = ab_ref[:, c:c + FFN_SLAB], ab_ref[:, D_FF + c:D_FF + c + FFN_SLAB]
            ds = ds_ref[:, c:c + FFN_SLAB]
            sg = _sigmoid(a)
            dab_ref[:, c:c + FFN_SLAB] = (ds * b * (sg * (1.0 + a * (1.0 - sg)))).astype(BF16)
            dab_ref[:, D_FF + c:D_FF + c + FFN_SLAB] = (ds * (a * sg)).astype(BF16)
        dx, dgpre = _rms_bwd(xh, rx, gpre, _mm(dab_ref[...], wgu_ref[...]))
        dx1_ref[...] = dx2 + dx
        _acc(dgpre_ref, first, dgpre)
        _acc(dgpost_ref, first, dgpost)
        if n:
            pl.when(pl.program_id(0) == steps - 1)(_each(stages, 1))

    row = lambda w: pl.BlockSpec((tm, w), lambda i: (i, 0))
    vec = pl.BlockSpec((1, D), lambda i: (0, 0))
    hbm = pl.BlockSpec(memory_space=pl.ANY)
    return pl.pallas_call(
        body, name=f"ffn_bwd_exchange{n}" if n else "ffn_bwd", grid=(steps,),
        in_specs=[row(D), row(D), row(D), _whole(), hbm, _whole()] + [hbm] * n,
        out_specs=[row(D), row(D), row(2 * D_FF), row(D_FF), row(D), vec, vec] + [hbm] * n,
        out_shape=[jax.ShapeDtypeStruct((t, D), F32), jax.ShapeDtypeStruct((t, D), BF16),
                   jax.ShapeDtypeStruct((t, 2 * D_FF), BF16), jax.ShapeDtypeStruct((t, D_FF), BF16),
                   jax.ShapeDtypeStruct((t, D), BF16), jax.ShapeDtypeStruct((1, D), F32),
                   jax.ShapeDtypeStruct((1, D), F32)]
        + [jax.ShapeDtypeStruct(src.shape if scatter else (N_DEV,) + src.shape, src.dtype) for src, scatter in carried],
        scratch_shapes=[pltpu.VMEM((tm, 2 * D_FF), F32), pltpu.VMEM((tm, D_FF), F32)] + _ffn_weight_scratch()
        + _comm_sems() * n,
        compiler_params=_cp(("arbitrary",)),
    )(x1, f, dx2, gpre, gathered, gpost, *[src for src, _ in carried])


def atb(a, b, tk):
    t, k = a.shape
    n = b.shape[1]
    tt = min(TT, t)
    tk = min(tk, k)
    steps = t // tt

    def body(a_ref, b_ref, o_ref, acc_ref):
        i = pl.program_id(1)
        _acc(acc_ref, i == 0, _mm_tn(a_ref[...], b_ref[...]))

        @pl.when(i == steps - 1)
        def _():
            o_ref[...] = acc_ref[...].astype(BF16)

    return pl.pallas_call(
        body, name="atb", grid=(k // tk, steps),
        in_specs=[pl.BlockSpec((tt, tk), lambda j, i: (i, j)), pl.BlockSpec((tt, n), lambda j, i: (i, 0))],
        out_specs=pl.BlockSpec((tk, n), lambda j, i: (j, 0)),
        out_shape=jax.ShapeDtypeStruct((k, n), BF16),
        scratch_shapes=[pltpu.VMEM((tk, n), F32)],
        compiler_params=_cp(("parallel", "arbitrary")),
    )(a, b)


def mix_bwd(dx1, z, ya, gm, lng, lnb, wsp, wspt, bias, cw, gout, wout, gpost):
    t = dx1.shape[0]
    tm = min(TM, t)
    ng = SG_W // GROUP

    def body(dx1_ref, zcv_ref, halo_ref, zsg_ref, ya_ref, gm_ref, lng_ref, lnb_ref, wsp_ref, wspt_ref, bias_ref,
             cw_ref, gout_ref, wout_ref, gpost_ref,
             dya_ref, dyc_ref, dzsg_ref, mix_ref, do_ref, dgpost_ref, dgout_ref, dlng_ref, dlnb_ref, dwsp_ref,
             dbias_ref, wc_ref, wct_ref, mixed_ref, dv_ref):
        i = pl.program_id(0)
        first = i == 0
        gm = gm_ref[...]
        for g in range(ng):
            wc_ref[g] = _tril_bf16(wsp_ref, g)
            wct_ref[g] = jnp.where(
                lax.broadcasted_iota(jnp.int32, (CHUNK, CHUNK), 0) <= lax.broadcasted_iota(jnp.int32, (CHUNK, CHUNK), 1),
                wspt_ref[g], 0.0).astype(BF16)
        zsg = zsg_ref[...]
        lng = lng_ref[...]
        u, v, vh, r, th = _sgu_fwd(zsg, gm, lng, lnb_ref[...], wc_ref, bias_ref[...], mixed_ref)
        mixed = mixed_ref[...]
        yb = u * mixed
        yc, _, _, _, _ = _conv_fwd(zcv_ref[...], halo_ref[...], first, cw_ref[...])
        gout, gpost = gout_ref[...], gpost_ref[...]
        ga, gb_, gc_ = gout[:, :512], gout[:, 512:768], gout[:, 768:]
        na, yah, ra = _rms_fwd(ya_ref[...], ga)
        nb, ybh, rb = _rms_fwd(yb, gb_)
        nc, ych, rc = _rms_fwd(yc, gc_)
        mix = jnp.concatenate([na, nb, nc], axis=1).astype(BF16)
        _, oh, ro = _rms_fwd(_mm(mix, wout_ref[...]), gpost)
        do, dgpost = _rms_bwd(oh, ro, gpost, dx1_ref[...])
        dob = do.astype(BF16)
        dmix = _mm_nt(dob, wout_ref[...])
        dya, dga = _rms_bwd(yah, ra, ga, dmix[:, :512])
        dyb, dgb = _rms_bwd(ybh, rb, gb_, dmix[:, 512:768])
        dyc, dgc = _rms_bwd(ych, rc, gc_, dmix[:, 768:])
        dya_ref[...] = dya
        dyc_ref[...] = dyc
        mix_ref[...] = mix
        do_ref[...] = dob
        _acc(dgpost_ref, first, dgpost)
        _acc(dgout_ref, first, jnp.concatenate([dga, dgb, dgc], axis=1))
        du = dyb * mixed
        dmixed = dyb * u
        lane = lax.broadcasted_iota(jnp.int32, (CHUNK, SG_W), 1)
        row = lax.broadcasted_iota(jnp.int32, (CHUNK, CHUNK), 0)
        col = lax.broadcasted_iota(jnp.int32, (CHUNK, CHUNK), 1)
        dbias = jnp.zeros((CHUNK, SG_W), F32)
        dw = [jnp.zeros((CHUNK, CHUNK), F32) for _ in range(ng)]
        for c in range(tm // CHUNK):
            rows = slice(c * CHUNK, (c + 1) * CHUNK)
            dm = dmixed[rows]
            dbias = dbias + dm
            dmb = dm.astype(BF16)
            vb = v[rows].astype(BF16)
            dvc = jnp.zeros((CHUNK, SG_W), F32)
            for g in range(ng):
                in_g = lane // GROUP == g
                dvc = dvc + jnp.where(in_g, _mm(wct_ref[g], dmb), 0.0)
                dw[g] = dw[g] + _mm_nt(jnp.where(in_g, dmb, jnp.zeros_like(dmb)), vb)
            dv_ref[rows, :] = dvc
        for g in range(ng):
            dwg = jnp.where(col <= row, dw[g], 0.0)

            @pl.when(first)
            def _():
                dwsp_ref[g] = dwg

            @pl.when(jnp.logical_not(first))
            def _():
                dwsp_ref[g] += dwg
        _acc(dbias_ref, first, _gmean(dbias, gm) * GROUP)
        dv = dv_ref[...]
        _acc(dlng_ref, first, jnp.sum(dv * vh, axis=0, keepdims=True))
        _acc(dlnb_ref, first, jnp.sum(dv, axis=0, keepdims=True))
        dvh = dv * lng
        dv0 = r * (dvh - _gmean(dvh, gm) - vh * _gmean(dvh * vh, gm))
        dzsg_ref[...] = (jnp.concatenate([du, dv0], axis=1) * _gelu_grad(zsg, th)).astype(BF16)

    hb = tm // 8
    row_ = lambda w: pl.BlockSpec((tm, w), lambda i: (i, 0))
    vec = lambda w: pl.BlockSpec((1, w), lambda i: (0, 0))
    return pl.pallas_call(
        body, name="mix_bwd", grid=(t // tm,),
        in_specs=[row_(D),
                  pl.BlockSpec((tm, 768), lambda i: (i, 0)),
                  pl.BlockSpec((8, 768), lambda i: (jnp.maximum(i * hb - 1, 0), 0)),
                  pl.BlockSpec((tm, 512), lambda i: (i, 3)),
                  row_(512),
                  _whole(), _whole(), _whole(), _whole(), _whole(), _whole(), _whole(), _whole(), _whole(), _whole()],
        out_specs=[row_(512), row_(CV_W), row_(512), row_(D), row_(D), vec(D), vec(D), vec(SG_W), vec(SG_W),
                   pl.BlockSpec((ng, CHUNK, CHUNK), lambda i: (0, 0, 0)),
                   pl.BlockSpec((CHUNK, SG_W), lambda i: (0, 0))],
        out_shape=[jax.ShapeDtypeStruct((t, 512), F32), jax.ShapeDtypeStruct((t, CV_W), F32),
                   jax.ShapeDtypeStruct((t, 512), BF16), jax.ShapeDtypeStruct((t, D), BF16),
                   jax.ShapeDtypeStruct((t, D), BF16), jax.ShapeDtypeStruct((1, D), F32),
                   jax.ShapeDtypeStruct((1, D), F32), jax.ShapeDtypeStruct((1, SG_W), F32),
                   jax.ShapeDtypeStruct((1, SG_W), F32), jax.ShapeDtypeStruct((ng, CHUNK, CHUNK), F32),
                   jax.ShapeDtypeStruct((CHUNK, SG_W), F32)],
        scratch_shapes=[pltpu.VMEM((ng, CHUNK, CHUNK), BF16), pltpu.VMEM((ng, CHUNK, CHUNK), BF16),
                        pltpu.VMEM((tm, SG_W), F32), pltpu.VMEM((tm, SG_W), F32)],
        compiler_params=_cp(("arbitrary",)),
    )(dx1, z, z, z, ya, gm, lng, lnb, wsp, wspt, bias, cw, gout, wout, gpost)


def conv_bwd(dyc, z, cw):
    t = dyc.shape[0]
    tm = min(TM, t)
    hb = tm // 8
    last_blk = t // 8 - 1

    def body(dyc_ref, dyct_ref, zcv_ref, head_ref, tail_ref, cw_ref, dz_ref, dcw_ref):
        i = pl.program_id(0)
        first = i == 0
        last = i == pl.num_programs(0) - 1
        cw = cw_ref[...]
        zcv = zcv_ref[...]
        gb, gc, hh = zcv[:, :CV_W], zcv[:, CV_W:2 * CV_W], zcv[:, 2 * CV_W:]
        _, conv, y, y1, y2 = _conv_fwd(zcv, head_ref[...], first, cw)
        dyc = dyc_ref[...]
        dconv = dyc * gb
        tail = jnp.where(last, 0.0, dyct_ref[...] * tail_ref[:, :CV_W])
        d1 = _shift_up(dconv, 1, tail)
        d2 = _shift_up(dconv, 2, tail)
        dy = dconv * cw[2:3, :] + d1 * cw[1:2, :] + d2 * cw[0:1, :]
        dz_ref[...] = jnp.concatenate([dyc * conv, dy * hh, dy * gc], axis=1).astype(BF16)
        tap = lax.broadcasted_iota(jnp.int32, (8, CV_W), 0)
        dcw = jnp.where(tap == 0, jnp.sum(dconv * y2, axis=0, keepdims=True),
                        jnp.where(tap == 1, jnp.sum(dconv * y1, axis=0, keepdims=True),
                                  jnp.where(tap == 2, jnp.sum(dconv * y, axis=0, keepdims=True), 0.0)))
        _acc(dcw_ref, first, dcw)

    return pl.pallas_call(
        body, name="conv_bwd", grid=(t // tm,),
        in_specs=[pl.BlockSpec((tm, CV_W), lambda i: (i, 0)),
                  pl.BlockSpec((8, CV_W), lambda i: (jnp.minimum((i + 1) * hb, last_blk), 0)),
                  pl.BlockSpec((tm, 768), lambda i: (i, 0)),
                  pl.BlockSpec((8, 768), lambda i: (jnp.maximum(i * hb - 1, 0), 0)),
                  pl.BlockSpec((8, 768), lambda i: (jnp.minimum((i + 1) * hb, last_blk), 0)),
                  _whole()],
        out_specs=[pl.BlockSpec((tm, 768), lambda i: (i, 0)), pl.BlockSpec((8, CV_W), lambda i: (0, 0))],
        out_shape=[jax.ShapeDtypeStruct((t, 768), BF16), jax.ShapeDtypeStruct((8, CV_W), F32)],
        compiler_params=_cp(("arbitrary",)),
    )(dyc, dyc, z, z, z, cw)


def attn_bwd(q, k, v, o, lse, do, carried=()):
    t = q.shape[0]
    tq = min(TQ, t)
    nq = t // tq
    last_pair = HEADS // 2 - 1
    n = len(carried)

    def body(*refs):
        j = pl.program_id(1)
        q_ref, k_ref, v_ref, o_ref, lse_ref, do_ref = refs[:6]
        dq_ref, dk_ref, dv_ref = refs[6 + n:9 + n]
        sems = refs[9 + 2 * n:]
        stages = [_exchange_steps(refs[6 + a], refs[9 + n + a], carried[a][1], *sems[3 * a:3 * a + 3]) for a in range(n)]
        if n:
            pl.when((pl.program_id(0) == 0) & (j == 0))(_each(stages, 0))

        @pl.when(j == 0)
        def _():
            dq_ref[...] = jnp.zeros_like(dq_ref)

        row = lax.broadcasted_iota(jnp.int32, (tq, tq), 0)
        col = lax.broadcasted_iota(jnp.int32, (tq, tq), 1)
        vlane = lax.broadcasted_iota(jnp.int32, (tq, 2 * V_DIM), 1)
        head_lanes = [slice(h * HEAD_PAD, (h + 1) * HEAD_PAD) for h in range(2)]

        def step(i, carry, masked):
            start = pl.multiple_of(i * tq, tq)
            do_blk = do_ref[pl.ds(start, tq), :]
            o_blk = o_ref[pl.ds(start, tq), :]
            vb = v_ref[...]
            dks, dv_acc = [], carry[2]
            for h in range(2):
                lanes = head_lanes[h]
                qb = q_ref[pl.ds(start, tq), lanes]
                kb = k_ref[:, lanes]
                dob = jnp.where((vlane // V_DIM) == h, do_blk, 0.0)
                delta = jnp.sum(dob * o_blk, axis=-1, keepdims=True)
                lse2 = lse_ref[pl.ds(start, tq), lanes][:, 0:1] * LOG2E
                s = _mm_nt(qb, kb)
                if masked:
                    s = jnp.where(col <= row, s, NEG)
                p = jnp.exp2(s * SCALE_LOG2E - lse2)
                dob16 = dob.astype(BF16)
                dp = _mm_nt(dob16, vb)
                ds = (p * (dp - delta) * SCALE).astype(BF16)
                dv_acc = dv_acc + _mm_tn(p.astype(BF16), dob16)
                dks.append(carry[h] + _mm_tn(ds, qb))
                dq_ref[pl.ds(start, tq), lanes] += _mm(ds, kb)
            return dks[0], dks[1], dv_acc

        zero = jnp.zeros((tq, HEAD_PAD), F32)
        carry = step(j, (zero, zero, jnp.zeros((tq, 2 * V_DIM), F32)), True)
        rest = nq - 1 - j
        carry = lax.fori_loop(0, rest // 2, lambda u, c: step(j + 2 + 2 * u, step(j + 1 + 2 * u, c, False), False), carry)
        dk0, dk1, dv_acc = lax.fori_loop(0, rest % 2, lambda _, c: step(nq - 1, c, False), carry)
        dk_ref[:, head_lanes[0]] = dk0
        dk_ref[:, head_lanes[1]] = dk1
        dv_ref[...] = dv_acc
        if n:
            pl.when((pl.program_id(0) == last_pair) & (j == nq - 1))(_each(stages, 1))

    hbm = pl.BlockSpec(memory_space=pl.ANY)
    return pl.pallas_call(
        body, name=f"attn_bwd_exchange{n}" if n else "attn_bwd", grid=(HEADS // 2, nq),
        in_specs=[pl.BlockSpec((t, 2 * HEAD_PAD), lambda p, j: (0, p)),
                  pl.BlockSpec((tq, 2 * HEAD_PAD), lambda p, j: (j, p)),
                  pl.BlockSpec((tq, 2 * V_DIM), lambda p, j: (j, p)),
                  pl.BlockSpec((t, 2 * V_DIM), lambda p, j: (0, p)),
                  pl.BlockSpec((t, 2 * HEAD_PAD), lambda p, j: (0, p)),
                  pl.BlockSpec((t, 2 * V_DIM), lambda p, j: (0, p))] + [hbm] * n,
        out_specs=[pl.BlockSpec((t, 2 * HEAD_PAD), lambda p, j: (0, p)),
                   pl.BlockSpec((tq, 2 * HEAD_PAD), lambda p, j: (j, p)),
                   pl.BlockSpec((tq, 2 * V_DIM), lambda p, j: (j, p))] + [hbm] * n,
        out_shape=[jax.ShapeDtypeStruct((t, HEADS * HEAD_PAD), F32), jax.ShapeDtypeStruct((t, HEADS * HEAD_PAD), F32),
                   jax.ShapeDtypeStruct((t, HEADS * V_DIM), F32)]
        + [jax.ShapeDtypeStruct(src.shape if scatter else (N_DEV,) + src.shape, src.dtype) for src, scatter in carried],
        scratch_shapes=_comm_sems() * n,
        compiler_params=_cp(("arbitrary", "arbitrary") if n else ("parallel", "arbitrary")),
    )(q, k, v, o, lse, do, *[src for src, _ in carried])


def mla_proj_bwd(dq, dk, dv, z, ca, sb, sc, gq, gkv, wuq, wukv):
    t = z.shape[0]
    tm = min(TM, t)

    def body(dq_ref, dk_ref, dv_ref, z_ref, ca_ref, sb_ref, sc_ref, gq_ref, gkv_ref, wuq_ref, wukv_ref,
             dz_ref, cq_ref, ckv_ref, dqp_ref, dkvp_ref, dgq_ref, dgkv_ref):
        first = pl.program_id(0) == 0
        z = z_ref[...]
        ca, sb, sc = ca_ref[...], sb_ref[...], sc_ref[...]
        gq, gkv = gq_ref[...], gkv_ref[...]
        cq, cqh, rq = _rms_fwd(z[:, :Q_RANK], gq)
        ckv, ckvh, rkv = _rms_fwd(z[:, Q_RANK:Q_RANK + KV_RANK], gkv)
        lane = lax.broadcasted_iota(jnp.int32, (tm, HEAD_PAD), 1)
        dkr = jnp.zeros((tm, HEAD_PAD), F32)
        for h in range(HEADS):
            lanes = slice(h * HEAD_PAD, (h + 1) * HEAD_PAD)
            dqp_ref[:, lanes] = _rope_t(dq_ref[:, lanes], ca, sb, sc).astype(BF16)
            dkh = dk_ref[:, lanes]
            dkr = dkr + dkh
            dkvp_ref[:, lanes] = jnp.where(lane < NOPE, dkh, 0.0).astype(BF16)
        dkvp_ref[:, HEADS * HEAD_PAD:] = dv_ref[...].astype(BF16)
        dkr = pltpu.roll(_rope_t(jnp.where(lane >= NOPE, dkr, 0.0), ca, sb, sc), HEAD_PAD - NOPE, 1)
        dkr = jnp.where(lane < ROPE, dkr, 0.0)
        dcq = _mm(dqp_ref[...], wuq_ref[...])
        dckv = _mm(dkvp_ref[...], wukv_ref[...])
        dzq, dgq = _rms_bwd(cqh, rq, gq, dcq)
        dzkv, dgkv = _rms_bwd(ckvh, rkv, gkv, dckv)
        dz_ref[...] = jnp.concatenate([dzq, dzkv, dkr], axis=1).astype(BF16)
        cq_ref[...] = cq.astype(BF16)
        ckv_ref[...] = ckv.astype(BF16)
        _acc(dgq_ref, first, dgq)
        _acc(dgkv_ref, first, dgkv)

    row = lambda w: pl.BlockSpec((tm, w), lambda i: (i, 0))
    vec = lambda w: pl.BlockSpec((1, w), lambda i: (0, 0))
    return pl.pallas_call(
        body, name="mla_proj_bwd", grid=(t // tm,),
        in_specs=[row(1024), row(1024), row(512), pl.BlockSpec((tm, 768), lambda i: (i, 1)),
                  row(HEAD_PAD), row(HEAD_PAD), row(HEAD_PAD), _whole(), _whole(), _whole(), _whole()],
        out_specs=[row(768), row(Q_RANK), row(KV_RANK), row(1024), row(1536), vec(Q_RANK), vec(KV_RANK)],
        out_shape=[jax.ShapeDtypeStruct((t, 768), BF16), jax.ShapeDtypeStruct((t, Q_RANK), BF16),
                   jax.ShapeDtypeStruct((t, KV_RANK), BF16), jax.ShapeDtypeStruct((t, 1024), BF16),
                   jax.ShapeDtypeStruct((t, 1536), BF16), jax.ShapeDtypeStruct((1, Q_RANK), F32),
                   jax.ShapeDtypeStruct((1, KV_RANK), F32)],
        compiler_params=_cp(("arbitrary",)),
    )(dq, dk, dv, z, ca, sb, sc, gq, gkv, wuq, wukv)


def pre_in_bwd(x, dx1, dzcv, dzmla, dzsg, g, w):
    t = x.shape[0]
    tm = min(TM, t)

    def body(x_ref, dx1_ref, dzcv_ref, dzmla_ref, dzsg_ref, g_ref, w_ref, dx_ref, h_ref, dz_ref, dg_ref):
        g = g_ref[...]
        h, xh, r = _rms_fwd(x_ref[...], g)
        dz = jnp.concatenate([dzcv_ref[...], dzmla_ref[...], dzsg_ref[...]], axis=1)
        dx, dg = _rms_bwd(xh, r, g, _mm(dz, w_ref[...]))
        dx_ref[...] = dx1_ref[...] + dx
        h_ref[...] = h.astype(BF16)
        dz_ref[...] = dz
        _acc(dg_ref, pl.program_id(0) == 0, dg)

    row = lambda w_: pl.BlockSpec((tm, w_), lambda i: (i, 0))
    return pl.pallas_call(
        body, name="pre_in_bwd", grid=(t // tm,),
        in_specs=[row(D), row(D), row(768), row(768), row(512), _whole(), _whole()],
        out_specs=[row(D), row(D), row(Z_W), pl.BlockSpec((1, D), lambda i: (0, 0))],
        out_shape=[jax.ShapeDtypeStruct((t, D), F32), jax.ShapeDtypeStruct((t, D), BF16),
                   jax.ShapeDtypeStruct((t, Z_W), BF16), jax.ShapeDtypeStruct((1, D), F32)],
        compiler_params=_cp(("arbitrary",)),
    )(x, dx1, dzcv, dzmla, dzsg, g, w)


MESH = pl.DeviceIdType.MESH


def _place():
    return lax.axis_index("x"), lax.axis_index("y"), lax.axis_index("c")


def _comm_sems():
    return [pltpu.SemaphoreType.DMA((7,)), pltpu.SemaphoreType.DMA((7,)), pltpu.SemaphoreType.DMA]


def _gather_steps(x_ref, out_ref, send_sems, recv_sems, local_sem):
    x, y, c = _place()
    me, sibling = (x, y, c), (x, y, 1 - c)
    chips = [(1 - x, y), (x, 1 - y), (1 - x, 1 - y)]

    def slot(px, py, pc):
        return out_ref.at[4 * px + 2 * py + pc]

    def copy(k, blk, to, src=None):
        return pltpu.make_async_remote_copy(
            src_ref=slot(*blk) if src is None else src, dst_ref=slot(*blk),
            send_sem=send_sems.at[k], recv_sem=recv_sems.at[k], device_id=to, device_id_type=MESH)

    mine = pltpu.make_async_copy(x_ref, slot(*me), local_sem)
    first = [copy(0, me, sibling, src=x_ref)] + [copy(1 + j, me, (*chip, c), src=x_ref) for j, chip in enumerate(chips)]
    passed = [copy(4 + j, (*chip, c), sibling) for j, chip in enumerate(chips)]

    def start():
        mine.start()
        for cp in first:
            cp.start()

    def forward():
        for j, chip in enumerate(chips):
            copy(1 + j, (*chip, c), me).wait_recv()
            passed[j].start()

    def finish():
        copy(0, sibling, me).wait_recv()
        for j, chip in enumerate(chips):
            copy(4 + j, (*chip, 1 - c), me).wait_recv()
        for cp in first + passed:
            cp.wait_send()
        mine.wait()

    return start, forward, finish


def _exchange_steps(src_ref, out_ref, scatter, send_sems, recv_sems, local_sem):
    x, y, c = _place()
    me = 4 * x + 2 * y + c
    own = pltpu.make_async_copy(src_ref.at[me] if scatter else src_ref, out_ref.at[me], local_sem)
    copies = []
    for k in range(1, N_DEV):
        px = 1 - x if k & 4 else x
        py = 1 - y if k & 2 else y
        pc = 1 - c if k & 1 else c
        copies.append(pltpu.make_async_remote_copy(
            src_ref=src_ref.at[4 * px + 2 * py + pc] if scatter else src_ref, dst_ref=out_ref.at[me],
            send_sem=send_sems.at[k - 1], recv_sem=recv_sems.at[k - 1], device_id=(px, py, pc), device_id_type=MESH))

    def start():
        own.start()
        for cp in copies:
            cp.start()

    def finish():
        for cp in copies:
            cp.wait_recv()
        for cp in copies:
            cp.wait_send()
        own.wait()

    return start, finish


def all_gather(block):
    def body(x_ref, out_ref, *sems):
        for stage in _gather_steps(x_ref, out_ref, *sems):
            stage()

    return pl.pallas_call(
        body, name="all_gather",
        in_specs=[pl.BlockSpec(memory_space=pl.ANY)],
        out_specs=pl.BlockSpec(memory_space=pl.ANY),
        out_shape=jax.ShapeDtypeStruct((N_DEV,) + block.shape, block.dtype),
        scratch_shapes=_comm_sems(),
    )(block)


def _row_tile(r, cap):
    return max(d for d in range(16, cap + 1, 16) if r % d == 0)


def sum_adamw(parts, w, m, v, cap, carried=()):
    nl, r, c = w.shape
    tr = _row_tile(r, cap)
    steps = r // tr
    n = len(carried)
    c1 = 1.0 / (1.0 - ADAM_B1 ** ADAM_STEP)
    c2 = 1.0 / (1.0 - ADAM_B2 ** ADAM_STEP)

    def body(*refs):
        p_refs = refs[:nl]
        w_ref, m_ref, v_ref = refs[nl:nl + 3]
        g_ref, d_ref, nm_ref, nv_ref = refs[nl + 3 + n:nl + 7 + n]
        sems = refs[nl + 7 + 2 * n:]
        stages = [_exchange_steps(refs[nl + 3 + a], refs[nl + 7 + n + a], carried[a][1], *sems[3 * a:3 * a + 3])
                  for a in range(n)]
        layer, i = pl.program_id(0), pl.program_id(1)
        if n:
            pl.when((layer == 0) & (i == 0))(_each(stages, 0))

        def update(p_ref):
            g = p_ref[0].astype(F32)
            for k in range(1, N_DEV):
                g = g + p_ref[k].astype(F32)
            m_new = ADAM_B1 * m_ref[...] + (1.0 - ADAM_B1) * g
            v_new = ADAM_B2 * v_ref[...] + (1.0 - ADAM_B2) * (g * g)
            g_ref[...] = g
            nm_ref[...] = m_new
            nv_ref[...] = v_new
            d_ref[...] = -ADAM_LR * ((m_new * c1) / (jnp.sqrt(v_new * c2) + ADAM_EPS) + ADAM_WD * w_ref[...])

        for k in range(nl):
            pl.when(layer == k)(functools.partial(update, p_refs[k]))
        if n:
            pl.when((layer == nl - 1) & (i == steps - 1))(_each(stages, 1))

    def parts_spec(k):
        return pl.BlockSpec((N_DEV, tr, c), lambda l, i: (0, jnp.where(l == k, i, jnp.where(l < k, 0, steps - 1)), 0))

    blk = pl.BlockSpec((None, tr, c), lambda l, i: (l, i, 0))
    out = jax.ShapeDtypeStruct((nl, r, c), F32)
    hbm = pl.BlockSpec(memory_space=pl.ANY)
    return pl.pallas_call(
        body, name=f"sum_adamw_exchange{n}" if n else "sum_adamw", grid=(nl, steps),
        in_specs=[parts_spec(k) for k in range(nl)] + [blk, blk, blk] + [hbm] * n,
        out_specs=[blk, blk, blk, blk] + [hbm] * n,
        out_shape=[out, out, out, out]
        + [jax.ShapeDtypeStruct(src.shape if scatter else (N_DEV,) + src.shape, src.dtype) for src, scatter in carried],
        scratch_shapes=_comm_sems() * n,
        compiler_params=_cp(("arbitrary", "arbitrary")),
    )(*parts, w, m, v, *[src for src, _ in carried])


PACK_W = 1024
MIX_PIECES = (("w_out", D // N_DEV, D, False), ("w_uq", HEADS * (NOPE + ROPE) // N_DEV, Q_RANK, True),
              ("w_ukv", HEADS * (NOPE + V_DIM) // N_DEV, KV_RANK, True), ("conv", 16, PACK_W, False),
              ("w_in", IN_W // N_DEV, D, True))
FFN_PIECES = (("w_gate", D_FF // N_DEV, D, True), ("w_up", D_FF // N_DEV, D, True), ("w_down", D_FF // N_DEV, D, False))
OFFSET = {}
for _pieces in (MIX_PIECES, FFN_PIECES):
    _off = 0
    for _name, _rows, _, _ in _pieces:
        OFFSET[_name] = _off
        _off += _rows + -_rows % 16
assert all(o % 16 == 0 for o in OFFSET.values())
assert [OFFSET[n] for n in ("w_gate", "w_up", "w_down")] == [0, FFN_SHARD, 2 * FFN_SHARD]
CONV_BITS = 3 * (CV_W // N_DEV) * 2


def _to_pack(shards, dtype, pieces, conv=None):
    nl = shards["w_in"].shape[0]
    parts = []
    for name, rows, cols, transposed in pieces:
        if name == "conv":
            if conv is None:
                a = jnp.zeros((nl, rows, PACK_W), dtype)
            else:
                bits = lax.bitcast_convert_type(conv.astype(F32), BF16).reshape(nl, CONV_BITS)
                a = jnp.pad(bits, ((0, 0), (0, rows * PACK_W - CONV_BITS))).reshape(nl, rows, PACK_W)
        else:
            a = shards[name].astype(dtype)
            a = jnp.swapaxes(a, 1, 2) if transposed else a
            a = jnp.pad(a, ((0, 0), (0, -rows % 16), (0, PACK_W - cols)))
        parts.append(a)
    return jnp.concatenate(parts, axis=1)


def _from_pack(pack, pieces):
    out = {}
    for name, rows, cols, transposed in pieces:
        if name != "conv":
            a = pack[:, OFFSET[name]:OFFSET[name] + rows, :cols]
            out[name] = jnp.swapaxes(a, 1, 2) if transposed else a
    return out


def _mix_weights(g):
    def rows(name):
        _, n, cols, _ = next(p for p in MIX_PIECES if p[0] == name)
        return g[:, OFFSET[name]:OFFSET[name] + n, :cols]

    w_in_t = rows("w_in").reshape(IN_W, D)
    w_in_p = jnp.concatenate([w_in_t[1184:], w_in_t[:672], jnp.zeros((96, D), BF16), w_in_t[672:1184]], axis=0)
    w_uq_p = jnp.pad(rows("w_uq"), ((0, 0), (0, HEAD_PAD - NOPE - ROPE), (0, 0))).reshape(HEADS * HEAD_PAD, Q_RANK)
    kv = rows("w_ukv")
    w_k = jnp.pad(kv[:, :NOPE], ((0, 0), (0, HEAD_PAD - NOPE), (0, 0))).reshape(HEADS * HEAD_PAD, KV_RANK)
    w_ukv_p = jnp.concatenate([w_k, kv[:, NOPE:].reshape(HEADS * V_DIM, KV_RANK)], axis=0)
    bits = rows("conv").reshape(N_DEV, -1)[:, :CONV_BITS].reshape(N_DEV, 3, CV_W // N_DEV, 2)
    conv_w = jnp.moveaxis(lax.bitcast_convert_type(bits, F32), 0, 1).reshape(3, CV_W)
    return dict(w_in=w_in_p, w_uq=w_uq_p, w_ukv=w_ukv_p, w_out=rows("w_out").reshape(D, D), conv_w=conv_w)


def _grad_chunks(full):
    if "w_in" in full:
        pieces = MIX_PIECES
        d_in = full["w_in"]
        d_in = jnp.concatenate([d_in[768:768 + 672], d_in[1536:], d_in[:768]], axis=0)
        d_uq = full["w_uq"].reshape(HEADS, HEAD_PAD, Q_RANK)[:, :NOPE + ROPE]
        d_k = full["w_ukv"][:HEADS * HEAD_PAD].reshape(HEADS, HEAD_PAD, KV_RANK)[:, :NOPE]
        d_v = full["w_ukv"][HEADS * HEAD_PAD:].reshape(HEADS, V_DIM, KV_RANK)
        mats = dict(w_in=d_in, w_uq=d_uq, w_ukv=jnp.concatenate([d_k, d_v], axis=1), w_out=full["w_out"])
    else:
        pieces = FFN_PIECES
        mats = dict(w_gate=full["w_gu"][:D_FF], w_up=full["w_gu"][D_FF:], w_down=full["w_down"])
    parts = []
    for name, rows, cols, _ in pieces:
        if name == "conv":
            parts.append(jnp.zeros((N_DEV, rows, PACK_W), BF16))
        else:
            parts.append(jnp.pad(mats[name].reshape(N_DEV, rows, cols), ((0, 0), (0, -rows % 16), (0, PACK_W - cols))))
    return jnp.concatenate(parts, axis=1)


SMALL = (("mix_pre_g", (D,)), ("mix_post_g", (D,)), ("ffn_pre_g", (D,)), ("ffn_post_g", (D,)), ("q_norm_g", (Q_RANK,)),
         ("kv_norm_g", (KV_RANK,)), ("sg_ln_g", (SG_W,)), ("sg_ln_b", (SG_W,)), ("w_sp", (4, CHUNK, CHUNK)),
         ("b_sp", (4, CHUNK)), ("out_norm_g", (D,)))
SMALL_ROWS = 576


def _pack_small(vals, nl):
    flat = jnp.concatenate([vals[name].reshape(nl, -1) for name, _ in SMALL] + [vals["conv_w"].reshape(nl, -1)], axis=1)
    return jnp.pad(flat, ((0, 0), (0, SMALL_ROWS * 128 - flat.shape[1]))).reshape(nl * SMALL_ROWS, 128)


def _unpack_small(pack, nl):
    flat = pack.reshape(nl, SMALL_ROWS * 128)
    out, off = {}, 0
    for name, shape in SMALL + (("conv_w", (3, CV_W)),):
        n = int(np.prod(shape))
        out[name] = flat[:, off:off + n].reshape((nl,) + shape)
        off += n
    return out


def _layer_fwd(x, lw, sp, tabs, consts, next_pack):
    ca, sb, sc = tabs
    z = pre_in_fwd(x, sp["mix_pre_g"], lw["w_in"])
    q, k, v = mla_proj_fwd(z, ca, sb, sc, sp["q_norm_g"], sp["kv_norm_g"], lw["w_uq"], lw["w_ukv"])
    ya, lse, ffn_gathered, *mix_gathered = attn_fwd(q, k, v, next_pack)
    lw["ffn"] = ffn_gathered
    x1 = mix_fwd(x, z, ya, consts["gm"], sp["sg_ln_g"], sp["sg_ln_b"], sp["w_sp"], sp["bias"], lw["conv_w"],
                 sp["out_norm_g"], lw["w_out"], sp["mix_post_g"])
    x2, f = ffn_fwd(x1, sp["ffn_pre_g"], lw["ffn"], sp["ffn_post_g"])
    return x2, (x, z, q, k, v, ya, lse, x1, f), mix_gathered


def _layer_bwd(dx2, saved, lw, sp, tabs, consts, pending):
    ca, sb, sc = tabs
    x, z, q, k, v, ya, lse, x1, f = saved
    dx1, h2, dab, s, df, d_ffn_pre, d_ffn_post, *received = ffn_bwd(x1, f, dx2, sp["ffn_pre_g"], lw["ffn"], sp["ffn_post_g"],
                                                                    pending)
    ffn_chunks = _grad_chunks(dict(w_gu=atb(dab, h2, 1408), w_down=atb(s, df, 1408)))
    dya, dyc, dzsg, mix, do, d_mix_post, d_out_norm, d_lng, d_lnb, d_wsp, d_bias = mix_bwd(
        dx1, z, ya, consts["gm"], sp["sg_ln_g"], sp["sg_ln_b"], sp["w_sp"], sp["w_sp_t"], sp["bias"], lw["conv_w"],
        sp["out_norm_g"], lw["w_out"], sp["mix_post_g"])
    d_w_out = atb(mix, do, 1024)
    dzcv, d_cw = conv_bwd(dyc, z, lw["conv_w"])
    dq, dk, dv, got_ffn = attn_bwd(q, k, v, ya, lse, dya, ((ffn_chunks, True),))
    dzmla, cq, ckv, dqp, dkvp, d_gq, d_gkv = mla_proj_bwd(dq, dk, dv, z, ca, sb, sc, sp["q_norm_g"], sp["kv_norm_g"],
                                                          lw["w_uq"], lw["w_ukv"])
    d_w_uq = atb(dqp, cq, 1024)
    d_w_ukv = atb(dkvp, ckv, 1536)
    dx, h1, dz, d_mix_pre = pre_in_bwd(x, dx1, dzcv, dzmla, dzsg, sp["mix_pre_g"], lw["w_in"])
    d_w_in = atb(dz, h1, 2048)
    mix_chunks = _grad_chunks(dict(w_in=d_w_in, w_uq=d_w_uq, w_ukv=d_w_ukv, w_out=d_w_out))
    d_bsp = d_bias[:, ::GROUP].T
    small = dict(mix_pre_g=d_mix_pre[0], mix_post_g=d_mix_post[0], ffn_pre_g=d_ffn_pre[0], ffn_post_g=d_ffn_post[0],
                 q_norm_g=d_gq[0], kv_norm_g=d_gkv[0], sg_ln_g=d_lng[0], sg_ln_b=d_lnb[0], w_sp=d_wsp, b_sp=d_bsp,
                 out_norm_g=d_out_norm[0], conv_w=d_cw[:3])
    small_pack = _pack_small({name: a[None] for name, a in small.items()}, 1)
    return dx, ((mix_chunks, True), (small_pack, False)), [got_ffn] + received


def kernel(x, positions, mix_pre_g, mix_post_g, ffn_pre_g, ffn_post_g, w_in, q_norm_g, w_uq, kv_norm_g, w_ukv, sg_ln_g, sg_ln_b, w_sp, b_sp, conv_w, out_norm_g, w_out, w_gate, w_up, w_down, loss_target, m_mix_pre_g, m_mix_post_g, m_ffn_pre_g, m_ffn_post_g, m_w_in, m_q_norm_g, m_w_uq, m_kv_norm_g, m_w_ukv, m_sg_ln_g, m_sg_ln_b, m_w_sp, m_b_sp, m_conv_w, m_out_norm_g, m_w_out, m_w_gate, m_w_up, m_w_down, v_mix_pre_g, v_mix_post_g, v_ffn_pre_g, v_ffn_post_g, v_w_in, v_q_norm_g, v_w_uq, v_kv_norm_g, v_w_ukv, v_sg_ln_g, v_sg_ln_b, v_w_sp, v_b_sp, v_conv_w, v_out_norm_g, v_w_out, v_w_gate, v_w_up, v_w_down):
    nl = w_in.shape[0]
    t = x.shape[1]
    w = dict(mix_pre_g=mix_pre_g, mix_post_g=mix_post_g, ffn_pre_g=ffn_pre_g, ffn_post_g=ffn_post_g, w_in=w_in,
             q_norm_g=q_norm_g, w_uq=w_uq, kv_norm_g=kv_norm_g, w_ukv=w_ukv, sg_ln_g=sg_ln_g, sg_ln_b=sg_ln_b, w_sp=w_sp,
             b_sp=b_sp, conv_w=conv_w, out_norm_g=out_norm_g, w_out=w_out, w_gate=w_gate, w_up=w_up, w_down=w_down)
    m = dict(mix_pre_g=m_mix_pre_g, mix_post_g=m_mix_post_g, ffn_pre_g=m_ffn_pre_g, ffn_post_g=m_ffn_post_g, w_in=m_w_in,
             q_norm_g=m_q_norm_g, w_uq=m_w_uq, kv_norm_g=m_kv_norm_g, w_ukv=m_w_ukv, sg_ln_g=m_sg_ln_g, sg_ln_b=m_sg_ln_b,
             w_sp=m_w_sp, b_sp=m_b_sp, conv_w=m_conv_w, out_norm_g=m_out_norm_g, w_out=m_w_out, w_gate=m_w_gate,
             w_up=m_w_up, w_down=m_w_down)
    v = dict(mix_pre_g=v_mix_pre_g, mix_post_g=v_mix_post_g, ffn_pre_g=v_ffn_pre_g, ffn_post_g=v_ffn_post_g, w_in=v_w_in,
             q_norm_g=v_q_norm_g, w_uq=v_w_uq, kv_norm_g=v_kv_norm_g, w_ukv=v_w_ukv, sg_ln_g=v_sg_ln_g, sg_ln_b=v_sg_ln_b,
             w_sp=v_w_sp, b_sp=v_b_sp, conv_w=v_conv_w, out_norm_g=v_out_norm_g, w_out=v_w_out, w_gate=v_w_gate,
             w_up=v_w_up, w_down=v_w_down)

    mix_pack = _to_pack(w, BF16, MIX_PIECES, conv=w["conv_w"])
    ffn_pack = _to_pack(w, BF16, FFN_PIECES)
    consts = dict(gm=jnp.asarray(np.kron(np.eye(SG_W // GROUP), np.full((GROUP, GROUP), 1.0 / GROUP)), BF16))
    smalls = []
    for l in range(nl):
        sp = {name: w[name][l].reshape(1, -1) for name, shape in SMALL if len(shape) == 1}
        sp["w_sp"] = w["w_sp"][l]
        sp["w_sp_t"] = jnp.swapaxes(w["w_sp"][l], 1, 2)
        sp["bias"] = jnp.repeat(w["b_sp"][l].T, GROUP, axis=1)
        smalls.append(sp)
    inv_freq = 1.0 / (ROPE_THETA ** (jnp.arange(0, ROPE // 2, dtype=F32) / (ROPE // 2)))
    inv = jnp.zeros((1, HEAD_PAD), F32).at[0, NOPE:NOPE + ROPE].set(jnp.concatenate([inv_freq, inv_freq]))
    tabs = rope_tables(positions.reshape(t, 1).astype(F32), inv)

    h = x[0]
    saved, layers = [], []
    mix_gathered = [all_gather(mix_pack[0])]
    for l in range(nl):
        layers.append(_mix_weights(mix_gathered[0]))
        carried = (ffn_pack[l],) + ((mix_pack[l + 1],) if l + 1 < nl else ())
        h, s, mix_gathered = _layer_fwd(h, layers[l], smalls[l], tabs, consts, carried)
        saved.append(s)
    sq, dh = loss_head(h, loss_target[0])
    loss = lax.psum(0.5 * sq[0, 0] / D, ("x", "y", "c"))

    got_ffn, got_mix, got_small = [None] * nl, [None] * nl, [None] * nl
    pending = ()
    for l in reversed(range(nl)):
        dh, new_pending, received = _layer_bwd(dh, saved[l], layers[l], smalls[l], tabs, consts, pending)
        got_ffn[l] = received[0]
        if pending:
            got_mix[l + 1], got_small[l + 1] = received[1:]
        pending = new_pending

    me = 4 * lax.axis_index("x") + 2 * lax.axis_index("y") + lax.axis_index("c")
    *ffn_new, got_mix[0], got_small[0] = sum_adamw(got_ffn, *[_to_pack(d, F32, FFN_PIECES) for d in (w, m, v)], 176,
                                                   carried=pending)
    mix_new = sum_adamw(got_mix, *[_to_pack(d, F32, MIX_PIECES) for d in (w, m, v)], 208)
    got_small = jnp.concatenate(got_small, axis=1)
    g_big, d_big, m_big, v_big = [{**_from_pack(a, FFN_PIECES), **_from_pack(b, MIX_PIECES)}
                                  for a, b in zip(ffn_new, mix_new)]

    def full_conv(a):
        return lax.dynamic_update_slice(jnp.zeros((nl, 3, CV_W), F32), a, (0, 0, me * (CV_W // N_DEV)))

    def small_pack(d):
        return _pack_small({**{name: d[name] for name, _ in SMALL}, "conv_w": full_conv(d["conv_w"])}, nl)

    g_small, d_small, m_small, v_small = [_unpack_small(p[0], nl) for p in
                                          sum_adamw([got_small], small_pack(w)[None], small_pack(m)[None],
                                                    small_pack(v)[None], 1152)]
    outs = []
    for big, small in ((g_big, g_small), (d_big, d_small), (m_big, m_small), (v_big, v_small)):
        for name in w:
            if name == "conv_w":
                outs.append(lax.dynamic_slice(small[name], (0, 0, me * (CV_W // N_DEV)), (nl, 3, CV_W // N_DEV)))
            elif name in small:
                outs.append(small[name])
            else:
                outs.append(big[name])
    return (loss, dh[None], *outs)
```

```python
import functools

import jax
import jax.numpy as jnp
import numpy as np
from jax import lax
from jax.experimental import pallas as pl
from jax.experimental.pallas import tpu as pltpu

F32 = jnp.float32
BF16 = jnp.bfloat16

D = 1024
Q_RANK = 384
KV_RANK = 256
ROPE = 32
HEADS = 8
NOPE = 64
V_DIM = 64
HEAD_PAD = 128
SG_W = 256
CV_W = 256
CHUNK = 128
GROUP = 64
D_FF = 2816
IN_W = 1952
Z_W = 2048
Z_CV, Z_MLA, Z_SG = 0, 768, 1536
EPS = 1e-6
ROPE_THETA = 10000.0
SCALE = (NOPE + ROPE) ** -0.5
LOG2E = 1.4426950408889634
SCALE_LOG2E = SCALE * LOG2E
NEG = -1e30
N_DEV = 8

ADAM_LR, ADAM_B1, ADAM_B2, ADAM_EPS, ADAM_WD, ADAM_STEP = 0.001, 0.9, 0.999, 1e-08, 0.01, 10

VMEM_LIMIT = 56 * 1024 * 1024

TM = 512
TM_FFN = 256
FFN_SLAB = 256
TQ = 512
TT = 2048


def _cp(sem, vmem=VMEM_LIMIT):
    return pltpu.CompilerParams(dimension_semantics=sem, vmem_limit_bytes=vmem)


def _whole():
    return pl.BlockSpec(memory_space=pltpu.VMEM)


def _mm(a, b):
    return jnp.dot(a, b, preferred_element_type=F32)


def _mm_nt(a, b):
    return lax.dot_general(a, b, (((1,), (1,)), ((), ())), preferred_element_type=F32)


def _mm_tn(a, b):
    return lax.dot_general(a, b, (((0,), (0,)), ((), ())), preferred_element_type=F32)


def _rms_fwd(x, g):
    r = lax.rsqrt(jnp.mean(x * x, axis=-1, keepdims=True) + EPS)
    xh = x * r
    return xh * g, xh, r


def _rms_bwd(xh, r, g, dy):
    dxh = dy * g
    dx = r * (dxh - xh * jnp.mean(dxh * xh, axis=-1, keepdims=True))
    dg = jnp.sum(dy * xh, axis=0, keepdims=True)
    return dx, dg


def _gmean(v, gm):
    hi = v.astype(BF16)
    lo = (v - hi.astype(F32)).astype(BF16)
    return _mm(hi, gm) + _mm(lo, gm)


def _gelu(x):
    c = np.float32(np.sqrt(2.0 / np.pi))
    u = c * (x + 0.044715 * x * x * x)
    t = jnp.tanh(u)
    return 0.5 * x * (1.0 + t), t


def _gelu_grad(x, t):
    c = np.float32(np.sqrt(2.0 / np.pi))
    return 0.5 * (1.0 + t) + 0.5 * x * (1.0 - t * t) * c * (1.0 + 3.0 * 0.044715 * x * x)


def _rope(t, ca, sb, sc):
    return t * ca + pltpu.roll(t, HEAD_PAD - 16, 1) * sb + pltpu.roll(t, 16, 1) * sc


def _rope_t(dt, ca, sb, sc):
    return dt * ca + pltpu.roll(dt * sb, 16, 1) + pltpu.roll(dt * sc, HEAD_PAD - 16, 1)


def _shift_down(y, k, head):
    n = y.shape[0]
    out = pltpu.roll(y, k, 0)
    row = lax.broadcasted_iota(jnp.int32, y.shape, 0)
    for j in range(k):
        out = jnp.where(row == j, head[8 - k + j:8 - k + j + 1, :], out)
    return out


def _shift_up(y, k, tail):
    n = y.shape[0]
    out = pltpu.roll(y, n - k, 0)
    row = lax.broadcasted_iota(jnp.int32, y.shape, 0)
    for j in range(k):
        out = jnp.where(row == n - k + j, tail[j:j + 1, :], out)
    return out


def rope_tables(pos, inv):
    t = pos.shape[0]
    tm = min(TM, t)

    def body(pos_ref, inv_ref, ca_ref, sb_ref, sc_ref):
        ang = pos_ref[...] * inv_ref[...]
        c = jnp.cos(ang)
        s = jnp.sin(ang)
        lane = lax.broadcasted_iota(jnp.int32, ang.shape, 1)
        ca_ref[...] = jnp.where(lane < NOPE, 1.0, jnp.where(lane < NOPE + ROPE, c, 0.0))
        sb_ref[...] = jnp.where((lane >= NOPE) & (lane < NOPE + 16), -s, 0.0)
        sc_ref[...] = jnp.where((lane >= NOPE + 16) & (lane < NOPE + ROPE), s, 0.0)

    out = jax.ShapeDtypeStruct((t, HEAD_PAD), F32)
    blk = pl.BlockSpec((tm, HEAD_PAD), lambda i: (i, 0))
    return pl.pallas_call(
        body, name="rope_tables", grid=(t // tm,),
        in_specs=[pl.BlockSpec((tm, 1), lambda i: (i, 0)), pl.BlockSpec((1, HEAD_PAD), lambda i: (0, 0))],
        out_specs=[blk, blk, blk], out_shape=[out, out, out],
        compiler_params=_cp(("parallel",)),
    )(pos, inv)


def pre_in_fwd(x, g, w):
    t = x.shape[0]
    tm = min(TM, t)

    def body(x_ref, g_ref, w_ref, z_ref):
        h, _, _ = _rms_fwd(x_ref[...], g_ref[...])
        z_ref[...] = _mm_nt(h.astype(BF16), w_ref[...])

    return pl.pallas_call(
        body, name="pre_in_fwd", grid=(t // tm,),
        in_specs=[pl.BlockSpec((tm, D), lambda i: (i, 0)), _whole(), _whole()],
        out_specs=pl.BlockSpec((tm, Z_W), lambda i: (i, 0)),
        out_shape=jax.ShapeDtypeStruct((t, Z_W), F32),
        compiler_params=_cp(("parallel",)),
    )(x, g, w)


def mla_proj_fwd(z, ca, sb, sc, gq, gkv, wuq, wukv):
    t = z.shape[0]
    tm = min(TM, t)

    def body(z_ref, ca_ref, sb_ref, sc_ref, gq_ref, gkv_ref, wuq_ref, wukv_ref, q_ref, k_ref, v_ref):
        z = z_ref[...]
        ca, sb, sc = ca_ref[...], sb_ref[...], sc_ref[...]
        cq, _, _ = _rms_fwd(z[:, :Q_RANK], gq_ref[...])
        ckv, _, _ = _rms_fwd(z[:, Q_RANK:Q_RANK + KV_RANK], gkv_ref[...])
        q = _mm_nt(cq.astype(BF16), wuq_ref[...])
        kv = _mm_nt(ckv.astype(BF16), wukv_ref[...])
        kr = _rope(pltpu.roll(z[:, Q_RANK + KV_RANK:], NOPE, 1), ca, sb, sc)
        for h in range(HEADS):
            lanes = slice(h * HEAD_PAD, (h + 1) * HEAD_PAD)
            q_ref[:, lanes] = _rope(q[:, lanes], ca, sb, sc).astype(BF16)
            k_ref[:, lanes] = (kv[:, lanes] + kr).astype(BF16)
        v_ref[...] = kv[:, HEADS * HEAD_PAD:].astype(BF16)

    tab = pl.BlockSpec((tm, HEAD_PAD), lambda i: (i, 0))
    return pl.pallas_call(
        body, name="mla_proj_fwd", grid=(t // tm,),
        in_specs=[pl.BlockSpec((tm, 768), lambda i: (i, 1)), tab, tab, tab, _whole(), _whole(), _whole(), _whole()],
        out_specs=[pl.BlockSpec((tm, HEADS * HEAD_PAD), lambda i: (i, 0)),
                   pl.BlockSpec((tm, HEADS * HEAD_PAD), lambda i: (i, 0)),
                   pl.BlockSpec((tm, HEADS * V_DIM), lambda i: (i, 0))],
        out_shape=[jax.ShapeDtypeStruct((t, HEADS * HEAD_PAD), BF16),
                   jax.ShapeDtypeStruct((t, HEADS * HEAD_PAD), BF16),
                   jax.ShapeDtypeStruct((t, HEADS * V_DIM), BF16)],
        compiler_params=_cp(("parallel",)),
    )(z, ca, sb, sc, gq, gkv, wuq, wukv)


def _each(stages, k):
    def run():
        for stage in stages:
            stage[k]()
    return run


def attn_fwd(q, k, v, carried=()):
    t = q.shape[0]
    tq = min(TQ, t)
    nq = t // tq
    last_pair = HEADS // 2 - 1
    n = len(carried)

    def body(*refs):
        q_ref, k_ref, v_ref = refs[:3]
        o_ref, lse_ref = refs[3 + n:5 + n]
        sems = refs[5 + 2 * n:]
        stages = [_gather_steps(refs[3 + a], refs[5 + n + a], *sems[3 * a:3 * a + 3]) for a in range(n)]
        if n:
            pair = pl.program_id(0)
            pl.when((pair == 0) & (pl.program_id(1) == 0))(_each(stages, 0))
            pl.when((pair == last_pair) & (pl.program_id(1) == 0))(_each(stages, 1))
        i = pl.program_id(1)
        row = lax.broadcasted_iota(jnp.int32, (tq, tq), 0)
        col = lax.broadcasted_iota(jnp.int32, (tq, tq), 1)
        head_lanes = [slice(h * HEAD_PAD, (h + 1) * HEAD_PAD) for h in range(2)]

        def step(j, carry, masked):
            start = pl.multiple_of(j * tq, tq)
            vb = v_ref[pl.ds(start, tq), :]
            out = []
            for h in range(2):
                m, l, acc = carry[h]
                s = _mm_nt(q_ref[:, head_lanes[h]], k_ref[pl.ds(start, tq), head_lanes[h]])
                if masked:
                    s = jnp.where(col <= row, s, NEG)
                m_new = jnp.maximum(m, jnp.max(s, axis=-1, keepdims=True))
                p = jnp.exp2((s - m_new) * SCALE_LOG2E)
                alpha = jnp.exp2((m - m_new) * SCALE_LOG2E)
                l = alpha * l + jnp.sum(p, axis=-1, keepdims=True)
                acc = alpha * acc + _mm(p.astype(BF16), vb)
                out.append((m_new, l, acc))
            return tuple(out)

        init = (jnp.full((tq, 1), NEG, F32), jnp.zeros((tq, 1), F32), jnp.zeros((tq, 2 * V_DIM), F32))
        carry = lax.fori_loop(0, i // 2, lambda j, c: step(2 * j + 1, step(2 * j, c, False), False), (init, init))
        carry = lax.fori_loop(0, i % 2, lambda _, c: step(i - 1, c, False), carry)
        outs = []
        for h, (m, l, acc) in enumerate(step(i, carry, True)):
            outs.append(acc / l)
            lse_ref[:, head_lanes[h]] = jnp.broadcast_to(m * SCALE + jnp.log(l), (tq, HEAD_PAD))
        lane = lax.broadcasted_iota(jnp.int32, (tq, 2 * V_DIM), 1)
        o_ref[...] = jnp.where(lane < V_DIM, outs[0], outs[1])
        if n:
            pl.when((pl.program_id(0) == last_pair) & (i == nq - 1))(_each(stages, 2))

    hbm = pl.BlockSpec(memory_space=pl.ANY)
    return pl.pallas_call(
        body, name=f"attn_fwd_gather{n}" if n else "attn_fwd", grid=(HEADS // 2, nq),
        in_specs=[pl.BlockSpec((tq, 2 * HEAD_PAD), lambda p, i: (i, p)),
                  pl.BlockSpec((t, 2 * HEAD_PAD), lambda p, i: (0, p)),
                  pl.BlockSpec((t, 2 * V_DIM), lambda p, i: (0, p))] + [hbm] * n,
        out_specs=[pl.BlockSpec((tq, 2 * V_DIM), lambda p, i: (i, p)),
                   pl.BlockSpec((tq, 2 * HEAD_PAD), lambda p, i: (i, p))] + [hbm] * n,
        out_shape=[jax.ShapeDtypeStruct((t, HEADS * V_DIM), F32), jax.ShapeDtypeStruct((t, HEADS * HEAD_PAD), F32)]
        + [jax.ShapeDtypeStruct((N_DEV,) + c.shape, c.dtype) for c in carried],
        scratch_shapes=_comm_sems() * n,
        compiler_params=_cp(("arbitrary", "arbitrary") if n else ("parallel", "parallel")),
    )(q, k, v, *carried)


def _sgu_fwd(zsg, gm, lng, lnb, wc_ref, bias, mixed_ref):
    uv, th = _gelu(zsg)
    u, v0 = uv[:, :SG_W], uv[:, SG_W:]
    vc = v0 - _gmean(v0, gm)
    r = lax.rsqrt(_gmean(vc * vc, gm) + EPS)
    vh = vc * r
    v = vh * lng + lnb
    lane = lax.broadcasted_iota(jnp.int32, (CHUNK, SG_W), 1)
    for c in range(zsg.shape[0] // CHUNK):
        rows = slice(c * CHUNK, (c + 1) * CHUNK)
        vb = v[rows].astype(BF16)
        mixed = bias
        for g in range(SG_W // GROUP):
            mixed = mixed + jnp.where(lane // GROUP == g, _mm(wc_ref[g], vb), 0.0)
        mixed_ref[rows, :] = mixed
    return u, v, vh, r, th


def _conv_fwd(zcv, halo, first, cw):
    gb, gc, hh = zcv[:, :CV_W], zcv[:, CV_W:2 * CV_W], zcv[:, 2 * CV_W:]
    y = gc * hh
    yh = jnp.where(first, 0.0, halo[:, CV_W:2 * CV_W] * halo[:, 2 * CV_W:])
    y1 = _shift_down(y, 1, yh)
    y2 = _shift_down(y, 2, yh)
    conv = y2 * cw[0:1, :] + y1 * cw[1:2, :] + y * cw[2:3, :]
    return gb * conv, conv, y, y1, y2


def _tril_bf16(w_ref, g):
    row = lax.broadcasted_iota(jnp.int32, (CHUNK, CHUNK), 0)
    col = lax.broadcasted_iota(jnp.int32, (CHUNK, CHUNK), 1)
    return jnp.where(col <= row, w_ref[g], 0.0).astype(BF16)


def mix_fwd(x, z, ya, gm, lng, lnb, wsp, bias, cw, gout, wout, gpost):
    t = x.shape[0]
    tm = min(TM, t)

    def body(x_ref, zcv_ref, halo_ref, zsg_ref, ya_ref, gm_ref, lng_ref, lnb_ref, wsp_ref, bias_ref, cw_ref,
             gout_ref, wout_ref, gpost_ref, x1_ref, wc_ref, mixed_ref):
        i = pl.program_id(0)
        for g in range(SG_W // GROUP):
            wc_ref[g] = _tril_bf16(wsp_ref, g)
        u, _, _, _, _ = _sgu_fwd(zsg_ref[...], gm_ref[...], lng_ref[...], lnb_ref[...], wc_ref, bias_ref[...], mixed_ref)
        yb = u * mixed_ref[...]
        yc, _, _, _, _ = _conv_fwd(zcv_ref[...], halo_ref[...], i == 0, cw_ref[...])
        gout = gout_ref[...]
        na, _, _ = _rms_fwd(ya_ref[...], gout[:, :512])
        nb, _, _ = _rms_fwd(yb, gout[:, 512:768])
        nc, _, _ = _rms_fwd(yc, gout[:, 768:])
        mix = jnp.concatenate([na, nb, nc], axis=1).astype(BF16)
        o, _, _ = _rms_fwd(_mm(mix, wout_ref[...]), gpost_ref[...])
        x1_ref[...] = x_ref[...] + o

    hb = tm // 8
    return pl.pallas_call(
        body, name="mix_fwd", grid=(t // tm,),
        in_specs=[pl.BlockSpec((tm, D), lambda i: (i, 0)),
                  pl.BlockSpec((tm, 768), lambda i: (i, 0)),
                  pl.BlockSpec((8, 768), lambda i: (jnp.maximum(i * hb - 1, 0), 0)),
                  pl.BlockSpec((tm, 512), lambda i: (i, 3)),
                  pl.BlockSpec((tm, 512), lambda i: (i, 0)),
                  _whole(), _whole(), _whole(), _whole(), _whole(), _whole(), _whole(), _whole(), _whole()],
        out_specs=pl.BlockSpec((tm, D), lambda i: (i, 0)),
        out_shape=jax.ShapeDtypeStruct((t, D), F32),
        scratch_shapes=[pltpu.VMEM((SG_W // GROUP, CHUNK, CHUNK), BF16), pltpu.VMEM((tm, SG_W), F32)],
        compiler_params=_cp(("arbitrary",)),
    )(x, z, z, z, ya, gm, lng, lnb, wsp, bias, cw, gout, wout, gpost)


def _sigmoid(a):
    return 1.0 / (1.0 + jnp.exp(-a))


FFN_SHARD = D_FF // N_DEV


def _load_ffn_weights(g_ref, wgu_ref, wd_ref, sems):
    copies = []
    for j in range(N_DEV):
        for p, (dst, base) in enumerate(((wgu_ref, 0), (wgu_ref, D_FF), (wd_ref, 0))):
            copies.append(pltpu.make_async_copy(g_ref.at[j, pl.ds(p * FFN_SHARD, FFN_SHARD)],
                                                dst.at[pl.ds(base + j * FFN_SHARD, FFN_SHARD)], sems.at[3 * j + p]))
    for cp in copies:
        cp.start()
    for cp in copies:
        cp.wait()


def _ffn_weight_scratch():
    return [pltpu.VMEM((2 * D_FF, D), BF16), pltpu.VMEM((D_FF, D), BF16), pltpu.SemaphoreType.DMA((3 * N_DEV,))]


def ffn_fwd(x1, gpre, gathered, gpost):
    t = x1.shape[0]
    tm = min(TM_FFN, t)

    def body(x_ref, gpre_ref, g_ref, gpost_ref, x2_ref, f_ref, wgu_ref, wd_ref, sems):
        @pl.when(pl.program_id(0) == 0)
        def _():
            _load_ffn_weights(g_ref, wgu_ref, wd_ref, sems)

        x = x_ref[...]
        h, _, _ = _rms_fwd(x, gpre_ref[...])
        ab = _mm_nt(h.astype(BF16), wgu_ref[...])
        a, b = ab[:, :D_FF], ab[:, D_FF:]
        s = a * _sigmoid(a) * b
        f = _mm(s.astype(BF16), wd_ref[...])
        f_ref[...] = f
        x2_ref[...] = x + _rms_fwd(f, gpost_ref[...])[0]

    row = pl.BlockSpec((tm, D), lambda i: (i, 0))
    return pl.pallas_call(
        body, name="ffn_fwd", grid=(t // tm,),
        in_specs=[row, _whole(), pl.BlockSpec(memory_space=pl.ANY), _whole()],
        out_specs=[row, row],
        out_shape=[jax.ShapeDtypeStruct((t, D), F32), jax.ShapeDtypeStruct((t, D), F32)],
        scratch_shapes=_ffn_weight_scratch(),
        compiler_params=_cp(("arbitrary",)),
    )(x1, gpre, gathered, gpost)


def loss_head(y, target):
    t = y.shape[0]
    tm = min(TM, t)

    def body(y_ref, t_ref, loss_ref, dy_ref):
        @pl.when(pl.program_id(0) == 0)
        def _():
            loss_ref[...] = jnp.zeros_like(loss_ref)

        e = y_ref[...] - t_ref[...]
        dy_ref[...] = e * (1.0 / D)
        loss_ref[...] += jnp.sum(jnp.sum(e * e, axis=-1, keepdims=True), axis=0, keepdims=True)

    return pl.pallas_call(
        body, name="loss_head", grid=(t // tm,),
        in_specs=[pl.BlockSpec((tm, D), lambda i: (i, 0)), pl.BlockSpec((tm, D), lambda i: (i, 0))],
        out_specs=[pl.BlockSpec((1, 128), lambda i: (0, 0)), pl.BlockSpec((tm, D), lambda i: (i, 0))],
        out_shape=[jax.ShapeDtypeStruct((1, 128), F32), jax.ShapeDtypeStruct((t, D), F32)],
        compiler_params=_cp(("arbitrary",)),
    )(y, target)


def _acc(ref, first, val):
    @pl.when(first)
    def _():
        ref[...] = val

    @pl.when(jnp.logical_not(first))
    def _():
        ref[...] += val


def ffn_bwd(x1, f, dx2, gpre, gathered, gpost, carried=()):
    t = x1.shape[0]
    tm = min(TM_FFN, t)
    steps = t // tm
    n = len(carried)

    def body(*refs):
        x_ref, f_ref, dx2_ref, gpre_ref, g_ref, gpost_ref = refs[:6]
        dx1_ref, h_ref, dab_ref, s_ref, df_ref, dgpre_ref, dgpost_ref = refs[6 + n:13 + n]
        ab_ref, ds_ref, wgu_ref, wd_ref, sems = refs[13 + 2 * n:18 + 2 * n]
        comm_sems = refs[18 + 2 * n:]
        stages = [_exchange_steps(refs[6 + a], refs[13 + n + a], carried[a][1], *comm_sems[3 * a:3 * a + 3])
                  for a in range(n)]
        first = pl.program_id(0) == 0
        if n:
            pl.when(first)(_each(stages, 0))

        @pl.when(first)
        def _():
            _load_ffn_weights(g_ref, wgu_ref, wd_ref, sems)

        dx2 = dx2_ref[...]
        gpre, gpost = gpre_ref[...], gpost_ref[...]
        h, xh, rx = _rms_fwd(x_ref[...], gpre)
        h_ref[...] = h.astype(BF16)
        ab_ref[...] = _mm_nt(h_ref[...], wgu_ref[...])
        for c in range(0, D_FF, FFN_SLAB):
            a, b = ab_ref[:, c:c + FFN_SLAB], ab_ref[:, D_FF + c:D_FF + c + FFN_SLAB]
            s_ref[:, c:c + FFN_SLAB] = (a * _sigmoid(a) * b).astype(BF16)
        _, fh, rf = _rms_fwd(f_ref[...], gpost)
        df, dgpost = _rms_bwd(fh, rf, gpost, dx2)
        df_ref[...] = df.astype(BF16)
        ds_ref[...] = _mm_nt(df_ref[...], wd_ref[...])
        for c in range(0, D_FF, FFN_SLAB):
            a, b = ab_ref[:, c:c + FFN_SLAB], ab_ref[:, D_FF + c:D_FF + c + FFN_SLAB]
            ds = ds_ref[:, c:c + FFN_SLAB]
            sg = _sigmoid(a)
            dab_ref[:, c:c + FFN_SLAB] = (ds * b * (sg * (1.0 + a * (1.0 - sg)))).astype(BF16)
            dab_ref[:, D_FF + c:D_FF + c + FFN_SLAB] = (ds * (a * sg)).astype(BF16)
        dx, dgpre = _rms_bwd(xh, rx, gpre, _mm(dab_ref[...], wgu_ref[...]))
        dx1_ref[...] = dx2 + dx
        _acc(dgpre_ref, first, dgpre)
        _acc(dgpost_ref, first, dgpost)
        if n:
            pl.when(pl.program_id(0) == steps - 1)(_each(stages, 1))

    row = lambda w: pl.BlockSpec((tm, w), lambda i: (i, 0))
    vec = pl.BlockSpec((1, D), lambda i: (0, 0))
    hbm = pl.BlockSpec(memory_space=pl.ANY)
    return pl.pallas_call(
        body, name=f"ffn_bwd_exchange{n}" if n else "ffn_bwd", grid=(steps,),
        in_specs=[row(D), row(D), row(D), _whole(), hbm, _whole()] + [hbm] * n,
        out_specs=[row(D), row(D), row(2 * D_FF), row(D_FF), row(D), vec, vec] + [hbm] * n,
        out_shape=[jax.ShapeDtypeStruct((t, D), F32), jax.ShapeDtypeStruct((t, D), BF16),
                   jax.ShapeDtypeStruct((t, 2 * D_FF), BF16), jax.ShapeDtypeStruct((t, D_FF), BF16),
                   jax.ShapeDtypeStruct((t, D), BF16), jax.ShapeDtypeStruct((1, D), F32),
                   jax.ShapeDtypeStruct((1, D), F32)]
        + [jax.ShapeDtypeStruct(src.shape if scatter else (N_DEV,) + src.shape, src.dtype) for src, scatter in carried],
        scratch_shapes=[pltpu.VMEM((tm, 2 * D_FF), F32), pltpu.VMEM((tm, D_FF), F32)] + _ffn_weight_scratch()
        + _comm_sems() * n,
        compiler_params=_cp(("arbitrary",)),
    )(x1, f, dx2, gpre, gathered, gpost, *[src for src, _ in carried])


def atb(a, b, tk):
    t, k = a.shape
    n = b.shape[1]
    tt = min(TT, t)
    tk = min(tk, k)
    steps = t // tt

    def body(a_ref, b_ref, o_ref, acc_ref):
        i = pl.program_id(1)
        _acc(acc_ref, i == 0, _mm_tn(a_ref[...], b_ref[...]))

        @pl.when(i == steps - 1)
        def _():
            o_ref[...] = acc_ref[...].astype(BF16)

    return pl.pallas_call(
        body, name="atb", grid=(k // tk, steps),
        in_specs=[pl.BlockSpec((tt, tk), lambda j, i: (i, j)), pl.BlockSpec((tt, n), lambda j, i: (i, 0))],
        out_specs=pl.BlockSpec((tk, n), lambda j, i: (j, 0)),
        out_shape=jax.ShapeDtypeStruct((k, n), BF16),
        scratch_shapes=[pltpu.VMEM((tk, n), F32)],
        compiler_params=_cp(("parallel", "arbitrary")),
    )(a, b)


FFN_TILE_SHARDS = 4


def atb_ffn_chunks(a, b, first_piece, chunks=None):
    t, k = a.shape
    tt = min(TT, t)
    tk = FFN_TILE_SHARDS * FFN_SHARD
    steps = t // tt
    per_piece = N_DEV // FFN_TILE_SHARDS

    def body(*refs):
        a_ref, b_ref, o_ref, acc_ref = refs[0], refs[1], refs[-2], refs[-1]
        i = pl.program_id(1)
        _acc(acc_ref, i == 0, _mm_tn(a_ref[...], b_ref[...]))

        @pl.when(i == steps - 1)
        def _():
            for d in range(FFN_TILE_SHARDS):
                o_ref[d, 0] = acc_ref[d * FFN_SHARD:(d + 1) * FFN_SHARD, :].astype(BF16)

    hbm = pl.BlockSpec(memory_space=pl.ANY)
    return pl.pallas_call(
        body, name="atb_ffn_chunks", grid=(k // tk, steps),
        in_specs=[pl.BlockSpec((tt, tk), lambda j, i: (i, j)), pl.BlockSpec((tt, D), lambda j, i: (i, 0))]
        + ([] if chunks is None else [hbm]),
        out_specs=pl.BlockSpec((FFN_TILE_SHARDS, 1, FFN_SHARD, D),
                               lambda j, i: (j % per_piece, first_piece + j // per_piece, 0, 0)),
        out_shape=jax.ShapeDtypeStruct((N_DEV, len(FFN_PIECES), FFN_SHARD, D), BF16),
        input_output_aliases={} if chunks is None else {2: 0},
        scratch_shapes=[pltpu.VMEM((tk, D), F32)],
        compiler_params=_cp(("parallel", "arbitrary")),
    )(a, b, *([] if chunks is None else [chunks]))


def mix_bwd(dx1, z, ya, gm, lng, lnb, wsp, wspt, bias, cw, gout, wout, gpost):
    t = dx1.shape[0]
    tm = min(TM, t)
    ng = SG_W // GROUP

    def body(dx1_ref, zcv_ref, halo_ref, zsg_ref, ya_ref, gm_ref, lng_ref, lnb_ref, wsp_ref, wspt_ref, bias_ref,
             cw_ref, gout_ref, wout_ref, gpost_ref,
             dya_ref, dyc_ref, dzsg_ref, mix_ref, do_ref, dgpost_ref, dgout_ref, dlng_ref, dlnb_ref, dwsp_ref,
             dbias_ref, wc_ref, wct_ref, mixed_ref, dv_ref):
        i = pl.program_id(0)
        first = i == 0
        gm = gm_ref[...]
        for g in range(ng):
            wc_ref[g] = _tril_bf16(wsp_ref, g)
            wct_ref[g] = jnp.where(
                lax.broadcasted_iota(jnp.int32, (CHUNK, CHUNK), 0) <= lax.broadcasted_iota(jnp.int32, (CHUNK, CHUNK), 1),
                wspt_ref[g], 0.0).astype(BF16)
        zsg = zsg_ref[...]
        lng = lng_ref[...]
        u, v, vh, r, th = _sgu_fwd(zsg, gm, lng, lnb_ref[...], wc_ref, bias_ref[...], mixed_ref)
        mixed = mixed_ref[...]
        yb = u * mixed
        yc, _, _, _, _ = _conv_fwd(zcv_ref[...], halo_ref[...], first, cw_ref[...])
        gout, gpost = gout_ref[...], gpost_ref[...]
        ga, gb_, gc_ = gout[:, :512], gout[:, 512:768], gout[:, 768:]
        na, yah, ra = _rms_fwd(ya_ref[...], ga)
        nb, ybh, rb = _rms_fwd(yb, gb_)
        nc, ych, rc = _rms_fwd(yc, gc_)
        mix = jnp.concatenate([na, nb, nc], axis=1).astype(BF16)
        _, oh, ro = _rms_fwd(_mm(mix, wout_ref[...]), gpost)
        do, dgpost = _rms_bwd(oh, ro, gpost, dx1_ref[...])
        dob = do.astype(BF16)
        dmix = _mm_nt(dob, wout_ref[...])
        dya, dga = _rms_bwd(yah, ra, ga, dmix[:, :512])
        dyb, dgb = _rms_bwd(ybh, rb, gb_, dmix[:, 512:768])
        dyc, dgc = _rms_bwd(ych, rc, gc_, dmix[:, 768:])
        dya_ref[...] = dya
        dyc_ref[...] = dyc
        mix_ref[...] = mix
        do_ref[...] = dob
        _acc(dgpost_ref, first, dgpost)
        _acc(dgout_ref, first, jnp.concatenate([dga, dgb, dgc], axis=1))
        du = dyb * mixed
        dmixed = dyb * u
        lane = lax.broadcasted_iota(jnp.int32, (CHUNK, SG_W), 1)
        row = lax.broadcasted_iota(jnp.int32, (CHUNK, CHUNK), 0)
        col = lax.broadcasted_iota(jnp.int32, (CHUNK, CHUNK), 1)
        dbias = jnp.zeros((CHUNK, SG_W), F32)
        dw = [jnp.zeros((CHUNK, CHUNK), F32) for _ in range(ng)]
        for c in range(tm // CHUNK):
            rows = slice(c * CHUNK, (c + 1) * CHUNK)
            dm = dmixed[rows]
            dbias = dbias + dm
            dmb = dm.astype(BF16)
            vb = v[rows].astype(BF16)
            dvc = jnp.zeros((CHUNK, SG_W), F32)
            for g in range(ng):
                in_g = lane // GROUP == g
                dvc = dvc + jnp.where(in_g, _mm(wct_ref[g], dmb), 0.0)
                dw[g] = dw[g] + _mm_nt(jnp.where(in_g, dmb, jnp.zeros_like(dmb)), vb)
            dv_ref[rows, :] = dvc
        for g in range(ng):
            dwg = jnp.where(col <= row, dw[g], 0.0)

            @pl.when(first)
            def _():
                dwsp_ref[g] = dwg

            @pl.when(jnp.logical_not(first))
            def _():
                dwsp_ref[g] += dwg
        _acc(dbias_ref, first, _gmean(dbias, gm) * GROUP)
        dv = dv_ref[...]
        _acc(dlng_ref, first, jnp.sum(dv * vh, axis=0, keepdims=True))
        _acc(dlnb_ref, first, jnp.sum(dv, axis=0, keepdims=True))
        dvh = dv * lng
        dv0 = r * (dvh - _gmean(dvh, gm) - vh * _gmean(dvh * vh, gm))
        dzsg_ref[...] = (jnp.concatenate([du, dv0], axis=1) * _gelu_grad(zsg, th)).astype(BF16)

    hb = tm // 8
    row_ = lambda w: pl.BlockSpec((tm, w), lambda i: (i, 0))
    vec = lambda w: pl.BlockSpec((1, w), lambda i: (0, 0))
    return pl.pallas_call(
        body, name="mix_bwd", grid=(t // tm,),
        in_specs=[row_(D),
                  pl.BlockSpec((tm, 768), lambda i: (i, 0)),
                  pl.BlockSpec((8, 768), lambda i: (jnp.maximum(i * hb - 1, 0), 0)),
                  pl.BlockSpec((tm, 512), lambda i: (i, 3)),
                  row_(512),
                  _whole(), _whole(), _whole(), _whole(), _whole(), _whole(), _whole(), _whole(), _whole(), _whole()],
        out_specs=[row_(512), row_(CV_W), row_(512), row_(D), row_(D), vec(D), vec(D), vec(SG_W), vec(SG_W),
                   pl.BlockSpec((ng, CHUNK, CHUNK), lambda i: (0, 0, 0)),
                   pl.BlockSpec((CHUNK, SG_W), lambda i: (0, 0))],
        out_shape=[jax.ShapeDtypeStruct((t, 512), F32), jax.ShapeDtypeStruct((t, CV_W), F32),
                   jax.ShapeDtypeStruct((t, 512), BF16), jax.ShapeDtypeStruct((t, D), BF16),
                   jax.ShapeDtypeStruct((t, D), BF16), jax.ShapeDtypeStruct((1, D), F32),
                   jax.ShapeDtypeStruct((1, D), F32), jax.ShapeDtypeStruct((1, SG_W), F32),
                   jax.ShapeDtypeStruct((1, SG_W), F32), jax.ShapeDtypeStruct((ng, CHUNK, CHUNK), F32),
                   jax.ShapeDtypeStruct((CHUNK, SG_W), F32)],
        scratch_shapes=[pltpu.VMEM((ng, CHUNK, CHUNK), BF16), pltpu.VMEM((ng, CHUNK, CHUNK), BF16),
                        pltpu.VMEM((tm, SG_W), F32), pltpu.VMEM((tm, SG_W), F32)],
        compiler_params=_cp(("arbitrary",)),
    )(dx1, z, z, z, ya, gm, lng, lnb, wsp, wspt, bias, cw, gout, wout, gpost)


def conv_bwd(dyc, z, cw):
    t = dyc.shape[0]
    tm = min(TM, t)
    hb = tm // 8
    last_blk = t // 8 - 1

    def body(dyc_ref, dyct_ref, zcv_ref, head_ref, tail_ref, cw_ref, dz_ref, dcw_ref):
        i = pl.program_id(0)
        first = i == 0
        last = i == pl.num_programs(0) - 1
        cw = cw_ref[...]
        zcv = zcv_ref[...]
        gb, gc, hh = zcv[:, :CV_W], zcv[:, CV_W:2 * CV_W], zcv[:, 2 * CV_W:]
        _, conv, y, y1, y2 = _conv_fwd(zcv, head_ref[...], first, cw)
        dyc = dyc_ref[...]
        dconv = dyc * gb
        tail = jnp.where(last, 0.0, dyct_ref[...] * tail_ref[:, :CV_W])
        d1 = _shift_up(dconv, 1, tail)
        d2 = _shift_up(dconv, 2, tail)
        dy = dconv * cw[2:3, :] + d1 * cw[1:2, :] + d2 * cw[0:1, :]
        dz_ref[...] = jnp.concatenate([dyc * conv, dy * hh, dy * gc], axis=1).astype(BF16)
        tap = lax.broadcasted_iota(jnp.int32, (8, CV_W), 0)
        dcw = jnp.where(tap == 0, jnp.sum(dconv * y2, axis=0, keepdims=True),
                        jnp.where(tap == 1, jnp.sum(dconv * y1, axis=0, keepdims=True),
                                  jnp.where(tap == 2, jnp.sum(dconv * y, axis=0, keepdims=True), 0.0)))
        _acc(dcw_ref, first, dcw)

    return pl.pallas_call(
        body, name="conv_bwd", grid=(t // tm,),
        in_specs=[pl.BlockSpec((tm, CV_W), lambda i: (i, 0)),
                  pl.BlockSpec((8, CV_W), lambda i: (jnp.minimum((i + 1) * hb, last_blk), 0)),
                  pl.BlockSpec((tm, 768), lambda i: (i, 0)),
                  pl.BlockSpec((8, 768), lambda i: (jnp.maximum(i * hb - 1, 0), 0)),
                  pl.BlockSpec((8, 768), lambda i: (jnp.minimum((i + 1) * hb, last_blk), 0)),
                  _whole()],
        out_specs=[pl.BlockSpec((tm, 768), lambda i: (i, 0)), pl.BlockSpec((8, CV_W), lambda i: (0, 0))],
        out_shape=[jax.ShapeDtypeStruct((t, 768), BF16), jax.ShapeDtypeStruct((8, CV_W), F32)],
        compiler_params=_cp(("arbitrary",)),
    )(dyc, dyc, z, z, z, cw)


def attn_bwd(q, k, v, o, lse, do, carried=()):
    t = q.shape[0]
    tq = min(TQ, t)
    nq = t // tq
    last_pair = HEADS // 2 - 1
    n = len(carried)

    def body(*refs):
        j = pl.program_id(1)
        q_ref, k_ref, v_ref, o_ref, lse_ref, do_ref = refs[:6]
        dq_out_ref, dk_ref, dv_ref = refs[6 + n:9 + n]
        dq_ref = refs[9 + 2 * n]
        sems = refs[10 + 2 * n:]
        stages = [_exchange_steps(refs[6 + a], refs[9 + n + a], carried[a][1], *sems[3 * a:3 * a + 3]) for a in range(n)]
        if n:
            pl.when((pl.program_id(0) == 0) & (j == 0))(_each(stages, 0))

        @pl.when(j == 0)
        def _():
            dq_ref[...] = jnp.zeros_like(dq_ref)

        row = lax.broadcasted_iota(jnp.int32, (tq, tq), 0)
        col = lax.broadcasted_iota(jnp.int32, (tq, tq), 1)
        vlane = lax.broadcasted_iota(jnp.int32, (tq, 2 * V_DIM), 1)
        head_lanes = [slice(h * HEAD_PAD, (h + 1) * HEAD_PAD) for h in range(2)]

        def step(i, carry, masked):
            start = pl.multiple_of(i * tq, tq)
            do_blk = do_ref[pl.ds(start, tq), :]
            o_blk = o_ref[pl.ds(start, tq), :]
            vb = v_ref[...]
            dks, dv_acc = [], carry[2]
            for h in range(2):
                lanes = head_lanes[h]
                qb = q_ref[pl.ds(start, tq), lanes]
                kb = k_ref[:, lanes]
                dob = jnp.where((vlane // V_DIM) == h, do_blk, 0.0)
                delta = jnp.sum(dob * o_blk, axis=-1, keepdims=True)
                lse2 = lse_ref[pl.ds(start, tq), lanes][:, 0:1] * LOG2E
                s = _mm_nt(qb, kb)
                if masked:
                    s = jnp.where(col <= row, s, NEG)
                p = jnp.exp2(s * SCALE_LOG2E - lse2)
                dob16 = dob.astype(BF16)
                dp = _mm_nt(dob16, vb)
                ds = (p * (dp - delta) * SCALE).astype(BF16)
                dv_acc = dv_acc + _mm_tn(p.astype(BF16), dob16)
                dks.append(carry[h] + _mm_tn(ds, qb))
                dq_ref[pl.ds(start, tq), lanes] += _mm(ds, kb)
            return dks[0], dks[1], dv_acc

        zero = jnp.zeros((tq, HEAD_PAD), F32)
        carry = step(j, (zero, zero, jnp.zeros((tq, 2 * V_DIM), F32)), True)
        rest = nq - 1 - j
        carry = lax.fori_loop(0, rest // 2, lambda u, c: step(j + 2 + 2 * u, step(j + 1 + 2 * u, c, False), False), carry)
        dk0, dk1, dv_acc = lax.fori_loop(0, rest % 2, lambda _, c: step(nq - 1, c, False), carry)
        dk_ref[:, head_lanes[0]] = dk0.astype(BF16)
        dk_ref[:, head_lanes[1]] = dk1.astype(BF16)
        dv_ref[...] = dv_acc.astype(BF16)

        @pl.when(j == nq - 1)
        def _():
            dq_out_ref[...] = dq_ref[...].astype(BF16)

        if n:
            pl.when((pl.program_id(0) == last_pair) & (j == nq - 1))(_each(stages, 1))

    hbm = pl.BlockSpec(memory_space=pl.ANY)
    return pl.pallas_call(
        body, name=f"attn_bwd_exchange{n}" if n else "attn_bwd", grid=(HEADS // 2, nq),
        in_specs=[pl.BlockSpec((t, 2 * HEAD_PAD), lambda p, j: (0, p)),
                  pl.BlockSpec((tq, 2 * HEAD_PAD), lambda p, j: (j, p)),
                  pl.BlockSpec((tq, 2 * V_DIM), lambda p, j: (j, p)),
                  pl.BlockSpec((t, 2 * V_DIM), lambda p, j: (0, p)),
                  pl.BlockSpec((t, 2 * HEAD_PAD), lambda p, j: (0, p)),
                  pl.BlockSpec((t, 2 * V_DIM), lambda p, j: (0, p))] + [hbm] * n,
        out_specs=[pl.BlockSpec((t, 2 * HEAD_PAD), lambda p, j: (0, p)),
                   pl.BlockSpec((tq, 2 * HEAD_PAD), lambda p, j: (j, p)),
                   pl.BlockSpec((tq, 2 * V_DIM), lambda p, j: (j, p))] + [hbm] * n,
        out_shape=[jax.ShapeDtypeStruct((t, HEADS * HEAD_PAD), BF16), jax.ShapeDtypeStruct((t, HEADS * HEAD_PAD), BF16),
                   jax.ShapeDtypeStruct((t, HEADS * V_DIM), BF16)]
        + [jax.ShapeDtypeStruct(src.shape if scatter else (N_DEV,) + src.shape, src.dtype) for src, scatter in carried],
        scratch_shapes=[pltpu.VMEM((t, 2 * HEAD_PAD), F32)] + _comm_sems() * n,
        compiler_params=_cp(("arbitrary", "arbitrary") if n else ("parallel", "arbitrary")),
    )(q, k, v, o, lse, do, *[src for src, _ in carried])


def mla_proj_bwd(dq, dk, dv, z, ca, sb, sc, gq, gkv, wuq, wukv):
    t = z.shape[0]
    tm = min(TM, t)

    def body(dq_ref, dk_ref, dv_ref, z_ref, ca_ref, sb_ref, sc_ref, gq_ref, gkv_ref, wuq_ref, wukv_ref,
             dz_ref, cq_ref, ckv_ref, dqp_ref, dkvp_ref, dgq_ref, dgkv_ref):
        first = pl.program_id(0) == 0
        z = z_ref[...]
        ca, sb, sc = ca_ref[...], sb_ref[...], sc_ref[...]
        gq, gkv = gq_ref[...], gkv_ref[...]
        cq, cqh, rq = _rms_fwd(z[:, :Q_RANK], gq)
        ckv, ckvh, rkv = _rms_fwd(z[:, Q_RANK:Q_RANK + KV_RANK], gkv)
        lane = lax.broadcasted_iota(jnp.int32, (tm, HEAD_PAD), 1)
        dkr = jnp.zeros((tm, HEAD_PAD), F32)
        for h in range(HEADS):
            lanes = slice(h * HEAD_PAD, (h + 1) * HEAD_PAD)
            dqp_ref[:, lanes] = _rope_t(dq_ref[:, lanes].astype(F32), ca, sb, sc).astype(BF16)
            dkh = dk_ref[:, lanes].astype(F32)
            dkr = dkr + dkh
            dkvp_ref[:, lanes] = jnp.where(lane < NOPE, dkh, 0.0).astype(BF16)
        dkvp_ref[:, HEADS * HEAD_PAD:] = dv_ref[...].astype(BF16)
        dkr = pltpu.roll(_rope_t(jnp.where(lane >= NOPE, dkr, 0.0), ca, sb, sc), HEAD_PAD - NOPE, 1)
        dkr = jnp.where(lane < ROPE, dkr, 0.0)
        dcq = _mm(dqp_ref[...], wuq_ref[...])
        dckv = _mm(dkvp_ref[...], wukv_ref[...])
        dzq, dgq = _rms_bwd(cqh, rq, gq, dcq)
        dzkv, dgkv = _rms_bwd(ckvh, rkv, gkv, dckv)
        dz_ref[...] = jnp.concatenate([dzq, dzkv, dkr], axis=1).astype(BF16)
        cq_ref[...] = cq.astype(BF16)
        ckv_ref[...] = ckv.astype(BF16)
        _acc(dgq_ref, first, dgq)
        _acc(dgkv_ref, first, dgkv)

    row = lambda w: pl.BlockSpec((tm, w), lambda i: (i, 0))
    vec = lambda w: pl.BlockSpec((1, w), lambda i: (0, 0))
    return pl.pallas_call(
        body, name="mla_proj_bwd", grid=(t // tm,),
        in_specs=[row(1024), row(1024), row(512), pl.BlockSpec((tm, 768), lambda i: (i, 1)),
                  row(HEAD_PAD), row(HEAD_PAD), row(HEAD_PAD), _whole(), _whole(), _whole(), _whole()],
        out_specs=[row(768), row(Q_RANK), row(KV_RANK), row(1024), row(1536), vec(Q_RANK), vec(KV_RANK)],
        out_shape=[jax.ShapeDtypeStruct((t, 768), BF16), jax.ShapeDtypeStruct((t, Q_RANK), BF16),
                   jax.ShapeDtypeStruct((t, KV_RANK), BF16), jax.ShapeDtypeStruct((t, 1024), BF16),
                   jax.ShapeDtypeStruct((t, 1536), BF16), jax.ShapeDtypeStruct((1, Q_RANK), F32),
                   jax.ShapeDtypeStruct((1, KV_RANK), F32)],
        compiler_params=_cp(("arbitrary",)),
    )(dq, dk, dv, z, ca, sb, sc, gq, gkv, wuq, wukv)


def pre_in_bwd(x, dx1, dzcv, dzmla, dzsg, g, w):
    t = x.shape[0]
    tm = min(TM, t)

    def body(x_ref, dx1_ref, dzcv_ref, dzmla_ref, dzsg_ref, g_ref, w_ref, dx_ref, h_ref, dz_ref, dg_ref):
        g = g_ref[...]
        h, xh, r = _rms_fwd(x_ref[...], g)
        dz = jnp.concatenate([dzcv_ref[...], dzmla_ref[...], dzsg_ref[...]], axis=1)
        dx, dg = _rms_bwd(xh, r, g, _mm(dz, w_ref[...]))
        dx_ref[...] = dx1_ref[...] + dx
        h_ref[...] = h.astype(BF16)
        dz_ref[...] = dz
        _acc(dg_ref, pl.program_id(0) == 0, dg)

    row = lambda w_: pl.BlockSpec((tm, w_), lambda i: (i, 0))
    return pl.pallas_call(
        body, name="pre_in_bwd", grid=(t // tm,),
        in_specs=[row(D), row(D), row(768), row(768), row(512), _whole(), _whole()],
        out_specs=[row(D), row(D), row(Z_W), pl.BlockSpec((1, D), lambda i: (0, 0))],
        out_shape=[jax.ShapeDtypeStruct((t, D), F32), jax.ShapeDtypeStruct((t, D), BF16),
                   jax.ShapeDtypeStruct((t, Z_W), BF16), jax.ShapeDtypeStruct((1, D), F32)],
        compiler_params=_cp(("arbitrary",)),
    )(x, dx1, dzcv, dzmla, dzsg, g, w)


MESH = pl.DeviceIdType.MESH


def _place():
    return lax.axis_index("x"), lax.axis_index("y"), lax.axis_index("c")


def _comm_sems():
    return [pltpu.SemaphoreType.DMA((7,)), pltpu.SemaphoreType.DMA((7,)), pltpu.SemaphoreType.DMA]


def _gather_steps(x_ref, out_ref, send_sems, recv_sems, local_sem):
    x, y, c = _place()
    me, sibling = (x, y, c), (x, y, 1 - c)
    chips = [(1 - x, y), (x, 1 - y), (1 - x, 1 - y)]

    def slot(px, py, pc):
        return out_ref.at[4 * px + 2 * py + pc]

    def copy(k, blk, to, src=None):
        return pltpu.make_async_remote_copy(
            src_ref=slot(*blk) if src is None else src, dst_ref=slot(*blk),
            send_sem=send_sems.at[k], recv_sem=recv_sems.at[k], device_id=to, device_id_type=MESH)

    mine = pltpu.make_async_copy(x_ref, slot(*me), local_sem)
    first = [copy(0, me, sibling, src=x_ref)] + [copy(1 + j, me, (*chip, c), src=x_ref) for j, chip in enumerate(chips)]
    passed = [copy(4 + j, (*chip, c), sibling) for j, chip in enumerate(chips)]

    def start():
        mine.start()
        for cp in first:
            cp.start()

    def forward():
        for j, chip in enumerate(chips):
            copy(1 + j, (*chip, c), me).wait_recv()
            passed[j].start()

    def finish():
        copy(0, sibling, me).wait_recv()
        for j, chip in enumerate(chips):
            copy(4 + j, (*chip, 1 - c), me).wait_recv()
        for cp in first + passed:
            cp.wait_send()
        mine.wait()

    return start, forward, finish


def _exchange_steps(src_ref, out_ref, scatter, send_sems, recv_sems, local_sem):
    x, y, c = _place()
    me = 4 * x + 2 * y + c
    own = pltpu.make_async_copy(src_ref.at[me] if scatter else src_ref, out_ref.at[me], local_sem)
    copies = []
    for k in range(1, N_DEV):
        px = 1 - x if k & 4 else x
        py = 1 - y if k & 2 else y
        pc = 1 - c if k & 1 else c
        copies.append(pltpu.make_async_remote_copy(
            src_ref=src_ref.at[4 * px + 2 * py + pc] if scatter else src_ref, dst_ref=out_ref.at[me],
            send_sem=send_sems.at[k - 1], recv_sem=recv_sems.at[k - 1], device_id=(px, py, pc), device_id_type=MESH))

    def start():
        own.start()
        for cp in copies:
            cp.start()

    def finish():
        for cp in copies:
            cp.wait_recv()
        for cp in copies:
            cp.wait_send()
        own.wait()

    return start, finish


def all_gather(block):
    def body(x_ref, out_ref, *sems):
        for stage in _gather_steps(x_ref, out_ref, *sems):
            stage()

    return pl.pallas_call(
        body, name="all_gather",
        in_specs=[pl.BlockSpec(memory_space=pl.ANY)],
        out_specs=pl.BlockSpec(memory_space=pl.ANY),
        out_shape=jax.ShapeDtypeStruct((N_DEV,) + block.shape, block.dtype),
        scratch_shapes=_comm_sems(),
    )(block)


def _row_tile(r, cap):
    return max(d for d in range(16, cap + 1, 16) if r % d == 0)


def sum_adamw(parts, w, m, v, cap, carried=()):
    nl, r, c = w.shape
    tr = _row_tile(r, cap)
    steps = r // tr
    n = len(carried)
    c1 = 1.0 / (1.0 - ADAM_B1 ** ADAM_STEP)
    c2 = 1.0 / (1.0 - ADAM_B2 ** ADAM_STEP)

    def body(*refs):
        p_refs = refs[:nl]
        w_ref, m_ref, v_ref = refs[nl:nl + 3]
        g_ref, d_ref, nm_ref, nv_ref = refs[nl + 3 + n:nl + 7 + n]
        sems = refs[nl + 7 + 2 * n:]
        stages = [_exchange_steps(refs[nl + 3 + a], refs[nl + 7 + n + a], carried[a][1], *sems[3 * a:3 * a + 3])
                  for a in range(n)]
        layer, i = pl.program_id(0), pl.program_id(1)
        if n:
            pl.when((layer == 0) & (i == 0))(_each(stages, 0))

        def update(p_ref):
            g = p_ref[0].astype(F32)
            for k in range(1, N_DEV):
                g = g + p_ref[k].astype(F32)
            m_new = ADAM_B1 * m_ref[...] + (1.0 - ADAM_B1) * g
            v_new = ADAM_B2 * v_ref[...] + (1.0 - ADAM_B2) * (g * g)
            g_ref[...] = g
            nm_ref[...] = m_new
            nv_ref[...] = v_new
            d_ref[...] = -ADAM_LR * ((m_new * c1) / (jnp.sqrt(v_new * c2) + ADAM_EPS) + ADAM_WD * w_ref[...])

        for k in range(nl):
            pl.when(layer == k)(functools.partial(update, p_refs[k]))
        if n:
            pl.when((layer == nl - 1) & (i == steps - 1))(_each(stages, 1))

    def parts_spec(k):
        return pl.BlockSpec((N_DEV, tr, c), lambda l, i: (0, jnp.where(l == k, i, jnp.where(l < k, 0, steps - 1)), 0))

    blk = pl.BlockSpec((None, tr, c), lambda l, i: (l, i, 0))
    out = jax.ShapeDtypeStruct((nl, r, c), F32)
    hbm = pl.BlockSpec(memory_space=pl.ANY)
    return pl.pallas_call(
        body, name=f"sum_adamw_exchange{n}" if n else "sum_adamw", grid=(nl, steps),
        in_specs=[parts_spec(k) for k in range(nl)] + [blk, blk, blk] + [hbm] * n,
        out_specs=[blk, blk, blk, blk] + [hbm] * n,
        out_shape=[out, out, out, out]
        + [jax.ShapeDtypeStruct(src.shape if scatter else (N_DEV,) + src.shape, src.dtype) for src, scatter in carried],
        scratch_shapes=_comm_sems() * n,
        compiler_params=_cp(("arbitrary", "arbitrary")),
    )(*parts, w, m, v, *[src for src, _ in carried])


PACK_W = 1024
MIX_PIECES = (("w_out", D // N_DEV, D, False), ("w_uq", HEADS * (NOPE + ROPE) // N_DEV, Q_RANK, True),
              ("w_ukv", HEADS * (NOPE + V_DIM) // N_DEV, KV_RANK, True), ("conv", 16, PACK_W, False),
              ("w_in", IN_W // N_DEV, D, True))
FFN_PIECES = (("w_gate", D_FF // N_DEV, D, True), ("w_up", D_FF // N_DEV, D, True), ("w_down", D_FF // N_DEV, D, False))
OFFSET = {}
for _pieces in (MIX_PIECES, FFN_PIECES):
    _off = 0
    for _name, _rows, _, _ in _pieces:
        OFFSET[_name] = _off
        _off += _rows + -_rows % 16
assert all(o % 16 == 0 for o in OFFSET.values())
assert [OFFSET[n] for n in ("w_gate", "w_up", "w_down")] == [0, FFN_SHARD, 2 * FFN_SHARD]
CONV_BITS = 3 * (CV_W // N_DEV) * 2


def _to_pack(shards, dtype, pieces, conv=None):
    nl = shards["w_in"].shape[0]
    parts = []
    for name, rows, cols, transposed in pieces:
        if name == "conv":
            if conv is None:
                a = jnp.zeros((nl, rows, PACK_W), dtype)
            else:
                bits = lax.bitcast_convert_type(conv.astype(F32), BF16).reshape(nl, CONV_BITS)
                a = jnp.pad(bits, ((0, 0), (0, rows * PACK_W - CONV_BITS))).reshape(nl, rows, PACK_W)
        else:
            a = shards[name].astype(dtype)
            a = jnp.swapaxes(a, 1, 2) if transposed else a
            a = jnp.pad(a, ((0, 0), (0, -rows % 16), (0, PACK_W - cols)))
        parts.append(a)
    return jnp.concatenate(parts, axis=1)


def _from_pack(pack, pieces):
    out = {}
    for name, rows, cols, transposed in pieces:
        if name != "conv":
            a = pack[:, OFFSET[name]:OFFSET[name] + rows, :cols]
            out[name] = jnp.swapaxes(a, 1, 2) if transposed else a
    return out


def _mix_weights(g):
    def rows(name):
        _, n, cols, _ = next(p for p in MIX_PIECES if p[0] == name)
        return g[:, OFFSET[name]:OFFSET[name] + n, :cols]

    w_in_t = rows("w_in").reshape(IN_W, D)
    w_in_p = jnp.concatenate([w_in_t[1184:], w_in_t[:672], jnp.zeros((96, D), BF16), w_in_t[672:1184]], axis=0)
    w_uq_p = jnp.pad(rows("w_uq"), ((0, 0), (0, HEAD_PAD - NOPE - ROPE), (0, 0))).reshape(HEADS * HEAD_PAD, Q_RANK)
    kv = rows("w_ukv")
    w_k = jnp.pad(kv[:, :NOPE], ((0, 0), (0, HEAD_PAD - NOPE), (0, 0))).reshape(HEADS * HEAD_PAD, KV_RANK)
    w_ukv_p = jnp.concatenate([w_k, kv[:, NOPE:].reshape(HEADS * V_DIM, KV_RANK)], axis=0)
    bits = rows("conv").reshape(N_DEV, -1)[:, :CONV_BITS].reshape(N_DEV, 3, CV_W // N_DEV, 2)
    conv_w = jnp.moveaxis(lax.bitcast_convert_type(bits, F32), 0, 1).reshape(3, CV_W)
    return dict(w_in=w_in_p, w_uq=w_uq_p, w_ukv=w_ukv_p, w_out=rows("w_out").reshape(D, D), conv_w=conv_w)


def _grad_chunks(full):
    d_in = full["w_in"]
    d_in = jnp.concatenate([d_in[768:768 + 672], d_in[1536:], d_in[:768]], axis=0)
    d_uq = full["w_uq"].reshape(HEADS, HEAD_PAD, Q_RANK)[:, :NOPE + ROPE]
    d_k = full["w_ukv"][:HEADS * HEAD_PAD].reshape(HEADS, HEAD_PAD, KV_RANK)[:, :NOPE]
    d_v = full["w_ukv"][HEADS * HEAD_PAD:].reshape(HEADS, V_DIM, KV_RANK)
    mats = dict(w_in=d_in, w_uq=d_uq, w_ukv=jnp.concatenate([d_k, d_v], axis=1), w_out=full["w_out"])
    parts = []
    for name, rows, cols, _ in MIX_PIECES:
        if name == "conv":
            parts.append(jnp.zeros((N_DEV, rows, PACK_W), BF16))
        else:
            parts.append(jnp.pad(mats[name].reshape(N_DEV, rows, cols), ((0, 0), (0, -rows % 16), (0, PACK_W - cols))))
    return jnp.concatenate(parts, axis=1)


SMALL = (("mix_pre_g", (D,)), ("mix_post_g", (D,)), ("ffn_pre_g", (D,)), ("ffn_post_g", (D,)), ("q_norm_g", (Q_RANK,)),
         ("kv_norm_g", (KV_RANK,)), ("sg_ln_g", (SG_W,)), ("sg_ln_b", (SG_W,)), ("w_sp", (4, CHUNK, CHUNK)),
         ("b_sp", (4, CHUNK)), ("out_norm_g", (D,)))
SMALL_ROWS = 576


def _pack_small(vals, nl):
    flat = jnp.concatenate([vals[name].reshape(nl, -1) for name, _ in SMALL] + [vals["conv_w"].reshape(nl, -1)], axis=1)
    return jnp.pad(flat, ((0, 0), (0, SMALL_ROWS * 128 - flat.shape[1]))).reshape(nl * SMALL_ROWS, 128)


def _unpack_small(pack, nl):
    flat = pack.reshape(nl, SMALL_ROWS * 128)
    out, off = {}, 0
    for name, shape in SMALL + (("conv_w", (3, CV_W)),):
        n = int(np.prod(shape))
        out[name] = flat[:, off:off + n].reshape((nl,) + shape)
        off += n
    return out


def _layer_fwd(x, lw, sp, tabs, consts, next_pack):
    ca, sb, sc = tabs
    z = pre_in_fwd(x, sp["mix_pre_g"], lw["w_in"])
    q, k, v = mla_proj_fwd(z, ca, sb, sc, sp["q_norm_g"], sp["kv_norm_g"], lw["w_uq"], lw["w_ukv"])
    ya, lse, ffn_gathered, *mix_gathered = attn_fwd(q, k, v, next_pack)
    lw["ffn"] = ffn_gathered
    x1 = mix_fwd(x, z, ya, consts["gm"], sp["sg_ln_g"], sp["sg_ln_b"], sp["w_sp"], sp["bias"], lw["conv_w"],
                 sp["out_norm_g"], lw["w_out"], sp["mix_post_g"])
    x2, f = ffn_fwd(x1, sp["ffn_pre_g"], lw["ffn"], sp["ffn_post_g"])
    return x2, (x, z, q, k, v, ya, lse, x1, f), mix_gathered


def _layer_bwd(dx2, saved, lw, sp, tabs, consts, pending):
    ca, sb, sc = tabs
    x, z, q, k, v, ya, lse, x1, f = saved
    dx1, h2, dab, s, df, d_ffn_pre, d_ffn_post, *received = ffn_bwd(x1, f, dx2, sp["ffn_pre_g"], lw["ffn"], sp["ffn_post_g"],
                                                                    pending)
    ffn_chunks = atb_ffn_chunks(s, df, 2, atb_ffn_chunks(dab, h2, 0)).reshape(N_DEV, len(FFN_PIECES) * FFN_SHARD, D)
    dya, dyc, dzsg, mix, do, d_mix_post, d_out_norm, d_lng, d_lnb, d_wsp, d_bias = mix_bwd(
        dx1, z, ya, consts["gm"], sp["sg_ln_g"], sp["sg_ln_b"], sp["w_sp"], sp["w_sp_t"], sp["bias"], lw["conv_w"],
        sp["out_norm_g"], lw["w_out"], sp["mix_post_g"])
    d_w_out = atb(mix, do, 1024)
    dzcv, d_cw = conv_bwd(dyc, z, lw["conv_w"])
    dq, dk, dv, got_ffn = attn_bwd(q, k, v, ya, lse, dya, ((ffn_chunks, True),))
    dzmla, cq, ckv, dqp, dkvp, d_gq, d_gkv = mla_proj_bwd(dq, dk, dv, z, ca, sb, sc, sp["q_norm_g"], sp["kv_norm_g"],
                                                          lw["w_uq"], lw["w_ukv"])
    d_w_uq = atb(dqp, cq, 1024)
    d_w_ukv = atb(dkvp, ckv, 1536)
    dx, h1, dz, d_mix_pre = pre_in_bwd(x, dx1, dzcv, dzmla, dzsg, sp["mix_pre_g"], lw["w_in"])
    d_w_in = atb(dz, h1, 2048)
    mix_chunks = _grad_chunks(dict(w_in=d_w_in, w_uq=d_w_uq, w_ukv=d_w_ukv, w_out=d_w_out))
    d_bsp = d_bias[:, ::GROUP].T
    small = dict(mix_pre_g=d_mix_pre[0], mix_post_g=d_mix_post[0], ffn_pre_g=d_ffn_pre[0], ffn_post_g=d_ffn_post[0],
                 q_norm_g=d_gq[0], kv_norm_g=d_gkv[0], sg_ln_g=d_lng[0], sg_ln_b=d_lnb[0], w_sp=d_wsp, b_sp=d_bsp,
                 out_norm_g=d_out_norm[0], conv_w=d_cw[:3])
    small_pack = _pack_small({name: a[None] for name, a in small.items()}, 1)
    return dx, ((mix_chunks, True), (small_pack, False)), [got_ffn] + received


def kernel(x, positions, mix_pre_g, mix_post_g, ffn_pre_g, ffn_post_g, w_in, q_norm_g, w_uq, kv_norm_g, w_ukv, sg_ln_g, sg_ln_b, w_sp, b_sp, conv_w, out_norm_g, w_out, w_gate, w_up, w_down, loss_target, m_mix_pre_g, m_mix_post_g, m_ffn_pre_g, m_ffn_post_g, m_w_in, m_q_norm_g, m_w_uq, m_kv_norm_g, m_w_ukv, m_sg_ln_g, m_sg_ln_b, m_w_sp, m_b_sp, m_conv_w, m_out_norm_g, m_w_out, m_w_gate, m_w_up, m_w_down, v_mix_pre_g, v_mix_post_g, v_ffn_pre_g, v_ffn_post_g, v_w_in, v_q_norm_g, v_w_uq, v_kv_norm_g, v_w_ukv, v_sg_ln_g, v_sg_ln_b, v_w_sp, v_b_sp, v_conv_w, v_out_norm_g, v_w_out, v_w_gate, v_w_up, v_w_down):
    nl = w_in.shape[0]
    t = x.shape[1]
    w = dict(mix_pre_g=mix_pre_g, mix_post_g=mix_post_g, ffn_pre_g=ffn_pre_g, ffn_post_g=ffn_post_g, w_in=w_in,
             q_norm_g=q_norm_g, w_uq=w_uq, kv_norm_g=kv_norm_g, w_ukv=w_ukv, sg_ln_g=sg_ln_g, sg_ln_b=sg_ln_b, w_sp=w_sp,
             b_sp=b_sp, conv_w=conv_w, out_norm_g=out_norm_g, w_out=w_out, w_gate=w_gate, w_up=w_up, w_down=w_down)
    m = dict(mix_pre_g=m_mix_pre_g, mix_post_g=m_mix_post_g, ffn_pre_g=m_ffn_pre_g, ffn_post_g=m_ffn_post_g, w_in=m_w_in,
             q_norm_g=m_q_norm_g, w_uq=m_w_uq, kv_norm_g=m_kv_norm_g, w_ukv=m_w_ukv, sg_ln_g=m_sg_ln_g, sg_ln_b=m_sg_ln_b,
             w_sp=m_w_sp, b_sp=m_b_sp, conv_w=m_conv_w, out_norm_g=m_out_norm_g, w_out=m_w_out, w_gate=m_w_gate,
             w_up=m_w_up, w_down=m_w_down)
    v = dict(mix_pre_g=v_mix_pre_g, mix_post_g=v_mix_post_g, ffn_pre_g=v_ffn_pre_g, ffn_post_g=v_ffn_post_g, w_in=v_w_in,
             q_norm_g=v_q_norm_g, w_uq=v_w_uq, kv_norm_g=v_kv_norm_g, w_ukv=v_w_ukv, sg_ln_g=v_sg_ln_g, sg_ln_b=v_sg_ln_b,
             w_sp=v_w_sp, b_sp=v_b_sp, conv_w=v_conv_w, out_norm_g=v_out_norm_g, w_out=v_w_out, w_gate=v_w_gate,
             w_up=v_w_up, w_down=v_w_down)

    mix_pack = _to_pack(w, BF16, MIX_PIECES, conv=w["conv_w"])
    ffn_pack = _to_pack(w, BF16, FFN_PIECES)
    consts = dict(gm=jnp.asarray(np.kron(np.eye(SG_W // GROUP), np.full((GROUP, GROUP), 1.0 / GROUP)), BF16))
    smalls = []
    for l in range(nl):
        sp = {name: w[name][l].reshape(1, -1) for name, shape in SMALL if len(shape) == 1}
        sp["w_sp"] = w["w_sp"][l]
        sp["w_sp_t"] = jnp.swapaxes(w["w_sp"][l], 1, 2)
        sp["bias"] = jnp.repeat(w["b_sp"][l].T, GROUP, axis=1)
        smalls.append(sp)
    inv_freq = 1.0 / (ROPE_THETA ** (jnp.arange(0, ROPE // 2, dtype=F32) / (ROPE // 2)))
    inv = jnp.zeros((1, HEAD_PAD), F32).at[0, NOPE:NOPE + ROPE].set(jnp.concatenate([inv_freq, inv_freq]))
    tabs = rope_tables(positions.reshape(t, 1).astype(F32), inv)

    h = x[0]
    saved, layers = [], []
    mix_gathered = [all_gather(mix_pack[0])]
    for l in range(nl):
        layers.append(_mix_weights(mix_gathered[0]))
        carried = (ffn_pack[l],) + ((mix_pack[l + 1],) if l + 1 < nl else ())
        h, s, mix_gathered = _layer_fwd(h, layers[l], smalls[l], tabs, consts, carried)
        saved.append(s)
    sq, dh = loss_head(h, loss_target[0])
    loss = lax.psum(0.5 * sq[0, 0] / D, ("x", "y", "c"))

    got_ffn, got_mix, got_small = [None] * nl, [None] * nl, [None] * nl
    pending = ()
    for l in reversed(range(nl)):
        dh, new_pending, received = _layer_bwd(dh, saved[l], layers[l], smalls[l], tabs, consts, pending)
        got_ffn[l] = received[0]
        if pending:
            got_mix[l + 1], got_small[l + 1] = received[1:]
        pending = new_pending

    me = 4 * lax.axis_index("x") + 2 * lax.axis_index("y") + lax.axis_index("c")
    *ffn_new, got_mix[0], got_small[0] = sum_adamw(got_ffn, *[_to_pack(d, F32, FFN_PIECES) for d in (w, m, v)], 176,
                                                   carried=pending)
    mix_new = sum_adamw(got_mix, *[_to_pack(d, F32, MIX_PIECES) for d in (w, m, v)], 208)
    got_small = jnp.concatenate(got_small, axis=1)
    g_big, d_big, m_big, v_big = [{**_from_pack(a, FFN_PIECES), **_from_pack(b, MIX_PIECES)}
                                  for a, b in zip(ffn_new, mix_new)]

    def full_conv(a):
        return lax.dynamic_update_slice(jnp.zeros((nl, 3, CV_W), F32), a, (0, 0, me * (CV_W // N_DEV)))

    def small_pack(d):
        return _pack_small({**{name: d[name] for name, _ in SMALL}, "conv_w": full_conv(d["conv_w"])}, nl)

    g_small, d_small, m_small, v_small = [_unpack_small(p[0], nl) for p in
                                          sum_adamw([got_small], small_pack(w)[None], small_pack(m)[None],
                                                    small_pack(v)[None], 1152)]
    outs = []
    for big, small in ((g_big, g_small), (d_big, d_small), (m_big, m_small), (v_big, v_small)):
        for name in w:
            if name == "conv_w":
                outs.append(lax.dynamic_slice(small[name], (0, 0, me * (CV_W // N_DEV)), (nl, 3, CV_W // N_DEV)))
            elif name in small:
                outs.append(small[name])
            else:
                outs.append(big[name])
    return (loss, dh[None], *outs)
```

```python
import functools

import jax
import jax.numpy as jnp
import numpy as np
from jax import lax
from jax.experimental import pallas as pl
from jax.experimental.pallas import tpu as pltpu

F32 = jnp.float32
BF16 = jnp.bfloat16

D = 1024
Q_RANK = 384
KV_RANK = 256
ROPE = 32
HEADS = 8
NOPE = 64
V_DIM = 64
HEAD_PAD = 128
SG_W = 256
CV_W = 256
CHUNK = 128
GROUP = 64
D_FF = 2816
IN_W = 1952
Z_W = 2048
Z_CV, Z_MLA, Z_SG = 0, 768, 1536
EPS = 1e-6
ROPE_THETA = 10000.0
SCALE = (NOPE + ROPE) ** -0.5
LOG2E = 1.4426950408889634
SCALE_LOG2E = SCALE * LOG2E
NEG = -1e30
N_DEV = 8

ADAM_LR, ADAM_B1, ADAM_B2, ADAM_EPS, ADAM_WD, ADAM_STEP = 0.001, 0.9, 0.999, 1e-08, 0.01, 10

VMEM_LIMIT = 56 * 1024 * 1024

TM = 512
TM_FFN = 256
FFN_SLAB = 256
HALO = 16
TQ = 512
FWD_UNROLL = 2
FWD_HEADS = 2
TT = 2048


def _cp(sem, vmem=VMEM_LIMIT):
    return pltpu.CompilerParams(dimension_semantics=sem, vmem_limit_bytes=vmem)


def _whole():
    return pl.BlockSpec(memory_space=pltpu.VMEM)


def _mm(a, b):
    return jnp.dot(a, b, preferred_element_type=F32)


def _mm_nt(a, b):
    return lax.dot_general(a, b, (((1,), (1,)), ((), ())), preferred_element_type=F32)


def _mm_tn(a, b):
    return lax.dot_general(a, b, (((0,), (0,)), ((), ())), preferred_element_type=F32)


def _rms_fwd(x, g):
    r = lax.rsqrt(jnp.mean(x * x, axis=-1, keepdims=True) + EPS)
    xh = x * r
    return xh * g, xh, r


def _rms_bwd(xh, r, g, dy):
    dxh = dy * g
    dx = r * (dxh - xh * jnp.mean(dxh * xh, axis=-1, keepdims=True))
    dg = jnp.sum(dy * xh, axis=0, keepdims=True)
    return dx, dg


def _gmean(v, gm):
    hi = v.astype(BF16)
    lo = (v - hi.astype(F32)).astype(BF16)
    return _mm(hi, gm) + _mm(lo, gm)


def _gelu(x):
    c = np.float32(np.sqrt(2.0 / np.pi))
    u = c * (x + 0.044715 * x * x * x)
    t = jnp.tanh(u)
    return 0.5 * x * (1.0 + t), t


def _gelu_grad(x, t):
    c = np.float32(np.sqrt(2.0 / np.pi))
    return 0.5 * (1.0 + t) + 0.5 * x * (1.0 - t * t) * c * (1.0 + 3.0 * 0.044715 * x * x)


def _rope(t, ca, sb, sc):
    return t * ca + pltpu.roll(t, HEAD_PAD - 16, 1) * sb + pltpu.roll(t, 16, 1) * sc


def _rope_t(dt, ca, sb, sc):
    return dt * ca + pltpu.roll(dt * sb, 16, 1) + pltpu.roll(dt * sc, HEAD_PAD - 16, 1)


def _shift_down(y, k, head):
    out = pltpu.roll(y, k, 0)
    row = lax.broadcasted_iota(jnp.int32, y.shape, 0)
    for j in range(k):
        out = jnp.where(row == j, head[head.shape[0] - k + j:head.shape[0] - k + j + 1, :], out)
    return out


def _shift_up(y, k, tail):
    n = y.shape[0]
    out = pltpu.roll(y, n - k, 0)
    row = lax.broadcasted_iota(jnp.int32, y.shape, 0)
    for j in range(k):
        out = jnp.where(row == n - k + j, tail[j:j + 1, :], out)
    return out


def rope_tables(pos, inv):
    t = pos.shape[0]
    tm = min(TM, t)

    def body(pos_ref, inv_ref, ca_ref, sb_ref, sc_ref):
        ang = pos_ref[...] * inv_ref[...]
        c = jnp.cos(ang)
        s = jnp.sin(ang)
        lane = lax.broadcasted_iota(jnp.int32, ang.shape, 1)
        ca_ref[...] = jnp.where(lane < NOPE, 1.0, jnp.where(lane < NOPE + ROPE, c, 0.0))
        sb_ref[...] = jnp.where((lane >= NOPE) & (lane < NOPE + 16), -s, 0.0)
        sc_ref[...] = jnp.where((lane >= NOPE + 16) & (lane < NOPE + ROPE), s, 0.0)

    out = jax.ShapeDtypeStruct((t, HEAD_PAD), F32)
    blk = pl.BlockSpec((tm, HEAD_PAD), lambda i: (i, 0))
    return pl.pallas_call(
        body, name="rope_tables", grid=(t // tm,),
        in_specs=[pl.BlockSpec((tm, 1), lambda i: (i, 0)), pl.BlockSpec((1, HEAD_PAD), lambda i: (0, 0))],
        out_specs=[blk, blk, blk], out_shape=[out, out, out],
        compiler_params=_cp(("parallel",)),
    )(pos, inv)


def pre_in_fwd(x, g, w):
    t = x.shape[0]
    tm = min(TM, t)

    def body(x_ref, g_ref, w_ref, z_ref):
        h, _, _ = _rms_fwd(x_ref[...], g_ref[...])
        z_ref[...] = _mm_nt(h.astype(BF16), w_ref[...]).astype(BF16)

    return pl.pallas_call(
        body, name="pre_in_fwd", grid=(t // tm,),
        in_specs=[pl.BlockSpec((tm, D), lambda i: (i, 0)), _whole(), _whole()],
        out_specs=pl.BlockSpec((tm, Z_W), lambda i: (i, 0)),
        out_shape=jax.ShapeDtypeStruct((t, Z_W), BF16),
        compiler_params=_cp(("parallel",)),
    )(x, g, w)


def mla_proj_fwd(z, ca, sb, sc, gq, gkv, wuq, wukv):
    t = z.shape[0]
    tm = min(TM, t)

    def body(z_ref, ca_ref, sb_ref, sc_ref, gq_ref, gkv_ref, wuq_ref, wukv_ref, q_ref, k_ref, v_ref):
        z = z_ref[...].astype(F32)
        ca, sb, sc = ca_ref[...], sb_ref[...], sc_ref[...]
        cq, _, _ = _rms_fwd(z[:, :Q_RANK], gq_ref[...])
        ckv, _, _ = _rms_fwd(z[:, Q_RANK:Q_RANK + KV_RANK], gkv_ref[...])
        q = _mm_nt(cq.astype(BF16), wuq_ref[...])
        kv = _mm_nt(ckv.astype(BF16), wukv_ref[...])
        kr = _rope(pltpu.roll(z[:, Q_RANK + KV_RANK:], NOPE, 1), ca, sb, sc)
        for h in range(HEADS):
            lanes = slice(h * HEAD_PAD, (h + 1) * HEAD_PAD)
            q_ref[:, lanes] = _rope(q[:, lanes], ca, sb, sc).astype(BF16)
            k_ref[:, lanes] = (kv[:, lanes] + kr).astype(BF16)
        v_ref[...] = kv[:, HEADS * HEAD_PAD:].astype(BF16)

    tab = pl.BlockSpec((tm, HEAD_PAD), lambda i: (i, 0))
    return pl.pallas_call(
        body, name="mla_proj_fwd", grid=(t // tm,),
        in_specs=[pl.BlockSpec((tm, 768), lambda i: (i, 1)), tab, tab, tab, _whole(), _whole(), _whole(), _whole()],
        out_specs=[pl.BlockSpec((tm, HEADS * HEAD_PAD), lambda i: (i, 0)),
                   pl.BlockSpec((tm, HEADS * HEAD_PAD), lambda i: (i, 0)),
                   pl.BlockSpec((tm, HEADS * V_DIM), lambda i: (i, 0))],
        out_shape=[jax.ShapeDtypeStruct((t, HEADS * HEAD_PAD), BF16),
                   jax.ShapeDtypeStruct((t, HEADS * HEAD_PAD), BF16),
                   jax.ShapeDtypeStruct((t, HEADS * V_DIM), BF16)],
        compiler_params=_cp(("parallel",)),
    )(z, ca, sb, sc, gq, gkv, wuq, wukv)


def _each(stages, k):
    def run():
        for stage in stages:
            stage[k]()
    return run


def attn_fwd(q, k, v, carried=()):
    t = q.shape[0]
    tq = min(TQ, t)
    nq = t // tq
    hs = FWD_HEADS
    last_pair = HEADS // hs - 1
    n = len(carried)

    def body(*refs):
        q_ref, k_ref, v_ref = refs[:3]
        o_ref, lse_ref = refs[3 + n:5 + n]
        sems = refs[5 + 2 * n:]
        stages = [_gather_steps(refs[3 + a], refs[5 + n + a], *sems[3 * a:3 * a + 3]) for a in range(n)]
        if n:
            pair = pl.program_id(0)
            pl.when((pair == 0) & (pl.program_id(1) == 0))(_each(stages, 0))
            pl.when((pair == last_pair) & (pl.program_id(1) == 0))(_each(stages, 1))
        i = pl.program_id(1)
        row = lax.broadcasted_iota(jnp.int32, (tq, tq), 0)
        col = lax.broadcasted_iota(jnp.int32, (tq, tq), 1)
        head_lanes = [slice(h * HEAD_PAD, (h + 1) * HEAD_PAD) for h in range(hs)]
        pair_lanes = [slice(p * 2 * V_DIM, (p + 1) * 2 * V_DIM) for p in range(hs // 2)]

        def step(j, carry, masked):
            start = pl.multiple_of(j * tq, tq)
            out = []
            for h in range(hs):
                m, l, acc = carry[h]
                s = _mm_nt(q_ref[:, head_lanes[h]], k_ref[pl.ds(start, tq), head_lanes[h]])
                if masked:
                    s = jnp.where(col <= row, s, NEG)
                m_new = jnp.maximum(m, jnp.max(s, axis=-1, keepdims=True))
                p = jnp.exp2((s - m_new) * SCALE_LOG2E)
                alpha = jnp.exp2((m - m_new) * SCALE_LOG2E)
                l = alpha * l + jnp.sum(p, axis=-1, keepdims=True)
                acc = alpha * acc + _mm(p.astype(BF16), v_ref[pl.ds(start, tq), pair_lanes[h // 2]])
                out.append((m_new, l, acc))
            return tuple(out)

        init = (jnp.full((tq, 1), NEG, F32), jnp.zeros((tq, 1), F32), jnp.zeros((tq, 2 * V_DIM), F32))
        def trip(j, c):
            for u in range(FWD_UNROLL):
                c = step(FWD_UNROLL * j + u, c, False)
            return c

        carry = lax.fori_loop(0, i // FWD_UNROLL, trip, (init,) * hs)
        carry = lax.fori_loop(i - i % FWD_UNROLL, i, lambda j, c: step(j, c, False), carry)
        outs = []
        for h, (m, l, acc) in enumerate(step(i, carry, True)):
            outs.append(acc / l)
            lse_ref[:, head_lanes[h]] = jnp.broadcast_to(m * SCALE + jnp.log(l), (tq, HEAD_PAD))
        lane = lax.broadcasted_iota(jnp.int32, (tq, 2 * V_DIM), 1)
        for p in range(hs // 2):
            o_ref[:, pair_lanes[p]] = jnp.where(lane < V_DIM, outs[2 * p], outs[2 * p + 1])
        if n:
            pl.when((pl.program_id(0) == last_pair) & (i == nq - 1))(_each(stages, 2))

    hbm = pl.BlockSpec(memory_space=pl.ANY)
    return pl.pallas_call(
        body, name=f"attn_fwd_gather{n}" if n else "attn_fwd", grid=(HEADS // hs, nq),
        in_specs=[pl.BlockSpec((tq, hs * HEAD_PAD), lambda p, i: (i, p)),
                  pl.BlockSpec((t, hs * HEAD_PAD), lambda p, i: (0, p)),
                  pl.BlockSpec((t, hs * V_DIM), lambda p, i: (0, p))] + [hbm] * n,
        out_specs=[pl.BlockSpec((tq, hs * V_DIM), lambda p, i: (i, p)),
                   pl.BlockSpec((tq, hs * HEAD_PAD), lambda p, i: (i, p))] + [hbm] * n,
        out_shape=[jax.ShapeDtypeStruct((t, HEADS * V_DIM), F32), jax.ShapeDtypeStruct((t, HEADS * HEAD_PAD), F32)]
        + [jax.ShapeDtypeStruct((N_DEV,) + c.shape, c.dtype) for c in carried],
        scratch_shapes=_comm_sems() * n,
        compiler_params=_cp(("arbitrary", "arbitrary") if n else ("parallel", "parallel")),
    )(q, k, v, *carried)


def _sgu_fwd(zsg, gm, lng, lnb, wc_ref, bias, mixed_ref):
    uv, th = _gelu(zsg)
    u, v0 = uv[:, :SG_W], uv[:, SG_W:]
    vc = v0 - _gmean(v0, gm)
    r = lax.rsqrt(_gmean(vc * vc, gm) + EPS)
    vh = vc * r
    v = vh * lng + lnb
    lane = lax.broadcasted_iota(jnp.int32, (CHUNK, SG_W), 1)
    for c in range(zsg.shape[0] // CHUNK):
        rows = slice(c * CHUNK, (c + 1) * CHUNK)
        vb = v[rows].astype(BF16)
        mixed = bias
        for g in range(SG_W // GROUP):
            mixed = mixed + jnp.where(lane // GROUP == g, _mm(wc_ref[g], vb), 0.0)
        mixed_ref[rows, :] = mixed
    return u, v, vh, r, th


def _conv_fwd(zcv, halo, first, cw):
    gb, gc, hh = zcv[:, :CV_W], zcv[:, CV_W:2 * CV_W], zcv[:, 2 * CV_W:]
    y = gc * hh
    yh = jnp.where(first, 0.0, halo[:, CV_W:2 * CV_W] * halo[:, 2 * CV_W:])
    y1 = _shift_down(y, 1, yh)
    y2 = _shift_down(y, 2, yh)
    conv = y2 * cw[0:1, :] + y1 * cw[1:2, :] + y * cw[2:3, :]
    return gb * conv, conv, y, y1, y2


def _tril_bf16(w_ref, g):
    row = lax.broadcasted_iota(jnp.int32, (CHUNK, CHUNK), 0)
    col = lax.broadcasted_iota(jnp.int32, (CHUNK, CHUNK), 1)
    return jnp.where(col <= row, w_ref[g], 0.0).astype(BF16)


def mix_fwd(x, z, ya, gm, lng, lnb, wsp, bias, cw, gout, wout, gpost):
    t = x.shape[0]
    tm = min(TM, t)

    def body(x_ref, zcv_ref, halo_ref, zsg_ref, ya_ref, gm_ref, lng_ref, lnb_ref, wsp_ref, bias_ref, cw_ref,
             gout_ref, wout_ref, gpost_ref, x1_ref, wc_ref, mixed_ref):
        i = pl.program_id(0)
        for g in range(SG_W // GROUP):
            wc_ref[g] = _tril_bf16(wsp_ref, g)
        u, _, _, _, _ = _sgu_fwd(zsg_ref[...].astype(F32), gm_ref[...], lng_ref[...], lnb_ref[...], wc_ref, bias_ref[...],
                                 mixed_ref)
        yb = u * mixed_ref[...]
        yc, _, _, _, _ = _conv_fwd(zcv_ref[...].astype(F32), halo_ref[...].astype(F32), i == 0, cw_ref[...])
        gout = gout_ref[...]
        na, _, _ = _rms_fwd(ya_ref[...], gout[:, :512])
        nb, _, _ = _rms_fwd(yb, gout[:, 512:768])
        nc, _, _ = _rms_fwd(yc, gout[:, 768:])
        mix = jnp.concatenate([na, nb, nc], axis=1).astype(BF16)
        o, _, _ = _rms_fwd(_mm(mix, wout_ref[...]), gpost_ref[...])
        x1_ref[...] = x_ref[...] + o

    return pl.pallas_call(
        body, name="mix_fwd", grid=(t // tm,),
        in_specs=[pl.BlockSpec((tm, D), lambda i: (i, 0)),
                  pl.BlockSpec((tm, 768), lambda i: (i, 0)),
                  pl.BlockSpec((HALO, 768), lambda i: (jnp.maximum(i * (tm // HALO) - 1, 0), 0)),
                  pl.BlockSpec((tm, 512), lambda i: (i, 3)),
                  pl.BlockSpec((tm, 512), lambda i: (i, 0)),
                  _whole(), _whole(), _whole(), _whole(), _whole(), _whole(), _whole(), _whole(), _whole()],
        out_specs=pl.BlockSpec((tm, D), lambda i: (i, 0)),
        out_shape=jax.ShapeDtypeStruct((t, D), F32),
        scratch_shapes=[pltpu.VMEM((SG_W // GROUP, CHUNK, CHUNK), BF16), pltpu.VMEM((tm, SG_W), F32)],
        compiler_params=_cp(("arbitrary",)),
    )(x, z, z, z, ya, gm, lng, lnb, wsp, bias, cw, gout, wout, gpost)


def _sigmoid(a):
    return 1.0 / (1.0 + jnp.exp(-a))


FFN_SHARD = D_FF // N_DEV


def _load_ffn_weights(g_ref, wgu_ref, wd_ref, sems):
    copies = []
    for j in range(N_DEV):
        for p, (dst, base) in enumerate(((wgu_ref, 0), (wgu_ref, D_FF), (wd_ref, 0))):
            copies.append(pltpu.make_async_copy(g_ref.at[j, pl.ds(p * FFN_SHARD, FFN_SHARD)],
                                                dst.at[pl.ds(base + j * FFN_SHARD, FFN_SHARD)], sems.at[3 * j + p]))
    for cp in copies:
        cp.start()
    for cp in copies:
        cp.wait()


def _ffn_weight_scratch():
    return [pltpu.VMEM((2 * D_FF, D), BF16), pltpu.VMEM((D_FF, D), BF16), pltpu.SemaphoreType.DMA((3 * N_DEV,))]


def ffn_fwd(x1, gpre, gathered, gpost):
    t = x1.shape[0]
    tm = min(TM_FFN, t)

    def body(x_ref, gpre_ref, g_ref, gpost_ref, x2_ref, f_ref, wgu_ref, wd_ref, sems):
        @pl.when(pl.program_id(0) == 0)
        def _():
            _load_ffn_weights(g_ref, wgu_ref, wd_ref, sems)

        x = x_ref[...]
        h, _, _ = _rms_fwd(x, gpre_ref[...])
        ab = _mm_nt(h.astype(BF16), wgu_ref[...])
        a, b = ab[:, :D_FF], ab[:, D_FF:]
        s = a * _sigmoid(a) * b
        f = _mm(s.astype(BF16), wd_ref[...])
        f_ref[...] = f
        x2_ref[...] = x + _rms_fwd(f, gpost_ref[...])[0]

    row = pl.BlockSpec((tm, D), lambda i: (i, 0))
    return pl.pallas_call(
        body, name="ffn_fwd", grid=(t // tm,),
        in_specs=[row, _whole(), pl.BlockSpec(memory_space=pl.ANY), _whole()],
        out_specs=[row, row],
        out_shape=[jax.ShapeDtypeStruct((t, D), F32), jax.ShapeDtypeStruct((t, D), F32)],
        scratch_shapes=_ffn_weight_scratch(),
        compiler_params=_cp(("arbitrary",)),
    )(x1, gpre, gathered, gpost)


def loss_head(y, target):
    t = y.shape[0]
    tm = min(TM, t)

    def body(y_ref, t_ref, loss_ref, dy_ref):
        @pl.when(pl.program_id(0) == 0)
        def _():
            loss_ref[...] = jnp.zeros_like(loss_ref)

        e = y_ref[...] - t_ref[...]
        dy_ref[...] = e * (1.0 / D)
        loss_ref[...] += jnp.sum(jnp.sum(e * e, axis=-1, keepdims=True), axis=0, keepdims=True)

    return pl.pallas_call(
        body, name="loss_head", grid=(t // tm,),
        in_specs=[pl.BlockSpec((tm, D), lambda i: (i, 0)), pl.BlockSpec((tm, D), lambda i: (i, 0))],
        out_specs=[pl.BlockSpec((1, 128), lambda i: (0, 0)), pl.BlockSpec((tm, D), lambda i: (i, 0))],
        out_shape=[jax.ShapeDtypeStruct((1, 128), F32), jax.ShapeDtypeStruct((t, D), F32)],
        compiler_params=_cp(("arbitrary",)),
    )(y, target)


def _acc(ref, first, val):
    @pl.when(first)
    def _():
        ref[...] = val

    @pl.when(jnp.logical_not(first))
    def _():
        ref[...] += val


def ffn_bwd(x1, f, dx2, gpre, gathered, gpost, carried=()):
    t = x1.shape[0]
    tm = min(TM_FFN, t)
    steps = t // tm
    n = len(carried)

    def body(*refs):
        x_ref, f_ref, dx2_ref, gpre_ref, g_ref, gpost_ref = refs[:6]
        dx1_ref, h_ref, dab_ref, s_ref, df_ref, dgpre_ref, dgpost_ref = refs[6 + n:13 + n]
        ab_ref, ds_ref, wgu_ref, wd_ref, sems = refs[13 + 2 * n:18 + 2 * n]
        comm_sems = refs[18 + 2 * n:]
        stages = [_exchange_steps(refs[6 + a], refs[13 + n + a], carried[a][1], *comm_sems[3 * a:3 * a + 3])
                  for a in range(n)]
        first = pl.program_id(0) == 0
        if n:
            pl.when(first)(_each(stages, 0))

        @pl.when(first)
        def _():
            _load_ffn_weights(g_ref, wgu_ref, wd_ref, sems)

        dx2 = dx2_ref[...]
        gpre, gpost = gpre_ref[...], gpost_ref[...]
        h, xh, rx = _rms_fwd(x_ref[...], gpre)
        h_ref[...] = h.astype(BF16)
        ab_ref[...] = _mm_nt(h_ref[...], wgu_ref[...])
        for c in range(0, D_FF, FFN_SLAB):
            a, b = ab_ref[:, c:c + FFN_SLAB], ab_ref[:, D_FF + c:D_FF + c + FFN_SLAB]
            s_ref[:, c:c + FFN_SLAB] = (a * _sigmoid(a) * b).astype(BF16)
        _, fh, rf = _rms_fwd(f_ref[...], gpost)
        df, dgpost = _rms_bwd(fh, rf, gpost, dx2)
        df_ref[...] = df.astype(BF16)
        ds_ref[...] = _mm_nt(df_ref[...], wd_ref[...])
        for c in range(0, D_FF, FFN_SLAB):
            a, b = ab_ref[:, c:c + FFN_SLAB], ab_ref[:, D_FF + c:D_FF + c + FFN_SLAB]
            ds = ds_ref[:, c:c + FFN_SLAB]
            sg = _sigmoid(a)
            dab_ref[:, c:c + FFN_SLAB] = (ds * b * (sg * (1.0 + a * (1.0 - sg)))).astype(BF16)
            dab_ref[:, D_FF + c:D_FF + c + FFN_SLAB] = (ds * (a * sg)).astype(BF16)
        dx, dgpre = _rms_bwd(xh, rx, gpre, _mm(dab_ref[...], wgu_ref[...]))
        dx1_ref[...] = dx2 + dx
        _acc(dgpre_ref, first, dgpre)
        _acc(dgpost_ref, first, dgpost)
        if n:
            pl.when(pl.program_id(0) == steps - 1)(_each(stages, 1))

    row = lambda w: pl.BlockSpec((tm, w), lambda i: (i, 0))
    vec = pl.BlockSpec((1, D), lambda i: (0, 0))
    hbm = pl.BlockSpec(memory_space=pl.ANY)
    return pl.pallas_call(
        body, name=f"ffn_bwd_exchange{n}" if n else "ffn_bwd", grid=(steps,),
        in_specs=[row(D), row(D), row(D), _whole(), hbm, _whole()] + [hbm] * n,
        out_specs=[row(D), row(D), row(2 * D_FF), row(D_FF), row(D), vec, vec] + [hbm] * n,
        out_shape=[jax.ShapeDtypeStruct((t, D), F32), jax.ShapeDtypeStruct((t, D), BF16),
                   jax.ShapeDtypeStruct((t, 2 * D_FF), BF16), jax.ShapeDtypeStruct((t, D_FF), BF16),
                   jax.ShapeDtypeStruct((t, D), BF16), jax.ShapeDtypeStruct((1, D), F32),
                   jax.ShapeDtypeStruct((1, D), F32)]
        + [jax.ShapeDtypeStruct(src.shape if scatter else (N_DEV,) + src.shape, src.dtype) for src, scatter in carried],
        scratch_shapes=[pltpu.VMEM((tm, 2 * D_FF), F32), pltpu.VMEM((tm, D_FF), F32)] + _ffn_weight_scratch()
        + _comm_sems() * n,
        compiler_params=_cp(("arbitrary",)),
    )(x1, f, dx2, gpre, gathered, gpost, *[src for src, _ in carried])


def atb(a, b, tk):
    t, k = a.shape
    n = b.shape[1]
    tt = min(TT, t)
    tk = min(tk, k)
    steps = t // tt

    def body(a_ref, b_ref, o_ref, acc_ref):
        i = pl.program_id(1)
        _acc(acc_ref, i == 0, _mm_tn(a_ref[...], b_ref[...]))

        @pl.when(i == steps - 1)
        def _():
            o_ref[...] = acc_ref[...].astype(BF16)

    return pl.pallas_call(
        body, name="atb", grid=(k // tk, steps),
        in_specs=[pl.BlockSpec((tt, tk), lambda j, i: (i, j)), pl.BlockSpec((tt, n), lambda j, i: (i, 0))],
        out_specs=pl.BlockSpec((tk, n), lambda j, i: (j, 0)),
        out_shape=jax.ShapeDtypeStruct((k, n), BF16),
        scratch_shapes=[pltpu.VMEM((tk, n), F32)],
        compiler_params=_cp(("parallel", "arbitrary")),
    )(a, b)


FFN_TILE_SHARDS = 4


def atb_ffn_chunks(a, b, first_piece, chunks=None):
    t, k = a.shape
    tt = min(TT, t)
    tk = FFN_TILE_SHARDS * FFN_SHARD
    steps = t // tt
    per_piece = N_DEV // FFN_TILE_SHARDS

    def body(*refs):
        a_ref, b_ref, o_ref, acc_ref = refs[0], refs[1], refs[-2], refs[-1]
        i = pl.program_id(1)
        _acc(acc_ref, i == 0, _mm_tn(a_ref[...], b_ref[...]))

        @pl.when(i == steps - 1)
        def _():
            for d in range(FFN_TILE_SHARDS):
                o_ref[d, 0] = acc_ref[d * FFN_SHARD:(d + 1) * FFN_SHARD, :].astype(BF16)

    hbm = pl.BlockSpec(memory_space=pl.ANY)
    return pl.pallas_call(
        body, name="atb_ffn_chunks", grid=(k // tk, steps),
        in_specs=[pl.BlockSpec((tt, tk), lambda j, i: (i, j)), pl.BlockSpec((tt, D), lambda j, i: (i, 0))]
        + ([] if chunks is None else [hbm]),
        out_specs=pl.BlockSpec((FFN_TILE_SHARDS, 1, FFN_SHARD, D),
                               lambda j, i: (j % per_piece, first_piece + j // per_piece, 0, 0)),
        out_shape=jax.ShapeDtypeStruct((N_DEV, len(FFN_PIECES), FFN_SHARD, D), BF16),
        input_output_aliases={} if chunks is None else {2: 0},
        scratch_shapes=[pltpu.VMEM((tk, D), F32)],
        compiler_params=_cp(("parallel", "arbitrary")),
    )(a, b, *([] if chunks is None else [chunks]))


def mix_bwd(dx1, z, ya, gm, lng, lnb, wsp, wspt, bias, cw, gout, wout, gpost):
    t = dx1.shape[0]
    tm = min(TM, t)
    ng = SG_W // GROUP

    def body(dx1_ref, zcv_ref, halo_ref, zsg_ref, ya_ref, gm_ref, lng_ref, lnb_ref, wsp_ref, wspt_ref, bias_ref,
             cw_ref, gout_ref, wout_ref, gpost_ref,
             dya_ref, dyc_ref, dzsg_ref, mix_ref, do_ref, dgpost_ref, dgout_ref, dlng_ref, dlnb_ref, dwsp_ref,
             dbias_ref, wc_ref, wct_ref, mixed_ref, dv_ref):
        i = pl.program_id(0)
        first = i == 0
        gm = gm_ref[...]
        for g in range(ng):
            wc_ref[g] = _tril_bf16(wsp_ref, g)
            wct_ref[g] = jnp.where(
                lax.broadcasted_iota(jnp.int32, (CHUNK, CHUNK), 0) <= lax.broadcasted_iota(jnp.int32, (CHUNK, CHUNK), 1),
                wspt_ref[g], 0.0).astype(BF16)
        zsg = zsg_ref[...].astype(F32)
        lng = lng_ref[...]
        u, v, vh, r, th = _sgu_fwd(zsg, gm, lng, lnb_ref[...], wc_ref, bias_ref[...], mixed_ref)
        mixed = mixed_ref[...]
        yb = u * mixed
        yc, _, _, _, _ = _conv_fwd(zcv_ref[...].astype(F32), halo_ref[...].astype(F32), first, cw_ref[...])
        gout, gpost = gout_ref[...], gpost_ref[...]
        ga, gb_, gc_ = gout[:, :512], gout[:, 512:768], gout[:, 768:]
        na, yah, ra = _rms_fwd(ya_ref[...], ga)
        nb, ybh, rb = _rms_fwd(yb, gb_)
        nc, ych, rc = _rms_fwd(yc, gc_)
        mix = jnp.concatenate([na, nb, nc], axis=1).astype(BF16)
        _, oh, ro = _rms_fwd(_mm(mix, wout_ref[...]), gpost)
        do, dgpost = _rms_bwd(oh, ro, gpost, dx1_ref[...])
        dob = do.astype(BF16)
        dmix = _mm_nt(dob, wout_ref[...])
        dya, dga = _rms_bwd(yah, ra, ga, dmix[:, :512])
        dyb, dgb = _rms_bwd(ybh, rb, gb_, dmix[:, 512:768])
        dyc, dgc = _rms_bwd(ych, rc, gc_, dmix[:, 768:])
        dya_ref[...] = dya
        dyc_ref[...] = dyc
        mix_ref[...] = mix
        do_ref[...] = dob
        _acc(dgpost_ref, first, dgpost)
        _acc(dgout_ref, first, jnp.concatenate([dga, dgb, dgc], axis=1))
        du = dyb * mixed
        dmixed = dyb * u
        lane = lax.broadcasted_iota(jnp.int32, (CHUNK, SG_W), 1)
        row = lax.broadcasted_iota(jnp.int32, (CHUNK, CHUNK), 0)
        col = lax.broadcasted_iota(jnp.int32, (CHUNK, CHUNK), 1)
        dbias = jnp.zeros((CHUNK, SG_W), F32)
        dw = [jnp.zeros((CHUNK, CHUNK), F32) for _ in range(ng)]
        for c in range(tm // CHUNK):
            rows = slice(c * CHUNK, (c + 1) * CHUNK)
            dm = dmixed[rows]
            dbias = dbias + dm
            dmb = dm.astype(BF16)
            vb = v[rows].astype(BF16)
            dvc = jnp.zeros((CHUNK, SG_W), F32)
            for g in range(ng):
                in_g = lane // GROUP == g
                dvc = dvc + jnp.where(in_g, _mm(wct_ref[g], dmb), 0.0)
                dw[g] = dw[g] + _mm_nt(jnp.where(in_g, dmb, jnp.zeros_like(dmb)), vb)
            dv_ref[rows, :] = dvc
        for g in range(ng):
            dwg = jnp.where(col <= row, dw[g], 0.0)

            @pl.when(first)
            def _():
                dwsp_ref[g] = dwg

            @pl.when(jnp.logical_not(first))
            def _():
                dwsp_ref[g] += dwg
        _acc(dbias_ref, first, _gmean(dbias, gm) * GROUP)
        dv = dv_ref[...]
        _acc(dlng_ref, first, jnp.sum(dv * vh, axis=0, keepdims=True))
        _acc(dlnb_ref, first, jnp.sum(dv, axis=0, keepdims=True))
        dvh = dv * lng
        dv0 = r * (dvh - _gmean(dvh, gm) - vh * _gmean(dvh * vh, gm))
        dzsg_ref[...] = (jnp.concatenate([du, dv0], axis=1) * _gelu_grad(zsg, th)).astype(BF16)

    row_ = lambda w: pl.BlockSpec((tm, w), lambda i: (i, 0))
    vec = lambda w: pl.BlockSpec((1, w), lambda i: (0, 0))
    return pl.pallas_call(
        body, name="mix_bwd", grid=(t // tm,),
        in_specs=[row_(D),
                  pl.BlockSpec((tm, 768), lambda i: (i, 0)),
                  pl.BlockSpec((HALO, 768), lambda i: (jnp.maximum(i * (tm // HALO) - 1, 0), 0)),
                  pl.BlockSpec((tm, 512), lambda i: (i, 3)),
                  row_(512),
                  _whole(), _whole(), _whole(), _whole(), _whole(), _whole(), _whole(), _whole(), _whole(), _whole()],
        out_specs=[row_(512), row_(CV_W), row_(512), row_(D), row_(D), vec(D), vec(D), vec(SG_W), vec(SG_W),
                   pl.BlockSpec((ng, CHUNK, CHUNK), lambda i: (0, 0, 0)),
                   pl.BlockSpec((CHUNK, SG_W), lambda i: (0, 0))],
        out_shape=[jax.ShapeDtypeStruct((t, 512), F32), jax.ShapeDtypeStruct((t, CV_W), F32),
                   jax.ShapeDtypeStruct((t, 512), BF16), jax.ShapeDtypeStruct((t, D), BF16),
                   jax.ShapeDtypeStruct((t, D), BF16), jax.ShapeDtypeStruct((1, D), F32),
                   jax.ShapeDtypeStruct((1, D), F32), jax.ShapeDtypeStruct((1, SG_W), F32),
                   jax.ShapeDtypeStruct((1, SG_W), F32), jax.ShapeDtypeStruct((ng, CHUNK, CHUNK), F32),
                   jax.ShapeDtypeStruct((CHUNK, SG_W), F32)],
        scratch_shapes=[pltpu.VMEM((ng, CHUNK, CHUNK), BF16), pltpu.VMEM((ng, CHUNK, CHUNK), BF16),
                        pltpu.VMEM((tm, SG_W), F32), pltpu.VMEM((tm, SG_W), F32)],
        compiler_params=_cp(("arbitrary",)),
    )(dx1, z, z, z, ya, gm, lng, lnb, wsp, wspt, bias, cw, gout, wout, gpost)


def conv_bwd(dyc, z, cw):
    t = dyc.shape[0]
    tm = min(TM, t)
    hb = tm // 8
    last_blk = t // 8 - 1

    def body(dyc_ref, dyct_ref, zcv_ref, head_ref, tail_ref, cw_ref, dz_ref, dcw_ref):
        i = pl.program_id(0)
        first = i == 0
        last = i == pl.num_programs(0) - 1
        cw = cw_ref[...]
        zcv = zcv_ref[...].astype(F32)
        gb, gc, hh = zcv[:, :CV_W], zcv[:, CV_W:2 * CV_W], zcv[:, 2 * CV_W:]
        _, conv, y, y1, y2 = _conv_fwd(zcv, head_ref[...].astype(F32), first, cw)
        dyc = dyc_ref[...]
        dconv = dyc * gb
        tail = jnp.where(last, 0.0, dyct_ref[...] * tail_ref[:8, :CV_W].astype(F32))
        d1 = _shift_up(dconv, 1, tail)
        d2 = _shift_up(dconv, 2, tail)
        dy = dconv * cw[2:3, :] + d1 * cw[1:2, :] + d2 * cw[0:1, :]
        dz_ref[...] = jnp.concatenate([dyc * conv, dy * hh, dy * gc], axis=1).astype(BF16)
        tap = lax.broadcasted_iota(jnp.int32, (8, CV_W), 0)
        dcw = jnp.where(tap == 0, jnp.sum(dconv * y2, axis=0, keepdims=True),
                        jnp.where(tap == 1, jnp.sum(dconv * y1, axis=0, keepdims=True),
                                  jnp.where(tap == 2, jnp.sum(dconv * y, axis=0, keepdims=True), 0.0)))
        _acc(dcw_ref, first, dcw)

    return pl.pallas_call(
        body, name="conv_bwd", grid=(t // tm,),
        in_specs=[pl.BlockSpec((tm, CV_W), lambda i: (i, 0)),
                  pl.BlockSpec((8, CV_W), lambda i: (jnp.minimum((i + 1) * hb, last_blk), 0)),
                  pl.BlockSpec((tm, 768), lambda i: (i, 0)),
                  pl.BlockSpec((HALO, 768), lambda i: (jnp.maximum(i * (tm // HALO) - 1, 0), 0)),
                  pl.BlockSpec((HALO, 768), lambda i: (jnp.minimum((i + 1) * (tm // HALO), t // HALO - 1), 0)),
                  _whole()],
        out_specs=[pl.BlockSpec((tm, 768), lambda i: (i, 0)), pl.BlockSpec((8, CV_W), lambda i: (0, 0))],
        out_shape=[jax.ShapeDtypeStruct((t, 768), BF16), jax.ShapeDtypeStruct((8, CV_W), F32)],
        compiler_params=_cp(("arbitrary",)),
    )(dyc, dyc, z, z, z, cw)


def attn_bwd(q, k, v, o, lse, do, carried=()):
    t = q.shape[0]
    tq = min(TQ, t)
    nq = t // tq
    last_pair = HEADS // 2 - 1
    n = len(carried)

    def body(*refs):
        j = pl.program_id(1)
        q_ref, k_ref, v_ref, o_ref, lse_ref, do_ref = refs[:6]
        dq_out_ref, dk_ref, dv_ref = refs[6 + n:9 + n]
        dq_ref = refs[9 + 2 * n]
        sems = refs[10 + 2 * n:]
        stages = [_exchange_steps(refs[6 + a], refs[9 + n + a], carried[a][1], *sems[3 * a:3 * a + 3]) for a in range(n)]
        if n:
            pl.when((pl.program_id(0) == 0) & (j == 0))(_each(stages, 0))

        @pl.when(j == 0)
        def _():
            dq_ref[...] = jnp.zeros_like(dq_ref)

        row = lax.broadcasted_iota(jnp.int32, (tq, tq), 0)
        col = lax.broadcasted_iota(jnp.int32, (tq, tq), 1)
        vlane = lax.broadcasted_iota(jnp.int32, (tq, 2 * V_DIM), 1)
        head_lanes = [slice(h * HEAD_PAD, (h + 1) * HEAD_PAD) for h in range(2)]

        def step(i, carry, masked):
            start = pl.multiple_of(i * tq, tq)
            do_blk = do_ref[pl.ds(start, tq), :]
            o_blk = o_ref[pl.ds(start, tq), :]
            vb = v_ref[...]
            dks, dv_acc = [], carry[2]
            for h in range(2):
                lanes = head_lanes[h]
                qb = q_ref[pl.ds(start, tq), lanes]
                kb = k_ref[:, lanes]
                dob = jnp.where((vlane // V_DIM) == h, do_blk, 0.0)
                delta = jnp.sum(dob * o_blk, axis=-1, keepdims=True)
                lse2 = lse_ref[pl.ds(start, tq), lanes][:, 0:1] * LOG2E
                s = _mm_nt(qb, kb)
                if masked:
                    s = jnp.where(col <= row, s, NEG)
                p = jnp.exp2(s * SCALE_LOG2E - lse2)
                dob16 = dob.astype(BF16)
                dp = _mm_nt(dob16, vb)
                ds = (p * (dp - delta) * SCALE).astype(BF16)
                dv_acc = dv_acc + _mm_tn(p.astype(BF16), dob16)
                dks.append(carry[h] + _mm_tn(ds, qb))
                dq_ref[pl.ds(start, tq), lanes] += _mm(ds, kb)
            return dks[0], dks[1], dv_acc

        zero = jnp.zeros((tq, HEAD_PAD), F32)
        carry = step(j, (zero, zero, jnp.zeros((tq, 2 * V_DIM), F32)), True)
        rest = nq - 1 - j
        carry = lax.fori_loop(0, rest // 2, lambda u, c: step(j + 2 + 2 * u, step(j + 1 + 2 * u, c, False), False), carry)
        dk0, dk1, dv_acc = lax.fori_loop(0, rest % 2, lambda _, c: step(nq - 1, c, False), carry)
        dk_ref[:, head_lanes[0]] = dk0.astype(BF16)
        dk_ref[:, head_lanes[1]] = dk1.astype(BF16)
        dv_ref[...] = dv_acc.astype(BF16)

        @pl.when(j == nq - 1)
        def _():
            dq_out_ref[...] = dq_ref[...].astype(BF16)

        if n:
            pl.when((pl.program_id(0) == last_pair) & (j == nq - 1))(_each(stages, 1))

    hbm = pl.BlockSpec(memory_space=pl.ANY)
    return pl.pallas_call(
        body, name=f"attn_bwd_exchange{n}" if n else "attn_bwd", grid=(HEADS // 2, nq),
        in_specs=[pl.BlockSpec((t, 2 * HEAD_PAD), lambda p, j: (0, p)),
                  pl.BlockSpec((tq, 2 * HEAD_PAD), lambda p, j: (j, p)),
                  pl.BlockSpec((tq, 2 * V_DIM), lambda p, j: (j, p)),
                  pl.BlockSpec((t, 2 * V_DIM), lambda p, j: (0, p)),
                  pl.BlockSpec((t, 2 * HEAD_PAD), lambda p, j: (0, p)),
                  pl.BlockSpec((t, 2 * V_DIM), lambda p, j: (0, p))] + [hbm] * n,
        out_specs=[pl.BlockSpec((t, 2 * HEAD_PAD), lambda p, j: (0, p)),
                   pl.BlockSpec((tq, 2 * HEAD_PAD), lambda p, j: (j, p)),
                   pl.BlockSpec((tq, 2 * V_DIM), lambda p, j: (j, p))] + [hbm] * n,
        out_shape=[jax.ShapeDtypeStruct((t, HEADS * HEAD_PAD), BF16), jax.ShapeDtypeStruct((t, HEADS * HEAD_PAD), BF16),
                   jax.ShapeDtypeStruct((t, HEADS * V_DIM), BF16)]
        + [jax.ShapeDtypeStruct(src.shape if scatter else (N_DEV,) + src.shape, src.dtype) for src, scatter in carried],
        scratch_shapes=[pltpu.VMEM((t, 2 * HEAD_PAD), F32)] + _comm_sems() * n,
        compiler_params=_cp(("arbitrary", "arbitrary") if n else ("parallel", "arbitrary")),
    )(q, k, v, o, lse, do, *[src for src, _ in carried])


def mla_proj_bwd(dq, dk, dv, z, ca, sb, sc, gq, gkv, wuq, wukv):
    t = z.shape[0]
    tm = min(TM, t)

    def body(dq_ref, dk_ref, dv_ref, z_ref, ca_ref, sb_ref, sc_ref, gq_ref, gkv_ref, wuq_ref, wukv_ref,
             dz_ref, cq_ref, ckv_ref, dqp_ref, dkvp_ref, dgq_ref, dgkv_ref):
        first = pl.program_id(0) == 0
        z = z_ref[...].astype(F32)
        ca, sb, sc = ca_ref[...], sb_ref[...], sc_ref[...]
        gq, gkv = gq_ref[...], gkv_ref[...]
        cq, cqh, rq = _rms_fwd(z[:, :Q_RANK], gq)
        ckv, ckvh, rkv = _rms_fwd(z[:, Q_RANK:Q_RANK + KV_RANK], gkv)
        lane = lax.broadcasted_iota(jnp.int32, (tm, HEAD_PAD), 1)
        dkr = jnp.zeros((tm, HEAD_PAD), F32)
        for h in range(HEADS):
            lanes = slice(h * HEAD_PAD, (h + 1) * HEAD_PAD)
            dqp_ref[:, lanes] = _rope_t(dq_ref[:, lanes].astype(F32), ca, sb, sc).astype(BF16)
            dkh = dk_ref[:, lanes].astype(F32)
            dkr = dkr + dkh
            dkvp_ref[:, lanes] = jnp.where(lane < NOPE, dkh, 0.0).astype(BF16)
        dkvp_ref[:, HEADS * HEAD_PAD:] = dv_ref[...].astype(BF16)
        dkr = pltpu.roll(_rope_t(jnp.where(lane >= NOPE, dkr, 0.0), ca, sb, sc), HEAD_PAD - NOPE, 1)
        dkr = jnp.where(lane < ROPE, dkr, 0.0)
        dcq = _mm(dqp_ref[...], wuq_ref[...])
        dckv = _mm(dkvp_ref[...], wukv_ref[...])
        dzq, dgq = _rms_bwd(cqh, rq, gq, dcq)
        dzkv, dgkv = _rms_bwd(ckvh, rkv, gkv, dckv)
        dz_ref[...] = jnp.concatenate([dzq, dzkv, dkr], axis=1).astype(BF16)
        cq_ref[...] = cq.astype(BF16)
        ckv_ref[...] = ckv.astype(BF16)
        _acc(dgq_ref, first, dgq)
        _acc(dgkv_ref, first, dgkv)

    row = lambda w: pl.BlockSpec((tm, w), lambda i: (i, 0))
    vec = lambda w: pl.BlockSpec((1, w), lambda i: (0, 0))
    return pl.pallas_call(
        body, name="mla_proj_bwd", grid=(t // tm,),
        in_specs=[row(1024), row(1024), row(512), pl.BlockSpec((tm, 768), lambda i: (i, 1)),
                  row(HEAD_PAD), row(HEAD_PAD), row(HEAD_PAD), _whole(), _whole(), _whole(), _whole()],
        out_specs=[row(768), row(Q_RANK), row(KV_RANK), row(1024), row(1536), vec(Q_RANK), vec(KV_RANK)],
        out_shape=[jax.ShapeDtypeStruct((t, 768), BF16), jax.ShapeDtypeStruct((t, Q_RANK), BF16),
                   jax.ShapeDtypeStruct((t, KV_RANK), BF16), jax.ShapeDtypeStruct((t, 1024), BF16),
                   jax.ShapeDtypeStruct((t, 1536), BF16), jax.ShapeDtypeStruct((1, Q_RANK), F32),
                   jax.ShapeDtypeStruct((1, KV_RANK), F32)],
        compiler_params=_cp(("arbitrary",)),
    )(dq, dk, dv, z, ca, sb, sc, gq, gkv, wuq, wukv)


def pre_in_bwd(x, dx1, dzcv, dzmla, dzsg, g, w):
    t = x.shape[0]
    tm = min(TM, t)

    def body(x_ref, dx1_ref, dzcv_ref, dzmla_ref, dzsg_ref, g_ref, w_ref, dx_ref, h_ref, dz_ref, dg_ref):
        g = g_ref[...]
        h, xh, r = _rms_fwd(x_ref[...], g)
        dz = jnp.concatenate([dzcv_ref[...], dzmla_ref[...], dzsg_ref[...]], axis=1)
        dx, dg = _rms_bwd(xh, r, g, _mm(dz, w_ref[...]))
        dx_ref[...] = dx1_ref[...] + dx
        h_ref[...] = h.astype(BF16)
        dz_ref[...] = dz
        _acc(dg_ref, pl.program_id(0) == 0, dg)

    row = lambda w_: pl.BlockSpec((tm, w_), lambda i: (i, 0))
    return pl.pallas_call(
        body, name="pre_in_bwd", grid=(t // tm,),
        in_specs=[row(D), row(D), row(768), row(768), row(512), _whole(), _whole()],
        out_specs=[row(D), row(D), row(Z_W), pl.BlockSpec((1, D), lambda i: (0, 0))],
        out_shape=[jax.ShapeDtypeStruct((t, D), F32), jax.ShapeDtypeStruct((t, D), BF16),
                   jax.ShapeDtypeStruct((t, Z_W), BF16), jax.ShapeDtypeStruct((1, D), F32)],
        compiler_params=_cp(("arbitrary",)),
    )(x, dx1, dzcv, dzmla, dzsg, g, w)


MESH = pl.DeviceIdType.MESH


def _place():
    return lax.axis_index("x"), lax.axis_index("y"), lax.axis_index("c")


def _comm_sems():
    return [pltpu.SemaphoreType.DMA((7,)), pltpu.SemaphoreType.DMA((7,)), pltpu.SemaphoreType.DMA]


def _gather_steps(x_ref, out_ref, send_sems, recv_sems, local_sem):
    x, y, c = _place()
    me, sibling = (x, y, c), (x, y, 1 - c)
    chips = [(1 - x, y), (x, 1 - y), (1 - x, 1 - y)]

    def slot(px, py, pc):
        return out_ref.at[4 * px + 2 * py + pc]

    def copy(k, blk, to, src=None):
        return pltpu.make_async_remote_copy(
            src_ref=slot(*blk) if src is None else src, dst_ref=slot(*blk),
            send_sem=send_sems.at[k], recv_sem=recv_sems.at[k], device_id=to, device_id_type=MESH)

    mine = pltpu.make_async_copy(x_ref, slot(*me), local_sem)
    first = [copy(0, me, sibling, src=x_ref)] + [copy(1 + j, me, (*chip, c), src=x_ref) for j, chip in enumerate(chips)]
    passed = [copy(4 + j, (*chip, c), sibling) for j, chip in enumerate(chips)]

    def start():
        mine.start()
        for cp in first:
            cp.start()

    def forward():
        for j, chip in enumerate(chips):
            copy(1 + j, (*chip, c), me).wait_recv()
            passed[j].start()

    def finish():
        copy(0, sibling, me).wait_recv()
        for j, chip in enumerate(chips):
            copy(4 + j, (*chip, 1 - c), me).wait_recv()
        for cp in first + passed:
            cp.wait_send()
        mine.wait()

    return start, forward, finish


def _exchange_steps(src_ref, out_ref, scatter, send_sems, recv_sems, local_sem):
    x, y, c = _place()
    me = 4 * x + 2 * y + c
    own = pltpu.make_async_copy(src_ref.at[me] if scatter else src_ref, out_ref.at[me], local_sem)
    copies = []
    for k in range(1, N_DEV):
        px = 1 - x if k & 4 else x
        py = 1 - y if k & 2 else y
        pc = 1 - c if k & 1 else c
        copies.append(pltpu.make_async_remote_copy(
            src_ref=src_ref.at[4 * px + 2 * py + pc] if scatter else src_ref, dst_ref=out_ref.at[me],
            send_sem=send_sems.at[k - 1], recv_sem=recv_sems.at[k - 1], device_id=(px, py, pc), device_id_type=MESH))

    def start():
        own.start()
        for cp in copies:
            cp.start()

    def finish():
        for cp in copies:
            cp.wait_recv()
        for cp in copies:
            cp.wait_send()
        own.wait()

    return start, finish


def all_gather(block):
    def body(x_ref, out_ref, *sems):
        for stage in _gather_steps(x_ref, out_ref, *sems):
            stage()

    return pl.pallas_call(
        body, name="all_gather",
        in_specs=[pl.BlockSpec(memory_space=pl.ANY)],
        out_specs=pl.BlockSpec(memory_space=pl.ANY),
        out_shape=jax.ShapeDtypeStruct((N_DEV,) + block.shape, block.dtype),
        scratch_shapes=_comm_sems(),
    )(block)


def _row_tile(r, cap):
    return max(d for d in range(16, cap + 1, 16) if r % d == 0)


def sum_adamw(parts, w, m, v, cap, carried=()):
    nl, r, c = w.shape
    tr = _row_tile(r, cap)
    steps = r // tr
    n = len(carried)
    c1 = 1.0 / (1.0 - ADAM_B1 ** ADAM_STEP)
    c2 = 1.0 / (1.0 - ADAM_B2 ** ADAM_STEP)

    def body(*refs):
        p_refs = refs[:nl]
        w_ref, m_ref, v_ref = refs[nl:nl + 3]
        g_ref, d_ref, nm_ref, nv_ref = refs[nl + 3 + n:nl + 7 + n]
        sems = refs[nl + 7 + 2 * n:]
        stages = [_exchange_steps(refs[nl + 3 + a], refs[nl + 7 + n + a], carried[a][1], *sems[3 * a:3 * a + 3])
                  for a in range(n)]
        layer, i = pl.program_id(0), pl.program_id(1)
        if n:
            pl.when((layer == 0) & (i == 0))(_each(stages, 0))

        def update(p_ref):
            g = p_ref[0].astype(F32)
            for k in range(1, N_DEV):
                g = g + p_ref[k].astype(F32)
            m_new = ADAM_B1 * m_ref[...] + (1.0 - ADAM_B1) * g
            v_new = ADAM_B2 * v_ref[...] + (1.0 - ADAM_B2) * (g * g)
            g_ref[...] = g
            nm_ref[...] = m_new
            nv_ref[...] = v_new
            d_ref[...] = -ADAM_LR * ((m_new * c1) / (jnp.sqrt(v_new * c2) + ADAM_EPS) + ADAM_WD * w_ref[...])

        for k in range(nl):
            pl.when(layer == k)(functools.partial(update, p_refs[k]))
        if n:
            pl.when((layer == nl - 1) & (i == steps - 1))(_each(stages, 1))

    def parts_spec(k):
        return pl.BlockSpec((N_DEV, tr, c), lambda l, i: (0, jnp.where(l == k, i, jnp.where(l < k, 0, steps - 1)), 0))

    blk = pl.BlockSpec((None, tr, c), lambda l, i: (l, i, 0))
    out = jax.ShapeDtypeStruct((nl, r, c), F32)
    hbm = pl.BlockSpec(memory_space=pl.ANY)
    return pl.pallas_call(
        body, name=f"sum_adamw_exchange{n}" if n else "sum_adamw", grid=(nl, steps),
        in_specs=[parts_spec(k) for k in range(nl)] + [blk, blk, blk] + [hbm] * n,
        out_specs=[blk, blk, blk, blk] + [hbm] * n,
        out_shape=[out, out, out, out]
        + [jax.ShapeDtypeStruct(src.shape if scatter else (N_DEV,) + src.shape, src.dtype) for src, scatter in carried],
        scratch_shapes=_comm_sems() * n,
        compiler_params=_cp(("arbitrary", "arbitrary")),
    )(*parts, w, m, v, *[src for src, _ in carried])


PACK_W = 1024
MIX_PIECES = (("w_out", D // N_DEV, D, False), ("w_uq", HEADS * (NOPE + ROPE) // N_DEV, Q_RANK, True),
              ("w_ukv", HEADS * (NOPE + V_DIM) // N_DEV, KV_RANK, True), ("conv", 16, PACK_W, False),
              ("w_in", IN_W // N_DEV, D, True))
FFN_PIECES = (("w_gate", D_FF // N_DEV, D, True), ("w_up", D_FF // N_DEV, D, True), ("w_down", D_FF // N_DEV, D, False))
OFFSET = {}
for _pieces in (MIX_PIECES, FFN_PIECES):
    _off = 0
    for _name, _rows, _, _ in _pieces:
        OFFSET[_name] = _off
        _off += _rows + -_rows % 16
assert all(o % 16 == 0 for o in OFFSET.values())
assert [OFFSET[n] for n in ("w_gate", "w_up", "w_down")] == [0, FFN_SHARD, 2 * FFN_SHARD]
CONV_BITS = 3 * (CV_W // N_DEV) * 2


def _to_pack(shards, dtype, pieces, conv=None):
    nl = shards["w_in"].shape[0]
    parts = []
    for name, rows, cols, transposed in pieces:
        if name == "conv":
            if conv is None:
                a = jnp.zeros((nl, rows, PACK_W), dtype)
            else:
                bits = lax.bitcast_convert_type(conv.astype(F32), BF16).reshape(nl, CONV_BITS)
                a = jnp.pad(bits, ((0, 0), (0, rows * PACK_W - CONV_BITS))).reshape(nl, rows, PACK_W)
        else:
            a = shards[name].astype(dtype)
            a = jnp.swapaxes(a, 1, 2) if transposed else a
            a = jnp.pad(a, ((0, 0), (0, -rows % 16), (0, PACK_W - cols)))
        parts.append(a)
    return jnp.concatenate(parts, axis=1)


def _from_pack(pack, pieces):
    out = {}
    for name, rows, cols, transposed in pieces:
        if name != "conv":
            a = pack[:, OFFSET[name]:OFFSET[name] + rows, :cols]
            out[name] = jnp.swapaxes(a, 1, 2) if transposed else a
    return out


def _mix_weights(g):
    def rows(name):
        _, n, cols, _ = next(p for p in MIX_PIECES if p[0] == name)
        return g[:, OFFSET[name]:OFFSET[name] + n, :cols]

    w_in_t = rows("w_in").reshape(IN_W, D)
    w_in_p = jnp.concatenate([w_in_t[1184:], w_in_t[:672], jnp.zeros((96, D), BF16), w_in_t[672:1184]], axis=0)
    w_uq_p = jnp.pad(rows("w_uq"), ((0, 0), (0, HEAD_PAD - NOPE - ROPE), (0, 0))).reshape(HEADS * HEAD_PAD, Q_RANK)
    kv = rows("w_ukv")
    w_k = jnp.pad(kv[:, :NOPE], ((0, 0), (0, HEAD_PAD - NOPE), (0, 0))).reshape(HEADS * HEAD_PAD, KV_RANK)
    w_ukv_p = jnp.concatenate([w_k, kv[:, NOPE:].reshape(HEADS * V_DIM, KV_RANK)], axis=0)
    bits = rows("conv").reshape(N_DEV, -1)[:, :CONV_BITS].reshape(N_DEV, 3, CV_W // N_DEV, 2)
    conv_w = jnp.moveaxis(lax.bitcast_convert_type(bits, F32), 0, 1).reshape(3, CV_W)
    return dict(w_in=w_in_p, w_uq=w_uq_p, w_ukv=w_ukv_p, w_out=rows("w_out").reshape(D, D), conv_w=conv_w)


def _grad_chunks(full):
    d_in = full["w_in"]
    d_in = jnp.concatenate([d_in[768:768 + 672], d_in[1536:], d_in[:768]], axis=0)
    d_uq = full["w_uq"].reshape(HEADS, HEAD_PAD, Q_RANK)[:, :NOPE + ROPE]
    d_k = full["w_ukv"][:HEADS * HEAD_PAD].reshape(HEADS, HEAD_PAD, KV_RANK)[:, :NOPE]
    d_v = full["w_ukv"][HEADS * HEAD_PAD:].reshape(HEADS, V_DIM, KV_RANK)
    mats = dict(w_in=d_in, w_uq=d_uq, w_ukv=jnp.concatenate([d_k, d_v], axis=1), w_out=full["w_out"])
    parts = []
    for name, rows, cols, _ in MIX_PIECES:
        if name == "conv":
            parts.append(jnp.zeros((N_DEV, rows, PACK_W), BF16))
        else:
            parts.append(jnp.pad(mats[name].reshape(N_DEV, rows, cols), ((0, 0), (0, -rows % 16), (0, PACK_W - cols))))
    return jnp.concatenate(parts, axis=1)


SMALL = (("mix_pre_g", (D,)), ("mix_post_g", (D,)), ("ffn_pre_g", (D,)), ("ffn_post_g", (D,)), ("q_norm_g", (Q_RANK,)),
         ("kv_norm_g", (KV_RANK,)), ("sg_ln_g", (SG_W,)), ("sg_ln_b", (SG_W,)), ("w_sp", (4, CHUNK, CHUNK)),
         ("b_sp", (4, CHUNK)), ("out_norm_g", (D,)))
SMALL_ROWS = 576


def _pack_small(vals, nl):
    flat = jnp.concatenate([vals[name].reshape(nl, -1) for name, _ in SMALL] + [vals["conv_w"].reshape(nl, -1)], axis=1)
    return jnp.pad(flat, ((0, 0), (0, SMALL_ROWS * 128 - flat.shape[1]))).reshape(nl * SMALL_ROWS, 128)


def _unpack_small(pack, nl):
    flat = pack.reshape(nl, SMALL_ROWS * 128)
    out, off = {}, 0
    for name, shape in SMALL + (("conv_w", (3, CV_W)),):
        n = int(np.prod(shape))
        out[name] = flat[:, off:off + n].reshape((nl,) + shape)
        off += n
    return out


def _layer_fwd(x, lw, sp, tabs, consts, next_pack):
    ca, sb, sc = tabs
    z = pre_in_fwd(x, sp["mix_pre_g"], lw["w_in"])
    q, k, v = mla_proj_fwd(z, ca, sb, sc, sp["q_norm_g"], sp["kv_norm_g"], lw["w_uq"], lw["w_ukv"])
    ya, lse, ffn_gathered, *mix_gathered = attn_fwd(q, k, v, next_pack)
    lw["ffn"] = ffn_gathered
    x1 = mix_fwd(x, z, ya, consts["gm"], sp["sg_ln_g"], sp["sg_ln_b"], sp["w_sp"], sp["bias"], lw["conv_w"],
                 sp["out_norm_g"], lw["w_out"], sp["mix_post_g"])
    x2, f = ffn_fwd(x1, sp["ffn_pre_g"], lw["ffn"], sp["ffn_post_g"])
    return x2, (x, z, q, k, v, ya, lse, x1, f), mix_gathered


def _layer_bwd(dx2, saved, lw, sp, tabs, consts, pending):
    ca, sb, sc = tabs
    x, z, q, k, v, ya, lse, x1, f = saved
    dx1, h2, dab, s, df, d_ffn_pre, d_ffn_post, *received = ffn_bwd(x1, f, dx2, sp["ffn_pre_g"], lw["ffn"], sp["ffn_post_g"],
                                                                    pending)
    ffn_chunks = atb_ffn_chunks(s, df, 2, atb_ffn_chunks(dab, h2, 0)).reshape(N_DEV, len(FFN_PIECES) * FFN_SHARD, D)
    dya, dyc, dzsg, mix, do, d_mix_post, d_out_norm, d_lng, d_lnb, d_wsp, d_bias = mix_bwd(
        dx1, z, ya, consts["gm"], sp["sg_ln_g"], sp["sg_ln_b"], sp["w_sp"], sp["w_sp_t"], sp["bias"], lw["conv_w"],
        sp["out_norm_g"], lw["w_out"], sp["mix_post_g"])
    d_w_out = atb(mix, do, 1024)
    dzcv, d_cw = conv_bwd(dyc, z, lw["conv_w"])
    dq, dk, dv, got_ffn = attn_bwd(q, k, v, ya, lse, dya, ((ffn_chunks, True),))
    dzmla, cq, ckv, dqp, dkvp, d_gq, d_gkv = mla_proj_bwd(dq, dk, dv, z, ca, sb, sc, sp["q_norm_g"], sp["kv_norm_g"],
                                                          lw["w_uq"], lw["w_ukv"])
    d_w_uq = atb(dqp, cq, 1024)
    d_w_ukv = atb(dkvp, ckv, 1536)
    dx, h1, dz, d_mix_pre = pre_in_bwd(x, dx1, dzcv, dzmla, dzsg, sp["mix_pre_g"], lw["w_in"])
    d_w_in = atb(dz, h1, 2048)
    mix_chunks = _grad_chunks(dict(w_in=d_w_in, w_uq=d_w_uq, w_ukv=d_w_ukv, w_out=d_w_out))
    d_bsp = d_bias[:, ::GROUP].T
    small = dict(mix_pre_g=d_mix_pre[0], mix_post_g=d_mix_post[0], ffn_pre_g=d_ffn_pre[0], ffn_post_g=d_ffn_post[0],
                 q_norm_g=d_gq[0], kv_norm_g=d_gkv[0], sg_ln_g=d_lng[0], sg_ln_b=d_lnb[0], w_sp=d_wsp, b_sp=d_bsp,
                 out_norm_g=d_out_norm[0], conv_w=d_cw[:3])
    small_pack = _pack_small({name: a[None] for name, a in small.items()}, 1)
    return dx, ((mix_chunks, True), (small_pack, False)), [got_ffn] + received


def kernel(x, positions, mix_pre_g, mix_post_g, ffn_pre_g, ffn_post_g, w_in, q_norm_g, w_uq, kv_norm_g, w_ukv, sg_ln_g, sg_ln_b, w_sp, b_sp, conv_w, out_norm_g, w_out, w_gate, w_up, w_down, loss_target, m_mix_pre_g, m_mix_post_g, m_ffn_pre_g, m_ffn_post_g, m_w_in, m_q_norm_g, m_w_uq, m_kv_norm_g, m_w_ukv, m_sg_ln_g, m_sg_ln_b, m_w_sp, m_b_sp, m_conv_w, m_out_norm_g, m_w_out, m_w_gate, m_w_up, m_w_down, v_mix_pre_g, v_mix_post_g, v_ffn_pre_g, v_ffn_post_g, v_w_in, v_q_norm_g, v_w_uq, v_kv_norm_g, v_w_ukv, v_sg_ln_g, v_sg_ln_b, v_w_sp, v_b_sp, v_conv_w, v_out_norm_g, v_w_out, v_w_gate, v_w_up, v_w_down):
    nl = w_in.shape[0]
    t = x.shape[1]
    w = dict(mix_pre_g=mix_pre_g, mix_post_g=mix_post_g, ffn_pre_g=ffn_pre_g, ffn_post_g=ffn_post_g, w_in=w_in,
             q_norm_g=q_norm_g, w_uq=w_uq, kv_norm_g=kv_norm_g, w_ukv=w_ukv, sg_ln_g=sg_ln_g, sg_ln_b=sg_ln_b, w_sp=w_sp,
             b_sp=b_sp, conv_w=conv_w, out_norm_g=out_norm_g, w_out=w_out, w_gate=w_gate, w_up=w_up, w_down=w_down)
    m = dict(mix_pre_g=m_mix_pre_g, mix_post_g=m_mix_post_g, ffn_pre_g=m_ffn_pre_g, ffn_post_g=m_ffn_post_g, w_in=m_w_in,
             q_norm_g=m_q_norm_g, w_uq=m_w_uq, kv_norm_g=m_kv_norm_g, w_ukv=m_w_ukv, sg_ln_g=m_sg_ln_g, sg_ln_b=m_sg_ln_b,
             w_sp=m_w_sp, b_sp=m_b_sp, conv_w=m_conv_w, out_norm_g=m_out_norm_g, w_out=m_w_out, w_gate=m_w_gate,
             w_up=m_w_up, w_down=m_w_down)
    v = dict(mix_pre_g=v_mix_pre_g, mix_post_g=v_mix_post_g, ffn_pre_g=v_ffn_pre_g, ffn_post_g=v_ffn_post_g, w_in=v_w_in,
             q_norm_g=v_q_norm_g, w_uq=v_w_uq, kv_norm_g=v_kv_norm_g, w_ukv=v_w_ukv, sg_ln_g=v_sg_ln_g, sg_ln_b=v_sg_ln_b,
             w_sp=v_w_sp, b_sp=v_b_sp, conv_w=v_conv_w, out_norm_g=v_out_norm_g, w_out=v_w_out, w_gate=v_w_gate,
             w_up=v_w_up, w_down=v_w_down)

    mix_pack = _to_pack(w, BF16, MIX_PIECES, conv=w["conv_w"])
    ffn_pack = _to_pack(w, BF16, FFN_PIECES)
    consts = dict(gm=jnp.asarray(np.kron(np.eye(SG_W // GROUP), np.full((GROUP, GROUP), 1.0 / GROUP)), BF16))
    smalls = []
    for l in range(nl):
        sp = {name: w[name][l].reshape(1, -1) for name, shape in SMALL if len(shape) == 1}
        sp["w_sp"] = w["w_sp"][l]
        sp["w_sp_t"] = jnp.swapaxes(w["w_sp"][l], 1, 2)
        sp["bias"] = jnp.repeat(w["b_sp"][l].T, GROUP, axis=1)
        smalls.append(sp)
    inv_freq = 1.0 / (ROPE_THETA ** (jnp.arange(0, ROPE // 2, dtype=F32) / (ROPE // 2)))
    inv = jnp.zeros((1, HEAD_PAD), F32).at[0, NOPE:NOPE + ROPE].set(jnp.concatenate([inv_freq, inv_freq]))
    tabs = rope_tables(positions.reshape(t, 1).astype(F32), inv)

    h = x[0]
    saved, layers = [], []
    mix_gathered = [all_gather(mix_pack[0])]
    for l in range(nl):
        layers.append(_mix_weights(mix_gathered[0]))
        carried = (ffn_pack[l],) + ((mix_pack[l + 1],) if l + 1 < nl else ())
        h, s, mix_gathered = _layer_fwd(h, layers[l], smalls[l], tabs, consts, carried)
        saved.append(s)
    sq, dh = loss_head(h, loss_target[0])
    loss = lax.psum(0.5 * sq[0, 0] / D, ("x", "y", "c"))

    got_ffn, got_mix, got_small = [None] * nl, [None] * nl, [None] * nl
    pending = ()
    for l in reversed(range(nl)):
        dh, new_pending, received = _layer_bwd(dh, saved[l], layers[l], smalls[l], tabs, consts, pending)
        got_ffn[l] = received[0]
        if pending:
            got_mix[l + 1], got_small[l + 1] = received[1:]
        pending = new_pending

    me = 4 * lax.axis_index("x") + 2 * lax.axis_index("y") + lax.axis_index("c")
    *ffn_new, got_mix[0], got_small[0] = sum_adamw(got_ffn, *[_to_pack(d, F32, FFN_PIECES) for d in (w, m, v)], 176,
                                                   carried=pending)
    mix_new = sum_adamw(got_mix, *[_to_pack(d, F32, MIX_PIECES) for d in (w, m, v)], 208)
    got_small = jnp.concatenate(got_small, axis=1)
    g_big, d_big, m_big, v_big = [{**_from_pack(a, FFN_PIECES), **_from_pack(b, MIX_PIECES)}
                                  for a, b in zip(ffn_new, mix_new)]

    def full_conv(a):
        return lax.dynamic_update_slice(jnp.zeros((nl, 3, CV_W), F32), a, (0, 0, me * (CV_W // N_DEV)))

    def small_pack(d):
        return _pack_small({**{name: d[name] for name, _ in SMALL}, "conv_w": full_conv(d["conv_w"])}, nl)

    g_small, d_small, m_small, v_small = [_unpack_small(p[0], nl) for p in
                                          sum_adamw([got_small], small_pack(w)[None], small_pack(m)[None],
                                                    small_pack(v)[None], 1152)]
    outs = []
    for big, small in ((g_big, g_small), (d_big, d_small), (m_big, m_small), (v_big, v_small)):
        for name in w:
            if name == "conv_w":
                outs.append(lax.dynamic_slice(small[name], (0, 0, me * (CV_W // N_DEV)), (nl, 3, CV_W // N_DEV)))
            elif name in small:
                outs.append(small[name])
            else:
                outs.append(big[name])
    return (loss, dh[None], *outs)
```

```python
import functools

import jax
import jax.numpy as jnp
import numpy as np
from jax import lax
from jax.experimental import pallas as pl
from jax.experimental.pallas import tpu as pltpu

F32 = jnp.float32
BF16 = jnp.bfloat16

D = 1024
Q_RANK = 384
KV_RANK = 256
ROPE = 32
HEADS = 8
NOPE = 64
V_DIM = 64
HEAD_PAD = 128
SG_W = 256
CV_W = 256
CHUNK = 128
GROUP = 64
D_FF = 2816
IN_W = 1952
Z_W = 2048
Z_CV, Z_MLA, Z_SG = 0, 768, 1536
EPS = 1e-6
ROPE_THETA = 10000.0
SCALE = (NOPE + ROPE) ** -0.5
LOG2E = 1.4426950408889634
SCALE_LOG2E = SCALE * LOG2E
NEG = -1e30
N_DEV = 8

ADAM_LR, ADAM_B1, ADAM_B2, ADAM_EPS, ADAM_WD, ADAM_STEP = 0.001, 0.9, 0.999, 1e-08, 0.01, 10

VMEM_LIMIT = 56 * 1024 * 1024

TM = 512
TM_FFN = 256
FFN_SLAB = 256
HALO = 16
TQ = 512
FWD_UNROLL = 2
FWD_HEADS = 2
TT = 2048


def _cp(sem, vmem=VMEM_LIMIT):
    return pltpu.CompilerParams(dimension_semantics=sem, vmem_limit_bytes=vmem)


def _whole():
    return pl.BlockSpec(memory_space=pltpu.VMEM)


def _mm(a, b):
    return jnp.dot(a, b, preferred_element_type=F32)


def _mm_nt(a, b):
    return lax.dot_general(a, b, (((1,), (1,)), ((), ())), preferred_element_type=F32)


def _mm_tn(a, b):
    return lax.dot_general(a, b, (((0,), (0,)), ((), ())), preferred_element_type=F32)


def _rms_fwd(x, g):
    r = lax.rsqrt(jnp.mean(x * x, axis=-1, keepdims=True) + EPS)
    xh = x * r
    return xh * g, xh, r


def _rms_bwd(xh, r, g, dy):
    dxh = dy * g
    dx = r * (dxh - xh * jnp.mean(dxh * xh, axis=-1, keepdims=True))
    dg = jnp.sum(dy * xh, axis=0, keepdims=True)
    return dx, dg


def _gmean(v, gm):
    hi = v.astype(BF16)
    lo = (v - hi.astype(F32)).astype(BF16)
    return _mm(hi, gm) + _mm(lo, gm)


def _gelu(x):
    c = np.float32(np.sqrt(2.0 / np.pi))
    u = c * (x + 0.044715 * x * x * x)
    t = jnp.tanh(u)
    return 0.5 * x * (1.0 + t), t


def _gelu_grad(x, t):
    c = np.float32(np.sqrt(2.0 / np.pi))
    return 0.5 * (1.0 + t) + 0.5 * x * (1.0 - t * t) * c * (1.0 + 3.0 * 0.044715 * x * x)


def _rope(t, ca, sb, sc):
    return t * ca + pltpu.roll(t, HEAD_PAD - 16, 1) * sb + pltpu.roll(t, 16, 1) * sc


def _rope_t(dt, ca, sb, sc):
    return dt * ca + pltpu.roll(dt * sb, 16, 1) + pltpu.roll(dt * sc, HEAD_PAD - 16, 1)


def _shift_down(y, k, head):
    out = pltpu.roll(y, k, 0)
    row = lax.broadcasted_iota(jnp.int32, y.shape, 0)
    for j in range(k):
        out = jnp.where(row == j, head[head.shape[0] - k + j:head.shape[0] - k + j + 1, :], out)
    return out


def _shift_up(y, k, tail):
    n = y.shape[0]
    out = pltpu.roll(y, n - k, 0)
    row = lax.broadcasted_iota(jnp.int32, y.shape, 0)
    for j in range(k):
        out = jnp.where(row == n - k + j, tail[j:j + 1, :], out)
    return out


def rope_tables(pos, inv):
    t = pos.shape[0]
    tm = min(TM, t)

    def body(pos_ref, inv_ref, ca_ref, sb_ref, sc_ref):
        ang = pos_ref[...] * inv_ref[...]
        c = jnp.cos(ang)
        s = jnp.sin(ang)
        lane = lax.broadcasted_iota(jnp.int32, ang.shape, 1)
        ca_ref[...] = jnp.where(lane < NOPE, 1.0, jnp.where(lane < NOPE + ROPE, c, 0.0))
        sb_ref[...] = jnp.where((lane >= NOPE) & (lane < NOPE + 16), -s, 0.0)
        sc_ref[...] = jnp.where((lane >= NOPE + 16) & (lane < NOPE + ROPE), s, 0.0)

    out = jax.ShapeDtypeStruct((t, HEAD_PAD), F32)
    blk = pl.BlockSpec((tm, HEAD_PAD), lambda i: (i, 0))
    return pl.pallas_call(
        body, name="rope_tables", grid=(t // tm,),
        in_specs=[pl.BlockSpec((tm, 1), lambda i: (i, 0)), pl.BlockSpec((1, HEAD_PAD), lambda i: (0, 0))],
        out_specs=[blk, blk, blk], out_shape=[out, out, out],
        compiler_params=_cp(("parallel",)),
    )(pos, inv)


def pre_in_fwd(x, g, w):
    t = x.shape[0]
    tm = min(TM, t)

    def body(x_ref, g_ref, w_ref, z_ref):
        h, _, _ = _rms_fwd(x_ref[...], g_ref[...])
        z_ref[...] = _mm_nt(h.astype(BF16), w_ref[...]).astype(BF16)

    return pl.pallas_call(
        body, name="pre_in_fwd", grid=(t // tm,),
        in_specs=[pl.BlockSpec((tm, D), lambda i: (i, 0)), _whole(), _whole()],
        out_specs=pl.BlockSpec((tm, Z_W), lambda i: (i, 0)),
        out_shape=jax.ShapeDtypeStruct((t, Z_W), BF16),
        compiler_params=_cp(("parallel",)),
    )(x, g, w)


def mla_proj_fwd(z, ca, sb, sc, gq, gkv, wuq, wukv):
    t = z.shape[0]
    tm = min(TM, t)

    def body(z_ref, ca_ref, sb_ref, sc_ref, gq_ref, gkv_ref, wuq_ref, wukv_ref, q_ref, k_ref, v_ref):
        z = z_ref[...].astype(F32)
        ca, sb, sc = ca_ref[...], sb_ref[...], sc_ref[...]
        cq, _, _ = _rms_fwd(z[:, :Q_RANK], gq_ref[...])
        ckv, _, _ = _rms_fwd(z[:, Q_RANK:Q_RANK + KV_RANK], gkv_ref[...])
        q = _mm_nt(cq.astype(BF16), wuq_ref[...])
        kv = _mm_nt(ckv.astype(BF16), wukv_ref[...])
        kr = _rope(pltpu.roll(z[:, Q_RANK + KV_RANK:], NOPE, 1), ca, sb, sc)
        for h in range(HEADS):
            lanes = slice(h * HEAD_PAD, (h + 1) * HEAD_PAD)
            q_ref[:, lanes] = _rope(q[:, lanes], ca, sb, sc).astype(BF16)
            k_ref[:, lanes] = (kv[:, lanes] + kr).astype(BF16)
        v_ref[...] = kv[:, HEADS * HEAD_PAD:].astype(BF16)

    tab = pl.BlockSpec((tm, HEAD_PAD), lambda i: (i, 0))
    return pl.pallas_call(
        body, name="mla_proj_fwd", grid=(t // tm,),
        in_specs=[pl.BlockSpec((tm, 768), lambda i: (i, 1)), tab, tab, tab, _whole(), _whole(), _whole(), _whole()],
        out_specs=[pl.BlockSpec((tm, HEADS * HEAD_PAD), lambda i: (i, 0)),
                   pl.BlockSpec((tm, HEADS * HEAD_PAD), lambda i: (i, 0)),
                   pl.BlockSpec((tm, HEADS * V_DIM), lambda i: (i, 0))],
        out_shape=[jax.ShapeDtypeStruct((t, HEADS * HEAD_PAD), BF16),
                   jax.ShapeDtypeStruct((t, HEADS * HEAD_PAD), BF16),
                   jax.ShapeDtypeStruct((t, HEADS * V_DIM), BF16)],
        compiler_params=_cp(("parallel",)),
    )(z, ca, sb, sc, gq, gkv, wuq, wukv)


def _each(stages, k):
    def run():
        for stage in stages:
            stage[k]()
    return run


def attn_fwd(q, k, v, carried=()):
    t = q.shape[0]
    tq = min(TQ, t)
    nq = t // tq
    hs = FWD_HEADS
    last_pair = HEADS // hs - 1
    n = len(carried)

    def body(*refs):
        q_ref, k_ref, v_ref = refs[:3]
        o_ref, lse_ref = refs[3 + n:5 + n]
        sems = refs[5 + 2 * n:]
        stages = [_gather_steps(refs[3 + a], refs[5 + n + a], *sems[3 * a:3 * a + 3]) for a in range(n)]
        if n:
            pair = pl.program_id(0)
            pl.when((pair == 0) & (pl.program_id(1) == 0))(_each(stages, 0))
            pl.when((pair == last_pair) & (pl.program_id(1) == 0))(_each(stages, 1))
        i = pl.program_id(1)
        row = lax.broadcasted_iota(jnp.int32, (tq, tq), 0)
        col = lax.broadcasted_iota(jnp.int32, (tq, tq), 1)
        head_lanes = [slice(h * HEAD_PAD, (h + 1) * HEAD_PAD) for h in range(hs)]
        pair_lanes = [slice(p * 2 * V_DIM, (p + 1) * 2 * V_DIM) for p in range(hs // 2)]

        def step(j, carry, masked):
            start = pl.multiple_of(j * tq, tq)
            out = []
            for h in range(hs):
                m, l, acc = carry[h]
                s = _mm_nt(q_ref[:, head_lanes[h]], k_ref[pl.ds(start, tq), head_lanes[h]])
                if masked:
                    s = jnp.where(col <= row, s, NEG)
                m_new = jnp.maximum(m, jnp.max(s, axis=-1, keepdims=True))
                p = jnp.exp2((s - m_new) * SCALE_LOG2E)
                alpha = jnp.exp2((m - m_new) * SCALE_LOG2E)
                l = alpha * l + jnp.sum(p, axis=-1, keepdims=True)
                acc = alpha * acc + _mm(p.astype(BF16), v_ref[pl.ds(start, tq), pair_lanes[h // 2]])
                out.append((m_new, l, acc))
            return tuple(out)

        init = (jnp.full((tq, 1), NEG, F32), jnp.zeros((tq, 1), F32), jnp.zeros((tq, 2 * V_DIM), F32))
        def trip(j, c):
            for u in range(FWD_UNROLL):
                c = step(FWD_UNROLL * j + u, c, False)
            return c

        carry = lax.fori_loop(0, i // FWD_UNROLL, trip, (init,) * hs)
        carry = lax.fori_loop(i - i % FWD_UNROLL, i, lambda j, c: step(j, c, False), carry)
        outs = []
        for h, (m, l, acc) in enumerate(step(i, carry, True)):
            outs.append(acc / l)
            lse_ref[:, head_lanes[h]] = jnp.broadcast_to(m * SCALE + jnp.log(l), (tq, HEAD_PAD))
        lane = lax.broadcasted_iota(jnp.int32, (tq, 2 * V_DIM), 1)
        for p in range(hs // 2):
            o_ref[:, pair_lanes[p]] = jnp.where(lane < V_DIM, outs[2 * p], outs[2 * p + 1])
        if n:
            pl.when((pl.program_id(0) == last_pair) & (i == nq - 1))(_each(stages, 2))

    hbm = pl.BlockSpec(memory_space=pl.ANY)
    return pl.pallas_call(
        body, name=f"attn_fwd_gather{n}" if n else "attn_fwd", grid=(HEADS // hs, nq),
        in_specs=[pl.BlockSpec((tq, hs * HEAD_PAD), lambda p, i: (i, p)),
                  pl.BlockSpec((t, hs * HEAD_PAD), lambda p, i: (0, p)),
                  pl.BlockSpec((t, hs * V_DIM), lambda p, i: (0, p))] + [hbm] * n,
        out_specs=[pl.BlockSpec((tq, hs * V_DIM), lambda p, i: (i, p)),
                   pl.BlockSpec((tq, hs * HEAD_PAD), lambda p, i: (i, p))] + [hbm] * n,
        out_shape=[jax.ShapeDtypeStruct((t, HEADS * V_DIM), F32), jax.ShapeDtypeStruct((t, HEADS * HEAD_PAD), F32)]
        + [jax.ShapeDtypeStruct((N_DEV,) + c.shape, c.dtype) for c in carried],
        scratch_shapes=_comm_sems() * n,
        compiler_params=_cp(("arbitrary", "arbitrary") if n else ("parallel", "parallel")),
    )(q, k, v, *carried)


def _sgu_fwd(zsg, gm, lng, lnb, wc_ref, bias, mixed_ref):
    uv, th = _gelu(zsg)
    u, v0 = uv[:, :SG_W], uv[:, SG_W:]
    vc = v0 - _gmean(v0, gm)
    r = lax.rsqrt(_gmean(vc * vc, gm) + EPS)
    vh = vc * r
    v = vh * lng + lnb
    lane = lax.broadcasted_iota(jnp.int32, (CHUNK, SG_W), 1)
    for c in range(zsg.shape[0] // CHUNK):
        rows = slice(c * CHUNK, (c + 1) * CHUNK)
        vb = v[rows].astype(BF16)
        mixed = bias
        for g in range(SG_W // GROUP):
            mixed = mixed + jnp.where(lane // GROUP == g, _mm(wc_ref[g], vb), 0.0)
        mixed_ref[rows, :] = mixed
    return u, v, vh, r, th


def _conv_fwd(zcv, halo, first, cw):
    gb, gc, hh = zcv[:, :CV_W], zcv[:, CV_W:2 * CV_W], zcv[:, 2 * CV_W:]
    y = gc * hh
    yh = jnp.where(first, 0.0, halo[:, CV_W:2 * CV_W] * halo[:, 2 * CV_W:])
    y1 = _shift_down(y, 1, yh)
    y2 = _shift_down(y, 2, yh)
    conv = y2 * cw[0:1, :] + y1 * cw[1:2, :] + y * cw[2:3, :]
    return gb * conv, conv, y, y1, y2


def _tril_bf16(w_ref, g):
    row = lax.broadcasted_iota(jnp.int32, (CHUNK, CHUNK), 0)
    col = lax.broadcasted_iota(jnp.int32, (CHUNK, CHUNK), 1)
    return jnp.where(col <= row, w_ref[g], 0.0).astype(BF16)


def mix_fwd(x, z, ya, gm, lng, lnb, wsp, bias, cw, gout, wout, gpost):
    t = x.shape[0]
    tm = min(TM, t)

    def body(x_ref, zcv_ref, halo_ref, zsg_ref, ya_ref, gm_ref, lng_ref, lnb_ref, wsp_ref, bias_ref, cw_ref,
             gout_ref, wout_ref, gpost_ref, x1_ref, wc_ref, mixed_ref):
        i = pl.program_id(0)
        for g in range(SG_W // GROUP):
            wc_ref[g] = _tril_bf16(wsp_ref, g)
        u, _, _, _, _ = _sgu_fwd(zsg_ref[...].astype(F32), gm_ref[...], lng_ref[...], lnb_ref[...], wc_ref, bias_ref[...],
                                 mixed_ref)
        yb = u * mixed_ref[...]
        yc, _, _, _, _ = _conv_fwd(zcv_ref[...].astype(F32), halo_ref[...].astype(F32), i == 0, cw_ref[...])
        gout = gout_ref[...]
        na, _, _ = _rms_fwd(ya_ref[...], gout[:, :512])
        nb, _, _ = _rms_fwd(yb, gout[:, 512:768])
        nc, _, _ = _rms_fwd(yc, gout[:, 768:])
        mix = jnp.concatenate([na, nb, nc], axis=1).astype(BF16)
        o, _, _ = _rms_fwd(_mm(mix, wout_ref[...]), gpost_ref[...])
        x1_ref[...] = x_ref[...] + o

    return pl.pallas_call(
        body, name="mix_fwd", grid=(t // tm,),
        in_specs=[pl.BlockSpec((tm, D), lambda i: (i, 0)),
                  pl.BlockSpec((tm, 768), lambda i: (i, 0)),
                  pl.BlockSpec((HALO, 768), lambda i: (jnp.maximum(i * (tm // HALO) - 1, 0), 0)),
                  pl.BlockSpec((tm, 512), lambda i: (i, 3)),
                  pl.BlockSpec((tm, 512), lambda i: (i, 0)),
                  _whole(), _whole(), _whole(), _whole(), _whole(), _whole(), _whole(), _whole(), _whole()],
        out_specs=pl.BlockSpec((tm, D), lambda i: (i, 0)),
        out_shape=jax.ShapeDtypeStruct((t, D), F32),
        scratch_shapes=[pltpu.VMEM((SG_W // GROUP, CHUNK, CHUNK), BF16), pltpu.VMEM((tm, SG_W), F32)],
        compiler_params=_cp(("arbitrary",)),
    )(x, z, z, z, ya, gm, lng, lnb, wsp, bias, cw, gout, wout, gpost)


def _sigmoid(a):
    return 1.0 / (1.0 + jnp.exp(-a))


FFN_SHARD = D_FF // N_DEV


def _load_ffn_weights(g_ref, wgu_ref, wd_ref, sems):
    copies = []
    for j in range(N_DEV):
        for p, (dst, base) in enumerate(((wgu_ref, 0), (wgu_ref, D_FF), (wd_ref, 0))):
            copies.append(pltpu.make_async_copy(g_ref.at[j, pl.ds(p * FFN_SHARD, FFN_SHARD)],
                                                dst.at[pl.ds(base + j * FFN_SHARD, FFN_SHARD)], sems.at[3 * j + p]))
    for cp in copies:
        cp.start()
    for cp in copies:
        cp.wait()


def _ffn_weight_scratch():
    return [pltpu.VMEM((2 * D_FF, D), BF16), pltpu.VMEM((D_FF, D), BF16), pltpu.SemaphoreType.DMA((3 * N_DEV,))]


def ffn_fwd(x1, gpre, gathered, gpost, carried=()):
    t = x1.shape[0]
    tm = min(TM_FFN, t)
    steps = t // tm
    n = len(carried)

    def body(*refs):
        x_ref, gpre_ref, g_ref, gpost_ref = refs[:4]
        x2_ref, f_ref = refs[4 + n:6 + n]
        wgu_ref, wd_ref, sems = refs[6 + 2 * n:9 + 2 * n]
        comm_sems = refs[9 + 2 * n:]
        stages = [_gather_steps(refs[4 + a], refs[6 + n + a], *comm_sems[3 * a:3 * a + 3]) for a in range(n)]
        if n:
            pl.when(pl.program_id(0) == 0)(_each(stages, 0))
            pl.when(pl.program_id(0) == (3 * steps) // 4)(_each(stages, 1))

        @pl.when(pl.program_id(0) == 0)
        def _():
            _load_ffn_weights(g_ref, wgu_ref, wd_ref, sems)

        x = x_ref[...]
        h, _, _ = _rms_fwd(x, gpre_ref[...])
        ab = _mm_nt(h.astype(BF16), wgu_ref[...])
        a, b = ab[:, :D_FF], ab[:, D_FF:]
        s = a * _sigmoid(a) * b
        f = _mm(s.astype(BF16), wd_ref[...])
        f_ref[...] = f
        x2_ref[...] = x + _rms_fwd(f, gpost_ref[...])[0]
        if n:
            pl.when(pl.program_id(0) == steps - 1)(_each(stages, 2))

    row = pl.BlockSpec((tm, D), lambda i: (i, 0))
    hbm = pl.BlockSpec(memory_space=pl.ANY)
    return pl.pallas_call(
        body, name=f"ffn_fwd_gather{n}" if n else "ffn_fwd", grid=(steps,),
        in_specs=[row, _whole(), hbm, _whole()] + [hbm] * n,
        out_specs=[row, row] + [hbm] * n,
        out_shape=[jax.ShapeDtypeStruct((t, D), F32), jax.ShapeDtypeStruct((t, D), F32)]
        + [jax.ShapeDtypeStruct((N_DEV,) + c.shape, c.dtype) for c in carried],
        scratch_shapes=_ffn_weight_scratch() + _comm_sems() * n,
        compiler_params=_cp(("arbitrary",)),
    )(x1, gpre, gathered, gpost, *carried)


def loss_head(y, target):
    t = y.shape[0]
    tm = min(TM, t)

    def body(y_ref, t_ref, loss_ref, dy_ref):
        @pl.when(pl.program_id(0) == 0)
        def _():
            loss_ref[...] = jnp.zeros_like(loss_ref)

        e = y_ref[...] - t_ref[...]
        dy_ref[...] = e * (1.0 / D)
        loss_ref[...] += jnp.sum(jnp.sum(e * e, axis=-1, keepdims=True), axis=0, keepdims=True)

    return pl.pallas_call(
        body, name="loss_head", grid=(t // tm,),
        in_specs=[pl.BlockSpec((tm, D), lambda i: (i, 0)), pl.BlockSpec((tm, D), lambda i: (i, 0))],
        out_specs=[pl.BlockSpec((1, 128), lambda i: (0, 0)), pl.BlockSpec((tm, D), lambda i: (i, 0))],
        out_shape=[jax.ShapeDtypeStruct((1, 128), F32), jax.ShapeDtypeStruct((t, D), F32)],
        compiler_params=_cp(("arbitrary",)),
    )(y, target)


def _acc(ref, first, val):
    @pl.when(first)
    def _():
        ref[...] = val

    @pl.when(jnp.logical_not(first))
    def _():
        ref[...] += val


def ffn_bwd(x1, f, dx2, gpre, gathered, gpost, carried=()):
    t = x1.shape[0]
    tm = min(TM_FFN, t)
    steps = t // tm
    n = len(carried)

    def body(*refs):
        x_ref, f_ref, dx2_ref, gpre_ref, g_ref, gpost_ref = refs[:6]
        dx1_ref, h_ref, dab_ref, s_ref, df_ref, dgpre_ref, dgpost_ref = refs[6 + n:13 + n]
        ab_ref, ds_ref, wgu_ref, wd_ref, sems = refs[13 + 2 * n:18 + 2 * n]
        comm_sems = refs[18 + 2 * n:]
        stages = [_exchange_steps(refs[6 + a], refs[13 + n + a], carried[a][1], *comm_sems[3 * a:3 * a + 3])
                  for a in range(n)]
        first = pl.program_id(0) == 0
        if n:
            pl.when(first)(_each(stages, 0))

        @pl.when(first)
        def _():
            _load_ffn_weights(g_ref, wgu_ref, wd_ref, sems)

        dx2 = dx2_ref[...]
        gpre, gpost = gpre_ref[...], gpost_ref[...]
        h, xh, rx = _rms_fwd(x_ref[...], gpre)
        h_ref[...] = h.astype(BF16)
        ab_ref[...] = _mm_nt(h_ref[...], wgu_ref[...])
        for c in range(0, D_FF, FFN_SLAB):
            a, b = ab_ref[:, c:c + FFN_SLAB], ab_ref[:, D_FF + c:D_FF + c + FFN_SLAB]
            s_ref[:, c:c + FFN_SLAB] = (a * _sigmoid(a) * b).astype(BF16)
        _, fh, rf = _rms_fwd(f_ref[...], gpost)
        df, dgpost = _rms_bwd(fh, rf, gpost, dx2)
        df_ref[...] = df.astype(BF16)
        ds_ref[...] = _mm_nt(df_ref[...], wd_ref[...])
        for c in range(0, D_FF, FFN_SLAB):
            a, b = ab_ref[:, c:c + FFN_SLAB], ab_ref[:, D_FF + c:D_FF + c + FFN_SLAB]
            ds = ds_ref[:, c:c + FFN_SLAB]
            sg = _sigmoid(a)
            dab_ref[:, c:c + FFN_SLAB] = (ds * b * (sg * (1.0 + a * (1.0 - sg)))).astype(BF16)
            dab_ref[:, D_FF + c:D_FF + c + FFN_SLAB] = (ds * (a * sg)).astype(BF16)
        dx, dgpre = _rms_bwd(xh, rx, gpre, _mm(dab_ref[...], wgu_ref[...]))
        dx1_ref[...] = dx2 + dx
        _acc(dgpre_ref, first, dgpre)
        _acc(dgpost_ref, first, dgpost)
        if n:
            pl.when(pl.program_id(0) == steps - 1)(_each(stages, 1))

    row = lambda w: pl.BlockSpec((tm, w), lambda i: (i, 0))
    vec = pl.BlockSpec((1, D), lambda i: (0, 0))
    hbm = pl.BlockSpec(memory_space=pl.ANY)
    return pl.pallas_call(
        body, name=f"ffn_bwd_exchange{n}" if n else "ffn_bwd", grid=(steps,),
        in_specs=[row(D), row(D), row(D), _whole(), hbm, _whole()] + [hbm] * n,
        out_specs=[row(D), row(D), row(2 * D_FF), row(D_FF), row(D), vec, vec] + [hbm] * n,
        out_shape=[jax.ShapeDtypeStruct((t, D), F32), jax.ShapeDtypeStruct((t, D), BF16),
                   jax.ShapeDtypeStruct((t, 2 * D_FF), BF16), jax.ShapeDtypeStruct((t, D_FF), BF16),
                   jax.ShapeDtypeStruct((t, D), BF16), jax.ShapeDtypeStruct((1, D), F32),
                   jax.ShapeDtypeStruct((1, D), F32)]
        + [jax.ShapeDtypeStruct(src.shape if scatter else (N_DEV,) + src.shape, src.dtype) for src, scatter in carried],
        scratch_shapes=[pltpu.VMEM((tm, 2 * D_FF), F32), pltpu.VMEM((tm, D_FF), F32)] + _ffn_weight_scratch()
        + _comm_sems() * n,
        compiler_params=_cp(("arbitrary",)),
    )(x1, f, dx2, gpre, gathered, gpost, *[src for src, _ in carried])


def atb(a, b, tk):
    t, k = a.shape
    n = b.shape[1]
    tt = min(TT, t)
    tk = min(tk, k)
    steps = t // tt

    def body(a_ref, b_ref, o_ref, acc_ref):
        i = pl.program_id(1)
        _acc(acc_ref, i == 0, _mm_tn(a_ref[...], b_ref[...]))

        @pl.when(i == steps - 1)
        def _():
            o_ref[...] = acc_ref[...].astype(BF16)

    return pl.pallas_call(
        body, name="atb", grid=(k // tk, steps),
        in_specs=[pl.BlockSpec((tt, tk), lambda j, i: (i, j)), pl.BlockSpec((tt, n), lambda j, i: (i, 0))],
        out_specs=pl.BlockSpec((tk, n), lambda j, i: (j, 0)),
        out_shape=jax.ShapeDtypeStruct((k, n), BF16),
        scratch_shapes=[pltpu.VMEM((tk, n), F32)],
        compiler_params=_cp(("parallel", "arbitrary")),
    )(a, b)


FFN_TILE_SHARDS = 4


def atb_ffn_chunks(a, b, first_piece, chunks=None):
    t, k = a.shape
    tt = min(TT, t)
    tk = FFN_TILE_SHARDS * FFN_SHARD
    steps = t // tt
    per_piece = N_DEV // FFN_TILE_SHARDS

    def body(*refs):
        a_ref, b_ref, o_ref, acc_ref = refs[0], refs[1], refs[-2], refs[-1]
        i = pl.program_id(1)
        _acc(acc_ref, i == 0, _mm_tn(a_ref[...], b_ref[...]))

        @pl.when(i == steps - 1)
        def _():
            for d in range(FFN_TILE_SHARDS):
                o_ref[d, 0] = acc_ref[d * FFN_SHARD:(d + 1) * FFN_SHARD, :].astype(BF16)

    hbm = pl.BlockSpec(memory_space=pl.ANY)
    return pl.pallas_call(
        body, name="atb_ffn_chunks", grid=(k // tk, steps),
        in_specs=[pl.BlockSpec((tt, tk), lambda j, i: (i, j)), pl.BlockSpec((tt, D), lambda j, i: (i, 0))]
        + ([] if chunks is None else [hbm]),
        out_specs=pl.BlockSpec((FFN_TILE_SHARDS, 1, FFN_SHARD, D),
                               lambda j, i: (j % per_piece, first_piece + j // per_piece, 0, 0)),
        out_shape=jax.ShapeDtypeStruct((N_DEV, len(FFN_PIECES), FFN_SHARD, D), BF16),
        input_output_aliases={} if chunks is None else {2: 0},
        scratch_shapes=[pltpu.VMEM((tk, D), F32)],
        compiler_params=_cp(("parallel", "arbitrary")),
    )(a, b, *([] if chunks is None else [chunks]))


def mix_bwd(dx1, z, ya, gm, lng, lnb, wsp, wspt, bias, cw, gout, wout, gpost):
    t = dx1.shape[0]
    tm = min(TM, t)
    ng = SG_W // GROUP

    def body(dx1_ref, zcv_ref, halo_ref, zsg_ref, ya_ref, gm_ref, lng_ref, lnb_ref, wsp_ref, wspt_ref, bias_ref,
             cw_ref, gout_ref, wout_ref, gpost_ref,
             dya_ref, dyc_ref, dzsg_ref, mix_ref, do_ref, dgpost_ref, dgout_ref, dlng_ref, dlnb_ref, dwsp_ref,
             dbias_ref, wc_ref, wct_ref, mixed_ref, dv_ref):
        i = pl.program_id(0)
        first = i == 0
        gm = gm_ref[...]
        for g in range(ng):
            wc_ref[g] = _tril_bf16(wsp_ref, g)
            wct_ref[g] = jnp.where(
                lax.broadcasted_iota(jnp.int32, (CHUNK, CHUNK), 0) <= lax.broadcasted_iota(jnp.int32, (CHUNK, CHUNK), 1),
                wspt_ref[g], 0.0).astype(BF16)
        zsg = zsg_ref[...].astype(F32)
        lng = lng_ref[...]
        u, v, vh, r, th = _sgu_fwd(zsg, gm, lng, lnb_ref[...], wc_ref, bias_ref[...], mixed_ref)
        mixed = mixed_ref[...]
        yb = u * mixed
        yc, _, _, _, _ = _conv_fwd(zcv_ref[...].astype(F32), halo_ref[...].astype(F32), first, cw_ref[...])
        gout, gpost = gout_ref[...], gpost_ref[...]
        ga, gb_, gc_ = gout[:, :512], gout[:, 512:768], gout[:, 768:]
        na, yah, ra = _rms_fwd(ya_ref[...], ga)
        nb, ybh, rb = _rms_fwd(yb, gb_)
        nc, ych, rc = _rms_fwd(yc, gc_)
        mix = jnp.concatenate([na, nb, nc], axis=1).astype(BF16)
        _, oh, ro = _rms_fwd(_mm(mix, wout_ref[...]), gpost)
        do, dgpost = _rms_bwd(oh, ro, gpost, dx1_ref[...])
        dob = do.astype(BF16)
        dmix = _mm_nt(dob, wout_ref[...])
        dya, dga = _rms_bwd(yah, ra, ga, dmix[:, :512])
        dyb, dgb = _rms_bwd(ybh, rb, gb_, dmix[:, 512:768])
        dyc, dgc = _rms_bwd(ych, rc, gc_, dmix[:, 768:])
        dya_ref[...] = dya
        dyc_ref[...] = dyc
        mix_ref[...] = mix
        do_ref[...] = dob
        _acc(dgpost_ref, first, dgpost)
        _acc(dgout_ref, first, jnp.concatenate([dga, dgb, dgc], axis=1))
        du = dyb * mixed
        dmixed = dyb * u
        lane = lax.broadcasted_iota(jnp.int32, (CHUNK, SG_W), 1)
        row = lax.broadcasted_iota(jnp.int32, (CHUNK, CHUNK), 0)
        col = lax.broadcasted_iota(jnp.int32, (CHUNK, CHUNK), 1)
        dbias = jnp.zeros((CHUNK, SG_W), F32)
        dw = [jnp.zeros((CHUNK, CHUNK), F32) for _ in range(ng)]
        for c in range(tm // CHUNK):
            rows = slice(c * CHUNK, (c + 1) * CHUNK)
            dm = dmixed[rows]
            dbias = dbias + dm
            dmb = dm.astype(BF16)
            vb = v[rows].astype(BF16)
            dvc = jnp.zeros((CHUNK, SG_W), F32)
            for g in range(ng):
                in_g = lane // GROUP == g
                dvc = dvc + jnp.where(in_g, _mm(wct_ref[g], dmb), 0.0)
                dw[g] = dw[g] + _mm_nt(jnp.where(in_g, dmb, jnp.zeros_like(dmb)), vb)
            dv_ref[rows, :] = dvc
        for g in range(ng):
            dwg = jnp.where(col <= row, dw[g], 0.0)

            @pl.when(first)
            def _():
                dwsp_ref[g] = dwg

            @pl.when(jnp.logical_not(first))
            def _():
                dwsp_ref[g] += dwg
        _acc(dbias_ref, first, _gmean(dbias, gm) * GROUP)
        dv = dv_ref[...]
        _acc(dlng_ref, first, jnp.sum(dv * vh, axis=0, keepdims=True))
        _acc(dlnb_ref, first, jnp.sum(dv, axis=0, keepdims=True))
        dvh = dv * lng
        dv0 = r * (dvh - _gmean(dvh, gm) - vh * _gmean(dvh * vh, gm))
        dzsg_ref[...] = (jnp.concatenate([du, dv0], axis=1) * _gelu_grad(zsg, th)).astype(BF16)

    row_ = lambda w: pl.BlockSpec((tm, w), lambda i: (i, 0))
    vec = lambda w: pl.BlockSpec((1, w), lambda i: (0, 0))
    return pl.pallas_call(
        body, name="mix_bwd", grid=(t // tm,),
        in_specs=[row_(D),
                  pl.BlockSpec((tm, 768), lambda i: (i, 0)),
                  pl.BlockSpec((HALO, 768), lambda i: (jnp.maximum(i * (tm // HALO) - 1, 0), 0)),
                  pl.BlockSpec((tm, 512), lambda i: (i, 3)),
                  row_(512),
                  _whole(), _whole(), _whole(), _whole(), _whole(), _whole(), _whole(), _whole(), _whole(), _whole()],
        out_specs=[row_(512), row_(CV_W), row_(512), row_(D), row_(D), vec(D), vec(D), vec(SG_W), vec(SG_W),
                   pl.BlockSpec((ng, CHUNK, CHUNK), lambda i: (0, 0, 0)),
                   pl.BlockSpec((CHUNK, SG_W), lambda i: (0, 0))],
        out_shape=[jax.ShapeDtypeStruct((t, 512), F32), jax.ShapeDtypeStruct((t, CV_W), F32),
                   jax.ShapeDtypeStruct((t, 512), BF16), jax.ShapeDtypeStruct((t, D), BF16),
                   jax.ShapeDtypeStruct((t, D), BF16), jax.ShapeDtypeStruct((1, D), F32),
                   jax.ShapeDtypeStruct((1, D), F32), jax.ShapeDtypeStruct((1, SG_W), F32),
                   jax.ShapeDtypeStruct((1, SG_W), F32), jax.ShapeDtypeStruct((ng, CHUNK, CHUNK), F32),
                   jax.ShapeDtypeStruct((CHUNK, SG_W), F32)],
        scratch_shapes=[pltpu.VMEM((ng, CHUNK, CHUNK), BF16), pltpu.VMEM((ng, CHUNK, CHUNK), BF16),
                        pltpu.VMEM((tm, SG_W), F32), pltpu.VMEM((tm, SG_W), F32)],
        compiler_params=_cp(("arbitrary",)),
    )(dx1, z, z, z, ya, gm, lng, lnb, wsp, wspt, bias, cw, gout, wout, gpost)


def conv_bwd(dyc, z, cw):
    t = dyc.shape[0]
    tm = min(TM, t)
    hb = tm // 8
    last_blk = t // 8 - 1

    def body(dyc_ref, dyct_ref, zcv_ref, head_ref, tail_ref, cw_ref, dz_ref, dcw_ref):
        i = pl.program_id(0)
        first = i == 0
        last = i == pl.num_programs(0) - 1
        cw = cw_ref[...]
        zcv = zcv_ref[...].astype(F32)
        gb, gc, hh = zcv[:, :CV_W], zcv[:, CV_W:2 * CV_W], zcv[:, 2 * CV_W:]
        _, conv, y, y1, y2 = _conv_fwd(zcv, head_ref[...].astype(F32), first, cw)
        dyc = dyc_ref[...]
        dconv = dyc * gb
        tail = jnp.where(last, 0.0, dyct_ref[...] * tail_ref[:8, :CV_W].astype(F32))
        d1 = _shift_up(dconv, 1, tail)
        d2 = _shift_up(dconv, 2, tail)
        dy = dconv * cw[2:3, :] + d1 * cw[1:2, :] + d2 * cw[0:1, :]
        dz_ref[...] = jnp.concatenate([dyc * conv, dy * hh, dy * gc], axis=1).astype(BF16)
        tap = lax.broadcasted_iota(jnp.int32, (8, CV_W), 0)
        dcw = jnp.where(tap == 0, jnp.sum(dconv * y2, axis=0, keepdims=True),
                        jnp.where(tap == 1, jnp.sum(dconv * y1, axis=0, keepdims=True),
                                  jnp.where(tap == 2, jnp.sum(dconv * y, axis=0, keepdims=True), 0.0)))
        _acc(dcw_ref, first, dcw)

    return pl.pallas_call(
        body, name="conv_bwd", grid=(t // tm,),
        in_specs=[pl.BlockSpec((tm, CV_W), lambda i: (i, 0)),
                  pl.BlockSpec((8, CV_W), lambda i: (jnp.minimum((i + 1) * hb, last_blk), 0)),
                  pl.BlockSpec((tm, 768), lambda i: (i, 0)),
                  pl.BlockSpec((HALO, 768), lambda i: (jnp.maximum(i * (tm // HALO) - 1, 0), 0)),
                  pl.BlockSpec((HALO, 768), lambda i: (jnp.minimum((i + 1) * (tm // HALO), t // HALO - 1), 0)),
                  _whole()],
        out_specs=[pl.BlockSpec((tm, 768), lambda i: (i, 0)), pl.BlockSpec((8, CV_W), lambda i: (0, 0))],
        out_shape=[jax.ShapeDtypeStruct((t, 768), BF16), jax.ShapeDtypeStruct((8, CV_W), F32)],
        compiler_params=_cp(("arbitrary",)),
    )(dyc, dyc, z, z, z, cw)


def attn_bwd(q, k, v, o, lse, do, carried=()):
    t = q.shape[0]
    tq = min(TQ, t)
    nq = t // tq
    last_pair = HEADS // 2 - 1
    n = len(carried)

    def body(*refs):
        j = pl.program_id(1)
        q_ref, k_ref, v_ref, o_ref, lse_ref, do_ref = refs[:6]
        dq_out_ref, dk_ref, dv_ref = refs[6 + n:9 + n]
        dq_ref = refs[9 + 2 * n]
        sems = refs[10 + 2 * n:]
        stages = [_exchange_steps(refs[6 + a], refs[9 + n + a], carried[a][1], *sems[3 * a:3 * a + 3]) for a in range(n)]
        if n:
            pl.when((pl.program_id(0) == 0) & (j == 0))(_each(stages, 0))

        @pl.when(j == 0)
        def _():
            dq_ref[...] = jnp.zeros_like(dq_ref)

        row = lax.broadcasted_iota(jnp.int32, (tq, tq), 0)
        col = lax.broadcasted_iota(jnp.int32, (tq, tq), 1)
        vlane = lax.broadcasted_iota(jnp.int32, (tq, 2 * V_DIM), 1)
        head_lanes = [slice(h * HEAD_PAD, (h + 1) * HEAD_PAD) for h in range(2)]

        def step(i, carry, masked):
            start = pl.multiple_of(i * tq, tq)
            do_blk = do_ref[pl.ds(start, tq), :]
            o_blk = o_ref[pl.ds(start, tq), :]
            vb = v_ref[...]
            dks, dv_acc = [], carry[2]
            for h in range(2):
                lanes = head_lanes[h]
                qb = q_ref[pl.ds(start, tq), lanes]
                kb = k_ref[:, lanes]
                dob = jnp.where((vlane // V_DIM) == h, do_blk, 0.0)
                delta = jnp.sum(dob * o_blk, axis=-1, keepdims=True)
                lse2 = lse_ref[pl.ds(start, tq), lanes][:, 0:1] * LOG2E
                s = _mm_nt(qb, kb)
                if masked:
                    s = jnp.where(col <= row, s, NEG)
                p = jnp.exp2(s * SCALE_LOG2E - lse2)
                dob16 = dob.astype(BF16)
                dp = _mm_nt(dob16, vb)
                ds = (p * (dp - delta) * SCALE).astype(BF16)
                dv_acc = dv_acc + _mm_tn(p.astype(BF16), dob16)
                dks.append(carry[h] + _mm_tn(ds, qb))
                dq_ref[pl.ds(start, tq), lanes] += _mm(ds, kb)
            return dks[0], dks[1], dv_acc

        zero = jnp.zeros((tq, HEAD_PAD), F32)
        carry = step(j, (zero, zero, jnp.zeros((tq, 2 * V_DIM), F32)), True)
        rest = nq - 1 - j
        carry = lax.fori_loop(0, rest // 2, lambda u, c: step(j + 2 + 2 * u, step(j + 1 + 2 * u, c, False), False), carry)
        dk0, dk1, dv_acc = lax.fori_loop(0, rest % 2, lambda _, c: step(nq - 1, c, False), carry)
        dk_ref[:, head_lanes[0]] = dk0.astype(BF16)
        dk_ref[:, head_lanes[1]] = dk1.astype(BF16)
        dv_ref[...] = dv_acc.astype(BF16)

        @pl.when(j == nq - 1)
        def _():
            dq_out_ref[...] = dq_ref[...].astype(BF16)

        if n:
            pl.when((pl.program_id(0) == last_pair) & (j == nq - 1))(_each(stages, 1))

    hbm = pl.BlockSpec(memory_space=pl.ANY)
    return pl.pallas_call(
        body, name=f"attn_bwd_exchange{n}" if n else "attn_bwd", grid=(HEADS // 2, nq),
        in_specs=[pl.BlockSpec((t, 2 * HEAD_PAD), lambda p, j: (0, p)),
                  pl.BlockSpec((tq, 2 * HEAD_PAD), lambda p, j: (j, p)),
                  pl.BlockSpec((tq, 2 * V_DIM), lambda p, j: (j, p)),
                  pl.BlockSpec((t, 2 * V_DIM), lambda p, j: (0, p)),
                  pl.BlockSpec((t, 2 * HEAD_PAD), lambda p, j: (0, p)),
                  pl.BlockSpec((t, 2 * V_DIM), lambda p, j: (0, p))] + [hbm] * n,
        out_specs=[pl.BlockSpec((t, 2 * HEAD_PAD), lambda p, j: (0, p)),
                   pl.BlockSpec((tq, 2 * HEAD_PAD), lambda p, j: (j, p)),
                   pl.BlockSpec((tq, 2 * V_DIM), lambda p, j: (j, p))] + [hbm] * n,
        out_shape=[jax.ShapeDtypeStruct((t, HEADS * HEAD_PAD), BF16), jax.ShapeDtypeStruct((t, HEADS * HEAD_PAD), BF16),
                   jax.ShapeDtypeStruct((t, HEADS * V_DIM), BF16)]
        + [jax.ShapeDtypeStruct(src.shape if scatter else (N_DEV,) + src.shape, src.dtype) for src, scatter in carried],
        scratch_shapes=[pltpu.VMEM((t, 2 * HEAD_PAD), F32)] + _comm_sems() * n,
        compiler_params=_cp(("arbitrary", "arbitrary") if n else ("parallel", "arbitrary")),
    )(q, k, v, o, lse, do, *[src for src, _ in carried])


def mla_proj_bwd(dq, dk, dv, z, ca, sb, sc, gq, gkv, wuq, wukv):
    t = z.shape[0]
    tm = min(TM, t)

    def body(dq_ref, dk_ref, dv_ref, z_ref, ca_ref, sb_ref, sc_ref, gq_ref, gkv_ref, wuq_ref, wukv_ref,
             dz_ref, cq_ref, ckv_ref, dqp_ref, dkvp_ref, dgq_ref, dgkv_ref):
        first = pl.program_id(0) == 0
        z = z_ref[...].astype(F32)
        ca, sb, sc = ca_ref[...], sb_ref[...], sc_ref[...]
        gq, gkv = gq_ref[...], gkv_ref[...]
        cq, cqh, rq = _rms_fwd(z[:, :Q_RANK], gq)
        ckv, ckvh, rkv = _rms_fwd(z[:, Q_RANK:Q_RANK + KV_RANK], gkv)
        lane = lax.broadcasted_iota(jnp.int32, (tm, HEAD_PAD), 1)
        dkr = jnp.zeros((tm, HEAD_PAD), F32)
        for h in range(HEADS):
            lanes = slice(h * HEAD_PAD, (h + 1) * HEAD_PAD)
            dqp_ref[:, lanes] = _rope_t(dq_ref[:, lanes].astype(F32), ca, sb, sc).astype(BF16)
            dkh = dk_ref[:, lanes].astype(F32)
            dkr = dkr + dkh
            dkvp_ref[:, lanes] = jnp.where(lane < NOPE, dkh, 0.0).astype(BF16)
        dkvp_ref[:, HEADS * HEAD_PAD:] = dv_ref[...].astype(BF16)
        dkr = pltpu.roll(_rope_t(jnp.where(lane >= NOPE, dkr, 0.0), ca, sb, sc), HEAD_PAD - NOPE, 1)
        dkr = jnp.where(lane < ROPE, dkr, 0.0)
        dcq = _mm(dqp_ref[...], wuq_ref[...])
        dckv = _mm(dkvp_ref[...], wukv_ref[...])
        dzq, dgq = _rms_bwd(cqh, rq, gq, dcq)
        dzkv, dgkv = _rms_bwd(ckvh, rkv, gkv, dckv)
        dz_ref[...] = jnp.concatenate([dzq, dzkv, dkr], axis=1).astype(BF16)
        cq_ref[...] = cq.astype(BF16)
        ckv_ref[...] = ckv.astype(BF16)
        _acc(dgq_ref, first, dgq)
        _acc(dgkv_ref, first, dgkv)

    row = lambda w: pl.BlockSpec((tm, w), lambda i: (i, 0))
    vec = lambda w: pl.BlockSpec((1, w), lambda i: (0, 0))
    return pl.pallas_call(
        body, name="mla_proj_bwd", grid=(t // tm,),
        in_specs=[row(1024), row(1024), row(512), pl.BlockSpec((tm, 768), lambda i: (i, 1)),
                  row(HEAD_PAD), row(HEAD_PAD), row(HEAD_PAD), _whole(), _whole(), _whole(), _whole()],
        out_specs=[row(768), row(Q_RANK), row(KV_RANK), row(1024), row(1536), vec(Q_RANK), vec(KV_RANK)],
        out_shape=[jax.ShapeDtypeStruct((t, 768), BF16), jax.ShapeDtypeStruct((t, Q_RANK), BF16),
                   jax.ShapeDtypeStruct((t, KV_RANK), BF16), jax.ShapeDtypeStruct((t, 1024), BF16),
                   jax.ShapeDtypeStruct((t, 1536), BF16), jax.ShapeDtypeStruct((1, Q_RANK), F32),
                   jax.ShapeDtypeStruct((1, KV_RANK), F32)],
        compiler_params=_cp(("arbitrary",)),
    )(dq, dk, dv, z, ca, sb, sc, gq, gkv, wuq, wukv)


def pre_in_bwd(x, dx1, dzcv, dzmla, dzsg, g, w):
    t = x.shape[0]
    tm = min(TM, t)

    def body(x_ref, dx1_ref, dzcv_ref, dzmla_ref, dzsg_ref, g_ref, w_ref, dx_ref, h_ref, dz_ref, dg_ref):
        g = g_ref[...]
        h, xh, r = _rms_fwd(x_ref[...], g)
        dz = jnp.concatenate([dzcv_ref[...], dzmla_ref[...], dzsg_ref[...]], axis=1)
        dx, dg = _rms_bwd(xh, r, g, _mm(dz, w_ref[...]))
        dx_ref[...] = dx1_ref[...] + dx
        h_ref[...] = h.astype(BF16)
        dz_ref[...] = dz
        _acc(dg_ref, pl.program_id(0) == 0, dg)

    row = lambda w_: pl.BlockSpec((tm, w_), lambda i: (i, 0))
    return pl.pallas_call(
        body, name="pre_in_bwd", grid=(t // tm,),
        in_specs=[row(D), row(D), row(768), row(768), row(512), _whole(), _whole()],
        out_specs=[row(D), row(D), row(Z_W), pl.BlockSpec((1, D), lambda i: (0, 0))],
        out_shape=[jax.ShapeDtypeStruct((t, D), F32), jax.ShapeDtypeStruct((t, D), BF16),
                   jax.ShapeDtypeStruct((t, Z_W), BF16), jax.ShapeDtypeStruct((1, D), F32)],
        compiler_params=_cp(("arbitrary",)),
    )(x, dx1, dzcv, dzmla, dzsg, g, w)


MESH = pl.DeviceIdType.MESH


def _place():
    return lax.axis_index("x"), lax.axis_index("y"), lax.axis_index("c")


def _comm_sems():
    return [pltpu.SemaphoreType.DMA((7,)), pltpu.SemaphoreType.DMA((7,)), pltpu.SemaphoreType.DMA]


def _gather_steps(x_ref, out_ref, send_sems, recv_sems, local_sem):
    x, y, c = _place()
    me, sibling = (x, y, c), (x, y, 1 - c)
    chips = [(1 - x, y), (x, 1 - y), (1 - x, 1 - y)]

    def slot(px, py, pc):
        return out_ref.at[4 * px + 2 * py + pc]

    def copy(k, blk, to, src=None):
        return pltpu.make_async_remote_copy(
            src_ref=slot(*blk) if src is None else src, dst_ref=slot(*blk),
            send_sem=send_sems.at[k], recv_sem=recv_sems.at[k], device_id=to, device_id_type=MESH)

    mine = pltpu.make_async_copy(x_ref, slot(*me), local_sem)
    first = [copy(0, me, sibling, src=x_ref)] + [copy(1 + j, me, (*chip, c), src=x_ref) for j, chip in enumerate(chips)]
    passed = [copy(4 + j, (*chip, c), sibling) for j, chip in enumerate(chips)]

    def start():
        mine.start()
        for cp in first:
            cp.start()

    def forward():
        for j, chip in enumerate(chips):
            copy(1 + j, (*chip, c), me).wait_recv()
            passed[j].start()

    def finish():
        copy(0, sibling, me).wait_recv()
        for j, chip in enumerate(chips):
            copy(4 + j, (*chip, 1 - c), me).wait_recv()
        for cp in first + passed:
            cp.wait_send()
        mine.wait()

    return start, forward, finish


def _exchange_steps(src_ref, out_ref, scatter, send_sems, recv_sems, local_sem):
    x, y, c = _place()
    me = 4 * x + 2 * y + c
    own = pltpu.make_async_copy(src_ref.at[me] if scatter else src_ref, out_ref.at[me], local_sem)
    copies = []
    for k in range(1, N_DEV):
        px = 1 - x if k & 4 else x
        py = 1 - y if k & 2 else y
        pc = 1 - c if k & 1 else c
        copies.append(pltpu.make_async_remote_copy(
            src_ref=src_ref.at[4 * px + 2 * py + pc] if scatter else src_ref, dst_ref=out_ref.at[me],
            send_sem=send_sems.at[k - 1], recv_sem=recv_sems.at[k - 1], device_id=(px, py, pc), device_id_type=MESH))

    def start():
        own.start()
        for cp in copies:
            cp.start()

    def finish():
        for cp in copies:
            cp.wait_recv()
        for cp in copies:
            cp.wait_send()
        own.wait()

    return start, finish


def all_gather(block):
    def body(x_ref, out_ref, *sems):
        for stage in _gather_steps(x_ref, out_ref, *sems):
            stage()

    return pl.pallas_call(
        body, name="all_gather",
        in_specs=[pl.BlockSpec(memory_space=pl.ANY)],
        out_specs=pl.BlockSpec(memory_space=pl.ANY),
        out_shape=jax.ShapeDtypeStruct((N_DEV,) + block.shape, block.dtype),
        scratch_shapes=_comm_sems(),
    )(block)


def _row_tile(r, cap):
    return max(d for d in range(16, cap + 1, 16) if r % d == 0)


def sum_adamw(parts, w, m, v, cap, carried=()):
    nl, r, c = w.shape
    tr = _row_tile(r, cap)
    steps = r // tr
    n = len(carried)
    c1 = 1.0 / (1.0 - ADAM_B1 ** ADAM_STEP)
    c2 = 1.0 / (1.0 - ADAM_B2 ** ADAM_STEP)

    def body(*refs):
        p_refs = refs[:nl]
        w_ref, m_ref, v_ref = refs[nl:nl + 3]
        g_ref, d_ref, nm_ref, nv_ref = refs[nl + 3 + n:nl + 7 + n]
        sems = refs[nl + 7 + 2 * n:]
        stages = [_exchange_steps(refs[nl + 3 + a], refs[nl + 7 + n + a], carried[a][1], *sems[3 * a:3 * a + 3])
                  for a in range(n)]
        layer, i = pl.program_id(0), pl.program_id(1)
        if n:
            pl.when((layer == 0) & (i == 0))(_each(stages, 0))

        def update(p_ref):
            g = p_ref[0].astype(F32)
            for k in range(1, N_DEV):
                g = g + p_ref[k].astype(F32)
            m_new = ADAM_B1 * m_ref[...] + (1.0 - ADAM_B1) * g
            v_new = ADAM_B2 * v_ref[...] + (1.0 - ADAM_B2) * (g * g)
            g_ref[...] = g
            nm_ref[...] = m_new
            nv_ref[...] = v_new
            d_ref[...] = -ADAM_LR * ((m_new * c1) / (jnp.sqrt(v_new * c2) + ADAM_EPS) + ADAM_WD * w_ref[...])

        for k in range(nl):
            pl.when(layer == k)(functools.partial(update, p_refs[k]))
        if n:
            pl.when((layer == nl - 1) & (i == steps - 1))(_each(stages, 1))

    def parts_spec(k):
        return pl.BlockSpec((N_DEV, tr, c), lambda l, i: (0, jnp.where(l == k, i, jnp.where(l < k, 0, steps - 1)), 0))

    blk = pl.BlockSpec((None, tr, c), lambda l, i: (l, i, 0))
    out = jax.ShapeDtypeStruct((nl, r, c), F32)
    hbm = pl.BlockSpec(memory_space=pl.ANY)
    return pl.pallas_call(
        body, name=f"sum_adamw_exchange{n}" if n else "sum_adamw", grid=(nl, steps),
        in_specs=[parts_spec(k) for k in range(nl)] + [blk, blk, blk] + [hbm] * n,
        out_specs=[blk, blk, blk, blk] + [hbm] * n,
        out_shape=[out, out, out, out]
        + [jax.ShapeDtypeStruct(src.shape if scatter else (N_DEV,) + src.shape, src.dtype) for src, scatter in carried],
        scratch_shapes=_comm_sems() * n,
        compiler_params=_cp(("arbitrary", "arbitrary")),
    )(*parts, w, m, v, *[src for src, _ in carried])


PACK_W = 1024
MIX_PIECES = (("w_out", D // N_DEV, D, False), ("w_uq", HEADS * (NOPE + ROPE) // N_DEV, Q_RANK, True),
              ("w_ukv", HEADS * (NOPE + V_DIM) // N_DEV, KV_RANK, True), ("conv", 16, PACK_W, False),
              ("w_in", IN_W // N_DEV, D, True))
FFN_PIECES = (("w_gate", D_FF // N_DEV, D, True), ("w_up", D_FF // N_DEV, D, True), ("w_down", D_FF // N_DEV, D, False))
def _packed_rows(rows, cols):
    return rows * cols // PACK_W


OFFSET = {}
for _pieces in (MIX_PIECES, FFN_PIECES):
    _off = 0
    for _name, _rows, _cols, _ in _pieces:
        assert _rows * _cols % PACK_W == 0
        OFFSET[_name] = _off
        _off += _packed_rows(_rows, _cols) + -_packed_rows(_rows, _cols) % 16
assert all(o % 16 == 0 for o in OFFSET.values())
assert [OFFSET[n] for n in ("w_gate", "w_up", "w_down")] == [0, FFN_SHARD, 2 * FFN_SHARD]
CONV_BITS = 3 * (CV_W // N_DEV) * 2


def _to_pack(shards, dtype, pieces, conv=None):
    nl = shards["w_in"].shape[0]
    parts = []
    for name, rows, cols, transposed in pieces:
        if name == "conv":
            if conv is None:
                a = jnp.zeros((nl, rows, PACK_W), dtype)
            else:
                bits = lax.bitcast_convert_type(conv.astype(F32), BF16).reshape(nl, CONV_BITS)
                a = jnp.pad(bits, ((0, 0), (0, rows * PACK_W - CONV_BITS))).reshape(nl, rows, PACK_W)
        else:
            a = shards[name].astype(dtype)
            a = (jnp.swapaxes(a, 1, 2) if transposed else a).reshape(nl, _packed_rows(rows, cols), PACK_W)
            a = jnp.pad(a, ((0, 0), (0, -a.shape[1] % 16), (0, 0)))
        parts.append(a)
    return jnp.concatenate(parts, axis=1)


def _from_pack(pack, pieces):
    out = {}
    for name, rows, cols, transposed in pieces:
        if name != "conv":
            a = pack[:, OFFSET[name]:OFFSET[name] + _packed_rows(rows, cols)].reshape(pack.shape[0], rows, cols)
            out[name] = jnp.swapaxes(a, 1, 2) if transposed else a
    return out


def _mix_weights(g):
    def rows(name):
        _, n, cols, _ = next(p for p in MIX_PIECES if p[0] == name)
        return g[:, OFFSET[name]:OFFSET[name] + _packed_rows(n, cols)].reshape(N_DEV, n, cols)

    w_in_t = rows("w_in").reshape(IN_W, D)
    w_in_p = jnp.concatenate([w_in_t[1184:], w_in_t[:672], jnp.zeros((96, D), BF16), w_in_t[672:1184]], axis=0)
    w_uq_p = jnp.pad(rows("w_uq"), ((0, 0), (0, HEAD_PAD - NOPE - ROPE), (0, 0))).reshape(HEADS * HEAD_PAD, Q_RANK)
    kv = rows("w_ukv")
    w_k = jnp.pad(kv[:, :NOPE], ((0, 0), (0, HEAD_PAD - NOPE), (0, 0))).reshape(HEADS * HEAD_PAD, KV_RANK)
    w_ukv_p = jnp.concatenate([w_k, kv[:, NOPE:].reshape(HEADS * V_DIM, KV_RANK)], axis=0)
    bits = rows("conv").reshape(N_DEV, -1)[:, :CONV_BITS].reshape(N_DEV, 3, CV_W // N_DEV, 2)
    conv_w = jnp.moveaxis(lax.bitcast_convert_type(bits, F32), 0, 1).reshape(3, CV_W)
    return dict(w_in=w_in_p, w_uq=w_uq_p, w_ukv=w_ukv_p, w_out=rows("w_out").reshape(D, D), conv_w=conv_w)


def _grad_chunks(full):
    d_in = full["w_in"]
    d_in = jnp.concatenate([d_in[768:768 + 672], d_in[1536:], d_in[:768]], axis=0)
    d_uq = full["w_uq"].reshape(HEADS, HEAD_PAD, Q_RANK)[:, :NOPE + ROPE]
    d_k = full["w_ukv"][:HEADS * HEAD_PAD].reshape(HEADS, HEAD_PAD, KV_RANK)[:, :NOPE]
    d_v = full["w_ukv"][HEADS * HEAD_PAD:].reshape(HEADS, V_DIM, KV_RANK)
    mats = dict(w_in=d_in, w_uq=d_uq, w_ukv=jnp.concatenate([d_k, d_v], axis=1), w_out=full["w_out"])
    parts = []
    for name, rows, cols, _ in MIX_PIECES:
        if name == "conv":
            parts.append(jnp.zeros((N_DEV, rows, PACK_W), BF16))
        else:
            a = mats[name].reshape(N_DEV, _packed_rows(rows, cols), PACK_W)
            parts.append(jnp.pad(a, ((0, 0), (0, -a.shape[1] % 16), (0, 0))))
    return jnp.concatenate(parts, axis=1)


SMALL = (("mix_pre_g", (D,)), ("mix_post_g", (D,)), ("ffn_pre_g", (D,)), ("ffn_post_g", (D,)), ("q_norm_g", (Q_RANK,)),
         ("kv_norm_g", (KV_RANK,)), ("sg_ln_g", (SG_W,)), ("sg_ln_b", (SG_W,)), ("w_sp", (4, CHUNK, CHUNK)),
         ("b_sp", (4, CHUNK)), ("out_norm_g", (D,)))
SMALL_ROWS = 576


def _pack_small(vals, nl):
    flat = jnp.concatenate([vals[name].reshape(nl, -1) for name, _ in SMALL] + [vals["conv_w"].reshape(nl, -1)], axis=1)
    return jnp.pad(flat, ((0, 0), (0, SMALL_ROWS * 128 - flat.shape[1]))).reshape(nl * SMALL_ROWS, 128)


def _unpack_small(pack, nl):
    flat = pack.reshape(nl, SMALL_ROWS * 128)
    out, off = {}, 0
    for name, shape in SMALL + (("conv_w", (3, CV_W)),):
        n = int(np.prod(shape))
        out[name] = flat[:, off:off + n].reshape((nl,) + shape)
        off += n
    return out


def _layer_fwd(x, lw, sp, tabs, consts, ffn_pack, next_mix_pack):
    ca, sb, sc = tabs
    z = pre_in_fwd(x, sp["mix_pre_g"], lw["w_in"])
    q, k, v = mla_proj_fwd(z, ca, sb, sc, sp["q_norm_g"], sp["kv_norm_g"], lw["w_uq"], lw["w_ukv"])
    ya, lse, lw["ffn"] = attn_fwd(q, k, v, (ffn_pack,))
    x1 = mix_fwd(x, z, ya, consts["gm"], sp["sg_ln_g"], sp["sg_ln_b"], sp["w_sp"], sp["bias"], lw["conv_w"],
                 sp["out_norm_g"], lw["w_out"], sp["mix_post_g"])
    x2, f, *mix_gathered = ffn_fwd(x1, sp["ffn_pre_g"], lw["ffn"], sp["ffn_post_g"], next_mix_pack)
    return x2, (x, z, q, k, v, ya, lse, x1, f), mix_gathered


def _layer_bwd(dx2, saved, lw, sp, tabs, consts, pending):
    ca, sb, sc = tabs
    x, z, q, k, v, ya, lse, x1, f = saved
    dx1, h2, dab, s, df, d_ffn_pre, d_ffn_post, *received = ffn_bwd(x1, f, dx2, sp["ffn_pre_g"], lw["ffn"], sp["ffn_post_g"],
                                                                    pending)
    ffn_chunks = atb_ffn_chunks(s, df, 2, atb_ffn_chunks(dab, h2, 0)).reshape(N_DEV, len(FFN_PIECES) * FFN_SHARD, D)
    dya, dyc, dzsg, mix, do, d_mix_post, d_out_norm, d_lng, d_lnb, d_wsp, d_bias = mix_bwd(
        dx1, z, ya, consts["gm"], sp["sg_ln_g"], sp["sg_ln_b"], sp["w_sp"], sp["w_sp_t"], sp["bias"], lw["conv_w"],
        sp["out_norm_g"], lw["w_out"], sp["mix_post_g"])
    d_w_out = atb(mix, do, 1024)
    dzcv, d_cw = conv_bwd(dyc, z, lw["conv_w"])
    dq, dk, dv, got_ffn = attn_bwd(q, k, v, ya, lse, dya, ((ffn_chunks, True),))
    dzmla, cq, ckv, dqp, dkvp, d_gq, d_gkv = mla_proj_bwd(dq, dk, dv, z, ca, sb, sc, sp["q_norm_g"], sp["kv_norm_g"],
                                                          lw["w_uq"], lw["w_ukv"])
    d_w_uq = atb(dqp, cq, 1024)
    d_w_ukv = atb(dkvp, ckv, 1536)
    dx, h1, dz, d_mix_pre = pre_in_bwd(x, dx1, dzcv, dzmla, dzsg, sp["mix_pre_g"], lw["w_in"])
    d_w_in = atb(dz, h1, 2048)
    mix_chunks = _grad_chunks(dict(w_in=d_w_in, w_uq=d_w_uq, w_ukv=d_w_ukv, w_out=d_w_out))
    d_bsp = d_bias[:, ::GROUP].T
    small = dict(mix_pre_g=d_mix_pre[0], mix_post_g=d_mix_post[0], ffn_pre_g=d_ffn_pre[0], ffn_post_g=d_ffn_post[0],
                 q_norm_g=d_gq[0], kv_norm_g=d_gkv[0], sg_ln_g=d_lng[0], sg_ln_b=d_lnb[0], w_sp=d_wsp, b_sp=d_bsp,
                 out_norm_g=d_out_norm[0], conv_w=d_cw[:3])
    small_pack = _pack_small({name: a[None] for name, a in small.items()}, 1)
    return dx, ((mix_chunks, True), (small_pack, False)), [got_ffn] + received


def kernel(x, positions, mix_pre_g, mix_post_g, ffn_pre_g, ffn_post_g, w_in, q_norm_g, w_uq, kv_norm_g, w_ukv, sg_ln_g, sg_ln_b, w_sp, b_sp, conv_w, out_norm_g, w_out, w_gate, w_up, w_down, loss_target, m_mix_pre_g, m_mix_post_g, m_ffn_pre_g, m_ffn_post_g, m_w_in, m_q_norm_g, m_w_uq, m_kv_norm_g, m_w_ukv, m_sg_ln_g, m_sg_ln_b, m_w_sp, m_b_sp, m_conv_w, m_out_norm_g, m_w_out, m_w_gate, m_w_up, m_w_down, v_mix_pre_g, v_mix_post_g, v_ffn_pre_g, v_ffn_post_g, v_w_in, v_q_norm_g, v_w_uq, v_kv_norm_g, v_w_ukv, v_sg_ln_g, v_sg_ln_b, v_w_sp, v_b_sp, v_conv_w, v_out_norm_g, v_w_out, v_w_gate, v_w_up, v_w_down):
    nl = w_in.shape[0]
    t = x.shape[1]
    w = dict(mix_pre_g=mix_pre_g, mix_post_g=mix_post_g, ffn_pre_g=ffn_pre_g, ffn_post_g=ffn_post_g, w_in=w_in,
             q_norm_g=q_norm_g, w_uq=w_uq, kv_norm_g=kv_norm_g, w_ukv=w_ukv, sg_ln_g=sg_ln_g, sg_ln_b=sg_ln_b, w_sp=w_sp,
             b_sp=b_sp, conv_w=conv_w, out_norm_g=out_norm_g, w_out=w_out, w_gate=w_gate, w_up=w_up, w_down=w_down)
    m = dict(mix_pre_g=m_mix_pre_g, mix_post_g=m_mix_post_g, ffn_pre_g=m_ffn_pre_g, ffn_post_g=m_ffn_post_g, w_in=m_w_in,
             q_norm_g=m_q_norm_g, w_uq=m_w_uq, kv_norm_g=m_kv_norm_g, w_ukv=m_w_ukv, sg_ln_g=m_sg_ln_g, sg_ln_b=m_sg_ln_b,
             w_sp=m_w_sp, b_sp=m_b_sp, conv_w=m_conv_w, out_norm_g=m_out_norm_g, w_out=m_w_out, w_gate=m_w_gate,
             w_up=m_w_up, w_down=m_w_down)
    v = dict(mix_pre_g=v_mix_pre_g, mix_post_g=v_mix_post_g, ffn_pre_g=v_ffn_pre_g, ffn_post_g=v_ffn_post_g, w_in=v_w_in,
             q_norm_g=v_q_norm_g, w_uq=v_w_uq, kv_norm_g=v_kv_norm_g, w_ukv=v_w_ukv, sg_ln_g=v_sg_ln_g, sg_ln_b=v_sg_ln_b,
             w_sp=v_w_sp, b_sp=v_b_sp, conv_w=v_conv_w, out_norm_g=v_out_norm_g, w_out=v_w_out, w_gate=v_w_gate,
             w_up=v_w_up, w_down=v_w_down)

    mix_pack = _to_pack(w, BF16, MIX_PIECES, conv=w["conv_w"])
    ffn_pack = _to_pack(w, BF16, FFN_PIECES)
    consts = dict(gm=jnp.asarray(np.kron(np.eye(SG_W // GROUP), np.full((GROUP, GROUP), 1.0 / GROUP)), BF16))
    smalls = []
    for l in range(nl):
        sp = {name: w[name][l].reshape(1, -1) for name, shape in SMALL if len(shape) == 1}
        sp["w_sp"] = w["w_sp"][l]
        sp["w_sp_t"] = jnp.swapaxes(w["w_sp"][l], 1, 2)
        sp["bias"] = jnp.repeat(w["b_sp"][l].T, GROUP, axis=1)
        smalls.append(sp)
    inv_freq = 1.0 / (ROPE_THETA ** (jnp.arange(0, ROPE // 2, dtype=F32) / (ROPE // 2)))
    inv = jnp.zeros((1, HEAD_PAD), F32).at[0, NOPE:NOPE + ROPE].set(jnp.concatenate([inv_freq, inv_freq]))
    tabs = rope_tables(positions.reshape(t, 1).astype(F32), inv)

    h = x[0]
    saved, layers = [], []
    mix_gathered = [all_gather(mix_pack[0])]
    for l in range(nl):
        layers.append(_mix_weights(mix_gathered[0]))
        h, s, mix_gathered = _layer_fwd(h, layers[l], smalls[l], tabs, consts, ffn_pack[l],
                                        (mix_pack[l + 1],) if l + 1 < nl else ())
        saved.append(s)
    sq, dh = loss_head(h, loss_target[0])
    loss = lax.psum(0.5 * sq[0, 0] / D, ("x", "y", "c"))

    got_ffn, got_mix, got_small = [None] * nl, [None] * nl, [None] * nl
    pending = ()
    for l in reversed(range(nl)):
        dh, new_pending, received = _layer_bwd(dh, saved[l], layers[l], smalls[l], tabs, consts, pending)
        got_ffn[l] = received[0]
        if pending:
            got_mix[l + 1], got_small[l + 1] = received[1:]
        pending = new_pending

    me = 4 * lax.axis_index("x") + 2 * lax.axis_index("y") + lax.axis_index("c")
    *ffn_new, got_mix[0], got_small[0] = sum_adamw(got_ffn, *[_to_pack(d, F32, FFN_PIECES) for d in (w, m, v)], 176,
                                                   carried=pending)
    mix_new = sum_adamw(got_mix, *[_to_pack(d, F32, MIX_PIECES) for d in (w, m, v)], 208)
    got_small = jnp.concatenate(got_small, axis=1)
    g_big, d_big, m_big, v_big = [{**_from_pack(a, FFN_PIECES), **_from_pack(b, MIX_PIECES)}
                                  for a, b in zip(ffn_new, mix_new)]

    def full_conv(a):
        return lax.dynamic_update_slice(jnp.zeros((nl, 3, CV_W), F32), a, (0, 0, me * (CV_W // N_DEV)))

    def small_pack(d):
        return _pack_small({**{name: d[name] for name, _ in SMALL}, "conv_w": full_conv(d["conv_w"])}, nl)

    g_small, d_small, m_small, v_small = [_unpack_small(p[0], nl) for p in
                                          sum_adamw([got_small], small_pack(w)[None], small_pack(m)[None],
                                                    small_pack(v)[None], 1152)]
    outs = []
    for big, small in ((g_big, g_small), (d_big, d_small), (m_big, m_small), (v_big, v_small)):
        for name in w:
            if name == "conv_w":
                outs.append(lax.dynamic_slice(small[name], (0, 0, me * (CV_W // N_DEV)), (nl, 3, CV_W // N_DEV)))
            elif name in small:
                outs.append(small[name])
            else:
                outs.append(big[name])
    return (loss, dh[None], *outs)
```

```python
import functools

import jax
import jax.numpy as jnp
import numpy as np
from jax import lax
from jax.experimental import pallas as pl
from jax.experimental.pallas import tpu as pltpu

F32 = jnp.float32
BF16 = jnp.bfloat16

D = 1024
Q_RANK = 384
KV_RANK = 256
ROPE = 32
HEADS = 8
NOPE = 64
V_DIM = 64
HEAD_PAD = 128
SG_W = 256
CV_W = 256
CHUNK = 128
GROUP = 64
D_FF = 2816
IN_W = 1952
Z_W = 2048
Z_CV, Z_MLA, Z_SG = 0, 768, 1536
EPS = 1e-6
ROPE_THETA = 10000.0
SCALE = (NOPE + ROPE) ** -0.5
LOG2E = 1.4426950408889634
SCALE_LOG2E = SCALE * LOG2E
NEG = -1e30
N_DEV = 8

ADAM_LR, ADAM_B1, ADAM_B2, ADAM_EPS, ADAM_WD, ADAM_STEP = 0.001, 0.9, 0.999, 1e-08, 0.01, 10

VMEM_LIMIT = 56 * 1024 * 1024

TM = 512
TM_FFN = 256
FFN_SLAB = 256
HALO = 16
TQ = 512
FWD_UNROLL = 2
FWD_HEADS = 2
TT = 2048


def _cp(sem, vmem=VMEM_LIMIT):
    return pltpu.CompilerParams(dimension_semantics=sem, vmem_limit_bytes=vmem)


def _whole():
    return pl.BlockSpec(memory_space=pltpu.VMEM)


def _mm(a, b):
    return jnp.dot(a, b, preferred_element_type=F32)


def _mm_nt(a, b):
    return lax.dot_general(a, b, (((1,), (1,)), ((), ())), preferred_element_type=F32)


def _mm_tn(a, b):
    return lax.dot_general(a, b, (((0,), (0,)), ((), ())), preferred_element_type=F32)


def _rms_fwd(x, g):
    r = lax.rsqrt(jnp.mean(x * x, axis=-1, keepdims=True) + EPS)
    xh = x * r
    return xh * g, xh, r


def _rms_bwd(xh, r, g, dy):
    dxh = dy * g
    dx = r * (dxh - xh * jnp.mean(dxh * xh, axis=-1, keepdims=True))
    dg = jnp.sum(dy * xh, axis=0, keepdims=True)
    return dx, dg


def _gmean(v, gm):
    hi = v.astype(BF16)
    lo = (v - hi.astype(F32)).astype(BF16)
    return _mm(hi, gm) + _mm(lo, gm)


def _gelu(x):
    c = np.float32(np.sqrt(2.0 / np.pi))
    u = c * (x + 0.044715 * x * x * x)
    t = jnp.tanh(u)
    return 0.5 * x * (1.0 + t), t


def _gelu_grad(x, t):
    c = np.float32(np.sqrt(2.0 / np.pi))
    return 0.5 * (1.0 + t) + 0.5 * x * (1.0 - t * t) * c * (1.0 + 3.0 * 0.044715 * x * x)


def _rope(t, ca, sb, sc):
    return t * ca + pltpu.roll(t, HEAD_PAD - 16, 1) * sb + pltpu.roll(t, 16, 1) * sc


def _rope_t(dt, ca, sb, sc):
    return dt * ca + pltpu.roll(dt * sb, 16, 1) + pltpu.roll(dt * sc, HEAD_PAD - 16, 1)


def _shift_down(y, k, head):
    out = pltpu.roll(y, k, 0)
    row = lax.broadcasted_iota(jnp.int32, y.shape, 0)
    for j in range(k):
        out = jnp.where(row == j, head[head.shape[0] - k + j:head.shape[0] - k + j + 1, :], out)
    return out


def _shift_up(y, k, tail):
    n = y.shape[0]
    out = pltpu.roll(y, n - k, 0)
    row = lax.broadcasted_iota(jnp.int32, y.shape, 0)
    for j in range(k):
        out = jnp.where(row == n - k + j, tail[j:j + 1, :], out)
    return out


def rope_tables(pos, inv):
    t = pos.shape[0]
    tm = min(TM, t)

    def body(pos_ref, inv_ref, ca_ref, sb_ref, sc_ref):
        ang = pos_ref[...] * inv_ref[...]
        c = jnp.cos(ang)
        s = jnp.sin(ang)
        lane = lax.broadcasted_iota(jnp.int32, ang.shape, 1)
        ca_ref[...] = jnp.where(lane < NOPE, 1.0, jnp.where(lane < NOPE + ROPE, c, 0.0))
        sb_ref[...] = jnp.where((lane >= NOPE) & (lane < NOPE + 16), -s, 0.0)
        sc_ref[...] = jnp.where((lane >= NOPE + 16) & (lane < NOPE + ROPE), s, 0.0)

    out = jax.ShapeDtypeStruct((t, HEAD_PAD), F32)
    blk = pl.BlockSpec((tm, HEAD_PAD), lambda i: (i, 0))
    return pl.pallas_call(
        body, name="rope_tables", grid=(t // tm,),
        in_specs=[pl.BlockSpec((tm, 1), lambda i: (i, 0)), pl.BlockSpec((1, HEAD_PAD), lambda i: (0, 0))],
        out_specs=[blk, blk, blk], out_shape=[out, out, out],
        compiler_params=_cp(("parallel",)),
    )(pos, inv)


def pre_in_fwd(x, g, w):
    t = x.shape[0]
    tm = min(TM, t)

    def body(x_ref, g_ref, w_ref, z_ref):
        h, _, _ = _rms_fwd(x_ref[...], g_ref[...])
        z_ref[...] = _mm_nt(h.astype(BF16), w_ref[...]).astype(BF16)

    return pl.pallas_call(
        body, name="pre_in_fwd", grid=(t // tm,),
        in_specs=[pl.BlockSpec((tm, D), lambda i: (i, 0)), _whole(), _whole()],
        out_specs=pl.BlockSpec((tm, Z_W), lambda i: (i, 0)),
        out_shape=jax.ShapeDtypeStruct((t, Z_W), BF16),
        compiler_params=_cp(("parallel",)),
    )(x, g, w)


def mla_proj_fwd(z, ca, sb, sc, gq, gkv, wuq, wukv):
    t = z.shape[0]
    tm = min(TM, t)

    def body(z_ref, ca_ref, sb_ref, sc_ref, gq_ref, gkv_ref, wuq_ref, wukv_ref, q_ref, k_ref, v_ref):
        z = z_ref[...].astype(F32)
        ca, sb, sc = ca_ref[...], sb_ref[...], sc_ref[...]
        cq, _, _ = _rms_fwd(z[:, :Q_RANK], gq_ref[...])
        ckv, _, _ = _rms_fwd(z[:, Q_RANK:Q_RANK + KV_RANK], gkv_ref[...])
        q = _mm_nt(cq.astype(BF16), wuq_ref[...])
        kv = _mm_nt(ckv.astype(BF16), wukv_ref[...])
        kr = _rope(pltpu.roll(z[:, Q_RANK + KV_RANK:], NOPE, 1), ca, sb, sc)
        for h in range(HEADS):
            lanes = slice(h * HEAD_PAD, (h + 1) * HEAD_PAD)
            q_ref[:, lanes] = _rope(q[:, lanes], ca, sb, sc).astype(BF16)
            k_ref[:, lanes] = (kv[:, lanes] + kr).astype(BF16)
        v_ref[...] = kv[:, HEADS * HEAD_PAD:].astype(BF16)

    tab = pl.BlockSpec((tm, HEAD_PAD), lambda i: (i, 0))
    return pl.pallas_call(
        body, name="mla_proj_fwd", grid=(t // tm,),
        in_specs=[pl.BlockSpec((tm, 768), lambda i: (i, 1)), tab, tab, tab, _whole(), _whole(), _whole(), _whole()],
        out_specs=[pl.BlockSpec((tm, HEADS * HEAD_PAD), lambda i: (i, 0)),
                   pl.BlockSpec((tm, HEADS * HEAD_PAD), lambda i: (i, 0)),
                   pl.BlockSpec((tm, HEADS * V_DIM), lambda i: (i, 0))],
        out_shape=[jax.ShapeDtypeStruct((t, HEADS * HEAD_PAD), BF16),
                   jax.ShapeDtypeStruct((t, HEADS * HEAD_PAD), BF16),
                   jax.ShapeDtypeStruct((t, HEADS * V_DIM), BF16)],
        compiler_params=_cp(("parallel",)),
    )(z, ca, sb, sc, gq, gkv, wuq, wukv)


def _each(stages, k):
    def run():
        for stage in stages:
            stage[k]()
    return run


def attn_fwd(q, k, v, carried=()):
    t = q.shape[0]
    tq = min(TQ, t)
    nq = t // tq
    hs = FWD_HEADS
    last_pair = HEADS // hs - 1
    n = len(carried)

    def body(*refs):
        q_ref, k_ref, v_ref = refs[:3]
        o_ref, lse_ref = refs[3 + n:5 + n]
        sems = refs[5 + 2 * n:]
        stages = [_gather_steps(refs[3 + a], refs[5 + n + a], *sems[3 * a:3 * a + 3]) for a in range(n)]
        if n:
            pair = pl.program_id(0)
            pl.when((pair == 0) & (pl.program_id(1) == 0))(_each(stages, 0))
            pl.when((pair == last_pair) & (pl.program_id(1) == 0))(_each(stages, 1))
        i = pl.program_id(1)
        row = lax.broadcasted_iota(jnp.int32, (tq, tq), 0)
        col = lax.broadcasted_iota(jnp.int32, (tq, tq), 1)
        head_lanes = [slice(h * HEAD_PAD, (h + 1) * HEAD_PAD) for h in range(hs)]
        pair_lanes = [slice(p * 2 * V_DIM, (p + 1) * 2 * V_DIM) for p in range(hs // 2)]

        def step(j, carry, masked):
            start = pl.multiple_of(j * tq, tq)
            out = []
            for h in range(hs):
                m, l, acc = carry[h]
                s = _mm_nt(q_ref[:, head_lanes[h]], k_ref[pl.ds(start, tq), head_lanes[h]])
                if masked:
                    s = jnp.where(col <= row, s, NEG)
                m_new = jnp.maximum(m, jnp.max(s, axis=-1, keepdims=True))
                p = jnp.exp2((s - m_new) * SCALE_LOG2E)
                alpha = jnp.exp2((m - m_new) * SCALE_LOG2E)
                l = alpha * l + jnp.sum(p, axis=-1, keepdims=True)
                acc = alpha * acc + _mm(p.astype(BF16), v_ref[pl.ds(start, tq), pair_lanes[h // 2]])
                out.append((m_new, l, acc))
            return tuple(out)

        init = (jnp.full((tq, 1), NEG, F32), jnp.zeros((tq, 1), F32), jnp.zeros((tq, 2 * V_DIM), F32))
        def trip(j, c):
            for u in range(FWD_UNROLL):
                c = step(FWD_UNROLL * j + u, c, False)
            return c

        carry = lax.fori_loop(0, i // FWD_UNROLL, trip, (init,) * hs)
        carry = lax.fori_loop(i - i % FWD_UNROLL, i, lambda j, c: step(j, c, False), carry)
        outs = []
        for h, (m, l, acc) in enumerate(step(i, carry, True)):
            outs.append(acc / l)
            lse_ref[:, head_lanes[h]] = jnp.broadcast_to(m * SCALE + jnp.log(l), (tq, HEAD_PAD))
        lane = lax.broadcasted_iota(jnp.int32, (tq, 2 * V_DIM), 1)
        for p in range(hs // 2):
            o_ref[:, pair_lanes[p]] = jnp.where(lane < V_DIM, outs[2 * p], outs[2 * p + 1])
        if n:
            pl.when((pl.program_id(0) == last_pair) & (i == nq - 1))(_each(stages, 2))

    hbm = pl.BlockSpec(memory_space=pl.ANY)
    return pl.pallas_call(
        body, name=f"attn_fwd_gather{n}" if n else "attn_fwd", grid=(HEADS // hs, nq),
        in_specs=[pl.BlockSpec((tq, hs * HEAD_PAD), lambda p, i: (i, p)),
                  pl.BlockSpec((t, hs * HEAD_PAD), lambda p, i: (0, p)),
                  pl.BlockSpec((t, hs * V_DIM), lambda p, i: (0, p))] + [hbm] * n,
        out_specs=[pl.BlockSpec((tq, hs * V_DIM), lambda p, i: (i, p)),
                   pl.BlockSpec((tq, hs * HEAD_PAD), lambda p, i: (i, p))] + [hbm] * n,
        out_shape=[jax.ShapeDtypeStruct((t, HEADS * V_DIM), F32), jax.ShapeDtypeStruct((t, HEADS * HEAD_PAD), F32)]
        + [jax.ShapeDtypeStruct((N_DEV,) + c.shape, c.dtype) for c in carried],
        scratch_shapes=_comm_sems() * n,
        compiler_params=_cp(("arbitrary", "arbitrary") if n else ("parallel", "parallel")),
    )(q, k, v, *carried)


def _sgu_fwd(zsg, gm, lng, lnb, wc_ref, bias, mixed_ref):
    uv, th = _gelu(zsg)
    u, v0 = uv[:, :SG_W], uv[:, SG_W:]
    vc = v0 - _gmean(v0, gm)
    r = lax.rsqrt(_gmean(vc * vc, gm) + EPS)
    vh = vc * r
    v = vh * lng + lnb
    lane = lax.broadcasted_iota(jnp.int32, (CHUNK, SG_W), 1)
    for c in range(zsg.shape[0] // CHUNK):
        rows = slice(c * CHUNK, (c + 1) * CHUNK)
        vb = v[rows].astype(BF16)
        mixed = bias
        for g in range(SG_W // GROUP):
            mixed = mixed + jnp.where(lane // GROUP == g, _mm(wc_ref[g], vb), 0.0)
        mixed_ref[rows, :] = mixed
    return u, v, vh, r, th


def _conv_fwd(zcv, halo, first, cw):
    gb, gc, hh = zcv[:, :CV_W], zcv[:, CV_W:2 * CV_W], zcv[:, 2 * CV_W:]
    y = gc * hh
    yh = jnp.where(first, 0.0, halo[:, CV_W:2 * CV_W] * halo[:, 2 * CV_W:])
    y1 = _shift_down(y, 1, yh)
    y2 = _shift_down(y, 2, yh)
    conv = y2 * cw[0:1, :] + y1 * cw[1:2, :] + y * cw[2:3, :]
    return gb * conv, conv, y, y1, y2


def _tril_bf16(w_ref, g):
    row = lax.broadcasted_iota(jnp.int32, (CHUNK, CHUNK), 0)
    col = lax.broadcasted_iota(jnp.int32, (CHUNK, CHUNK), 1)
    return jnp.where(col <= row, w_ref[g], 0.0).astype(BF16)


def mix_fwd(x, z, ya, gm, lng, lnb, wsp, bias, cw, gout, wout, gpost):
    t = x.shape[0]
    tm = min(TM, t)

    def body(x_ref, zcv_ref, halo_ref, zsg_ref, ya_ref, gm_ref, lng_ref, lnb_ref, wsp_ref, bias_ref, cw_ref,
             gout_ref, wout_ref, gpost_ref, x1_ref, wc_ref, mixed_ref):
        i = pl.program_id(0)
        for g in range(SG_W // GROUP):
            wc_ref[g] = _tril_bf16(wsp_ref, g)
        u, _, _, _, _ = _sgu_fwd(zsg_ref[...].astype(F32), gm_ref[...], lng_ref[...], lnb_ref[...], wc_ref, bias_ref[...],
                                 mixed_ref)
        yb = u * mixed_ref[...]
        yc, _, _, _, _ = _conv_fwd(zcv_ref[...].astype(F32), halo_ref[...].astype(F32), i == 0, cw_ref[...])
        gout = gout_ref[...]
        na, _, _ = _rms_fwd(ya_ref[...], gout[:, :512])
        nb, _, _ = _rms_fwd(yb, gout[:, 512:768])
        nc, _, _ = _rms_fwd(yc, gout[:, 768:])
        mix = jnp.concatenate([na, nb, nc], axis=1).astype(BF16)
        o, _, _ = _rms_fwd(_mm(mix, wout_ref[...]), gpost_ref[...])
        x1_ref[...] = x_ref[...] + o

    return pl.pallas_call(
        body, name="mix_fwd", grid=(t // tm,),
        in_specs=[pl.BlockSpec((tm, D), lambda i: (i, 0)),
                  pl.BlockSpec((tm, 768), lambda i: (i, 0)),
                  pl.BlockSpec((HALO, 768), lambda i: (jnp.maximum(i * (tm // HALO) - 1, 0), 0)),
                  pl.BlockSpec((tm, 512), lambda i: (i, 3)),
                  pl.BlockSpec((tm, 512), lambda i: (i, 0)),
                  _whole(), _whole(), _whole(), _whole(), _whole(), _whole(), _whole(), _whole(), _whole()],
        out_specs=pl.BlockSpec((tm, D), lambda i: (i, 0)),
        out_shape=jax.ShapeDtypeStruct((t, D), F32),
        scratch_shapes=[pltpu.VMEM((SG_W // GROUP, CHUNK, CHUNK), BF16), pltpu.VMEM((tm, SG_W), F32)],
        compiler_params=_cp(("arbitrary",)),
    )(x, z, z, z, ya, gm, lng, lnb, wsp, bias, cw, gout, wout, gpost)


def _sigmoid(a):
    return 1.0 / (1.0 + jnp.exp(-a))


FFN_SHARD = D_FF // N_DEV


def _load_ffn_weights(g_ref, wgu_ref, wd_ref, sems):
    copies = []
    for j in range(N_DEV):
        for p, (dst, base) in enumerate(((wgu_ref, 0), (wgu_ref, D_FF), (wd_ref, 0))):
            copies.append(pltpu.make_async_copy(g_ref.at[j, pl.ds(p * FFN_SHARD, FFN_SHARD)],
                                                dst.at[pl.ds(base + j * FFN_SHARD, FFN_SHARD)], sems.at[3 * j + p]))
    for cp in copies:
        cp.start()
    for cp in copies:
        cp.wait()


def _ffn_weight_scratch():
    return [pltpu.VMEM((2 * D_FF, D), BF16), pltpu.VMEM((D_FF, D), BF16), pltpu.SemaphoreType.DMA((3 * N_DEV,))]


def ffn_fwd(x1, gpre, gathered, gpost, carried=()):
    t = x1.shape[0]
    tm = min(TM_FFN, t)
    steps = t // tm
    n = len(carried)

    def body(*refs):
        x_ref, gpre_ref, g_ref, gpost_ref = refs[:4]
        x2_ref, f_ref = refs[4 + n:6 + n]
        wgu_ref, wd_ref, sems = refs[6 + 2 * n:9 + 2 * n]
        comm_sems = refs[9 + 2 * n:]
        stages = [_gather_steps(refs[4 + a], refs[6 + n + a], *comm_sems[3 * a:3 * a + 3]) for a in range(n)]
        if n:
            pl.when(pl.program_id(0) == 0)(_each(stages, 0))
            pl.when(pl.program_id(0) == (3 * steps) // 4)(_each(stages, 1))

        @pl.when(pl.program_id(0) == 0)
        def _():
            _load_ffn_weights(g_ref, wgu_ref, wd_ref, sems)

        x = x_ref[...]
        h, _, _ = _rms_fwd(x, gpre_ref[...])
        ab = _mm_nt(h.astype(BF16), wgu_ref[...])
        a, b = ab[:, :D_FF], ab[:, D_FF:]
        s = a * _sigmoid(a) * b
        f = _mm(s.astype(BF16), wd_ref[...])
        f_ref[...] = f
        x2_ref[...] = x + _rms_fwd(f, gpost_ref[...])[0]
        if n:
            pl.when(pl.program_id(0) == steps - 1)(_each(stages, 2))

    row = pl.BlockSpec((tm, D), lambda i: (i, 0))
    hbm = pl.BlockSpec(memory_space=pl.ANY)
    return pl.pallas_call(
        body, name=f"ffn_fwd_gather{n}" if n else "ffn_fwd", grid=(steps,),
        in_specs=[row, _whole(), hbm, _whole()] + [hbm] * n,
        out_specs=[row, row] + [hbm] * n,
        out_shape=[jax.ShapeDtypeStruct((t, D), F32), jax.ShapeDtypeStruct((t, D), F32)]
        + [jax.ShapeDtypeStruct((N_DEV,) + c.shape, c.dtype) for c in carried],
        scratch_shapes=_ffn_weight_scratch() + _comm_sems() * n,
        compiler_params=_cp(("arbitrary",)),
    )(x1, gpre, gathered, gpost, *carried)


def loss_head(y, target):
    t = y.shape[0]
    tm = min(TM, t)

    def body(y_ref, t_ref, loss_ref, dy_ref):
        @pl.when(pl.program_id(0) == 0)
        def _():
            loss_ref[...] = jnp.zeros_like(loss_ref)

        e = y_ref[...] - t_ref[...]
        dy_ref[...] = e * (1.0 / D)
        loss_ref[...] += jnp.sum(jnp.sum(e * e, axis=-1, keepdims=True), axis=0, keepdims=True)

    return pl.pallas_call(
        body, name="loss_head", grid=(t // tm,),
        in_specs=[pl.BlockSpec((tm, D), lambda i: (i, 0)), pl.BlockSpec((tm, D), lambda i: (i, 0))],
        out_specs=[pl.BlockSpec((1, 128), lambda i: (0, 0)), pl.BlockSpec((tm, D), lambda i: (i, 0))],
        out_shape=[jax.ShapeDtypeStruct((1, 128), F32), jax.ShapeDtypeStruct((t, D), F32)],
        compiler_params=_cp(("arbitrary",)),
    )(y, target)


def _acc(ref, first, val):
    @pl.when(first)
    def _():
        ref[...] = val

    @pl.when(jnp.logical_not(first))
    def _():
        ref[...] += val


def ffn_bwd(x1, f, dx2, gpre, gathered, gpost, carried=()):
    t = x1.shape[0]
    tm = min(TM_FFN, t)
    steps = t // tm
    n = len(carried)

    def body(*refs):
        x_ref, f_ref, dx2_ref, gpre_ref, g_ref, gpost_ref = refs[:6]
        dx1_ref, h_ref, dab_ref, s_ref, df_ref, dgpre_ref, dgpost_ref = refs[6 + n:13 + n]
        ab_ref, ds_ref, wgu_ref, wd_ref, sems = refs[13 + 2 * n:18 + 2 * n]
        comm_sems = refs[18 + 2 * n:]
        stages = [_exchange_steps(refs[6 + a], refs[13 + n + a], carried[a][1], *comm_sems[3 * a:3 * a + 3])
                  for a in range(n)]
        first = pl.program_id(0) == 0
        if n:
            pl.when(first)(_each(stages, 0))

        @pl.when(first)
        def _():
            _load_ffn_weights(g_ref, wgu_ref, wd_ref, sems)

        dx2 = dx2_ref[...]
        gpre, gpost = gpre_ref[...], gpost_ref[...]
        h, xh, rx = _rms_fwd(x_ref[...], gpre)
        h_ref[...] = h.astype(BF16)
        ab_ref[...] = _mm_nt(h_ref[...], wgu_ref[...])
        for c in range(0, D_FF, FFN_SLAB):
            a, b = ab_ref[:, c:c + FFN_SLAB], ab_ref[:, D_FF + c:D_FF + c + FFN_SLAB]
            s_ref[:, c:c + FFN_SLAB] = (a * _sigmoid(a) * b).astype(BF16)
        _, fh, rf = _rms_fwd(f_ref[...], gpost)
        df, dgpost = _rms_bwd(fh, rf, gpost, dx2)
        df_ref[...] = df.astype(BF16)
        ds_ref[...] = _mm_nt(df_ref[...], wd_ref[...])
        for c in range(0, D_FF, FFN_SLAB):
            a, b = ab_ref[:, c:c + FFN_SLAB], ab_ref[:, D_FF + c:D_FF + c + FFN_SLAB]
            ds = ds_ref[:, c:c + FFN_SLAB]
            sg = _sigmoid(a)
            dab_ref[:, c:c + FFN_SLAB] = (ds * b * (sg * (1.0 + a * (1.0 - sg)))).astype(BF16)
            dab_ref[:, D_FF + c:D_FF + c + FFN_SLAB] = (ds * (a * sg)).astype(BF16)
        dx, dgpre = _rms_bwd(xh, rx, gpre, _mm(dab_ref[...], wgu_ref[...]))
        dx1_ref[...] = dx2 + dx
        _acc(dgpre_ref, first, dgpre)
        _acc(dgpost_ref, first, dgpost)
        if n:
            pl.when(pl.program_id(0) == steps - 1)(_each(stages, 1))

    row = lambda w: pl.BlockSpec((tm, w), lambda i: (i, 0))
    vec = pl.BlockSpec((1, D), lambda i: (0, 0))
    hbm = pl.BlockSpec(memory_space=pl.ANY)
    return pl.pallas_call(
        body, name=f"ffn_bwd_exchange{n}" if n else "ffn_bwd", grid=(steps,),
        in_specs=[row(D), row(D), row(D), _whole(), hbm, _whole()] + [hbm] * n,
        out_specs=[row(D), row(D), row(2 * D_FF), row(D_FF), row(D), vec, vec] + [hbm] * n,
        out_shape=[jax.ShapeDtypeStruct((t, D), F32), jax.ShapeDtypeStruct((t, D), BF16),
                   jax.ShapeDtypeStruct((t, 2 * D_FF), BF16), jax.ShapeDtypeStruct((t, D_FF), BF16),
                   jax.ShapeDtypeStruct((t, D), BF16), jax.ShapeDtypeStruct((1, D), F32),
                   jax.ShapeDtypeStruct((1, D), F32)]
        + [jax.ShapeDtypeStruct(src.shape if scatter else (N_DEV,) + src.shape, src.dtype) for src, scatter in carried],
        scratch_shapes=[pltpu.VMEM((tm, 2 * D_FF), F32), pltpu.VMEM((tm, D_FF), F32)] + _ffn_weight_scratch()
        + _comm_sems() * n,
        compiler_params=_cp(("arbitrary",)),
    )(x1, f, dx2, gpre, gathered, gpost, *[src for src, _ in carried])


def atb(a, b, tk):
    t, k = a.shape
    n = b.shape[1]
    tt = min(TT, t)
    tk = min(tk, k)
    steps = t // tt

    def body(a_ref, b_ref, o_ref, acc_ref):
        i = pl.program_id(1)
        _acc(acc_ref, i == 0, _mm_tn(a_ref[...], b_ref[...]))

        @pl.when(i == steps - 1)
        def _():
            o_ref[...] = acc_ref[...].astype(BF16)

    return pl.pallas_call(
        body, name="atb", grid=(k // tk, steps),
        in_specs=[pl.BlockSpec((tt, tk), lambda j, i: (i, j)), pl.BlockSpec((tt, n), lambda j, i: (i, 0))],
        out_specs=pl.BlockSpec((tk, n), lambda j, i: (j, 0)),
        out_shape=jax.ShapeDtypeStruct((k, n), BF16),
        scratch_shapes=[pltpu.VMEM((tk, n), F32)],
        compiler_params=_cp(("parallel", "arbitrary")),
    )(a, b)


FFN_TILE_SHARDS = 4


def atb_ffn_chunks(a, b, first_piece, chunks=None):
    t, k = a.shape
    tt = min(TT, t)
    tk = FFN_TILE_SHARDS * FFN_SHARD
    steps = t // tt
    per_piece = N_DEV // FFN_TILE_SHARDS

    def body(*refs):
        a_ref, b_ref, o_ref, acc_ref = refs[0], refs[1], refs[-2], refs[-1]
        i = pl.program_id(1)
        _acc(acc_ref, i == 0, _mm_tn(a_ref[...], b_ref[...]))

        @pl.when(i == steps - 1)
        def _():
            for d in range(FFN_TILE_SHARDS):
                o_ref[d, 0] = acc_ref[d * FFN_SHARD:(d + 1) * FFN_SHARD, :].astype(BF16)

    hbm = pl.BlockSpec(memory_space=pl.ANY)
    return pl.pallas_call(
        body, name="atb_ffn_chunks", grid=(k // tk, steps),
        in_specs=[pl.BlockSpec((tt, tk), lambda j, i: (i, j)), pl.BlockSpec((tt, D), lambda j, i: (i, 0))]
        + ([] if chunks is None else [hbm]),
        out_specs=pl.BlockSpec((FFN_TILE_SHARDS, 1, FFN_SHARD, D),
                               lambda j, i: (j % per_piece, first_piece + j // per_piece, 0, 0)),
        out_shape=jax.ShapeDtypeStruct((N_DEV, len(FFN_PIECES), FFN_SHARD, D), BF16),
        input_output_aliases={} if chunks is None else {2: 0},
        scratch_shapes=[pltpu.VMEM((tk, D), F32)],
        compiler_params=_cp(("parallel", "arbitrary")),
    )(a, b, *([] if chunks is None else [chunks]))


def mix_bwd(dx1, z, ya, gm, lng, lnb, wsp, wspt, bias, cw, gout, wout, gpost):
    t = dx1.shape[0]
    tm = min(TM, t)
    steps = t // tm
    ng = SG_W // GROUP

    def body(dx1_ref, zcv_ref, halo_ref, zsg_ref, ya_ref, gm_ref, lng_ref, lnb_ref, wsp_ref, wspt_ref, bias_ref,
             cw_ref, gout_ref, wout_ref, gpost_ref,
             dya_ref, dyc_ref, dzsg_ref, dwout_ref, dgpost_ref, dgout_ref, dlng_ref, dlnb_ref, dwsp_ref,
             dbias_ref, wc_ref, wct_ref, mixed_ref, dv_ref, dwout_acc):
        i = pl.program_id(0)
        first = i == 0
        gm = gm_ref[...]
        for g in range(ng):
            wc_ref[g] = _tril_bf16(wsp_ref, g)
            wct_ref[g] = jnp.where(
                lax.broadcasted_iota(jnp.int32, (CHUNK, CHUNK), 0) <= lax.broadcasted_iota(jnp.int32, (CHUNK, CHUNK), 1),
                wspt_ref[g], 0.0).astype(BF16)
        zsg = zsg_ref[...].astype(F32)
        lng = lng_ref[...]
        u, v, vh, r, th = _sgu_fwd(zsg, gm, lng, lnb_ref[...], wc_ref, bias_ref[...], mixed_ref)
        mixed = mixed_ref[...]
        yb = u * mixed
        yc, _, _, _, _ = _conv_fwd(zcv_ref[...].astype(F32), halo_ref[...].astype(F32), first, cw_ref[...])
        gout, gpost = gout_ref[...], gpost_ref[...]
        ga, gb_, gc_ = gout[:, :512], gout[:, 512:768], gout[:, 768:]
        na, yah, ra = _rms_fwd(ya_ref[...], ga)
        nb, ybh, rb = _rms_fwd(yb, gb_)
        nc, ych, rc = _rms_fwd(yc, gc_)
        mix = jnp.concatenate([na, nb, nc], axis=1).astype(BF16)
        _, oh, ro = _rms_fwd(_mm(mix, wout_ref[...]), gpost)
        do, dgpost = _rms_bwd(oh, ro, gpost, dx1_ref[...])
        dob = do.astype(BF16)
        dmix = _mm_nt(dob, wout_ref[...])
        dya, dga = _rms_bwd(yah, ra, ga, dmix[:, :512])
        dyb, dgb = _rms_bwd(ybh, rb, gb_, dmix[:, 512:768])
        dyc, dgc = _rms_bwd(ych, rc, gc_, dmix[:, 768:])
        dya_ref[...] = dya
        dyc_ref[...] = dyc
        _acc(dwout_acc, first, _mm_tn(mix, dob))

        @pl.when(i == steps - 1)
        def _():
            dwout_ref[...] = dwout_acc[...].astype(BF16)

        _acc(dgpost_ref, first, dgpost)
        _acc(dgout_ref, first, jnp.concatenate([dga, dgb, dgc], axis=1))
        du = dyb * mixed
        dmixed = dyb * u
        lane = lax.broadcasted_iota(jnp.int32, (CHUNK, SG_W), 1)
        row = lax.broadcasted_iota(jnp.int32, (CHUNK, CHUNK), 0)
        col = lax.broadcasted_iota(jnp.int32, (CHUNK, CHUNK), 1)
        dbias = jnp.zeros((CHUNK, SG_W), F32)
        dw = [jnp.zeros((CHUNK, CHUNK), F32) for _ in range(ng)]
        for c in range(tm // CHUNK):
            rows = slice(c * CHUNK, (c + 1) * CHUNK)
            dm = dmixed[rows]
            dbias = dbias + dm
            dmb = dm.astype(BF16)
            vb = v[rows].astype(BF16)
            dvc = jnp.zeros((CHUNK, SG_W), F32)
            for g in range(ng):
                in_g = lane // GROUP == g
                dvc = dvc + jnp.where(in_g, _mm(wct_ref[g], dmb), 0.0)
                dw[g] = dw[g] + _mm_nt(jnp.where(in_g, dmb, jnp.zeros_like(dmb)), vb)
            dv_ref[rows, :] = dvc
        for g in range(ng):
            dwg = jnp.where(col <= row, dw[g], 0.0)

            @pl.when(first)
            def _():
                dwsp_ref[g] = dwg

            @pl.when(jnp.logical_not(first))
            def _():
                dwsp_ref[g] += dwg
        _acc(dbias_ref, first, _gmean(dbias, gm) * GROUP)
        dv = dv_ref[...]
        _acc(dlng_ref, first, jnp.sum(dv * vh, axis=0, keepdims=True))
        _acc(dlnb_ref, first, jnp.sum(dv, axis=0, keepdims=True))
        dvh = dv * lng
        dv0 = r * (dvh - _gmean(dvh, gm) - vh * _gmean(dvh * vh, gm))
        dzsg_ref[...] = (jnp.concatenate([du, dv0], axis=1) * _gelu_grad(zsg, th)).astype(BF16)

    row_ = lambda w: pl.BlockSpec((tm, w), lambda i: (i, 0))
    vec = lambda w: pl.BlockSpec((1, w), lambda i: (0, 0))
    return pl.pallas_call(
        body, name="mix_bwd", grid=(steps,),
        in_specs=[row_(D),
                  pl.BlockSpec((tm, 768), lambda i: (i, 0)),
                  pl.BlockSpec((HALO, 768), lambda i: (jnp.maximum(i * (tm // HALO) - 1, 0), 0)),
                  pl.BlockSpec((tm, 512), lambda i: (i, 3)),
                  row_(512),
                  _whole(), _whole(), _whole(), _whole(), _whole(), _whole(), _whole(), _whole(), _whole(), _whole()],
        out_specs=[row_(512), row_(CV_W), row_(512), pl.BlockSpec((D, D), lambda i: (0, 0)), vec(D), vec(D), vec(SG_W),
                   vec(SG_W), pl.BlockSpec((ng, CHUNK, CHUNK), lambda i: (0, 0, 0)),
                   pl.BlockSpec((CHUNK, SG_W), lambda i: (0, 0))],
        out_shape=[jax.ShapeDtypeStruct((t, 512), F32), jax.ShapeDtypeStruct((t, CV_W), F32),
                   jax.ShapeDtypeStruct((t, 512), BF16), jax.ShapeDtypeStruct((D, D), BF16),
                   jax.ShapeDtypeStruct((1, D), F32),
                   jax.ShapeDtypeStruct((1, D), F32), jax.ShapeDtypeStruct((1, SG_W), F32),
                   jax.ShapeDtypeStruct((1, SG_W), F32), jax.ShapeDtypeStruct((ng, CHUNK, CHUNK), F32),
                   jax.ShapeDtypeStruct((CHUNK, SG_W), F32)],
        scratch_shapes=[pltpu.VMEM((ng, CHUNK, CHUNK), BF16), pltpu.VMEM((ng, CHUNK, CHUNK), BF16),
                        pltpu.VMEM((tm, SG_W), F32), pltpu.VMEM((tm, SG_W), F32), pltpu.VMEM((D, D), F32)],
        compiler_params=_cp(("arbitrary",)),
    )(dx1, z, z, z, ya, gm, lng, lnb, wsp, wspt, bias, cw, gout, wout, gpost)


def conv_bwd(dyc, z, cw):
    t = dyc.shape[0]
    tm = min(TM, t)
    hb = tm // 8
    last_blk = t // 8 - 1

    def body(dyc_ref, dyct_ref, zcv_ref, head_ref, tail_ref, cw_ref, dz_ref, dcw_ref):
        i = pl.program_id(0)
        first = i == 0
        last = i == pl.num_programs(0) - 1
        cw = cw_ref[...]
        zcv = zcv_ref[...].astype(F32)
        gb, gc, hh = zcv[:, :CV_W], zcv[:, CV_W:2 * CV_W], zcv[:, 2 * CV_W:]
        _, conv, y, y1, y2 = _conv_fwd(zcv, head_ref[...].astype(F32), first, cw)
        dyc = dyc_ref[...]
        dconv = dyc * gb
        tail = jnp.where(last, 0.0, dyct_ref[...] * tail_ref[:8, :CV_W].astype(F32))
        d1 = _shift_up(dconv, 1, tail)
        d2 = _shift_up(dconv, 2, tail)
        dy = dconv * cw[2:3, :] + d1 * cw[1:2, :] + d2 * cw[0:1, :]
        dz_ref[...] = jnp.concatenate([dyc * conv, dy * hh, dy * gc], axis=1).astype(BF16)
        tap = lax.broadcasted_iota(jnp.int32, (8, CV_W), 0)
        dcw = jnp.where(tap == 0, jnp.sum(dconv * y2, axis=0, keepdims=True),
                        jnp.where(tap == 1, jnp.sum(dconv * y1, axis=0, keepdims=True),
                                  jnp.where(tap == 2, jnp.sum(dconv * y, axis=0, keepdims=True), 0.0)))
        _acc(dcw_ref, first, dcw)

    return pl.pallas_call(
        body, name="conv_bwd", grid=(t // tm,),
        in_specs=[pl.BlockSpec((tm, CV_W), lambda i: (i, 0)),
                  pl.BlockSpec((8, CV_W), lambda i: (jnp.minimum((i + 1) * hb, last_blk), 0)),
                  pl.BlockSpec((tm, 768), lambda i: (i, 0)),
                  pl.BlockSpec((HALO, 768), lambda i: (jnp.maximum(i * (tm // HALO) - 1, 0), 0)),
                  pl.BlockSpec((HALO, 768), lambda i: (jnp.minimum((i + 1) * (tm // HALO), t // HALO - 1), 0)),
                  _whole()],
        out_specs=[pl.BlockSpec((tm, 768), lambda i: (i, 0)), pl.BlockSpec((8, CV_W), lambda i: (0, 0))],
        out_shape=[jax.ShapeDtypeStruct((t, 768), BF16), jax.ShapeDtypeStruct((8, CV_W), F32)],
        compiler_params=_cp(("arbitrary",)),
    )(dyc, dyc, z, z, z, cw)


def attn_bwd(q, k, v, o, lse, do, carried=()):
    t = q.shape[0]
    tq = min(TQ, t)
    nq = t // tq
    last_pair = HEADS // 2 - 1
    n = len(carried)

    def body(*refs):
        j = pl.program_id(1)
        q_ref, k_ref, v_ref, o_ref, lse_ref, do_ref = refs[:6]
        dq_out_ref, dk_ref, dv_ref = refs[6 + n:9 + n]
        dq_ref = refs[9 + 2 * n]
        sems = refs[10 + 2 * n:]
        stages = [_exchange_steps(refs[6 + a], refs[9 + n + a], carried[a][1], *sems[3 * a:3 * a + 3]) for a in range(n)]
        if n:
            pl.when((pl.program_id(0) == 0) & (j == 0))(_each(stages, 0))

        @pl.when(j == 0)
        def _():
            dq_ref[...] = jnp.zeros_like(dq_ref)

        row = lax.broadcasted_iota(jnp.int32, (tq, tq), 0)
        col = lax.broadcasted_iota(jnp.int32, (tq, tq), 1)
        vlane = lax.broadcasted_iota(jnp.int32, (tq, 2 * V_DIM), 1)
        head_lanes = [slice(h * HEAD_PAD, (h + 1) * HEAD_PAD) for h in range(2)]

        def step(i, carry, masked):
            start = pl.multiple_of(i * tq, tq)
            do_blk = do_ref[pl.ds(start, tq), :]
            o_blk = o_ref[pl.ds(start, tq), :]
            vb = v_ref[...]
            dks, dv_acc = [], carry[2]
            for h in range(2):
                lanes = head_lanes[h]
                qb = q_ref[pl.ds(start, tq), lanes]
                kb = k_ref[:, lanes]
                dob = jnp.where((vlane // V_DIM) == h, do_blk, 0.0)
                delta = jnp.sum(dob * o_blk, axis=-1, keepdims=True)
                lse2 = lse_ref[pl.ds(start, tq), lanes][:, 0:1] * LOG2E
                s = _mm_nt(qb, kb)
                if masked:
                    s = jnp.where(col <= row, s, NEG)
                p = jnp.exp2(s * SCALE_LOG2E - lse2)
                dob16 = dob.astype(BF16)
                dp = _mm_nt(dob16, vb)
                ds = (p * (dp - delta) * SCALE).astype(BF16)
                dv_acc = dv_acc + _mm_tn(p.astype(BF16), dob16)
                dks.append(carry[h] + _mm_tn(ds, qb))
                dq_ref[pl.ds(start, tq), lanes] += _mm(ds, kb)
            return dks[0], dks[1], dv_acc

        zero = jnp.zeros((tq, HEAD_PAD), F32)
        carry = step(j, (zero, zero, jnp.zeros((tq, 2 * V_DIM), F32)), True)
        rest = nq - 1 - j
        carry = lax.fori_loop(0, rest // 2, lambda u, c: step(j + 2 + 2 * u, step(j + 1 + 2 * u, c, False), False), carry)
        dk0, dk1, dv_acc = lax.fori_loop(0, rest % 2, lambda _, c: step(nq - 1, c, False), carry)
        dk_ref[:, head_lanes[0]] = dk0.astype(BF16)
        dk_ref[:, head_lanes[1]] = dk1.astype(BF16)
        dv_ref[...] = dv_acc.astype(BF16)

        @pl.when(j == nq - 1)
        def _():
            dq_out_ref[...] = dq_ref[...].astype(BF16)

        if n:
            pl.when((pl.program_id(0) == last_pair) & (j == nq - 1))(_each(stages, 1))

    hbm = pl.BlockSpec(memory_space=pl.ANY)
    return pl.pallas_call(
        body, name=f"attn_bwd_exchange{n}" if n else "attn_bwd", grid=(HEADS // 2, nq),
        in_specs=[pl.BlockSpec((t, 2 * HEAD_PAD), lambda p, j: (0, p)),
                  pl.BlockSpec((tq, 2 * HEAD_PAD), lambda p, j: (j, p)),
                  pl.BlockSpec((tq, 2 * V_DIM), lambda p, j: (j, p)),
                  pl.BlockSpec((t, 2 * V_DIM), lambda p, j: (0, p)),
                  pl.BlockSpec((t, 2 * HEAD_PAD), lambda p, j: (0, p)),
                  pl.BlockSpec((t, 2 * V_DIM), lambda p, j: (0, p))] + [hbm] * n,
        out_specs=[pl.BlockSpec((t, 2 * HEAD_PAD), lambda p, j: (0, p)),
                   pl.BlockSpec((tq, 2 * HEAD_PAD), lambda p, j: (j, p)),
                   pl.BlockSpec((tq, 2 * V_DIM), lambda p, j: (j, p))] + [hbm] * n,
        out_shape=[jax.ShapeDtypeStruct((t, HEADS * HEAD_PAD), BF16), jax.ShapeDtypeStruct((t, HEADS * HEAD_PAD), BF16),
                   jax.ShapeDtypeStruct((t, HEADS * V_DIM), BF16)]
        + [jax.ShapeDtypeStruct(src.shape if scatter else (N_DEV,) + src.shape, src.dtype) for src, scatter in carried],
        scratch_shapes=[pltpu.VMEM((t, 2 * HEAD_PAD), F32)] + _comm_sems() * n,
        compiler_params=_cp(("arbitrary", "arbitrary") if n else ("parallel", "arbitrary")),
    )(q, k, v, o, lse, do, *[src for src, _ in carried])


def mla_proj_bwd(dq, dk, dv, z, ca, sb, sc, gq, gkv, wuq, wukv):
    t = z.shape[0]
    tm = min(TM, t)

    def body(dq_ref, dk_ref, dv_ref, z_ref, ca_ref, sb_ref, sc_ref, gq_ref, gkv_ref, wuq_ref, wukv_ref,
             dz_ref, cq_ref, ckv_ref, dqp_ref, dkvp_ref, dgq_ref, dgkv_ref):
        first = pl.program_id(0) == 0
        z = z_ref[...].astype(F32)
        ca, sb, sc = ca_ref[...], sb_ref[...], sc_ref[...]
        gq, gkv = gq_ref[...], gkv_ref[...]
        cq, cqh, rq = _rms_fwd(z[:, :Q_RANK], gq)
        ckv, ckvh, rkv = _rms_fwd(z[:, Q_RANK:Q_RANK + KV_RANK], gkv)
        lane = lax.broadcasted_iota(jnp.int32, (tm, HEAD_PAD), 1)
        dkr = jnp.zeros((tm, HEAD_PAD), F32)
        for h in range(HEADS):
            lanes = slice(h * HEAD_PAD, (h + 1) * HEAD_PAD)
            dqp_ref[:, lanes] = _rope_t(dq_ref[:, lanes].astype(F32), ca, sb, sc).astype(BF16)
            dkh = dk_ref[:, lanes].astype(F32)
            dkr = dkr + dkh
            dkvp_ref[:, lanes] = jnp.where(lane < NOPE, dkh, 0.0).astype(BF16)
        dkvp_ref[:, HEADS * HEAD_PAD:] = dv_ref[...].astype(BF16)
        dkr = pltpu.roll(_rope_t(jnp.where(lane >= NOPE, dkr, 0.0), ca, sb, sc), HEAD_PAD - NOPE, 1)
        dkr = jnp.where(lane < ROPE, dkr, 0.0)
        dcq = _mm(dqp_ref[...], wuq_ref[...])
        dckv = _mm(dkvp_ref[...], wukv_ref[...])
        dzq, dgq = _rms_bwd(cqh, rq, gq, dcq)
        dzkv, dgkv = _rms_bwd(ckvh, rkv, gkv, dckv)
        dz_ref[...] = jnp.concatenate([dzq, dzkv, dkr], axis=1).astype(BF16)
        cq_ref[...] = cq.astype(BF16)
        ckv_ref[...] = ckv.astype(BF16)
        _acc(dgq_ref, first, dgq)
        _acc(dgkv_ref, first, dgkv)

    row = lambda w: pl.BlockSpec((tm, w), lambda i: (i, 0))
    vec = lambda w: pl.BlockSpec((1, w), lambda i: (0, 0))
    return pl.pallas_call(
        body, name="mla_proj_bwd", grid=(t // tm,),
        in_specs=[row(1024), row(1024), row(512), pl.BlockSpec((tm, 768), lambda i: (i, 1)),
                  row(HEAD_PAD), row(HEAD_PAD), row(HEAD_PAD), _whole(), _whole(), _whole(), _whole()],
        out_specs=[row(768), row(Q_RANK), row(KV_RANK), row(1024), row(1536), vec(Q_RANK), vec(KV_RANK)],
        out_shape=[jax.ShapeDtypeStruct((t, 768), BF16), jax.ShapeDtypeStruct((t, Q_RANK), BF16),
                   jax.ShapeDtypeStruct((t, KV_RANK), BF16), jax.ShapeDtypeStruct((t, 1024), BF16),
                   jax.ShapeDtypeStruct((t, 1536), BF16), jax.ShapeDtypeStruct((1, Q_RANK), F32),
                   jax.ShapeDtypeStruct((1, KV_RANK), F32)],
        compiler_params=_cp(("arbitrary",)),
    )(dq, dk, dv, z, ca, sb, sc, gq, gkv, wuq, wukv)


def pre_in_bwd(x, dx1, dzcv, dzmla, dzsg, g, w):
    t = x.shape[0]
    tm = min(TM, t)
    steps = t // tm

    def body(x_ref, dx1_ref, dzcv_ref, dzmla_ref, dzsg_ref, g_ref, w_ref, dx_ref, dw_ref, dg_ref, acc_ref):
        first = pl.program_id(0) == 0
        g = g_ref[...]
        h, xh, r = _rms_fwd(x_ref[...], g)
        dz = jnp.concatenate([dzcv_ref[...], dzmla_ref[...], dzsg_ref[...]], axis=1)
        dx, dg = _rms_bwd(xh, r, g, _mm(dz, w_ref[...]))
        dx_ref[...] = dx1_ref[...] + dx
        _acc(acc_ref, first, _mm_tn(dz, h.astype(BF16)))
        _acc(dg_ref, first, dg)

        @pl.when(pl.program_id(0) == steps - 1)
        def _():
            dw_ref[...] = acc_ref[...].astype(BF16)

    row = lambda w_: pl.BlockSpec((tm, w_), lambda i: (i, 0))
    return pl.pallas_call(
        body, name="pre_in_bwd", grid=(steps,),
        in_specs=[row(D), row(D), row(768), row(768), row(512), _whole(), _whole()],
        out_specs=[row(D), pl.BlockSpec((Z_W, D), lambda i: (0, 0)), pl.BlockSpec((1, D), lambda i: (0, 0))],
        out_shape=[jax.ShapeDtypeStruct((t, D), F32), jax.ShapeDtypeStruct((Z_W, D), BF16),
                   jax.ShapeDtypeStruct((1, D), F32)],
        scratch_shapes=[pltpu.VMEM((Z_W, D), F32)],
        compiler_params=_cp(("arbitrary",)),
    )(x, dx1, dzcv, dzmla, dzsg, g, w)


MESH = pl.DeviceIdType.MESH


def _place():
    return lax.axis_index("x"), lax.axis_index("y"), lax.axis_index("c")


def _comm_sems():
    return [pltpu.SemaphoreType.DMA((7,)), pltpu.SemaphoreType.DMA((7,)), pltpu.SemaphoreType.DMA]


def _gather_steps(x_ref, out_ref, send_sems, recv_sems, local_sem):
    x, y, c = _place()
    me, sibling = (x, y, c), (x, y, 1 - c)
    chips = [(1 - x, y), (x, 1 - y), (1 - x, 1 - y)]

    def slot(px, py, pc):
        return out_ref.at[4 * px + 2 * py + pc]

    def copy(k, blk, to, src=None):
        return pltpu.make_async_remote_copy(
            src_ref=slot(*blk) if src is None else src, dst_ref=slot(*blk),
            send_sem=send_sems.at[k], recv_sem=recv_sems.at[k], device_id=to, device_id_type=MESH)

    mine = pltpu.make_async_copy(x_ref, slot(*me), local_sem)
    first = [copy(0, me, sibling, src=x_ref)] + [copy(1 + j, me, (*chip, c), src=x_ref) for j, chip in enumerate(chips)]
    passed = [copy(4 + j, (*chip, c), sibling) for j, chip in enumerate(chips)]

    def start():
        mine.start()
        for cp in first:
            cp.start()

    def forward():
        for j, chip in enumerate(chips):
            copy(1 + j, (*chip, c), me).wait_recv()
            passed[j].start()

    def finish():
        copy(0, sibling, me).wait_recv()
        for j, chip in enumerate(chips):
            copy(4 + j, (*chip, 1 - c), me).wait_recv()
        for cp in first + passed:
            cp.wait_send()
        mine.wait()

    return start, forward, finish


def _exchange_steps(src_ref, out_ref, scatter, send_sems, recv_sems, local_sem):
    x, y, c = _place()
    me = 4 * x + 2 * y + c
    own = pltpu.make_async_copy(src_ref.at[me] if scatter else src_ref, out_ref.at[me], local_sem)
    copies = []
    for k in range(1, N_DEV):
        px = 1 - x if k & 4 else x
        py = 1 - y if k & 2 else y
        pc = 1 - c if k & 1 else c
        copies.append(pltpu.make_async_remote_copy(
            src_ref=src_ref.at[4 * px + 2 * py + pc] if scatter else src_ref, dst_ref=out_ref.at[me],
            send_sem=send_sems.at[k - 1], recv_sem=recv_sems.at[k - 1], device_id=(px, py, pc), device_id_type=MESH))

    def start():
        own.start()
        for cp in copies:
            cp.start()

    def finish():
        for cp in copies:
            cp.wait_recv()
        for cp in copies:
            cp.wait_send()
        own.wait()

    return start, finish


def all_gather(block):
    def body(x_ref, out_ref, *sems):
        for stage in _gather_steps(x_ref, out_ref, *sems):
            stage()

    return pl.pallas_call(
        body, name="all_gather",
        in_specs=[pl.BlockSpec(memory_space=pl.ANY)],
        out_specs=pl.BlockSpec(memory_space=pl.ANY),
        out_shape=jax.ShapeDtypeStruct((N_DEV,) + block.shape, block.dtype),
        scratch_shapes=_comm_sems(),
    )(block)


def _row_tile(r, cap):
    return max(d for d in range(16, cap + 1, 16) if r % d == 0)


def sum_adamw(parts, w, m, v, cap, carried=()):
    nl, r, c = w.shape
    tr = _row_tile(r, cap)
    steps = r // tr
    n = len(carried)
    c1 = 1.0 / (1.0 - ADAM_B1 ** ADAM_STEP)
    c2 = 1.0 / (1.0 - ADAM_B2 ** ADAM_STEP)

    def body(*refs):
        p_refs = refs[:nl]
        w_ref, m_ref, v_ref = refs[nl:nl + 3]
        g_ref, d_ref, nm_ref, nv_ref = refs[nl + 3 + n:nl + 7 + n]
        sems = refs[nl + 7 + 2 * n:]
        stages = [_exchange_steps(refs[nl + 3 + a], refs[nl + 7 + n + a], carried[a][1], *sems[3 * a:3 * a + 3])
                  for a in range(n)]
        layer, i = pl.program_id(0), pl.program_id(1)
        if n:
            pl.when((layer == 0) & (i == 0))(_each(stages, 0))

        def update(p_ref):
            g = p_ref[0].astype(F32)
            for k in range(1, N_DEV):
                g = g + p_ref[k].astype(F32)
            m_new = ADAM_B1 * m_ref[...] + (1.0 - ADAM_B1) * g
            v_new = ADAM_B2 * v_ref[...] + (1.0 - ADAM_B2) * (g * g)
            g_ref[...] = g
            nm_ref[...] = m_new
            nv_ref[...] = v_new
            d_ref[...] = -ADAM_LR * ((m_new * c1) / (jnp.sqrt(v_new * c2) + ADAM_EPS) + ADAM_WD * w_ref[...])

        for k in range(nl):
            pl.when(layer == k)(functools.partial(update, p_refs[k]))
        if n:
            pl.when((layer == nl - 1) & (i == steps - 1))(_each(stages, 1))

    def parts_spec(k):
        return pl.BlockSpec((N_DEV, tr, c), lambda l, i: (0, jnp.where(l == k, i, jnp.where(l < k, 0, steps - 1)), 0))

    blk = pl.BlockSpec((None, tr, c), lambda l, i: (l, i, 0))
    out = jax.ShapeDtypeStruct((nl, r, c), F32)
    hbm = pl.BlockSpec(memory_space=pl.ANY)
    return pl.pallas_call(
        body, name=f"sum_adamw_exchange{n}" if n else "sum_adamw", grid=(nl, steps),
        in_specs=[parts_spec(k) for k in range(nl)] + [blk, blk, blk] + [hbm] * n,
        out_specs=[blk, blk, blk, blk] + [hbm] * n,
        out_shape=[out, out, out, out]
        + [jax.ShapeDtypeStruct(src.shape if scatter else (N_DEV,) + src.shape, src.dtype) for src, scatter in carried],
        scratch_shapes=_comm_sems() * n,
        compiler_params=_cp(("arbitrary", "arbitrary")),
    )(*parts, w, m, v, *[src for src, _ in carried])


PACK_W = 1024
MIX_PIECES = (("w_out", D // N_DEV, D, False), ("w_uq", HEADS * (NOPE + ROPE) // N_DEV, Q_RANK, True),
              ("w_ukv", HEADS * (NOPE + V_DIM) // N_DEV, KV_RANK, True), ("conv", 16, PACK_W, False),
              ("w_in", IN_W // N_DEV, D, True))
FFN_PIECES = (("w_gate", D_FF // N_DEV, D, True), ("w_up", D_FF // N_DEV, D, True), ("w_down", D_FF // N_DEV, D, False))
def _packed_rows(rows, cols):
    return rows * cols // PACK_W


OFFSET = {}
for _pieces in (MIX_PIECES, FFN_PIECES):
    _off = 0
    for _name, _rows, _cols, _ in _pieces:
        assert _rows * _cols % PACK_W == 0
        OFFSET[_name] = _off
        _off += _packed_rows(_rows, _cols) + -_packed_rows(_rows, _cols) % 16
assert all(o % 16 == 0 for o in OFFSET.values())
assert [OFFSET[n] for n in ("w_gate", "w_up", "w_down")] == [0, FFN_SHARD, 2 * FFN_SHARD]
CONV_BITS = 3 * (CV_W // N_DEV) * 2


def _to_pack(shards, dtype, pieces, conv=None):
    nl = shards["w_in"].shape[0]
    parts = []
    for name, rows, cols, transposed in pieces:
        if name == "conv":
            if conv is None:
                a = jnp.zeros((nl, rows, PACK_W), dtype)
            else:
                bits = lax.bitcast_convert_type(conv.astype(F32), BF16).reshape(nl, CONV_BITS)
                a = jnp.pad(bits, ((0, 0), (0, rows * PACK_W - CONV_BITS))).reshape(nl, rows, PACK_W)
        else:
            a = shards[name].astype(dtype)
            a = (jnp.swapaxes(a, 1, 2) if transposed else a).reshape(nl, _packed_rows(rows, cols), PACK_W)
            a = jnp.pad(a, ((0, 0), (0, -a.shape[1] % 16), (0, 0)))
        parts.append(a)
    return jnp.concatenate(parts, axis=1)


def _from_pack(pack, pieces):
    out = {}
    for name, rows, cols, transposed in pieces:
        if name != "conv":
            a = pack[:, OFFSET[name]:OFFSET[name] + _packed_rows(rows, cols)].reshape(pack.shape[0], rows, cols)
            out[name] = jnp.swapaxes(a, 1, 2) if transposed else a
    return out


def _mix_weights(g):
    def rows(name):
        _, n, cols, _ = next(p for p in MIX_PIECES if p[0] == name)
        return g[:, OFFSET[name]:OFFSET[name] + _packed_rows(n, cols)].reshape(N_DEV, n, cols)

    w_in_t = rows("w_in").reshape(IN_W, D)
    w_in_p = jnp.concatenate([w_in_t[1184:], w_in_t[:672], jnp.zeros((96, D), BF16), w_in_t[672:1184]], axis=0)
    w_uq_p = jnp.pad(rows("w_uq"), ((0, 0), (0, HEAD_PAD - NOPE - ROPE), (0, 0))).reshape(HEADS * HEAD_PAD, Q_RANK)
    kv = rows("w_ukv")
    w_k = jnp.pad(kv[:, :NOPE], ((0, 0), (0, HEAD_PAD - NOPE), (0, 0))).reshape(HEADS * HEAD_PAD, KV_RANK)
    w_ukv_p = jnp.concatenate([w_k, kv[:, NOPE:].reshape(HEADS * V_DIM, KV_RANK)], axis=0)
    bits = rows("conv").reshape(N_DEV, -1)[:, :CONV_BITS].reshape(N_DEV, 3, CV_W // N_DEV, 2)
    conv_w = jnp.moveaxis(lax.bitcast_convert_type(bits, F32), 0, 1).reshape(3, CV_W)
    return dict(w_in=w_in_p, w_uq=w_uq_p, w_ukv=w_ukv_p, w_out=rows("w_out").reshape(D, D), conv_w=conv_w)


def _grad_chunks(full):
    d_in = full["w_in"]
    d_in = jnp.concatenate([d_in[768:768 + 672], d_in[1536:], d_in[:768]], axis=0)
    d_uq = full["w_uq"].reshape(HEADS, HEAD_PAD, Q_RANK)[:, :NOPE + ROPE]
    d_k = full["w_ukv"][:HEADS * HEAD_PAD].reshape(HEADS, HEAD_PAD, KV_RANK)[:, :NOPE]
    d_v = full["w_ukv"][HEADS * HEAD_PAD:].reshape(HEADS, V_DIM, KV_RANK)
    mats = dict(w_in=d_in, w_uq=d_uq, w_ukv=jnp.concatenate([d_k, d_v], axis=1), w_out=full["w_out"])
    parts = []
    for name, rows, cols, _ in MIX_PIECES:
        if name == "conv":
            parts.append(jnp.zeros((N_DEV, rows, PACK_W), BF16))
        else:
            a = mats[name].reshape(N_DEV, _packed_rows(rows, cols), PACK_W)
            parts.append(jnp.pad(a, ((0, 0), (0, -a.shape[1] % 16), (0, 0))))
    return jnp.concatenate(parts, axis=1)


SMALL = (("mix_pre_g", (D,)), ("mix_post_g", (D,)), ("ffn_pre_g", (D,)), ("ffn_post_g", (D,)), ("q_norm_g", (Q_RANK,)),
         ("kv_norm_g", (KV_RANK,)), ("sg_ln_g", (SG_W,)), ("sg_ln_b", (SG_W,)), ("w_sp", (4, CHUNK, CHUNK)),
         ("b_sp", (4, CHUNK)), ("out_norm_g", (D,)))
SMALL_ROWS = 576


def _pack_small(vals, nl):
    flat = jnp.concatenate([vals[name].reshape(nl, -1) for name, _ in SMALL] + [vals["conv_w"].reshape(nl, -1)], axis=1)
    return jnp.pad(flat, ((0, 0), (0, SMALL_ROWS * 128 - flat.shape[1]))).reshape(nl * SMALL_ROWS, 128)


def _unpack_small(pack, nl):
    flat = pack.reshape(nl, SMALL_ROWS * 128)
    out, off = {}, 0
    for name, shape in SMALL + (("conv_w", (3, CV_W)),):
        n = int(np.prod(shape))
        out[name] = flat[:, off:off + n].reshape((nl,) + shape)
        off += n
    return out


def _layer_fwd(x, lw, sp, tabs, consts, ffn_pack, next_mix_pack):
    ca, sb, sc = tabs
    z = pre_in_fwd(x, sp["mix_pre_g"], lw["w_in"])
    q, k, v = mla_proj_fwd(z, ca, sb, sc, sp["q_norm_g"], sp["kv_norm_g"], lw["w_uq"], lw["w_ukv"])
    ya, lse, lw["ffn"] = attn_fwd(q, k, v, (ffn_pack,))
    x1 = mix_fwd(x, z, ya, consts["gm"], sp["sg_ln_g"], sp["sg_ln_b"], sp["w_sp"], sp["bias"], lw["conv_w"],
                 sp["out_norm_g"], lw["w_out"], sp["mix_post_g"])
    x2, f, *mix_gathered = ffn_fwd(x1, sp["ffn_pre_g"], lw["ffn"], sp["ffn_post_g"], next_mix_pack)
    return x2, (x, z, q, k, v, ya, lse, x1, f), mix_gathered


def _layer_bwd(dx2, saved, lw, sp, tabs, consts, pending):
    ca, sb, sc = tabs
    x, z, q, k, v, ya, lse, x1, f = saved
    dx1, h2, dab, s, df, d_ffn_pre, d_ffn_post, *received = ffn_bwd(x1, f, dx2, sp["ffn_pre_g"], lw["ffn"], sp["ffn_post_g"],
                                                                    pending)
    ffn_chunks = atb_ffn_chunks(s, df, 2, atb_ffn_chunks(dab, h2, 0)).reshape(N_DEV, len(FFN_PIECES) * FFN_SHARD, D)
    dya, dyc, dzsg, d_w_out, d_mix_post, d_out_norm, d_lng, d_lnb, d_wsp, d_bias = mix_bwd(
        dx1, z, ya, consts["gm"], sp["sg_ln_g"], sp["sg_ln_b"], sp["w_sp"], sp["w_sp_t"], sp["bias"], lw["conv_w"],
        sp["out_norm_g"], lw["w_out"], sp["mix_post_g"])
    dzcv, d_cw = conv_bwd(dyc, z, lw["conv_w"])
    dq, dk, dv, got_ffn = attn_bwd(q, k, v, ya, lse, dya, ((ffn_chunks, True),))
    dzmla, cq, ckv, dqp, dkvp, d_gq, d_gkv = mla_proj_bwd(dq, dk, dv, z, ca, sb, sc, sp["q_norm_g"], sp["kv_norm_g"],
                                                          lw["w_uq"], lw["w_ukv"])
    d_w_uq = atb(dqp, cq, 1024)
    d_w_ukv = atb(dkvp, ckv, 1536)
    dx, d_w_in, d_mix_pre = pre_in_bwd(x, dx1, dzcv, dzmla, dzsg, sp["mix_pre_g"], lw["w_in"])
    mix_chunks = _grad_chunks(dict(w_in=d_w_in, w_uq=d_w_uq, w_ukv=d_w_ukv, w_out=d_w_out))
    d_bsp = d_bias[:, ::GROUP].T
    small = dict(mix_pre_g=d_mix_pre[0], mix_post_g=d_mix_post[0], ffn_pre_g=d_ffn_pre[0], ffn_post_g=d_ffn_post[0],
                 q_norm_g=d_gq[0], kv_norm_g=d_gkv[0], sg_ln_g=d_lng[0], sg_ln_b=d_lnb[0], w_sp=d_wsp, b_sp=d_bsp,
                 out_norm_g=d_out_norm[0], conv_w=d_cw[:3])
    small_pack = _pack_small({name: a[None] for name, a in small.items()}, 1)
    return dx, ((mix_chunks, True), (small_pack, False)), [got_ffn] + received


def kernel(x, positions, mix_pre_g, mix_post_g, ffn_pre_g, ffn_post_g, w_in, q_norm_g, w_uq, kv_norm_g, w_ukv, sg_ln_g, sg_ln_b, w_sp, b_sp, conv_w, out_norm_g, w_out, w_gate, w_up, w_down, loss_target, m_mix_pre_g, m_mix_post_g, m_ffn_pre_g, m_ffn_post_g, m_w_in, m_q_norm_g, m_w_uq, m_kv_norm_g, m_w_ukv, m_sg_ln_g, m_sg_ln_b, m_w_sp, m_b_sp, m_conv_w, m_out_norm_g, m_w_out, m_w_gate, m_w_up, m_w_down, v_mix_pre_g, v_mix_post_g, v_ffn_pre_g, v_ffn_post_g, v_w_in, v_q_norm_g, v_w_uq, v_kv_norm_g, v_w_ukv, v_sg_ln_g, v_sg_ln_b, v_w_sp, v_b_sp, v_conv_w, v_out_norm_g, v_w_out, v_w_gate, v_w_up, v_w_down):
    nl = w_in.shape[0]
    t = x.shape[1]
    w = dict(mix_pre_g=mix_pre_g, mix_post_g=mix_post_g, ffn_pre_g=ffn_pre_g, ffn_post_g=ffn_post_g, w_in=w_in,
             q_norm_g=q_norm_g, w_uq=w_uq, kv_norm_g=kv_norm_g, w_ukv=w_ukv, sg_ln_g=sg_ln_g, sg_ln_b=sg_ln_b, w_sp=w_sp,
             b_sp=b_sp, conv_w=conv_w, out_norm_g=out_norm_g, w_out=w_out, w_gate=w_gate, w_up=w_up, w_down=w_down)
    m = dict(mix_pre_g=m_mix_pre_g, mix_post_g=m_mix_post_g, ffn_pre_g=m_ffn_pre_g, ffn_post_g=m_ffn_post_g, w_in=m_w_in,
             q_norm_g=m_q_norm_g, w_uq=m_w_uq, kv_norm_g=m_kv_norm_g, w_ukv=m_w_ukv, sg_ln_g=m_sg_ln_g, sg_ln_b=m_sg_ln_b,
             w_sp=m_w_sp, b_sp=m_b_sp, conv_w=m_conv_w, out_norm_g=m_out_norm_g, w_out=m_w_out, w_gate=m_w_gate,
             w_up=m_w_up, w_down=m_w_down)
    v = dict(mix_pre_g=v_mix_pre_g, mix_post_g=v_mix_post_g, ffn_pre_g=v_ffn_pre_g, ffn_post_g=v_ffn_post_g, w_in=v_w_in,
             q_norm_g=v_q_norm_g, w_uq=v_w_uq, kv_norm_g=v_kv_norm_g, w_ukv=v_w_ukv, sg_ln_g=v_sg_ln_g, sg_ln_b=v_sg_ln_b,
             w_sp=v_w_sp, b_sp=v_b_sp, conv_w=v_conv_w, out_norm_g=v_out_norm_g, w_out=v_w_out, w_gate=v_w_gate,
             w_up=v_w_up, w_down=v_w_down)

    mix_pack = _to_pack(w, BF16, MIX_PIECES, conv=w["conv_w"])
    ffn_pack = _to_pack(w, BF16, FFN_PIECES)
    consts = dict(gm=jnp.asarray(np.kron(np.eye(SG_W // GROUP), np.full((GROUP, GROUP), 1.0 / GROUP)), BF16))
    smalls = []
    for l in range(nl):
        sp = {name: w[name][l].reshape(1, -1) for name, shape in SMALL if len(shape) == 1}
        sp["w_sp"] = w["w_sp"][l]
        sp["w_sp_t"] = jnp.swapaxes(w["w_sp"][l], 1, 2)
        sp["bias"] = jnp.repeat(w["b_sp"][l].T, GROUP, axis=1)
        smalls.append(sp)
    inv_freq = 1.0 / (ROPE_THETA ** (jnp.arange(0, ROPE // 2, dtype=F32) / (ROPE // 2)))
    inv = jnp.zeros((1, HEAD_PAD), F32).at[0, NOPE:NOPE + ROPE].set(jnp.concatenate([inv_freq, inv_freq]))
    tabs = rope_tables(positions.reshape(t, 1).astype(F32), inv)

    h = x[0]
    saved, layers = [], []
    mix_gathered = [all_gather(mix_pack[0])]
    for l in range(nl):
        layers.append(_mix_weights(mix_gathered[0]))
        h, s, mix_gathered = _layer_fwd(h, layers[l], smalls[l], tabs, consts, ffn_pack[l],
                                        (mix_pack[l + 1],) if l + 1 < nl else ())
        saved.append(s)
    sq, dh = loss_head(h, loss_target[0])
    loss = lax.psum(0.5 * sq[0, 0] / D, ("x", "y", "c"))

    got_ffn, got_mix, got_small = [None] * nl, [None] * nl, [None] * nl
    pending = ()
    for l in reversed(range(nl)):
        dh, new_pending, received = _layer_bwd(dh, saved[l], layers[l], smalls[l], tabs, consts, pending)
        got_ffn[l] = received[0]
        if pending:
            got_mix[l + 1], got_small[l + 1] = received[1:]
        pending = new_pending

    me = 4 * lax.axis_index("x") + 2 * lax.axis_index("y") + lax.axis_index("c")
    *ffn_new, got_mix[0], got_small[0] = sum_adamw(got_ffn, *[_to_pack(d, F32, FFN_PIECES) for d in (w, m, v)], 176,
                                                   carried=pending)
    mix_new = sum_adamw(got_mix, *[_to_pack(d, F32, MIX_PIECES) for d in (w, m, v)], 208)
    got_small = jnp.concatenate(got_small, axis=1)
    g_big, d_big, m_big, v_big = [{**_from_pack(a, FFN_PIECES), **_from_pack(b, MIX_PIECES)}
                                  for a, b in zip(ffn_new, mix_new)]

    def full_conv(a):
        return lax.dynamic_update_slice(jnp.zeros((nl, 3, CV_W), F32), a, (0, 0, me * (CV_W // N_DEV)))

    def small_pack(d):
        return _pack_small({**{name: d[name] for name, _ in SMALL}, "conv_w": full_conv(d["conv_w"])}, nl)

    g_small, d_small, m_small, v_small = [_unpack_small(p[0], nl) for p in
                                          sum_adamw([got_small], small_pack(w)[None], small_pack(m)[None],
                                                    small_pack(v)[None], 1152)]
    outs = []
    for big, small in ((g_big, g_small), (d_big, d_small), (m_big, m_small), (v_big, v_small)):
        for name in w:
            if name == "conv_w":
                outs.append(lax.dynamic_slice(small[name], (0, 0, me * (CV_W // N_DEV)), (nl, 3, CV_W // N_DEV)))
            elif name in small:
                outs.append(small[name])
            else:
                outs.append(big[name])
    return (loss, dh[None], *outs)
```

```python
import functools

import jax
import jax.numpy as jnp
import numpy as np
from jax import lax
from jax.experimental import pallas as pl
from jax.experimental.pallas import tpu as pltpu

F32 = jnp.float32
BF16 = jnp.bfloat16

D = 1024
Q_RANK = 384
KV_RANK = 256
ROPE = 32
HEADS = 8
NOPE = 64
V_DIM = 64
HEAD_PAD = 128
SG_W = 256
CV_W = 256
CHUNK = 128
GROUP = 64
D_FF = 2816
IN_W = 1952
Z_W = 2048
Z_CV, Z_MLA, Z_SG = 0, 768, 1536
EPS = 1e-6
ROPE_THETA = 10000.0
SCALE = (NOPE + ROPE) ** -0.5
LOG2E = 1.4426950408889634
SCALE_LOG2E = SCALE * LOG2E
NEG = -1e30
N_DEV = 8

ADAM_LR, ADAM_B1, ADAM_B2, ADAM_EPS, ADAM_WD, ADAM_STEP = 0.001, 0.9, 0.999, 1e-08, 0.01, 10

VMEM_LIMIT = 56 * 1024 * 1024

TM = 512
TM_FFN = 256
FFN_SLAB = 256
HALO = 16
TQ = 512
FWD_UNROLL = 2
FWD_HEADS = 2
TT = 2048


def _cp(sem, vmem=VMEM_LIMIT):
    return pltpu.CompilerParams(dimension_semantics=sem, vmem_limit_bytes=vmem)


def _whole():
    return pl.BlockSpec(memory_space=pltpu.VMEM)


def _mm(a, b):
    return jnp.dot(a, b, preferred_element_type=F32)


def _mm_nt(a, b):
    return lax.dot_general(a, b, (((1,), (1,)), ((), ())), preferred_element_type=F32)


def _mm_tn(a, b):
    return lax.dot_general(a, b, (((0,), (0,)), ((), ())), preferred_element_type=F32)


def _rms_fwd(x, g):
    r = lax.rsqrt(jnp.mean(x * x, axis=-1, keepdims=True) + EPS)
    xh = x * r
    return xh * g, xh, r


def _rms_bwd(xh, r, g, dy):
    dxh = dy * g
    dx = r * (dxh - xh * jnp.mean(dxh * xh, axis=-1, keepdims=True))
    dg = jnp.sum(dy * xh, axis=0, keepdims=True)
    return dx, dg


def _gmean(v, gm):
    hi = v.astype(BF16)
    lo = (v - hi.astype(F32)).astype(BF16)
    return _mm(hi, gm) + _mm(lo, gm)


def _gelu(x):
    c = np.float32(np.sqrt(2.0 / np.pi))
    u = c * (x + 0.044715 * x * x * x)
    t = jnp.tanh(u)
    return 0.5 * x * (1.0 + t), t


def _gelu_grad(x, t):
    c = np.float32(np.sqrt(2.0 / np.pi))
    return 0.5 * (1.0 + t) + 0.5 * x * (1.0 - t * t) * c * (1.0 + 3.0 * 0.044715 * x * x)


def _rope(t, ca, sb, sc):
    return t * ca + pltpu.roll(t, HEAD_PAD - 16, 1) * sb + pltpu.roll(t, 16, 1) * sc


def _rope_t(dt, ca, sb, sc):
    return dt * ca + pltpu.roll(dt * sb, 16, 1) + pltpu.roll(dt * sc, HEAD_PAD - 16, 1)


def _shift_down(y, k, head):
    out = pltpu.roll(y, k, 0)
    row = lax.broadcasted_iota(jnp.int32, y.shape, 0)
    for j in range(k):
        out = jnp.where(row == j, head[head.shape[0] - k + j:head.shape[0] - k + j + 1, :], out)
    return out


def _shift_up(y, k, tail):
    n = y.shape[0]
    out = pltpu.roll(y, n - k, 0)
    row = lax.broadcasted_iota(jnp.int32, y.shape, 0)
    for j in range(k):
        out = jnp.where(row == n - k + j, tail[j:j + 1, :], out)
    return out


def rope_tables(pos, inv):
    t = pos.shape[0]
    tm = min(TM, t)

    def body(pos_ref, inv_ref, ca_ref, sb_ref, sc_ref):
        ang = pos_ref[...] * inv_ref[...]
        c = jnp.cos(ang)
        s = jnp.sin(ang)
        lane = lax.broadcasted_iota(jnp.int32, ang.shape, 1)
        ca_ref[...] = jnp.where(lane < NOPE, 1.0, jnp.where(lane < NOPE + ROPE, c, 0.0))
        sb_ref[...] = jnp.where((lane >= NOPE) & (lane < NOPE + 16), -s, 0.0)
        sc_ref[...] = jnp.where((lane >= NOPE + 16) & (lane < NOPE + ROPE), s, 0.0)

    out = jax.ShapeDtypeStruct((t, HEAD_PAD), F32)
    blk = pl.BlockSpec((tm, HEAD_PAD), lambda i: (i, 0))
    return pl.pallas_call(
        body, name="rope_tables", grid=(t // tm,),
        in_specs=[pl.BlockSpec((tm, 1), lambda i: (i, 0)), pl.BlockSpec((1, HEAD_PAD), lambda i: (0, 0))],
        out_specs=[blk, blk, blk], out_shape=[out, out, out],
        compiler_params=_cp(("parallel",)),
    )(pos, inv)


def pre_in_fwd(x, g, w):
    t = x.shape[0]
    tm = min(TM, t)

    def body(x_ref, g_ref, w_ref, z_ref):
        h, _, _ = _rms_fwd(x_ref[...], g_ref[...])
        z_ref[...] = _mm_nt(h.astype(BF16), w_ref[...]).astype(BF16)

    return pl.pallas_call(
        body, name="pre_in_fwd", grid=(t // tm,),
        in_specs=[pl.BlockSpec((tm, D), lambda i: (i, 0)), _whole(), _whole()],
        out_specs=pl.BlockSpec((tm, Z_W), lambda i: (i, 0)),
        out_shape=jax.ShapeDtypeStruct((t, Z_W), BF16),
        compiler_params=_cp(("parallel",)),
    )(x, g, w)


def mla_proj_fwd(z, ca, sb, sc, gq, gkv, wuq, wukv):
    t = z.shape[0]
    tm = min(TM, t)

    def body(z_ref, ca_ref, sb_ref, sc_ref, gq_ref, gkv_ref, wuq_ref, wukv_ref, q_ref, k_ref, v_ref):
        z = z_ref[...].astype(F32)
        ca, sb, sc = ca_ref[...], sb_ref[...], sc_ref[...]
        cq, _, _ = _rms_fwd(z[:, :Q_RANK], gq_ref[...])
        ckv, _, _ = _rms_fwd(z[:, Q_RANK:Q_RANK + KV_RANK], gkv_ref[...])
        q = _mm_nt(cq.astype(BF16), wuq_ref[...])
        kv = _mm_nt(ckv.astype(BF16), wukv_ref[...])
        kr = _rope(pltpu.roll(z[:, Q_RANK + KV_RANK:], NOPE, 1), ca, sb, sc)
        for h in range(HEADS):
            lanes = slice(h * HEAD_PAD, (h + 1) * HEAD_PAD)
            q_ref[:, lanes] = _rope(q[:, lanes], ca, sb, sc).astype(BF16)
            k_ref[:, lanes] = (kv[:, lanes] + kr).astype(BF16)
        v_ref[...] = kv[:, HEADS * HEAD_PAD:].astype(BF16)

    tab = pl.BlockSpec((tm, HEAD_PAD), lambda i: (i, 0))
    return pl.pallas_call(
        body, name="mla_proj_fwd", grid=(t // tm,),
        in_specs=[pl.BlockSpec((tm, 768), lambda i: (i, 1)), tab, tab, tab, _whole(), _whole(), _whole(), _whole()],
        out_specs=[pl.BlockSpec((tm, HEADS * HEAD_PAD), lambda i: (i, 0)),
                   pl.BlockSpec((tm, HEADS * HEAD_PAD), lambda i: (i, 0)),
                   pl.BlockSpec((tm, HEADS * V_DIM), lambda i: (i, 0))],
        out_shape=[jax.ShapeDtypeStruct((t, HEADS * HEAD_PAD), BF16),
                   jax.ShapeDtypeStruct((t, HEADS * HEAD_PAD), BF16),
                   jax.ShapeDtypeStruct((t, HEADS * V_DIM), BF16)],
        compiler_params=_cp(("parallel",)),
    )(z, ca, sb, sc, gq, gkv, wuq, wukv)


def _each(stages, k):
    def run():
        for stage in stages:
            stage[k]()
    return run


def attn_fwd(q, k, v, carried=()):
    t = q.shape[0]
    tq = min(TQ, t)
    nq = t // tq
    hs = FWD_HEADS
    last_pair = HEADS // hs - 1
    n = len(carried)

    def body(*refs):
        q_ref, k_ref, v_ref = refs[:3]
        o_ref, lse_ref = refs[3 + n:5 + n]
        sems = refs[5 + 2 * n:]
        stages = [_gather_steps(refs[3 + a], refs[5 + n + a], *sems[3 * a:3 * a + 3]) for a in range(n)]
        if n:
            pair = pl.program_id(0)
            pl.when((pair == 0) & (pl.program_id(1) == 0))(_each(stages, 0))
            pl.when((pair == last_pair) & (pl.program_id(1) == 0))(_each(stages, 1))
        i = pl.program_id(1)
        row = lax.broadcasted_iota(jnp.int32, (tq, tq), 0)
        col = lax.broadcasted_iota(jnp.int32, (tq, tq), 1)
        head_lanes = [slice(h * HEAD_PAD, (h + 1) * HEAD_PAD) for h in range(hs)]
        pair_lanes = [slice(p * 2 * V_DIM, (p + 1) * 2 * V_DIM) for p in range(hs // 2)]

        def step(j, carry, masked):
            start = pl.multiple_of(j * tq, tq)
            out = []
            for h in range(hs):
                m, l, acc = carry[h]
                s = _mm_nt(q_ref[:, head_lanes[h]], k_ref[pl.ds(start, tq), head_lanes[h]])
                if masked:
                    s = jnp.where(col <= row, s, NEG)
                m_new = jnp.maximum(m, jnp.max(s, axis=-1, keepdims=True))
                p = jnp.exp2((s - m_new) * SCALE_LOG2E)
                alpha = jnp.exp2((m - m_new) * SCALE_LOG2E)
                l = alpha * l + jnp.sum(p, axis=-1, keepdims=True)
                acc = alpha * acc + _mm(p.astype(BF16), v_ref[pl.ds(start, tq), pair_lanes[h // 2]])
                out.append((m_new, l, acc))
            return tuple(out)

        init = (jnp.full((tq, 1), NEG, F32), jnp.zeros((tq, 1), F32), jnp.zeros((tq, 2 * V_DIM), F32))
        def trip(j, c):
            for u in range(FWD_UNROLL):
                c = step(FWD_UNROLL * j + u, c, False)
            return c

        carry = lax.fori_loop(0, i // FWD_UNROLL, trip, (init,) * hs)
        carry = lax.fori_loop(i - i % FWD_UNROLL, i, lambda j, c: step(j, c, False), carry)
        outs = []
        for h, (m, l, acc) in enumerate(step(i, carry, True)):
            outs.append(acc / l)
            lse_ref[:, head_lanes[h]] = jnp.broadcast_to(m * SCALE + jnp.log(l), (tq, HEAD_PAD))
        lane = lax.broadcasted_iota(jnp.int32, (tq, 2 * V_DIM), 1)
        for p in range(hs // 2):
            o_ref[:, pair_lanes[p]] = jnp.where(lane < V_DIM, outs[2 * p], outs[2 * p + 1])
        if n:
            pl.when((pl.program_id(0) == last_pair) & (i == nq - 1))(_each(stages, 2))

    hbm = pl.BlockSpec(memory_space=pl.ANY)
    return pl.pallas_call(
        body, name=f"attn_fwd_gather{n}" if n else "attn_fwd", grid=(HEADS // hs, nq),
        in_specs=[pl.BlockSpec((tq, hs * HEAD_PAD), lambda p, i: (i, p)),
                  pl.BlockSpec((t, hs * HEAD_PAD), lambda p, i: (0, p)),
                  pl.BlockSpec((t, hs * V_DIM), lambda p, i: (0, p))] + [hbm] * n,
        out_specs=[pl.BlockSpec((tq, hs * V_DIM), lambda p, i: (i, p)),
                   pl.BlockSpec((tq, hs * HEAD_PAD), lambda p, i: (i, p))] + [hbm] * n,
        out_shape=[jax.ShapeDtypeStruct((t, HEADS * V_DIM), F32), jax.ShapeDtypeStruct((t, HEADS * HEAD_PAD), F32)]
        + [jax.ShapeDtypeStruct((N_DEV,) + c.shape, c.dtype) for c in carried],
        scratch_shapes=_comm_sems() * n,
        compiler_params=_cp(("arbitrary", "arbitrary") if n else ("parallel", "parallel")),
    )(q, k, v, *carried)


def _sgu_fwd(zsg, gm, lng, lnb, wc_ref, bias, mixed_ref):
    uv, th = _gelu(zsg)
    u, v0 = uv[:, :SG_W], uv[:, SG_W:]
    vc = v0 - _gmean(v0, gm)
    r = lax.rsqrt(_gmean(vc * vc, gm) + EPS)
    vh = vc * r
    v = vh * lng + lnb
    lane = lax.broadcasted_iota(jnp.int32, (CHUNK, SG_W), 1)
    for c in range(zsg.shape[0] // CHUNK):
        rows = slice(c * CHUNK, (c + 1) * CHUNK)
        vb = v[rows].astype(BF16)
        mixed = bias
        for g in range(SG_W // GROUP):
            mixed = mixed + jnp.where(lane // GROUP == g, _mm(wc_ref[g], vb), 0.0)
        mixed_ref[rows, :] = mixed
    return u, v, vh, r, th


def _conv_fwd(zcv, halo, first, cw):
    gb, gc, hh = zcv[:, :CV_W], zcv[:, CV_W:2 * CV_W], zcv[:, 2 * CV_W:]
    y = gc * hh
    yh = jnp.where(first, 0.0, halo[:, CV_W:2 * CV_W] * halo[:, 2 * CV_W:])
    y1 = _shift_down(y, 1, yh)
    y2 = _shift_down(y, 2, yh)
    conv = y2 * cw[0:1, :] + y1 * cw[1:2, :] + y * cw[2:3, :]
    return gb * conv, conv, y, y1, y2


def _tril_bf16(w_ref, g):
    row = lax.broadcasted_iota(jnp.int32, (CHUNK, CHUNK), 0)
    col = lax.broadcasted_iota(jnp.int32, (CHUNK, CHUNK), 1)
    return jnp.where(col <= row, w_ref[g], 0.0).astype(BF16)


def mix_fwd(x, z, ya, gm, lng, lnb, wsp, bias, cw, gout, wout, gpost):
    t = x.shape[0]
    tm = min(TM, t)

    def body(x_ref, zcv_ref, halo_ref, zsg_ref, ya_ref, gm_ref, lng_ref, lnb_ref, wsp_ref, bias_ref, cw_ref,
             gout_ref, wout_ref, gpost_ref, x1_ref, wc_ref, mixed_ref):
        i = pl.program_id(0)
        for g in range(SG_W // GROUP):
            wc_ref[g] = _tril_bf16(wsp_ref, g)
        u, _, _, _, _ = _sgu_fwd(zsg_ref[...].astype(F32), gm_ref[...], lng_ref[...], lnb_ref[...], wc_ref, bias_ref[...],
                                 mixed_ref)
        yb = u * mixed_ref[...]
        yc, _, _, _, _ = _conv_fwd(zcv_ref[...].astype(F32), halo_ref[...].astype(F32), i == 0, cw_ref[...])
        gout = gout_ref[...]
        na, _, _ = _rms_fwd(ya_ref[...], gout[:, :512])
        nb, _, _ = _rms_fwd(yb, gout[:, 512:768])
        nc, _, _ = _rms_fwd(yc, gout[:, 768:])
        mix = jnp.concatenate([na, nb, nc], axis=1).astype(BF16)
        o, _, _ = _rms_fwd(_mm(mix, wout_ref[...]), gpost_ref[...])
        x1_ref[...] = x_ref[...] + o

    return pl.pallas_call(
        body, name="mix_fwd", grid=(t // tm,),
        in_specs=[pl.BlockSpec((tm, D), lambda i: (i, 0)),
                  pl.BlockSpec((tm, 768), lambda i: (i, 0)),
                  pl.BlockSpec((HALO, 768), lambda i: (jnp.maximum(i * (tm // HALO) - 1, 0), 0)),
                  pl.BlockSpec((tm, 512), lambda i: (i, 3)),
                  pl.BlockSpec((tm, 512), lambda i: (i, 0)),
                  _whole(), _whole(), _whole(), _whole(), _whole(), _whole(), _whole(), _whole(), _whole()],
        out_specs=pl.BlockSpec((tm, D), lambda i: (i, 0)),
        out_shape=jax.ShapeDtypeStruct((t, D), F32),
        scratch_shapes=[pltpu.VMEM((SG_W // GROUP, CHUNK, CHUNK), BF16), pltpu.VMEM((tm, SG_W), F32)],
        compiler_params=_cp(("arbitrary",)),
    )(x, z, z, z, ya, gm, lng, lnb, wsp, bias, cw, gout, wout, gpost)


def _sigmoid(a):
    return 1.0 / (1.0 + jnp.exp(-a))


FFN_SHARD = D_FF // N_DEV


def _load_ffn_weights(g_ref, wgu_ref, wd_ref, sems):
    copies = []
    for j in range(N_DEV):
        for p, (dst, base) in enumerate(((wgu_ref, 0), (wgu_ref, D_FF), (wd_ref, 0))):
            copies.append(pltpu.make_async_copy(g_ref.at[j, pl.ds(p * FFN_SHARD, FFN_SHARD)],
                                                dst.at[pl.ds(base + j * FFN_SHARD, FFN_SHARD)], sems.at[3 * j + p]))
    for cp in copies:
        cp.start()
    for cp in copies:
        cp.wait()


def _ffn_weight_scratch():
    return [pltpu.VMEM((2 * D_FF, D), BF16), pltpu.VMEM((D_FF, D), BF16), pltpu.SemaphoreType.DMA((3 * N_DEV,))]


def ffn_fwd(x1, gpre, gathered, gpost, carried=()):
    t = x1.shape[0]
    tm = min(TM_FFN, t)
    steps = t // tm
    n = len(carried)

    def body(*refs):
        x_ref, gpre_ref, g_ref, gpost_ref = refs[:4]
        x2_ref, f_ref = refs[4 + n:6 + n]
        wgu_ref, wd_ref, sems = refs[6 + 2 * n:9 + 2 * n]
        comm_sems = refs[9 + 2 * n:]
        stages = [_gather_steps(refs[4 + a], refs[6 + n + a], *comm_sems[3 * a:3 * a + 3]) for a in range(n)]
        if n:
            pl.when(pl.program_id(0) == 0)(_each(stages, 0))
            pl.when(pl.program_id(0) == (3 * steps) // 4)(_each(stages, 1))

        @pl.when(pl.program_id(0) == 0)
        def _():
            _load_ffn_weights(g_ref, wgu_ref, wd_ref, sems)

        x = x_ref[...]
        h, _, _ = _rms_fwd(x, gpre_ref[...])
        ab = _mm_nt(h.astype(BF16), wgu_ref[...])
        a, b = ab[:, :D_FF], ab[:, D_FF:]
        s = a * _sigmoid(a) * b
        f = _mm(s.astype(BF16), wd_ref[...])
        f_ref[...] = f
        x2_ref[...] = x + _rms_fwd(f, gpost_ref[...])[0]
        if n:
            pl.when(pl.program_id(0) == steps - 1)(_each(stages, 2))

    row = pl.BlockSpec((tm, D), lambda i: (i, 0))
    hbm = pl.BlockSpec(memory_space=pl.ANY)
    return pl.pallas_call(
        body, name=f"ffn_fwd_gather{n}" if n else "ffn_fwd", grid=(steps,),
        in_specs=[row, _whole(), hbm, _whole()] + [hbm] * n,
        out_specs=[row, row] + [hbm] * n,
        out_shape=[jax.ShapeDtypeStruct((t, D), F32), jax.ShapeDtypeStruct((t, D), F32)]
        + [jax.ShapeDtypeStruct((N_DEV,) + c.shape, c.dtype) for c in carried],
        scratch_shapes=_ffn_weight_scratch() + _comm_sems() * n,
        compiler_params=_cp(("arbitrary",)),
    )(x1, gpre, gathered, gpost, *carried)


def loss_head(y, target):
    t = y.shape[0]
    tm = min(TM, t)

    def body(y_ref, t_ref, loss_ref, dy_ref):
        @pl.when(pl.program_id(0) == 0)
        def _():
            loss_ref[...] = jnp.zeros_like(loss_ref)

        e = y_ref[...] - t_ref[...]
        dy_ref[...] = e * (1.0 / D)
        loss_ref[...] += jnp.sum(jnp.sum(e * e, axis=-1, keepdims=True), axis=0, keepdims=True)

    return pl.pallas_call(
        body, name="loss_head", grid=(t // tm,),
        in_specs=[pl.BlockSpec((tm, D), lambda i: (i, 0)), pl.BlockSpec((tm, D), lambda i: (i, 0))],
        out_specs=[pl.BlockSpec((1, 128), lambda i: (0, 0)), pl.BlockSpec((tm, D), lambda i: (i, 0))],
        out_shape=[jax.ShapeDtypeStruct((1, 128), F32), jax.ShapeDtypeStruct((t, D), F32)],
        compiler_params=_cp(("arbitrary",)),
    )(y, target)


def _acc(ref, first, val):
    @pl.when(first)
    def _():
        ref[...] = val

    @pl.when(jnp.logical_not(first))
    def _():
        ref[...] += val


def ffn_bwd(x1, f, dx2, gpre, gathered, gpost, carried=()):
    t = x1.shape[0]
    tm = min(TM_FFN, t)
    steps = t // tm
    n = len(carried)

    def body(*refs):
        x_ref, f_ref, dx2_ref, gpre_ref, g_ref, gpost_ref = refs[:6]
        dx1_ref, h_ref, dab_ref, s_ref, df_ref, dgpre_ref, dgpost_ref = refs[6 + n:13 + n]
        ab_ref, ds_ref, wgu_ref, wd_ref, sems = refs[13 + 2 * n:18 + 2 * n]
        comm_sems = refs[18 + 2 * n:]
        stages = [_exchange_steps(refs[6 + a], refs[13 + n + a], carried[a][1], *comm_sems[3 * a:3 * a + 3])
                  for a in range(n)]
        first = pl.program_id(0) == 0
        if n:
            pl.when(first)(_each(stages, 0))

        @pl.when(first)
        def _():
            _load_ffn_weights(g_ref, wgu_ref, wd_ref, sems)

        dx2 = dx2_ref[...]
        gpre, gpost = gpre_ref[...], gpost_ref[...]
        h, xh, rx = _rms_fwd(x_ref[...], gpre)
        h_ref[...] = h.astype(BF16)
        ab_ref[...] = _mm_nt(h_ref[...], wgu_ref[...])
        for c in range(0, D_FF, FFN_SLAB):
            a, b = ab_ref[:, c:c + FFN_SLAB], ab_ref[:, D_FF + c:D_FF + c + FFN_SLAB]
            s_ref[:, c:c + FFN_SLAB] = (a * _sigmoid(a) * b).astype(BF16)
        _, fh, rf = _rms_fwd(f_ref[...], gpost)
        df, dgpost = _rms_bwd(fh, rf, gpost, dx2)
        df_ref[...] = df.astype(BF16)
        ds_ref[...] = _mm_nt(df_ref[...], wd_ref[...])
        for c in range(0, D_FF, FFN_SLAB):
            a, b = ab_ref[:, c:c + FFN_SLAB], ab_ref[:, D_FF + c:D_FF + c + FFN_SLAB]
            ds = ds_ref[:, c:c + FFN_SLAB]
            sg = _sigmoid(a)
            dab_ref[:, c:c + FFN_SLAB] = (ds * b * (sg * (1.0 + a * (1.0 - sg)))).astype(BF16)
            dab_ref[:, D_FF + c:D_FF + c + FFN_SLAB] = (ds * (a * sg)).astype(BF16)
        dx, dgpre = _rms_bwd(xh, rx, gpre, _mm(dab_ref[...], wgu_ref[...]))
        dx1_ref[...] = dx2 + dx
        _acc(dgpre_ref, first, dgpre)
        _acc(dgpost_ref, first, dgpost)
        if n:
            pl.when(pl.program_id(0) == steps - 1)(_each(stages, 1))

    row = lambda w: pl.BlockSpec((tm, w), lambda i: (i, 0))
    vec = pl.BlockSpec((1, D), lambda i: (0, 0))
    hbm = pl.BlockSpec(memory_space=pl.ANY)
    return pl.pallas_call(
        body, name=f"ffn_bwd_exchange{n}" if n else "ffn_bwd", grid=(steps,),
        in_specs=[row(D), row(D), row(D), _whole(), hbm, _whole()] + [hbm] * n,
        out_specs=[row(D), row(D), row(2 * D_FF), row(D_FF), row(D), vec, vec] + [hbm] * n,
        out_shape=[jax.ShapeDtypeStruct((t, D), F32), jax.ShapeDtypeStruct((t, D), BF16),
                   jax.ShapeDtypeStruct((t, 2 * D_FF), BF16), jax.ShapeDtypeStruct((t, D_FF), BF16),
                   jax.ShapeDtypeStruct((t, D), BF16), jax.ShapeDtypeStruct((1, D), F32),
                   jax.ShapeDtypeStruct((1, D), F32)]
        + [jax.ShapeDtypeStruct(src.shape if scatter else (N_DEV,) + src.shape, src.dtype) for src, scatter in carried],
        scratch_shapes=[pltpu.VMEM((tm, 2 * D_FF), F32), pltpu.VMEM((tm, D_FF), F32)] + _ffn_weight_scratch()
        + _comm_sems() * n,
        compiler_params=_cp(("arbitrary",)),
    )(x1, f, dx2, gpre, gathered, gpost, *[src for src, _ in carried])


FFN_TILE_SHARDS = 4


def atb_ffn_chunks(a, b, first_piece, chunks=None):
    t, k = a.shape
    tt = min(TT, t)
    tk = FFN_TILE_SHARDS * FFN_SHARD
    steps = t // tt
    per_piece = N_DEV // FFN_TILE_SHARDS

    def body(*refs):
        a_ref, b_ref, o_ref, acc_ref = refs[0], refs[1], refs[-2], refs[-1]
        i = pl.program_id(1)
        _acc(acc_ref, i == 0, _mm_tn(a_ref[...], b_ref[...]))

        @pl.when(i == steps - 1)
        def _():
            for d in range(FFN_TILE_SHARDS):
                o_ref[d, 0] = acc_ref[d * FFN_SHARD:(d + 1) * FFN_SHARD, :].astype(BF16)

    hbm = pl.BlockSpec(memory_space=pl.ANY)
    return pl.pallas_call(
        body, name="atb_ffn_chunks", grid=(k // tk, steps),
        in_specs=[pl.BlockSpec((tt, tk), lambda j, i: (i, j)), pl.BlockSpec((tt, D), lambda j, i: (i, 0))]
        + ([] if chunks is None else [hbm]),
        out_specs=pl.BlockSpec((FFN_TILE_SHARDS, 1, FFN_SHARD, D),
                               lambda j, i: (j % per_piece, first_piece + j // per_piece, 0, 0)),
        out_shape=jax.ShapeDtypeStruct((N_DEV, len(FFN_PIECES), FFN_SHARD, D), BF16),
        input_output_aliases={} if chunks is None else {2: 0},
        scratch_shapes=[pltpu.VMEM((tk, D), F32)],
        compiler_params=_cp(("parallel", "arbitrary")),
    )(a, b, *([] if chunks is None else [chunks]))


def mix_bwd(dx1, z, ya, gm, lng, lnb, wsp, wspt, bias, cw, gout, wout, gpost):
    t = dx1.shape[0]
    tm = min(TM, t)
    steps = t // tm
    ng = SG_W // GROUP

    def body(dx1_ref, zcv_ref, halo_ref, zsg_ref, ya_ref, gm_ref, lng_ref, lnb_ref, wsp_ref, wspt_ref, bias_ref,
             cw_ref, gout_ref, wout_ref, gpost_ref,
             dya_ref, dyc_ref, dzsg_ref, dwout_ref, dgpost_ref, dgout_ref, dlng_ref, dlnb_ref, dwsp_ref,
             dbias_ref, wc_ref, wct_ref, mixed_ref, dv_ref, dwout_acc):
        i = pl.program_id(0)
        first = i == 0
        gm = gm_ref[...]
        for g in range(ng):
            wc_ref[g] = _tril_bf16(wsp_ref, g)
            wct_ref[g] = jnp.where(
                lax.broadcasted_iota(jnp.int32, (CHUNK, CHUNK), 0) <= lax.broadcasted_iota(jnp.int32, (CHUNK, CHUNK), 1),
                wspt_ref[g], 0.0).astype(BF16)
        zsg = zsg_ref[...].astype(F32)
        lng = lng_ref[...]
        u, v, vh, r, th = _sgu_fwd(zsg, gm, lng, lnb_ref[...], wc_ref, bias_ref[...], mixed_ref)
        mixed = mixed_ref[...]
        yb = u * mixed
        yc, _, _, _, _ = _conv_fwd(zcv_ref[...].astype(F32), halo_ref[...].astype(F32), first, cw_ref[...])
        gout, gpost = gout_ref[...], gpost_ref[...]
        ga, gb_, gc_ = gout[:, :512], gout[:, 512:768], gout[:, 768:]
        na, yah, ra = _rms_fwd(ya_ref[...], ga)
        nb, ybh, rb = _rms_fwd(yb, gb_)
        nc, ych, rc = _rms_fwd(yc, gc_)
        mix = jnp.concatenate([na, nb, nc], axis=1).astype(BF16)
        _, oh, ro = _rms_fwd(_mm(mix, wout_ref[...]), gpost)
        do, dgpost = _rms_bwd(oh, ro, gpost, dx1_ref[...])
        dob = do.astype(BF16)
        dmix = _mm_nt(dob, wout_ref[...])
        dya, dga = _rms_bwd(yah, ra, ga, dmix[:, :512])
        dyb, dgb = _rms_bwd(ybh, rb, gb_, dmix[:, 512:768])
        dyc, dgc = _rms_bwd(ych, rc, gc_, dmix[:, 768:])
        dya_ref[...] = dya
        dyc_ref[...] = dyc
        _acc(dwout_acc, first, _mm_tn(mix, dob))

        @pl.when(i == steps - 1)
        def _():
            dwout_ref[...] = dwout_acc[...].astype(BF16)

        _acc(dgpost_ref, first, dgpost)
        _acc(dgout_ref, first, jnp.concatenate([dga, dgb, dgc], axis=1))
        du = dyb * mixed
        dmixed = dyb * u
        lane = lax.broadcasted_iota(jnp.int32, (CHUNK, SG_W), 1)
        row = lax.broadcasted_iota(jnp.int32, (CHUNK, CHUNK), 0)
        col = lax.broadcasted_iota(jnp.int32, (CHUNK, CHUNK), 1)
        dbias = jnp.zeros((CHUNK, SG_W), F32)
        dw = [jnp.zeros((CHUNK, CHUNK), F32) for _ in range(ng)]
        for c in range(tm // CHUNK):
            rows = slice(c * CHUNK, (c + 1) * CHUNK)
            dm = dmixed[rows]
            dbias = dbias + dm
            dmb = dm.astype(BF16)
            vb = v[rows].astype(BF16)
            dvc = jnp.zeros((CHUNK, SG_W), F32)
            for g in range(ng):
                in_g = lane // GROUP == g
                dvc = dvc + jnp.where(in_g, _mm(wct_ref[g], dmb), 0.0)
                dw[g] = dw[g] + _mm_nt(jnp.where(in_g, dmb, jnp.zeros_like(dmb)), vb)
            dv_ref[rows, :] = dvc
        for g in range(ng):
            dwg = jnp.where(col <= row, dw[g], 0.0)

            @pl.when(first)
            def _():
                dwsp_ref[g] = dwg

            @pl.when(jnp.logical_not(first))
            def _():
                dwsp_ref[g] += dwg
        _acc(dbias_ref, first, _gmean(dbias, gm) * GROUP)
        dv = dv_ref[...]
        _acc(dlng_ref, first, jnp.sum(dv * vh, axis=0, keepdims=True))
        _acc(dlnb_ref, first, jnp.sum(dv, axis=0, keepdims=True))
        dvh = dv * lng
        dv0 = r * (dvh - _gmean(dvh, gm) - vh * _gmean(dvh * vh, gm))
        dzsg_ref[...] = (jnp.concatenate([du, dv0], axis=1) * _gelu_grad(zsg, th)).astype(BF16)

    row_ = lambda w: pl.BlockSpec((tm, w), lambda i: (i, 0))
    vec = lambda w: pl.BlockSpec((1, w), lambda i: (0, 0))
    return pl.pallas_call(
        body, name="mix_bwd", grid=(steps,),
        in_specs=[row_(D),
                  pl.BlockSpec((tm, 768), lambda i: (i, 0)),
                  pl.BlockSpec((HALO, 768), lambda i: (jnp.maximum(i * (tm // HALO) - 1, 0), 0)),
                  pl.BlockSpec((tm, 512), lambda i: (i, 3)),
                  row_(512),
                  _whole(), _whole(), _whole(), _whole(), _whole(), _whole(), _whole(), _whole(), _whole(), _whole()],
        out_specs=[row_(512), row_(CV_W), row_(512), pl.BlockSpec((D, D), lambda i: (0, 0)), vec(D), vec(D), vec(SG_W),
                   vec(SG_W), pl.BlockSpec((ng, CHUNK, CHUNK), lambda i: (0, 0, 0)),
                   pl.BlockSpec((CHUNK, SG_W), lambda i: (0, 0))],
        out_shape=[jax.ShapeDtypeStruct((t, 512), F32), jax.ShapeDtypeStruct((t, CV_W), F32),
                   jax.ShapeDtypeStruct((t, 512), BF16), jax.ShapeDtypeStruct((D, D), BF16),
                   jax.ShapeDtypeStruct((1, D), F32),
                   jax.ShapeDtypeStruct((1, D), F32), jax.ShapeDtypeStruct((1, SG_W), F32),
                   jax.ShapeDtypeStruct((1, SG_W), F32), jax.ShapeDtypeStruct((ng, CHUNK, CHUNK), F32),
                   jax.ShapeDtypeStruct((CHUNK, SG_W), F32)],
        scratch_shapes=[pltpu.VMEM((ng, CHUNK, CHUNK), BF16), pltpu.VMEM((ng, CHUNK, CHUNK), BF16),
                        pltpu.VMEM((tm, SG_W), F32), pltpu.VMEM((tm, SG_W), F32), pltpu.VMEM((D, D), F32)],
        compiler_params=_cp(("arbitrary",)),
    )(dx1, z, z, z, ya, gm, lng, lnb, wsp, wspt, bias, cw, gout, wout, gpost)


def conv_bwd(dyc, z, cw):
    t = dyc.shape[0]
    tm = min(TM, t)
    hb = tm // 8
    last_blk = t // 8 - 1

    def body(dyc_ref, dyct_ref, zcv_ref, head_ref, tail_ref, cw_ref, dz_ref, dcw_ref):
        i = pl.program_id(0)
        first = i == 0
        last = i == pl.num_programs(0) - 1
        cw = cw_ref[...]
        zcv = zcv_ref[...].astype(F32)
        gb, gc, hh = zcv[:, :CV_W], zcv[:, CV_W:2 * CV_W], zcv[:, 2 * CV_W:]
        _, conv, y, y1, y2 = _conv_fwd(zcv, head_ref[...].astype(F32), first, cw)
        dyc = dyc_ref[...]
        dconv = dyc * gb
        tail = jnp.where(last, 0.0, dyct_ref[...] * tail_ref[:8, :CV_W].astype(F32))
        d1 = _shift_up(dconv, 1, tail)
        d2 = _shift_up(dconv, 2, tail)
        dy = dconv * cw[2:3, :] + d1 * cw[1:2, :] + d2 * cw[0:1, :]
        dz_ref[...] = jnp.concatenate([dyc * conv, dy * hh, dy * gc], axis=1).astype(BF16)
        tap = lax.broadcasted_iota(jnp.int32, (8, CV_W), 0)
        dcw = jnp.where(tap == 0, jnp.sum(dconv * y2, axis=0, keepdims=True),
                        jnp.where(tap == 1, jnp.sum(dconv * y1, axis=0, keepdims=True),
                                  jnp.where(tap == 2, jnp.sum(dconv * y, axis=0, keepdims=True), 0.0)))
        _acc(dcw_ref, first, dcw)

    return pl.pallas_call(
        body, name="conv_bwd", grid=(t // tm,),
        in_specs=[pl.BlockSpec((tm, CV_W), lambda i: (i, 0)),
                  pl.BlockSpec((8, CV_W), lambda i: (jnp.minimum((i + 1) * hb, last_blk), 0)),
                  pl.BlockSpec((tm, 768), lambda i: (i, 0)),
                  pl.BlockSpec((HALO, 768), lambda i: (jnp.maximum(i * (tm // HALO) - 1, 0), 0)),
                  pl.BlockSpec((HALO, 768), lambda i: (jnp.minimum((i + 1) * (tm // HALO), t // HALO - 1), 0)),
                  _whole()],
        out_specs=[pl.BlockSpec((tm, 768), lambda i: (i, 0)), pl.BlockSpec((8, CV_W), lambda i: (0, 0))],
        out_shape=[jax.ShapeDtypeStruct((t, 768), BF16), jax.ShapeDtypeStruct((8, CV_W), F32)],
        compiler_params=_cp(("arbitrary",)),
    )(dyc, dyc, z, z, z, cw)


def attn_bwd(q, k, v, o, lse, do, carried=()):
    t = q.shape[0]
    tq = min(TQ, t)
    nq = t // tq
    last_pair = HEADS // 2 - 1
    n = len(carried)

    def body(*refs):
        j = pl.program_id(1)
        q_ref, k_ref, v_ref, o_ref, lse_ref, do_ref = refs[:6]
        dq_out_ref, dk_ref, dv_ref = refs[6 + n:9 + n]
        dq_ref = refs[9 + 2 * n]
        sems = refs[10 + 2 * n:]
        stages = [_exchange_steps(refs[6 + a], refs[9 + n + a], carried[a][1], *sems[3 * a:3 * a + 3]) for a in range(n)]
        if n:
            pl.when((pl.program_id(0) == 0) & (j == 0))(_each(stages, 0))

        @pl.when(j == 0)
        def _():
            dq_ref[...] = jnp.zeros_like(dq_ref)

        row = lax.broadcasted_iota(jnp.int32, (tq, tq), 0)
        col = lax.broadcasted_iota(jnp.int32, (tq, tq), 1)
        vlane = lax.broadcasted_iota(jnp.int32, (tq, 2 * V_DIM), 1)
        head_lanes = [slice(h * HEAD_PAD, (h + 1) * HEAD_PAD) for h in range(2)]

        def step(i, carry, masked):
            start = pl.multiple_of(i * tq, tq)
            do_blk = do_ref[pl.ds(start, tq), :]
            o_blk = o_ref[pl.ds(start, tq), :]
            vb = v_ref[...]
            dks, dv_acc = [], carry[2]
            for h in range(2):
                lanes = head_lanes[h]
                qb = q_ref[pl.ds(start, tq), lanes]
                kb = k_ref[:, lanes]
                dob = jnp.where((vlane // V_DIM) == h, do_blk, 0.0)
                delta = jnp.sum(dob * o_blk, axis=-1, keepdims=True)
                lse2 = lse_ref[pl.ds(start, tq), lanes][:, 0:1] * LOG2E
                s = _mm_nt(qb, kb)
                if masked:
                    s = jnp.where(col <= row, s, NEG)
                p = jnp.exp2(s * SCALE_LOG2E - lse2)
                dob16 = dob.astype(BF16)
                dp = _mm_nt(dob16, vb)
                ds = (p * (dp - delta) * SCALE).astype(BF16)
                dv_acc = dv_acc + _mm_tn(p.astype(BF16), dob16)
                dks.append(carry[h] + _mm_tn(ds, qb))
                dq_ref[pl.ds(start, tq), lanes] += _mm(ds, kb)
            return dks[0], dks[1], dv_acc

        zero = jnp.zeros((tq, HEAD_PAD), F32)
        carry = step(j, (zero, zero, jnp.zeros((tq, 2 * V_DIM), F32)), True)
        rest = nq - 1 - j
        carry = lax.fori_loop(0, rest // 2, lambda u, c: step(j + 2 + 2 * u, step(j + 1 + 2 * u, c, False), False), carry)
        dk0, dk1, dv_acc = lax.fori_loop(0, rest % 2, lambda _, c: step(nq - 1, c, False), carry)
        dk_ref[:, head_lanes[0]] = dk0.astype(BF16)
        dk_ref[:, head_lanes[1]] = dk1.astype(BF16)
        dv_ref[...] = dv_acc.astype(BF16)

        @pl.when(j == nq - 1)
        def _():
            dq_out_ref[...] = dq_ref[...].astype(BF16)

        if n:
            pl.when((pl.program_id(0) == last_pair) & (j == nq - 1))(_each(stages, 1))

    hbm = pl.BlockSpec(memory_space=pl.ANY)
    return pl.pallas_call(
        body, name=f"attn_bwd_exchange{n}" if n else "attn_bwd", grid=(HEADS // 2, nq),
        in_specs=[pl.BlockSpec((t, 2 * HEAD_PAD), lambda p, j: (0, p)),
                  pl.BlockSpec((tq, 2 * HEAD_PAD), lambda p, j: (j, p)),
                  pl.BlockSpec((tq, 2 * V_DIM), lambda p, j: (j, p)),
                  pl.BlockSpec((t, 2 * V_DIM), lambda p, j: (0, p)),
                  pl.BlockSpec((t, 2 * HEAD_PAD), lambda p, j: (0, p)),
                  pl.BlockSpec((t, 2 * V_DIM), lambda p, j: (0, p))] + [hbm] * n,
        out_specs=[pl.BlockSpec((t, 2 * HEAD_PAD), lambda p, j: (0, p)),
                   pl.BlockSpec((tq, 2 * HEAD_PAD), lambda p, j: (j, p)),
                   pl.BlockSpec((tq, 2 * V_DIM), lambda p, j: (j, p))] + [hbm] * n,
        out_shape=[jax.ShapeDtypeStruct((t, HEADS * HEAD_PAD), BF16), jax.ShapeDtypeStruct((t, HEADS * HEAD_PAD), BF16),
                   jax.ShapeDtypeStruct((t, HEADS * V_DIM), BF16)]
        + [jax.ShapeDtypeStruct(src.shape if scatter else (N_DEV,) + src.shape, src.dtype) for src, scatter in carried],
        scratch_shapes=[pltpu.VMEM((t, 2 * HEAD_PAD), F32)] + _comm_sems() * n,
        compiler_params=_cp(("arbitrary", "arbitrary") if n else ("parallel", "arbitrary")),
    )(q, k, v, o, lse, do, *[src for src, _ in carried])


def mla_proj_bwd(dq, dk, dv, z, ca, sb, sc, gq, gkv, wuq, wukv):
    t = z.shape[0]
    tm = min(TM, t)
    steps = t // tm

    def body(dq_ref, dk_ref, dv_ref, z_ref, ca_ref, sb_ref, sc_ref, gq_ref, gkv_ref, wuq_ref, wukv_ref,
             dz_ref, dwuq_ref, dwukv_ref, dgq_ref, dgkv_ref, dqp_ref, dkvp_ref, uq_acc, ukv_acc):
        first = pl.program_id(0) == 0
        z = z_ref[...].astype(F32)
        ca, sb, sc = ca_ref[...], sb_ref[...], sc_ref[...]
        gq, gkv = gq_ref[...], gkv_ref[...]
        cq, cqh, rq = _rms_fwd(z[:, :Q_RANK], gq)
        ckv, ckvh, rkv = _rms_fwd(z[:, Q_RANK:Q_RANK + KV_RANK], gkv)
        lane = lax.broadcasted_iota(jnp.int32, (tm, HEAD_PAD), 1)
        dkr = jnp.zeros((tm, HEAD_PAD), F32)
        for h in range(HEADS):
            lanes = slice(h * HEAD_PAD, (h + 1) * HEAD_PAD)
            dqp_ref[:, lanes] = _rope_t(dq_ref[:, lanes].astype(F32), ca, sb, sc).astype(BF16)
            dkh = dk_ref[:, lanes].astype(F32)
            dkr = dkr + dkh
            dkvp_ref[:, lanes] = jnp.where(lane < NOPE, dkh, 0.0).astype(BF16)
        dkvp_ref[:, HEADS * HEAD_PAD:] = dv_ref[...].astype(BF16)
        dkr = pltpu.roll(_rope_t(jnp.where(lane >= NOPE, dkr, 0.0), ca, sb, sc), HEAD_PAD - NOPE, 1)
        dkr = jnp.where(lane < ROPE, dkr, 0.0)
        dcq = _mm(dqp_ref[...], wuq_ref[...])
        dckv = _mm(dkvp_ref[...], wukv_ref[...])
        dzq, dgq = _rms_bwd(cqh, rq, gq, dcq)
        dzkv, dgkv = _rms_bwd(ckvh, rkv, gkv, dckv)
        dz_ref[...] = jnp.concatenate([dzq, dzkv, dkr], axis=1).astype(BF16)
        _acc(uq_acc, first, _mm_tn(cq.astype(BF16), dqp_ref[...]))
        _acc(ukv_acc, first, _mm_tn(ckv.astype(BF16), dkvp_ref[...]))
        _acc(dgq_ref, first, dgq)
        _acc(dgkv_ref, first, dgkv)

        @pl.when(pl.program_id(0) == steps - 1)
        def _():
            dwuq_ref[...] = uq_acc[...].astype(BF16)
            dwukv_ref[...] = ukv_acc[...].astype(BF16)

    row = lambda w: pl.BlockSpec((tm, w), lambda i: (i, 0))
    vec = lambda w: pl.BlockSpec((1, w), lambda i: (0, 0))
    whole = lambda r, c: pl.BlockSpec((r, c), lambda i: (0, 0))
    nq, nkv = HEADS * HEAD_PAD, HEADS * (HEAD_PAD + V_DIM)
    return pl.pallas_call(
        body, name="mla_proj_bwd", grid=(steps,),
        in_specs=[row(1024), row(1024), row(512), pl.BlockSpec((tm, 768), lambda i: (i, 1)),
                  row(HEAD_PAD), row(HEAD_PAD), row(HEAD_PAD), _whole(), _whole(), _whole(), _whole()],
        out_specs=[row(768), whole(Q_RANK, nq), whole(KV_RANK, nkv), vec(Q_RANK), vec(KV_RANK)],
        out_shape=[jax.ShapeDtypeStruct((t, 768), BF16), jax.ShapeDtypeStruct((Q_RANK, nq), BF16),
                   jax.ShapeDtypeStruct((KV_RANK, nkv), BF16), jax.ShapeDtypeStruct((1, Q_RANK), F32),
                   jax.ShapeDtypeStruct((1, KV_RANK), F32)],
        scratch_shapes=[pltpu.VMEM((tm, nq), BF16), pltpu.VMEM((tm, nkv), BF16), pltpu.VMEM((Q_RANK, nq), F32),
                        pltpu.VMEM((KV_RANK, nkv), F32)],
        compiler_params=_cp(("arbitrary",)),
    )(dq, dk, dv, z, ca, sb, sc, gq, gkv, wuq, wukv)


def pre_in_bwd(x, dx1, dzcv, dzmla, dzsg, g, w):
    t = x.shape[0]
    tm = min(TM, t)
    steps = t // tm

    def body(x_ref, dx1_ref, dzcv_ref, dzmla_ref, dzsg_ref, g_ref, w_ref, dx_ref, dw_ref, dg_ref, acc_ref):
        first = pl.program_id(0) == 0
        g = g_ref[...]
        h, xh, r = _rms_fwd(x_ref[...], g)
        dz = jnp.concatenate([dzcv_ref[...], dzmla_ref[...], dzsg_ref[...]], axis=1)
        dx, dg = _rms_bwd(xh, r, g, _mm(dz, w_ref[...]))
        dx_ref[...] = dx1_ref[...] + dx
        _acc(acc_ref, first, _mm_tn(dz, h.astype(BF16)))
        _acc(dg_ref, first, dg)

        @pl.when(pl.program_id(0) == steps - 1)
        def _():
            dw_ref[...] = acc_ref[...].astype(BF16)

    row = lambda w_: pl.BlockSpec((tm, w_), lambda i: (i, 0))
    return pl.pallas_call(
        body, name="pre_in_bwd", grid=(steps,),
        in_specs=[row(D), row(D), row(768), row(768), row(512), _whole(), _whole()],
        out_specs=[row(D), pl.BlockSpec((Z_W, D), lambda i: (0, 0)), pl.BlockSpec((1, D), lambda i: (0, 0))],
        out_shape=[jax.ShapeDtypeStruct((t, D), F32), jax.ShapeDtypeStruct((Z_W, D), BF16),
                   jax.ShapeDtypeStruct((1, D), F32)],
        scratch_shapes=[pltpu.VMEM((Z_W, D), F32)],
        compiler_params=_cp(("arbitrary",)),
    )(x, dx1, dzcv, dzmla, dzsg, g, w)


MESH = pl.DeviceIdType.MESH


def _place():
    return lax.axis_index("x"), lax.axis_index("y"), lax.axis_index("c")


def _comm_sems():
    return [pltpu.SemaphoreType.DMA((7,)), pltpu.SemaphoreType.DMA((7,)), pltpu.SemaphoreType.DMA]


def _gather_steps(x_ref, out_ref, send_sems, recv_sems, local_sem):
    x, y, c = _place()
    me, sibling = (x, y, c), (x, y, 1 - c)
    chips = [(1 - x, y), (x, 1 - y), (1 - x, 1 - y)]

    def slot(px, py, pc):
        return out_ref.at[4 * px + 2 * py + pc]

    def copy(k, blk, to, src=None):
        return pltpu.make_async_remote_copy(
            src_ref=slot(*blk) if src is None else src, dst_ref=slot(*blk),
            send_sem=send_sems.at[k], recv_sem=recv_sems.at[k], device_id=to, device_id_type=MESH)

    mine = pltpu.make_async_copy(x_ref, slot(*me), local_sem)
    first = [copy(0, me, sibling, src=x_ref)] + [copy(1 + j, me, (*chip, c), src=x_ref) for j, chip in enumerate(chips)]
    passed = [copy(4 + j, (*chip, c), sibling) for j, chip in enumerate(chips)]

    def start():
        mine.start()
        for cp in first:
            cp.start()

    def forward():
        for j, chip in enumerate(chips):
            copy(1 + j, (*chip, c), me).wait_recv()
            passed[j].start()

    def finish():
        copy(0, sibling, me).wait_recv()
        for j, chip in enumerate(chips):
            copy(4 + j, (*chip, 1 - c), me).wait_recv()
        for cp in first + passed:
            cp.wait_send()
        mine.wait()

    return start, forward, finish


def _exchange_steps(src_ref, out_ref, scatter, send_sems, recv_sems, local_sem):
    x, y, c = _place()
    me = 4 * x + 2 * y + c
    own = pltpu.make_async_copy(src_ref.at[me] if scatter else src_ref, out_ref.at[me], local_sem)
    copies = []
    for k in range(1, N_DEV):
        px = 1 - x if k & 4 else x
        py = 1 - y if k & 2 else y
        pc = 1 - c if k & 1 else c
        copies.append(pltpu.make_async_remote_copy(
            src_ref=src_ref.at[4 * px + 2 * py + pc] if scatter else src_ref, dst_ref=out_ref.at[me],
            send_sem=send_sems.at[k - 1], recv_sem=recv_sems.at[k - 1], device_id=(px, py, pc), device_id_type=MESH))

    def start():
        own.start()
        for cp in copies:
            cp.start()

    def finish():
        for cp in copies:
            cp.wait_recv()
        for cp in copies:
            cp.wait_send()
        own.wait()

    return start, finish


def all_gather(block):
    def body(x_ref, out_ref, *sems):
        for stage in _gather_steps(x_ref, out_ref, *sems):
            stage()

    return pl.pallas_call(
        body, name="all_gather",
        in_specs=[pl.BlockSpec(memory_space=pl.ANY)],
        out_specs=pl.BlockSpec(memory_space=pl.ANY),
        out_shape=jax.ShapeDtypeStruct((N_DEV,) + block.shape, block.dtype),
        scratch_shapes=_comm_sems(),
    )(block)


def _row_tile(r, cap):
    return max(d for d in range(16, cap + 1, 16) if r % d == 0)


def sum_adamw(parts, w, m, v, cap, carried=()):
    nl, r, c = w.shape
    tr = _row_tile(r, cap)
    steps = r // tr
    n = len(carried)
    c1 = 1.0 / (1.0 - ADAM_B1 ** ADAM_STEP)
    c2 = 1.0 / (1.0 - ADAM_B2 ** ADAM_STEP)

    def body(*refs):
        p_refs = refs[:nl]
        w_ref, m_ref, v_ref = refs[nl:nl + 3]
        g_ref, d_ref, nm_ref, nv_ref = refs[nl + 3 + n:nl + 7 + n]
        sems = refs[nl + 7 + 2 * n:]
        stages = [_exchange_steps(refs[nl + 3 + a], refs[nl + 7 + n + a], carried[a][1], *sems[3 * a:3 * a + 3])
                  for a in range(n)]
        layer, i = pl.program_id(0), pl.program_id(1)
        if n:
            pl.when((layer == 0) & (i == 0))(_each(stages, 0))

        def update(p_ref):
            g = p_ref[0].astype(F32)
            for k in range(1, N_DEV):
                g = g + p_ref[k].astype(F32)
            m_new = ADAM_B1 * m_ref[...] + (1.0 - ADAM_B1) * g
            v_new = ADAM_B2 * v_ref[...] + (1.0 - ADAM_B2) * (g * g)
            g_ref[...] = g
            nm_ref[...] = m_new
            nv_ref[...] = v_new
            d_ref[...] = -ADAM_LR * ((m_new * c1) / (jnp.sqrt(v_new * c2) + ADAM_EPS) + ADAM_WD * w_ref[...])

        for k in range(nl):
            pl.when(layer == k)(functools.partial(update, p_refs[k]))
        if n:
            pl.when((layer == nl - 1) & (i == steps - 1))(_each(stages, 1))

    def parts_spec(k):
        return pl.BlockSpec((N_DEV, tr, c), lambda l, i: (0, jnp.where(l == k, i, jnp.where(l < k, 0, steps - 1)), 0))

    blk = pl.BlockSpec((None, tr, c), lambda l, i: (l, i, 0))
    out = jax.ShapeDtypeStruct((nl, r, c), F32)
    hbm = pl.BlockSpec(memory_space=pl.ANY)
    return pl.pallas_call(
        body, name=f"sum_adamw_exchange{n}" if n else "sum_adamw", grid=(nl, steps),
        in_specs=[parts_spec(k) for k in range(nl)] + [blk, blk, blk] + [hbm] * n,
        out_specs=[blk, blk, blk, blk] + [hbm] * n,
        out_shape=[out, out, out, out]
        + [jax.ShapeDtypeStruct(src.shape if scatter else (N_DEV,) + src.shape, src.dtype) for src, scatter in carried],
        scratch_shapes=_comm_sems() * n,
        compiler_params=_cp(("arbitrary", "arbitrary")),
    )(*parts, w, m, v, *[src for src, _ in carried])


PACK_W = 1024
MIX_PIECES = (("w_out", D // N_DEV, D, False), ("w_uq", HEADS * (NOPE + ROPE) // N_DEV, Q_RANK, True),
              ("w_ukv", HEADS * (NOPE + V_DIM) // N_DEV, KV_RANK, True), ("conv", 16, PACK_W, False),
              ("w_in", IN_W // N_DEV, D, True))
FFN_PIECES = (("w_gate", D_FF // N_DEV, D, True), ("w_up", D_FF // N_DEV, D, True), ("w_down", D_FF // N_DEV, D, False))
def _packed_rows(rows, cols):
    return rows * cols // PACK_W


OFFSET = {}
for _pieces in (MIX_PIECES, FFN_PIECES):
    _off = 0
    for _name, _rows, _cols, _ in _pieces:
        assert _rows * _cols % PACK_W == 0
        OFFSET[_name] = _off
        _off += _packed_rows(_rows, _cols) + -_packed_rows(_rows, _cols) % 16
assert all(o % 16 == 0 for o in OFFSET.values())
assert [OFFSET[n] for n in ("w_gate", "w_up", "w_down")] == [0, FFN_SHARD, 2 * FFN_SHARD]
CONV_BITS = 3 * (CV_W // N_DEV) * 2


def _to_pack(shards, dtype, pieces, conv=None):
    nl = shards["w_in"].shape[0]
    parts = []
    for name, rows, cols, transposed in pieces:
        if name == "conv":
            if conv is None:
                a = jnp.zeros((nl, rows, PACK_W), dtype)
            else:
                bits = lax.bitcast_convert_type(conv.astype(F32), BF16).reshape(nl, CONV_BITS)
                a = jnp.pad(bits, ((0, 0), (0, rows * PACK_W - CONV_BITS))).reshape(nl, rows, PACK_W)
        else:
            a = shards[name].astype(dtype)
            a = (jnp.swapaxes(a, 1, 2) if transposed else a).reshape(nl, _packed_rows(rows, cols), PACK_W)
            a = jnp.pad(a, ((0, 0), (0, -a.shape[1] % 16), (0, 0)))
        parts.append(a)
    return jnp.concatenate(parts, axis=1)


def _from_pack(pack, pieces):
    out = {}
    for name, rows, cols, transposed in pieces:
        if name != "conv":
            a = pack[:, OFFSET[name]:OFFSET[name] + _packed_rows(rows, cols)].reshape(pack.shape[0], rows, cols)
            out[name] = jnp.swapaxes(a, 1, 2) if transposed else a
    return out


def _mix_weights(g):
    def rows(name):
        _, n, cols, _ = next(p for p in MIX_PIECES if p[0] == name)
        return g[:, OFFSET[name]:OFFSET[name] + _packed_rows(n, cols)].reshape(N_DEV, n, cols)

    w_in_t = rows("w_in").reshape(IN_W, D)
    w_in_p = jnp.concatenate([w_in_t[1184:], w_in_t[:672], jnp.zeros((96, D), BF16), w_in_t[672:1184]], axis=0)
    w_uq_p = jnp.pad(rows("w_uq"), ((0, 0), (0, HEAD_PAD - NOPE - ROPE), (0, 0))).reshape(HEADS * HEAD_PAD, Q_RANK)
    kv = rows("w_ukv")
    w_k = jnp.pad(kv[:, :NOPE], ((0, 0), (0, HEAD_PAD - NOPE), (0, 0))).reshape(HEADS * HEAD_PAD, KV_RANK)
    w_ukv_p = jnp.concatenate([w_k, kv[:, NOPE:].reshape(HEADS * V_DIM, KV_RANK)], axis=0)
    bits = rows("conv").reshape(N_DEV, -1)[:, :CONV_BITS].reshape(N_DEV, 3, CV_W // N_DEV, 2)
    conv_w = jnp.moveaxis(lax.bitcast_convert_type(bits, F32), 0, 1).reshape(3, CV_W)
    return dict(w_in=w_in_p, w_uq=w_uq_p, w_ukv=w_ukv_p, w_out=rows("w_out").reshape(D, D), conv_w=conv_w)


def _grad_chunks(full):
    d_in = full["w_in"]
    d_in = jnp.concatenate([d_in[768:768 + 672], d_in[1536:], d_in[:768]], axis=0)
    d_uq = full["w_uq"].reshape(HEADS, HEAD_PAD, Q_RANK)[:, :NOPE + ROPE]
    d_k = full["w_ukv"][:HEADS * HEAD_PAD].reshape(HEADS, HEAD_PAD, KV_RANK)[:, :NOPE]
    d_v = full["w_ukv"][HEADS * HEAD_PAD:].reshape(HEADS, V_DIM, KV_RANK)
    mats = dict(w_in=d_in, w_uq=d_uq, w_ukv=jnp.concatenate([d_k, d_v], axis=1), w_out=full["w_out"])
    parts = []
    for name, rows, cols, _ in MIX_PIECES:
        if name == "conv":
            parts.append(jnp.zeros((N_DEV, rows, PACK_W), BF16))
        else:
            a = mats[name].reshape(N_DEV, _packed_rows(rows, cols), PACK_W)
            parts.append(jnp.pad(a, ((0, 0), (0, -a.shape[1] % 16), (0, 0))))
    return jnp.concatenate(parts, axis=1)


SMALL = (("mix_pre_g", (D,)), ("mix_post_g", (D,)), ("ffn_pre_g", (D,)), ("ffn_post_g", (D,)), ("q_norm_g", (Q_RANK,)),
         ("kv_norm_g", (KV_RANK,)), ("sg_ln_g", (SG_W,)), ("sg_ln_b", (SG_W,)), ("w_sp", (4, CHUNK, CHUNK)),
         ("b_sp", (4, CHUNK)), ("out_norm_g", (D,)))
SMALL_ROWS = 576


def _pack_small(vals, nl):
    flat = jnp.concatenate([vals[name].reshape(nl, -1) for name, _ in SMALL] + [vals["conv_w"].reshape(nl, -1)], axis=1)
    return jnp.pad(flat, ((0, 0), (0, SMALL_ROWS * 128 - flat.shape[1]))).reshape(nl * SMALL_ROWS, 128)


def _unpack_small(pack, nl):
    flat = pack.reshape(nl, SMALL_ROWS * 128)
    out, off = {}, 0
    for name, shape in SMALL + (("conv_w", (3, CV_W)),):
        n = int(np.prod(shape))
        out[name] = flat[:, off:off + n].reshape((nl,) + shape)
        off += n
    return out


def _layer_fwd(x, lw, sp, tabs, consts, ffn_pack, next_mix_pack):
    ca, sb, sc = tabs
    z = pre_in_fwd(x, sp["mix_pre_g"], lw["w_in"])
    q, k, v = mla_proj_fwd(z, ca, sb, sc, sp["q_norm_g"], sp["kv_norm_g"], lw["w_uq"], lw["w_ukv"])
    ya, lse, lw["ffn"] = attn_fwd(q, k, v, (ffn_pack,))
    x1 = mix_fwd(x, z, ya, consts["gm"], sp["sg_ln_g"], sp["sg_ln_b"], sp["w_sp"], sp["bias"], lw["conv_w"],
                 sp["out_norm_g"], lw["w_out"], sp["mix_post_g"])
    x2, f, *mix_gathered = ffn_fwd(x1, sp["ffn_pre_g"], lw["ffn"], sp["ffn_post_g"], next_mix_pack)
    return x2, (x, z, q, k, v, ya, lse, x1, f), mix_gathered


def _layer_bwd(dx2, saved, lw, sp, tabs, consts, pending):
    ca, sb, sc = tabs
    x, z, q, k, v, ya, lse, x1, f = saved
    dx1, h2, dab, s, df, d_ffn_pre, d_ffn_post, *received = ffn_bwd(x1, f, dx2, sp["ffn_pre_g"], lw["ffn"], sp["ffn_post_g"],
                                                                    pending)
    ffn_chunks = atb_ffn_chunks(s, df, 2, atb_ffn_chunks(dab, h2, 0)).reshape(N_DEV, len(FFN_PIECES) * FFN_SHARD, D)
    dya, dyc, dzsg, d_w_out, d_mix_post, d_out_norm, d_lng, d_lnb, d_wsp, d_bias = mix_bwd(
        dx1, z, ya, consts["gm"], sp["sg_ln_g"], sp["sg_ln_b"], sp["w_sp"], sp["w_sp_t"], sp["bias"], lw["conv_w"],
        sp["out_norm_g"], lw["w_out"], sp["mix_post_g"])
    dzcv, d_cw = conv_bwd(dyc, z, lw["conv_w"])
    dq, dk, dv, got_ffn = attn_bwd(q, k, v, ya, lse, dya, ((ffn_chunks, True),))
    dzmla, d_w_uq, d_w_ukv, d_gq, d_gkv = mla_proj_bwd(dq, dk, dv, z, ca, sb, sc, sp["q_norm_g"], sp["kv_norm_g"],
                                                       lw["w_uq"], lw["w_ukv"])
    dx, d_w_in, d_mix_pre = pre_in_bwd(x, dx1, dzcv, dzmla, dzsg, sp["mix_pre_g"], lw["w_in"])
    mix_chunks = _grad_chunks(dict(w_in=d_w_in, w_uq=d_w_uq.T, w_ukv=d_w_ukv.T, w_out=d_w_out))
    d_bsp = d_bias[:, ::GROUP].T
    small = dict(mix_pre_g=d_mix_pre[0], mix_post_g=d_mix_post[0], ffn_pre_g=d_ffn_pre[0], ffn_post_g=d_ffn_post[0],
                 q_norm_g=d_gq[0], kv_norm_g=d_gkv[0], sg_ln_g=d_lng[0], sg_ln_b=d_lnb[0], w_sp=d_wsp, b_sp=d_bsp,
                 out_norm_g=d_out_norm[0], conv_w=d_cw[:3])
    small_pack = _pack_small({name: a[None] for name, a in small.items()}, 1)
    return dx, ((mix_chunks, True), (small_pack, False)), [got_ffn] + received


def kernel(x, positions, mix_pre_g, mix_post_g, ffn_pre_g, ffn_post_g, w_in, q_norm_g, w_uq, kv_norm_g, w_ukv, sg_ln_g, sg_ln_b, w_sp, b_sp, conv_w, out_norm_g, w_out, w_gate, w_up, w_down, loss_target, m_mix_pre_g, m_mix_post_g, m_ffn_pre_g, m_ffn_post_g, m_w_in, m_q_norm_g, m_w_uq, m_kv_norm_g, m_w_ukv, m_sg_ln_g, m_sg_ln_b, m_w_sp, m_b_sp, m_conv_w, m_out_norm_g, m_w_out, m_w_gate, m_w_up, m_w_down, v_mix_pre_g, v_mix_post_g, v_ffn_pre_g, v_ffn_post_g, v_w_in, v_q_norm_g, v_w_uq, v_kv_norm_g, v_w_ukv, v_sg_ln_g, v_sg_ln_b, v_w_sp, v_b_sp, v_conv_w, v_out_norm_g, v_w_out, v_w_gate, v_w_up, v_w_down):
    nl = w_in.shape[0]
    t = x.shape[1]
    w = dict(mix_pre_g=mix_pre_g, mix_post_g=mix_post_g, ffn_pre_g=ffn_pre_g, ffn_post_g=ffn_post_g, w_in=w_in,
             q_norm_g=q_norm_g, w_uq=w_uq, kv_norm_g=kv_norm_g, w_ukv=w_ukv, sg_ln_g=sg_ln_g, sg_ln_b=sg_ln_b, w_sp=w_sp,
             b_sp=b_sp, conv_w=conv_w, out_norm_g=out_norm_g, w_out=w_out, w_gate=w_gate, w_up=w_up, w_down=w_down)
    m = dict(mix_pre_g=m_mix_pre_g, mix_post_g=m_mix_post_g, ffn_pre_g=m_ffn_pre_g, ffn_post_g=m_ffn_post_g, w_in=m_w_in,
             q_norm_g=m_q_norm_g, w_uq=m_w_uq, kv_norm_g=m_kv_norm_g, w_ukv=m_w_ukv, sg_ln_g=m_sg_ln_g, sg_ln_b=m_sg_ln_b,
             w_sp=m_w_sp, b_sp=m_b_sp, conv_w=m_conv_w, out_norm_g=m_out_norm_g, w_out=m_w_out, w_gate=m_w_gate,
             w_up=m_w_up, w_down=m_w_down)
    v = dict(mix_pre_g=v_mix_pre_g, mix_post_g=v_mix_post_g, ffn_pre_g=v_ffn_pre_g, ffn_post_g=v_ffn_post_g, w_in=v_w_in,
             q_norm_g=v_q_norm_g, w_uq=v_w_uq, kv_norm_g=v_kv_norm_g, w_ukv=v_w_ukv, sg_ln_g=v_sg_ln_g, sg_ln_b=v_sg_ln_b,
             w_sp=v_w_sp, b_sp=v_b_sp, conv_w=v_conv_w, out_norm_g=v_out_norm_g, w_out=v_w_out, w_gate=v_w_gate,
             w_up=v_w_up, w_down=v_w_down)

    mix_pack = _to_pack(w, BF16, MIX_PIECES, conv=w["conv_w"])
    ffn_pack = _to_pack(w, BF16, FFN_PIECES)
    consts = dict(gm=jnp.asarray(np.kron(np.eye(SG_W // GROUP), np.full((GROUP, GROUP), 1.0 / GROUP)), BF16))
    smalls = []
    for l in range(nl):
        sp = {name: w[name][l].reshape(1, -1) for name, shape in SMALL if len(shape) == 1}
        sp["w_sp"] = w["w_sp"][l]
        sp["w_sp_t"] = jnp.swapaxes(w["w_sp"][l], 1, 2)
        sp["bias"] = jnp.repeat(w["b_sp"][l].T, GROUP, axis=1)
        smalls.append(sp)
    inv_freq = 1.0 / (ROPE_THETA ** (jnp.arange(0, ROPE // 2, dtype=F32) / (ROPE // 2)))
    inv = jnp.zeros((1, HEAD_PAD), F32).at[0, NOPE:NOPE + ROPE].set(jnp.concatenate([inv_freq, inv_freq]))
    tabs = rope_tables(positions.reshape(t, 1).astype(F32), inv)

    h = x[0]
    saved, layers = [], []
    mix_gathered = [all_gather(mix_pack[0])]
    for l in range(nl):
        layers.append(_mix_weights(mix_gathered[0]))
        h, s, mix_gathered = _layer_fwd(h, layers[l], smalls[l], tabs, consts, ffn_pack[l],
                                        (mix_pack[l + 1],) if l + 1 < nl else ())
        saved.append(s)
    sq, dh = loss_head(h, loss_target[0])
    loss = lax.psum(0.5 * sq[0, 0] / D, ("x", "y", "c"))

    got_ffn, got_mix, got_small = [None] * nl, [None] * nl, [None] * nl
    pending = ()
    for l in reversed(range(nl)):
        dh, new_pending, received = _layer_bwd(dh, saved[l], layers[l], smalls[l], tabs, consts, pending)
        got_ffn[l] = received[0]
        if pending:
            got_mix[l + 1], got_small[l + 1] = received[1:]
        pending = new_pending

    me = 4 * lax.axis_index("x") + 2 * lax.axis_index("y") + lax.axis_index("c")
    *ffn_new, got_mix[0], got_small[0] = sum_adamw(got_ffn, *[_to_pack(d, F32, FFN_PIECES) for d in (w, m, v)], 176,
                                                   carried=pending)
    mix_new = sum_adamw(got_mix, *[_to_pack(d, F32, MIX_PIECES) for d in (w, m, v)], 208)
    got_small = jnp.concatenate(got_small, axis=1)
    g_big, d_big, m_big, v_big = [{**_from_pack(a, FFN_PIECES), **_from_pack(b, MIX_PIECES)}
                                  for a, b in zip(ffn_new, mix_new)]

    def full_conv(a):
        return lax.dynamic_update_slice(jnp.zeros((nl, 3, CV_W), F32), a, (0, 0, me * (CV_W // N_DEV)))

    def small_pack(d):
        return _pack_small({**{name: d[name] for name, _ in SMALL}, "conv_w": full_conv(d["conv_w"])}, nl)

    g_small, d_small, m_small, v_small = [_unpack_small(p[0], nl) for p in
                                          sum_adamw([got_small], small_pack(w)[None], small_pack(m)[None],
                                                    small_pack(v)[None], 1152)]
    outs = []
    for big, small in ((g_big, g_small), (d_big, d_small), (m_big, m_small), (v_big, v_small)):
        for name in w:
            if name == "conv_w":
                outs.append(lax.dynamic_slice(small[name], (0, 0, me * (CV_W // N_DEV)), (nl, 3, CV_W // N_DEV)))
            elif name in small:
                outs.append(small[name])
            else:
                outs.append(big[name])
    return (loss, dh[None], *outs)
```

```python
import functools

import jax
import jax.numpy as jnp
import numpy as np
from jax import lax
from jax.experimental import pallas as pl
from jax.experimental.pallas import tpu as pltpu

F32 = jnp.float32
BF16 = jnp.bfloat16

D = 1024
Q_RANK = 384
KV_RANK = 256
ROPE = 32
HEADS = 8
NOPE = 64
V_DIM = 64
HEAD_PAD = 128
SG_W = 256
CV_W = 256
CHUNK = 128
GROUP = 64
D_FF = 2816
IN_W = 1952
Z_W = 2048
Z_CV, Z_MLA, Z_SG = 0, 768, 1536
EPS = 1e-6
ROPE_THETA = 10000.0
SCALE = (NOPE + ROPE) ** -0.5
LOG2E = 1.4426950408889634
SCALE_LOG2E = SCALE * LOG2E
NEG = -1e30
N_DEV = 8

ADAM_LR, ADAM_B1, ADAM_B2, ADAM_EPS, ADAM_WD, ADAM_STEP = 0.001, 0.9, 0.999, 1e-08, 0.01, 10

VMEM_LIMIT = 56 * 1024 * 1024

TM = 512
TM_ROW = 1024
TM_FFN = 256
FFN_SLAB = 256
HALO = 16
TQ = 512
FWD_UNROLL = 2
FWD_HEADS = 2
TT = 2048


def _cp(sem, vmem=VMEM_LIMIT):
    return pltpu.CompilerParams(dimension_semantics=sem, vmem_limit_bytes=vmem)


def _whole():
    return pl.BlockSpec(memory_space=pltpu.VMEM)


def _mm(a, b):
    return jnp.dot(a, b, preferred_element_type=F32)


def _mm_nt(a, b):
    return lax.dot_general(a, b, (((1,), (1,)), ((), ())), preferred_element_type=F32)


def _mm_tn(a, b):
    return lax.dot_general(a, b, (((0,), (0,)), ((), ())), preferred_element_type=F32)


def _rms_fwd(x, g):
    r = lax.rsqrt(jnp.mean(x * x, axis=-1, keepdims=True) + EPS)
    xh = x * r
    return xh * g, xh, r


def _rms_bwd(xh, r, g, dy):
    dxh = dy * g
    dx = r * (dxh - xh * jnp.mean(dxh * xh, axis=-1, keepdims=True))
    dg = jnp.sum(dy * xh, axis=0, keepdims=True)
    return dx, dg


def _gmean(v, gm):
    hi = v.astype(BF16)
    lo = (v - hi.astype(F32)).astype(BF16)
    return _mm(hi, gm) + _mm(lo, gm)


def _gelu(x):
    c = np.float32(np.sqrt(2.0 / np.pi))
    u = c * (x + 0.044715 * x * x * x)
    t = jnp.tanh(u)
    return 0.5 * x * (1.0 + t), t


def _gelu_grad(x, t):
    c = np.float32(np.sqrt(2.0 / np.pi))
    return 0.5 * (1.0 + t) + 0.5 * x * (1.0 - t * t) * c * (1.0 + 3.0 * 0.044715 * x * x)


def _rope(t, ca, sb, sc):
    return t * ca + pltpu.roll(t, HEAD_PAD - 16, 1) * sb + pltpu.roll(t, 16, 1) * sc


def _rope_t(dt, ca, sb, sc):
    return dt * ca + pltpu.roll(dt * sb, 16, 1) + pltpu.roll(dt * sc, HEAD_PAD - 16, 1)


def _shift_down(y, k, head):
    out = pltpu.roll(y, k, 0)
    row = lax.broadcasted_iota(jnp.int32, y.shape, 0)
    for j in range(k):
        out = jnp.where(row == j, head[head.shape[0] - k + j:head.shape[0] - k + j + 1, :], out)
    return out


def _shift_up(y, k, tail):
    n = y.shape[0]
    out = pltpu.roll(y, n - k, 0)
    row = lax.broadcasted_iota(jnp.int32, y.shape, 0)
    for j in range(k):
        out = jnp.where(row == n - k + j, tail[j:j + 1, :], out)
    return out


def rope_tables(pos, inv):
    t = pos.shape[0]
    tm = min(TM, t)

    def body(pos_ref, inv_ref, ca_ref, sb_ref, sc_ref):
        ang = pos_ref[...] * inv_ref[...]
        c = jnp.cos(ang)
        s = jnp.sin(ang)
        lane = lax.broadcasted_iota(jnp.int32, ang.shape, 1)
        ca_ref[...] = jnp.where(lane < NOPE, 1.0, jnp.where(lane < NOPE + ROPE, c, 0.0))
        sb_ref[...] = jnp.where((lane >= NOPE) & (lane < NOPE + 16), -s, 0.0)
        sc_ref[...] = jnp.where((lane >= NOPE + 16) & (lane < NOPE + ROPE), s, 0.0)

    out = jax.ShapeDtypeStruct((t, HEAD_PAD), F32)
    blk = pl.BlockSpec((tm, HEAD_PAD), lambda i: (i, 0))
    return pl.pallas_call(
        body, name="rope_tables", grid=(t // tm,),
        in_specs=[pl.BlockSpec((tm, 1), lambda i: (i, 0)), pl.BlockSpec((1, HEAD_PAD), lambda i: (0, 0))],
        out_specs=[blk, blk, blk], out_shape=[out, out, out],
        compiler_params=_cp(("parallel",)),
    )(pos, inv)


def pre_in_fwd(x, g, w):
    t = x.shape[0]
    tm = min(TM_ROW, t)

    def body(x_ref, g_ref, w_ref, z_ref):
        h, _, _ = _rms_fwd(x_ref[...], g_ref[...])
        z_ref[...] = _mm_nt(h.astype(BF16), w_ref[...]).astype(BF16)

    return pl.pallas_call(
        body, name="pre_in_fwd", grid=(t // tm,),
        in_specs=[pl.BlockSpec((tm, D), lambda i: (i, 0)), _whole(), _whole()],
        out_specs=pl.BlockSpec((tm, Z_W), lambda i: (i, 0)),
        out_shape=jax.ShapeDtypeStruct((t, Z_W), BF16),
        compiler_params=_cp(("parallel",)),
    )(x, g, w)


def mla_proj_fwd(z, ca, sb, sc, gq, gkv, wuq, wukv):
    t = z.shape[0]
    tm = min(TM_ROW, t)

    def body(z_ref, ca_ref, sb_ref, sc_ref, gq_ref, gkv_ref, wuq_ref, wukv_ref, q_ref, k_ref, v_ref):
        z = z_ref[...].astype(F32)
        ca, sb, sc = ca_ref[...], sb_ref[...], sc_ref[...]
        cq, _, _ = _rms_fwd(z[:, :Q_RANK], gq_ref[...])
        ckv, _, _ = _rms_fwd(z[:, Q_RANK:Q_RANK + KV_RANK], gkv_ref[...])
        q = _mm_nt(cq.astype(BF16), wuq_ref[...])
        kv = _mm_nt(ckv.astype(BF16), wukv_ref[...])
        kr = _rope(pltpu.roll(z[:, Q_RANK + KV_RANK:], NOPE, 1), ca, sb, sc)
        for h in range(HEADS):
            lanes = slice(h * HEAD_PAD, (h + 1) * HEAD_PAD)
            q_ref[:, lanes] = _rope(q[:, lanes], ca, sb, sc).astype(BF16)
            k_ref[:, lanes] = (kv[:, lanes] + kr).astype(BF16)
        v_ref[...] = kv[:, HEADS * HEAD_PAD:].astype(BF16)

    tab = pl.BlockSpec((tm, HEAD_PAD), lambda i: (i, 0))
    return pl.pallas_call(
        body, name="mla_proj_fwd", grid=(t // tm,),
        in_specs=[pl.BlockSpec((tm, 768), lambda i: (i, 1)), tab, tab, tab, _whole(), _whole(), _whole(), _whole()],
        out_specs=[pl.BlockSpec((tm, HEADS * HEAD_PAD), lambda i: (i, 0)),
                   pl.BlockSpec((tm, HEADS * HEAD_PAD), lambda i: (i, 0)),
                   pl.BlockSpec((tm, HEADS * V_DIM), lambda i: (i, 0))],
        out_shape=[jax.ShapeDtypeStruct((t, HEADS * HEAD_PAD), BF16),
                   jax.ShapeDtypeStruct((t, HEADS * HEAD_PAD), BF16),
                   jax.ShapeDtypeStruct((t, HEADS * V_DIM), BF16)],
        compiler_params=_cp(("parallel",)),
    )(z, ca, sb, sc, gq, gkv, wuq, wukv)


def _each(stages, k):
    def run():
        for stage in stages:
            stage[k]()
    return run


def attn_fwd(q, k, v, carried=()):
    t = q.shape[0]
    tq = min(TQ, t)
    nq = t // tq
    hs = FWD_HEADS
    last_pair = HEADS // hs - 1
    n = len(carried)

    def body(*refs):
        q_ref, k_ref, v_ref = refs[:3]
        o_ref, lse_ref = refs[3 + n:5 + n]
        sems = refs[5 + 2 * n:]
        stages = [_gather_steps(refs[3 + a], refs[5 + n + a], *sems[3 * a:3 * a + 3]) for a in range(n)]
        if n:
            pair = pl.program_id(0)
            pl.when((pair == 0) & (pl.program_id(1) == 0))(_each(stages, 0))
            pl.when((pair == last_pair) & (pl.program_id(1) == 0))(_each(stages, 1))
        i = pl.program_id(1)
        row = lax.broadcasted_iota(jnp.int32, (tq, tq), 0)
        col = lax.broadcasted_iota(jnp.int32, (tq, tq), 1)
        head_lanes = [slice(h * HEAD_PAD, (h + 1) * HEAD_PAD) for h in range(hs)]
        pair_lanes = [slice(p * 2 * V_DIM, (p + 1) * 2 * V_DIM) for p in range(hs // 2)]

        def step(j, carry, masked):
            start = pl.multiple_of(j * tq, tq)
            out = []
            for h in range(hs):
                m, l, acc = carry[h]
                s = _mm_nt(q_ref[:, head_lanes[h]], k_ref[pl.ds(start, tq), head_lanes[h]])
                if masked:
                    s = jnp.where(col <= row, s, NEG)
                m_new = jnp.maximum(m, jnp.max(s, axis=-1, keepdims=True))
                p = jnp.exp2((s - m_new) * SCALE_LOG2E)
                alpha = jnp.exp2((m - m_new) * SCALE_LOG2E)
                l = alpha * l + jnp.sum(p, axis=-1, keepdims=True)
                acc = alpha * acc + _mm(p.astype(BF16), v_ref[pl.ds(start, tq), pair_lanes[h // 2]])
                out.append((m_new, l, acc))
            return tuple(out)

        init = (jnp.full((tq, 1), NEG, F32), jnp.zeros((tq, 1), F32), jnp.zeros((tq, 2 * V_DIM), F32))
        def trip(j, c):
            for u in range(FWD_UNROLL):
                c = step(FWD_UNROLL * j + u, c, False)
            return c

        carry = lax.fori_loop(0, i // FWD_UNROLL, trip, (init,) * hs)
        carry = lax.fori_loop(i - i % FWD_UNROLL, i, lambda j, c: step(j, c, False), carry)
        outs = []
        for h, (m, l, acc) in enumerate(step(i, carry, True)):
            outs.append(acc / l)
            lse_ref[:, head_lanes[h]] = jnp.broadcast_to(m * SCALE + jnp.log(l), (tq, HEAD_PAD))
        lane = lax.broadcasted_iota(jnp.int32, (tq, 2 * V_DIM), 1)
        for p in range(hs // 2):
            o_ref[:, pair_lanes[p]] = jnp.where(lane < V_DIM, outs[2 * p], outs[2 * p + 1])
        if n:
            pl.when((pl.program_id(0) == last_pair) & (i == nq - 1))(_each(stages, 2))

    hbm = pl.BlockSpec(memory_space=pl.ANY)
    return pl.pallas_call(
        body, name=f"attn_fwd_gather{n}" if n else "attn_fwd", grid=(HEADS // hs, nq),
        in_specs=[pl.BlockSpec((tq, hs * HEAD_PAD), lambda p, i: (i, p)),
                  pl.BlockSpec((t, hs * HEAD_PAD), lambda p, i: (0, p)),
                  pl.BlockSpec((t, hs * V_DIM), lambda p, i: (0, p))] + [hbm] * n,
        out_specs=[pl.BlockSpec((tq, hs * V_DIM), lambda p, i: (i, p)),
                   pl.BlockSpec((tq, hs * HEAD_PAD), lambda p, i: (i, p))] + [hbm] * n,
        out_shape=[jax.ShapeDtypeStruct((t, HEADS * V_DIM), F32), jax.ShapeDtypeStruct((t, HEADS * HEAD_PAD), F32)]
        + [jax.ShapeDtypeStruct((N_DEV,) + c.shape, c.dtype) for c in carried],
        scratch_shapes=_comm_sems() * n,
        compiler_params=_cp(("arbitrary", "arbitrary") if n else ("parallel", "parallel")),
    )(q, k, v, *carried)


def _sgu_fwd(zsg, gm, lng, lnb, wc_ref, bias, mixed_ref):
    uv, th = _gelu(zsg)
    u, v0 = uv[:, :SG_W], uv[:, SG_W:]
    vc = v0 - _gmean(v0, gm)
    r = lax.rsqrt(_gmean(vc * vc, gm) + EPS)
    vh = vc * r
    v = vh * lng + lnb
    lane = lax.broadcasted_iota(jnp.int32, (CHUNK, SG_W), 1)
    for c in range(zsg.shape[0] // CHUNK):
        rows = slice(c * CHUNK, (c + 1) * CHUNK)
        vb = v[rows].astype(BF16)
        mixed = bias
        for g in range(SG_W // GROUP):
            mixed = mixed + jnp.where(lane // GROUP == g, _mm(wc_ref[g], vb), 0.0)
        mixed_ref[rows, :] = mixed
    return u, v, vh, r, th


def _conv_fwd(zcv, halo, first, cw):
    gb, gc, hh = zcv[:, :CV_W], zcv[:, CV_W:2 * CV_W], zcv[:, 2 * CV_W:]
    y = gc * hh
    yh = jnp.where(first, 0.0, halo[:, CV_W:2 * CV_W] * halo[:, 2 * CV_W:])
    y1 = _shift_down(y, 1, yh)
    y2 = _shift_down(y, 2, yh)
    conv = y2 * cw[0:1, :] + y1 * cw[1:2, :] + y * cw[2:3, :]
    return gb * conv, conv, y, y1, y2


def _tril_bf16(w_ref, g):
    row = lax.broadcasted_iota(jnp.int32, (CHUNK, CHUNK), 0)
    col = lax.broadcasted_iota(jnp.int32, (CHUNK, CHUNK), 1)
    return jnp.where(col <= row, w_ref[g], 0.0).astype(BF16)


def mix_fwd(x, z, ya, gm, lng, lnb, wsp, bias, cw, gout, wout, gpost):
    t = x.shape[0]
    tm = min(TM_ROW, t)

    def body(x_ref, zcv_ref, halo_ref, zsg_ref, ya_ref, gm_ref, lng_ref, lnb_ref, wsp_ref, bias_ref, cw_ref,
             gout_ref, wout_ref, gpost_ref, x1_ref, wc_ref, mixed_ref):
        i = pl.program_id(0)
        for g in range(SG_W // GROUP):
            wc_ref[g] = _tril_bf16(wsp_ref, g)
        u, _, _, _, _ = _sgu_fwd(zsg_ref[...].astype(F32), gm_ref[...], lng_ref[...], lnb_ref[...], wc_ref, bias_ref[...],
                                 mixed_ref)
        yb = u * mixed_ref[...]
        yc, _, _, _, _ = _conv_fwd(zcv_ref[...].astype(F32), halo_ref[...].astype(F32), i == 0, cw_ref[...])
        gout = gout_ref[...]
        na, _, _ = _rms_fwd(ya_ref[...], gout[:, :512])
        nb, _, _ = _rms_fwd(yb, gout[:, 512:768])
        nc, _, _ = _rms_fwd(yc, gout[:, 768:])
        mix = jnp.concatenate([na, nb, nc], axis=1).astype(BF16)
        o, _, _ = _rms_fwd(_mm(mix, wout_ref[...]), gpost_ref[...])
        x1_ref[...] = x_ref[...] + o

    return pl.pallas_call(
        body, name="mix_fwd", grid=(t // tm,),
        in_specs=[pl.BlockSpec((tm, D), lambda i: (i, 0)),
                  pl.BlockSpec((tm, 768), lambda i: (i, 0)),
                  pl.BlockSpec((HALO, 768), lambda i: (jnp.maximum(i * (tm // HALO) - 1, 0), 0)),
                  pl.BlockSpec((tm, 512), lambda i: (i, 3)),
                  pl.BlockSpec((tm, 512), lambda i: (i, 0)),
                  _whole(), _whole(), _whole(), _whole(), _whole(), _whole(), _whole(), _whole(), _whole()],
        out_specs=pl.BlockSpec((tm, D), lambda i: (i, 0)),
        out_shape=jax.ShapeDtypeStruct((t, D), F32),
        scratch_shapes=[pltpu.VMEM((SG_W // GROUP, CHUNK, CHUNK), BF16), pltpu.VMEM((tm, SG_W), F32)],
        compiler_params=_cp(("arbitrary",)),
    )(x, z, z, z, ya, gm, lng, lnb, wsp, bias, cw, gout, wout, gpost)


def _sigmoid(a):
    return 1.0 / (1.0 + jnp.exp(-a))


FFN_SHARD = D_FF // N_DEV


def _load_ffn_weights(g_ref, wgu_ref, wd_ref, sems):
    copies = []
    for j in range(N_DEV):
        for p, (dst, base) in enumerate(((wgu_ref, 0), (wgu_ref, D_FF), (wd_ref, 0))):
            copies.append(pltpu.make_async_copy(g_ref.at[j, pl.ds(p * FFN_SHARD, FFN_SHARD)],
                                                dst.at[pl.ds(base + j * FFN_SHARD, FFN_SHARD)], sems.at[3 * j + p]))
    for cp in copies:
        cp.start()
    for cp in copies:
        cp.wait()


def _ffn_weight_scratch():
    return [pltpu.VMEM((2 * D_FF, D), BF16), pltpu.VMEM((D_FF, D), BF16), pltpu.SemaphoreType.DMA((3 * N_DEV,))]


def ffn_fwd(x1, gpre, gathered, gpost, carried=()):
    t = x1.shape[0]
    tm = min(TM_FFN, t)
    steps = t // tm
    n = len(carried)

    def body(*refs):
        x_ref, gpre_ref, g_ref, gpost_ref = refs[:4]
        x2_ref, f_ref = refs[4 + n:6 + n]
        wgu_ref, wd_ref, sems = refs[6 + 2 * n:9 + 2 * n]
        comm_sems = refs[9 + 2 * n:]
        stages = [_gather_steps(refs[4 + a], refs[6 + n + a], *comm_sems[3 * a:3 * a + 3]) for a in range(n)]
        if n:
            pl.when(pl.program_id(0) == 0)(_each(stages, 0))
            pl.when(pl.program_id(0) == (3 * steps) // 4)(_each(stages, 1))

        @pl.when(pl.program_id(0) == 0)
        def _():
            _load_ffn_weights(g_ref, wgu_ref, wd_ref, sems)

        x = x_ref[...]
        h, _, _ = _rms_fwd(x, gpre_ref[...])
        ab = _mm_nt(h.astype(BF16), wgu_ref[...])
        a, b = ab[:, :D_FF], ab[:, D_FF:]
        s = a * _sigmoid(a) * b
        f = _mm(s.astype(BF16), wd_ref[...])
        f_ref[...] = f
        x2_ref[...] = x + _rms_fwd(f, gpost_ref[...])[0]
        if n:
            pl.when(pl.program_id(0) == steps - 1)(_each(stages, 2))

    row = pl.BlockSpec((tm, D), lambda i: (i, 0))
    hbm = pl.BlockSpec(memory_space=pl.ANY)
    return pl.pallas_call(
        body, name=f"ffn_fwd_gather{n}" if n else "ffn_fwd", grid=(steps,),
        in_specs=[row, _whole(), hbm, _whole()] + [hbm] * n,
        out_specs=[row, row] + [hbm] * n,
        out_shape=[jax.ShapeDtypeStruct((t, D), F32), jax.ShapeDtypeStruct((t, D), F32)]
        + [jax.ShapeDtypeStruct((N_DEV,) + c.shape, c.dtype) for c in carried],
        scratch_shapes=_ffn_weight_scratch() + _comm_sems() * n,
        compiler_params=_cp(("arbitrary",)),
    )(x1, gpre, gathered, gpost, *carried)


def loss_head(y, target):
    t = y.shape[0]
    tm = min(TM, t)

    def body(y_ref, t_ref, loss_ref, dy_ref):
        @pl.when(pl.program_id(0) == 0)
        def _():
            loss_ref[...] = jnp.zeros_like(loss_ref)

        e = y_ref[...] - t_ref[...]
        dy_ref[...] = e * (1.0 / D)
        loss_ref[...] += jnp.sum(jnp.sum(e * e, axis=-1, keepdims=True), axis=0, keepdims=True)

    return pl.pallas_call(
        body, name="loss_head", grid=(t // tm,),
        in_specs=[pl.BlockSpec((tm, D), lambda i: (i, 0)), pl.BlockSpec((tm, D), lambda i: (i, 0))],
        out_specs=[pl.BlockSpec((1, 128), lambda i: (0, 0)), pl.BlockSpec((tm, D), lambda i: (i, 0))],
        out_shape=[jax.ShapeDtypeStruct((1, 128), F32), jax.ShapeDtypeStruct((t, D), F32)],
        compiler_params=_cp(("arbitrary",)),
    )(y, target)


def _acc(ref, first, val):
    @pl.when(first)
    def _():
        ref[...] = val

    @pl.when(jnp.logical_not(first))
    def _():
        ref[...] += val


def ffn_bwd(x1, f, dx2, gpre, gathered, gpost, carried=()):
    t = x1.shape[0]
    tm = min(TM_FFN, t)
    steps = t // tm
    n = len(carried)

    def body(*refs):
        x_ref, f_ref, dx2_ref, gpre_ref, g_ref, gpost_ref = refs[:6]
        dx1_ref, h_ref, dab_ref, s_ref, df_ref, dgpre_ref, dgpost_ref = refs[6 + n:13 + n]
        ab_ref, ds_ref, wgu_ref, wd_ref, sems = refs[13 + 2 * n:18 + 2 * n]
        comm_sems = refs[18 + 2 * n:]
        stages = [_exchange_steps(refs[6 + a], refs[13 + n + a], carried[a][1], *comm_sems[3 * a:3 * a + 3])
                  for a in range(n)]
        first = pl.program_id(0) == 0
        if n:
            pl.when(first)(_each(stages, 0))

        @pl.when(first)
        def _():
            _load_ffn_weights(g_ref, wgu_ref, wd_ref, sems)

        dx2 = dx2_ref[...]
        gpre, gpost = gpre_ref[...], gpost_ref[...]
        h, xh, rx = _rms_fwd(x_ref[...], gpre)
        h_ref[...] = h.astype(BF16)
        ab_ref[...] = _mm_nt(h_ref[...], wgu_ref[...])
        for c in range(0, D_FF, FFN_SLAB):
            a, b = ab_ref[:, c:c + FFN_SLAB], ab_ref[:, D_FF + c:D_FF + c + FFN_SLAB]
            s_ref[:, c:c + FFN_SLAB] = (a * _sigmoid(a) * b).astype(BF16)
        _, fh, rf = _rms_fwd(f_ref[...], gpost)
        df, dgpost = _rms_bwd(fh, rf, gpost, dx2)
        df_ref[...] = df.astype(BF16)
        ds_ref[...] = _mm_nt(df_ref[...], wd_ref[...])
        for c in range(0, D_FF, FFN_SLAB):
            a, b = ab_ref[:, c:c + FFN_SLAB], ab_ref[:, D_FF + c:D_FF + c + FFN_SLAB]
            ds = ds_ref[:, c:c + FFN_SLAB]
            sg = _sigmoid(a)
            dab_ref[:, c:c + FFN_SLAB] = (ds * b * (sg * (1.0 + a * (1.0 - sg)))).astype(BF16)
            dab_ref[:, D_FF + c:D_FF + c + FFN_SLAB] = (ds * (a * sg)).astype(BF16)
        dx, dgpre = _rms_bwd(xh, rx, gpre, _mm(dab_ref[...], wgu_ref[...]))
        dx1_ref[...] = dx2 + dx
        _acc(dgpre_ref, first, dgpre)
        _acc(dgpost_ref, first, dgpost)
        if n:
            pl.when(pl.program_id(0) == steps - 1)(_each(stages, 1))

    row = lambda w: pl.BlockSpec((tm, w), lambda i: (i, 0))
    vec = pl.BlockSpec((1, D), lambda i: (0, 0))
    hbm = pl.BlockSpec(memory_space=pl.ANY)
    return pl.pallas_call(
        body, name=f"ffn_bwd_exchange{n}" if n else "ffn_bwd", grid=(steps,),
        in_specs=[row(D), row(D), row(D), _whole(), hbm, _whole()] + [hbm] * n,
        out_specs=[row(D), row(D), row(2 * D_FF), row(D_FF), row(D), vec, vec] + [hbm] * n,
        out_shape=[jax.ShapeDtypeStruct((t, D), F32), jax.ShapeDtypeStruct((t, D), BF16),
                   jax.ShapeDtypeStruct((t, 2 * D_FF), BF16), jax.ShapeDtypeStruct((t, D_FF), BF16),
                   jax.ShapeDtypeStruct((t, D), BF16), jax.ShapeDtypeStruct((1, D), F32),
                   jax.ShapeDtypeStruct((1, D), F32)]
        + [jax.ShapeDtypeStruct(src.shape if scatter else (N_DEV,) + src.shape, src.dtype) for src, scatter in carried],
        scratch_shapes=[pltpu.VMEM((tm, 2 * D_FF), F32), pltpu.VMEM((tm, D_FF), F32)] + _ffn_weight_scratch()
        + _comm_sems() * n,
        compiler_params=_cp(("arbitrary",)),
    )(x1, f, dx2, gpre, gathered, gpost, *[src for src, _ in carried])


FFN_TILE_SHARDS = 4


def atb_ffn_chunks(a, b, first_piece, chunks=None):
    t, k = a.shape
    tt = min(TT, t)
    tk = FFN_TILE_SHARDS * FFN_SHARD
    steps = t // tt
    per_piece = N_DEV // FFN_TILE_SHARDS

    def body(*refs):
        a_ref, b_ref, o_ref, acc_ref = refs[0], refs[1], refs[-2], refs[-1]
        i = pl.program_id(1)
        _acc(acc_ref, i == 0, _mm_tn(a_ref[...], b_ref[...]))

        @pl.when(i == steps - 1)
        def _():
            for d in range(FFN_TILE_SHARDS):
                o_ref[d, 0] = acc_ref[d * FFN_SHARD:(d + 1) * FFN_SHARD, :].astype(BF16)

    hbm = pl.BlockSpec(memory_space=pl.ANY)
    return pl.pallas_call(
        body, name="atb_ffn_chunks", grid=(k // tk, steps),
        in_specs=[pl.BlockSpec((tt, tk), lambda j, i: (i, j)), pl.BlockSpec((tt, D), lambda j, i: (i, 0))]
        + ([] if chunks is None else [hbm]),
        out_specs=pl.BlockSpec((FFN_TILE_SHARDS, 1, FFN_SHARD, D),
                               lambda j, i: (j % per_piece, first_piece + j // per_piece, 0, 0)),
        out_shape=jax.ShapeDtypeStruct((N_DEV, len(FFN_PIECES), FFN_SHARD, D), BF16),
        input_output_aliases={} if chunks is None else {2: 0},
        scratch_shapes=[pltpu.VMEM((tk, D), F32)],
        compiler_params=_cp(("parallel", "arbitrary")),
    )(a, b, *([] if chunks is None else [chunks]))


def mix_bwd(dx1, z, ya, gm, lng, lnb, wsp, wspt, bias, cw, gout, wout, gpost):
    t = dx1.shape[0]
    tm = min(TM, t)
    steps = t // tm
    ng = SG_W // GROUP

    def body(dx1_ref, zcv_ref, halo_ref, zsg_ref, ya_ref, gm_ref, lng_ref, lnb_ref, wsp_ref, wspt_ref, bias_ref,
             cw_ref, gout_ref, wout_ref, gpost_ref,
             dya_ref, dyc_ref, dzsg_ref, dwout_ref, dgpost_ref, dgout_ref, dlng_ref, dlnb_ref, dwsp_ref,
             dbias_ref, wc_ref, wct_ref, mixed_ref, dv_ref, dwout_acc):
        i = pl.program_id(0)
        first = i == 0
        gm = gm_ref[...]
        for g in range(ng):
            wc_ref[g] = _tril_bf16(wsp_ref, g)
            wct_ref[g] = jnp.where(
                lax.broadcasted_iota(jnp.int32, (CHUNK, CHUNK), 0) <= lax.broadcasted_iota(jnp.int32, (CHUNK, CHUNK), 1),
                wspt_ref[g], 0.0).astype(BF16)
        zsg = zsg_ref[...].astype(F32)
        lng = lng_ref[...]
        u, v, vh, r, th = _sgu_fwd(zsg, gm, lng, lnb_ref[...], wc_ref, bias_ref[...], mixed_ref)
        mixed = mixed_ref[...]
        yb = u * mixed
        yc, _, _, _, _ = _conv_fwd(zcv_ref[...].astype(F32), halo_ref[...].astype(F32), first, cw_ref[...])
        gout, gpost = gout_ref[...], gpost_ref[...]
        ga, gb_, gc_ = gout[:, :512], gout[:, 512:768], gout[:, 768:]
        na, yah, ra = _rms_fwd(ya_ref[...], ga)
        nb, ybh, rb = _rms_fwd(yb, gb_)
        nc, ych, rc = _rms_fwd(yc, gc_)
        mix = jnp.concatenate([na, nb, nc], axis=1).astype(BF16)
        _, oh, ro = _rms_fwd(_mm(mix, wout_ref[...]), gpost)
        do, dgpost = _rms_bwd(oh, ro, gpost, dx1_ref[...])
        dob = do.astype(BF16)
        dmix = _mm_nt(dob, wout_ref[...])
        dya, dga = _rms_bwd(yah, ra, ga, dmix[:, :512])
        dyb, dgb = _rms_bwd(ybh, rb, gb_, dmix[:, 512:768])
        dyc, dgc = _rms_bwd(ych, rc, gc_, dmix[:, 768:])
        dya_ref[...] = dya
        dyc_ref[...] = dyc
        _acc(dwout_acc, first, _mm_tn(mix, dob))

        @pl.when(i == steps - 1)
        def _():
            dwout_ref[...] = dwout_acc[...].astype(BF16)

        _acc(dgpost_ref, first, dgpost)
        _acc(dgout_ref, first, jnp.concatenate([dga, dgb, dgc], axis=1))
        du = dyb * mixed
        dmixed = dyb * u
        lane = lax.broadcasted_iota(jnp.int32, (CHUNK, SG_W), 1)
        row = lax.broadcasted_iota(jnp.int32, (CHUNK, CHUNK), 0)
        col = lax.broadcasted_iota(jnp.int32, (CHUNK, CHUNK), 1)
        dbias = jnp.zeros((CHUNK, SG_W), F32)
        dw = [jnp.zeros((CHUNK, CHUNK), F32) for _ in range(ng)]
        for c in range(tm // CHUNK):
            rows = slice(c * CHUNK, (c + 1) * CHUNK)
            dm = dmixed[rows]
            dbias = dbias + dm
            dmb = dm.astype(BF16)
            vb = v[rows].astype(BF16)
            dvc = jnp.zeros((CHUNK, SG_W), F32)
            for g in range(ng):
                in_g = lane // GROUP == g
                dvc = dvc + jnp.where(in_g, _mm(wct_ref[g], dmb), 0.0)
                dw[g] = dw[g] + _mm_nt(jnp.where(in_g, dmb, jnp.zeros_like(dmb)), vb)
            dv_ref[rows, :] = dvc
        for g in range(ng):
            dwg = jnp.where(col <= row, dw[g], 0.0)

            @pl.when(first)
            def _():
                dwsp_ref[g] = dwg

            @pl.when(jnp.logical_not(first))
            def _():
                dwsp_ref[g] += dwg
        _acc(dbias_ref, first, _gmean(dbias, gm) * GROUP)
        dv = dv_ref[...]
        _acc(dlng_ref, first, jnp.sum(dv * vh, axis=0, keepdims=True))
        _acc(dlnb_ref, first, jnp.sum(dv, axis=0, keepdims=True))
        dvh = dv * lng
        dv0 = r * (dvh - _gmean(dvh, gm) - vh * _gmean(dvh * vh, gm))
        dzsg_ref[...] = (jnp.concatenate([du, dv0], axis=1) * _gelu_grad(zsg, th)).astype(BF16)

    row_ = lambda w: pl.BlockSpec((tm, w), lambda i: (i, 0))
    vec = lambda w: pl.BlockSpec((1, w), lambda i: (0, 0))
    return pl.pallas_call(
        body, name="mix_bwd", grid=(steps,),
        in_specs=[row_(D),
                  pl.BlockSpec((tm, 768), lambda i: (i, 0)),
                  pl.BlockSpec((HALO, 768), lambda i: (jnp.maximum(i * (tm // HALO) - 1, 0), 0)),
                  pl.BlockSpec((tm, 512), lambda i: (i, 3)),
                  row_(512),
                  _whole(), _whole(), _whole(), _whole(), _whole(), _whole(), _whole(), _whole(), _whole(), _whole()],
        out_specs=[row_(512), row_(CV_W), row_(512), pl.BlockSpec((D, D), lambda i: (0, 0)), vec(D), vec(D), vec(SG_W),
                   vec(SG_W), pl.BlockSpec((ng, CHUNK, CHUNK), lambda i: (0, 0, 0)),
                   pl.BlockSpec((CHUNK, SG_W), lambda i: (0, 0))],
        out_shape=[jax.ShapeDtypeStruct((t, 512), F32), jax.ShapeDtypeStruct((t, CV_W), F32),
                   jax.ShapeDtypeStruct((t, 512), BF16), jax.ShapeDtypeStruct((D, D), BF16),
                   jax.ShapeDtypeStruct((1, D), F32),
                   jax.ShapeDtypeStruct((1, D), F32), jax.ShapeDtypeStruct((1, SG_W), F32),
                   jax.ShapeDtypeStruct((1, SG_W), F32), jax.ShapeDtypeStruct((ng, CHUNK, CHUNK), F32),
                   jax.ShapeDtypeStruct((CHUNK, SG_W), F32)],
        scratch_shapes=[pltpu.VMEM((ng, CHUNK, CHUNK), BF16), pltpu.VMEM((ng, CHUNK, CHUNK), BF16),
                        pltpu.VMEM((tm, SG_W), F32), pltpu.VMEM((tm, SG_W), F32), pltpu.VMEM((D, D), F32)],
        compiler_params=_cp(("arbitrary",)),
    )(dx1, z, z, z, ya, gm, lng, lnb, wsp, wspt, bias, cw, gout, wout, gpost)


def conv_bwd(dyc, z, cw):
    t = dyc.shape[0]
    tm = min(TM_ROW, t)
    hb = tm // 8
    last_blk = t // 8 - 1

    def body(dyc_ref, dyct_ref, zcv_ref, head_ref, tail_ref, cw_ref, dz_ref, dcw_ref):
        i = pl.program_id(0)
        first = i == 0
        last = i == pl.num_programs(0) - 1
        cw = cw_ref[...]
        zcv = zcv_ref[...].astype(F32)
        gb, gc, hh = zcv[:, :CV_W], zcv[:, CV_W:2 * CV_W], zcv[:, 2 * CV_W:]
        _, conv, y, y1, y2 = _conv_fwd(zcv, head_ref[...].astype(F32), first, cw)
        dyc = dyc_ref[...]
        dconv = dyc * gb
        tail = jnp.where(last, 0.0, dyct_ref[...] * tail_ref[:8, :CV_W].astype(F32))
        d1 = _shift_up(dconv, 1, tail)
        d2 = _shift_up(dconv, 2, tail)
        dy = dconv * cw[2:3, :] + d1 * cw[1:2, :] + d2 * cw[0:1, :]
        dz_ref[...] = jnp.concatenate([dyc * conv, dy * hh, dy * gc], axis=1).astype(BF16)
        tap = lax.broadcasted_iota(jnp.int32, (8, CV_W), 0)
        dcw = jnp.where(tap == 0, jnp.sum(dconv * y2, axis=0, keepdims=True),
                        jnp.where(tap == 1, jnp.sum(dconv * y1, axis=0, keepdims=True),
                                  jnp.where(tap == 2, jnp.sum(dconv * y, axis=0, keepdims=True), 0.0)))
        _acc(dcw_ref, first, dcw)

    return pl.pallas_call(
        body, name="conv_bwd", grid=(t // tm,),
        in_specs=[pl.BlockSpec((tm, CV_W), lambda i: (i, 0)),
                  pl.BlockSpec((8, CV_W), lambda i: (jnp.minimum((i + 1) * hb, last_blk), 0)),
                  pl.BlockSpec((tm, 768), lambda i: (i, 0)),
                  pl.BlockSpec((HALO, 768), lambda i: (jnp.maximum(i * (tm // HALO) - 1, 0), 0)),
                  pl.BlockSpec((HALO, 768), lambda i: (jnp.minimum((i + 1) * (tm // HALO), t // HALO - 1), 0)),
                  _whole()],
        out_specs=[pl.BlockSpec((tm, 768), lambda i: (i, 0)), pl.BlockSpec((8, CV_W), lambda i: (0, 0))],
        out_shape=[jax.ShapeDtypeStruct((t, 768), BF16), jax.ShapeDtypeStruct((8, CV_W), F32)],
        compiler_params=_cp(("arbitrary",)),
    )(dyc, dyc, z, z, z, cw)


def attn_bwd(q, k, v, o, lse, do, carried=()):
    t = q.shape[0]
    tq = min(TQ, t)
    nq = t // tq
    last_pair = HEADS // 2 - 1
    n = len(carried)

    def body(*refs):
        j = pl.program_id(1)
        q_ref, k_ref, v_ref, o_ref, lse_ref, do_ref = refs[:6]
        dq_out_ref, dk_ref, dv_ref = refs[6 + n:9 + n]
        dq_ref = refs[9 + 2 * n]
        sems = refs[10 + 2 * n:]
        stages = [_exchange_steps(refs[6 + a], refs[9 + n + a], carried[a][1], *sems[3 * a:3 * a + 3]) for a in range(n)]
        if n:
            pl.when((pl.program_id(0) == 0) & (j == 0))(_each(stages, 0))

        @pl.when(j == 0)
        def _():
            dq_ref[...] = jnp.zeros_like(dq_ref)

        row = lax.broadcasted_iota(jnp.int32, (tq, tq), 0)
        col = lax.broadcasted_iota(jnp.int32, (tq, tq), 1)
        vlane = lax.broadcasted_iota(jnp.int32, (tq, 2 * V_DIM), 1)
        head_lanes = [slice(h * HEAD_PAD, (h + 1) * HEAD_PAD) for h in range(2)]

        def step(i, carry, masked):
            start = pl.multiple_of(i * tq, tq)
            do_blk = do_ref[pl.ds(start, tq), :]
            o_blk = o_ref[pl.ds(start, tq), :]
            vb = v_ref[...]
            dks, dv_acc = [], carry[2]
            for h in range(2):
                lanes = head_lanes[h]
                qb = q_ref[pl.ds(start, tq), lanes]
                kb = k_ref[:, lanes]
                dob = jnp.where((vlane // V_DIM) == h, do_blk, 0.0)
                delta = jnp.sum(dob * o_blk, axis=-1, keepdims=True)
                lse2 = lse_ref[pl.ds(start, tq), lanes][:, 0:1] * LOG2E
                s = _mm_nt(qb, kb)
                if masked:
                    s = jnp.where(col <= row, s, NEG)
                p = jnp.exp2(s * SCALE_LOG2E - lse2)
                dob16 = dob.astype(BF16)
                dp = _mm_nt(dob16, vb)
                ds = (p * (dp - delta) * SCALE).astype(BF16)
                dv_acc = dv_acc + _mm_tn(p.astype(BF16), dob16)
                dks.append(carry[h] + _mm_tn(ds, qb))
                dq_ref[pl.ds(start, tq), lanes] += _mm(ds, kb)
            return dks[0], dks[1], dv_acc

        zero = jnp.zeros((tq, HEAD_PAD), F32)
        carry = step(j, (zero, zero, jnp.zeros((tq, 2 * V_DIM), F32)), True)
        rest = nq - 1 - j
        carry = lax.fori_loop(0, rest // 2, lambda u, c: step(j + 2 + 2 * u, step(j + 1 + 2 * u, c, False), False), carry)
        dk0, dk1, dv_acc = lax.fori_loop(0, rest % 2, lambda _, c: step(nq - 1, c, False), carry)
        dk_ref[:, head_lanes[0]] = dk0.astype(BF16)
        dk_ref[:, head_lanes[1]] = dk1.astype(BF16)
        dv_ref[...] = dv_acc.astype(BF16)

        @pl.when(j == nq - 1)
        def _():
            dq_out_ref[...] = dq_ref[...].astype(BF16)

        if n:
            pl.when((pl.program_id(0) == last_pair) & (j == nq - 1))(_each(stages, 1))

    hbm = pl.BlockSpec(memory_space=pl.ANY)
    return pl.pallas_call(
        body, name=f"attn_bwd_exchange{n}" if n else "attn_bwd", grid=(HEADS // 2, nq),
        in_specs=[pl.BlockSpec((t, 2 * HEAD_PAD), lambda p, j: (0, p)),
                  pl.BlockSpec((tq, 2 * HEAD_PAD), lambda p, j: (j, p)),
                  pl.BlockSpec((tq, 2 * V_DIM), lambda p, j: (j, p)),
                  pl.BlockSpec((t, 2 * V_DIM), lambda p, j: (0, p)),
                  pl.BlockSpec((t, 2 * HEAD_PAD), lambda p, j: (0, p)),
                  pl.BlockSpec((t, 2 * V_DIM), lambda p, j: (0, p))] + [hbm] * n,
        out_specs=[pl.BlockSpec((t, 2 * HEAD_PAD), lambda p, j: (0, p)),
                   pl.BlockSpec((tq, 2 * HEAD_PAD), lambda p, j: (j, p)),
                   pl.BlockSpec((tq, 2 * V_DIM), lambda p, j: (j, p))] + [hbm] * n,
        out_shape=[jax.ShapeDtypeStruct((t, HEADS * HEAD_PAD), BF16), jax.ShapeDtypeStruct((t, HEADS * HEAD_PAD), BF16),
                   jax.ShapeDtypeStruct((t, HEADS * V_DIM), BF16)]
        + [jax.ShapeDtypeStruct(src.shape if scatter else (N_DEV,) + src.shape, src.dtype) for src, scatter in carried],
        scratch_shapes=[pltpu.VMEM((t, 2 * HEAD_PAD), F32)] + _comm_sems() * n,
        compiler_params=_cp(("arbitrary", "arbitrary") if n else ("parallel", "arbitrary")),
    )(q, k, v, o, lse, do, *[src for src, _ in carried])


def mla_proj_bwd(dq, dk, dv, z, ca, sb, sc, gq, gkv, wuq, wukv):
    t = z.shape[0]
    tm = min(TM_ROW, t)
    steps = t // tm

    def body(dq_ref, dk_ref, dv_ref, z_ref, ca_ref, sb_ref, sc_ref, gq_ref, gkv_ref, wuq_ref, wukv_ref,
             dz_ref, dwuq_ref, dwukv_ref, dgq_ref, dgkv_ref, dqp_ref, dkvp_ref, uq_acc, ukv_acc):
        first = pl.program_id(0) == 0
        z = z_ref[...].astype(F32)
        ca, sb, sc = ca_ref[...], sb_ref[...], sc_ref[...]
        gq, gkv = gq_ref[...], gkv_ref[...]
        cq, cqh, rq = _rms_fwd(z[:, :Q_RANK], gq)
        ckv, ckvh, rkv = _rms_fwd(z[:, Q_RANK:Q_RANK + KV_RANK], gkv)
        lane = lax.broadcasted_iota(jnp.int32, (tm, HEAD_PAD), 1)
        dkr = jnp.zeros((tm, HEAD_PAD), F32)
        for h in range(HEADS):
            lanes = slice(h * HEAD_PAD, (h + 1) * HEAD_PAD)
            dqp_ref[:, lanes] = _rope_t(dq_ref[:, lanes].astype(F32), ca, sb, sc).astype(BF16)
            dkh = dk_ref[:, lanes].astype(F32)
            dkr = dkr + dkh
            dkvp_ref[:, lanes] = jnp.where(lane < NOPE, dkh, 0.0).astype(BF16)
        dkvp_ref[:, HEADS * HEAD_PAD:] = dv_ref[...].astype(BF16)
        dkr = pltpu.roll(_rope_t(jnp.where(lane >= NOPE, dkr, 0.0), ca, sb, sc), HEAD_PAD - NOPE, 1)
        dkr = jnp.where(lane < ROPE, dkr, 0.0)
        dcq = _mm(dqp_ref[...], wuq_ref[...])
        dckv = _mm(dkvp_ref[...], wukv_ref[...])
        dzq, dgq = _rms_bwd(cqh, rq, gq, dcq)
        dzkv, dgkv = _rms_bwd(ckvh, rkv, gkv, dckv)
        dz_ref[...] = jnp.concatenate([dzq, dzkv, dkr], axis=1).astype(BF16)
        _acc(uq_acc, first, _mm_tn(cq.astype(BF16), dqp_ref[...]))
        _acc(ukv_acc, first, _mm_tn(ckv.astype(BF16), dkvp_ref[...]))
        _acc(dgq_ref, first, dgq)
        _acc(dgkv_ref, first, dgkv)

        @pl.when(pl.program_id(0) == steps - 1)
        def _():
            dwuq_ref[...] = uq_acc[...].astype(BF16)
            dwukv_ref[...] = ukv_acc[...].astype(BF16)

    row = lambda w: pl.BlockSpec((tm, w), lambda i: (i, 0))
    vec = lambda w: pl.BlockSpec((1, w), lambda i: (0, 0))
    whole = lambda r, c: pl.BlockSpec((r, c), lambda i: (0, 0))
    nq, nkv = HEADS * HEAD_PAD, HEADS * (HEAD_PAD + V_DIM)
    return pl.pallas_call(
        body, name="mla_proj_bwd", grid=(steps,),
        in_specs=[row(1024), row(1024), row(512), pl.BlockSpec((tm, 768), lambda i: (i, 1)),
                  row(HEAD_PAD), row(HEAD_PAD), row(HEAD_PAD), _whole(), _whole(), _whole(), _whole()],
        out_specs=[row(768), whole(Q_RANK, nq), whole(KV_RANK, nkv), vec(Q_RANK), vec(KV_RANK)],
        out_shape=[jax.ShapeDtypeStruct((t, 768), BF16), jax.ShapeDtypeStruct((Q_RANK, nq), BF16),
                   jax.ShapeDtypeStruct((KV_RANK, nkv), BF16), jax.ShapeDtypeStruct((1, Q_RANK), F32),
                   jax.ShapeDtypeStruct((1, KV_RANK), F32)],
        scratch_shapes=[pltpu.VMEM((tm, nq), BF16), pltpu.VMEM((tm, nkv), BF16), pltpu.VMEM((Q_RANK, nq), F32),
                        pltpu.VMEM((KV_RANK, nkv), F32)],
        compiler_params=_cp(("arbitrary",)),
    )(dq, dk, dv, z, ca, sb, sc, gq, gkv, wuq, wukv)


def pre_in_bwd(x, dx1, dzcv, dzmla, dzsg, g, w):
    t = x.shape[0]
    tm = min(TM, t)
    steps = t // tm

    def body(x_ref, dx1_ref, dzcv_ref, dzmla_ref, dzsg_ref, g_ref, w_ref, dx_ref, dw_ref, dg_ref, acc_ref):
        first = pl.program_id(0) == 0
        g = g_ref[...]
        h, xh, r = _rms_fwd(x_ref[...], g)
        dz = jnp.concatenate([dzcv_ref[...], dzmla_ref[...], dzsg_ref[...]], axis=1)
        dx, dg = _rms_bwd(xh, r, g, _mm(dz, w_ref[...]))
        dx_ref[...] = dx1_ref[...] + dx
        _acc(acc_ref, first, _mm_tn(dz, h.astype(BF16)))
        _acc(dg_ref, first, dg)

        @pl.when(pl.program_id(0) == steps - 1)
        def _():
            dw_ref[...] = acc_ref[...].astype(BF16)

    row = lambda w_: pl.BlockSpec((tm, w_), lambda i: (i, 0))
    return pl.pallas_call(
        body, name="pre_in_bwd", grid=(steps,),
        in_specs=[row(D), row(D), row(768), row(768), row(512), _whole(), _whole()],
        out_specs=[row(D), pl.BlockSpec((Z_W, D), lambda i: (0, 0)), pl.BlockSpec((1, D), lambda i: (0, 0))],
        out_shape=[jax.ShapeDtypeStruct((t, D), F32), jax.ShapeDtypeStruct((Z_W, D), BF16),
                   jax.ShapeDtypeStruct((1, D), F32)],
        scratch_shapes=[pltpu.VMEM((Z_W, D), F32)],
        compiler_params=_cp(("arbitrary",)),
    )(x, dx1, dzcv, dzmla, dzsg, g, w)


MESH = pl.DeviceIdType.MESH


def _place():
    return lax.axis_index("x"), lax.axis_index("y"), lax.axis_index("c")


def _comm_sems():
    return [pltpu.SemaphoreType.DMA((7,)), pltpu.SemaphoreType.DMA((7,)), pltpu.SemaphoreType.DMA]


def _gather_steps(x_ref, out_ref, send_sems, recv_sems, local_sem):
    x, y, c = _place()
    me, sibling = (x, y, c), (x, y, 1 - c)
    chips = [(1 - x, y), (x, 1 - y), (1 - x, 1 - y)]

    def slot(px, py, pc):
        return out_ref.at[4 * px + 2 * py + pc]

    def copy(k, blk, to, src=None):
        return pltpu.make_async_remote_copy(
            src_ref=slot(*blk) if src is None else src, dst_ref=slot(*blk),
            send_sem=send_sems.at[k], recv_sem=recv_sems.at[k], device_id=to, device_id_type=MESH)

    mine = pltpu.make_async_copy(x_ref, slot(*me), local_sem)
    first = [copy(0, me, sibling, src=x_ref)] + [copy(1 + j, me, (*chip, c), src=x_ref) for j, chip in enumerate(chips)]
    passed = [copy(4 + j, (*chip, c), sibling) for j, chip in enumerate(chips)]

    def start():
        mine.start()
        for cp in first:
            cp.start()

    def forward():
        for j, chip in enumerate(chips):
            copy(1 + j, (*chip, c), me).wait_recv()
            passed[j].start()

    def finish():
        copy(0, sibling, me).wait_recv()
        for j, chip in enumerate(chips):
            copy(4 + j, (*chip, 1 - c), me).wait_recv()
        for cp in first + passed:
            cp.wait_send()
        mine.wait()

    return start, forward, finish


def _exchange_steps(src_ref, out_ref, scatter, send_sems, recv_sems, local_sem):
    x, y, c = _place()
    me = 4 * x + 2 * y + c
    own = pltpu.make_async_copy(src_ref.at[me] if scatter else src_ref, out_ref.at[me], local_sem)
    copies = []
    for k in range(1, N_DEV):
        px = 1 - x if k & 4 else x
        py = 1 - y if k & 2 else y
        pc = 1 - c if k & 1 else c
        copies.append(pltpu.make_async_remote_copy(
            src_ref=src_ref.at[4 * px + 2 * py + pc] if scatter else src_ref, dst_ref=out_ref.at[me],
            send_sem=send_sems.at[k - 1], recv_sem=recv_sems.at[k - 1], device_id=(px, py, pc), device_id_type=MESH))

    def start():
        own.start()
        for cp in copies:
            cp.start()

    def finish():
        for cp in copies:
            cp.wait_recv()
        for cp in copies:
            cp.wait_send()
        own.wait()

    return start, finish


def all_gather(block):
    def body(x_ref, out_ref, *sems):
        for stage in _gather_steps(x_ref, out_ref, *sems):
            stage()

    return pl.pallas_call(
        body, name="all_gather",
        in_specs=[pl.BlockSpec(memory_space=pl.ANY)],
        out_specs=pl.BlockSpec(memory_space=pl.ANY),
        out_shape=jax.ShapeDtypeStruct((N_DEV,) + block.shape, block.dtype),
        scratch_shapes=_comm_sems(),
    )(block)


def _row_tile(r, cap):
    return max(d for d in range(16, cap + 1, 16) if r % d == 0)


def sum_adamw(parts, w, m, v, cap, carried=()):
    nl, r, c = w.shape
    tr = _row_tile(r, cap)
    steps = r // tr
    n = len(carried)
    c1 = 1.0 / (1.0 - ADAM_B1 ** ADAM_STEP)
    c2 = 1.0 / (1.0 - ADAM_B2 ** ADAM_STEP)

    def body(*refs):
        p_refs = refs[:nl]
        w_ref, m_ref, v_ref = refs[nl:nl + 3]
        g_ref, d_ref, nm_ref, nv_ref = refs[nl + 3 + n:nl + 7 + n]
        sems = refs[nl + 7 + 2 * n:]
        stages = [_exchange_steps(refs[nl + 3 + a], refs[nl + 7 + n + a], carried[a][1], *sems[3 * a:3 * a + 3])
                  for a in range(n)]
        layer, i = pl.program_id(0), pl.program_id(1)
        if n:
            pl.when((layer == 0) & (i == 0))(_each(stages, 0))

        def update(p_ref):
            g = p_ref[0].astype(F32)
            for k in range(1, N_DEV):
                g = g + p_ref[k].astype(F32)
            m_new = ADAM_B1 * m_ref[...] + (1.0 - ADAM_B1) * g
            v_new = ADAM_B2 * v_ref[...] + (1.0 - ADAM_B2) * (g * g)
            g_ref[...] = g
            nm_ref[...] = m_new
            nv_ref[...] = v_new
            d_ref[...] = -ADAM_LR * ((m_new * c1) / (jnp.sqrt(v_new * c2) + ADAM_EPS) + ADAM_WD * w_ref[...])

        for k in range(nl):
            pl.when(layer == k)(functools.partial(update, p_refs[k]))
        if n:
            pl.when((layer == nl - 1) & (i == steps - 1))(_each(stages, 1))

    def parts_spec(k):
        return pl.BlockSpec((N_DEV, tr, c), lambda l, i: (0, jnp.where(l == k, i, jnp.where(l < k, 0, steps - 1)), 0))

    blk = pl.BlockSpec((None, tr, c), lambda l, i: (l, i, 0))
    out = jax.ShapeDtypeStruct((nl, r, c), F32)
    hbm = pl.BlockSpec(memory_space=pl.ANY)
    return pl.pallas_call(
        body, name=f"sum_adamw_exchange{n}" if n else "sum_adamw", grid=(nl, steps),
        in_specs=[parts_spec(k) for k in range(nl)] + [blk, blk, blk] + [hbm] * n,
        out_specs=[blk, blk, blk, blk] + [hbm] * n,
        out_shape=[out, out, out, out]
        + [jax.ShapeDtypeStruct(src.shape if scatter else (N_DEV,) + src.shape, src.dtype) for src, scatter in carried],
        scratch_shapes=_comm_sems() * n,
        compiler_params=_cp(("arbitrary", "arbitrary")),
    )(*parts, w, m, v, *[src for src, _ in carried])


PACK_W = 1024
MIX_PIECES = (("w_out", D // N_DEV, D, False), ("w_uq", HEADS * (NOPE + ROPE) // N_DEV, Q_RANK, True),
              ("w_ukv", HEADS * (NOPE + V_DIM) // N_DEV, KV_RANK, True), ("conv", 16, PACK_W, False),
              ("w_in", IN_W // N_DEV, D, True))
FFN_PIECES = (("w_gate", D_FF // N_DEV, D, True), ("w_up", D_FF // N_DEV, D, True), ("w_down", D_FF // N_DEV, D, False))
def _packed_rows(rows, cols):
    return rows * cols // PACK_W


OFFSET = {}
for _pieces in (MIX_PIECES, FFN_PIECES):
    _off = 0
    for _name, _rows, _cols, _ in _pieces:
        assert _rows * _cols % PACK_W == 0
        OFFSET[_name] = _off
        _off += _packed_rows(_rows, _cols) + -_packed_rows(_rows, _cols) % 16
assert all(o % 16 == 0 for o in OFFSET.values())
assert [OFFSET[n] for n in ("w_gate", "w_up", "w_down")] == [0, FFN_SHARD, 2 * FFN_SHARD]
CONV_BITS = 3 * (CV_W // N_DEV) * 2


def _to_pack(shards, dtype, pieces, conv=None):
    nl = shards["w_in"].shape[0]
    parts = []
    for name, rows, cols, transposed in pieces:
        if name == "conv":
            if conv is None:
                a = jnp.zeros((nl, rows, PACK_W), dtype)
            else:
                bits = lax.bitcast_convert_type(conv.astype(F32), BF16).reshape(nl, CONV_BITS)
                a = jnp.pad(bits, ((0, 0), (0, rows * PACK_W - CONV_BITS))).reshape(nl, rows, PACK_W)
        else:
            a = shards[name].astype(dtype)
            a = (jnp.swapaxes(a, 1, 2) if transposed else a).reshape(nl, _packed_rows(rows, cols), PACK_W)
            a = jnp.pad(a, ((0, 0), (0, -a.shape[1] % 16), (0, 0)))
        parts.append(a)
    return jnp.concatenate(parts, axis=1)


def _from_pack(pack, pieces):
    out = {}
    for name, rows, cols, transposed in pieces:
        if name != "conv":
            a = pack[:, OFFSET[name]:OFFSET[name] + _packed_rows(rows, cols)].reshape(pack.shape[0], rows, cols)
            out[name] = jnp.swapaxes(a, 1, 2) if transposed else a
    return out


def _mix_weights(g):
    def rows(name):
        _, n, cols, _ = next(p for p in MIX_PIECES if p[0] == name)
        return g[:, OFFSET[name]:OFFSET[name] + _packed_rows(n, cols)].reshape(N_DEV, n, cols)

    w_in_t = rows("w_in").reshape(IN_W, D)
    w_in_p = jnp.concatenate([w_in_t[1184:], w_in_t[:672], jnp.zeros((96, D), BF16), w_in_t[672:1184]], axis=0)
    w_uq_p = jnp.pad(rows("w_uq"), ((0, 0), (0, HEAD_PAD - NOPE - ROPE), (0, 0))).reshape(HEADS * HEAD_PAD, Q_RANK)
    kv = rows("w_ukv")
    w_k = jnp.pad(kv[:, :NOPE], ((0, 0), (0, HEAD_PAD - NOPE), (0, 0))).reshape(HEADS * HEAD_PAD, KV_RANK)
    w_ukv_p = jnp.concatenate([w_k, kv[:, NOPE:].reshape(HEADS * V_DIM, KV_RANK)], axis=0)
    bits = rows("conv").reshape(N_DEV, -1)[:, :CONV_BITS].reshape(N_DEV, 3, CV_W // N_DEV, 2)
    conv_w = jnp.moveaxis(lax.bitcast_convert_type(bits, F32), 0, 1).reshape(3, CV_W)
    return dict(w_in=w_in_p, w_uq=w_uq_p, w_ukv=w_ukv_p, w_out=rows("w_out").reshape(D, D), conv_w=conv_w)


def _grad_chunks(full):
    d_in = full["w_in"]
    d_in = jnp.concatenate([d_in[768:768 + 672], d_in[1536:], d_in[:768]], axis=0)
    d_uq = full["w_uq"].reshape(HEADS, HEAD_PAD, Q_RANK)[:, :NOPE + ROPE]
    d_k = full["w_ukv"][:HEADS * HEAD_PAD].reshape(HEADS, HEAD_PAD, KV_RANK)[:, :NOPE]
    d_v = full["w_ukv"][HEADS * HEAD_PAD:].reshape(HEADS, V_DIM, KV_RANK)
    mats = dict(w_in=d_in, w_uq=d_uq, w_ukv=jnp.concatenate([d_k, d_v], axis=1), w_out=full["w_out"])
    parts = []
    for name, rows, cols, _ in MIX_PIECES:
        if name == "conv":
            parts.append(jnp.zeros((N_DEV, rows, PACK_W), BF16))
        else:
            a = mats[name].reshape(N_DEV, _packed_rows(rows, cols), PACK_W)
            parts.append(jnp.pad(a, ((0, 0), (0, -a.shape[1] % 16), (0, 0))))
    return jnp.concatenate(parts, axis=1)


SMALL = (("mix_pre_g", (D,)), ("mix_post_g", (D,)), ("ffn_pre_g", (D,)), ("ffn_post_g", (D,)), ("q_norm_g", (Q_RANK,)),
         ("kv_norm_g", (KV_RANK,)), ("sg_ln_g", (SG_W,)), ("sg_ln_b", (SG_W,)), ("w_sp", (4, CHUNK, CHUNK)),
         ("b_sp", (4, CHUNK)), ("out_norm_g", (D,)))
SMALL_ROWS = 576


def _pack_small(vals, nl):
    flat = jnp.concatenate([vals[name].reshape(nl, -1) for name, _ in SMALL] + [vals["conv_w"].reshape(nl, -1)], axis=1)
    return jnp.pad(flat, ((0, 0), (0, SMALL_ROWS * 128 - flat.shape[1]))).reshape(nl * SMALL_ROWS, 128)


def _unpack_small(pack, nl):
    flat = pack.reshape(nl, SMALL_ROWS * 128)
    out, off = {}, 0
    for name, shape in SMALL + (("conv_w", (3, CV_W)),):
        n = int(np.prod(shape))
        out[name] = flat[:, off:off + n].reshape((nl,) + shape)
        off += n
    return out


def _layer_fwd(x, lw, sp, tabs, consts, ffn_pack, next_mix_pack):
    ca, sb, sc = tabs
    z = pre_in_fwd(x, sp["mix_pre_g"], lw["w_in"])
    q, k, v = mla_proj_fwd(z, ca, sb, sc, sp["q_norm_g"], sp["kv_norm_g"], lw["w_uq"], lw["w_ukv"])
    ya, lse, lw["ffn"] = attn_fwd(q, k, v, (ffn_pack,))
    x1 = mix_fwd(x, z, ya, consts["gm"], sp["sg_ln_g"], sp["sg_ln_b"], sp["w_sp"], sp["bias"], lw["conv_w"],
                 sp["out_norm_g"], lw["w_out"], sp["mix_post_g"])
    x2, f, *mix_gathered = ffn_fwd(x1, sp["ffn_pre_g"], lw["ffn"], sp["ffn_post_g"], next_mix_pack)
    return x2, (x, z, q, k, v, ya, lse, x1, f), mix_gathered


def _layer_bwd(dx2, saved, lw, sp, tabs, consts, pending):
    ca, sb, sc = tabs
    x, z, q, k, v, ya, lse, x1, f = saved
    dx1, h2, dab, s, df, d_ffn_pre, d_ffn_post, *received = ffn_bwd(x1, f, dx2, sp["ffn_pre_g"], lw["ffn"], sp["ffn_post_g"],
                                                                    pending)
    ffn_chunks = atb_ffn_chunks(s, df, 2, atb_ffn_chunks(dab, h2, 0)).reshape(N_DEV, len(FFN_PIECES) * FFN_SHARD, D)
    dya, dyc, dzsg, d_w_out, d_mix_post, d_out_norm, d_lng, d_lnb, d_wsp, d_bias = mix_bwd(
        dx1, z, ya, consts["gm"], sp["sg_ln_g"], sp["sg_ln_b"], sp["w_sp"], sp["w_sp_t"], sp["bias"], lw["conv_w"],
        sp["out_norm_g"], lw["w_out"], sp["mix_post_g"])
    dzcv, d_cw = conv_bwd(dyc, z, lw["conv_w"])
    dq, dk, dv, got_ffn = attn_bwd(q, k, v, ya, lse, dya, ((ffn_chunks, True),))
    dzmla, d_w_uq, d_w_ukv, d_gq, d_gkv = mla_proj_bwd(dq, dk, dv, z, ca, sb, sc, sp["q_norm_g"], sp["kv_norm_g"],
                                                       lw["w_uq"], lw["w_ukv"])
    dx, d_w_in, d_mix_pre = pre_in_bwd(x, dx1, dzcv, dzmla, dzsg, sp["mix_pre_g"], lw["w_in"])
    mix_chunks = _grad_chunks(dict(w_in=d_w_in, w_uq=d_w_uq.T, w_ukv=d_w_ukv.T, w_out=d_w_out))
    d_bsp = d_bias[:, ::GROUP].T
    small = dict(mix_pre_g=d_mix_pre[0], mix_post_g=d_mix_post[0], ffn_pre_g=d_ffn_pre[0], ffn_post_g=d_ffn_post[0],
                 q_norm_g=d_gq[0], kv_norm_g=d_gkv[0], sg_ln_g=d_lng[0], sg_ln_b=d_lnb[0], w_sp=d_wsp, b_sp=d_bsp,
                 out_norm_g=d_out_norm[0], conv_w=d_cw[:3])
    small_pack = _pack_small({name: a[None] for name, a in small.items()}, 1)
    return dx, ((mix_chunks, True), (small_pack, False)), [got_ffn] + received


def kernel(x, positions, mix_pre_g, mix_post_g, ffn_pre_g, ffn_post_g, w_in, q_norm_g, w_uq, kv_norm_g, w_ukv, sg_ln_g, sg_ln_b, w_sp, b_sp, conv_w, out_norm_g, w_out, w_gate, w_up, w_down, loss_target, m_mix_pre_g, m_mix_post_g, m_ffn_pre_g, m_ffn_post_g, m_w_in, m_q_norm_g, m_w_uq, m_kv_norm_g, m_w_ukv, m_sg_ln_g, m_sg_ln_b, m_w_sp, m_b_sp, m_conv_w, m_out_norm_g, m_w_out, m_w_gate, m_w_up, m_w_down, v_mix_pre_g, v_mix_post_g, v_ffn_pre_g, v_ffn_post_g, v_w_in, v_q_norm_g, v_w_uq, v_kv_norm_g, v_w_ukv, v_sg_ln_g, v_sg_ln_b, v_w_sp, v_b_sp, v_conv_w, v_out_norm_g, v_w_out, v_w_gate, v_w_up, v_w_down):
    nl = w_in.shape[0]
    t = x.shape[1]
    w = dict(mix_pre_g=mix_pre_g, mix_post_g=mix_post_g, ffn_pre_g=ffn_pre_g, ffn_post_g=ffn_post_g, w_in=w_in,
             q_norm_g=q_norm_g, w_uq=w_uq, kv_norm_g=kv_norm_g, w_ukv=w_ukv, sg_ln_g=sg_ln_g, sg_ln_b=sg_ln_b, w_sp=w_sp,
             b_sp=b_sp, conv_w=conv_w, out_norm_g=out_norm_g, w_out=w_out, w_gate=w_gate, w_up=w_up, w_down=w_down)
    m = dict(mix_pre_g=m_mix_pre_g, mix_post_g=m_mix_post_g, ffn_pre_g=m_ffn_pre_g, ffn_post_g=m_ffn_post_g, w_in=m_w_in,
             q_norm_g=m_q_norm_g, w_uq=m_w_uq, kv_norm_g=m_kv_norm_g, w_ukv=m_w_ukv, sg_ln_g=m_sg_ln_g, sg_ln_b=m_sg_ln_b,
             w_sp=m_w_sp, b_sp=m_b_sp, conv_w=m_conv_w, out_norm_g=m_out_norm_g, w_out=m_w_out, w_gate=m_w_gate,
             w_up=m_w_up, w_down=m_w_down)
    v = dict(mix_pre_g=v_mix_pre_g, mix_post_g=v_mix_post_g, ffn_pre_g=v_ffn_pre_g, ffn_post_g=v_ffn_post_g, w_in=v_w_in,
             q_norm_g=v_q_norm_g, w_uq=v_w_uq, kv_norm_g=v_kv_norm_g, w_ukv=v_w_ukv, sg_ln_g=v_sg_ln_g, sg_ln_b=v_sg_ln_b,
             w_sp=v_w_sp, b_sp=v_b_sp, conv_w=v_conv_w, out_norm_g=v_out_norm_g, w_out=v_w_out, w_gate=v_w_gate,
             w_up=v_w_up, w_down=v_w_down)

    mix_pack = _to_pack(w, BF16, MIX_PIECES, conv=w["conv_w"])
    ffn_pack = _to_pack(w, BF16, FFN_PIECES)
    consts = dict(gm=jnp.asarray(np.kron(np.eye(SG_W // GROUP), np.full((GROUP, GROUP), 1.0 / GROUP)), BF16))
    smalls = []
    for l in range(nl):
        sp = {name: w[name][l].reshape(1, -1) for name, shape in SMALL if len(shape) == 1}
        sp["w_sp"] = w["w_sp"][l]
        sp["w_sp_t"] = jnp.swapaxes(w["w_sp"][l], 1, 2)
        sp["bias"] = jnp.repeat(w["b_sp"][l].T, GROUP, axis=1)
        smalls.append(sp)
    inv_freq = 1.0 / (ROPE_THETA ** (jnp.arange(0, ROPE // 2, dtype=F32) / (ROPE // 2)))
    inv = jnp.zeros((1, HEAD_PAD), F32).at[0, NOPE:NOPE + ROPE].set(jnp.concatenate([inv_freq, inv_freq]))
    tabs = rope_tables(positions.reshape(t, 1).astype(F32), inv)

    h = x[0]
    saved, layers = [], []
    mix_gathered = [all_gather(mix_pack[0])]
    for l in range(nl):
        layers.append(_mix_weights(mix_gathered[0]))
        h, s, mix_gathered = _layer_fwd(h, layers[l], smalls[l], tabs, consts, ffn_pack[l],
                                        (mix_pack[l + 1],) if l + 1 < nl else ())
        saved.append(s)
    sq, dh = loss_head(h, loss_target[0])
    loss = lax.psum(0.5 * sq[0, 0] / D, ("x", "y", "c"))

    got_ffn, got_mix, got_small = [None] * nl, [None] * nl, [None] * nl
    pending = ()
    for l in reversed(range(nl)):
        dh, new_pending, received = _layer_bwd(dh, saved[l], layers[l], smalls[l], tabs, consts, pending)
        got_ffn[l] = received[0]
        if pending:
            got_mix[l + 1], got_small[l + 1] = received[1:]
        pending = new_pending

    me = 4 * lax.axis_index("x") + 2 * lax.axis_index("y") + lax.axis_index("c")
    *ffn_new, got_mix[0], got_small[0] = sum_adamw(got_ffn, *[_to_pack(d, F32, FFN_PIECES) for d in (w, m, v)], 176,
                                                   carried=pending)
    mix_new = sum_adamw(got_mix, *[_to_pack(d, F32, MIX_PIECES) for d in (w, m, v)], 208)
    got_small = jnp.concatenate(got_small, axis=1)
    g_big, d_big, m_big, v_big = [{**_from_pack(a, FFN_PIECES), **_from_pack(b, MIX_PIECES)}
                                  for a, b in zip(ffn_new, mix_new)]

    def full_conv(a):
        return lax.dynamic_update_slice(jnp.zeros((nl, 3, CV_W), F32), a, (0, 0, me * (CV_W // N_DEV)))

    def small_pack(d):
        return _pack_small({**{name: d[name] for name, _ in SMALL}, "conv_w": full_conv(d["conv_w"])}, nl)

    g_small, d_small, m_small, v_small = [_unpack_small(p[0], nl) for p in
                                          sum_adamw([got_small], small_pack(w)[None], small_pack(m)[None],
                                                    small_pack(v)[None], 1152)]
    outs = []
    for big, small in ((g_big, g_small), (d_big, d_small), (m_big, m_small), (v_big, v_small)):
        for name in w:
            if name == "conv_w":
                outs.append(lax.dynamic_slice(small[name], (0, 0, me * (CV_W // N_DEV)), (nl, 3, CV_W // N_DEV)))
            elif name in small:
                outs.append(small[name])
            else:
                outs.append(big[name])
    return (loss, dh[None], *outs)
```

```python
import functools

import jax
import jax.numpy as jnp
import numpy as np
from jax import lax
from jax.experimental import pallas as pl
from jax.experimental.pallas import tpu as pltpu

F32 = jnp.float32
BF16 = jnp.bfloat16

D = 1024
Q_RANK = 384
KV_RANK = 256
ROPE = 32
HEADS = 8
NOPE = 64
V_DIM = 64
HEAD_PAD = 128
SG_W = 256
CV_W = 256
CHUNK = 128
GROUP = 64
D_FF = 2816
IN_W = 1952
Z_W = 2048
Z_CV, Z_MLA, Z_SG = 0, 768, 1536
EPS = 1e-6
ROPE_THETA = 10000.0
SCALE = (NOPE + ROPE) ** -0.5
LOG2E = 1.4426950408889634
SCALE_LOG2E = SCALE * LOG2E
NEG = -1e30
N_DEV = 8

ADAM_LR, ADAM_B1, ADAM_B2, ADAM_EPS, ADAM_WD, ADAM_STEP = 0.001, 0.9, 0.999, 1e-08, 0.01, 10

VMEM_LIMIT = 56 * 1024 * 1024

TM = 512
TM_ROW = 1024
TM_FFN = 256
FFN_SLAB = 256
HALO = 16
TQ = 512
FWD_UNROLL = 2
FWD_HEADS = 2
TT = 4096


def _cp(sem, vmem=VMEM_LIMIT):
    return pltpu.CompilerParams(dimension_semantics=sem, vmem_limit_bytes=vmem)


def _whole():
    return pl.BlockSpec(memory_space=pltpu.VMEM)


def _mm(a, b):
    return jnp.dot(a, b, preferred_element_type=F32)


def _mm_nt(a, b):
    return lax.dot_general(a, b, (((1,), (1,)), ((), ())), preferred_element_type=F32)


def _mm_tn(a, b):
    return lax.dot_general(a, b, (((0,), (0,)), ((), ())), preferred_element_type=F32)


def _rms_fwd(x, g):
    r = lax.rsqrt(jnp.mean(x * x, axis=-1, keepdims=True) + EPS)
    xh = x * r
    return xh * g, xh, r


def _rms_bwd(xh, r, g, dy):
    dxh = dy * g
    dx = r * (dxh - xh * jnp.mean(dxh * xh, axis=-1, keepdims=True))
    dg = jnp.sum(dy * xh, axis=0, keepdims=True)
    return dx, dg


def _gmean(v, gm):
    hi = v.astype(BF16)
    lo = (v - hi.astype(F32)).astype(BF16)
    return _mm(hi, gm) + _mm(lo, gm)


def _gelu(x):
    c = np.float32(np.sqrt(2.0 / np.pi))
    u = c * (x + 0.044715 * x * x * x)
    t = jnp.tanh(u)
    return 0.5 * x * (1.0 + t), t


def _gelu_grad(x, t):
    c = np.float32(np.sqrt(2.0 / np.pi))
    return 0.5 * (1.0 + t) + 0.5 * x * (1.0 - t * t) * c * (1.0 + 3.0 * 0.044715 * x * x)


def _rope(t, ca, sb, sc):
    return t * ca + pltpu.roll(t, HEAD_PAD - 16, 1) * sb + pltpu.roll(t, 16, 1) * sc


def _rope_t(dt, ca, sb, sc):
    return dt * ca + pltpu.roll(dt * sb, 16, 1) + pltpu.roll(dt * sc, HEAD_PAD - 16, 1)


def _shift_down(y, k, head):
    out = pltpu.roll(y, k, 0)
    row = lax.broadcasted_iota(jnp.int32, y.shape, 0)
    for j in range(k):
        out = jnp.where(row == j, head[head.shape[0] - k + j:head.shape[0] - k + j + 1, :], out)
    return out


def _shift_up(y, k, tail):
    n = y.shape[0]
    out = pltpu.roll(y, n - k, 0)
    row = lax.broadcasted_iota(jnp.int32, y.shape, 0)
    for j in range(k):
        out = jnp.where(row == n - k + j, tail[j:j + 1, :], out)
    return out


def rope_tables(pos, inv):
    t = pos.shape[0]
    tm = min(TM, t)

    def body(pos_ref, inv_ref, ca_ref, sb_ref, sc_ref):
        ang = pos_ref[...] * inv_ref[...]
        c = jnp.cos(ang)
        s = jnp.sin(ang)
        lane = lax.broadcasted_iota(jnp.int32, ang.shape, 1)
        ca_ref[...] = jnp.where(lane < NOPE, 1.0, jnp.where(lane < NOPE + ROPE, c, 0.0))
        sb_ref[...] = jnp.where((lane >= NOPE) & (lane < NOPE + 16), -s, 0.0)
        sc_ref[...] = jnp.where((lane >= NOPE + 16) & (lane < NOPE + ROPE), s, 0.0)

    out = jax.ShapeDtypeStruct((t, HEAD_PAD), F32)
    blk = pl.BlockSpec((tm, HEAD_PAD), lambda i: (i, 0))
    return pl.pallas_call(
        body, name="rope_tables", grid=(t // tm,),
        in_specs=[pl.BlockSpec((tm, 1), lambda i: (i, 0)), pl.BlockSpec((1, HEAD_PAD), lambda i: (0, 0))],
        out_specs=[blk, blk, blk], out_shape=[out, out, out],
        compiler_params=_cp(("parallel",)),
    )(pos, inv)


def pre_in_fwd(x, g, w):
    t = x.shape[0]
    tm = min(TM_ROW, t)

    def body(x_ref, g_ref, w_ref, z_ref):
        h, _, _ = _rms_fwd(x_ref[...], g_ref[...])
        z_ref[...] = _mm_nt(h.astype(BF16), w_ref[...]).astype(BF16)

    return pl.pallas_call(
        body, name="pre_in_fwd", grid=(t // tm,),
        in_specs=[pl.BlockSpec((tm, D), lambda i: (i, 0)), _whole(), _whole()],
        out_specs=pl.BlockSpec((tm, Z_W), lambda i: (i, 0)),
        out_shape=jax.ShapeDtypeStruct((t, Z_W), BF16),
        compiler_params=_cp(("parallel",)),
    )(x, g, w)


def mla_proj_fwd(z, ca, sb, sc, gq, gkv, wuq, wukv):
    t = z.shape[0]
    tm = min(TM_ROW, t)

    def body(z_ref, ca_ref, sb_ref, sc_ref, gq_ref, gkv_ref, wuq_ref, wukv_ref, q_ref, k_ref, v_ref):
        z = z_ref[...].astype(F32)
        ca, sb, sc = ca_ref[...], sb_ref[...], sc_ref[...]
        cq, _, _ = _rms_fwd(z[:, :Q_RANK], gq_ref[...])
        ckv, _, _ = _rms_fwd(z[:, Q_RANK:Q_RANK + KV_RANK], gkv_ref[...])
        q = _mm_nt(cq.astype(BF16), wuq_ref[...])
        kv = _mm_nt(ckv.astype(BF16), wukv_ref[...])
        kr = _rope(pltpu.roll(z[:, Q_RANK + KV_RANK:], NOPE, 1), ca, sb, sc)
        for h in range(HEADS):
            lanes = slice(h * HEAD_PAD, (h + 1) * HEAD_PAD)
            q_ref[:, lanes] = _rope(q[:, lanes], ca, sb, sc).astype(BF16)
            k_ref[:, lanes] = (kv[:, lanes] + kr).astype(BF16)
        v_ref[...] = kv[:, HEADS * HEAD_PAD:].astype(BF16)

    tab = pl.BlockSpec((tm, HEAD_PAD), lambda i: (i, 0))
    return pl.pallas_call(
        body, name="mla_proj_fwd", grid=(t // tm,),
        in_specs=[pl.BlockSpec((tm, 768), lambda i: (i, 1)), tab, tab, tab, _whole(), _whole(), _whole(), _whole()],
        out_specs=[pl.BlockSpec((tm, HEADS * HEAD_PAD), lambda i: (i, 0)),
                   pl.BlockSpec((tm, HEADS * HEAD_PAD), lambda i: (i, 0)),
                   pl.BlockSpec((tm, HEADS * V_DIM), lambda i: (i, 0))],
        out_shape=[jax.ShapeDtypeStruct((t, HEADS * HEAD_PAD), BF16),
                   jax.ShapeDtypeStruct((t, HEADS * HEAD_PAD), BF16),
                   jax.ShapeDtypeStruct((t, HEADS * V_DIM), BF16)],
        compiler_params=_cp(("parallel",)),
    )(z, ca, sb, sc, gq, gkv, wuq, wukv)


def _each(stages, k):
    def run():
        for stage in stages:
            stage[k]()
    return run


def attn_fwd(q, k, v, carried=()):
    t = q.shape[0]
    tq = min(TQ, t)
    nq = t // tq
    hs = FWD_HEADS
    last_pair = HEADS // hs - 1
    n = len(carried)

    def body(*refs):
        q_ref, k_ref, v_ref = refs[:3]
        o_ref, lse_ref = refs[3 + n:5 + n]
        sems = refs[5 + 2 * n:]
        stages = [_gather_steps(refs[3 + a], refs[5 + n + a], *sems[3 * a:3 * a + 3]) for a in range(n)]
        if n:
            pair = pl.program_id(0)
            pl.when((pair == 0) & (pl.program_id(1) == 0))(_each(stages, 0))
            pl.when((pair == last_pair) & (pl.program_id(1) == 0))(_each(stages, 1))
        i = pl.program_id(1)
        row = lax.broadcasted_iota(jnp.int32, (tq, tq), 0)
        col = lax.broadcasted_iota(jnp.int32, (tq, tq), 1)
        head_lanes = [slice(h * HEAD_PAD, (h + 1) * HEAD_PAD) for h in range(hs)]
        pair_lanes = [slice(p * 2 * V_DIM, (p + 1) * 2 * V_DIM) for p in range(hs // 2)]

        def step(j, carry, masked):
            start = pl.multiple_of(j * tq, tq)
            out = []
            for h in range(hs):
                m, l, acc = carry[h]
                s = _mm_nt(q_ref[:, head_lanes[h]], k_ref[pl.ds(start, tq), head_lanes[h]])
                if masked:
                    s = jnp.where(col <= row, s, NEG)
                m_new = jnp.maximum(m, jnp.max(s, axis=-1, keepdims=True))
                p = jnp.exp2((s - m_new) * SCALE_LOG2E)
                alpha = jnp.exp2((m - m_new) * SCALE_LOG2E)
                l = alpha * l + jnp.sum(p, axis=-1, keepdims=True)
                acc = alpha * acc + _mm(p.astype(BF16), v_ref[pl.ds(start, tq), pair_lanes[h // 2]])
                out.append((m_new, l, acc))
            return tuple(out)

        init = (jnp.full((tq, 1), NEG, F32), jnp.zeros((tq, 1), F32), jnp.zeros((tq, 2 * V_DIM), F32))
        def trip(j, c):
            for u in range(FWD_UNROLL):
                c = step(FWD_UNROLL * j + u, c, False)
            return c

        carry = lax.fori_loop(0, i // FWD_UNROLL, trip, (init,) * hs)
        carry = lax.fori_loop(i - i % FWD_UNROLL, i, lambda j, c: step(j, c, False), carry)
        outs = []
        for h, (m, l, acc) in enumerate(step(i, carry, True)):
            outs.append(acc / l)
            lse_ref[:, head_lanes[h]] = jnp.broadcast_to(m * SCALE + jnp.log(l), (tq, HEAD_PAD))
        lane = lax.broadcasted_iota(jnp.int32, (tq, 2 * V_DIM), 1)
        for p in range(hs // 2):
            o_ref[:, pair_lanes[p]] = jnp.where(lane < V_DIM, outs[2 * p], outs[2 * p + 1])
        if n:
            pl.when((pl.program_id(0) == last_pair) & (i == nq - 1))(_each(stages, 2))

    hbm = pl.BlockSpec(memory_space=pl.ANY)
    return pl.pallas_call(
        body, name=f"attn_fwd_gather{n}" if n else "attn_fwd", grid=(HEADS // hs, nq),
        in_specs=[pl.BlockSpec((tq, hs * HEAD_PAD), lambda p, i: (i, p)),
                  pl.BlockSpec((t, hs * HEAD_PAD), lambda p, i: (0, p)),
                  pl.BlockSpec((t, hs * V_DIM), lambda p, i: (0, p))] + [hbm] * n,
        out_specs=[pl.BlockSpec((tq, hs * V_DIM), lambda p, i: (i, p)),
                   pl.BlockSpec((tq, hs * HEAD_PAD), lambda p, i: (i, p))] + [hbm] * n,
        out_shape=[jax.ShapeDtypeStruct((t, HEADS * V_DIM), F32), jax.ShapeDtypeStruct((t, HEADS * HEAD_PAD), F32)]
        + [jax.ShapeDtypeStruct((N_DEV,) + c.shape, c.dtype) for c in carried],
        scratch_shapes=_comm_sems() * n,
        compiler_params=_cp(("arbitrary", "arbitrary") if n else ("parallel", "parallel")),
    )(q, k, v, *carried)


def _sgu_fwd(zsg, gm, lng, lnb, wc_ref, bias, mixed_ref):
    uv, th = _gelu(zsg)
    u, v0 = uv[:, :SG_W], uv[:, SG_W:]
    vc = v0 - _gmean(v0, gm)
    r = lax.rsqrt(_gmean(vc * vc, gm) + EPS)
    vh = vc * r
    v = vh * lng + lnb
    lane = lax.broadcasted_iota(jnp.int32, (CHUNK, SG_W), 1)
    for c in range(zsg.shape[0] // CHUNK):
        rows = slice(c * CHUNK, (c + 1) * CHUNK)
        vb = v[rows].astype(BF16)
        mixed = bias
        for g in range(SG_W // GROUP):
            mixed = mixed + jnp.where(lane // GROUP == g, _mm(wc_ref[g], vb), 0.0)
        mixed_ref[rows, :] = mixed
    return u, v, vh, r, th


def _conv_fwd(zcv, halo, first, cw):
    gb, gc, hh = zcv[:, :CV_W], zcv[:, CV_W:2 * CV_W], zcv[:, 2 * CV_W:]
    y = gc * hh
    yh = jnp.where(first, 0.0, halo[:, CV_W:2 * CV_W] * halo[:, 2 * CV_W:])
    y1 = _shift_down(y, 1, yh)
    y2 = _shift_down(y, 2, yh)
    conv = y2 * cw[0:1, :] + y1 * cw[1:2, :] + y * cw[2:3, :]
    return gb * conv, conv, y, y1, y2


def _tril_bf16(w_ref, g):
    row = lax.broadcasted_iota(jnp.int32, (CHUNK, CHUNK), 0)
    col = lax.broadcasted_iota(jnp.int32, (CHUNK, CHUNK), 1)
    return jnp.where(col <= row, w_ref[g], 0.0).astype(BF16)


def mix_fwd(x, z, ya, gm, lng, lnb, wsp, bias, cw, gout, wout, gpost):
    t = x.shape[0]
    tm = min(TM_ROW, t)

    def body(x_ref, zcv_ref, halo_ref, zsg_ref, ya_ref, gm_ref, lng_ref, lnb_ref, wsp_ref, bias_ref, cw_ref,
             gout_ref, wout_ref, gpost_ref, x1_ref, wc_ref, mixed_ref):
        i = pl.program_id(0)
        for g in range(SG_W // GROUP):
            wc_ref[g] = _tril_bf16(wsp_ref, g)
        u, _, _, _, _ = _sgu_fwd(zsg_ref[...].astype(F32), gm_ref[...], lng_ref[...], lnb_ref[...], wc_ref, bias_ref[...],
                                 mixed_ref)
        yb = u * mixed_ref[...]
        yc, _, _, _, _ = _conv_fwd(zcv_ref[...].astype(F32), halo_ref[...].astype(F32), i == 0, cw_ref[...])
        gout = gout_ref[...]
        na, _, _ = _rms_fwd(ya_ref[...], gout[:, :512])
        nb, _, _ = _rms_fwd(yb, gout[:, 512:768])
        nc, _, _ = _rms_fwd(yc, gout[:, 768:])
        mix = jnp.concatenate([na, nb, nc], axis=1).astype(BF16)
        o, _, _ = _rms_fwd(_mm(mix, wout_ref[...]), gpost_ref[...])
        x1_ref[...] = x_ref[...] + o

    return pl.pallas_call(
        body, name="mix_fwd", grid=(t // tm,),
        in_specs=[pl.BlockSpec((tm, D), lambda i: (i, 0)),
                  pl.BlockSpec((tm, 768), lambda i: (i, 0)),
                  pl.BlockSpec((HALO, 768), lambda i: (jnp.maximum(i * (tm // HALO) - 1, 0), 0)),
                  pl.BlockSpec((tm, 512), lambda i: (i, 3)),
                  pl.BlockSpec((tm, 512), lambda i: (i, 0)),
                  _whole(), _whole(), _whole(), _whole(), _whole(), _whole(), _whole(), _whole(), _whole()],
        out_specs=pl.BlockSpec((tm, D), lambda i: (i, 0)),
        out_shape=jax.ShapeDtypeStruct((t, D), F32),
        scratch_shapes=[pltpu.VMEM((SG_W // GROUP, CHUNK, CHUNK), BF16), pltpu.VMEM((tm, SG_W), F32)],
        compiler_params=_cp(("arbitrary",)),
    )(x, z, z, z, ya, gm, lng, lnb, wsp, bias, cw, gout, wout, gpost)


def _sigmoid(a):
    return 1.0 / (1.0 + jnp.exp(-a))


FFN_SHARD = D_FF // N_DEV


def _load_ffn_weights(g_ref, wgu_ref, wd_ref, sems):
    copies = []
    for j in range(N_DEV):
        for p, (dst, base) in enumerate(((wgu_ref, 0), (wgu_ref, D_FF), (wd_ref, 0))):
            copies.append(pltpu.make_async_copy(g_ref.at[j, pl.ds(p * FFN_SHARD, FFN_SHARD)],
                                                dst.at[pl.ds(base + j * FFN_SHARD, FFN_SHARD)], sems.at[3 * j + p]))
    for cp in copies:
        cp.start()
    for cp in copies:
        cp.wait()


def _ffn_weight_scratch():
    return [pltpu.VMEM((2 * D_FF, D), BF16), pltpu.VMEM((D_FF, D), BF16), pltpu.SemaphoreType.DMA((3 * N_DEV,))]


def ffn_fwd(x1, gpre, gathered, gpost, carried=()):
    t = x1.shape[0]
    tm = min(TM_FFN, t)
    steps = t // tm
    n = len(carried)

    def body(*refs):
        x_ref, gpre_ref, g_ref, gpost_ref = refs[:4]
        x2_ref, f_ref = refs[4 + n:6 + n]
        wgu_ref, wd_ref, sems = refs[6 + 2 * n:9 + 2 * n]
        comm_sems = refs[9 + 2 * n:]
        stages = [_gather_steps(refs[4 + a], refs[6 + n + a], *comm_sems[3 * a:3 * a + 3]) for a in range(n)]
        if n:
            pl.when(pl.program_id(0) == 0)(_each(stages, 0))
            pl.when(pl.program_id(0) == (3 * steps) // 4)(_each(stages, 1))

        @pl.when(pl.program_id(0) == 0)
        def _():
            _load_ffn_weights(g_ref, wgu_ref, wd_ref, sems)

        x = x_ref[...]
        h, _, _ = _rms_fwd(x, gpre_ref[...])
        ab = _mm_nt(h.astype(BF16), wgu_ref[...])
        a, b = ab[:, :D_FF], ab[:, D_FF:]
        s = a * _sigmoid(a) * b
        f = _mm(s.astype(BF16), wd_ref[...])
        f_ref[...] = f
        x2_ref[...] = x + _rms_fwd(f, gpost_ref[...])[0]
        if n:
            pl.when(pl.program_id(0) == steps - 1)(_each(stages, 2))

    row = pl.BlockSpec((tm, D), lambda i: (i, 0))
    hbm = pl.BlockSpec(memory_space=pl.ANY)
    return pl.pallas_call(
        body, name=f"ffn_fwd_gather{n}" if n else "ffn_fwd", grid=(steps,),
        in_specs=[row, _whole(), hbm, _whole()] + [hbm] * n,
        out_specs=[row, row] + [hbm] * n,
        out_shape=[jax.ShapeDtypeStruct((t, D), F32), jax.ShapeDtypeStruct((t, D), F32)]
        + [jax.ShapeDtypeStruct((N_DEV,) + c.shape, c.dtype) for c in carried],
        scratch_shapes=_ffn_weight_scratch() + _comm_sems() * n,
        compiler_params=_cp(("arbitrary",)),
    )(x1, gpre, gathered, gpost, *carried)


def loss_head(y, target):
    t = y.shape[0]
    tm = min(TM, t)

    def body(y_ref, t_ref, loss_ref, dy_ref):
        @pl.when(pl.program_id(0) == 0)
        def _():
            loss_ref[...] = jnp.zeros_like(loss_ref)

        e = y_ref[...] - t_ref[...]
        dy_ref[...] = e * (1.0 / D)
        loss_ref[...] += jnp.sum(jnp.sum(e * e, axis=-1, keepdims=True), axis=0, keepdims=True)

    return pl.pallas_call(
        body, name="loss_head", grid=(t // tm,),
        in_specs=[pl.BlockSpec((tm, D), lambda i: (i, 0)), pl.BlockSpec((tm, D), lambda i: (i, 0))],
        out_specs=[pl.BlockSpec((1, 128), lambda i: (0, 0)), pl.BlockSpec((tm, D), lambda i: (i, 0))],
        out_shape=[jax.ShapeDtypeStruct((1, 128), F32), jax.ShapeDtypeStruct((t, D), F32)],
        compiler_params=_cp(("arbitrary",)),
    )(y, target)


def _acc(ref, first, val):
    @pl.when(first)
    def _():
        ref[...] = val

    @pl.when(jnp.logical_not(first))
    def _():
        ref[...] += val


def ffn_bwd(x1, f, dx2, gpre, gathered, gpost, carried=()):
    t = x1.shape[0]
    tm = min(TM_FFN, t)
    steps = t // tm
    n = len(carried)

    def body(*refs):
        x_ref, f_ref, dx2_ref, gpre_ref, g_ref, gpost_ref = refs[:6]
        dx1_ref, h_ref, dab_ref, s_ref, df_ref, dgpre_ref, dgpost_ref = refs[6 + n:13 + n]
        ab_ref, ds_ref, wgu_ref, wd_ref, sems = refs[13 + 2 * n:18 + 2 * n]
        comm_sems = refs[18 + 2 * n:]
        stages = [_exchange_steps(refs[6 + a], refs[13 + n + a], carried[a][1], *comm_sems[3 * a:3 * a + 3])
                  for a in range(n)]
        first = pl.program_id(0) == 0
        if n:
            pl.when(first)(_each(stages, 0))

        @pl.when(first)
        def _():
            _load_ffn_weights(g_ref, wgu_ref, wd_ref, sems)

        dx2 = dx2_ref[...]
        gpre, gpost = gpre_ref[...], gpost_ref[...]
        h, xh, rx = _rms_fwd(x_ref[...], gpre)
        h_ref[...] = h.astype(BF16)
        ab_ref[...] = _mm_nt(h_ref[...], wgu_ref[...])
        for c in range(0, D_FF, FFN_SLAB):
            a, b = ab_ref[:, c:c + FFN_SLAB], ab_ref[:, D_FF + c:D_FF + c + FFN_SLAB]
            s_ref[:, c:c + FFN_SLAB] = (a * _sigmoid(a) * b).astype(BF16)
        _, fh, rf = _rms_fwd(f_ref[...], gpost)
        df, dgpost = _rms_bwd(fh, rf, gpost, dx2)
        df_ref[...] = df.astype(BF16)
        ds_ref[...] = _mm_nt(df_ref[...], wd_ref[...])
        for c in range(0, D_FF, FFN_SLAB):
            a, b = ab_ref[:, c:c + FFN_SLAB], ab_ref[:, D_FF + c:D_FF + c + FFN_SLAB]
            ds = ds_ref[:, c:c + FFN_SLAB]
            sg = _sigmoid(a)
            dab_ref[:, c:c + FFN_SLAB] = (ds * b * (sg * (1.0 + a * (1.0 - sg)))).astype(BF16)
            dab_ref[:, D_FF + c:D_FF + c + FFN_SLAB] = (ds * (a * sg)).astype(BF16)
        dx, dgpre = _rms_bwd(xh, rx, gpre, _mm(dab_ref[...], wgu_ref[...]))
        dx1_ref[...] = dx2 + dx
        _acc(dgpre_ref, first, dgpre)
        _acc(dgpost_ref, first, dgpost)
        if n:
            pl.when(pl.program_id(0) == steps - 1)(_each(stages, 1))

    row = lambda w: pl.BlockSpec((tm, w), lambda i: (i, 0))
    vec = pl.BlockSpec((1, D), lambda i: (0, 0))
    hbm = pl.BlockSpec(memory_space=pl.ANY)
    return pl.pallas_call(
        body, name=f"ffn_bwd_exchange{n}" if n else "ffn_bwd", grid=(steps,),
        in_specs=[row(D), row(D), row(D), _whole(), hbm, _whole()] + [hbm] * n,
        out_specs=[row(D), row(D), row(2 * D_FF), row(D_FF), row(D), vec, vec] + [hbm] * n,
        out_shape=[jax.ShapeDtypeStruct((t, D), F32), jax.ShapeDtypeStruct((t, D), BF16),
                   jax.ShapeDtypeStruct((t, 2 * D_FF), BF16), jax.ShapeDtypeStruct((t, D_FF), BF16),
                   jax.ShapeDtypeStruct((t, D), BF16), jax.ShapeDtypeStruct((1, D), F32),
                   jax.ShapeDtypeStruct((1, D), F32)]
        + [jax.ShapeDtypeStruct(src.shape if scatter else (N_DEV,) + src.shape, src.dtype) for src, scatter in carried],
        scratch_shapes=[pltpu.VMEM((tm, 2 * D_FF), F32), pltpu.VMEM((tm, D_FF), F32)] + _ffn_weight_scratch()
        + _comm_sems() * n,
        compiler_params=_cp(("arbitrary",)),
    )(x1, f, dx2, gpre, gathered, gpost, *[src for src, _ in carried])


FFN_TILE_SHARDS = 4


def atb_ffn_chunks(a, b, first_piece, chunks=None):
    t, k = a.shape
    tt = min(TT, t)
    tk = FFN_TILE_SHARDS * FFN_SHARD
    steps = t // tt
    per_piece = N_DEV // FFN_TILE_SHARDS

    def body(*refs):
        a_ref, b_ref, o_ref, acc_ref = refs[0], refs[1], refs[-2], refs[-1]
        i = pl.program_id(1)
        _acc(acc_ref, i == 0, _mm_tn(a_ref[...], b_ref[...]))

        @pl.when(i == steps - 1)
        def _():
            for d in range(FFN_TILE_SHARDS):
                o_ref[d, 0] = acc_ref[d * FFN_SHARD:(d + 1) * FFN_SHARD, :].astype(BF16)

    hbm = pl.BlockSpec(memory_space=pl.ANY)
    return pl.pallas_call(
        body, name="atb_ffn_chunks", grid=(k // tk, steps),
        in_specs=[pl.BlockSpec((tt, tk), lambda j, i: (i, j)), pl.BlockSpec((tt, D), lambda j, i: (i, 0))]
        + ([] if chunks is None else [hbm]),
        out_specs=pl.BlockSpec((FFN_TILE_SHARDS, 1, FFN_SHARD, D),
                               lambda j, i: (j % per_piece, first_piece + j // per_piece, 0, 0)),
        out_shape=jax.ShapeDtypeStruct((N_DEV, len(FFN_PIECES), FFN_SHARD, D), BF16),
        input_output_aliases={} if chunks is None else {2: 0},
        scratch_shapes=[pltpu.VMEM((tk, D), F32)],
        compiler_params=_cp(("parallel", "arbitrary")),
    )(a, b, *([] if chunks is None else [chunks]))


def mix_bwd(dx1, z, ya, gm, lng, lnb, wsp, wspt, bias, cw, gout, wout, gpost):
    t = dx1.shape[0]
    tm = min(TM, t)
    steps = t // tm
    ng = SG_W // GROUP

    def body(dx1_ref, zcv_ref, halo_ref, zsg_ref, ya_ref, gm_ref, lng_ref, lnb_ref, wsp_ref, wspt_ref, bias_ref,
             cw_ref, gout_ref, wout_ref, gpost_ref,
             dya_ref, dyc_ref, dzsg_ref, dwout_ref, dgpost_ref, dgout_ref, dlng_ref, dlnb_ref, dwsp_ref,
             dbias_ref, wc_ref, wct_ref, mixed_ref, dv_ref, dwout_acc):
        i = pl.program_id(0)
        first = i == 0
        gm = gm_ref[...]
        for g in range(ng):
            wc_ref[g] = _tril_bf16(wsp_ref, g)
            wct_ref[g] = jnp.where(
                lax.broadcasted_iota(jnp.int32, (CHUNK, CHUNK), 0) <= lax.broadcasted_iota(jnp.int32, (CHUNK, CHUNK), 1),
                wspt_ref[g], 0.0).astype(BF16)
        zsg = zsg_ref[...].astype(F32)
        lng = lng_ref[...]
        u, v, vh, r, th = _sgu_fwd(zsg, gm, lng, lnb_ref[...], wc_ref, bias_ref[...], mixed_ref)
        mixed = mixed_ref[...]
        yb = u * mixed
        yc, _, _, _, _ = _conv_fwd(zcv_ref[...].astype(F32), halo_ref[...].astype(F32), first, cw_ref[...])
        gout, gpost = gout_ref[...], gpost_ref[...]
        ga, gb_, gc_ = gout[:, :512], gout[:, 512:768], gout[:, 768:]
        na, yah, ra = _rms_fwd(ya_ref[...], ga)
        nb, ybh, rb = _rms_fwd(yb, gb_)
        nc, ych, rc = _rms_fwd(yc, gc_)
        mix = jnp.concatenate([na, nb, nc], axis=1).astype(BF16)
        _, oh, ro = _rms_fwd(_mm(mix, wout_ref[...]), gpost)
        do, dgpost = _rms_bwd(oh, ro, gpost, dx1_ref[...])
        dob = do.astype(BF16)
        dmix = _mm_nt(dob, wout_ref[...])
        dya, dga = _rms_bwd(yah, ra, ga, dmix[:, :512])
        dyb, dgb = _rms_bwd(ybh, rb, gb_, dmix[:, 512:768])
        dyc, dgc = _rms_bwd(ych, rc, gc_, dmix[:, 768:])
        dya_ref[...] = dya
        dyc_ref[...] = dyc
        _acc(dwout_acc, first, _mm_tn(mix, dob))

        @pl.when(i == steps - 1)
        def _():
            dwout_ref[...] = dwout_acc[...].astype(BF16)

        _acc(dgpost_ref, first, dgpost)
        _acc(dgout_ref, first, jnp.concatenate([dga, dgb, dgc], axis=1))
        du = dyb * mixed
        dmixed = dyb * u
        lane = lax.broadcasted_iota(jnp.int32, (CHUNK, SG_W), 1)
        row = lax.broadcasted_iota(jnp.int32, (CHUNK, CHUNK), 0)
        col = lax.broadcasted_iota(jnp.int32, (CHUNK, CHUNK), 1)
        dbias = jnp.zeros((CHUNK, SG_W), F32)
        dw = [jnp.zeros((CHUNK, CHUNK), F32) for _ in range(ng)]
        for c in range(tm // CHUNK):
            rows = slice(c * CHUNK, (c + 1) * CHUNK)
            dm = dmixed[rows]
            dbias = dbias + dm
            dmb = dm.astype(BF16)
            vb = v[rows].astype(BF16)
            dvc = jnp.zeros((CHUNK, SG_W), F32)
            for g in range(ng):
                in_g = lane // GROUP == g
                dvc = dvc + jnp.where(in_g, _mm(wct_ref[g], dmb), 0.0)
                dw[g] = dw[g] + _mm_nt(jnp.where(in_g, dmb, jnp.zeros_like(dmb)), vb)
            dv_ref[rows, :] = dvc
        for g in range(ng):
            dwg = jnp.where(col <= row, dw[g], 0.0)

            @pl.when(first)
            def _():
                dwsp_ref[g] = dwg

            @pl.when(jnp.logical_not(first))
            def _():
                dwsp_ref[g] += dwg
        _acc(dbias_ref, first, _gmean(dbias, gm) * GROUP)
        dv = dv_ref[...]
        _acc(dlng_ref, first, jnp.sum(dv * vh, axis=0, keepdims=True))
        _acc(dlnb_ref, first, jnp.sum(dv, axis=0, keepdims=True))
        dvh = dv * lng
        dv0 = r * (dvh - _gmean(dvh, gm) - vh * _gmean(dvh * vh, gm))
        dzsg_ref[...] = (jnp.concatenate([du, dv0], axis=1) * _gelu_grad(zsg, th)).astype(BF16)

    row_ = lambda w: pl.BlockSpec((tm, w), lambda i: (i, 0))
    vec = lambda w: pl.BlockSpec((1, w), lambda i: (0, 0))
    return pl.pallas_call(
        body, name="mix_bwd", grid=(steps,),
        in_specs=[row_(D),
                  pl.BlockSpec((tm, 768), lambda i: (i, 0)),
                  pl.BlockSpec((HALO, 768), lambda i: (jnp.maximum(i * (tm // HALO) - 1, 0), 0)),
                  pl.BlockSpec((tm, 512), lambda i: (i, 3)),
                  row_(512),
                  _whole(), _whole(), _whole(), _whole(), _whole(), _whole(), _whole(), _whole(), _whole(), _whole()],
        out_specs=[row_(512), row_(CV_W), row_(512), pl.BlockSpec((D, D), lambda i: (0, 0)), vec(D), vec(D), vec(SG_W),
                   vec(SG_W), pl.BlockSpec((ng, CHUNK, CHUNK), lambda i: (0, 0, 0)),
                   pl.BlockSpec((CHUNK, SG_W), lambda i: (0, 0))],
        out_shape=[jax.ShapeDtypeStruct((t, 512), F32), jax.ShapeDtypeStruct((t, CV_W), F32),
                   jax.ShapeDtypeStruct((t, 512), BF16), jax.ShapeDtypeStruct((D, D), BF16),
                   jax.ShapeDtypeStruct((1, D), F32),
                   jax.ShapeDtypeStruct((1, D), F32), jax.ShapeDtypeStruct((1, SG_W), F32),
                   jax.ShapeDtypeStruct((1, SG_W), F32), jax.ShapeDtypeStruct((ng, CHUNK, CHUNK), F32),
                   jax.ShapeDtypeStruct((CHUNK, SG_W), F32)],
        scratch_shapes=[pltpu.VMEM((ng, CHUNK, CHUNK), BF16), pltpu.VMEM((ng, CHUNK, CHUNK), BF16),
                        pltpu.VMEM((tm, SG_W), F32), pltpu.VMEM((tm, SG_W), F32), pltpu.VMEM((D, D), F32)],
        compiler_params=_cp(("arbitrary",)),
    )(dx1, z, z, z, ya, gm, lng, lnb, wsp, wspt, bias, cw, gout, wout, gpost)


def conv_bwd(dyc, z, cw):
    t = dyc.shape[0]
    tm = min(TM_ROW, t)
    hb = tm // 8
    last_blk = t // 8 - 1

    def body(dyc_ref, dyct_ref, zcv_ref, head_ref, tail_ref, cw_ref, dz_ref, dcw_ref):
        i = pl.program_id(0)
        first = i == 0
        last = i == pl.num_programs(0) - 1
        cw = cw_ref[...]
        zcv = zcv_ref[...].astype(F32)
        gb, gc, hh = zcv[:, :CV_W], zcv[:, CV_W:2 * CV_W], zcv[:, 2 * CV_W:]
        _, conv, y, y1, y2 = _conv_fwd(zcv, head_ref[...].astype(F32), first, cw)
        dyc = dyc_ref[...]
        dconv = dyc * gb
        tail = jnp.where(last, 0.0, dyct_ref[...] * tail_ref[:8, :CV_W].astype(F32))
        d1 = _shift_up(dconv, 1, tail)
        d2 = _shift_up(dconv, 2, tail)
        dy = dconv * cw[2:3, :] + d1 * cw[1:2, :] + d2 * cw[0:1, :]
        dz_ref[...] = jnp.concatenate([dyc * conv, dy * hh, dy * gc], axis=1).astype(BF16)
        tap = lax.broadcasted_iota(jnp.int32, (8, CV_W), 0)
        dcw = jnp.where(tap == 0, jnp.sum(dconv * y2, axis=0, keepdims=True),
                        jnp.where(tap == 1, jnp.sum(dconv * y1, axis=0, keepdims=True),
                                  jnp.where(tap == 2, jnp.sum(dconv * y, axis=0, keepdims=True), 0.0)))
        _acc(dcw_ref, first, dcw)

    return pl.pallas_call(
        body, name="conv_bwd", grid=(t // tm,),
        in_specs=[pl.BlockSpec((tm, CV_W), lambda i: (i, 0)),
                  pl.BlockSpec((8, CV_W), lambda i: (jnp.minimum((i + 1) * hb, last_blk), 0)),
                  pl.BlockSpec((tm, 768), lambda i: (i, 0)),
                  pl.BlockSpec((HALO, 768), lambda i: (jnp.maximum(i * (tm // HALO) - 1, 0), 0)),
                  pl.BlockSpec((HALO, 768), lambda i: (jnp.minimum((i + 1) * (tm // HALO), t // HALO - 1), 0)),
                  _whole()],
        out_specs=[pl.BlockSpec((tm, 768), lambda i: (i, 0)), pl.BlockSpec((8, CV_W), lambda i: (0, 0))],
        out_shape=[jax.ShapeDtypeStruct((t, 768), BF16), jax.ShapeDtypeStruct((8, CV_W), F32)],
        compiler_params=_cp(("arbitrary",)),
    )(dyc, dyc, z, z, z, cw)


def attn_bwd(q, k, v, o, lse, do, carried=()):
    t = q.shape[0]
    tq = min(TQ, t)
    nq = t // tq
    last_pair = HEADS // 2 - 1
    n = len(carried)

    def body(*refs):
        j = pl.program_id(1)
        q_ref, k_ref, v_ref, o_ref, lse_ref, do_ref = refs[:6]
        dq_out_ref, dk_ref, dv_ref = refs[6 + n:9 + n]
        dq_ref = refs[9 + 2 * n]
        sems = refs[10 + 2 * n:]
        stages = [_exchange_steps(refs[6 + a], refs[9 + n + a], carried[a][1], *sems[3 * a:3 * a + 3]) for a in range(n)]
        if n:
            pl.when((pl.program_id(0) == 0) & (j == 0))(_each(stages, 0))

        @pl.when(j == 0)
        def _():
            dq_ref[...] = jnp.zeros_like(dq_ref)

        row = lax.broadcasted_iota(jnp.int32, (tq, tq), 0)
        col = lax.broadcasted_iota(jnp.int32, (tq, tq), 1)
        vlane = lax.broadcasted_iota(jnp.int32, (tq, 2 * V_DIM), 1)
        head_lanes = [slice(h * HEAD_PAD, (h + 1) * HEAD_PAD) for h in range(2)]

        def step(i, carry, masked):
            start = pl.multiple_of(i * tq, tq)
            do_blk = do_ref[pl.ds(start, tq), :]
            o_blk = o_ref[pl.ds(start, tq), :]
            vb = v_ref[...]
            dks, dv_acc = [], carry[2]
            for h in range(2):
                lanes = head_lanes[h]
                qb = q_ref[pl.ds(start, tq), lanes]
                kb = k_ref[:, lanes]
                dob = jnp.where((vlane // V_DIM) == h, do_blk, 0.0)
                delta = jnp.sum(dob * o_blk, axis=-1, keepdims=True)
                lse2 = lse_ref[pl.ds(start, tq), lanes][:, 0:1] * LOG2E
                s = _mm_nt(qb, kb)
                if masked:
                    s = jnp.where(col <= row, s, NEG)
                p = jnp.exp2(s * SCALE_LOG2E - lse2)
                dob16 = dob.astype(BF16)
                dp = _mm_nt(dob16, vb)
                ds = (p * (dp - delta) * SCALE).astype(BF16)
                dv_acc = dv_acc + _mm_tn(p.astype(BF16), dob16)
                dks.append(carry[h] + _mm_tn(ds, qb))
                dq_ref[pl.ds(start, tq), lanes] += _mm(ds, kb)
            return dks[0], dks[1], dv_acc

        zero = jnp.zeros((tq, HEAD_PAD), F32)
        carry = step(j, (zero, zero, jnp.zeros((tq, 2 * V_DIM), F32)), True)
        rest = nq - 1 - j
        carry = lax.fori_loop(0, rest // 2, lambda u, c: step(j + 2 + 2 * u, step(j + 1 + 2 * u, c, False), False), carry)
        dk0, dk1, dv_acc = lax.fori_loop(0, rest % 2, lambda _, c: step(nq - 1, c, False), carry)
        dk_ref[:, head_lanes[0]] = dk0.astype(BF16)
        dk_ref[:, head_lanes[1]] = dk1.astype(BF16)
        dv_ref[...] = dv_acc.astype(BF16)

        @pl.when(j == nq - 1)
        def _():
            dq_out_ref[...] = dq_ref[...].astype(BF16)

        if n:
            pl.when((pl.program_id(0) == last_pair) & (j == nq - 1))(_each(stages, 1))

    hbm = pl.BlockSpec(memory_space=pl.ANY)
    return pl.pallas_call(
        body, name=f"attn_bwd_exchange{n}" if n else "attn_bwd", grid=(HEADS // 2, nq),
        in_specs=[pl.BlockSpec((t, 2 * HEAD_PAD), lambda p, j: (0, p)),
                  pl.BlockSpec((tq, 2 * HEAD_PAD), lambda p, j: (j, p)),
                  pl.BlockSpec((tq, 2 * V_DIM), lambda p, j: (j, p)),
                  pl.BlockSpec((t, 2 * V_DIM), lambda p, j: (0, p)),
                  pl.BlockSpec((t, 2 * HEAD_PAD), lambda p, j: (0, p)),
                  pl.BlockSpec((t, 2 * V_DIM), lambda p, j: (0, p))] + [hbm] * n,
        out_specs=[pl.BlockSpec((t, 2 * HEAD_PAD), lambda p, j: (0, p)),
                   pl.BlockSpec((tq, 2 * HEAD_PAD), lambda p, j: (j, p)),
                   pl.BlockSpec((tq, 2 * V_DIM), lambda p, j: (j, p))] + [hbm] * n,
        out_shape=[jax.ShapeDtypeStruct((t, HEADS * HEAD_PAD), BF16), jax.ShapeDtypeStruct((t, HEADS * HEAD_PAD), BF16),
                   jax.ShapeDtypeStruct((t, HEADS * V_DIM), BF16)]
        + [jax.ShapeDtypeStruct(src.shape if scatter else (N_DEV,) + src.shape, src.dtype) for src, scatter in carried],
        scratch_shapes=[pltpu.VMEM((t, 2 * HEAD_PAD), F32)] + _comm_sems() * n,
        compiler_params=_cp(("arbitrary", "arbitrary") if n else ("parallel", "arbitrary")),
    )(q, k, v, o, lse, do, *[src for src, _ in carried])


def mla_proj_bwd(dq, dk, dv, z, ca, sb, sc, gq, gkv, wuq, wukv):
    t = z.shape[0]
    tm = min(TM_ROW, t)
    steps = t // tm

    def body(dq_ref, dk_ref, dv_ref, z_ref, ca_ref, sb_ref, sc_ref, gq_ref, gkv_ref, wuq_ref, wukv_ref,
             dz_ref, dwuq_ref, dwukv_ref, dgq_ref, dgkv_ref, dqp_ref, dkvp_ref, uq_acc, ukv_acc):
        first = pl.program_id(0) == 0
        z = z_ref[...].astype(F32)
        ca, sb, sc = ca_ref[...], sb_ref[...], sc_ref[...]
        gq, gkv = gq_ref[...], gkv_ref[...]
        cq, cqh, rq = _rms_fwd(z[:, :Q_RANK], gq)
        ckv, ckvh, rkv = _rms_fwd(z[:, Q_RANK:Q_RANK + KV_RANK], gkv)
        lane = lax.broadcasted_iota(jnp.int32, (tm, HEAD_PAD), 1)
        dkr = jnp.zeros((tm, HEAD_PAD), F32)
        for h in range(HEADS):
            lanes = slice(h * HEAD_PAD, (h + 1) * HEAD_PAD)
            dqp_ref[:, lanes] = _rope_t(dq_ref[:, lanes].astype(F32), ca, sb, sc).astype(BF16)
            dkh = dk_ref[:, lanes].astype(F32)
            dkr = dkr + dkh
            dkvp_ref[:, lanes] = jnp.where(lane < NOPE, dkh, 0.0).astype(BF16)
        dkvp_ref[:, HEADS * HEAD_PAD:] = dv_ref[...].astype(BF16)
        dkr = pltpu.roll(_rope_t(jnp.where(lane >= NOPE, dkr, 0.0), ca, sb, sc), HEAD_PAD - NOPE, 1)
        dkr = jnp.where(lane < ROPE, dkr, 0.0)
        dcq = _mm(dqp_ref[...], wuq_ref[...])
        dckv = _mm(dkvp_ref[...], wukv_ref[...])
        dzq, dgq = _rms_bwd(cqh, rq, gq, dcq)
        dzkv, dgkv = _rms_bwd(ckvh, rkv, gkv, dckv)
        dz_ref[...] = jnp.concatenate([dzq, dzkv, dkr], axis=1).astype(BF16)
        _acc(uq_acc, first, _mm_tn(cq.astype(BF16), dqp_ref[...]))
        _acc(ukv_acc, first, _mm_tn(ckv.astype(BF16), dkvp_ref[...]))
        _acc(dgq_ref, first, dgq)
        _acc(dgkv_ref, first, dgkv)

        @pl.when(pl.program_id(0) == steps - 1)
        def _():
            dwuq_ref[...] = uq_acc[...].astype(BF16)
            dwukv_ref[...] = ukv_acc[...].astype(BF16)

    row = lambda w: pl.BlockSpec((tm, w), lambda i: (i, 0))
    vec = lambda w: pl.BlockSpec((1, w), lambda i: (0, 0))
    whole = lambda r, c: pl.BlockSpec((r, c), lambda i: (0, 0))
    nq, nkv = HEADS * HEAD_PAD, HEADS * (HEAD_PAD + V_DIM)
    return pl.pallas_call(
        body, name="mla_proj_bwd", grid=(steps,),
        in_specs=[row(1024), row(1024), row(512), pl.BlockSpec((tm, 768), lambda i: (i, 1)),
                  row(HEAD_PAD), row(HEAD_PAD), row(HEAD_PAD), _whole(), _whole(), _whole(), _whole()],
        out_specs=[row(768), whole(Q_RANK, nq), whole(KV_RANK, nkv), vec(Q_RANK), vec(KV_RANK)],
        out_shape=[jax.ShapeDtypeStruct((t, 768), BF16), jax.ShapeDtypeStruct((Q_RANK, nq), BF16),
                   jax.ShapeDtypeStruct((KV_RANK, nkv), BF16), jax.ShapeDtypeStruct((1, Q_RANK), F32),
                   jax.ShapeDtypeStruct((1, KV_RANK), F32)],
        scratch_shapes=[pltpu.VMEM((tm, nq), BF16), pltpu.VMEM((tm, nkv), BF16), pltpu.VMEM((Q_RANK, nq), F32),
                        pltpu.VMEM((KV_RANK, nkv), F32)],
        compiler_params=_cp(("arbitrary",)),
    )(dq, dk, dv, z, ca, sb, sc, gq, gkv, wuq, wukv)


def pre_in_bwd(x, dx1, dzcv, dzmla, dzsg, g, w):
    t = x.shape[0]
    tm = min(TM, t)
    steps = t // tm

    def body(x_ref, dx1_ref, dzcv_ref, dzmla_ref, dzsg_ref, g_ref, w_ref, dx_ref, dw_ref, dg_ref, acc_ref):
        first = pl.program_id(0) == 0
        g = g_ref[...]
        h, xh, r = _rms_fwd(x_ref[...], g)
        dz = jnp.concatenate([dzcv_ref[...], dzmla_ref[...], dzsg_ref[...]], axis=1)
        dx, dg = _rms_bwd(xh, r, g, _mm(dz, w_ref[...]))
        dx_ref[...] = dx1_ref[...] + dx
        _acc(acc_ref, first, _mm_tn(dz, h.astype(BF16)))
        _acc(dg_ref, first, dg)

        @pl.when(pl.program_id(0) == steps - 1)
        def _():
            dw_ref[...] = acc_ref[...].astype(BF16)

    row = lambda w_: pl.BlockSpec((tm, w_), lambda i: (i, 0))
    return pl.pallas_call(
        body, name="pre_in_bwd", grid=(steps,),
        in_specs=[row(D), row(D), row(768), row(768), row(512), _whole(), _whole()],
        out_specs=[row(D), pl.BlockSpec((Z_W, D), lambda i: (0, 0)), pl.BlockSpec((1, D), lambda i: (0, 0))],
        out_shape=[jax.ShapeDtypeStruct((t, D), F32), jax.ShapeDtypeStruct((Z_W, D), BF16),
                   jax.ShapeDtypeStruct((1, D), F32)],
        scratch_shapes=[pltpu.VMEM((Z_W, D), F32)],
        compiler_params=_cp(("arbitrary",)),
    )(x, dx1, dzcv, dzmla, dzsg, g, w)


MESH = pl.DeviceIdType.MESH


def _place():
    return lax.axis_index("x"), lax.axis_index("y"), lax.axis_index("c")


def _comm_sems():
    return [pltpu.SemaphoreType.DMA((7,)), pltpu.SemaphoreType.DMA((7,)), pltpu.SemaphoreType.DMA]


def _gather_steps(x_ref, out_ref, send_sems, recv_sems, local_sem):
    x, y, c = _place()
    me, sibling = (x, y, c), (x, y, 1 - c)
    chips = [(1 - x, y), (x, 1 - y), (1 - x, 1 - y)]

    def slot(px, py, pc):
        return out_ref.at[4 * px + 2 * py + pc]

    def copy(k, blk, to, src=None):
        return pltpu.make_async_remote_copy(
            src_ref=slot(*blk) if src is None else src, dst_ref=slot(*blk),
            send_sem=send_sems.at[k], recv_sem=recv_sems.at[k], device_id=to, device_id_type=MESH)

    mine = pltpu.make_async_copy(x_ref, slot(*me), local_sem)
    first = [copy(0, me, sibling, src=x_ref)] + [copy(1 + j, me, (*chip, c), src=x_ref) for j, chip in enumerate(chips)]
    passed = [copy(4 + j, (*chip, c), sibling) for j, chip in enumerate(chips)]

    def start():
        mine.start()
        for cp in first:
            cp.start()

    def forward():
        for j, chip in enumerate(chips):
            copy(1 + j, (*chip, c), me).wait_recv()
            passed[j].start()

    def finish():
        copy(0, sibling, me).wait_recv()
        for j, chip in enumerate(chips):
            copy(4 + j, (*chip, 1 - c), me).wait_recv()
        for cp in first + passed:
            cp.wait_send()
        mine.wait()

    return start, forward, finish


def _exchange_steps(src_ref, out_ref, scatter, send_sems, recv_sems, local_sem):
    x, y, c = _place()
    me = 4 * x + 2 * y + c
    own = pltpu.make_async_copy(src_ref.at[me] if scatter else src_ref, out_ref.at[me], local_sem)
    copies = []
    for k in range(1, N_DEV):
        px = 1 - x if k & 4 else x
        py = 1 - y if k & 2 else y
        pc = 1 - c if k & 1 else c
        copies.append(pltpu.make_async_remote_copy(
            src_ref=src_ref.at[4 * px + 2 * py + pc] if scatter else src_ref, dst_ref=out_ref.at[me],
            send_sem=send_sems.at[k - 1], recv_sem=recv_sems.at[k - 1], device_id=(px, py, pc), device_id_type=MESH))

    def start():
        own.start()
        for cp in copies:
            cp.start()

    def finish():
        for cp in copies:
            cp.wait_recv()
        for cp in copies:
            cp.wait_send()
        own.wait()

    return start, finish


def all_gather(block):
    def body(x_ref, out_ref, *sems):
        for stage in _gather_steps(x_ref, out_ref, *sems):
            stage()

    return pl.pallas_call(
        body, name="all_gather",
        in_specs=[pl.BlockSpec(memory_space=pl.ANY)],
        out_specs=pl.BlockSpec(memory_space=pl.ANY),
        out_shape=jax.ShapeDtypeStruct((N_DEV,) + block.shape, block.dtype),
        scratch_shapes=_comm_sems(),
    )(block)


def _row_tile(r, cap):
    return max(d for d in range(16, cap + 1, 16) if r % d == 0)


def sum_adamw(parts, w, m, v, cap, carried=()):
    nl, r, c = w.shape
    tr = _row_tile(r, cap)
    steps = r // tr
    n = len(carried)
    c1 = 1.0 / (1.0 - ADAM_B1 ** ADAM_STEP)
    c2 = 1.0 / (1.0 - ADAM_B2 ** ADAM_STEP)

    def body(*refs):
        p_refs = refs[:nl]
        w_ref, m_ref, v_ref = refs[nl:nl + 3]
        g_ref, d_ref, nm_ref, nv_ref = refs[nl + 3 + n:nl + 7 + n]
        sems = refs[nl + 7 + 2 * n:]
        stages = [_exchange_steps(refs[nl + 3 + a], refs[nl + 7 + n + a], carried[a][1], *sems[3 * a:3 * a + 3])
                  for a in range(n)]
        layer, i = pl.program_id(0), pl.program_id(1)
        if n:
            pl.when((layer == 0) & (i == 0))(_each(stages, 0))

        def update(p_ref):
            g = p_ref[0].astype(F32)
            for k in range(1, N_DEV):
                g = g + p_ref[k].astype(F32)
            m_new = ADAM_B1 * m_ref[...] + (1.0 - ADAM_B1) * g
            v_new = ADAM_B2 * v_ref[...] + (1.0 - ADAM_B2) * (g * g)
            g_ref[...] = g
            nm_ref[...] = m_new
            nv_ref[...] = v_new
            d_ref[...] = -ADAM_LR * ((m_new * c1) / (jnp.sqrt(v_new * c2) + ADAM_EPS) + ADAM_WD * w_ref[...])

        for k in range(nl):
            pl.when(layer == k)(functools.partial(update, p_refs[k]))
        if n:
            pl.when((layer == nl - 1) & (i == steps - 1))(_each(stages, 1))

    def parts_spec(k):
        return pl.BlockSpec((N_DEV, tr, c), lambda l, i: (0, jnp.where(l == k, i, jnp.where(l < k, 0, steps - 1)), 0))

    blk = pl.BlockSpec((None, tr, c), lambda l, i: (l, i, 0))
    out = jax.ShapeDtypeStruct((nl, r, c), F32)
    hbm = pl.BlockSpec(memory_space=pl.ANY)
    return pl.pallas_call(
        body, name=f"sum_adamw_exchange{n}" if n else "sum_adamw", grid=(nl, steps),
        in_specs=[parts_spec(k) for k in range(nl)] + [blk, blk, blk] + [hbm] * n,
        out_specs=[blk, blk, blk, blk] + [hbm] * n,
        out_shape=[out, out, out, out]
        + [jax.ShapeDtypeStruct(src.shape if scatter else (N_DEV,) + src.shape, src.dtype) for src, scatter in carried],
        scratch_shapes=_comm_sems() * n,
        compiler_params=_cp(("arbitrary", "arbitrary")),
    )(*parts, w, m, v, *[src for src, _ in carried])


PACK_W = 1024
MIX_PIECES = (("w_out", D // N_DEV, D, False), ("w_uq", HEADS * (NOPE + ROPE) // N_DEV, Q_RANK, True),
              ("w_ukv", HEADS * (NOPE + V_DIM) // N_DEV, KV_RANK, True), ("conv", 16, PACK_W, False),
              ("w_in", IN_W // N_DEV, D, True))
FFN_PIECES = (("w_gate", D_FF // N_DEV, D, True), ("w_up", D_FF // N_DEV, D, True), ("w_down", D_FF // N_DEV, D, False))
def _packed_rows(rows, cols):
    return rows * cols // PACK_W


OFFSET = {}
for _pieces in (MIX_PIECES, FFN_PIECES):
    _off = 0
    for _name, _rows, _cols, _ in _pieces:
        assert _rows * _cols % PACK_W == 0
        OFFSET[_name] = _off
        _off += _packed_rows(_rows, _cols) + -_packed_rows(_rows, _cols) % 16
assert all(o % 16 == 0 for o in OFFSET.values())
assert [OFFSET[n] for n in ("w_gate", "w_up", "w_down")] == [0, FFN_SHARD, 2 * FFN_SHARD]
CONV_BITS = 3 * (CV_W // N_DEV) * 2


def _to_pack(shards, dtype, pieces, conv=None):
    nl = shards["w_in"].shape[0]
    parts = []
    for name, rows, cols, transposed in pieces:
        if name == "conv":
            if conv is None:
                a = jnp.zeros((nl, rows, PACK_W), dtype)
            else:
                bits = lax.bitcast_convert_type(conv.astype(F32), BF16).reshape(nl, CONV_BITS)
                a = jnp.pad(bits, ((0, 0), (0, rows * PACK_W - CONV_BITS))).reshape(nl, rows, PACK_W)
        else:
            a = shards[name].astype(dtype)
            a = (jnp.swapaxes(a, 1, 2) if transposed else a).reshape(nl, _packed_rows(rows, cols), PACK_W)
            a = jnp.pad(a, ((0, 0), (0, -a.shape[1] % 16), (0, 0)))
        parts.append(a)
    return jnp.concatenate(parts, axis=1)


def _from_pack(pack, pieces):
    out = {}
    for name, rows, cols, transposed in pieces:
        if name != "conv":
            a = pack[:, OFFSET[name]:OFFSET[name] + _packed_rows(rows, cols)].reshape(pack.shape[0], rows, cols)
            out[name] = jnp.swapaxes(a, 1, 2) if transposed else a
    return out


def _mix_weights(g):
    def rows(name):
        _, n, cols, _ = next(p for p in MIX_PIECES if p[0] == name)
        return g[:, OFFSET[name]:OFFSET[name] + _packed_rows(n, cols)].reshape(N_DEV, n, cols)

    w_in_t = rows("w_in").reshape(IN_W, D)
    w_in_p = jnp.concatenate([w_in_t[1184:], w_in_t[:672], jnp.zeros((96, D), BF16), w_in_t[672:1184]], axis=0)
    w_uq_p = jnp.pad(rows("w_uq"), ((0, 0), (0, HEAD_PAD - NOPE - ROPE), (0, 0))).reshape(HEADS * HEAD_PAD, Q_RANK)
    kv = rows("w_ukv")
    w_k = jnp.pad(kv[:, :NOPE], ((0, 0), (0, HEAD_PAD - NOPE), (0, 0))).reshape(HEADS * HEAD_PAD, KV_RANK)
    w_ukv_p = jnp.concatenate([w_k, kv[:, NOPE:].reshape(HEADS * V_DIM, KV_RANK)], axis=0)
    bits = rows("conv").reshape(N_DEV, -1)[:, :CONV_BITS].reshape(N_DEV, 3, CV_W // N_DEV, 2)
    conv_w = jnp.moveaxis(lax.bitcast_convert_type(bits, F32), 0, 1).reshape(3, CV_W)
    return dict(w_in=w_in_p, w_uq=w_uq_p, w_ukv=w_ukv_p, w_out=rows("w_out").reshape(D, D), conv_w=conv_w)


def _grad_chunks(full):
    d_in = full["w_in"]
    d_in = jnp.concatenate([d_in[768:768 + 672], d_in[1536:], d_in[:768]], axis=0)
    d_uq = full["w_uq"].reshape(HEADS, HEAD_PAD, Q_RANK)[:, :NOPE + ROPE]
    d_k = full["w_ukv"][:HEADS * HEAD_PAD].reshape(HEADS, HEAD_PAD, KV_RANK)[:, :NOPE]
    d_v = full["w_ukv"][HEADS * HEAD_PAD:].reshape(HEADS, V_DIM, KV_RANK)
    mats = dict(w_in=d_in, w_uq=d_uq, w_ukv=jnp.concatenate([d_k, d_v], axis=1), w_out=full["w_out"])
    parts = []
    for name, rows, cols, _ in MIX_PIECES:
        if name == "conv":
            parts.append(jnp.zeros((N_DEV, rows, PACK_W), BF16))
        else:
            a = mats[name].reshape(N_DEV, _packed_rows(rows, cols), PACK_W)
            parts.append(jnp.pad(a, ((0, 0), (0, -a.shape[1] % 16), (0, 0))))
    return jnp.concatenate(parts, axis=1)


SMALL = (("mix_pre_g", (D,)), ("mix_post_g", (D,)), ("ffn_pre_g", (D,)), ("ffn_post_g", (D,)), ("q_norm_g", (Q_RANK,)),
         ("kv_norm_g", (KV_RANK,)), ("sg_ln_g", (SG_W,)), ("sg_ln_b", (SG_W,)), ("w_sp", (4, CHUNK, CHUNK)),
         ("b_sp", (4, CHUNK)), ("out_norm_g", (D,)))
SMALL_ROWS = 576


def _pack_small(vals, nl):
    flat = jnp.concatenate([vals[name].reshape(nl, -1) for name, _ in SMALL] + [vals["conv_w"].reshape(nl, -1)], axis=1)
    return jnp.pad(flat, ((0, 0), (0, SMALL_ROWS * 128 - flat.shape[1]))).reshape(nl * SMALL_ROWS, 128)


def _unpack_small(pack, nl):
    flat = pack.reshape(nl, SMALL_ROWS * 128)
    out, off = {}, 0
    for name, shape in SMALL + (("conv_w", (3, CV_W)),):
        n = int(np.prod(shape))
        out[name] = flat[:, off:off + n].reshape((nl,) + shape)
        off += n
    return out


def _layer_fwd(x, lw, sp, tabs, consts, ffn_pack, next_mix_pack):
    ca, sb, sc = tabs
    z = pre_in_fwd(x, sp["mix_pre_g"], lw["w_in"])
    q, k, v = mla_proj_fwd(z, ca, sb, sc, sp["q_norm_g"], sp["kv_norm_g"], lw["w_uq"], lw["w_ukv"])
    ya, lse, lw["ffn"] = attn_fwd(q, k, v, (ffn_pack,))
    x1 = mix_fwd(x, z, ya, consts["gm"], sp["sg_ln_g"], sp["sg_ln_b"], sp["w_sp"], sp["bias"], lw["conv_w"],
                 sp["out_norm_g"], lw["w_out"], sp["mix_post_g"])
    x2, f, *mix_gathered = ffn_fwd(x1, sp["ffn_pre_g"], lw["ffn"], sp["ffn_post_g"], next_mix_pack)
    return x2, (x, z, q, k, v, ya, lse, x1, f), mix_gathered


def _layer_bwd(dx2, saved, lw, sp, tabs, consts, pending):
    ca, sb, sc = tabs
    x, z, q, k, v, ya, lse, x1, f = saved
    dx1, h2, dab, s, df, d_ffn_pre, d_ffn_post, *received = ffn_bwd(x1, f, dx2, sp["ffn_pre_g"], lw["ffn"], sp["ffn_post_g"],
                                                                    pending)
    ffn_chunks = atb_ffn_chunks(s, df, 2, atb_ffn_chunks(dab, h2, 0)).reshape(N_DEV, len(FFN_PIECES) * FFN_SHARD, D)
    dya, dyc, dzsg, d_w_out, d_mix_post, d_out_norm, d_lng, d_lnb, d_wsp, d_bias = mix_bwd(
        dx1, z, ya, consts["gm"], sp["sg_ln_g"], sp["sg_ln_b"], sp["w_sp"], sp["w_sp_t"], sp["bias"], lw["conv_w"],
        sp["out_norm_g"], lw["w_out"], sp["mix_post_g"])
    dzcv, d_cw = conv_bwd(dyc, z, lw["conv_w"])
    dq, dk, dv, got_ffn = attn_bwd(q, k, v, ya, lse, dya, ((ffn_chunks, True),))
    dzmla, d_w_uq, d_w_ukv, d_gq, d_gkv = mla_proj_bwd(dq, dk, dv, z, ca, sb, sc, sp["q_norm_g"], sp["kv_norm_g"],
                                                       lw["w_uq"], lw["w_ukv"])
    dx, d_w_in, d_mix_pre = pre_in_bwd(x, dx1, dzcv, dzmla, dzsg, sp["mix_pre_g"], lw["w_in"])
    mix_chunks = _grad_chunks(dict(w_in=d_w_in, w_uq=d_w_uq.T, w_ukv=d_w_ukv.T, w_out=d_w_out))
    d_bsp = d_bias[:, ::GROUP].T
    small = dict(mix_pre_g=d_mix_pre[0], mix_post_g=d_mix_post[0], ffn_pre_g=d_ffn_pre[0], ffn_post_g=d_ffn_post[0],
                 q_norm_g=d_gq[0], kv_norm_g=d_gkv[0], sg_ln_g=d_lng[0], sg_ln_b=d_lnb[0], w_sp=d_wsp, b_sp=d_bsp,
                 out_norm_g=d_out_norm[0], conv_w=d_cw[:3])
    small_pack = _pack_small({name: a[None] for name, a in small.items()}, 1)
    return dx, ((mix_chunks, True), (small_pack, False)), [got_ffn] + received


def kernel(x, positions, mix_pre_g, mix_post_g, ffn_pre_g, ffn_post_g, w_in, q_norm_g, w_uq, kv_norm_g, w_ukv, sg_ln_g, sg_ln_b, w_sp, b_sp, conv_w, out_norm_g, w_out, w_gate, w_up, w_down, loss_target, m_mix_pre_g, m_mix_post_g, m_ffn_pre_g, m_ffn_post_g, m_w_in, m_q_norm_g, m_w_uq, m_kv_norm_g, m_w_ukv, m_sg_ln_g, m_sg_ln_b, m_w_sp, m_b_sp, m_conv_w, m_out_norm_g, m_w_out, m_w_gate, m_w_up, m_w_down, v_mix_pre_g, v_mix_post_g, v_ffn_pre_g, v_ffn_post_g, v_w_in, v_q_norm_g, v_w_uq, v_kv_norm_g, v_w_ukv, v_sg_ln_g, v_sg_ln_b, v_w_sp, v_b_sp, v_conv_w, v_out_norm_g, v_w_out, v_w_gate, v_w_up, v_w_down):
    nl = w_in.shape[0]
    t = x.shape[1]
    w = dict(mix_pre_g=mix_pre_g, mix_post_g=mix_post_g, ffn_pre_g=ffn_pre_g, ffn_post_g=ffn_post_g, w_in=w_in,
             q_norm_g=q_norm_g, w_uq=w_uq, kv_norm_g=kv_norm_g, w_ukv=w_ukv, sg_ln_g=sg_ln_g, sg_ln_b=sg_ln_b, w_sp=w_sp,
             b_sp=b_sp, conv_w=conv_w, out_norm_g=out_norm_g, w_out=w_out, w_gate=w_gate, w_up=w_up, w_down=w_down)
    m = dict(mix_pre_g=m_mix_pre_g, mix_post_g=m_mix_post_g, ffn_pre_g=m_ffn_pre_g, ffn_post_g=m_ffn_post_g, w_in=m_w_in,
             q_norm_g=m_q_norm_g, w_uq=m_w_uq, kv_norm_g=m_kv_norm_g, w_ukv=m_w_ukv, sg_ln_g=m_sg_ln_g, sg_ln_b=m_sg_ln_b,
             w_sp=m_w_sp, b_sp=m_b_sp, conv_w=m_conv_w, out_norm_g=m_out_norm_g, w_out=m_w_out, w_gate=m_w_gate,
             w_up=m_w_up, w_down=m_w_down)
    v = dict(mix_pre_g=v_mix_pre_g, mix_post_g=v_mix_post_g, ffn_pre_g=v_ffn_pre_g, ffn_post_g=v_ffn_post_g, w_in=v_w_in,
             q_norm_g=v_q_norm_g, w_uq=v_w_uq, kv_norm_g=v_kv_norm_g, w_ukv=v_w_ukv, sg_ln_g=v_sg_ln_g, sg_ln_b=v_sg_ln_b,
             w_sp=v_w_sp, b_sp=v_b_sp, conv_w=v_conv_w, out_norm_g=v_out_norm_g, w_out=v_w_out, w_gate=v_w_gate,
             w_up=v_w_up, w_down=v_w_down)

    mix_pack = _to_pack(w, BF16, MIX_PIECES, conv=w["conv_w"])
    ffn_pack = _to_pack(w, BF16, FFN_PIECES)
    consts = dict(gm=jnp.asarray(np.kron(np.eye(SG_W // GROUP), np.full((GROUP, GROUP), 1.0 / GROUP)), BF16))
    smalls = []
    for l in range(nl):
        sp = {name: w[name][l].reshape(1, -1) for name, shape in SMALL if len(shape) == 1}
        sp["w_sp"] = w["w_sp"][l]
        sp["w_sp_t"] = jnp.swapaxes(w["w_sp"][l], 1, 2)
        sp["bias"] = jnp.repeat(w["b_sp"][l].T, GROUP, axis=1)
        smalls.append(sp)
    inv_freq = 1.0 / (ROPE_THETA ** (jnp.arange(0, ROPE // 2, dtype=F32) / (ROPE // 2)))
    inv = jnp.zeros((1, HEAD_PAD), F32).at[0, NOPE:NOPE + ROPE].set(jnp.concatenate([inv_freq, inv_freq]))
    tabs = rope_tables(positions.reshape(t, 1).astype(F32), inv)

    h = x[0]
    saved, layers = [], []
    mix_gathered = [all_gather(mix_pack[0])]
    for l in range(nl):
        layers.append(_mix_weights(mix_gathered[0]))
        h, s, mix_gathered = _layer_fwd(h, layers[l], smalls[l], tabs, consts, ffn_pack[l],
                                        (mix_pack[l + 1],) if l + 1 < nl else ())
        saved.append(s)
    sq, dh = loss_head(h, loss_target[0])
    loss = lax.psum(0.5 * sq[0, 0] / D, ("x", "y", "c"))

    got_ffn, got_mix, got_small = [None] * nl, [None] * nl, [None] * nl
    pending = ()
    for l in reversed(range(nl)):
        dh, new_pending, received = _layer_bwd(dh, saved[l], layers[l], smalls[l], tabs, consts, pending)
        got_ffn[l] = received[0]
        if pending:
            got_mix[l + 1], got_small[l + 1] = received[1:]
        pending = new_pending

    me = 4 * lax.axis_index("x") + 2 * lax.axis_index("y") + lax.axis_index("c")
    *ffn_new, got_mix[0], got_small[0] = sum_adamw(got_ffn, *[_to_pack(d, F32, FFN_PIECES) for d in (w, m, v)], 176,
                                                   carried=pending)
    mix_new = sum_adamw(got_mix, *[_to_pack(d, F32, MIX_PIECES) for d in (w, m, v)], 208)
    got_small = jnp.concatenate(got_small, axis=1)
    g_big, d_big, m_big, v_big = [{**_from_pack(a, FFN_PIECES), **_from_pack(b, MIX_PIECES)}
                                  for a, b in zip(ffn_new, mix_new)]

    def full_conv(a):
        return lax.dynamic_update_slice(jnp.zeros((nl, 3, CV_W), F32), a, (0, 0, me * (CV_W // N_DEV)))

    def small_pack(d):
        return _pack_small({**{name: d[name] for name, _ in SMALL}, "conv_w": full_conv(d["conv_w"])}, nl)

    g_small, d_small, m_small, v_small = [_unpack_small(p[0], nl) for p in
                                          sum_adamw([got_small], small_pack(w)[None], small_pack(m)[None],
                                                    small_pack(v)[None], 1152)]
    outs = []
    for big, small in ((g_big, g_small), (d_big, d_small), (m_big, m_small), (v_big, v_small)):
        for name in w:
            if name == "conv_w":
                outs.append(lax.dynamic_slice(small[name], (0, 0, me * (CV_W // N_DEV)), (nl, 3, CV_W // N_DEV)))
            elif name in small:
                outs.append(small[name])
            else:
                outs.append(big[name])
    return (loss, dh[None], *outs)
```

```python
import functools

import jax
import jax.numpy as jnp
import numpy as np
from jax import lax
from jax.experimental import pallas as pl
from jax.experimental.pallas import tpu as pltpu

F32 = jnp.float32
BF16 = jnp.bfloat16

D = 1024
Q_RANK = 384
KV_RANK = 256
ROPE = 32
HEADS = 8
NOPE = 64
V_DIM = 64
HEAD_PAD = 128
SG_W = 256
CV_W = 256
CHUNK = 128
GROUP = 64
D_FF = 2816
IN_W = 1952
Z_W = 2048
Z_CV, Z_MLA, Z_SG = 0, 768, 1536
EPS = 1e-6
ROPE_THETA = 10000.0
SCALE = (NOPE + ROPE) ** -0.5
LOG2E = 1.4426950408889634
SCALE_LOG2E = SCALE * LOG2E
NEG = -1e30
N_DEV = 8

ADAM_LR, ADAM_B1, ADAM_B2, ADAM_EPS, ADAM_WD, ADAM_STEP = 0.001, 0.9, 0.999, 1e-08, 0.01, 10

VMEM_LIMIT = 56 * 1024 * 1024

TM = 512
TM_ROW = 1024
TM_FFN = 256
FFN_SLAB = 256
HALO = 16
TQ = 512
FWD_UNROLL = 2
FWD_HEADS = 2
FWD_ROW_SPLIT = 2
TT = 2048


def _cp(sem, vmem=VMEM_LIMIT):
    return pltpu.CompilerParams(dimension_semantics=sem, vmem_limit_bytes=vmem)


def _whole():
    return pl.BlockSpec(memory_space=pltpu.VMEM)


def _mm(a, b):
    return jnp.dot(a, b, preferred_element_type=F32)


def _mm_nt(a, b):
    return lax.dot_general(a, b, (((1,), (1,)), ((), ())), preferred_element_type=F32)


def _mm_tn(a, b):
    return lax.dot_general(a, b, (((0,), (0,)), ((), ())), preferred_element_type=F32)


def _rms_fwd(x, g):
    r = lax.rsqrt(jnp.mean(x * x, axis=-1, keepdims=True) + EPS)
    xh = x * r
    return xh * g, xh, r


def _rms_bwd(xh, r, g, dy):
    dxh = dy * g
    dx = r * (dxh - xh * jnp.mean(dxh * xh, axis=-1, keepdims=True))
    dg = jnp.sum(dy * xh, axis=0, keepdims=True)
    return dx, dg


def _gmean(v, gm):
    hi = v.astype(BF16)
    lo = (v - hi.astype(F32)).astype(BF16)
    return _mm(hi, gm) + _mm(lo, gm)


def _gelu(x):
    c = np.float32(np.sqrt(2.0 / np.pi))
    u = c * (x + 0.044715 * x * x * x)
    t = jnp.tanh(u)
    return 0.5 * x * (1.0 + t), t


def _gelu_grad(x, t):
    c = np.float32(np.sqrt(2.0 / np.pi))
    return 0.5 * (1.0 + t) + 0.5 * x * (1.0 - t * t) * c * (1.0 + 3.0 * 0.044715 * x * x)


def _rope(t, ca, sb, sc):
    return t * ca + pltpu.roll(t, HEAD_PAD - 16, 1) * sb + pltpu.roll(t, 16, 1) * sc


def _rope_t(dt, ca, sb, sc):
    return dt * ca + pltpu.roll(dt * sb, 16, 1) + pltpu.roll(dt * sc, HEAD_PAD - 16, 1)


def _shift_down(y, k, head):
    out = pltpu.roll(y, k, 0)
    row = lax.broadcasted_iota(jnp.int32, y.shape, 0)
    for j in range(k):
        out = jnp.where(row == j, head[head.shape[0] - k + j:head.shape[0] - k + j + 1, :], out)
    return out


def _shift_up(y, k, tail):
    n = y.shape[0]
    out = pltpu.roll(y, n - k, 0)
    row = lax.broadcasted_iota(jnp.int32, y.shape, 0)
    for j in range(k):
        out = jnp.where(row == n - k + j, tail[j:j + 1, :], out)
    return out


def rope_tables(pos, inv):
    t = pos.shape[0]
    tm = min(TM, t)

    def body(pos_ref, inv_ref, ca_ref, sb_ref, sc_ref):
        ang = pos_ref[...] * inv_ref[...]
        c = jnp.cos(ang)
        s = jnp.sin(ang)
        lane = lax.broadcasted_iota(jnp.int32, ang.shape, 1)
        ca_ref[...] = jnp.where(lane < NOPE, 1.0, jnp.where(lane < NOPE + ROPE, c, 0.0))
        sb_ref[...] = jnp.where((lane >= NOPE) & (lane < NOPE + 16), -s, 0.0)
        sc_ref[...] = jnp.where((lane >= NOPE + 16) & (lane < NOPE + ROPE), s, 0.0)

    out = jax.ShapeDtypeStruct((t, HEAD_PAD), F32)
    blk = pl.BlockSpec((tm, HEAD_PAD), lambda i: (i, 0))
    return pl.pallas_call(
        body, name="rope_tables", grid=(t // tm,),
        in_specs=[pl.BlockSpec((tm, 1), lambda i: (i, 0)), pl.BlockSpec((1, HEAD_PAD), lambda i: (0, 0))],
        out_specs=[blk, blk, blk], out_shape=[out, out, out],
        compiler_params=_cp(("parallel",)),
    )(pos, inv)


def pre_in_fwd(x, g, w):
    t = x.shape[0]
    tm = min(TM_ROW, t)

    def body(x_ref, g_ref, w_ref, z_ref):
        h, _, _ = _rms_fwd(x_ref[...], g_ref[...])
        z_ref[...] = _mm_nt(h.astype(BF16), w_ref[...]).astype(BF16)

    return pl.pallas_call(
        body, name="pre_in_fwd", grid=(t // tm,),
        in_specs=[pl.BlockSpec((tm, D), lambda i: (i, 0)), _whole(), _whole()],
        out_specs=pl.BlockSpec((tm, Z_W), lambda i: (i, 0)),
        out_shape=jax.ShapeDtypeStruct((t, Z_W), BF16),
        compiler_params=_cp(("parallel",)),
    )(x, g, w)


def mla_proj_fwd(z, ca, sb, sc, gq, gkv, wuq, wukv):
    t = z.shape[0]
    tm = min(TM_ROW, t)

    def body(z_ref, ca_ref, sb_ref, sc_ref, gq_ref, gkv_ref, wuq_ref, wukv_ref, q_ref, k_ref, v_ref):
        z = z_ref[...].astype(F32)
        ca, sb, sc = ca_ref[...], sb_ref[...], sc_ref[...]
        cq, _, _ = _rms_fwd(z[:, :Q_RANK], gq_ref[...])
        ckv, _, _ = _rms_fwd(z[:, Q_RANK:Q_RANK + KV_RANK], gkv_ref[...])
        q = _mm_nt(cq.astype(BF16), wuq_ref[...])
        kv = _mm_nt(ckv.astype(BF16), wukv_ref[...])
        kr = _rope(pltpu.roll(z[:, Q_RANK + KV_RANK:], NOPE, 1), ca, sb, sc)
        for h in range(HEADS):
            lanes = slice(h * HEAD_PAD, (h + 1) * HEAD_PAD)
            q_ref[:, lanes] = _rope(q[:, lanes], ca, sb, sc).astype(BF16)
            k_ref[:, lanes] = (kv[:, lanes] + kr).astype(BF16)
        v_ref[...] = kv[:, HEADS * HEAD_PAD:].astype(BF16)

    tab = pl.BlockSpec((tm, HEAD_PAD), lambda i: (i, 0))
    return pl.pallas_call(
        body, name="mla_proj_fwd", grid=(t // tm,),
        in_specs=[pl.BlockSpec((tm, 768), lambda i: (i, 1)), tab, tab, tab, _whole(), _whole(), _whole(), _whole()],
        out_specs=[pl.BlockSpec((tm, HEADS * HEAD_PAD), lambda i: (i, 0)),
                   pl.BlockSpec((tm, HEADS * HEAD_PAD), lambda i: (i, 0)),
                   pl.BlockSpec((tm, HEADS * V_DIM), lambda i: (i, 0))],
        out_shape=[jax.ShapeDtypeStruct((t, HEADS * HEAD_PAD), BF16),
                   jax.ShapeDtypeStruct((t, HEADS * HEAD_PAD), BF16),
                   jax.ShapeDtypeStruct((t, HEADS * V_DIM), BF16)],
        compiler_params=_cp(("parallel",)),
    )(z, ca, sb, sc, gq, gkv, wuq, wukv)


def _each(stages, k):
    def run():
        for stage in stages:
            stage[k]()
    return run


def attn_fwd(q, k, v, carried=()):
    t = q.shape[0]
    tq = min(TQ, t)
    nq = t // tq
    hs = FWD_HEADS
    last_pair = HEADS // hs - 1
    n = len(carried)

    def body(*refs):
        q_ref, k_ref, v_ref = refs[:3]
        o_ref, lse_ref = refs[3 + n:5 + n]
        sems = refs[5 + 2 * n:]
        stages = [_gather_steps(refs[3 + a], refs[5 + n + a], *sems[3 * a:3 * a + 3]) for a in range(n)]
        if n:
            pair = pl.program_id(0)
            pl.when((pair == 0) & (pl.program_id(1) == 0))(_each(stages, 0))
            pl.when((pair == last_pair) & (pl.program_id(1) == 0))(_each(stages, 1))
        i = pl.program_id(1)
        tr = tq // FWD_ROW_SPLIT
        row = lax.broadcasted_iota(jnp.int32, (tr, tq), 0)
        col = lax.broadcasted_iota(jnp.int32, (tr, tq), 1)
        head_lanes = [slice(h * HEAD_PAD, (h + 1) * HEAD_PAD) for h in range(hs)]
        pair_lanes = [slice(p * 2 * V_DIM, (p + 1) * 2 * V_DIM) for p in range(hs // 2)]
        chains = [(h, r) for h in range(hs) for r in range(FWD_ROW_SPLIT)]

        def step(j, carry, masked):
            start = pl.multiple_of(j * tq, tq)
            out = []
            for c, (h, r) in enumerate(chains):
                m, l, acc = carry[c]
                s = _mm_nt(q_ref[r * tr:(r + 1) * tr, head_lanes[h]], k_ref[pl.ds(start, tq), head_lanes[h]])
                if masked:
                    s = jnp.where(col <= row + r * tr, s, NEG)
                m_new = jnp.maximum(m, jnp.max(s, axis=-1, keepdims=True))
                p = jnp.exp2((s - m_new) * SCALE_LOG2E)
                alpha = jnp.exp2((m - m_new) * SCALE_LOG2E)
                l = alpha * l + jnp.sum(p, axis=-1, keepdims=True)
                acc = alpha * acc + _mm(p.astype(BF16), v_ref[pl.ds(start, tq), pair_lanes[h // 2]])
                out.append((m_new, l, acc))
            return tuple(out)

        init = (jnp.full((tr, 1), NEG, F32), jnp.zeros((tr, 1), F32), jnp.zeros((tr, 2 * V_DIM), F32))
        def trip(j, c):
            for u in range(FWD_UNROLL):
                c = step(FWD_UNROLL * j + u, c, False)
            return c

        carry = lax.fori_loop(0, i // FWD_UNROLL, trip, (init,) * len(chains))
        carry = lax.fori_loop(i - i % FWD_UNROLL, i, lambda j, c: step(j, c, False), carry)
        outs = {}
        for (h, r), (m, l, acc) in zip(chains, step(i, carry, True)):
            outs[h, r] = acc / l
            lse_ref[r * tr:(r + 1) * tr, head_lanes[h]] = jnp.broadcast_to(m * SCALE + jnp.log(l), (tr, HEAD_PAD))
        lane = lax.broadcasted_iota(jnp.int32, (tr, 2 * V_DIM), 1)
        for p in range(hs // 2):
            for r in range(FWD_ROW_SPLIT):
                o_ref[r * tr:(r + 1) * tr, pair_lanes[p]] = jnp.where(lane < V_DIM, outs[2 * p, r], outs[2 * p + 1, r])
        if n:
            pl.when((pl.program_id(0) == last_pair) & (i == nq - 1))(_each(stages, 2))

    hbm = pl.BlockSpec(memory_space=pl.ANY)
    return pl.pallas_call(
        body, name=f"attn_fwd_gather{n}" if n else "attn_fwd", grid=(HEADS // hs, nq),
        in_specs=[pl.BlockSpec((tq, hs * HEAD_PAD), lambda p, i: (i, p)),
                  pl.BlockSpec((t, hs * HEAD_PAD), lambda p, i: (0, p)),
                  pl.BlockSpec((t, hs * V_DIM), lambda p, i: (0, p))] + [hbm] * n,
        out_specs=[pl.BlockSpec((tq, hs * V_DIM), lambda p, i: (i, p)),
                   pl.BlockSpec((tq, hs * HEAD_PAD), lambda p, i: (i, p))] + [hbm] * n,
        out_shape=[jax.ShapeDtypeStruct((t, HEADS * V_DIM), F32), jax.ShapeDtypeStruct((t, HEADS * HEAD_PAD), F32)]
        + [jax.ShapeDtypeStruct((N_DEV,) + c.shape, c.dtype) for c in carried],
        scratch_shapes=_comm_sems() * n,
        compiler_params=_cp(("arbitrary", "arbitrary") if n else ("parallel", "parallel")),
    )(q, k, v, *carried)


def _sgu_fwd(zsg, gm, lng, lnb, wc_ref, bias, mixed_ref):
    uv, th = _gelu(zsg)
    u, v0 = uv[:, :SG_W], uv[:, SG_W:]
    vc = v0 - _gmean(v0, gm)
    r = lax.rsqrt(_gmean(vc * vc, gm) + EPS)
    vh = vc * r
    v = vh * lng + lnb
    lane = lax.broadcasted_iota(jnp.int32, (CHUNK, SG_W), 1)
    for c in range(zsg.shape[0] // CHUNK):
        rows = slice(c * CHUNK, (c + 1) * CHUNK)
        vb = v[rows].astype(BF16)
        mixed = bias
        for g in range(SG_W // GROUP):
            mixed = mixed + jnp.where(lane // GROUP == g, _mm(wc_ref[g], vb), 0.0)
        mixed_ref[rows, :] = mixed
    return u, v, vh, r, th


def _conv_fwd(zcv, halo, first, cw):
    gb, gc, hh = zcv[:, :CV_W], zcv[:, CV_W:2 * CV_W], zcv[:, 2 * CV_W:]
    y = gc * hh
    yh = jnp.where(first, 0.0, halo[:, CV_W:2 * CV_W] * halo[:, 2 * CV_W:])
    y1 = _shift_down(y, 1, yh)
    y2 = _shift_down(y, 2, yh)
    conv = y2 * cw[0:1, :] + y1 * cw[1:2, :] + y * cw[2:3, :]
    return gb * conv, conv, y, y1, y2


def _tril_bf16(w_ref, g):
    row = lax.broadcasted_iota(jnp.int32, (CHUNK, CHUNK), 0)
    col = lax.broadcasted_iota(jnp.int32, (CHUNK, CHUNK), 1)
    return jnp.where(col <= row, w_ref[g], 0.0).astype(BF16)


def mix_fwd(x, z, ya, gm, lng, lnb, wsp, bias, cw, gout, wout, gpost):
    t = x.shape[0]
    tm = min(TM_ROW, t)

    def body(x_ref, zcv_ref, halo_ref, zsg_ref, ya_ref, gm_ref, lng_ref, lnb_ref, wsp_ref, bias_ref, cw_ref,
             gout_ref, wout_ref, gpost_ref, x1_ref, wc_ref, mixed_ref):
        i = pl.program_id(0)
        for g in range(SG_W // GROUP):
            wc_ref[g] = _tril_bf16(wsp_ref, g)
        u, _, _, _, _ = _sgu_fwd(zsg_ref[...].astype(F32), gm_ref[...], lng_ref[...], lnb_ref[...], wc_ref, bias_ref[...],
                                 mixed_ref)
        yb = u * mixed_ref[...]
        yc, _, _, _, _ = _conv_fwd(zcv_ref[...].astype(F32), halo_ref[...].astype(F32), i == 0, cw_ref[...])
        gout = gout_ref[...]
        na, _, _ = _rms_fwd(ya_ref[...], gout[:, :512])
        nb, _, _ = _rms_fwd(yb, gout[:, 512:768])
        nc, _, _ = _rms_fwd(yc, gout[:, 768:])
        mix = jnp.concatenate([na, nb, nc], axis=1).astype(BF16)
        o, _, _ = _rms_fwd(_mm(mix, wout_ref[...]), gpost_ref[...])
        x1_ref[...] = x_ref[...] + o

    return pl.pallas_call(
        body, name="mix_fwd", grid=(t // tm,),
        in_specs=[pl.BlockSpec((tm, D), lambda i: (i, 0)),
                  pl.BlockSpec((tm, 768), lambda i: (i, 0)),
                  pl.BlockSpec((HALO, 768), lambda i: (jnp.maximum(i * (tm // HALO) - 1, 0), 0)),
                  pl.BlockSpec((tm, 512), lambda i: (i, 3)),
                  pl.BlockSpec((tm, 512), lambda i: (i, 0)),
                  _whole(), _whole(), _whole(), _whole(), _whole(), _whole(), _whole(), _whole(), _whole()],
        out_specs=pl.BlockSpec((tm, D), lambda i: (i, 0)),
        out_shape=jax.ShapeDtypeStruct((t, D), F32),
        scratch_shapes=[pltpu.VMEM((SG_W // GROUP, CHUNK, CHUNK), BF16), pltpu.VMEM((tm, SG_W), F32)],
        compiler_params=_cp(("arbitrary",)),
    )(x, z, z, z, ya, gm, lng, lnb, wsp, bias, cw, gout, wout, gpost)


def _sigmoid(a):
    return 1.0 / (1.0 + jnp.exp(-a))


FFN_SHARD = D_FF // N_DEV


def _load_ffn_weights(g_ref, wgu_ref, wd_ref, sems):
    copies = []
    for j in range(N_DEV):
        for p, (dst, base) in enumerate(((wgu_ref, 0), (wgu_ref, D_FF), (wd_ref, 0))):
            copies.append(pltpu.make_async_copy(g_ref.at[j, pl.ds(p * FFN_SHARD, FFN_SHARD)],
                                                dst.at[pl.ds(base + j * FFN_SHARD, FFN_SHARD)], sems.at[3 * j + p]))
    for cp in copies:
        cp.start()
    for cp in copies:
        cp.wait()


def _ffn_weight_scratch():
    return [pltpu.VMEM((2 * D_FF, D), BF16), pltpu.VMEM((D_FF, D), BF16), pltpu.SemaphoreType.DMA((3 * N_DEV,))]


def ffn_fwd(x1, gpre, gathered, gpost, carried=()):
    t = x1.shape[0]
    tm = min(TM_FFN, t)
    steps = t // tm
    n = len(carried)

    def body(*refs):
        x_ref, gpre_ref, g_ref, gpost_ref = refs[:4]
        x2_ref, f_ref = refs[4 + n:6 + n]
        wgu_ref, wd_ref, sems = refs[6 + 2 * n:9 + 2 * n]
        comm_sems = refs[9 + 2 * n:]
        stages = [_gather_steps(refs[4 + a], refs[6 + n + a], *comm_sems[3 * a:3 * a + 3]) for a in range(n)]
        if n:
            pl.when(pl.program_id(0) == 0)(_each(stages, 0))
            pl.when(pl.program_id(0) == (3 * steps) // 4)(_each(stages, 1))

        @pl.when(pl.program_id(0) == 0)
        def _():
            _load_ffn_weights(g_ref, wgu_ref, wd_ref, sems)

        x = x_ref[...]
        h, _, _ = _rms_fwd(x, gpre_ref[...])
        ab = _mm_nt(h.astype(BF16), wgu_ref[...])
        a, b = ab[:, :D_FF], ab[:, D_FF:]
        s = a * _sigmoid(a) * b
        f = _mm(s.astype(BF16), wd_ref[...])
        f_ref[...] = f
        x2_ref[...] = x + _rms_fwd(f, gpost_ref[...])[0]
        if n:
            pl.when(pl.program_id(0) == steps - 1)(_each(stages, 2))

    row = pl.BlockSpec((tm, D), lambda i: (i, 0))
    hbm = pl.BlockSpec(memory_space=pl.ANY)
    return pl.pallas_call(
        body, name=f"ffn_fwd_gather{n}" if n else "ffn_fwd", grid=(steps,),
        in_specs=[row, _whole(), hbm, _whole()] + [hbm] * n,
        out_specs=[row, row] + [hbm] * n,
        out_shape=[jax.ShapeDtypeStruct((t, D), F32), jax.ShapeDtypeStruct((t, D), F32)]
        + [jax.ShapeDtypeStruct((N_DEV,) + c.shape, c.dtype) for c in carried],
        scratch_shapes=_ffn_weight_scratch() + _comm_sems() * n,
        compiler_params=_cp(("arbitrary",)),
    )(x1, gpre, gathered, gpost, *carried)


def loss_head(y, target):
    t = y.shape[0]
    tm = min(TM, t)

    def body(y_ref, t_ref, loss_ref, dy_ref):
        @pl.when(pl.program_id(0) == 0)
        def _():
            loss_ref[...] = jnp.zeros_like(loss_ref)

        e = y_ref[...] - t_ref[...]
        dy_ref[...] = e * (1.0 / D)
        loss_ref[...] += jnp.sum(jnp.sum(e * e, axis=-1, keepdims=True), axis=0, keepdims=True)

    return pl.pallas_call(
        body, name="loss_head", grid=(t // tm,),
        in_specs=[pl.BlockSpec((tm, D), lambda i: (i, 0)), pl.BlockSpec((tm, D), lambda i: (i, 0))],
        out_specs=[pl.BlockSpec((1, 128), lambda i: (0, 0)), pl.BlockSpec((tm, D), lambda i: (i, 0))],
        out_shape=[jax.ShapeDtypeStruct((1, 128), F32), jax.ShapeDtypeStruct((t, D), F32)],
        compiler_params=_cp(("arbitrary",)),
    )(y, target)


def _acc(ref, first, val):
    @pl.when(first)
    def _():
        ref[...] = val

    @pl.when(jnp.logical_not(first))
    def _():
        ref[...] += val


def ffn_bwd(x1, f, dx2, gpre, gathered, gpost, carried=()):
    t = x1.shape[0]
    tm = min(TM_FFN, t)
    steps = t // tm
    n = len(carried)

    def body(*refs):
        x_ref, f_ref, dx2_ref, gpre_ref, g_ref, gpost_ref = refs[:6]
        dx1_ref, h_ref, dab_ref, s_ref, df_ref, dgpre_ref, dgpost_ref = refs[6 + n:13 + n]
        ab_ref, ds_ref, wgu_ref, wd_ref, sems = refs[13 + 2 * n:18 + 2 * n]
        comm_sems = refs[18 + 2 * n:]
        stages = [_exchange_steps(refs[6 + a], refs[13 + n + a], carried[a][1], *comm_sems[3 * a:3 * a + 3])
                  for a in range(n)]
        first = pl.program_id(0) == 0
        if n:
            pl.when(first)(_each(stages, 0))

        @pl.when(first)
        def _():
            _load_ffn_weights(g_ref, wgu_ref, wd_ref, sems)

        dx2 = dx2_ref[...]
        gpre, gpost = gpre_ref[...], gpost_ref[...]
        h, xh, rx = _rms_fwd(x_ref[...], gpre)
        h_ref[...] = h.astype(BF16)
        ab_ref[...] = _mm_nt(h_ref[...], wgu_ref[...])
        for c in range(0, D_FF, FFN_SLAB):
            a, b = ab_ref[:, c:c + FFN_SLAB], ab_ref[:, D_FF + c:D_FF + c + FFN_SLAB]
            s_ref[:, c:c + FFN_SLAB] = (a * _sigmoid(a) * b).astype(BF16)
        _, fh, rf = _rms_fwd(f_ref[...], gpost)
        df, dgpost = _rms_bwd(fh, rf, gpost, dx2)
        df_ref[...] = df.astype(BF16)
        ds_ref[...] = _mm_nt(df_ref[...], wd_ref[...])
        for c in range(0, D_FF, FFN_SLAB):
            a, b = ab_ref[:, c:c + FFN_SLAB], ab_ref[:, D_FF + c:D_FF + c + FFN_SLAB]
            ds = ds_ref[:, c:c + FFN_SLAB]
            sg = _sigmoid(a)
            dab_ref[:, c:c + FFN_SLAB] = (ds * b * (sg * (1.0 + a * (1.0 - sg)))).astype(BF16)
            dab_ref[:, D_FF + c:D_FF + c + FFN_SLAB] = (ds * (a * sg)).astype(BF16)
        dx, dgpre = _rms_bwd(xh, rx, gpre, _mm(dab_ref[...], wgu_ref[...]))
        dx1_ref[...] = dx2 + dx
        _acc(dgpre_ref, first, dgpre)
        _acc(dgpost_ref, first, dgpost)
        if n:
            pl.when(pl.program_id(0) == steps - 1)(_each(stages, 1))

    row = lambda w: pl.BlockSpec((tm, w), lambda i: (i, 0))
    vec = pl.BlockSpec((1, D), lambda i: (0, 0))
    hbm = pl.BlockSpec(memory_space=pl.ANY)
    return pl.pallas_call(
        body, name=f"ffn_bwd_exchange{n}" if n else "ffn_bwd", grid=(steps,),
        in_specs=[row(D), row(D), row(D), _whole(), hbm, _whole()] + [hbm] * n,
        out_specs=[row(D), row(D), row(2 * D_FF), row(D_FF), row(D), vec, vec] + [hbm] * n,
        out_shape=[jax.ShapeDtypeStruct((t, D), F32), jax.ShapeDtypeStruct((t, D), BF16),
                   jax.ShapeDtypeStruct((t, 2 * D_FF), BF16), jax.ShapeDtypeStruct((t, D_FF), BF16),
                   jax.ShapeDtypeStruct((t, D), BF16), jax.ShapeDtypeStruct((1, D), F32),
                   jax.ShapeDtypeStruct((1, D), F32)]
        + [jax.ShapeDtypeStruct(src.shape if scatter else (N_DEV,) + src.shape, src.dtype) for src, scatter in carried],
        scratch_shapes=[pltpu.VMEM((tm, 2 * D_FF), F32), pltpu.VMEM((tm, D_FF), F32)] + _ffn_weight_scratch()
        + _comm_sems() * n,
        compiler_params=_cp(("arbitrary",)),
    )(x1, f, dx2, gpre, gathered, gpost, *[src for src, _ in carried])


FFN_TILE_SHARDS = 4


def atb_ffn_chunks(a, b, first_piece, chunks=None):
    t, k = a.shape
    tt = min(TT, t)
    tk = FFN_TILE_SHARDS * FFN_SHARD
    steps = t // tt
    per_piece = N_DEV // FFN_TILE_SHARDS

    def body(*refs):
        a_ref, b_ref, o_ref, acc_ref = refs[0], refs[1], refs[-2], refs[-1]
        i = pl.program_id(1)
        _acc(acc_ref, i == 0, _mm_tn(a_ref[...], b_ref[...]))

        @pl.when(i == steps - 1)
        def _():
            for d in range(FFN_TILE_SHARDS):
                o_ref[d, 0] = acc_ref[d * FFN_SHARD:(d + 1) * FFN_SHARD, :].astype(BF16)

    hbm = pl.BlockSpec(memory_space=pl.ANY)
    return pl.pallas_call(
        body, name="atb_ffn_chunks", grid=(k // tk, steps),
        in_specs=[pl.BlockSpec((tt, tk), lambda j, i: (i, j)), pl.BlockSpec((tt, D), lambda j, i: (i, 0))]
        + ([] if chunks is None else [hbm]),
        out_specs=pl.BlockSpec((FFN_TILE_SHARDS, 1, FFN_SHARD, D),
                               lambda j, i: (j % per_piece, first_piece + j // per_piece, 0, 0)),
        out_shape=jax.ShapeDtypeStruct((N_DEV, len(FFN_PIECES), FFN_SHARD, D), BF16),
        input_output_aliases={} if chunks is None else {2: 0},
        scratch_shapes=[pltpu.VMEM((tk, D), F32)],
        compiler_params=_cp(("parallel", "arbitrary")),
    )(a, b, *([] if chunks is None else [chunks]))


def mix_bwd(dx1, z, ya, gm, lng, lnb, wsp, wspt, bias, cw, gout, wout, gpost):
    t = dx1.shape[0]
    tm = min(TM, t)
    steps = t // tm
    ng = SG_W // GROUP

    def body(dx1_ref, zcv_ref, halo_ref, zsg_ref, ya_ref, gm_ref, lng_ref, lnb_ref, wsp_ref, wspt_ref, bias_ref,
             cw_ref, gout_ref, wout_ref, gpost_ref,
             dya_ref, dyc_ref, dzsg_ref, dwout_ref, dgpost_ref, dgout_ref, dlng_ref, dlnb_ref, dwsp_ref,
             dbias_ref, wc_ref, wct_ref, mixed_ref, dv_ref, dwout_acc):
        i = pl.program_id(0)
        first = i == 0
        gm = gm_ref[...]
        for g in range(ng):
            wc_ref[g] = _tril_bf16(wsp_ref, g)
            wct_ref[g] = jnp.where(
                lax.broadcasted_iota(jnp.int32, (CHUNK, CHUNK), 0) <= lax.broadcasted_iota(jnp.int32, (CHUNK, CHUNK), 1),
                wspt_ref[g], 0.0).astype(BF16)
        zsg = zsg_ref[...].astype(F32)
        lng = lng_ref[...]
        u, v, vh, r, th = _sgu_fwd(zsg, gm, lng, lnb_ref[...], wc_ref, bias_ref[...], mixed_ref)
        mixed = mixed_ref[...]
        yb = u * mixed
        yc, _, _, _, _ = _conv_fwd(zcv_ref[...].astype(F32), halo_ref[...].astype(F32), first, cw_ref[...])
        gout, gpost = gout_ref[...], gpost_ref[...]
        ga, gb_, gc_ = gout[:, :512], gout[:, 512:768], gout[:, 768:]
        na, yah, ra = _rms_fwd(ya_ref[...], ga)
        nb, ybh, rb = _rms_fwd(yb, gb_)
        nc, ych, rc = _rms_fwd(yc, gc_)
        mix = jnp.concatenate([na, nb, nc], axis=1).astype(BF16)
        _, oh, ro = _rms_fwd(_mm(mix, wout_ref[...]), gpost)
        do, dgpost = _rms_bwd(oh, ro, gpost, dx1_ref[...])
        dob = do.astype(BF16)
        dmix = _mm_nt(dob, wout_ref[...])
        dya, dga = _rms_bwd(yah, ra, ga, dmix[:, :512])
        dyb, dgb = _rms_bwd(ybh, rb, gb_, dmix[:, 512:768])
        dyc, dgc = _rms_bwd(ych, rc, gc_, dmix[:, 768:])
        dya_ref[...] = dya
        dyc_ref[...] = dyc
        _acc(dwout_acc, first, _mm_tn(mix, dob))

        @pl.when(i == steps - 1)
        def _():
            dwout_ref[...] = dwout_acc[...].astype(BF16)

        _acc(dgpost_ref, first, dgpost)
        _acc(dgout_ref, first, jnp.concatenate([dga, dgb, dgc], axis=1))
        du = dyb * mixed
        dmixed = dyb * u
        lane = lax.broadcasted_iota(jnp.int32, (CHUNK, SG_W), 1)
        row = lax.broadcasted_iota(jnp.int32, (CHUNK, CHUNK), 0)
        col = lax.broadcasted_iota(jnp.int32, (CHUNK, CHUNK), 1)
        dbias = jnp.zeros((CHUNK, SG_W), F32)
        dw = [jnp.zeros((CHUNK, CHUNK), F32) for _ in range(ng)]
        for c in range(tm // CHUNK):
            rows = slice(c * CHUNK, (c + 1) * CHUNK)
            dm = dmixed[rows]
            dbias = dbias + dm
            dmb = dm.astype(BF16)
            vb = v[rows].astype(BF16)
            dvc = jnp.zeros((CHUNK, SG_W), F32)
            for g in range(ng):
                in_g = lane // GROUP == g
                dvc = dvc + jnp.where(in_g, _mm(wct_ref[g], dmb), 0.0)
                dw[g] = dw[g] + _mm_nt(jnp.where(in_g, dmb, jnp.zeros_like(dmb)), vb)
            dv_ref[rows, :] = dvc
        for g in range(ng):
            dwg = jnp.where(col <= row, dw[g], 0.0)

            @pl.when(first)
            def _():
                dwsp_ref[g] = dwg

            @pl.when(jnp.logical_not(first))
            def _():
                dwsp_ref[g] += dwg
        _acc(dbias_ref, first, _gmean(dbias, gm) * GROUP)
        dv = dv_ref[...]
        _acc(dlng_ref, first, jnp.sum(dv * vh, axis=0, keepdims=True))
        _acc(dlnb_ref, first, jnp.sum(dv, axis=0, keepdims=True))
        dvh = dv * lng
        dv0 = r * (dvh - _gmean(dvh, gm) - vh * _gmean(dvh * vh, gm))
        dzsg_ref[...] = (jnp.concatenate([du, dv0], axis=1) * _gelu_grad(zsg, th)).astype(BF16)

    row_ = lambda w: pl.BlockSpec((tm, w), lambda i: (i, 0))
    vec = lambda w: pl.BlockSpec((1, w), lambda i: (0, 0))
    return pl.pallas_call(
        body, name="mix_bwd", grid=(steps,),
        in_specs=[row_(D),
                  pl.BlockSpec((tm, 768), lambda i: (i, 0)),
                  pl.BlockSpec((HALO, 768), lambda i: (jnp.maximum(i * (tm // HALO) - 1, 0), 0)),
                  pl.BlockSpec((tm, 512), lambda i: (i, 3)),
                  row_(512),
                  _whole(), _whole(), _whole(), _whole(), _whole(), _whole(), _whole(), _whole(), _whole(), _whole()],
        out_specs=[row_(512), row_(CV_W), row_(512), pl.BlockSpec((D, D), lambda i: (0, 0)), vec(D), vec(D), vec(SG_W),
                   vec(SG_W), pl.BlockSpec((ng, CHUNK, CHUNK), lambda i: (0, 0, 0)),
                   pl.BlockSpec((CHUNK, SG_W), lambda i: (0, 0))],
        out_shape=[jax.ShapeDtypeStruct((t, 512), F32), jax.ShapeDtypeStruct((t, CV_W), F32),
                   jax.ShapeDtypeStruct((t, 512), BF16), jax.ShapeDtypeStruct((D, D), BF16),
                   jax.ShapeDtypeStruct((1, D), F32),
                   jax.ShapeDtypeStruct((1, D), F32), jax.ShapeDtypeStruct((1, SG_W), F32),
                   jax.ShapeDtypeStruct((1, SG_W), F32), jax.ShapeDtypeStruct((ng, CHUNK, CHUNK), F32),
                   jax.ShapeDtypeStruct((CHUNK, SG_W), F32)],
        scratch_shapes=[pltpu.VMEM((ng, CHUNK, CHUNK), BF16), pltpu.VMEM((ng, CHUNK, CHUNK), BF16),
                        pltpu.VMEM((tm, SG_W), F32), pltpu.VMEM((tm, SG_W), F32), pltpu.VMEM((D, D), F32)],
        compiler_params=_cp(("arbitrary",)),
    )(dx1, z, z, z, ya, gm, lng, lnb, wsp, wspt, bias, cw, gout, wout, gpost)


def conv_bwd(dyc, z, cw):
    t = dyc.shape[0]
    tm = min(TM_ROW, t)
    hb = tm // 8
    last_blk = t // 8 - 1

    def body(dyc_ref, dyct_ref, zcv_ref, head_ref, tail_ref, cw_ref, dz_ref, dcw_ref):
        i = pl.program_id(0)
        first = i == 0
        last = i == pl.num_programs(0) - 1
        cw = cw_ref[...]
        zcv = zcv_ref[...].astype(F32)
        gb, gc, hh = zcv[:, :CV_W], zcv[:, CV_W:2 * CV_W], zcv[:, 2 * CV_W:]
        _, conv, y, y1, y2 = _conv_fwd(zcv, head_ref[...].astype(F32), first, cw)
        dyc = dyc_ref[...]
        dconv = dyc * gb
        tail = jnp.where(last, 0.0, dyct_ref[...] * tail_ref[:8, :CV_W].astype(F32))
        d1 = _shift_up(dconv, 1, tail)
        d2 = _shift_up(dconv, 2, tail)
        dy = dconv * cw[2:3, :] + d1 * cw[1:2, :] + d2 * cw[0:1, :]
        dz_ref[...] = jnp.concatenate([dyc * conv, dy * hh, dy * gc], axis=1).astype(BF16)
        tap = lax.broadcasted_iota(jnp.int32, (8, CV_W), 0)
        dcw = jnp.where(tap == 0, jnp.sum(dconv * y2, axis=0, keepdims=True),
                        jnp.where(tap == 1, jnp.sum(dconv * y1, axis=0, keepdims=True),
                                  jnp.where(tap == 2, jnp.sum(dconv * y, axis=0, keepdims=True), 0.0)))
        _acc(dcw_ref, first, dcw)

    return pl.pallas_call(
        body, name="conv_bwd", grid=(t // tm,),
        in_specs=[pl.BlockSpec((tm, CV_W), lambda i: (i, 0)),
                  pl.BlockSpec((8, CV_W), lambda i: (jnp.minimum((i + 1) * hb, last_blk), 0)),
                  pl.BlockSpec((tm, 768), lambda i: (i, 0)),
                  pl.BlockSpec((HALO, 768), lambda i: (jnp.maximum(i * (tm // HALO) - 1, 0), 0)),
                  pl.BlockSpec((HALO, 768), lambda i: (jnp.minimum((i + 1) * (tm // HALO), t // HALO - 1), 0)),
                  _whole()],
        out_specs=[pl.BlockSpec((tm, 768), lambda i: (i, 0)), pl.BlockSpec((8, CV_W), lambda i: (0, 0))],
        out_shape=[jax.ShapeDtypeStruct((t, 768), BF16), jax.ShapeDtypeStruct((8, CV_W), F32)],
        compiler_params=_cp(("arbitrary",)),
    )(dyc, dyc, z, z, z, cw)


def attn_bwd(q, k, v, o, lse, do, carried=()):
    t = q.shape[0]
    tq = min(TQ, t)
    nq = t // tq
    last_pair = HEADS // 2 - 1
    n = len(carried)

    def body(*refs):
        j = pl.program_id(1)
        q_ref, k_ref, v_ref, o_ref, lse_ref, do_ref = refs[:6]
        dq_out_ref, dk_ref, dv_ref = refs[6 + n:9 + n]
        dq_ref = refs[9 + 2 * n]
        sems = refs[10 + 2 * n:]
        stages = [_exchange_steps(refs[6 + a], refs[9 + n + a], carried[a][1], *sems[3 * a:3 * a + 3]) for a in range(n)]
        if n:
            pl.when((pl.program_id(0) == 0) & (j == 0))(_each(stages, 0))

        @pl.when(j == 0)
        def _():
            dq_ref[...] = jnp.zeros_like(dq_ref)

        row = lax.broadcasted_iota(jnp.int32, (tq, tq), 0)
        col = lax.broadcasted_iota(jnp.int32, (tq, tq), 1)
        vlane = lax.broadcasted_iota(jnp.int32, (tq, 2 * V_DIM), 1)
        head_lanes = [slice(h * HEAD_PAD, (h + 1) * HEAD_PAD) for h in range(2)]

        def step(i, carry, masked):
            start = pl.multiple_of(i * tq, tq)
            do_blk = do_ref[pl.ds(start, tq), :]
            o_blk = o_ref[pl.ds(start, tq), :]
            vb = v_ref[...]
            dks, dv_acc = [], carry[2]
            for h in range(2):
                lanes = head_lanes[h]
                qb = q_ref[pl.ds(start, tq), lanes]
                kb = k_ref[:, lanes]
                dob = jnp.where((vlane // V_DIM) == h, do_blk, 0.0)
                delta = jnp.sum(dob * o_blk, axis=-1, keepdims=True)
                lse2 = lse_ref[pl.ds(start, tq), lanes][:, 0:1] * LOG2E
                s = _mm_nt(qb, kb)
                if masked:
                    s = jnp.where(col <= row, s, NEG)
                p = jnp.exp2(s * SCALE_LOG2E - lse2)
                dob16 = dob.astype(BF16)
                dp = _mm_nt(dob16, vb)
                ds = (p * (dp - delta) * SCALE).astype(BF16)
                dv_acc = dv_acc + _mm_tn(p.astype(BF16), dob16)
                dks.append(carry[h] + _mm_tn(ds, qb))
                dq_ref[pl.ds(start, tq), lanes] += _mm(ds, kb)
            return dks[0], dks[1], dv_acc

        zero = jnp.zeros((tq, HEAD_PAD), F32)
        carry = step(j, (zero, zero, jnp.zeros((tq, 2 * V_DIM), F32)), True)
        rest = nq - 1 - j
        carry = lax.fori_loop(0, rest // 2, lambda u, c: step(j + 2 + 2 * u, step(j + 1 + 2 * u, c, False), False), carry)
        dk0, dk1, dv_acc = lax.fori_loop(0, rest % 2, lambda _, c: step(nq - 1, c, False), carry)
        dk_ref[:, head_lanes[0]] = dk0.astype(BF16)
        dk_ref[:, head_lanes[1]] = dk1.astype(BF16)
        dv_ref[...] = dv_acc.astype(BF16)

        @pl.when(j == nq - 1)
        def _():
            dq_out_ref[...] = dq_ref[...].astype(BF16)

        if n:
            pl.when((pl.program_id(0) == last_pair) & (j == nq - 1))(_each(stages, 1))

    hbm = pl.BlockSpec(memory_space=pl.ANY)
    return pl.pallas_call(
        body, name=f"attn_bwd_exchange{n}" if n else "attn_bwd", grid=(HEADS // 2, nq),
        in_specs=[pl.BlockSpec((t, 2 * HEAD_PAD), lambda p, j: (0, p)),
                  pl.BlockSpec((tq, 2 * HEAD_PAD), lambda p, j: (j, p)),
                  pl.BlockSpec((tq, 2 * V_DIM), lambda p, j: (j, p)),
                  pl.BlockSpec((t, 2 * V_DIM), lambda p, j: (0, p)),
                  pl.BlockSpec((t, 2 * HEAD_PAD), lambda p, j: (0, p)),
                  pl.BlockSpec((t, 2 * V_DIM), lambda p, j: (0, p))] + [hbm] * n,
        out_specs=[pl.BlockSpec((t, 2 * HEAD_PAD), lambda p, j: (0, p)),
                   pl.BlockSpec((tq, 2 * HEAD_PAD), lambda p, j: (j, p)),
                   pl.BlockSpec((tq, 2 * V_DIM), lambda p, j: (j, p))] + [hbm] * n,
        out_shape=[jax.ShapeDtypeStruct((t, HEADS * HEAD_PAD), BF16), jax.ShapeDtypeStruct((t, HEADS * HEAD_PAD), BF16),
                   jax.ShapeDtypeStruct((t, HEADS * V_DIM), BF16)]
        + [jax.ShapeDtypeStruct(src.shape if scatter else (N_DEV,) + src.shape, src.dtype) for src, scatter in carried],
        scratch_shapes=[pltpu.VMEM((t, 2 * HEAD_PAD), F32)] + _comm_sems() * n,
        compiler_params=_cp(("arbitrary", "arbitrary") if n else ("parallel", "arbitrary")),
    )(q, k, v, o, lse, do, *[src for src, _ in carried])


def mla_proj_bwd(dq, dk, dv, z, ca, sb, sc, gq, gkv, wuq, wukv):
    t = z.shape[0]
    tm = min(TM_ROW, t)
    steps = t // tm

    def body(dq_ref, dk_ref, dv_ref, z_ref, ca_ref, sb_ref, sc_ref, gq_ref, gkv_ref, wuq_ref, wukv_ref,
             dz_ref, dwuq_ref, dwukv_ref, dgq_ref, dgkv_ref, dqp_ref, dkvp_ref, uq_acc, ukv_acc):
        first = pl.program_id(0) == 0
        z = z_ref[...].astype(F32)
        ca, sb, sc = ca_ref[...], sb_ref[...], sc_ref[...]
        gq, gkv = gq_ref[...], gkv_ref[...]
        cq, cqh, rq = _rms_fwd(z[:, :Q_RANK], gq)
        ckv, ckvh, rkv = _rms_fwd(z[:, Q_RANK:Q_RANK + KV_RANK], gkv)
        lane = lax.broadcasted_iota(jnp.int32, (tm, HEAD_PAD), 1)
        dkr = jnp.zeros((tm, HEAD_PAD), F32)
        for h in range(HEADS):
            lanes = slice(h * HEAD_PAD, (h + 1) * HEAD_PAD)
            dqp_ref[:, lanes] = _rope_t(dq_ref[:, lanes].astype(F32), ca, sb, sc).astype(BF16)
            dkh = dk_ref[:, lanes].astype(F32)
            dkr = dkr + dkh
            dkvp_ref[:, lanes] = jnp.where(lane < NOPE, dkh, 0.0).astype(BF16)
        dkvp_ref[:, HEADS * HEAD_PAD:] = dv_ref[...].astype(BF16)
        dkr = pltpu.roll(_rope_t(jnp.where(lane >= NOPE, dkr, 0.0), ca, sb, sc), HEAD_PAD - NOPE, 1)
        dkr = jnp.where(lane < ROPE, dkr, 0.0)
        dcq = _mm(dqp_ref[...], wuq_ref[...])
        dckv = _mm(dkvp_ref[...], wukv_ref[...])
        dzq, dgq = _rms_bwd(cqh, rq, gq, dcq)
        dzkv, dgkv = _rms_bwd(ckvh, rkv, gkv, dckv)
        dz_ref[...] = jnp.concatenate([dzq, dzkv, dkr], axis=1).astype(BF16)
        _acc(uq_acc, first, _mm_tn(cq.astype(BF16), dqp_ref[...]))
        _acc(ukv_acc, first, _mm_tn(ckv.astype(BF16), dkvp_ref[...]))
        _acc(dgq_ref, first, dgq)
        _acc(dgkv_ref, first, dgkv)

        @pl.when(pl.program_id(0) == steps - 1)
        def _():
            dwuq_ref[...] = uq_acc[...].astype(BF16)
            dwukv_ref[...] = ukv_acc[...].astype(BF16)

    row = lambda w: pl.BlockSpec((tm, w), lambda i: (i, 0))
    vec = lambda w: pl.BlockSpec((1, w), lambda i: (0, 0))
    whole = lambda r, c: pl.BlockSpec((r, c), lambda i: (0, 0))
    nq, nkv = HEADS * HEAD_PAD, HEADS * (HEAD_PAD + V_DIM)
    return pl.pallas_call(
        body, name="mla_proj_bwd", grid=(steps,),
        in_specs=[row(1024), row(1024), row(512), pl.BlockSpec((tm, 768), lambda i: (i, 1)),
                  row(HEAD_PAD), row(HEAD_PAD), row(HEAD_PAD), _whole(), _whole(), _whole(), _whole()],
        out_specs=[row(768), whole(Q_RANK, nq), whole(KV_RANK, nkv), vec(Q_RANK), vec(KV_RANK)],
        out_shape=[jax.ShapeDtypeStruct((t, 768), BF16), jax.ShapeDtypeStruct((Q_RANK, nq), BF16),
                   jax.ShapeDtypeStruct((KV_RANK, nkv), BF16), jax.ShapeDtypeStruct((1, Q_RANK), F32),
                   jax.ShapeDtypeStruct((1, KV_RANK), F32)],
        scratch_shapes=[pltpu.VMEM((tm, nq), BF16), pltpu.VMEM((tm, nkv), BF16), pltpu.VMEM((Q_RANK, nq), F32),
                        pltpu.VMEM((KV_RANK, nkv), F32)],
        compiler_params=_cp(("arbitrary",)),
    )(dq, dk, dv, z, ca, sb, sc, gq, gkv, wuq, wukv)


def pre_in_bwd(x, dx1, dzcv, dzmla, dzsg, g, w):
    t = x.shape[0]
    tm = min(TM, t)
    steps = t // tm

    def body(x_ref, dx1_ref, dzcv_ref, dzmla_ref, dzsg_ref, g_ref, w_ref, dx_ref, dw_ref, dg_ref, acc_ref):
        first = pl.program_id(0) == 0
        g = g_ref[...]
        h, xh, r = _rms_fwd(x_ref[...], g)
        dz = jnp.concatenate([dzcv_ref[...], dzmla_ref[...], dzsg_ref[...]], axis=1)
        dx, dg = _rms_bwd(xh, r, g, _mm(dz, w_ref[...]))
        dx_ref[...] = dx1_ref[...] + dx
        _acc(acc_ref, first, _mm_tn(dz, h.astype(BF16)))
        _acc(dg_ref, first, dg)

        @pl.when(pl.program_id(0) == steps - 1)
        def _():
            dw_ref[...] = acc_ref[...].astype(BF16)

    row = lambda w_: pl.BlockSpec((tm, w_), lambda i: (i, 0))
    return pl.pallas_call(
        body, name="pre_in_bwd", grid=(steps,),
        in_specs=[row(D), row(D), row(768), row(768), row(512), _whole(), _whole()],
        out_specs=[row(D), pl.BlockSpec((Z_W, D), lambda i: (0, 0)), pl.BlockSpec((1, D), lambda i: (0, 0))],
        out_shape=[jax.ShapeDtypeStruct((t, D), F32), jax.ShapeDtypeStruct((Z_W, D), BF16),
                   jax.ShapeDtypeStruct((1, D), F32)],
        scratch_shapes=[pltpu.VMEM((Z_W, D), F32)],
        compiler_params=_cp(("arbitrary",)),
    )(x, dx1, dzcv, dzmla, dzsg, g, w)


MESH = pl.DeviceIdType.MESH


def _place():
    return lax.axis_index("x"), lax.axis_index("y"), lax.axis_index("c")


def _comm_sems():
    return [pltpu.SemaphoreType.DMA((7,)), pltpu.SemaphoreType.DMA((7,)), pltpu.SemaphoreType.DMA]


def _gather_steps(x_ref, out_ref, send_sems, recv_sems, local_sem):
    x, y, c = _place()
    me, sibling = (x, y, c), (x, y, 1 - c)
    chips = [(1 - x, y), (x, 1 - y), (1 - x, 1 - y)]

    def slot(px, py, pc):
        return out_ref.at[4 * px + 2 * py + pc]

    def copy(k, blk, to, src=None):
        return pltpu.make_async_remote_copy(
            src_ref=slot(*blk) if src is None else src, dst_ref=slot(*blk),
            send_sem=send_sems.at[k], recv_sem=recv_sems.at[k], device_id=to, device_id_type=MESH)

    mine = pltpu.make_async_copy(x_ref, slot(*me), local_sem)
    first = [copy(0, me, sibling, src=x_ref)] + [copy(1 + j, me, (*chip, c), src=x_ref) for j, chip in enumerate(chips)]
    passed = [copy(4 + j, (*chip, c), sibling) for j, chip in enumerate(chips)]

    def start():
        mine.start()
        for cp in first:
            cp.start()

    def forward():
        for j, chip in enumerate(chips):
            copy(1 + j, (*chip, c), me).wait_recv()
            passed[j].start()

    def finish():
        copy(0, sibling, me).wait_recv()
        for j, chip in enumerate(chips):
            copy(4 + j, (*chip, 1 - c), me).wait_recv()
        for cp in first + passed:
            cp.wait_send()
        mine.wait()

    return start, forward, finish


def _exchange_steps(src_ref, out_ref, scatter, send_sems, recv_sems, local_sem):
    x, y, c = _place()
    me = 4 * x + 2 * y + c
    own = pltpu.make_async_copy(src_ref.at[me] if scatter else src_ref, out_ref.at[me], local_sem)
    copies = []
    for k in range(1, N_DEV):
        px = 1 - x if k & 4 else x
        py = 1 - y if k & 2 else y
        pc = 1 - c if k & 1 else c
        copies.append(pltpu.make_async_remote_copy(
            src_ref=src_ref.at[4 * px + 2 * py + pc] if scatter else src_ref, dst_ref=out_ref.at[me],
            send_sem=send_sems.at[k - 1], recv_sem=recv_sems.at[k - 1], device_id=(px, py, pc), device_id_type=MESH))

    def start():
        own.start()
        for cp in copies:
            cp.start()

    def finish():
        for cp in copies:
            cp.wait_recv()
        for cp in copies:
            cp.wait_send()
        own.wait()

    return start, finish


def all_gather(block):
    def body(x_ref, out_ref, *sems):
        for stage in _gather_steps(x_ref, out_ref, *sems):
            stage()

    return pl.pallas_call(
        body, name="all_gather",
        in_specs=[pl.BlockSpec(memory_space=pl.ANY)],
        out_specs=pl.BlockSpec(memory_space=pl.ANY),
        out_shape=jax.ShapeDtypeStruct((N_DEV,) + block.shape, block.dtype),
        scratch_shapes=_comm_sems(),
    )(block)


def _row_tile(r, cap):
    return max(d for d in range(16, cap + 1, 16) if r % d == 0)


def sum_adamw(parts, w, m, v, cap, carried=()):
    nl, r, c = w.shape
    tr = _row_tile(r, cap)
    steps = r // tr
    n = len(carried)
    c1 = 1.0 / (1.0 - ADAM_B1 ** ADAM_STEP)
    c2 = 1.0 / (1.0 - ADAM_B2 ** ADAM_STEP)

    def body(*refs):
        p_refs = refs[:nl]
        w_ref, m_ref, v_ref = refs[nl:nl + 3]
        g_ref, d_ref, nm_ref, nv_ref = refs[nl + 3 + n:nl + 7 + n]
        sems = refs[nl + 7 + 2 * n:]
        stages = [_exchange_steps(refs[nl + 3 + a], refs[nl + 7 + n + a], carried[a][1], *sems[3 * a:3 * a + 3])
                  for a in range(n)]
        layer, i = pl.program_id(0), pl.program_id(1)
        if n:
            pl.when((layer == 0) & (i == 0))(_each(stages, 0))

        def update(p_ref):
            g = p_ref[0].astype(F32)
            for k in range(1, N_DEV):
                g = g + p_ref[k].astype(F32)
            m_new = ADAM_B1 * m_ref[...] + (1.0 - ADAM_B1) * g
            v_new = ADAM_B2 * v_ref[...] + (1.0 - ADAM_B2) * (g * g)
            g_ref[...] = g
            nm_ref[...] = m_new
            nv_ref[...] = v_new
            d_ref[...] = -ADAM_LR * ((m_new * c1) / (jnp.sqrt(v_new * c2) + ADAM_EPS) + ADAM_WD * w_ref[...])

        for k in range(nl):
            pl.when(layer == k)(functools.partial(update, p_refs[k]))
        if n:
            pl.when((layer == nl - 1) & (i == steps - 1))(_each(stages, 1))

    def parts_spec(k):
        return pl.BlockSpec((N_DEV, tr, c), lambda l, i: (0, jnp.where(l == k, i, jnp.where(l < k, 0, steps - 1)), 0))

    blk = pl.BlockSpec((None, tr, c), lambda l, i: (l, i, 0))
    out = jax.ShapeDtypeStruct((nl, r, c), F32)
    hbm = pl.BlockSpec(memory_space=pl.ANY)
    return pl.pallas_call(
        body, name=f"sum_adamw_exchange{n}" if n else "sum_adamw", grid=(nl, steps),
        in_specs=[parts_spec(k) for k in range(nl)] + [blk, blk, blk] + [hbm] * n,
        out_specs=[blk, blk, blk, blk] + [hbm] * n,
        out_shape=[out, out, out, out]
        + [jax.ShapeDtypeStruct(src.shape if scatter else (N_DEV,) + src.shape, src.dtype) for src, scatter in carried],
        scratch_shapes=_comm_sems() * n,
        compiler_params=_cp(("arbitrary", "arbitrary")),
    )(*parts, w, m, v, *[src for src, _ in carried])


PACK_W = 1024
MIX_PIECES = (("w_out", D // N_DEV, D, False), ("w_uq", HEADS * (NOPE + ROPE) // N_DEV, Q_RANK, True),
              ("w_ukv", HEADS * (NOPE + V_DIM) // N_DEV, KV_RANK, True), ("conv", 16, PACK_W, False),
              ("w_in", IN_W // N_DEV, D, True))
FFN_PIECES = (("w_gate", D_FF // N_DEV, D, True), ("w_up", D_FF // N_DEV, D, True), ("w_down", D_FF // N_DEV, D, False))
def _packed_rows(rows, cols):
    return rows * cols // PACK_W


OFFSET = {}
for _pieces in (MIX_PIECES, FFN_PIECES):
    _off = 0
    for _name, _rows, _cols, _ in _pieces:
        assert _rows * _cols % PACK_W == 0
        OFFSET[_name] = _off
        _off += _packed_rows(_rows, _cols) + -_packed_rows(_rows, _cols) % 16
assert all(o % 16 == 0 for o in OFFSET.values())
assert [OFFSET[n] for n in ("w_gate", "w_up", "w_down")] == [0, FFN_SHARD, 2 * FFN_SHARD]
CONV_BITS = 3 * (CV_W // N_DEV) * 2


def _to_pack(shards, dtype, pieces, conv=None):
    nl = shards["w_in"].shape[0]
    parts = []
    for name, rows, cols, transposed in pieces:
        if name == "conv":
            if conv is None:
                a = jnp.zeros((nl, rows, PACK_W), dtype)
            else:
                bits = lax.bitcast_convert_type(conv.astype(F32), BF16).reshape(nl, CONV_BITS)
                a = jnp.pad(bits, ((0, 0), (0, rows * PACK_W - CONV_BITS))).reshape(nl, rows, PACK_W)
        else:
            a = shards[name].astype(dtype)
            a = (jnp.swapaxes(a, 1, 2) if transposed else a).reshape(nl, _packed_rows(rows, cols), PACK_W)
            a = jnp.pad(a, ((0, 0), (0, -a.shape[1] % 16), (0, 0)))
        parts.append(a)
    return jnp.concatenate(parts, axis=1)


def _from_pack(pack, pieces):
    out = {}
    for name, rows, cols, transposed in pieces:
        if name != "conv":
            a = pack[:, OFFSET[name]:OFFSET[name] + _packed_rows(rows, cols)].reshape(pack.shape[0], rows, cols)
            out[name] = jnp.swapaxes(a, 1, 2) if transposed else a
    return out


def _mix_weights(g):
    def rows(name):
        _, n, cols, _ = next(p for p in MIX_PIECES if p[0] == name)
        return g[:, OFFSET[name]:OFFSET[name] + _packed_rows(n, cols)].reshape(N_DEV, n, cols)

    w_in_t = rows("w_in").reshape(IN_W, D)
    w_in_p = jnp.concatenate([w_in_t[1184:], w_in_t[:672], jnp.zeros((96, D), BF16), w_in_t[672:1184]], axis=0)
    w_uq_p = jnp.pad(rows("w_uq"), ((0, 0), (0, HEAD_PAD - NOPE - ROPE), (0, 0))).reshape(HEADS * HEAD_PAD, Q_RANK)
    kv = rows("w_ukv")
    w_k = jnp.pad(kv[:, :NOPE], ((0, 0), (0, HEAD_PAD - NOPE), (0, 0))).reshape(HEADS * HEAD_PAD, KV_RANK)
    w_ukv_p = jnp.concatenate([w_k, kv[:, NOPE:].reshape(HEADS * V_DIM, KV_RANK)], axis=0)
    bits = rows("conv").reshape(N_DEV, -1)[:, :CONV_BITS].reshape(N_DEV, 3, CV_W // N_DEV, 2)
    conv_w = jnp.moveaxis(lax.bitcast_convert_type(bits, F32), 0, 1).reshape(3, CV_W)
    return dict(w_in=w_in_p, w_uq=w_uq_p, w_ukv=w_ukv_p, w_out=rows("w_out").reshape(D, D), conv_w=conv_w)


def _grad_chunks(full):
    d_in = full["w_in"]
    d_in = jnp.concatenate([d_in[768:768 + 672], d_in[1536:], d_in[:768]], axis=0)
    d_uq = full["w_uq"].reshape(HEADS, HEAD_PAD, Q_RANK)[:, :NOPE + ROPE]
    d_k = full["w_ukv"][:HEADS * HEAD_PAD].reshape(HEADS, HEAD_PAD, KV_RANK)[:, :NOPE]
    d_v = full["w_ukv"][HEADS * HEAD_PAD:].reshape(HEADS, V_DIM, KV_RANK)
    mats = dict(w_in=d_in, w_uq=d_uq, w_ukv=jnp.concatenate([d_k, d_v], axis=1), w_out=full["w_out"])
    parts = []
    for name, rows, cols, _ in MIX_PIECES:
        if name == "conv":
            parts.append(jnp.zeros((N_DEV, rows, PACK_W), BF16))
        else:
            a = mats[name].reshape(N_DEV, _packed_rows(rows, cols), PACK_W)
            parts.append(jnp.pad(a, ((0, 0), (0, -a.shape[1] % 16), (0, 0))))
    return jnp.concatenate(parts, axis=1)


SMALL = (("mix_pre_g", (D,)), ("mix_post_g", (D,)), ("ffn_pre_g", (D,)), ("ffn_post_g", (D,)), ("q_norm_g", (Q_RANK,)),
         ("kv_norm_g", (KV_RANK,)), ("sg_ln_g", (SG_W,)), ("sg_ln_b", (SG_W,)), ("w_sp", (4, CHUNK, CHUNK)),
         ("b_sp", (4, CHUNK)), ("out_norm_g", (D,)))
SMALL_ROWS = 576


def _pack_small(vals, nl):
    flat = jnp.concatenate([vals[name].reshape(nl, -1) for name, _ in SMALL] + [vals["conv_w"].reshape(nl, -1)], axis=1)
    return jnp.pad(flat, ((0, 0), (0, SMALL_ROWS * 128 - flat.shape[1]))).reshape(nl * SMALL_ROWS, 128)


def _unpack_small(pack, nl):
    flat = pack.reshape(nl, SMALL_ROWS * 128)
    out, off = {}, 0
    for name, shape in SMALL + (("conv_w", (3, CV_W)),):
        n = int(np.prod(shape))
        out[name] = flat[:, off:off + n].reshape((nl,) + shape)
        off += n
    return out


def _layer_fwd(x, lw, sp, tabs, consts, ffn_pack, next_mix_pack):
    ca, sb, sc = tabs
    z = pre_in_fwd(x, sp["mix_pre_g"], lw["w_in"])
    q, k, v = mla_proj_fwd(z, ca, sb, sc, sp["q_norm_g"], sp["kv_norm_g"], lw["w_uq"], lw["w_ukv"])
    ya, lse, lw["ffn"] = attn_fwd(q, k, v, (ffn_pack,))
    x1 = mix_fwd(x, z, ya, consts["gm"], sp["sg_ln_g"], sp["sg_ln_b"], sp["w_sp"], sp["bias"], lw["conv_w"],
                 sp["out_norm_g"], lw["w_out"], sp["mix_post_g"])
    x2, f, *mix_gathered = ffn_fwd(x1, sp["ffn_pre_g"], lw["ffn"], sp["ffn_post_g"], next_mix_pack)
    return x2, (x, z, q, k, v, ya, lse, x1, f), mix_gathered


def _layer_bwd(dx2, saved, lw, sp, tabs, consts, pending):
    ca, sb, sc = tabs
    x, z, q, k, v, ya, lse, x1, f = saved
    dx1, h2, dab, s, df, d_ffn_pre, d_ffn_post, *received = ffn_bwd(x1, f, dx2, sp["ffn_pre_g"], lw["ffn"], sp["ffn_post_g"],
                                                                    pending)
    ffn_chunks = atb_ffn_chunks(s, df, 2, atb_ffn_chunks(dab, h2, 0)).reshape(N_DEV, len(FFN_PIECES) * FFN_SHARD, D)
    dya, dyc, dzsg, d_w_out, d_mix_post, d_out_norm, d_lng, d_lnb, d_wsp, d_bias = mix_bwd(
        dx1, z, ya, consts["gm"], sp["sg_ln_g"], sp["sg_ln_b"], sp["w_sp"], sp["w_sp_t"], sp["bias"], lw["conv_w"],
        sp["out_norm_g"], lw["w_out"], sp["mix_post_g"])
    dzcv, d_cw = conv_bwd(dyc, z, lw["conv_w"])
    dq, dk, dv, got_ffn = attn_bwd(q, k, v, ya, lse, dya, ((ffn_chunks, True),))
    dzmla, d_w_uq, d_w_ukv, d_gq, d_gkv = mla_proj_bwd(dq, dk, dv, z, ca, sb, sc, sp["q_norm_g"], sp["kv_norm_g"],
                                                       lw["w_uq"], lw["w_ukv"])
    dx, d_w_in, d_mix_pre = pre_in_bwd(x, dx1, dzcv, dzmla, dzsg, sp["mix_pre_g"], lw["w_in"])
    mix_chunks = _grad_chunks(dict(w_in=d_w_in, w_uq=d_w_uq.T, w_ukv=d_w_ukv.T, w_out=d_w_out))
    d_bsp = d_bias[:, ::GROUP].T
    small = dict(mix_pre_g=d_mix_pre[0], mix_post_g=d_mix_post[0], ffn_pre_g=d_ffn_pre[0], ffn_post_g=d_ffn_post[0],
                 q_norm_g=d_gq[0], kv_norm_g=d_gkv[0], sg_ln_g=d_lng[0], sg_ln_b=d_lnb[0], w_sp=d_wsp, b_sp=d_bsp,
                 out_norm_g=d_out_norm[0], conv_w=d_cw[:3])
    small_pack = _pack_small({name: a[None] for name, a in small.items()}, 1)
    return dx, ((mix_chunks, True), (small_pack, False)), [got_ffn] + received


def kernel(x, positions, mix_pre_g, mix_post_g, ffn_pre_g, ffn_post_g, w_in, q_norm_g, w_uq, kv_norm_g, w_ukv, sg_ln_g, sg_ln_b, w_sp, b_sp, conv_w, out_norm_g, w_out, w_gate, w_up, w_down, loss_target, m_mix_pre_g, m_mix_post_g, m_ffn_pre_g, m_ffn_post_g, m_w_in, m_q_norm_g, m_w_uq, m_kv_norm_g, m_w_ukv, m_sg_ln_g, m_sg_ln_b, m_w_sp, m_b_sp, m_conv_w, m_out_norm_g, m_w_out, m_w_gate, m_w_up, m_w_down, v_mix_pre_g, v_mix_post_g, v_ffn_pre_g, v_ffn_post_g, v_w_in, v_q_norm_g, v_w_uq, v_kv_norm_g, v_w_ukv, v_sg_ln_g, v_sg_ln_b, v_w_sp, v_b_sp, v_conv_w, v_out_norm_g, v_w_out, v_w_gate, v_w_up, v_w_down):
    nl = w_in.shape[0]
    t = x.shape[1]
    w = dict(mix_pre_g=mix_pre_g, mix_post_g=mix_post_g, ffn_pre_g=ffn_pre_g, ffn_post_g=ffn_post_g, w_in=w_in,
             q_norm_g=q_norm_g, w_uq=w_uq, kv_norm_g=kv_norm_g, w_ukv=w_ukv, sg_ln_g=sg_ln_g, sg_ln_b=sg_ln_b, w_sp=w_sp,
             b_sp=b_sp, conv_w=conv_w, out_norm_g=out_norm_g, w_out=w_out, w_gate=w_gate, w_up=w_up, w_down=w_down)
    m = dict(mix_pre_g=m_mix_pre_g, mix_post_g=m_mix_post_g, ffn_pre_g=m_ffn_pre_g, ffn_post_g=m_ffn_post_g, w_in=m_w_in,
             q_norm_g=m_q_norm_g, w_uq=m_w_uq, kv_norm_g=m_kv_norm_g, w_ukv=m_w_ukv, sg_ln_g=m_sg_ln_g, sg_ln_b=m_sg_ln_b,
             w_sp=m_w_sp, b_sp=m_b_sp, conv_w=m_conv_w, out_norm_g=m_out_norm_g, w_out=m_w_out, w_gate=m_w_gate,
             w_up=m_w_up, w_down=m_w_down)
    v = dict(mix_pre_g=v_mix_pre_g, mix_post_g=v_mix_post_g, ffn_pre_g=v_ffn_pre_g, ffn_post_g=v_ffn_post_g, w_in=v_w_in,
             q_norm_g=v_q_norm_g, w_uq=v_w_uq, kv_norm_g=v_kv_norm_g, w_ukv=v_w_ukv, sg_ln_g=v_sg_ln_g, sg_ln_b=v_sg_ln_b,
             w_sp=v_w_sp, b_sp=v_b_sp, conv_w=v_conv_w, out_norm_g=v_out_norm_g, w_out=v_w_out, w_gate=v_w_gate,
             w_up=v_w_up, w_down=v_w_down)

    mix_pack = _to_pack(w, BF16, MIX_PIECES, conv=w["conv_w"])
    ffn_pack = _to_pack(w, BF16, FFN_PIECES)
    consts = dict(gm=jnp.asarray(np.kron(np.eye(SG_W // GROUP), np.full((GROUP, GROUP), 1.0 / GROUP)), BF16))
    smalls = []
    for l in range(nl):
        sp = {name: w[name][l].reshape(1, -1) for name, shape in SMALL if len(shape) == 1}
        sp["w_sp"] = w["w_sp"][l]
        sp["w_sp_t"] = jnp.swapaxes(w["w_sp"][l], 1, 2)
        sp["bias"] = jnp.repeat(w["b_sp"][l].T, GROUP, axis=1)
        smalls.append(sp)
    inv_freq = 1.0 / (ROPE_THETA ** (jnp.arange(0, ROPE // 2, dtype=F32) / (ROPE // 2)))
    inv = jnp.zeros((1, HEAD_PAD), F32).at[0, NOPE:NOPE + ROPE].set(jnp.concatenate([inv_freq, inv_freq]))
    tabs = rope_tables(positions.reshape(t, 1).astype(F32), inv)

    h = x[0]
    saved, layers = [], []
    mix_gathered = [all_gather(mix_pack[0])]
    for l in range(nl):
        layers.append(_mix_weights(mix_gathered[0]))
        h, s, mix_gathered = _layer_fwd(h, layers[l], smalls[l], tabs, consts, ffn_pack[l],
                                        (mix_pack[l + 1],) if l + 1 < nl else ())
        saved.append(s)
    sq, dh = loss_head(h, loss_target[0])
    loss = lax.psum(0.5 * sq[0, 0] / D, ("x", "y", "c"))

    got_ffn, got_mix, got_small = [None] * nl, [None] * nl, [None] * nl
    pending = ()
    for l in reversed(range(nl)):
        dh, new_pending, received = _layer_bwd(dh, saved[l], layers[l], smalls[l], tabs, consts, pending)
        got_ffn[l] = received[0]
        if pending:
            got_mix[l + 1], got_small[l + 1] = received[1:]
        pending = new_pending

    me = 4 * lax.axis_index("x") + 2 * lax.axis_index("y") + lax.axis_index("c")
    *ffn_new, got_mix[0], got_small[0] = sum_adamw(got_ffn, *[_to_pack(d, F32, FFN_PIECES) for d in (w, m, v)], 176,
                                                   carried=pending)
    mix_new = sum_adamw(got_mix, *[_to_pack(d, F32, MIX_PIECES) for d in (w, m, v)], 208)
    got_small = jnp.concatenate(got_small, axis=1)
    g_big, d_big, m_big, v_big = [{**_from_pack(a, FFN_PIECES), **_from_pack(b, MIX_PIECES)}
                                  for a, b in zip(ffn_new, mix_new)]

    def full_conv(a):
        return lax.dynamic_update_slice(jnp.zeros((nl, 3, CV_W), F32), a, (0, 0, me * (CV_W // N_DEV)))

    def small_pack(d):
        return _pack_small({**{name: d[name] for name, _ in SMALL}, "conv_w": full_conv(d["conv_w"])}, nl)

    g_small, d_small, m_small, v_small = [_unpack_small(p[0], nl) for p in
                                          sum_adamw([got_small], small_pack(w)[None], small_pack(m)[None],
                                                    small_pack(v)[None], 1152)]
    outs = []
    for big, small in ((g_big, g_small), (d_big, d_small), (m_big, m_small), (v_big, v_small)):
        for name in w:
            if name == "conv_w":
                outs.append(lax.dynamic_slice(small[name], (0, 0, me * (CV_W // N_DEV)), (nl, 3, CV_W // N_DEV)))
            elif name in small:
                outs.append(small[name])
            else:
                outs.append(big[name])
    return (loss, dh[None], *outs)
```
